```python
import jax, jax.numpy as jnp
from jax import lax
import numpy as np


D_MODEL = 1024
BATCH = 8
SEQ = 8192
DEPTH = 2

MIX_WIDTH = D_MODEL
POOL_WIDTH = D_MODEL // 4
SG_WIDTH = D_MODEL // 4
SB_WIDTH = D_MODEL // 2
POOL_WINDOWS = (2, 4, 8, 16)
N_POOL_GROUPS = len(POOL_WINDOWS)
POOL_GW = POOL_WIDTH // N_POOL_GROUPS
CHUNK = 128
SG_HEADS = 4
SG_HD = SG_WIDTH // SG_HEADS
SB_HD = 64
SB_HEADS = SB_WIDTH // SB_HD
Q_BLOCK = 128
IN_COLS = POOL_WIDTH + 2 * SG_WIDTH + 3 * SB_WIDTH
D_FF = 4 * D_MODEL
EPS = 1e-6

kernel_name = "hybrid_pool_sgmlp_stickbreak_block"


def rms_norm(x, g):
    xf = x.astype(jnp.float32)
    y = xf * lax.rsqrt(jnp.mean(xf * xf, axis=-1, keepdims=True) + EPS)
    return (y * g.astype(jnp.float32)).astype(x.dtype)


def pool_mixer(h, w_grp, scale):
    B, S, _ = h.shape
    hf = h.astype(jnp.float32)
    cs = jnp.cumsum(hf, axis=1)
    t = jnp.arange(S)
    pooled = []
    for gi, w in enumerate(POOL_WINDOWS):
        c = cs[..., gi * POOL_GW:(gi + 1) * POOL_GW]
        lag = jnp.pad(c, ((0, 0), (w, 0), (0, 0)))[:, :S]
        cnt = jnp.minimum(t + 1, w).astype(jnp.float32)[None, :, None]
        pooled.append((c - lag) / cnt)
    d = (jnp.concatenate(pooled, axis=-1) - hf).astype(h.dtype)
    d = d.reshape(B, S, N_POOL_GROUPS, POOL_GW)
    y = jnp.einsum('bsgc,gcd->bsgd', d, w_grp).reshape(B, S, POOL_WIDTH)
    return y * scale


def spatial_gate(z, g_norm, w_s, b_s):
    B, S, _ = z.shape
    u, v = z[..., :SG_WIDTH], z[..., SG_WIDTH:]
    v = rms_norm(v, g_norm)
    v = v.reshape(B, S // CHUNK, CHUNK, SG_HEADS, SG_HD)
    mask = jnp.tril(jnp.ones((CHUNK, CHUNK), dtype=w_s.dtype))
    sv = jnp.einsum('hts,bnshc->bnthc', w_s * mask, v)
    sv = sv + b_s.T[None, None, :, :, None]
    return u * sv.reshape(B, S, SG_WIDTH)


def stick_breaking_attention(q, k, v):
    B, S, _ = q.shape
    nb = S // Q_BLOCK
    q = q.reshape(B, S, SB_HEADS, SB_HD).transpose(0, 2, 1, 3)
    k = k.reshape(B, S, SB_HEADS, SB_HD).transpose(0, 2, 1, 3)
    v = v.reshape(B, S, SB_HEADS, SB_HD).transpose(0, 2, 1, 3)
    qb = q.reshape(B, SB_HEADS, nb, Q_BLOCK, SB_HD).transpose(2, 0, 1, 3, 4)
    inv_sqrt_d = 1.0 / np.sqrt(SB_HD).astype(np.float32)
    tk = jnp.arange(S)

    def block(args):
        i, qi = args
        z = jnp.einsum('bhqd,bhkd->bhqk', qi, k).astype(jnp.float32) * inv_sqrt_d
        tq = i * Q_BLOCK + jnp.arange(Q_BLOCK)
        causal = (tk[None, :] < tq[:, None])[None, None]
        log_beta = jax.nn.log_sigmoid(z)
        log_1m = jnp.where(causal, jax.nn.log_sigmoid(-z), 0.0)
        after = lax.cumsum(log_1m, axis=3, reverse=True) - log_1m
        a = jnp.where(causal, jnp.exp(log_beta + after), 0.0)
        return jnp.einsum('bhqk,bhkd->bhqd', a.astype(v.dtype), v)

    out = lax.map(block, (jnp.arange(nb), qb))
    return out.transpose(1, 0, 3, 2, 4).reshape(B, S, SB_WIDTH)


def _fwd_setup_inputs(seed: int = 0) -> dict:
    key = jax.random.key(seed)
    ks = jax.random.split(key, 16)
    f32 = jnp.float32
    nrm = lambda k, shape, s: jax.random.normal(k, shape, f32) * s
    return {
        "x": jax.random.normal(ks[0], (BATCH, SEQ, D_MODEL), f32),
        "norm1": 1.0 + nrm(ks[1], (DEPTH, D_MODEL), 0.02),
        "w_in": nrm(ks[2], (DEPTH, D_MODEL, IN_COLS), D_MODEL ** -0.5),
        "pool_w": nrm(ks[3], (DEPTH, N_POOL_GROUPS, POOL_GW, POOL_GW), POOL_GW ** -0.5),
        "pool_scale": 1.0 + nrm(ks[4], (DEPTH, POOL_WIDTH), 0.02),
        "sg_norm": 1.0 + nrm(ks[5], (DEPTH, SG_WIDTH), 0.02),
        "sg_w": nrm(ks[6], (DEPTH, SG_HEADS, CHUNK, CHUNK), 0.5 * CHUNK ** -0.5),
        "sg_b": 1.0 + nrm(ks[7], (DEPTH, SG_HEADS, CHUNK), 0.1),
        "w_out": nrm(ks[8], (DEPTH, MIX_WIDTH, D_MODEL), MIX_WIDTH ** -0.5),
        "norm2": 1.0 + nrm(ks[9], (DEPTH, D_MODEL), 0.02),
        "w_up": nrm(ks[10], (DEPTH, D_MODEL, D_FF), D_MODEL ** -0.5),
        "w_down": nrm(ks[11], (DEPTH, D_FF, D_MODEL), 0.5 * D_FF ** -0.5),
        "final_norm": 1.0 + nrm(ks[12], (D_MODEL,), 0.02),
    }


def _fwd_reference(x, norm1, w_in, pool_w, pool_scale, sg_norm, sg_w, sg_b, w_out, norm2, w_up, w_down, final_norm):
    c1 = POOL_WIDTH
    c2 = c1 + 2 * SG_WIDTH
    c3 = c2 + SB_WIDTH
    c4 = c3 + SB_WIDTH
    for l in range(DEPTH):
        h = rms_norm(x, norm1[l])
        proj = h @ w_in[l]
        a_in, b_in = proj[..., :c1], proj[..., c1:c2]
        q, k, v = proj[..., c2:c3], proj[..., c3:c4], proj[..., c4:]
        ya = pool_mixer(a_in, pool_w[l], pool_scale[l])
        yb = spatial_gate(jax.nn.gelu(b_in), sg_norm[l], sg_w[l], sg_b[l])
        yc = stick_breaking_attention(q, k, v)
        x = x + jnp.concatenate([ya, yb, yc], axis=-1) @ w_out[l]
        h = rms_norm(x, norm2[l])
        x = x + jnp.square(jax.nn.relu(h @ w_up[l])) @ w_down[l]
    return rms_norm(x, final_norm)


import jax as _jax
import jax.numpy as _jnp

TWIN_FORMAT = 'train_step'
FWD_PARAMS = ['x', 'norm1', 'w_in', 'pool_w', 'pool_scale', 'sg_norm', 'sg_w', 'sg_b', 'w_out', 'norm2', 'w_up', 'w_down', 'final_norm']
TWIN_WEIGHTS = ['norm1', 'w_in', 'pool_w', 'pool_scale', 'sg_norm', 'sg_w', 'sg_b', 'w_out', 'norm2', 'w_up', 'w_down', 'final_norm']
TWIN_DIFF_INPUT = 'x'
TWIN_INPUTS = ['x', 'norm1', 'w_in', 'pool_w', 'pool_scale', 'sg_norm', 'sg_w', 'sg_b', 'w_out', 'norm2', 'w_up', 'w_down', 'final_norm', 'loss_target', 'm_norm1', 'm_w_in', 'm_pool_w', 'm_pool_scale', 'm_sg_norm', 'm_sg_w', 'm_sg_b', 'm_w_out', 'm_norm2', 'm_w_up', 'm_w_down', 'm_final_norm', 'v_norm1', 'v_w_in', 'v_pool_w', 'v_pool_scale', 'v_sg_norm', 'v_sg_w', 'v_sg_b', 'v_w_out', 'v_norm2', 'v_w_up', 'v_w_down', 'v_final_norm']
TWIN_OUTPUTS = ['loss', 'grad_x', 'grad_norm1', 'grad_w_in', 'grad_pool_w', 'grad_pool_scale', 'grad_sg_norm', 'grad_sg_w', 'grad_sg_b', 'grad_w_out', 'grad_norm2', 'grad_w_up', 'grad_w_down', 'grad_final_norm', 'delta_norm1', 'delta_w_in', 'delta_pool_w', 'delta_pool_scale', 'delta_sg_norm', 'delta_sg_w', 'delta_sg_b', 'delta_w_out', 'delta_norm2', 'delta_w_up', 'delta_w_down', 'delta_final_norm', 'new_m_norm1', 'new_m_w_in', 'new_m_pool_w', 'new_m_pool_scale', 'new_m_sg_norm', 'new_m_sg_w', 'new_m_sg_b', 'new_m_w_out', 'new_m_norm2', 'new_m_w_up', 'new_m_w_down', 'new_m_final_norm', 'new_v_norm1', 'new_v_w_in', 'new_v_pool_w', 'new_v_pool_scale', 'new_v_sg_norm', 'new_v_sg_w', 'new_v_sg_b', 'new_v_w_out', 'new_v_norm2', 'new_v_w_up', 'new_v_w_down', 'new_v_final_norm']
TWIN_LEAF_KINDS = {'loss': 'loss', 'grad_x': 'grad_x', 'grad_norm1': 'grad_w', 'grad_w_in': 'grad_w', 'grad_pool_w': 'grad_w', 'grad_pool_scale': 'grad_w', 'grad_sg_norm': 'grad_w', 'grad_sg_w': 'grad_w', 'grad_sg_b': 'grad_w', 'grad_w_out': 'grad_w', 'grad_norm2': 'grad_w', 'grad_w_up': 'grad_w', 'grad_w_down': 'grad_w', 'grad_final_norm': 'grad_w', 'delta_norm1': 'delta_w', 'delta_w_in': 'delta_w', 'delta_pool_w': 'delta_w', 'delta_pool_scale': 'delta_w', 'delta_sg_norm': 'delta_w', 'delta_sg_w': 'delta_w', 'delta_sg_b': 'delta_w', 'delta_w_out': 'delta_w', 'delta_norm2': 'delta_w', 'delta_w_up': 'delta_w', 'delta_w_down': 'delta_w', 'delta_final_norm': 'delta_w', 'new_m_norm1': 'new_m', 'new_m_w_in': 'new_m', 'new_m_pool_w': 'new_m', 'new_m_pool_scale': 'new_m', 'new_m_sg_norm': 'new_m', 'new_m_sg_w': 'new_m', 'new_m_sg_b': 'new_m', 'new_m_w_out': 'new_m', 'new_m_norm2': 'new_m', 'new_m_w_up': 'new_m', 'new_m_w_down': 'new_m', 'new_m_final_norm': 'new_m', 'new_v_norm1': 'new_v', 'new_v_w_in': 'new_v', 'new_v_pool_w': 'new_v', 'new_v_pool_scale': 'new_v', 'new_v_sg_norm': 'new_v', 'new_v_sg_w': 'new_v', 'new_v_sg_b': 'new_v', 'new_v_w_out': 'new_v', 'new_v_norm2': 'new_v', 'new_v_w_up': 'new_v', 'new_v_w_down': 'new_v', 'new_v_final_norm': 'new_v'}


def _forward(args):
    return _fwd_reference(*[args[k] for k in FWD_PARAMS])


def _output_shape():
    def fwd():
        inp = _fwd_setup_inputs(0)
        return _fwd_reference(*[inp[k] for k in FWD_PARAMS])
    out = _jax.eval_shape(fwd)
    return out.shape, out.dtype

N_MICROBATCH = 1
ADAM_LR = 0.001
ADAM_B1 = 0.9
ADAM_B2 = 0.999
ADAM_EPS = 1e-08
ADAM_WD = 0.01
ADAM_STEP = 10
PER_EXAMPLE_BATCH_AXIS = {'x': 0, 'loss_target': 0}
SHARED_INPUTS = []
_WEIGHT_DTYPES = {'norm1': _jnp.float32, 'w_in': _jnp.float32, 'pool_w': _jnp.float32, 'pool_scale': _jnp.float32, 'sg_norm': _jnp.float32, 'sg_w': _jnp.float32, 'sg_b': _jnp.float32, 'w_out': _jnp.float32, 'norm2': _jnp.float32, 'w_up': _jnp.float32, 'w_down': _jnp.float32, 'final_norm': _jnp.float32}
MOMENT_SCALE = {'norm1': 1.672939e-01, 'w_in': 1.050267e-01, 'pool_w': 1.689080e-01, 'pool_scale': 1.853340e-01, 'sg_norm': 4.442754e-02, 'sg_w': 6.379618e-02, 'sg_b': 9.500520e-02, 'w_out': 1.480489e-01, 'norm2': 1.239892e-01, 'w_up': 6.149406e-02, 'w_down': 2.451610e-01, 'final_norm': 6.440977e+01}


def _to_microbatches(a, axis):
    t = _jnp.moveaxis(a, axis, 0)
    t = t.reshape((N_MICROBATCH, t.shape[0] // N_MICROBATCH) + t.shape[1:])
    return _jnp.moveaxis(t, 1, axis + 1)


def setup_inputs(seed: int = 0) -> dict:
    inp = _fwd_setup_inputs(seed)
    key = _jax.random.fold_in(_jax.random.key(seed), 7919)
    shape, _ = _output_shape()
    out = dict(inp)
    out["loss_target"] = _jax.random.normal(_jax.random.fold_in(key, 0), shape, _jnp.float32)
    for i, name in enumerate(TWIN_WEIGHTS):
        w = inp[name].astype(_jnp.float32)
        if MOMENT_SCALE is None:
            s = _jnp.sqrt(_jnp.mean(_jnp.square(w)) + 1e-30)
        else:
            s = MOMENT_SCALE[name]
        km, kv = _jax.random.split(_jax.random.fold_in(key, i + 1))
        out[name] = w
        out["m_" + name] = s * _jax.random.normal(km, w.shape, _jnp.float32)
        out["v_" + name] = (s * s) * _jax.random.uniform(kv, w.shape, _jnp.float32, 0.5, 1.5)
    if N_MICROBATCH > 1:
        for name, axis in PER_EXAMPLE_BATCH_AXIS.items():
            out[name] = _to_microbatches(out[name], axis)
    return {'x': out['x'], 'norm1': out['norm1'], 'w_in': out['w_in'], 'pool_w': out['pool_w'], 'pool_scale': out['pool_scale'], 'sg_norm': out['sg_norm'], 'sg_w': out['sg_w'], 'sg_b': out['sg_b'], 'w_out': out['w_out'], 'norm2': out['norm2'], 'w_up': out['w_up'], 'w_down': out['w_down'], 'final_norm': out['final_norm'], 'loss_target': out['loss_target'], 'm_norm1': out['m_norm1'], 'm_w_in': out['m_w_in'], 'm_pool_w': out['m_pool_w'], 'm_pool_scale': out['m_pool_scale'], 'm_sg_norm': out['m_sg_norm'], 'm_sg_w': out['m_sg_w'], 'm_sg_b': out['m_sg_b'], 'm_w_out': out['m_w_out'], 'm_norm2': out['m_norm2'], 'm_w_up': out['m_w_up'], 'm_w_down': out['m_w_down'], 'm_final_norm': out['m_final_norm'], 'v_norm1': out['v_norm1'], 'v_w_in': out['v_w_in'], 'v_pool_w': out['v_pool_w'], 'v_pool_scale': out['v_pool_scale'], 'v_sg_norm': out['v_sg_norm'], 'v_sg_w': out['v_sg_w'], 'v_sg_b': out['v_sg_b'], 'v_w_out': out['v_w_out'], 'v_norm2': out['v_norm2'], 'v_w_up': out['v_w_up'], 'v_w_down': out['v_w_down'], 'v_final_norm': out['v_final_norm']}


def _loss(weights, diff, rest, loss_target):
    with _jax.named_scope("forward"):
        args = {**rest, TWIN_DIFF_INPUT: diff, **{k: w.astype(_WEIGHT_DTYPES[k]) for k, w in weights.items()}}
        y = _forward(args)
    with _jax.named_scope("loss_head"):
        err = _jnp.square(y.astype(_jnp.float32) - loss_target)
        return 0.5 * _jnp.sum(_jnp.mean(err, axis=-1)) if err.ndim else 0.5 * err


def _adamw(w, g, m, v):
    m = ADAM_B1 * m + (1.0 - ADAM_B1) * g
    v = ADAM_B2 * v + (1.0 - ADAM_B2) * _jnp.square(g)
    m_hat = m / (1.0 - ADAM_B1 ** ADAM_STEP)
    v_hat = v / (1.0 - ADAM_B2 ** ADAM_STEP)
    delta = -ADAM_LR * (m_hat / (_jnp.sqrt(v_hat) + ADAM_EPS) + ADAM_WD * w)
    return delta, m, v


def reference(x, norm1, w_in, pool_w, pool_scale, sg_norm, sg_w, sg_b, w_out, norm2, w_up, w_down, final_norm, loss_target, m_norm1, m_w_in, m_pool_w, m_pool_scale, m_sg_norm, m_sg_w, m_sg_b, m_w_out, m_norm2, m_w_up, m_w_down, m_final_norm, v_norm1, v_w_in, v_pool_w, v_pool_scale, v_sg_norm, v_sg_w, v_sg_b, v_w_out, v_norm2, v_w_up, v_w_down, v_final_norm):
    given = dict(x=x, norm1=norm1, w_in=w_in, pool_w=pool_w, pool_scale=pool_scale, sg_norm=sg_norm, sg_w=sg_w, sg_b=sg_b, w_out=w_out, norm2=norm2, w_up=w_up, w_down=w_down, final_norm=final_norm, loss_target=loss_target, m_norm1=m_norm1, m_w_in=m_w_in, m_pool_w=m_pool_w, m_pool_scale=m_pool_scale, m_sg_norm=m_sg_norm, m_sg_w=m_sg_w, m_sg_b=m_sg_b, m_w_out=m_w_out, m_norm2=m_norm2, m_w_up=m_w_up, m_w_down=m_w_down, m_final_norm=m_final_norm, v_norm1=v_norm1, v_w_in=v_w_in, v_pool_w=v_pool_w, v_pool_scale=v_pool_scale, v_sg_norm=v_sg_norm, v_sg_w=v_sg_w, v_sg_b=v_sg_b, v_w_out=v_w_out, v_norm2=v_norm2, v_w_up=v_w_up, v_w_down=v_w_down, v_final_norm=v_final_norm)
    weights = {n: given[n] for n in TWIN_WEIGHTS}
    shared = {n: given[n] for n in SHARED_INPUTS}
    per_example = {n: given[n] for n in ['x']}
    grad_fn = _jax.value_and_grad(_loss, argnums=(0, 1))

    def one_microbatch(ex, loss_target):
        ex = dict(ex)
        diff = ex.pop(TWIN_DIFF_INPUT)
        return grad_fn(weights, diff, {**shared, **ex}, loss_target)

    if N_MICROBATCH == 1:
        loss, (grad_w, grad_x) = one_microbatch(per_example, given["loss_target"])
    else:
        def body(carry, xs):
            loss_sum, grad_sum = carry
            l_k, (gw_k, gx_k) = one_microbatch(xs[0], xs[1])
            with _jax.named_scope("update"):
                return (loss_sum + l_k, _jax.tree.map(_jnp.add, grad_sum, gw_k)), gx_k

        init = (_jnp.zeros((), _jnp.float32), _jax.tree.map(_jnp.zeros_like, weights))
        (loss, grad_w), grad_x = _jax.lax.scan(body, init, (per_example, given["loss_target"]))
    with _jax.named_scope("update"):
        delta_w, new_m, new_v = {}, {}, {}
        for n in TWIN_WEIGHTS:
            delta_w[n], new_m[n], new_v[n] = _adamw(weights[n], grad_w[n], given["m_" + n], given["v_" + n])
    return (loss, grad_x, *[grad_w[n] for n in TWIN_WEIGHTS], *[delta_w[n] for n in TWIN_WEIGHTS],
            *[new_m[n] for n in TWIN_WEIGHTS], *[new_v[n] for n in TWIN_WEIGHTS])
```

```python
import functools

import jax
import jax.numpy as jnp
from jax import lax
from jax.experimental import pallas as pl
from jax.experimental.pallas import tpu as pltpu

F32 = jnp.float32
BF16 = jnp.bfloat16

D_MODEL = 1024
DEPTH = 2
POOL_WIDTH = 256
SG_WIDTH = 256
SB_WIDTH = 512
POOL_WINDOWS = (2, 4, 8, 16)
POOL_GW = 64
POOL_HALO = 16
CHUNK = 128
SG_HEADS = 4
SG_HD = 64
SB_HD = 64
SB_PAIRS = SB_WIDTH // 128
AB_COLS = POOL_WIDTH + 2 * SG_WIDTH
IN_COLS = AB_COLS + 3 * SB_WIDTH
D_FF = 4096
EPS = 1e-6
N_DEV = 8
FF_SHARD = D_FF // N_DEV
IN_SHARD = IN_COLS // N_DEV
OUT_SHARD = D_MODEL // N_DEV
ADAM_LR = 0.001
ADAM_B1 = 0.9
ADAM_B2 = 0.999
ADAM_EPS = 1e-08
ADAM_WD = 0.01
ADAM_STEP = 10
VMEM_LIMIT = 52 * 1024 * 1024
MESH_AXES = ("x", "y", "c")


def _dot(a, b):
    return jnp.dot(a, b, preferred_element_type=F32)


def _dot_nt(a, b):
    return lax.dot_general(a, b, (((1,), (1,)), ((), ())), preferred_element_type=F32)


def _dot_tn(a, b):
    return lax.dot_general(a, b, (((0,), (0,)), ((), ())), preferred_element_type=F32)


def _rstd(x):
    return lax.rsqrt(jnp.mean(x * x, axis=-1, keepdims=True) + EPS)


def _rms_bwd(x, r, g, dh):
    gq = dh * g
    dx = r * gq - x * (r * r * r) * jnp.mean(gq * x, axis=-1, keepdims=True)
    dg = jnp.sum(dh * x * r, axis=0, keepdims=True)
    return dx, dg


def _params(sem=None):
    kw = dict(vmem_limit_bytes=VMEM_LIMIT)
    if sem is not None:
        kw["dimension_semantics"] = sem
    return pltpu.CompilerParams(**kw)


def _row_tile(rows, cap):
    best = 8
    for t in range(8, min(rows, cap) + 1, 8):
        if rows % t == 0:
            best = t
    return best


def _peer(k):
    x, y, c = lax.axis_index("x"), lax.axis_index("y"), lax.axis_index("c")
    return (1 - x if k & 4 else x, 1 - y if k & 2 else y, 1 - c if k & 1 else c)


def _my_index():
    return 4 * lax.axis_index("x") + 2 * lax.axis_index("y") + lax.axis_index("c")


def _all_gather(shard, name):
    m, n = shard.shape

    def body(x_ref, out_ref, send_sems, recv_sems, local_sem):
        me = _my_index()

        def copy(k):
            return pltpu.make_async_remote_copy(
                src_ref=x_ref, dst_ref=out_ref.at[me],
                send_sem=send_sems.at[k - 1], recv_sem=recv_sems.at[k - 1],
                device_id=_peer(k), device_id_type=pl.DeviceIdType.MESH)

        mine = pltpu.make_async_copy(x_ref, out_ref.at[me], local_sem)
        mine.start()
        copies = [copy(k) for k in range(1, N_DEV)]
        for cp in copies:
            cp.start()
        for cp in copies:
            cp.wait_recv()
        for cp in copies:
            cp.wait_send()
        mine.wait()

    return pl.pallas_call(
        body, name=name,
        out_shape=jax.ShapeDtypeStruct((N_DEV, m, n), shard.dtype),
        in_specs=[pl.BlockSpec(memory_space=pl.ANY)],
        out_specs=pl.BlockSpec(memory_space=pl.ANY),
        scratch_shapes=[pltpu.SemaphoreType.DMA((N_DEV - 1,)), pltpu.SemaphoreType.DMA((N_DEV - 1,)),
                        pltpu.SemaphoreType.DMA],
    )(shard)


def _all_to_all(parts, name):
    _, m, n = parts.shape

    def body(x_ref, out_ref, send_sems, recv_sems, local_sem):
        me = _my_index()

        def copy(k):
            px, py, pc = _peer(k)
            return pltpu.make_async_remote_copy(
                src_ref=x_ref.at[4 * px + 2 * py + pc], dst_ref=out_ref.at[me],
                send_sem=send_sems.at[k - 1], recv_sem=recv_sems.at[k - 1],
                device_id=(px, py, pc), device_id_type=pl.DeviceIdType.MESH)

        mine = pltpu.make_async_copy(x_ref.at[me], out_ref.at[me], local_sem)
        mine.start()
        copies = [copy(k) for k in range(1, N_DEV)]
        for cp in copies:
            cp.start()
        for cp in copies:
            cp.wait_recv()
        for cp in copies:
            cp.wait_send()
        mine.wait()

    return pl.pallas_call(
        body, name=name,
        out_shape=jax.ShapeDtypeStruct((N_DEV, m, n), parts.dtype),
        in_specs=[pl.BlockSpec(memory_space=pl.ANY)],
        out_specs=pl.BlockSpec(memory_space=pl.ANY),
        scratch_shapes=[pltpu.SemaphoreType.DMA((N_DEV - 1,)), pltpu.SemaphoreType.DMA((N_DEV - 1,)),
                        pltpu.SemaphoreType.DMA],
    )(parts)


def _reduce_adamw(parts, w, m, v, name):
    _, rows, n = parts.shape
    tr = _row_tile(rows, max(8, (1 << 18) // n))
    c1 = 1.0 - ADAM_B1 ** ADAM_STEP
    c2 = 1.0 - ADAM_B2 ** ADAM_STEP

    def body(p_ref, w_ref, m_ref, v_ref, g_ref, d_ref, nm_ref, nv_ref):
        g = p_ref[0]
        for s in range(1, N_DEV):
            g = g + p_ref[s]
        nm = ADAM_B1 * m_ref[...] + (1.0 - ADAM_B1) * g
        nv = ADAM_B2 * v_ref[...] + (1.0 - ADAM_B2) * (g * g)
        m_hat = nm / c1
        v_hat = nv / c2
        g_ref[...] = g
        d_ref[...] = -ADAM_LR * (m_hat / (jnp.sqrt(v_hat) + ADAM_EPS) + ADAM_WD * w_ref[...])
        nm_ref[...] = nm
        nv_ref[...] = nv

    blk = pl.BlockSpec((tr, n), lambda i: (i, 0))
    out = jax.ShapeDtypeStruct((rows, n), F32)
    return pl.pallas_call(
        body, name=name, grid=(rows // tr,),
        in_specs=[pl.BlockSpec((N_DEV, tr, n), lambda i: (0, i, 0)), blk, blk, blk],
        out_specs=[blk, blk, blk, blk], out_shape=[out, out, out, out],
        compiler_params=_params(("parallel",)),
    )(parts, w, m, v)


def _inproj_fwd(x, g, w, tm, name):
    t, d = x.shape
    n = w.shape[1]

    def body(x_ref, g_ref, w_ref, ab_ref, qkv_ref):
        xx = x_ref[...]
        h = (xx * _rstd(xx) * g_ref[...]).astype(BF16)
        ab_ref[...] = _dot(h, w_ref[:, :AB_COLS])
        qkv_ref[...] = _dot(h, w_ref[:, AB_COLS:]).astype(BF16)

    return pl.pallas_call(
        body, name=name, grid=(t // tm,),
        in_specs=[pl.BlockSpec((tm, d), lambda i: (i, 0)), pl.BlockSpec((1, d), lambda i: (0, 0)),
                  pl.BlockSpec((d, n), lambda i: (0, 0))],
        out_specs=[pl.BlockSpec((tm, AB_COLS), lambda i: (i, 0)), pl.BlockSpec((tm, n - AB_COLS), lambda i: (i, 0))],
        out_shape=[jax.ShapeDtypeStruct((t, AB_COLS), F32), jax.ShapeDtypeStruct((t, n - AB_COLS), BF16)],
        compiler_params=_params(("parallel",)),
    )(x, g, w)


def _pool_window_sums(xx, forward):
    n = xx.shape[0]
    sh = (lambda k: n - k) if forward else (lambda k: k)
    s2 = xx + pltpu.roll(xx, sh(1), 0)
    s4 = s2 + pltpu.roll(s2, sh(2), 0)
    s8 = s4 + pltpu.roll(s4, sh(4), 0)
    s16 = s8 + pltpu.roll(s8, sh(8), 0)
    grp = lax.broadcasted_iota(jnp.int32, (1, POOL_WIDTH), 1) // POOL_GW
    return jnp.where(grp == 0, s2, jnp.where(grp == 1, s4, jnp.where(grp == 2, s8, s16)))


def _pool_count(t0, rows):
    grp = lax.broadcasted_iota(jnp.int32, (1, POOL_WIDTH), 1) // POOL_GW
    win = jnp.where(grp == 0, 2, jnp.where(grp == 1, 4, jnp.where(grp == 2, 8, 16)))
    tt = t0 + lax.broadcasted_iota(jnp.int32, (rows, 1), 0)
    return jnp.minimum(tt + 1, win).astype(F32)


def _pool_diff(cur, prev, t0):
    tm = cur.shape[0]
    sums = _pool_window_sums(jnp.concatenate([prev, cur], axis=0), False)[POOL_HALO:]
    return sums / _pool_count(t0, tm) - cur


def _pool_fwd(ab, wbd, scale, tm, name):
    t = ab.shape[0]
    hb = tm // POOL_HALO

    def body(cur_ref, prev_ref, w_ref, s_ref, y_ref):
        i = pl.program_id(0)
        prev = jnp.where(i == 0, 0.0, prev_ref[...])
        d = _pool_diff(cur_ref[...], prev, i * tm).astype(BF16)
        y_ref[...] = (_dot(d, w_ref[...]) * s_ref[...]).astype(BF16)

    return pl.pallas_call(
        body, name=name, grid=(t // tm,),
        in_specs=[pl.BlockSpec((tm, POOL_WIDTH), lambda i: (i, 0)),
                  pl.BlockSpec((POOL_HALO, POOL_WIDTH), lambda i: (jnp.maximum(i * hb - 1, 0), 0)),
                  pl.BlockSpec((POOL_WIDTH, POOL_WIDTH), lambda i: (0, 0)),
                  pl.BlockSpec((1, POOL_WIDTH), lambda i: (0, 0))],
        out_specs=pl.BlockSpec((tm, POOL_WIDTH), lambda i: (i, 0)),
        out_shape=jax.ShapeDtypeStruct((t, POOL_WIDTH), BF16),
        compiler_params=_params(("parallel",)),
    )(ab, ab, wbd, scale)


_GELU_K = 0.7978845608028654
_GELU_A = 0.044715


def _gelu(x):
    return 0.5 * x * (1.0 + jnp.tanh(_GELU_K * (x + _GELU_A * (x * x * x))))


def _gelu_grad(x):
    th = jnp.tanh(_GELU_K * (x + _GELU_A * (x * x * x)))
    return 0.5 * (1.0 + th) + 0.5 * x * (1.0 - th * th) * (_GELU_K * (1.0 + 3.0 * _GELU_A * (x * x)))


def _head_lanes(h):
    return lax.broadcasted_iota(jnp.int32, (1, SG_WIDTH), 1) // SG_HD == h


def _sg_fwd(ab, gn, wm, bfull, tm, name):
    t = ab.shape[0]

    def body(u_ref, v_ref, gn_ref, wm_ref, b_ref, y_ref):
        v = _gelu(v_ref[...])
        vn = (v * _rstd(v) * gn_ref[...]).astype(BF16)
        for c in range(tm // CHUNK):
            rows = slice(c * CHUNK, (c + 1) * CHUNK)
            vc = vn[rows]
            sv = b_ref[...]
            for h in range(SG_HEADS):
                sv = sv + jnp.where(_head_lanes(h), _dot(wm_ref[h], vc), 0.0)
            y_ref[rows, :] = (_gelu(u_ref[rows, :]) * sv).astype(BF16)

    return pl.pallas_call(
        body, name=name, grid=(t // tm,),
        in_specs=[pl.BlockSpec((tm, SG_WIDTH), lambda i: (i, 1)), pl.BlockSpec((tm, SG_WIDTH), lambda i: (i, 2)),
                  pl.BlockSpec((1, SG_WIDTH), lambda i: (0, 0)),
                  pl.BlockSpec((SG_HEADS, CHUNK, CHUNK), lambda i: (0, 0, 0)),
                  pl.BlockSpec((CHUNK, SG_WIDTH), lambda i: (0, 0))],
        out_specs=pl.BlockSpec((tm, SG_WIDTH), lambda i: (i, 0)),
        out_shape=jax.ShapeDtypeStruct((t, SG_WIDTH), BF16),
        compiler_params=_params(("parallel",)),
    )(ab, ab, gn, wm, bfull)


def _log_sigmoids(z):
    lb = jnp.minimum(z, 0.0) - jnp.log1p(jnp.exp(-jnp.abs(z)))
    return lb, lb - z


def _split(x):
    hi = x.astype(BF16)
    return hi, (x - hi.astype(F32)).astype(BF16)


def _tri_dot(tri, x):
    hi, lo = _split(x)
    return _dot(tri, hi) + _dot(tri, lo)


def _sba_fwd(qkv, vtb, tq, name):
    t = qkv.shape[0]
    nb = t // tq
    upper = (jnp.arange(tq)[None, :] > jnp.arange(tq)[:, None]).astype(BF16)

    def body(q_ref, k_ref, vt_ref, up_ref, ot_ref, c_ref):
        i = pl.program_id(1)
        q = q_ref[...]
        up = up_ref[...]
        lane_head = lax.broadcasted_iota(jnp.int32, (1, 128), 1) // SB_HD
        sub_head = lax.broadcasted_iota(jnp.int32, (128, 1), 0) // SB_HD
        causal = (lax.broadcasted_iota(jnp.int32, (tq, tq), 0) < lax.broadcasted_iota(jnp.int32, (tq, tq), 1))
        acc = jnp.zeros((128, tq), F32)
        for h in range(2):
            qh = jnp.where(lane_head == h, q, jnp.zeros_like(q))

            def block(j, carry, diag, h=h, qh=qh):
                c, acc = carry
                kj = k_ref[pl.ds(pl.multiple_of(j * tq, tq), tq), :]
                z = _dot_nt(kj, qh) * 0.125
                lb, l1 = _log_sigmoids(z)
                if diag:
                    l1 = jnp.where(causal, l1, 0.0)
                after = _tri_dot(up, l1)
                c_ref[h, j] = c
                a = jnp.exp(lb + after + c)
                if diag:
                    a = jnp.where(causal, a, 0.0)
                vt = vt_ref[j]
                vth = jnp.where(sub_head == h, vt, jnp.zeros_like(vt))
                acc = acc + _dot(vth, a.astype(BF16))
                return c + after[0:1, :] + l1[0:1, :], acc

            carry = block(i, (jnp.zeros((1, tq), F32), acc), True)
            carry = lax.fori_loop(0, i, lambda s, cr: block(i - 1 - s, cr, False), carry)
            acc = carry[1]
        ot_ref[...] = acc.astype(BF16)

    return pl.pallas_call(
        body, name=name, grid=(SB_PAIRS, nb),
        in_specs=[pl.BlockSpec((tq, 128), lambda p, i: (i, p)),
                  pl.BlockSpec((t, 128), lambda p, i: (0, SB_PAIRS + p)),
                  pl.BlockSpec((None, nb, 128, tq), lambda p, i: (p, 0, 0, 0)),
                  pl.BlockSpec((tq, tq), lambda p, i: (0, 0))],
        out_specs=[pl.BlockSpec((128, tq), lambda p, i: (p, i)),
                   pl.BlockSpec((2, nb, 1, tq), lambda p, i: (p, 0, 0, i))],
        out_shape=[jax.ShapeDtypeStruct((SB_WIDTH, t), BF16),
                   jax.ShapeDtypeStruct((2 * SB_PAIRS, nb, 1, t), F32)],
        compiler_params=_params(("parallel", "parallel")),
    )(qkv, qkv, vtb, upper)


def _outproj_mlp_fwd(x, ya, yb, yc, wo, g2, wup, wdn, tm, name):
    t, d = x.shape
    nf = wup.shape[0]

    def body(x_ref, ya_ref, yb_ref, yc_ref, wo_ref, g_ref, wu_ref, wd_ref, x1_ref, u_ref, x2_ref, h_ref, acc_ref):
        j = pl.program_id(1)

        @pl.when(j == 0)
        def _():
            x1 = (x_ref[...] + _dot(ya_ref[...], wo_ref[0:POOL_WIDTH, :])
                  + _dot(yb_ref[...], wo_ref[POOL_WIDTH:POOL_WIDTH + SG_WIDTH, :])
                  + _dot(yc_ref[...], wo_ref[POOL_WIDTH + SG_WIDTH:, :]))
            x1_ref[...] = x1
            h_ref[...] = (x1 * _rstd(x1) * g_ref[...]).astype(BF16)
            acc_ref[...] = x1

        u = _dot(h_ref[...], wu_ref[...])
        u_ref[...] = u
        r = jnp.maximum(u, 0.0)
        acc_ref[...] += _dot((r * r).astype(BF16), wd_ref[...])

        @pl.when(j == nf - 1)
        def _():
            x2_ref[...] = acc_ref[...]

    row = lambda w: pl.BlockSpec((tm, w), lambda i, j: (i, 0))
    return pl.pallas_call(
        body, name=name, grid=(t // tm, nf),
        in_specs=[row(d), row(POOL_WIDTH), row(SG_WIDTH), row(SB_WIDTH),
                  pl.BlockSpec((d, d), lambda i, j: (0, 0)), pl.BlockSpec((1, d), lambda i, j: (0, 0)),
                  pl.BlockSpec((None, d, FF_SHARD), lambda i, j: (j, 0, 0)),
                  pl.BlockSpec((None, FF_SHARD, d), lambda i, j: (j, 0, 0))],
        out_specs=[row(d), pl.BlockSpec((tm, FF_SHARD), lambda i, j: (i, j)), row(d)],
        out_shape=[jax.ShapeDtypeStruct((t, d), F32), jax.ShapeDtypeStruct((t, nf * FF_SHARD), F32),
                   jax.ShapeDtypeStruct((t, d), F32)],
        scratch_shapes=[pltpu.VMEM((tm, d), BF16), pltpu.VMEM((tm, d), F32)],
        compiler_params=_params(("parallel", "arbitrary")),
    )(x, ya, yb, yc, wo, g2, wup, wdn)


def _loss_grad(x, g, target, tm, name):
    t, d = x.shape
    nt = t // tm

    def body(x_ref, g_ref, t_ref, loss_ref, dx_ref, dg_ref, sq_ref):
        i = pl.program_id(0)

        @pl.when(i == 0)
        def _():
            sq_ref[...] = jnp.zeros_like(sq_ref)
            dg_ref[...] = jnp.zeros_like(dg_ref)

        xx = x_ref[...]
        r = _rstd(xx)
        err = xx * r * g_ref[...] - t_ref[...]
        sq_ref[...] += jnp.sum(err * err, axis=0, keepdims=True)
        dx, dg = _rms_bwd(xx, r, g_ref[...], err * (1.0 / d))
        dx_ref[...] = dx
        dg_ref[...] += dg

        @pl.when(i == nt - 1)
        def _():
            loss_ref[...] = jnp.sum(sq_ref[...], axis=1, keepdims=True) * (0.5 / d)

    return pl.pallas_call(
        body, name=name, grid=(nt,),
        in_specs=[pl.BlockSpec((tm, d), lambda i: (i, 0)), pl.BlockSpec((1, d), lambda i: (0, 0)),
                  pl.BlockSpec((tm, d), lambda i: (i, 0))],
        out_specs=[pl.BlockSpec((1, 1), lambda i: (0, 0)), pl.BlockSpec((tm, d), lambda i: (i, 0)),
                   pl.BlockSpec((1, d), lambda i: (0, 0))],
        out_shape=[jax.ShapeDtypeStruct((1, 1), F32), jax.ShapeDtypeStruct((t, d), F32),
                   jax.ShapeDtypeStruct((1, d), F32)],
        scratch_shapes=[pltpu.VMEM((1, d), F32)],
        compiler_params=_params(("arbitrary",)),
    )(x, g, target)


def _mlp_bwd(dx2, x1, g2, u, wup, wdn, tm, name):
    t, d = dx2.shape
    nf = wup.shape[0]
    nt = t // tm

    def body(dx2_ref, x1_ref, g_ref, u_ref, wu_ref, wd_ref, dx1_ref, du_ref, r_ref, h_ref, dxb_ref, dg_ref, acc_ref):
        i, j = pl.program_id(0), pl.program_id(1)

        @pl.when(j == 0)
        def _():
            x1 = x1_ref[...]
            h_ref[...] = (x1 * _rstd(x1) * g_ref[...]).astype(BF16)
            dxb_ref[...] = dx2_ref[...].astype(BF16)
            acc_ref[...] = jnp.zeros_like(acc_ref)

        @pl.when((i == 0) & (j == 0))
        def _():
            dg_ref[...] = jnp.zeros_like(dg_ref)

        dr = _dot_nt(dxb_ref[...], wd_ref[...])
        ru = jnp.maximum(u_ref[...], 0.0)
        du = (dr * (2.0 * ru)).astype(BF16)
        du_ref[...] = du
        r_ref[...] = (ru * ru).astype(BF16)
        acc_ref[...] += _dot_nt(du, wu_ref[...])

        @pl.when(j == nf - 1)
        def _():
            x1 = x1_ref[...]
            dx, dg = _rms_bwd(x1, _rstd(x1), g_ref[...], acc_ref[...])
            dx1_ref[...] = dx2_ref[...] + dx
            dg_ref[...] += dg

    row = lambda w: pl.BlockSpec((tm, w), lambda i, j: (i, 0))
    col = pl.BlockSpec((tm, FF_SHARD), lambda i, j: (i, j))
    return pl.pallas_call(
        body, name=name, grid=(nt, nf),
        in_specs=[row(d), row(d), pl.BlockSpec((1, d), lambda i, j: (0, 0)), col,
                  pl.BlockSpec((None, d, FF_SHARD), lambda i, j: (j, 0, 0)),
                  pl.BlockSpec((None, FF_SHARD, d), lambda i, j: (j, 0, 0))],
        out_specs=[row(d), col, col, row(d), row(d), pl.BlockSpec((1, d), lambda i, j: (0, 0))],
        out_shape=[jax.ShapeDtypeStruct((t, d), F32), jax.ShapeDtypeStruct((t, nf * FF_SHARD), BF16),
                   jax.ShapeDtypeStruct((t, nf * FF_SHARD), BF16), jax.ShapeDtypeStruct((t, d), BF16),
                   jax.ShapeDtypeStruct((t, d), BF16), jax.ShapeDtypeStruct((1, d), F32)],
        scratch_shapes=[pltpu.VMEM((tm, d), F32)],
        compiler_params=_params(("arbitrary", "arbitrary")),
    )(dx2, x1, g2, u, wup, wdn)


def _matmul_tn(a, b, bm, bn, bt, name):
    t, m = a.shape
    n = b.shape[1]

    def body(a_ref, b_ref, o_ref):
        @pl.when(pl.program_id(2) == 0)
        def _():
            o_ref[...] = jnp.zeros_like(o_ref)

        o_ref[...] += _dot_tn(a_ref[...], b_ref[...])

    return pl.pallas_call(
        body, name=name, grid=(m // bm, n // bn, t // bt),
        in_specs=[pl.BlockSpec((bt, bm), lambda i, j, k: (k, i)), pl.BlockSpec((bt, bn), lambda i, j, k: (k, j))],
        out_specs=pl.BlockSpec((bm, bn), lambda i, j, k: (i, j)),
        out_shape=jax.ShapeDtypeStruct((m, n), F32),
        compiler_params=_params(("parallel", "parallel", "arbitrary")),
    )(a, b)


def _outproj_bwd(dx1, wo, tm, name):
    t, d = dx1.shape
    c2 = POOL_WIDTH + SG_WIDTH

    def body(dx_ref, wo_ref, dya_ref, dyb_ref, dyc_ref, dyct_ref, dxb_ref):
        dxb = dx_ref[...].astype(BF16)
        dxb_ref[...] = dxb
        dya_ref[...] = _dot_nt(dxb, wo_ref[0:POOL_WIDTH, :])
        dyb_ref[...] = _dot_nt(dxb, wo_ref[POOL_WIDTH:c2, :])
        dyc_ref[...] = _dot_nt(dxb, wo_ref[c2:, :]).astype(BF16)
        dyct_ref[...] = _dot_nt(wo_ref[c2:, :], dxb).astype(BF16)

    row = lambda w: pl.BlockSpec((tm, w), lambda i: (i, 0))
    return pl.pallas_call(
        body, name=name, grid=(t // tm,),
        in_specs=[row(d), pl.BlockSpec((d, d), lambda i: (0, 0))],
        out_specs=[row(POOL_WIDTH), row(SG_WIDTH), row(SB_WIDTH), pl.BlockSpec((SB_WIDTH, tm), lambda i: (0, i)), row(d)],
        out_shape=[jax.ShapeDtypeStruct((t, POOL_WIDTH), F32), jax.ShapeDtypeStruct((t, SG_WIDTH), F32),
                   jax.ShapeDtypeStruct((t, SB_WIDTH), BF16), jax.ShapeDtypeStruct((SB_WIDTH, t), BF16),
                   jax.ShapeDtypeStruct((t, d), BF16)],
        compiler_params=_params(("parallel",)),
    )(dx1, wo)


def _sba_bwd(qkv, ktb, dyc, dyct, cc, tq, name):
    t = qkv.shape[0]
    nb = t // tq
    idx = jnp.arange(tq)
    upper = (idx[None, :] > idx[:, None]).astype(BF16)
    lower = (idx[None, :] < idx[:, None]).astype(BF16)

    def body(q_ref, k_ref, v_ref, kt_ref, do_ref, dot_ref, c_ref, up_ref, lo_ref, dqt_ref, dk_ref, dv_ref):
        i = pl.program_id(1)

        @pl.when(i == 0)
        def _():
            dk_ref[...] = jnp.zeros_like(dk_ref)
            dv_ref[...] = jnp.zeros_like(dv_ref)

        q = q_ref[...]
        do = do_ref[...]
        dot = dot_ref[...]
        up = up_ref[...]
        lo = lo_ref[...]
        lane_head = lax.broadcasted_iota(jnp.int32, (1, 128), 1) // SB_HD
        sub_head = lax.broadcasted_iota(jnp.int32, (128, 1), 0) // SB_HD
        causal = (lax.broadcasted_iota(jnp.int32, (tq, tq), 0) < lax.broadcasted_iota(jnp.int32, (tq, tq), 1))
        dqt = jnp.zeros((128, tq), F32)
        for h in range(2):
            hm = lane_head == h
            qh = jnp.where(hm, q, jnp.zeros_like(q))
            doh = jnp.where(hm, do, jnp.zeros_like(do))

            def block(j, carry, diag, h=h, hm=hm, qh=qh, doh=doh):
                cg, dqt = carry
                rows = pl.ds(pl.multiple_of(j * tq, tq), tq)
                kj = k_ref[rows, :]
                vj = v_ref[rows, :]
                z = _dot_nt(kj, qh) * 0.125
                lb, l1 = _log_sigmoids(z)
                if diag:
                    l1 = jnp.where(causal, l1, 0.0)
                a = jnp.exp(lb + _tri_dot(up, l1) + c_ref[h, j])
                if diag:
                    a = jnp.where(causal, a, 0.0)
                g = a * _dot(jnp.where(hm, vj, jnp.zeros_like(vj)), dot)
                gsum = _tri_dot(lo, g) + cg
                dz = (g - jnp.exp(lb) * (g + gsum)) * 0.125
                if diag:
                    dz = jnp.where(causal, dz, 0.0)
                dzb = dz.astype(BF16)
                kt = kt_ref[j]
                dqt = dqt + _dot(jnp.where(sub_head == h, kt, jnp.zeros_like(kt)), dzb)
                dk_ref[rows, :] += _dot(dzb, qh)
                dv_ref[rows, :] += _dot(a.astype(BF16), doh)
                return gsum[tq - 1:tq, :] + g[tq - 1:tq, :], dqt

            carry = lax.fori_loop(0, i, lambda s, cr: block(s, cr, False), (jnp.zeros((1, tq), F32), dqt))
            dqt = block(i, carry, True)[1]
        dqt_ref[...] = dqt

    return pl.pallas_call(
        body, name=name, grid=(SB_PAIRS, nb),
        in_specs=[pl.BlockSpec((tq, 128), lambda p, i: (i, p)),
                  pl.BlockSpec((t, 128), lambda p, i: (0, SB_PAIRS + p)),
                  pl.BlockSpec((t, 128), lambda p, i: (0, 2 * SB_PAIRS + p)),
                  pl.BlockSpec((None, nb, 128, tq), lambda p, i: (p, 0, 0, 0)),
                  pl.BlockSpec((tq, 128), lambda p, i: (i, p)),
                  pl.BlockSpec((128, tq), lambda p, i: (p, i)),
                  pl.BlockSpec((2, nb, 1, tq), lambda p, i: (p, 0, 0, i)),
                  pl.BlockSpec((tq, tq), lambda p, i: (0, 0)),
                  pl.BlockSpec((tq, tq), lambda p, i: (0, 0))],
        out_specs=[pl.BlockSpec((128, tq), lambda p, i: (p, i)),
                   pl.BlockSpec((t, 128), lambda p, i: (0, p)),
                   pl.BlockSpec((t, 128), lambda p, i: (0, p))],
        out_shape=[jax.ShapeDtypeStruct((SB_WIDTH, t), F32), jax.ShapeDtypeStruct((t, SB_WIDTH), F32),
                   jax.ShapeDtypeStruct((t, SB_WIDTH), F32)],
        compiler_params=_params(("parallel", "arbitrary")),
    )(qkv, qkv, qkv, ktb, dyc, dyct, cc, upper, lower)


def _sg_bwd(ab, dyb, gn, wm, wmt, bfull, tm, name):
    t = ab.shape[0]
    nt = t // tm
    sel = (jnp.arange(SG_WIDTH)[:, None] // SG_HD == jnp.arange(CHUNK)[None, :]).astype(F32)

    def body(u_ref, v_ref, dy_ref, gn_ref, wm_ref, wmt_ref, b_ref, sel_ref,
             dup_ref, dvp_ref, dgn_ref, dw_ref, db_ref, dbacc_ref):
        i = pl.program_id(0)

        @pl.when(i == 0)
        def _():
            dgn_ref[...] = jnp.zeros_like(dgn_ref)
            dw_ref[...] = jnp.zeros_like(dw_ref)
            dbacc_ref[...] = jnp.zeros_like(dbacc_ref)

        tril = (lax.broadcasted_iota(jnp.int32, (CHUNK, CHUNK), 0) >= lax.broadcasted_iota(jnp.int32, (CHUNK, CHUNK), 1))
        gn_ = gn_ref[...]
        for c in range(tm // CHUNK):
            rows = slice(c * CHUNK, (c + 1) * CHUNK)
            up, vp, dy = u_ref[rows, :], v_ref[rows, :], dy_ref[rows, :]
            u, v = _gelu(up), _gelu(vp)
            r = _rstd(v)
            vn = (v * r * gn_).astype(BF16)
            sv = b_ref[...]
            for h in range(SG_HEADS):
                sv = sv + jnp.where(_head_lanes(h), _dot(wm_ref[h], vn), 0.0)
            dup_ref[rows, :] = dy * sv * _gelu_grad(up)
            dsv = dy * u
            dbacc_ref[...] += dsv
            dvn = jnp.zeros((CHUNK, SG_WIDTH), F32)
            for h in range(SG_HEADS):
                dsv_h = jnp.where(_head_lanes(h), dsv, 0.0).astype(BF16)
                dvn = dvn + _dot(wmt_ref[h], dsv_h)
                dw_ref[h] += jnp.where(tril, _dot_nt(dsv_h, vn), 0.0)
            dv, dgn = _rms_bwd(v, r, gn_, dvn)
            dgn_ref[...] += dgn
            dvp_ref[rows, :] = dv * _gelu_grad(vp)

        @pl.when(i == nt - 1)
        def _():
            db_ref[...] = jnp.dot(dbacc_ref[...], sel_ref[...], preferred_element_type=F32,
                                  precision=lax.Precision.HIGHEST)

    const = lambda shape: pl.BlockSpec(shape, lambda i: tuple(0 for _ in shape))
    return pl.pallas_call(
        body, name=name, grid=(nt,),
        in_specs=[pl.BlockSpec((tm, SG_WIDTH), lambda i: (i, 1)), pl.BlockSpec((tm, SG_WIDTH), lambda i: (i, 2)),
                  pl.BlockSpec((tm, SG_WIDTH), lambda i: (i, 0)), const((1, SG_WIDTH)),
                  const((SG_HEADS, CHUNK, CHUNK)), const((SG_HEADS, CHUNK, CHUNK)), const((CHUNK, SG_WIDTH)),
                  const((SG_WIDTH, CHUNK))],
        out_specs=[pl.BlockSpec((tm, SG_WIDTH), lambda i: (i, 0)), pl.BlockSpec((tm, SG_WIDTH), lambda i: (i, 0)),
                   const((1, SG_WIDTH)), const((SG_HEADS, CHUNK, CHUNK)), const((CHUNK, CHUNK))],
        out_shape=[jax.ShapeDtypeStruct((t, SG_WIDTH), F32), jax.ShapeDtypeStruct((t, SG_WIDTH), F32),
                   jax.ShapeDtypeStruct((1, SG_WIDTH), F32), jax.ShapeDtypeStruct((SG_HEADS, CHUNK, CHUNK), F32),
                   jax.ShapeDtypeStruct((CHUNK, CHUNK), F32)],
        scratch_shapes=[pltpu.VMEM((CHUNK, SG_WIDTH), F32)],
        compiler_params=_params(("arbitrary",)),
    )(ab, ab, dyb, gn, wm, wmt, bfull, sel)


def _pool_bwd(ab, dya, wbd, scale, tm, name):
    t = ab.shape[0]
    nt = t // tm
    hb = tm // POOL_HALO
    nh = t // POOL_HALO

    def body(cur_ref, prev_ref, dy_ref, dyn_ref, w_ref, s_ref, da_ref, dw_ref, ds_ref):
        i = pl.program_id(0)

        @pl.when(i == 0)
        def _():
            dw_ref[...] = jnp.zeros_like(dw_ref)
            ds_ref[...] = jnp.zeros_like(ds_ref)

        prev = jnp.where(i == 0, 0.0, prev_ref[...])
        d = _pool_diff(cur_ref[...], prev, i * tm).astype(BF16)
        dy = dy_ref[...]
        ds_ref[...] += jnp.sum(dy * _dot(d, w_ref[...]), axis=0, keepdims=True)
        dyn = jnp.where(i == nt - 1, 0.0, dyn_ref[...])
        dys = (jnp.concatenate([dy, dyn], axis=0) * s_ref[...]).astype(BF16)
        dw_ref[...] += _dot_tn(d, dys[:tm])
        dd = _dot_nt(dys, w_ref[...])
        fwd = _pool_window_sums(dd / _pool_count(i * tm, tm + POOL_HALO), True)
        da_ref[...] = fwd[:tm] - dd[:tm]

    return pl.pallas_call(
        body, name=name, grid=(nt,),
        in_specs=[pl.BlockSpec((tm, POOL_WIDTH), lambda i: (i, 0)),
                  pl.BlockSpec((POOL_HALO, POOL_WIDTH), lambda i: (jnp.maximum(i * hb - 1, 0), 0)),
                  pl.BlockSpec((tm, POOL_WIDTH), lambda i: (i, 0)),
                  pl.BlockSpec((POOL_HALO, POOL_WIDTH), lambda i: (jnp.minimum((i + 1) * hb, nh - 1), 0)),
                  pl.BlockSpec((POOL_WIDTH, POOL_WIDTH), lambda i: (0, 0)),
                  pl.BlockSpec((1, POOL_WIDTH), lambda i: (0, 0))],
        out_specs=[pl.BlockSpec((tm, POOL_WIDTH), lambda i: (i, 0)),
                   pl.BlockSpec((POOL_WIDTH, POOL_WIDTH), lambda i: (0, 0)),
                   pl.BlockSpec((1, POOL_WIDTH), lambda i: (0, 0))],
        out_shape=[jax.ShapeDtypeStruct((t, POOL_WIDTH), F32), jax.ShapeDtypeStruct((POOL_WIDTH, POOL_WIDTH), F32),
                   jax.ShapeDtypeStruct((1, POOL_WIDTH), F32)],
        compiler_params=_params(("arbitrary",)),
    )(ab, ab, dya, dya, wbd, scale)


def _inproj_bwd(dx1, x, g, da, dup, dvp, dq, dk, dv, w, tm, name):
    t, d = x.shape
    n = w.shape[1]
    nt = t // tm

    def body(dx1_ref, x_ref, g_ref, da_ref, du_ref, dv_ref, dq_ref, dk_ref, dvv_ref, w_ref,
             dx_ref, h_ref, dp_ref, dg_ref):
        @pl.when(pl.program_id(0) == 0)
        def _():
            dg_ref[...] = jnp.zeros_like(dg_ref)

        dp = jnp.concatenate([da_ref[...], du_ref[...], dv_ref[...], dq_ref[...], dk_ref[...], dvv_ref[...]],
                             axis=1).astype(BF16)
        dp_ref[...] = dp
        xx = x_ref[...]
        r = _rstd(xx)
        h_ref[...] = (xx * r * g_ref[...]).astype(BF16)
        dx, dg = _rms_bwd(xx, r, g_ref[...], _dot_nt(dp, w_ref[...]))
        dx_ref[...] = dx1_ref[...] + dx
        dg_ref[...] += dg

    row = lambda w_: pl.BlockSpec((tm, w_), lambda i: (i, 0))
    return pl.pallas_call(
        body, name=name, grid=(nt,),
        in_specs=[row(d), row(d), pl.BlockSpec((1, d), lambda i: (0, 0)), row(POOL_WIDTH), row(SG_WIDTH),
                  row(SG_WIDTH), row(SB_WIDTH), row(SB_WIDTH), row(SB_WIDTH), pl.BlockSpec((d, n), lambda i: (0, 0))],
        out_specs=[row(d), row(d), row(n), pl.BlockSpec((1, d), lambda i: (0, 0))],
        out_shape=[jax.ShapeDtypeStruct((t, d), F32), jax.ShapeDtypeStruct((t, d), BF16),
                   jax.ShapeDtypeStruct((t, n), BF16), jax.ShapeDtypeStruct((1, d), F32)],
        compiler_params=_params(("arbitrary",)),
    )(dx1, x, g, da, dup, dvp, dq, dk, dv, w)


def _pack_big(w_in, w_out, w_up, w_down):
    rows = []
    for l in range(DEPTH):
        rows += [w_in[l].reshape(-1, D_MODEL), w_out[l], w_up[l].reshape(-1, D_MODEL), w_down[l]]
    return jnp.concatenate(rows, axis=0)


_BIG_ROWS = (IN_SHARD, OUT_SHARD, FF_SHARD, FF_SHARD)


def _unpack_big(buf):
    outs = [[], [], [], []]
    off = 0
    for _ in range(DEPTH):
        for k, r in enumerate(_BIG_ROWS):
            outs[k].append(buf[off:off + r])
            off += r
    w_in, w_out, w_up, w_down = (jnp.stack(o) for o in outs)
    return (w_in.reshape(DEPTH, D_MODEL, IN_SHARD), w_out, w_up.reshape(DEPTH, D_MODEL, FF_SHARD), w_down)


def _gathered_weights(buf, l):
    off = l * sum(_BIG_ROWS)
    w_in = buf[:, off:off + IN_SHARD].reshape(N_DEV, D_MODEL, IN_SHARD).transpose(1, 0, 2).reshape(D_MODEL, IN_COLS)
    off += IN_SHARD
    w_out = buf[:, off:off + OUT_SHARD].reshape(D_MODEL, D_MODEL)
    off += OUT_SHARD
    w_up = buf[:, off:off + FF_SHARD].reshape(N_DEV, D_MODEL, FF_SHARD)
    off += FF_SHARD
    w_down = buf[:, off:off + FF_SHARD]
    return w_in, w_out, w_up, w_down


def _scatter_layout(dw_in, dw_out, dw_up, dw_down):
    d_in = dw_in.reshape(D_MODEL, N_DEV, IN_SHARD).transpose(1, 0, 2).reshape(N_DEV, IN_SHARD, D_MODEL)
    d_out = dw_out.reshape(N_DEV, OUT_SHARD, D_MODEL)
    d_up = dw_up.reshape(D_MODEL, N_DEV, FF_SHARD).transpose(1, 0, 2).reshape(N_DEV, FF_SHARD, D_MODEL)
    d_down = dw_down.reshape(N_DEV, FF_SHARD, D_MODEL)
    return jnp.concatenate([d_in, d_out, d_up, d_down], axis=1)


_SMALL_SHAPES = ((DEPTH, D_MODEL), (DEPTH, 4, POOL_GW, POOL_GW), (DEPTH, POOL_WIDTH), (DEPTH, SG_WIDTH),
                 (DEPTH, SG_HEADS, CHUNK, CHUNK), (DEPTH, SG_HEADS, CHUNK), (DEPTH, D_MODEL), (D_MODEL,))


def _pack_small(arrs):
    return jnp.concatenate([a.reshape(-1) for a in arrs]).reshape(-1, 128)


def _unpack_small(buf):
    flat = buf.reshape(-1)
    outs, off = [], 0
    for shp in _SMALL_SHAPES:
        size = 1
        for s in shp:
            size *= s
        outs.append(flat[off:off + size].reshape(shp))
        off += size
    return outs


def _local_step(x0, target, gathered, norm1, pool_w, pool_scale, sg_norm, sg_w, sg_b, norm2, final_norm):
    t = x0.shape[0]
    tm = min(512, t)
    tq = min(256, t // 4)
    nb = t // tq
    tril = jnp.tril(jnp.ones((CHUNK, CHUNK), F32))
    saved = []
    xl = x0
    for l in range(DEPTH):
        wi, wo, wu, wd = _gathered_weights(gathered, l)
        wbd = jnp.zeros((4, POOL_GW, 4, POOL_GW), F32)
        for gi in range(4):
            wbd = wbd.at[gi, :, gi, :].set(pool_w[l, gi])
        wbd = wbd.reshape(POOL_WIDTH, POOL_WIDTH).astype(BF16)
        wm = (sg_w[l] * tril).astype(BF16)
        wmt = wm.transpose(0, 2, 1)
        bfull = jnp.repeat(sg_b[l].T, SG_HD, axis=1)
        g1, g2 = norm1[l][None, :], norm2[l][None, :]
        psc, gn = pool_scale[l][None, :], sg_norm[l][None, :]

        ab, qkv = _inproj_fwd(xl, g1, wi, tm, f"inproj_fwd{l}")
        ya = _pool_fwd(ab, wbd, psc, tm, f"pool_fwd{l}")
        yb = _sg_fwd(ab, gn, wm, bfull, tm, f"sg_fwd{l}")
        vtb = qkv[:, 2 * SB_WIDTH:].reshape(nb, tq, SB_PAIRS, 128).transpose(2, 0, 3, 1)
        oct_, cc = _sba_fwd(qkv, vtb, tq, f"sba_fwd{l}")
        yc = oct_.T
        x1, u, x2 = _outproj_mlp_fwd(xl, ya, yb, yc, wo, g2, wu, wd, tm, f"mlp_fwd{l}")
        saved.append(dict(x=xl, ab=ab, qkv=qkv, cc=cc, ya=ya, yb=yb, yc=yc, x1=x1, u=u, wi=wi, wo=wo, wu=wu, wd=wd,
                          wbd=wbd, wm=wm, wmt=wmt, bfull=bfull, g1=g1, g2=g2, psc=psc, gn=gn))
        xl = x2

    loss_local, dx, dfinal = _loss_grad(xl, final_norm[None, :], target, tm, "loss_grad")

    big_parts = [None] * DEPTH
    small = {}
    for l in reversed(range(DEPTH)):
        s = saved[l]
        dx1, du, r, h2, dx2b, dn2 = _mlp_bwd(dx, s["x1"], s["g2"], s["u"], s["wu"], s["wd"], tm, f"mlp_bwd{l}")
        dw_up = _matmul_tn(h2, du, D_MODEL, 1024, tm, f"dw_up{l}")
        dw_down = _matmul_tn(r, dx2b, 1024, D_MODEL, tm, f"dw_down{l}")
        dya, dyb, dyc, dyct, dx1b = _outproj_bwd(dx1, s["wo"], tm, f"outproj_bwd{l}")
        mix = jnp.concatenate([s["ya"], s["yb"], s["yc"]], axis=1)
        dw_out = _matmul_tn(mix, dx1b, D_MODEL, D_MODEL, tm, f"dw_out{l}")
        ktb = s["qkv"][:, SB_WIDTH:2 * SB_WIDTH].reshape(nb, tq, SB_PAIRS, 128).transpose(2, 0, 3, 1)
        dqt, dk, dv = _sba_bwd(s["qkv"], ktb, dyc, dyct, s["cc"], tq, f"sba_bwd{l}")
        dup, dvp, dgn, dwm, dbm = _sg_bwd(s["ab"], dyb, s["gn"], s["wm"], s["wmt"], s["bfull"], tm, f"sg_bwd{l}")
        da, dwbd, dpsc = _pool_bwd(s["ab"], dya, s["wbd"], s["psc"], tm, f"pool_bwd{l}")
        dx, h1, dproj, dn1 = _inproj_bwd(dx1, s["x"], s["g1"], da, dup, dvp, dqt.T, dk, dv, s["wi"], tm,
                                         f"inproj_bwd{l}")
        dw_in = _matmul_tn(h1, dproj, D_MODEL, IN_COLS // 3, tm, f"dw_in{l}")
        big_parts[l] = _scatter_layout(dw_in, dw_out, dw_up, dw_down)
        dpw = jnp.stack([dwbd[gi * POOL_GW:(gi + 1) * POOL_GW, gi * POOL_GW:(gi + 1) * POOL_GW] for gi in range(4)])
        small[l] = (dn1[0], dpw, dpsc[0], dgn[0], dwm, dbm[:, :SG_HEADS].T, dn2[0])
    small_local = [jnp.stack([small[l][k] for l in range(DEPTH)]) for k in range(7)] + [dfinal[0]]
    return loss_local, dx, jnp.concatenate(big_parts, axis=1), small_local


def kernel(x, norm1, w_in, pool_w, pool_scale, sg_norm, sg_w, sg_b, w_out, norm2, w_up, w_down, final_norm, loss_target, m_norm1, m_w_in, m_pool_w, m_pool_scale, m_sg_norm, m_sg_w, m_sg_b, m_w_out, m_norm2, m_w_up, m_w_down, m_final_norm, v_norm1, v_w_in, v_pool_w, v_pool_scale, v_sg_norm, v_sg_w, v_sg_b, v_w_out, v_norm2, v_w_up, v_w_down, v_final_norm):
    t = x.shape[1]
    gathered = _all_gather(_pack_big(w_in, w_out, w_up, w_down).astype(BF16), "gather_weights")
    loss_local, dx, big_parts, small_local = _local_step(
        x.reshape(t, D_MODEL), loss_target.reshape(t, D_MODEL), gathered,
        norm1, pool_w, pool_scale, sg_norm, sg_w, sg_b, norm2, final_norm)
    loss = lax.psum(loss_local[0, 0], MESH_AXES)
    grad_x = dx.reshape(x.shape)

    received = _all_to_all(big_parts, "scatter_grads")
    gb, db, mb, vb = _reduce_adamw(received, _pack_big(w_in, w_out, w_up, w_down), _pack_big(m_w_in, m_w_out, m_w_up, m_w_down),
                                   _pack_big(v_w_in, v_w_out, v_w_up, v_w_down), "adamw_sharded")
    big = [_unpack_big(b) for b in (gb, db, mb, vb)]

    small_all = _all_gather(_pack_small(small_local), "gather_small_grads")
    gs, ds, ms, vs = _reduce_adamw(
        small_all,
        _pack_small([norm1, pool_w, pool_scale, sg_norm, sg_w, sg_b, norm2, final_norm]),
        _pack_small([m_norm1, m_pool_w, m_pool_scale, m_sg_norm, m_sg_w, m_sg_b, m_norm2, m_final_norm]),
        _pack_small([v_norm1, v_pool_w, v_pool_scale, v_sg_norm, v_sg_w, v_sg_b, v_norm2, v_final_norm]),
        "adamw_replicated")
    sm = [_unpack_small(b) for b in (gs, ds, ms, vs)]

    def leaves(bg, smv):
        n1, pw, ps, sn, sw, sb, n2, fn = smv
        wi_, wo_, wu_, wd_ = bg
        return [n1, wi_, pw, ps, sn, sw, sb, wo_, n2, wu_, wd_, fn]

    out = [loss, grad_x]
    for k in range(4):
        out += leaves(big[k], sm[k])
    return tuple(out)
```

```python
import functools

import jax
import jax.numpy as jnp
from jax import lax
from jax.experimental import pallas as pl
from jax.experimental.pallas import tpu as pltpu

F32 = jnp.float32
BF16 = jnp.bfloat16

D_MODEL = 1024
DEPTH = 2
POOL_WIDTH = 256
SG_WIDTH = 256
SB_WIDTH = 512
POOL_WINDOWS = (2, 4, 8, 16)
POOL_GW = 64
POOL_HALO = 16
CHUNK = 128
SG_HEADS = 4
SG_HD = 64
SB_HD = 64
SB_PAIRS = SB_WIDTH // 128
AB_COLS = POOL_WIDTH + 2 * SG_WIDTH
IN_COLS = AB_COLS + 3 * SB_WIDTH
D_FF = 4096
EPS = 1e-6
N_DEV = 8
FF_SHARD = D_FF // N_DEV
IN_SHARD = IN_COLS // N_DEV
OUT_SHARD = D_MODEL // N_DEV
ADAM_LR = 0.001
ADAM_B1 = 0.9
ADAM_B2 = 0.999
ADAM_EPS = 1e-08
ADAM_WD = 0.01
ADAM_STEP = 10
VMEM_LIMIT = 52 * 1024 * 1024
MESH_AXES = ("x", "y", "c")


def _dot(a, b):
    return jnp.dot(a, b, preferred_element_type=F32)


def _dot_nt(a, b):
    return lax.dot_general(a, b, (((1,), (1,)), ((), ())), preferred_element_type=F32)


def _dot_tn(a, b):
    return lax.dot_general(a, b, (((0,), (0,)), ((), ())), preferred_element_type=F32)


def _rstd(x):
    return lax.rsqrt(jnp.mean(x * x, axis=-1, keepdims=True) + EPS)


def _rms_bwd(x, r, g, dh):
    gq = dh * g
    dx = r * gq - x * (r * r * r) * jnp.mean(gq * x, axis=-1, keepdims=True)
    dg = jnp.sum(dh * x * r, axis=0, keepdims=True)
    return dx, dg


def _params(sem=None):
    kw = dict(vmem_limit_bytes=VMEM_LIMIT)
    if sem is not None:
        kw["dimension_semantics"] = sem
    return pltpu.CompilerParams(**kw)


def _row_tile(rows, cap):
    best = 8
    for t in range(8, min(rows, cap) + 1, 8):
        if rows % t == 0:
            best = t
    return best


def _peer(k):
    x, y, c = lax.axis_index("x"), lax.axis_index("y"), lax.axis_index("c")
    return (1 - x if k & 4 else x, 1 - y if k & 2 else y, 1 - c if k & 1 else c)


def _my_index():
    return 4 * lax.axis_index("x") + 2 * lax.axis_index("y") + lax.axis_index("c")


def _all_gather(shard, name):
    m, n = shard.shape

    def body(x_ref, out_ref, send_sems, recv_sems, local_sem):
        me = _my_index()

        def copy(k):
            return pltpu.make_async_remote_copy(
                src_ref=x_ref, dst_ref=out_ref.at[me],
                send_sem=send_sems.at[k - 1], recv_sem=recv_sems.at[k - 1],
                device_id=_peer(k), device_id_type=pl.DeviceIdType.MESH)

        mine = pltpu.make_async_copy(x_ref, out_ref.at[me], local_sem)
        mine.start()
        copies = [copy(k) for k in range(1, N_DEV)]
        for cp in copies:
            cp.start()
        for cp in copies:
            cp.wait_recv()
        for cp in copies:
            cp.wait_send()
        mine.wait()

    return pl.pallas_call(
        body, name=name,
        out_shape=jax.ShapeDtypeStruct((N_DEV, m, n), shard.dtype),
        in_specs=[pl.BlockSpec(memory_space=pl.ANY)],
        out_specs=pl.BlockSpec(memory_space=pl.ANY),
        scratch_shapes=[pltpu.SemaphoreType.DMA((N_DEV - 1,)), pltpu.SemaphoreType.DMA((N_DEV - 1,)),
                        pltpu.SemaphoreType.DMA],
    )(shard)


def _all_to_all(parts, name):
    _, m, n = parts.shape

    def body(x_ref, out_ref, send_sems, recv_sems, local_sem):
        me = _my_index()

        def copy(k):
            px, py, pc = _peer(k)
            return pltpu.make_async_remote_copy(
                src_ref=x_ref.at[4 * px + 2 * py + pc], dst_ref=out_ref.at[me],
                send_sem=send_sems.at[k - 1], recv_sem=recv_sems.at[k - 1],
                device_id=(px, py, pc), device_id_type=pl.DeviceIdType.MESH)

        mine = pltpu.make_async_copy(x_ref.at[me], out_ref.at[me], local_sem)
        mine.start()
        copies = [copy(k) for k in range(1, N_DEV)]
        for cp in copies:
            cp.start()
        for cp in copies:
            cp.wait_recv()
        for cp in copies:
            cp.wait_send()
        mine.wait()

    return pl.pallas_call(
        body, name=name,
        out_shape=jax.ShapeDtypeStruct((N_DEV, m, n), parts.dtype),
        in_specs=[pl.BlockSpec(memory_space=pl.ANY)],
        out_specs=pl.BlockSpec(memory_space=pl.ANY),
        scratch_shapes=[pltpu.SemaphoreType.DMA((N_DEV - 1,)), pltpu.SemaphoreType.DMA((N_DEV - 1,)),
                        pltpu.SemaphoreType.DMA],
    )(parts)


def _reduce_adamw(parts, w, m, v, name):
    _, rows, n = parts.shape
    tr = _row_tile(rows, max(8, (1 << 18) // n))
    c1 = 1.0 - ADAM_B1 ** ADAM_STEP
    c2 = 1.0 - ADAM_B2 ** ADAM_STEP

    def body(p_ref, w_ref, m_ref, v_ref, g_ref, d_ref, nm_ref, nv_ref):
        g = p_ref[0]
        for s in range(1, N_DEV):
            g = g + p_ref[s]
        nm = ADAM_B1 * m_ref[...] + (1.0 - ADAM_B1) * g
        nv = ADAM_B2 * v_ref[...] + (1.0 - ADAM_B2) * (g * g)
        m_hat = nm / c1
        v_hat = nv / c2
        g_ref[...] = g
        d_ref[...] = -ADAM_LR * (m_hat / (jnp.sqrt(v_hat) + ADAM_EPS) + ADAM_WD * w_ref[...])
        nm_ref[...] = nm
        nv_ref[...] = nv

    blk = pl.BlockSpec((tr, n), lambda i: (i, 0))
    out = jax.ShapeDtypeStruct((rows, n), F32)
    return pl.pallas_call(
        body, name=name, grid=(rows // tr,),
        in_specs=[pl.BlockSpec((N_DEV, tr, n), lambda i: (0, i, 0)), blk, blk, blk],
        out_specs=[blk, blk, blk, blk], out_shape=[out, out, out, out],
        compiler_params=_params(("parallel",)),
    )(parts, w, m, v)


def _inproj_fwd(x, g, w, tm, name):
    t, d = x.shape
    n = w.shape[1]

    def body(x_ref, g_ref, w_ref, ab_ref, qkv_ref):
        xx = x_ref[...]
        h = (xx * _rstd(xx) * g_ref[...]).astype(BF16)
        ab_ref[...] = _dot(h, w_ref[:, :AB_COLS])
        qkv_ref[...] = _dot(h, w_ref[:, AB_COLS:]).astype(BF16)

    return pl.pallas_call(
        body, name=name, grid=(t // tm,),
        in_specs=[pl.BlockSpec((tm, d), lambda i: (i, 0)), pl.BlockSpec((1, d), lambda i: (0, 0)),
                  pl.BlockSpec((d, n), lambda i: (0, 0))],
        out_specs=[pl.BlockSpec((tm, AB_COLS), lambda i: (i, 0)), pl.BlockSpec((tm, n - AB_COLS), lambda i: (i, 0))],
        out_shape=[jax.ShapeDtypeStruct((t, AB_COLS), F32), jax.ShapeDtypeStruct((t, n - AB_COLS), BF16)],
        compiler_params=_params(("parallel",)),
    )(x, g, w)


def _pool_window_sums(xx, forward):
    n = xx.shape[0]
    sh = (lambda k: n - k) if forward else (lambda k: k)
    s2 = xx + pltpu.roll(xx, sh(1), 0)
    s4 = s2 + pltpu.roll(s2, sh(2), 0)
    s8 = s4 + pltpu.roll(s4, sh(4), 0)
    s16 = s8 + pltpu.roll(s8, sh(8), 0)
    grp = lax.broadcasted_iota(jnp.int32, (1, POOL_WIDTH), 1) // POOL_GW
    return jnp.where(grp == 0, s2, jnp.where(grp == 1, s4, jnp.where(grp == 2, s8, s16)))


def _pool_count(t0, rows):
    grp = lax.broadcasted_iota(jnp.int32, (1, POOL_WIDTH), 1) // POOL_GW
    win = jnp.where(grp == 0, 2, jnp.where(grp == 1, 4, jnp.where(grp == 2, 8, 16)))
    tt = t0 + lax.broadcasted_iota(jnp.int32, (rows, 1), 0)
    return jnp.minimum(tt + 1, win).astype(F32)


def _pool_diff(cur, prev, t0):
    tm = cur.shape[0]
    sums = _pool_window_sums(jnp.concatenate([prev, cur], axis=0), False)[POOL_HALO:]
    return sums / _pool_count(t0, tm) - cur


def _pool_fwd(ab, wbd, scale, tm, name):
    t = ab.shape[0]
    hb = tm // POOL_HALO

    def body(cur_ref, prev_ref, w_ref, s_ref, y_ref):
        i = pl.program_id(0)
        prev = jnp.where(i == 0, 0.0, prev_ref[...])
        d = _pool_diff(cur_ref[...], prev, i * tm).astype(BF16)
        y_ref[...] = (_dot(d, w_ref[...]) * s_ref[...]).astype(BF16)

    return pl.pallas_call(
        body, name=name, grid=(t // tm,),
        in_specs=[pl.BlockSpec((tm, POOL_WIDTH), lambda i: (i, 0)),
                  pl.BlockSpec((POOL_HALO, POOL_WIDTH), lambda i: (jnp.maximum(i * hb - 1, 0), 0)),
                  pl.BlockSpec((POOL_WIDTH, POOL_WIDTH), lambda i: (0, 0)),
                  pl.BlockSpec((1, POOL_WIDTH), lambda i: (0, 0))],
        out_specs=pl.BlockSpec((tm, POOL_WIDTH), lambda i: (i, 0)),
        out_shape=jax.ShapeDtypeStruct((t, POOL_WIDTH), BF16),
        compiler_params=_params(("parallel",)),
    )(ab, ab, wbd, scale)


_GELU_K = 0.7978845608028654
_GELU_A = 0.044715


def _gelu(x):
    return 0.5 * x * (1.0 + jnp.tanh(_GELU_K * (x + _GELU_A * (x * x * x))))


def _gelu_grad(x):
    th = jnp.tanh(_GELU_K * (x + _GELU_A * (x * x * x)))
    return 0.5 * (1.0 + th) + 0.5 * x * (1.0 - th * th) * (_GELU_K * (1.0 + 3.0 * _GELU_A * (x * x)))


def _head_lanes(h):
    return lax.broadcasted_iota(jnp.int32, (1, SG_WIDTH), 1) // SG_HD == h


def _sg_fwd(ab, gn, wm, bfull, tm, name):
    t = ab.shape[0]

    def body(u_ref, v_ref, gn_ref, wm_ref, b_ref, y_ref):
        v = _gelu(v_ref[...])
        vn = (v * _rstd(v) * gn_ref[...]).astype(BF16)
        for c in range(tm // CHUNK):
            rows = slice(c * CHUNK, (c + 1) * CHUNK)
            vc = vn[rows]
            sv = b_ref[...]
            for h in range(SG_HEADS):
                sv = sv + jnp.where(_head_lanes(h), _dot(wm_ref[h], vc), 0.0)
            y_ref[rows, :] = (_gelu(u_ref[rows, :]) * sv).astype(BF16)

    return pl.pallas_call(
        body, name=name, grid=(t // tm,),
        in_specs=[pl.BlockSpec((tm, SG_WIDTH), lambda i: (i, 1)), pl.BlockSpec((tm, SG_WIDTH), lambda i: (i, 2)),
                  pl.BlockSpec((1, SG_WIDTH), lambda i: (0, 0)),
                  pl.BlockSpec((SG_HEADS, CHUNK, CHUNK), lambda i: (0, 0, 0)),
                  pl.BlockSpec((CHUNK, SG_WIDTH), lambda i: (0, 0))],
        out_specs=pl.BlockSpec((tm, SG_WIDTH), lambda i: (i, 0)),
        out_shape=jax.ShapeDtypeStruct((t, SG_WIDTH), BF16),
        compiler_params=_params(("parallel",)),
    )(ab, ab, gn, wm, bfull)


LOG2E = 1.4426950408889634
SB_SCALE = 0.125 * LOG2E
SB_DEAD_LOG2 = 152.0


def _log2_sigmoids(y):
    neg_abs = lax.bitcast_convert_type(lax.bitcast_convert_type(y, jnp.uint32) | jnp.uint32(0x80000000), F32)
    lb = jnp.minimum(y, 0.0) - jnp.log(1.0 + jnp.exp2(neg_abs)) * LOG2E
    return lb, lb - y


def _split(x):
    hi = x.astype(BF16)
    return hi, (x - hi.astype(F32)).astype(BF16)


def _tri_dot(tri, x):
    hi, lo = _split(x)
    return _dot(tri, hi) + _dot(tri, lo)


def _sba_fwd(qkv, vtb, tq, name):
    t = qkv.shape[0]
    nb = t // tq
    upper = (jnp.arange(tq)[None, :] > jnp.arange(tq)[:, None]).astype(BF16)

    def body(q_ref, k_ref, vt_ref, up_ref, ot_ref, c_ref, n_ref):
        i = pl.program_id(1)
        q = q_ref[...]
        up = up_ref[...]
        lane_head = lax.broadcasted_iota(jnp.int32, (1, 128), 1) // SB_HD
        sub_head = lax.broadcasted_iota(jnp.int32, (128, 1), 0) // SB_HD
        causal = (lax.broadcasted_iota(jnp.int32, (tq, tq), 0) < lax.broadcasted_iota(jnp.int32, (tq, tq), 1))
        qh = [jnp.where(lane_head == h, q, jnp.zeros_like(q)) for h in range(2)]

        def block(j, carry, diag):
            cs, acc = list(carry[:2]), carry[2]
            kj = k_ref[pl.ds(pl.multiple_of(j * tq, tq), tq), :]
            vt = vt_ref[j]
            hs = range(2)
            z = [_dot_nt(kj, qh[h]) for h in hs]
            ls = [_log2_sigmoids(z[h] * SB_SCALE) for h in hs]
            lb = [ls[h][0] for h in hs]
            l1 = [jnp.where(causal, ls[h][1], 0.0) if diag else ls[h][1] for h in hs]
            after = [_tri_dot(up, l1[h]) for h in hs]
            a = [jnp.exp2(lb[h] + after[h] + cs[h]) for h in hs]
            if diag:
                a = [jnp.where(causal, a[h], 0.0) for h in hs]
            for h in hs:
                c_ref[h, j] = cs[h]
                acc = acc + _dot(jnp.where(sub_head == h, vt, jnp.zeros_like(vt)), a[h].astype(BF16))
                cs[h] = cs[h] + after[h][0:1, :] + l1[h][0:1, :]
            return cs[0], cs[1], acc

        def alive(c0, c1):
            return (jnp.max(jnp.maximum(c0, c1)) > -SB_DEAD_LOG2).astype(jnp.int32)

        def step(state):
            s, _, c0, c1, acc = state
            c0, c1, acc = block(i - 1 - s, (c0, c1, acc), False)
            return s + 1, alive(c0, c1), c0, c1, acc

        zero = jnp.zeros((1, tq), F32)
        c0, c1, acc = block(i, (zero, zero, jnp.zeros((128, tq), F32)), True)
        state = lax.while_loop(lambda st: (st[0] < i) & (st[1] > 0), step, (jnp.int32(0), alive(c0, c1), c0, c1, acc))
        ot_ref[...] = state[4].astype(BF16)
        n_ref[pl.program_id(0), i] = (state[0] + 1).astype(F32)

    return pl.pallas_call(
        body, name=name, grid=(SB_PAIRS, nb),
        in_specs=[pl.BlockSpec((tq, 128), lambda p, i: (i, p)),
                  pl.BlockSpec((t, 128), lambda p, i: (0, SB_PAIRS + p)),
                  pl.BlockSpec((None, nb, 128, tq), lambda p, i: (p, 0, 0, 0)),
                  pl.BlockSpec((tq, tq), lambda p, i: (0, 0))],
        out_specs=[pl.BlockSpec((128, tq), lambda p, i: (p, i)),
                   pl.BlockSpec((2, nb, 1, tq), lambda p, i: (p, 0, 0, i)),
                   pl.BlockSpec(memory_space=pltpu.SMEM)],
        out_shape=[jax.ShapeDtypeStruct((SB_WIDTH, t), BF16),
                   jax.ShapeDtypeStruct((2 * SB_PAIRS, nb, 1, t), F32),
                   jax.ShapeDtypeStruct((SB_PAIRS, nb), F32)],
        compiler_params=_params(("arbitrary", "arbitrary")),
    )(qkv, qkv, vtb, upper)


def _outproj_mlp_fwd(x, ya, yb, yc, wo, g2, wup, wdn, tm, name):
    t, d = x.shape
    nf = wup.shape[0]

    def body(x_ref, ya_ref, yb_ref, yc_ref, wo_ref, g_ref, wu_ref, wd_ref, x1_ref, u_ref, x2_ref, h_ref, acc_ref):
        j = pl.program_id(1)

        @pl.when(j == 0)
        def _():
            x1 = (x_ref[...] + _dot(ya_ref[...], wo_ref[0:POOL_WIDTH, :])
                  + _dot(yb_ref[...], wo_ref[POOL_WIDTH:POOL_WIDTH + SG_WIDTH, :])
                  + _dot(yc_ref[...], wo_ref[POOL_WIDTH + SG_WIDTH:, :]))
            x1_ref[...] = x1
            h_ref[...] = (x1 * _rstd(x1) * g_ref[...]).astype(BF16)
            acc_ref[...] = x1

        u = _dot(h_ref[...], wu_ref[...])
        u_ref[...] = u
        r = jnp.maximum(u, 0.0)
        acc_ref[...] += _dot((r * r).astype(BF16), wd_ref[...])

        @pl.when(j == nf - 1)
        def _():
            x2_ref[...] = acc_ref[...]

    row = lambda w: pl.BlockSpec((tm, w), lambda i, j: (i, 0))
    return pl.pallas_call(
        body, name=name, grid=(t // tm, nf),
        in_specs=[row(d), row(POOL_WIDTH), row(SG_WIDTH), row(SB_WIDTH),
                  pl.BlockSpec((d, d), lambda i, j: (0, 0)), pl.BlockSpec((1, d), lambda i, j: (0, 0)),
                  pl.BlockSpec((None, d, FF_SHARD), lambda i, j: (j, 0, 0)),
                  pl.BlockSpec((None, FF_SHARD, d), lambda i, j: (j, 0, 0))],
        out_specs=[row(d), pl.BlockSpec((tm, FF_SHARD), lambda i, j: (i, j)), row(d)],
        out_shape=[jax.ShapeDtypeStruct((t, d), F32), jax.ShapeDtypeStruct((t, nf * FF_SHARD), F32),
                   jax.ShapeDtypeStruct((t, d), F32)],
        scratch_shapes=[pltpu.VMEM((tm, d), BF16), pltpu.VMEM((tm, d), F32)],
        compiler_params=_params(("parallel", "arbitrary")),
    )(x, ya, yb, yc, wo, g2, wup, wdn)


def _loss_grad(x, g, target, tm, name):
    t, d = x.shape
    nt = t // tm

    def body(x_ref, g_ref, t_ref, loss_ref, dx_ref, dg_ref, sq_ref):
        i = pl.program_id(0)

        @pl.when(i == 0)
        def _():
            sq_ref[...] = jnp.zeros_like(sq_ref)
            dg_ref[...] = jnp.zeros_like(dg_ref)

        xx = x_ref[...]
        r = _rstd(xx)
        err = xx * r * g_ref[...] - t_ref[...]
        sq_ref[...] += jnp.sum(err * err, axis=0, keepdims=True)
        dx, dg = _rms_bwd(xx, r, g_ref[...], err * (1.0 / d))
        dx_ref[...] = dx
        dg_ref[...] += dg

        @pl.when(i == nt - 1)
        def _():
            loss_ref[...] = jnp.sum(sq_ref[...], axis=1, keepdims=True) * (0.5 / d)

    return pl.pallas_call(
        body, name=name, grid=(nt,),
        in_specs=[pl.BlockSpec((tm, d), lambda i: (i, 0)), pl.BlockSpec((1, d), lambda i: (0, 0)),
                  pl.BlockSpec((tm, d), lambda i: (i, 0))],
        out_specs=[pl.BlockSpec((1, 1), lambda i: (0, 0)), pl.BlockSpec((tm, d), lambda i: (i, 0)),
                   pl.BlockSpec((1, d), lambda i: (0, 0))],
        out_shape=[jax.ShapeDtypeStruct((1, 1), F32), jax.ShapeDtypeStruct((t, d), F32),
                   jax.ShapeDtypeStruct((1, d), F32)],
        scratch_shapes=[pltpu.VMEM((1, d), F32)],
        compiler_params=_params(("arbitrary",)),
    )(x, g, target)


def _mlp_bwd(dx2, x1, g2, u, wup, wdn, tm, name):
    t, d = dx2.shape
    nf = wup.shape[0]
    nt = t // tm

    def body(dx2_ref, x1_ref, g_ref, u_ref, wu_ref, wd_ref, dx1_ref, du_ref, r_ref, h_ref, dxb_ref, dg_ref, acc_ref):
        i, j = pl.program_id(0), pl.program_id(1)

        @pl.when(j == 0)
        def _():
            x1 = x1_ref[...]
            h_ref[...] = (x1 * _rstd(x1) * g_ref[...]).astype(BF16)
            dxb_ref[...] = dx2_ref[...].astype(BF16)
            acc_ref[...] = jnp.zeros_like(acc_ref)

        @pl.when((i == 0) & (j == 0))
        def _():
            dg_ref[...] = jnp.zeros_like(dg_ref)

        dr = _dot_nt(dxb_ref[...], wd_ref[...])
        ru = jnp.maximum(u_ref[...], 0.0)
        du = (dr * (2.0 * ru)).astype(BF16)
        du_ref[...] = du
        r_ref[...] = (ru * ru).astype(BF16)
        acc_ref[...] += _dot_nt(du, wu_ref[...])

        @pl.when(j == nf - 1)
        def _():
            x1 = x1_ref[...]
            dx, dg = _rms_bwd(x1, _rstd(x1), g_ref[...], acc_ref[...])
            dx1_ref[...] = dx2_ref[...] + dx
            dg_ref[...] += dg

    row = lambda w: pl.BlockSpec((tm, w), lambda i, j: (i, 0))
    col = pl.BlockSpec((tm, FF_SHARD), lambda i, j: (i, j))
    return pl.pallas_call(
        body, name=name, grid=(nt, nf),
        in_specs=[row(d), row(d), pl.BlockSpec((1, d), lambda i, j: (0, 0)), col,
                  pl.BlockSpec((None, d, FF_SHARD), lambda i, j: (j, 0, 0)),
                  pl.BlockSpec((None, FF_SHARD, d), lambda i, j: (j, 0, 0))],
        out_specs=[row(d), col, col, row(d), row(d), pl.BlockSpec((1, d), lambda i, j: (0, 0))],
        out_shape=[jax.ShapeDtypeStruct((t, d), F32), jax.ShapeDtypeStruct((t, nf * FF_SHARD), BF16),
                   jax.ShapeDtypeStruct((t, nf * FF_SHARD), BF16), jax.ShapeDtypeStruct((t, d), BF16),
                   jax.ShapeDtypeStruct((t, d), BF16), jax.ShapeDtypeStruct((1, d), F32)],
        scratch_shapes=[pltpu.VMEM((tm, d), F32)],
        compiler_params=_params(("arbitrary", "arbitrary")),
    )(dx2, x1, g2, u, wup, wdn)


def _matmul_tn(a, b, bm, bn, bt, name):
    t, m = a.shape
    n = b.shape[1]

    def body(a_ref, b_ref, o_ref):
        @pl.when(pl.program_id(2) == 0)
        def _():
            o_ref[...] = jnp.zeros_like(o_ref)

        o_ref[...] += _dot_tn(a_ref[...], b_ref[...])

    return pl.pallas_call(
        body, name=name, grid=(m // bm, n // bn, t // bt),
        in_specs=[pl.BlockSpec((bt, bm), lambda i, j, k: (k, i)), pl.BlockSpec((bt, bn), lambda i, j, k: (k, j))],
        out_specs=pl.BlockSpec((bm, bn), lambda i, j, k: (i, j)),
        out_shape=jax.ShapeDtypeStruct((m, n), F32),
        compiler_params=_params(("parallel", "parallel", "arbitrary")),
    )(a, b)


def _outproj_bwd(dx1, wo, tm, name):
    t, d = dx1.shape
    c2 = POOL_WIDTH + SG_WIDTH

    def body(dx_ref, wo_ref, dya_ref, dyb_ref, dyc_ref, dyct_ref, dxb_ref):
        dxb = dx_ref[...].astype(BF16)
        dxb_ref[...] = dxb
        dya_ref[...] = _dot_nt(dxb, wo_ref[0:POOL_WIDTH, :])
        dyb_ref[...] = _dot_nt(dxb, wo_ref[POOL_WIDTH:c2, :])
        dyc_ref[...] = _dot_nt(dxb, wo_ref[c2:, :]).astype(BF16)
        dyct_ref[...] = _dot_nt(wo_ref[c2:, :], dxb).astype(BF16)

    row = lambda w: pl.BlockSpec((tm, w), lambda i: (i, 0))
    return pl.pallas_call(
        body, name=name, grid=(t // tm,),
        in_specs=[row(d), pl.BlockSpec((d, d), lambda i: (0, 0))],
        out_specs=[row(POOL_WIDTH), row(SG_WIDTH), row(SB_WIDTH), pl.BlockSpec((SB_WIDTH, tm), lambda i: (0, i)), row(d)],
        out_shape=[jax.ShapeDtypeStruct((t, POOL_WIDTH), F32), jax.ShapeDtypeStruct((t, SG_WIDTH), F32),
                   jax.ShapeDtypeStruct((t, SB_WIDTH), BF16), jax.ShapeDtypeStruct((SB_WIDTH, t), BF16),
                   jax.ShapeDtypeStruct((t, d), BF16)],
        compiler_params=_params(("parallel",)),
    )(dx1, wo)


def _sba_bwd(qkv, ktb, dyc, dyct, cc, nvis, tq, name):
    t = qkv.shape[0]
    nb = t // tq
    idx = jnp.arange(tq)
    upper = (idx[None, :] > idx[:, None]).astype(BF16)
    lower = (idx[None, :] < idx[:, None]).astype(BF16)

    def body(q_ref, k_ref, v_ref, kt_ref, do_ref, dot_ref, c_ref, n_ref, up_ref, lo_ref, dqt_ref, dk_ref, dv_ref):
        i = pl.program_id(1)

        @pl.when(i == 0)
        def _():
            dk_ref[...] = jnp.zeros_like(dk_ref)
            dv_ref[...] = jnp.zeros_like(dv_ref)

        q = q_ref[...]
        do = do_ref[...]
        dot = dot_ref[...]
        up = up_ref[...]
        lo = lo_ref[...]
        lane_head = lax.broadcasted_iota(jnp.int32, (1, 128), 1) // SB_HD
        sub_head = lax.broadcasted_iota(jnp.int32, (128, 1), 0) // SB_HD
        causal = (lax.broadcasted_iota(jnp.int32, (tq, tq), 0) < lax.broadcasted_iota(jnp.int32, (tq, tq), 1))
        hms = [lane_head == h for h in range(2)]
        qh = [jnp.where(hm, q, jnp.zeros_like(q)) for hm in hms]
        qs = [x * 0.125 for x in qh]
        doh = [jnp.where(hm, do, jnp.zeros_like(do)) for hm in hms]

        def block(j, carry, diag):
            cgs, dqt = list(carry[:2]), carry[2]
            rows = pl.ds(pl.multiple_of(j * tq, tq), tq)
            kj = k_ref[rows, :]
            vj = v_ref[rows, :]
            kt = kt_ref[j]
            hs = range(2)
            z = [_dot_nt(kj, qh[h]) for h in hs]
            da = [_dot(jnp.where(hms[h], vj, jnp.zeros_like(vj)), dot) for h in hs]
            ls = [_log2_sigmoids(z[h] * SB_SCALE) for h in hs]
            lb = [ls[h][0] for h in hs]
            l1 = [jnp.where(causal, ls[h][1], 0.0) if diag else ls[h][1] for h in hs]
            after = [_tri_dot(up, l1[h]) for h in hs]
            a = [jnp.exp2(lb[h] + after[h] + c_ref[h, j]) for h in hs]
            if diag:
                a = [jnp.where(causal, a[h], 0.0) for h in hs]
            g = [a[h] * da[h] for h in hs]
            gsum = [_tri_dot(lo, g[h]) + cgs[h] for h in hs]
            dz = [g[h] - jnp.exp2(lb[h]) * (g[h] + gsum[h]) for h in hs]
            if diag:
                dz = [jnp.where(causal, dz[h], 0.0) for h in hs]
            dzb = [dz[h].astype(BF16) for h in hs]
            ab = [a[h].astype(BF16) for h in hs]
            for h in hs:
                dqt = dqt + _dot(jnp.where(sub_head == h, kt, jnp.zeros_like(kt)), dzb[h])
            dk_ref[rows, :] += _dot(dzb[0], qs[0]) + _dot(dzb[1], qs[1])
            dv_ref[rows, :] += _dot(ab[0], doh[0]) + _dot(ab[1], doh[1])
            return (gsum[0][tq - 1:tq, :] + g[0][tq - 1:tq, :], gsum[1][tq - 1:tq, :] + g[1][tq - 1:tq, :], dqt)

        n = jnp.clip(n_ref[pl.program_id(0), i].astype(jnp.int32), 1, i + 1)
        zero = jnp.zeros((1, tq), F32)
        carry = lax.fori_loop(i + 1 - n, i, lambda s, cr: block(s, cr, False), (zero, zero, jnp.zeros((128, tq), F32)))
        dqt_ref[...] = block(i, carry, True)[2] * 0.125

    return pl.pallas_call(
        body, name=name, grid=(SB_PAIRS, nb),
        in_specs=[pl.BlockSpec((tq, 128), lambda p, i: (i, p)),
                  pl.BlockSpec((t, 128), lambda p, i: (0, SB_PAIRS + p)),
                  pl.BlockSpec((t, 128), lambda p, i: (0, 2 * SB_PAIRS + p)),
                  pl.BlockSpec((None, nb, 128, tq), lambda p, i: (p, 0, 0, 0)),
                  pl.BlockSpec((tq, 128), lambda p, i: (i, p)),
                  pl.BlockSpec((128, tq), lambda p, i: (p, i)),
                  pl.BlockSpec((2, nb, 1, tq), lambda p, i: (p, 0, 0, i)),
                  pl.BlockSpec(memory_space=pltpu.SMEM),
                  pl.BlockSpec((tq, tq), lambda p, i: (0, 0)),
                  pl.BlockSpec((tq, tq), lambda p, i: (0, 0))],
        out_specs=[pl.BlockSpec((128, tq), lambda p, i: (p, i)),
                   pl.BlockSpec((t, 128), lambda p, i: (0, p)),
                   pl.BlockSpec((t, 128), lambda p, i: (0, p))],
        out_shape=[jax.ShapeDtypeStruct((SB_WIDTH, t), F32), jax.ShapeDtypeStruct((t, SB_WIDTH), F32),
                   jax.ShapeDtypeStruct((t, SB_WIDTH), F32)],
        compiler_params=_params(("arbitrary", "arbitrary")),
    )(qkv, qkv, qkv, ktb, dyc, dyct, cc, nvis, upper, lower)


def _sg_bwd(ab, dyb, gn, wm, wmt, bfull, tm, name):
    t = ab.shape[0]
    nt = t // tm
    sel = (jnp.arange(SG_WIDTH)[:, None] // SG_HD == jnp.arange(CHUNK)[None, :]).astype(F32)

    def body(u_ref, v_ref, dy_ref, gn_ref, wm_ref, wmt_ref, b_ref, sel_ref,
             dup_ref, dvp_ref, dgn_ref, dw_ref, db_ref, dbacc_ref):
        i = pl.program_id(0)

        @pl.when(i == 0)
        def _():
            dgn_ref[...] = jnp.zeros_like(dgn_ref)
            dw_ref[...] = jnp.zeros_like(dw_ref)
            dbacc_ref[...] = jnp.zeros_like(dbacc_ref)

        tril = (lax.broadcasted_iota(jnp.int32, (CHUNK, CHUNK), 0) >= lax.broadcasted_iota(jnp.int32, (CHUNK, CHUNK), 1))
        gn_ = gn_ref[...]
        for c in range(tm // CHUNK):
            rows = slice(c * CHUNK, (c + 1) * CHUNK)
            up, vp, dy = u_ref[rows, :], v_ref[rows, :], dy_ref[rows, :]
            u, v = _gelu(up), _gelu(vp)
            r = _rstd(v)
            vn = (v * r * gn_).astype(BF16)
            sv = b_ref[...]
            for h in range(SG_HEADS):
                sv = sv + jnp.where(_head_lanes(h), _dot(wm_ref[h], vn), 0.0)
            dup_ref[rows, :] = dy * sv * _gelu_grad(up)
            dsv = dy * u
            dbacc_ref[...] += dsv
            dvn = jnp.zeros((CHUNK, SG_WIDTH), F32)
            for h in range(SG_HEADS):
                dsv_h = jnp.where(_head_lanes(h), dsv, 0.0).astype(BF16)
                dvn = dvn + _dot(wmt_ref[h], dsv_h)
                dw_ref[h] += jnp.where(tril, _dot_nt(dsv_h, vn), 0.0)
            dv, dgn = _rms_bwd(v, r, gn_, dvn)
            dgn_ref[...] += dgn
            dvp_ref[rows, :] = dv * _gelu_grad(vp)

        @pl.when(i == nt - 1)
        def _():
            db_ref[...] = jnp.dot(dbacc_ref[...], sel_ref[...], preferred_element_type=F32,
                                  precision=lax.Precision.HIGHEST)

    const = lambda shape: pl.BlockSpec(shape, lambda i: tuple(0 for _ in shape))
    return pl.pallas_call(
        body, name=name, grid=(nt,),
        in_specs=[pl.BlockSpec((tm, SG_WIDTH), lambda i: (i, 1)), pl.BlockSpec((tm, SG_WIDTH), lambda i: (i, 2)),
                  pl.BlockSpec((tm, SG_WIDTH), lambda i: (i, 0)), const((1, SG_WIDTH)),
                  const((SG_HEADS, CHUNK, CHUNK)), const((SG_HEADS, CHUNK, CHUNK)), const((CHUNK, SG_WIDTH)),
                  const((SG_WIDTH, CHUNK))],
        out_specs=[pl.BlockSpec((tm, SG_WIDTH), lambda i: (i, 0)), pl.BlockSpec((tm, SG_WIDTH), lambda i: (i, 0)),
                   const((1, SG_WIDTH)), const((SG_HEADS, CHUNK, CHUNK)), const((CHUNK, CHUNK))],
        out_shape=[jax.ShapeDtypeStruct((t, SG_WIDTH), F32), jax.ShapeDtypeStruct((t, SG_WIDTH), F32),
                   jax.ShapeDtypeStruct((1, SG_WIDTH), F32), jax.ShapeDtypeStruct((SG_HEADS, CHUNK, CHUNK), F32),
                   jax.ShapeDtypeStruct((CHUNK, CHUNK), F32)],
        scratch_shapes=[pltpu.VMEM((CHUNK, SG_WIDTH), F32)],
        compiler_params=_params(("arbitrary",)),
    )(ab, ab, dyb, gn, wm, wmt, bfull, sel)


def _pool_bwd(ab, dya, wbd, scale, tm, name):
    t = ab.shape[0]
    nt = t // tm
    hb = tm // POOL_HALO
    nh = t // POOL_HALO

    def body(cur_ref, prev_ref, dy_ref, dyn_ref, w_ref, s_ref, da_ref, dw_ref, ds_ref):
        i = pl.program_id(0)

        @pl.when(i == 0)
        def _():
            dw_ref[...] = jnp.zeros_like(dw_ref)
            ds_ref[...] = jnp.zeros_like(ds_ref)

        prev = jnp.where(i == 0, 0.0, prev_ref[...])
        d = _pool_diff(cur_ref[...], prev, i * tm).astype(BF16)
        dy = dy_ref[...]
        ds_ref[...] += jnp.sum(dy * _dot(d, w_ref[...]), axis=0, keepdims=True)
        dyn = jnp.where(i == nt - 1, 0.0, dyn_ref[...])
        dys = (jnp.concatenate([dy, dyn], axis=0) * s_ref[...]).astype(BF16)
        dw_ref[...] += _dot_tn(d, dys[:tm])
        dd = _dot_nt(dys, w_ref[...])
        fwd = _pool_window_sums(dd / _pool_count(i * tm, tm + POOL_HALO), True)
        da_ref[...] = fwd[:tm] - dd[:tm]

    return pl.pallas_call(
        body, name=name, grid=(nt,),
        in_specs=[pl.BlockSpec((tm, POOL_WIDTH), lambda i: (i, 0)),
                  pl.BlockSpec((POOL_HALO, POOL_WIDTH), lambda i: (jnp.maximum(i * hb - 1, 0), 0)),
                  pl.BlockSpec((tm, POOL_WIDTH), lambda i: (i, 0)),
                  pl.BlockSpec((POOL_HALO, POOL_WIDTH), lambda i: (jnp.minimum((i + 1) * hb, nh - 1), 0)),
                  pl.BlockSpec((POOL_WIDTH, POOL_WIDTH), lambda i: (0, 0)),
                  pl.BlockSpec((1, POOL_WIDTH), lambda i: (0, 0))],
        out_specs=[pl.BlockSpec((tm, POOL_WIDTH), lambda i: (i, 0)),
                   pl.BlockSpec((POOL_WIDTH, POOL_WIDTH), lambda i: (0, 0)),
                   pl.BlockSpec((1, POOL_WIDTH), lambda i: (0, 0))],
        out_shape=[jax.ShapeDtypeStruct((t, POOL_WIDTH), F32), jax.ShapeDtypeStruct((POOL_WIDTH, POOL_WIDTH), F32),
                   jax.ShapeDtypeStruct((1, POOL_WIDTH), F32)],
        compiler_params=_params(("arbitrary",)),
    )(ab, ab, dya, dya, wbd, scale)


def _inproj_bwd(dx1, x, g, da, dup, dvp, dq, dk, dv, w, tm, name):
    t, d = x.shape
    n = w.shape[1]
    nt = t // tm

    def body(dx1_ref, x_ref, g_ref, da_ref, du_ref, dv_ref, dq_ref, dk_ref, dvv_ref, w_ref,
             dx_ref, h_ref, dp_ref, dg_ref):
        @pl.when(pl.program_id(0) == 0)
        def _():
            dg_ref[...] = jnp.zeros_like(dg_ref)

        dp = jnp.concatenate([da_ref[...], du_ref[...], dv_ref[...], dq_ref[...], dk_ref[...], dvv_ref[...]],
                             axis=1).astype(BF16)
        dp_ref[...] = dp
        xx = x_ref[...]
        r = _rstd(xx)
        h_ref[...] = (xx * r * g_ref[...]).astype(BF16)
        dx, dg = _rms_bwd(xx, r, g_ref[...], _dot_nt(dp, w_ref[...]))
        dx_ref[...] = dx1_ref[...] + dx
        dg_ref[...] += dg

    row = lambda w_: pl.BlockSpec((tm, w_), lambda i: (i, 0))
    return pl.pallas_call(
        body, name=name, grid=(nt,),
        in_specs=[row(d), row(d), pl.BlockSpec((1, d), lambda i: (0, 0)), row(POOL_WIDTH), row(SG_WIDTH),
                  row(SG_WIDTH), row(SB_WIDTH), row(SB_WIDTH), row(SB_WIDTH), pl.BlockSpec((d, n), lambda i: (0, 0))],
        out_specs=[row(d), row(d), row(n), pl.BlockSpec((1, d), lambda i: (0, 0))],
        out_shape=[jax.ShapeDtypeStruct((t, d), F32), jax.ShapeDtypeStruct((t, d), BF16),
                   jax.ShapeDtypeStruct((t, n), BF16), jax.ShapeDtypeStruct((1, d), F32)],
        compiler_params=_params(("arbitrary",)),
    )(dx1, x, g, da, dup, dvp, dq, dk, dv, w)


def _pack_big(w_in, w_out, w_up, w_down):
    rows = []
    for l in range(DEPTH):
        rows += [w_in[l].reshape(-1, D_MODEL), w_out[l], w_up[l].reshape(-1, D_MODEL), w_down[l]]
    return jnp.concatenate(rows, axis=0)


_BIG_ROWS = (IN_SHARD, OUT_SHARD, FF_SHARD, FF_SHARD)


def _unpack_big(buf):
    outs = [[], [], [], []]
    off = 0
    for _ in range(DEPTH):
        for k, r in enumerate(_BIG_ROWS):
            outs[k].append(buf[off:off + r])
            off += r
    w_in, w_out, w_up, w_down = (jnp.stack(o) for o in outs)
    return (w_in.reshape(DEPTH, D_MODEL, IN_SHARD), w_out, w_up.reshape(DEPTH, D_MODEL, FF_SHARD), w_down)


def _gathered_weights(buf, l):
    off = l * sum(_BIG_ROWS)
    w_in = buf[:, off:off + IN_SHARD].reshape(N_DEV, D_MODEL, IN_SHARD).transpose(1, 0, 2).reshape(D_MODEL, IN_COLS)
    off += IN_SHARD
    w_out = buf[:, off:off + OUT_SHARD].reshape(D_MODEL, D_MODEL)
    off += OUT_SHARD
    w_up = buf[:, off:off + FF_SHARD].reshape(N_DEV, D_MODEL, FF_SHARD)
    off += FF_SHARD
    w_down = buf[:, off:off + FF_SHARD]
    return w_in, w_out, w_up, w_down


def _scatter_layout(dw_in, dw_out, dw_up, dw_down):
    d_in = dw_in.reshape(D_MODEL, N_DEV, IN_SHARD).transpose(1, 0, 2).reshape(N_DEV, IN_SHARD, D_MODEL)
    d_out = dw_out.reshape(N_DEV, OUT_SHARD, D_MODEL)
    d_up = dw_up.reshape(D_MODEL, N_DEV, FF_SHARD).transpose(1, 0, 2).reshape(N_DEV, FF_SHARD, D_MODEL)
    d_down = dw_down.reshape(N_DEV, FF_SHARD, D_MODEL)
    return jnp.concatenate([d_in, d_out, d_up, d_down], axis=1)


_SMALL_SHAPES = ((DEPTH, D_MODEL), (DEPTH, 4, POOL_GW, POOL_GW), (DEPTH, POOL_WIDTH), (DEPTH, SG_WIDTH),
                 (DEPTH, SG_HEADS, CHUNK, CHUNK), (DEPTH, SG_HEADS, CHUNK), (DEPTH, D_MODEL), (D_MODEL,))


def _pack_small(arrs):
    return jnp.concatenate([a.reshape(-1) for a in arrs]).reshape(-1, 128)


def _unpack_small(buf):
    flat = buf.reshape(-1)
    outs, off = [], 0
    for shp in _SMALL_SHAPES:
        size = 1
        for s in shp:
            size *= s
        outs.append(flat[off:off + size].reshape(shp))
        off += size
    return outs


def _local_step(x0, target, gathered, norm1, pool_w, pool_scale, sg_norm, sg_w, sg_b, norm2, final_norm):
    t = x0.shape[0]
    tm = min(512, t)
    tq = min(256, t // 4)
    nb = t // tq
    tril = jnp.tril(jnp.ones((CHUNK, CHUNK), F32))
    saved = []
    xl = x0
    for l in range(DEPTH):
        wi, wo, wu, wd = _gathered_weights(gathered, l)
        wbd = jnp.zeros((4, POOL_GW, 4, POOL_GW), F32)
        for gi in range(4):
            wbd = wbd.at[gi, :, gi, :].set(pool_w[l, gi])
        wbd = wbd.reshape(POOL_WIDTH, POOL_WIDTH).astype(BF16)
        wm = (sg_w[l] * tril).astype(BF16)
        wmt = wm.transpose(0, 2, 1)
        bfull = jnp.repeat(sg_b[l].T, SG_HD, axis=1)
        g1, g2 = norm1[l][None, :], norm2[l][None, :]
        psc, gn = pool_scale[l][None, :], sg_norm[l][None, :]

        ab, qkv = _inproj_fwd(xl, g1, wi, tm, f"inproj_fwd{l}")
        ya = _pool_fwd(ab, wbd, psc, tm, f"pool_fwd{l}")
        yb = _sg_fwd(ab, gn, wm, bfull, tm, f"sg_fwd{l}")
        vtb = qkv[:, 2 * SB_WIDTH:].reshape(nb, tq, SB_PAIRS, 128).transpose(2, 0, 3, 1)
        oct_, cc, nvis = _sba_fwd(qkv, vtb, tq, f"sba_fwd{l}")
        yc = oct_.T
        x1, u, x2 = _outproj_mlp_fwd(xl, ya, yb, yc, wo, g2, wu, wd, tm, f"mlp_fwd{l}")
        saved.append(dict(x=xl, ab=ab, qkv=qkv, cc=cc, nvis=nvis, ya=ya, yb=yb, yc=yc, x1=x1, u=u, wi=wi, wo=wo, wu=wu, wd=wd,
                          wbd=wbd, wm=wm, wmt=wmt, bfull=bfull, g1=g1, g2=g2, psc=psc, gn=gn))
        xl = x2

    loss_local, dx, dfinal = _loss_grad(xl, final_norm[None, :], target, tm, "loss_grad")

    big_parts = [None] * DEPTH
    small = {}
    for l in reversed(range(DEPTH)):
        s = saved[l]
        dx1, du, r, h2, dx2b, dn2 = _mlp_bwd(dx, s["x1"], s["g2"], s["u"], s["wu"], s["wd"], tm, f"mlp_bwd{l}")
        dw_up = _matmul_tn(h2, du, D_MODEL, 1024, tm, f"dw_up{l}")
        dw_down = _matmul_tn(r, dx2b, 1024, D_MODEL, tm, f"dw_down{l}")
        dya, dyb, dyc, dyct, dx1b = _outproj_bwd(dx1, s["wo"], tm, f"outproj_bwd{l}")
        mix = jnp.concatenate([s["ya"], s["yb"], s["yc"]], axis=1)
        dw_out = _matmul_tn(mix, dx1b, D_MODEL, D_MODEL, tm, f"dw_out{l}")
        ktb = s["qkv"][:, SB_WIDTH:2 * SB_WIDTH].reshape(nb, tq, SB_PAIRS, 128).transpose(2, 0, 3, 1)
        dqt, dk, dv = _sba_bwd(s["qkv"], ktb, dyc, dyct, s["cc"], s["nvis"], tq, f"sba_bwd{l}")
        dup, dvp, dgn, dwm, dbm = _sg_bwd(s["ab"], dyb, s["gn"], s["wm"], s["wmt"], s["bfull"], tm, f"sg_bwd{l}")
        da, dwbd, dpsc = _pool_bwd(s["ab"], dya, s["wbd"], s["psc"], tm, f"pool_bwd{l}")
        dx, h1, dproj, dn1 = _inproj_bwd(dx1, s["x"], s["g1"], da, dup, dvp, dqt.T, dk, dv, s["wi"], tm,
                                         f"inproj_bwd{l}")
        dw_in = _matmul_tn(h1, dproj, D_MODEL, IN_COLS // 3, tm, f"dw_in{l}")
        big_parts[l] = _scatter_layout(dw_in, dw_out, dw_up, dw_down)
        dpw = jnp.stack([dwbd[gi * POOL_GW:(gi + 1) * POOL_GW, gi * POOL_GW:(gi + 1) * POOL_GW] for gi in range(4)])
        small[l] = (dn1[0], dpw, dpsc[0], dgn[0], dwm, dbm[:, :SG_HEADS].T, dn2[0])
    small_local = [jnp.stack([small[l][k] for l in range(DEPTH)]) for k in range(7)] + [dfinal[0]]
    return loss_local, dx, jnp.concatenate(big_parts, axis=1), small_local


def kernel(x, norm1, w_in, pool_w, pool_scale, sg_norm, sg_w, sg_b, w_out, norm2, w_up, w_down, final_norm, loss_target, m_norm1, m_w_in, m_pool_w, m_pool_scale, m_sg_norm, m_sg_w, m_sg_b, m_w_out, m_norm2, m_w_up, m_w_down, m_final_norm, v_norm1, v_w_in, v_pool_w, v_pool_scale, v_sg_norm, v_sg_w, v_sg_b, v_w_out, v_norm2, v_w_up, v_w_down, v_final_norm):
    t = x.shape[1]
    gathered = _all_gather(_pack_big(w_in, w_out, w_up, w_down).astype(BF16), "gather_weights")
    loss_local, dx, big_parts, small_local = _local_step(
        x.reshape(t, D_MODEL), loss_target.reshape(t, D_MODEL), gathered,
        norm1, pool_w, pool_scale, sg_norm, sg_w, sg_b, norm2, final_norm)
    loss = lax.psum(loss_local[0, 0], MESH_AXES)
    grad_x = dx.reshape(x.shape)

    received = _all_to_all(big_parts, "scatter_grads")
    gb, db, mb, vb = _reduce_adamw(received, _pack_big(w_in, w_out, w_up, w_down), _pack_big(m_w_in, m_w_out, m_w_up, m_w_down),
                                   _pack_big(v_w_in, v_w_out, v_w_up, v_w_down), "adamw_sharded")
    big = [_unpack_big(b) for b in (gb, db, mb, vb)]

    small_all = _all_gather(_pack_small(small_local), "gather_small_grads")
    gs, ds, ms, vs = _reduce_adamw(
        small_all,
        _pack_small([norm1, pool_w, pool_scale, sg_norm, sg_w, sg_b, norm2, final_norm]),
        _pack_small([m_norm1, m_pool_w, m_pool_scale, m_sg_norm, m_sg_w, m_sg_b, m_norm2, m_final_norm]),
        _pack_small([v_norm1, v_pool_w, v_pool_scale, v_sg_norm, v_sg_w, v_sg_b, v_norm2, v_final_norm]),
        "adamw_replicated")
    sm = [_unpack_small(b) for b in (gs, ds, ms, vs)]

    def leaves(bg, smv):
        n1, pw, ps, sn, sw, sb, n2, fn = smv
        wi_, wo_, wu_, wd_ = bg
        return [n1, wi_, pw, ps, sn, sw, sb, wo_, n2, wu_, wd_, fn]

    out = [loss, grad_x]
    for k in range(4):
        out += leaves(big[k], sm[k])
    return tuple(out)
```

```python
import functools

import jax
import jax.numpy as jnp
from jax import lax
from jax.experimental import pallas as pl
from jax.experimental.pallas import tpu as pltpu

F32 = jnp.float32
BF16 = jnp.bfloat16

D_MODEL = 1024
DEPTH = 2
POOL_WIDTH = 256
SG_WIDTH = 256
SB_WIDTH = 512
POOL_WINDOWS = (2, 4, 8, 16)
POOL_GW = 64
POOL_HALO = 16
CHUNK = 128
SG_HEADS = 4
SG_HD = 64
SB_HD = 64
SB_PAIRS = SB_WIDTH // 128
AB_COLS = POOL_WIDTH + 2 * SG_WIDTH
IN_COLS = AB_COLS + 3 * SB_WIDTH
D_FF = 4096
EPS = 1e-6
N_DEV = 8
FF_SHARD = D_FF // N_DEV
IN_SHARD = IN_COLS // N_DEV
OUT_SHARD = D_MODEL // N_DEV
ADAM_LR = 0.001
ADAM_B1 = 0.9
ADAM_B2 = 0.999
ADAM_EPS = 1e-08
ADAM_WD = 0.01
ADAM_STEP = 10
VMEM_LIMIT = 52 * 1024 * 1024
MESH_AXES = ("x", "y", "c")


def _dot(a, b):
    return jnp.dot(a, b, preferred_element_type=F32)


def _dot_nt(a, b):
    return lax.dot_general(a, b, (((1,), (1,)), ((), ())), preferred_element_type=F32)


def _dot_tn(a, b):
    return lax.dot_general(a, b, (((0,), (0,)), ((), ())), preferred_element_type=F32)


def _rstd(x):
    return lax.rsqrt(jnp.mean(x * x, axis=-1, keepdims=True) + EPS)


def _rms_bwd(x, r, g, dh):
    gq = dh * g
    dx = r * gq - x * (r * r * r) * jnp.mean(gq * x, axis=-1, keepdims=True)
    dg = jnp.sum(dh * x * r, axis=0, keepdims=True)
    return dx, dg


def _params(sem=None):
    kw = dict(vmem_limit_bytes=VMEM_LIMIT)
    if sem is not None:
        kw["dimension_semantics"] = sem
    return pltpu.CompilerParams(**kw)


def _row_tile(rows, cap):
    best = 8
    for t in range(8, min(rows, cap) + 1, 8):
        if rows % t == 0:
            best = t
    return best


def _peer(k):
    x, y, c = lax.axis_index("x"), lax.axis_index("y"), lax.axis_index("c")
    return (1 - x if k & 4 else x, 1 - y if k & 2 else y, 1 - c if k & 1 else c)


def _my_index():
    return 4 * lax.axis_index("x") + 2 * lax.axis_index("y") + lax.axis_index("c")


def _exchange(arrs, scatter, name):
    na = len(arrs)

    def body(*refs):
        ins, outs = refs[:na], refs[na:2 * na]
        send_sems, recv_sems, local_sems = refs[2 * na:]
        me = _my_index()
        local, copies = [], []
        for a in range(na):
            local.append(pltpu.make_async_copy(ins[a].at[me] if scatter else ins[a], outs[a].at[me], local_sems.at[a]))
            for k in range(1, N_DEV):
                px, py, pc = _peer(k)
                copies.append(pltpu.make_async_remote_copy(
                    src_ref=ins[a].at[4 * px + 2 * py + pc] if scatter else ins[a], dst_ref=outs[a].at[me],
                    send_sem=send_sems.at[a, k - 1], recv_sem=recv_sems.at[a, k - 1],
                    device_id=(px, py, pc), device_id_type=pl.DeviceIdType.MESH))
        for cp in local + copies:
            cp.start()
        for cp in copies:
            cp.wait_recv()
        for cp in copies:
            cp.wait_send()
        for cp in local:
            cp.wait()

    any_spec = pl.BlockSpec(memory_space=pl.ANY)
    return pl.pallas_call(
        body, name=name,
        out_shape=[jax.ShapeDtypeStruct((N_DEV,) + a.shape[-2:], a.dtype) for a in arrs],
        in_specs=[any_spec] * na, out_specs=[any_spec] * na,
        scratch_shapes=[pltpu.SemaphoreType.DMA((na, N_DEV - 1)), pltpu.SemaphoreType.DMA((na, N_DEV - 1)),
                        pltpu.SemaphoreType.DMA((na,))],
    )(*arrs)


def _reduce_adamw(parts, w, m, v, l, prev, name):
    _, rows, n = parts.shape
    tr = _row_tile(rows, max(8, (1 << 18) // n))
    c1 = 1.0 - ADAM_B1 ** ADAM_STEP
    c2 = 1.0 - ADAM_B2 ** ADAM_STEP

    def body(p_ref, w_ref, m_ref, v_ref, *rest):
        g_ref, d_ref, nm_ref, nv_ref = rest[-4:]
        g = p_ref[0].astype(F32)
        for s in range(1, N_DEV):
            g = g + p_ref[s].astype(F32)
        nm = ADAM_B1 * m_ref[...] + (1.0 - ADAM_B1) * g
        nv = ADAM_B2 * v_ref[...] + (1.0 - ADAM_B2) * (g * g)
        m_hat = nm / c1
        v_hat = nv / c2
        g_ref[...] = g
        d_ref[...] = -ADAM_LR * (m_hat / (jnp.sqrt(v_hat) + ADAM_EPS) + ADAM_WD * w_ref[...])
        nm_ref[...] = nm
        nv_ref[...] = nv

    blk = pl.BlockSpec((None, tr, n), lambda i: (l, i, 0))
    out = jax.ShapeDtypeStruct(w.shape, F32)
    prev = list(prev) if prev is not None else []
    return pl.pallas_call(
        body, name=name, grid=(rows // tr,),
        in_specs=[pl.BlockSpec((N_DEV, tr, n), lambda i: (0, i, 0)), blk, blk, blk] + [pl.BlockSpec(memory_space=pl.ANY)] * len(prev),
        out_specs=[blk, blk, blk, blk], out_shape=[out, out, out, out],
        input_output_aliases={4 + k: k for k in range(len(prev))},
        compiler_params=_params(("parallel",)),
    )(parts, w, m, v, *prev)


def _inproj_fwd(x, g, w, tm, name):
    t, d = x.shape
    n = w.shape[1]

    def body(x_ref, g_ref, w_ref, ab_ref, qkv_ref):
        xx = x_ref[...]
        h = (xx * _rstd(xx) * g_ref[...]).astype(BF16)
        ab_ref[...] = _dot(h, w_ref[:, :AB_COLS])
        qkv_ref[...] = _dot(h, w_ref[:, AB_COLS:]).astype(BF16)

    return pl.pallas_call(
        body, name=name, grid=(t // tm,),
        in_specs=[pl.BlockSpec((tm, d), lambda i: (i, 0)), pl.BlockSpec((1, d), lambda i: (0, 0)),
                  pl.BlockSpec((d, n), lambda i: (0, 0))],
        out_specs=[pl.BlockSpec((tm, AB_COLS), lambda i: (i, 0)), pl.BlockSpec((tm, n - AB_COLS), lambda i: (i, 0))],
        out_shape=[jax.ShapeDtypeStruct((t, AB_COLS), F32), jax.ShapeDtypeStruct((t, n - AB_COLS), BF16)],
        compiler_params=_params(("parallel",)),
    )(x, g, w)


def _pool_window_sums(xx, forward):
    n = xx.shape[0]
    sh = (lambda k: n - k) if forward else (lambda k: k)
    s2 = xx + pltpu.roll(xx, sh(1), 0)
    s4 = s2 + pltpu.roll(s2, sh(2), 0)
    s8 = s4 + pltpu.roll(s4, sh(4), 0)
    s16 = s8 + pltpu.roll(s8, sh(8), 0)
    grp = lax.broadcasted_iota(jnp.int32, (1, POOL_WIDTH), 1) // POOL_GW
    return jnp.where(grp == 0, s2, jnp.where(grp == 1, s4, jnp.where(grp == 2, s8, s16)))


def _pool_count(t0, rows):
    grp = lax.broadcasted_iota(jnp.int32, (1, POOL_WIDTH), 1) // POOL_GW
    win = jnp.where(grp == 0, 2, jnp.where(grp == 1, 4, jnp.where(grp == 2, 8, 16)))
    tt = t0 + lax.broadcasted_iota(jnp.int32, (rows, 1), 0)
    return jnp.minimum(tt + 1, win).astype(F32)


def _pool_diff(cur, prev, t0):
    tm = cur.shape[0]
    sums = _pool_window_sums(jnp.concatenate([prev, cur], axis=0), False)[POOL_HALO:]
    return sums / _pool_count(t0, tm) - cur


def _pool_fwd(ab, wbd, scale, tm, name):
    t = ab.shape[0]
    hb = tm // POOL_HALO

    def body(cur_ref, prev_ref, w_ref, s_ref, y_ref):
        i = pl.program_id(0)
        prev = jnp.where(i == 0, 0.0, prev_ref[...])
        d = _pool_diff(cur_ref[...], prev, i * tm).astype(BF16)
        y_ref[...] = (_dot(d, w_ref[...]) * s_ref[...]).astype(BF16)

    return pl.pallas_call(
        body, name=name, grid=(t // tm,),
        in_specs=[pl.BlockSpec((tm, POOL_WIDTH), lambda i: (i, 0)),
                  pl.BlockSpec((POOL_HALO, POOL_WIDTH), lambda i: (jnp.maximum(i * hb - 1, 0), 0)),
                  pl.BlockSpec((POOL_WIDTH, POOL_WIDTH), lambda i: (0, 0)),
                  pl.BlockSpec((1, POOL_WIDTH), lambda i: (0, 0))],
        out_specs=pl.BlockSpec((tm, POOL_WIDTH), lambda i: (i, 0)),
        out_shape=jax.ShapeDtypeStruct((t, POOL_WIDTH), BF16),
        compiler_params=_params(("parallel",)),
    )(ab, ab, wbd, scale)


_GELU_K = 0.7978845608028654
_GELU_A = 0.044715


def _gelu(x):
    return 0.5 * x * (1.0 + jnp.tanh(_GELU_K * (x + _GELU_A * (x * x * x))))


def _gelu_grad(x):
    th = jnp.tanh(_GELU_K * (x + _GELU_A * (x * x * x)))
    return 0.5 * (1.0 + th) + 0.5 * x * (1.0 - th * th) * (_GELU_K * (1.0 + 3.0 * _GELU_A * (x * x)))


def _head_lanes(h):
    return lax.broadcasted_iota(jnp.int32, (1, SG_WIDTH), 1) // SG_HD == h


def _sg_fwd(ab, gn, wm, bfull, tm, name):
    t = ab.shape[0]

    def body(u_ref, v_ref, gn_ref, wm_ref, b_ref, y_ref):
        v = _gelu(v_ref[...])
        vn = (v * _rstd(v) * gn_ref[...]).astype(BF16)
        for c in range(tm // CHUNK):
            rows = slice(c * CHUNK, (c + 1) * CHUNK)
            vc = vn[rows]
            sv = b_ref[...]
            for h in range(SG_HEADS):
                sv = sv + jnp.where(_head_lanes(h), _dot(wm_ref[h], vc), 0.0)
            y_ref[rows, :] = (_gelu(u_ref[rows, :]) * sv).astype(BF16)

    return pl.pallas_call(
        body, name=name, grid=(t // tm,),
        in_specs=[pl.BlockSpec((tm, SG_WIDTH), lambda i: (i, 1)), pl.BlockSpec((tm, SG_WIDTH), lambda i: (i, 2)),
                  pl.BlockSpec((1, SG_WIDTH), lambda i: (0, 0)),
                  pl.BlockSpec((SG_HEADS, CHUNK, CHUNK), lambda i: (0, 0, 0)),
                  pl.BlockSpec((CHUNK, SG_WIDTH), lambda i: (0, 0))],
        out_specs=pl.BlockSpec((tm, SG_WIDTH), lambda i: (i, 0)),
        out_shape=jax.ShapeDtypeStruct((t, SG_WIDTH), BF16),
        compiler_params=_params(("parallel",)),
    )(ab, ab, gn, wm, bfull)


LOG2E = 1.4426950408889634
SB_SCALE = 0.125 * LOG2E
SB_DEAD_LOG2 = 152.0


def _log2_sigmoids(y):
    neg_abs = lax.bitcast_convert_type(lax.bitcast_convert_type(y, jnp.uint32) | jnp.uint32(0x80000000), F32)
    lb = jnp.minimum(y, 0.0) - jnp.log(1.0 + jnp.exp2(neg_abs)) * LOG2E
    return lb, lb - y


def _split(x):
    hi = x.astype(BF16)
    return hi, (x - hi.astype(F32)).astype(BF16)


def _tri_dot(tri, x):
    hi, lo = _split(x)
    return _dot(tri, hi) + _dot(tri, lo)


def _sba_fwd(qkv, vtb, tq, name):
    t = qkv.shape[0]
    nb = t // tq
    upper = (jnp.arange(tq)[None, :] > jnp.arange(tq)[:, None]).astype(BF16)

    def body(q_ref, k_ref, vt_ref, up_ref, ot_ref, c_ref, n_ref):
        i = pl.program_id(1)
        q = q_ref[...]
        up = up_ref[...]
        lane_head = lax.broadcasted_iota(jnp.int32, (1, 128), 1) // SB_HD
        sub_head = lax.broadcasted_iota(jnp.int32, (128, 1), 0) // SB_HD
        causal = (lax.broadcasted_iota(jnp.int32, (tq, tq), 0) < lax.broadcasted_iota(jnp.int32, (tq, tq), 1))
        qh = [jnp.where(lane_head == h, q, jnp.zeros_like(q)) for h in range(2)]

        def block(j, carry, diag):
            cs, acc = list(carry[:2]), carry[2]
            kj = k_ref[pl.ds(pl.multiple_of(j * tq, tq), tq), :]
            vt = vt_ref[j]
            hs = range(2)
            z = [_dot_nt(kj, qh[h]) for h in hs]
            ls = [_log2_sigmoids(z[h] * SB_SCALE) for h in hs]
            lb = [ls[h][0] for h in hs]
            l1 = [jnp.where(causal, ls[h][1], 0.0) if diag else ls[h][1] for h in hs]
            after = [_tri_dot(up, l1[h]) for h in hs]
            a = [jnp.exp2(lb[h] + after[h] + cs[h]) for h in hs]
            if diag:
                a = [jnp.where(causal, a[h], 0.0) for h in hs]
            for h in hs:
                c_ref[h, j] = cs[h]
                acc = acc + _dot(jnp.where(sub_head == h, vt, jnp.zeros_like(vt)), a[h].astype(BF16))
                cs[h] = cs[h] + after[h][0:1, :] + l1[h][0:1, :]
            return cs[0], cs[1], acc

        def alive(c0, c1):
            return (jnp.max(jnp.maximum(c0, c1)) > -SB_DEAD_LOG2).astype(jnp.int32)

        def step(state):
            s, _, c0, c1, acc = state
            c0, c1, acc = block(i - 1 - s, (c0, c1, acc), False)
            return s + 1, alive(c0, c1), c0, c1, acc

        zero = jnp.zeros((1, tq), F32)
        c0, c1, acc = block(i, (zero, zero, jnp.zeros((128, tq), F32)), True)
        state = lax.while_loop(lambda st: (st[0] < i) & (st[1] > 0), step, (jnp.int32(0), alive(c0, c1), c0, c1, acc))
        ot_ref[...] = state[4].astype(BF16)
        n_ref[pl.program_id(0), i] = (state[0] + 1).astype(F32)

    return pl.pallas_call(
        body, name=name, grid=(SB_PAIRS, nb),
        in_specs=[pl.BlockSpec((tq, 128), lambda p, i: (i, p)),
                  pl.BlockSpec((t, 128), lambda p, i: (0, SB_PAIRS + p)),
                  pl.BlockSpec((None, nb, 128, tq), lambda p, i: (p, 0, 0, 0)),
                  pl.BlockSpec((tq, tq), lambda p, i: (0, 0))],
        out_specs=[pl.BlockSpec((128, tq), lambda p, i: (p, i)),
                   pl.BlockSpec((2, nb, 1, tq), lambda p, i: (p, 0, 0, i)),
                   pl.BlockSpec(memory_space=pltpu.SMEM)],
        out_shape=[jax.ShapeDtypeStruct((SB_WIDTH, t), BF16),
                   jax.ShapeDtypeStruct((2 * SB_PAIRS, nb, 1, t), F32),
                   jax.ShapeDtypeStruct((SB_PAIRS, nb), F32)],
        compiler_params=_params(("arbitrary", "arbitrary")),
    )(qkv, qkv, vtb, upper)


def _outproj_mlp_fwd(x, ya, yb, yc, wo, g2, wup, wdn, tm, name):
    t, d = x.shape
    nf = wup.shape[0]

    def body(x_ref, ya_ref, yb_ref, yc_ref, wo_ref, g_ref, wu_ref, wd_ref, x1_ref, u_ref, x2_ref, h_ref, acc_ref):
        j = pl.program_id(1)

        @pl.when(j == 0)
        def _():
            x1 = (x_ref[...] + _dot(ya_ref[...], wo_ref[0:POOL_WIDTH, :])
                  + _dot(yb_ref[...], wo_ref[POOL_WIDTH:POOL_WIDTH + SG_WIDTH, :])
                  + _dot(yc_ref[...], wo_ref[POOL_WIDTH + SG_WIDTH:, :]))
            x1_ref[...] = x1
            h_ref[...] = (x1 * _rstd(x1) * g_ref[...]).astype(BF16)
            acc_ref[...] = x1

        u = _dot(h_ref[...], wu_ref[...])
        u_ref[...] = u
        r = jnp.maximum(u, 0.0)
        acc_ref[...] += _dot((r * r).astype(BF16), wd_ref[...])

        @pl.when(j == nf - 1)
        def _():
            x2_ref[...] = acc_ref[...]

    row = lambda w: pl.BlockSpec((tm, w), lambda i, j: (i, 0))
    return pl.pallas_call(
        body, name=name, grid=(t // tm, nf),
        in_specs=[row(d), row(POOL_WIDTH), row(SG_WIDTH), row(SB_WIDTH),
                  pl.BlockSpec((d, d), lambda i, j: (0, 0)), pl.BlockSpec((1, d), lambda i, j: (0, 0)),
                  pl.BlockSpec((None, d, FF_SHARD), lambda i, j: (j, 0, 0)),
                  pl.BlockSpec((None, FF_SHARD, d), lambda i, j: (j, 0, 0))],
        out_specs=[row(d), pl.BlockSpec((tm, FF_SHARD), lambda i, j: (i, j)), row(d)],
        out_shape=[jax.ShapeDtypeStruct((t, d), F32), jax.ShapeDtypeStruct((t, nf * FF_SHARD), F32),
                   jax.ShapeDtypeStruct((t, d), F32)],
        scratch_shapes=[pltpu.VMEM((tm, d), BF16), pltpu.VMEM((tm, d), F32)],
        compiler_params=_params(("parallel", "arbitrary")),
    )(x, ya, yb, yc, wo, g2, wup, wdn)


def _loss_grad(x, g, target, tm, name):
    t, d = x.shape
    nt = t // tm

    def body(x_ref, g_ref, t_ref, loss_ref, dx_ref, dg_ref, sq_ref):
        i = pl.program_id(0)

        @pl.when(i == 0)
        def _():
            sq_ref[...] = jnp.zeros_like(sq_ref)
            dg_ref[...] = jnp.zeros_like(dg_ref)

        xx = x_ref[...]
        r = _rstd(xx)
        err = xx * r * g_ref[...] - t_ref[...]
        sq_ref[...] += jnp.sum(err * err, axis=0, keepdims=True)
        dx, dg = _rms_bwd(xx, r, g_ref[...], err * (1.0 / d))
        dx_ref[...] = dx
        dg_ref[...] += dg

        @pl.when(i == nt - 1)
        def _():
            loss_ref[...] = jnp.sum(sq_ref[...], axis=1, keepdims=True) * (0.5 / d)

    return pl.pallas_call(
        body, name=name, grid=(nt,),
        in_specs=[pl.BlockSpec((tm, d), lambda i: (i, 0)), pl.BlockSpec((1, d), lambda i: (0, 0)),
                  pl.BlockSpec((tm, d), lambda i: (i, 0))],
        out_specs=[pl.BlockSpec((1, 1), lambda i: (0, 0)), pl.BlockSpec((tm, d), lambda i: (i, 0)),
                   pl.BlockSpec((1, d), lambda i: (0, 0))],
        out_shape=[jax.ShapeDtypeStruct((1, 1), F32), jax.ShapeDtypeStruct((t, d), F32),
                   jax.ShapeDtypeStruct((1, d), F32)],
        scratch_shapes=[pltpu.VMEM((1, d), F32)],
        compiler_params=_params(("arbitrary",)),
    )(x, g, target)


def _mlp_bwd(dx2, x1, g2, u, wup, wdn, tm, name):
    t, d = dx2.shape
    nf = wup.shape[0]
    nt = t // tm

    def body(dx2_ref, x1_ref, g_ref, u_ref, wu_ref, wd_ref, dx1_ref, du_ref, r_ref, h_ref, dxb_ref, dg_ref, acc_ref):
        i, j = pl.program_id(0), pl.program_id(1)

        @pl.when(j == 0)
        def _():
            x1 = x1_ref[...]
            h_ref[...] = (x1 * _rstd(x1) * g_ref[...]).astype(BF16)
            dxb_ref[...] = dx2_ref[...].astype(BF16)
            acc_ref[...] = jnp.zeros_like(acc_ref)

        @pl.when((i == 0) & (j == 0))
        def _():
            dg_ref[...] = jnp.zeros_like(dg_ref)

        dr = _dot_nt(dxb_ref[...], wd_ref[...])
        ru = jnp.maximum(u_ref[...], 0.0)
        du = (dr * (2.0 * ru)).astype(BF16)
        du_ref[...] = du
        r_ref[...] = (ru * ru).astype(BF16)
        acc_ref[...] += _dot_nt(du, wu_ref[...])

        @pl.when(j == nf - 1)
        def _():
            x1 = x1_ref[...]
            dx, dg = _rms_bwd(x1, _rstd(x1), g_ref[...], acc_ref[...])
            dx1_ref[...] = dx2_ref[...] + dx
            dg_ref[...] += dg

    row = lambda w: pl.BlockSpec((tm, w), lambda i, j: (i, 0))
    col = pl.BlockSpec((tm, FF_SHARD), lambda i, j: (i, j))
    return pl.pallas_call(
        body, name=name, grid=(nt, nf),
        in_specs=[row(d), row(d), pl.BlockSpec((1, d), lambda i, j: (0, 0)), col,
                  pl.BlockSpec((None, d, FF_SHARD), lambda i, j: (j, 0, 0)),
                  pl.BlockSpec((None, FF_SHARD, d), lambda i, j: (j, 0, 0))],
        out_specs=[row(d), col, col, row(d), row(d), pl.BlockSpec((1, d), lambda i, j: (0, 0))],
        out_shape=[jax.ShapeDtypeStruct((t, d), F32), jax.ShapeDtypeStruct((t, nf * FF_SHARD), BF16),
                   jax.ShapeDtypeStruct((t, nf * FF_SHARD), BF16), jax.ShapeDtypeStruct((t, d), BF16),
                   jax.ShapeDtypeStruct((t, d), BF16), jax.ShapeDtypeStruct((1, d), F32)],
        scratch_shapes=[pltpu.VMEM((tm, d), F32)],
        compiler_params=_params(("arbitrary", "arbitrary")),
    )(dx2, x1, g2, u, wup, wdn)


def _matmul_tn(a, b, bm, bn, bt, name, by_column_block=False):
    t, m = a.shape
    n = b.shape[1]
    nk = t // bt

    def body(a_ref, b_ref, o_ref, acc_ref):
        k = pl.program_id(2)

        @pl.when(k == 0)
        def _():
            acc_ref[...] = jnp.zeros_like(acc_ref)

        acc_ref[...] += _dot_tn(a_ref[...], b_ref[...])

        @pl.when(k == nk - 1)
        def _():
            o_ref[...] = acc_ref[...].astype(BF16)

    if by_column_block:
        out_spec = pl.BlockSpec((None, bm, bn), lambda i, j, k: (j, i, 0))
        out_shape = jax.ShapeDtypeStruct((n // bn, m, bn), BF16)
    else:
        out_spec = pl.BlockSpec((bm, bn), lambda i, j, k: (i, j))
        out_shape = jax.ShapeDtypeStruct((m, n), BF16)
    return pl.pallas_call(
        body, name=name, grid=(m // bm, n // bn, nk),
        in_specs=[pl.BlockSpec((bt, bm), lambda i, j, k: (k, i)), pl.BlockSpec((bt, bn), lambda i, j, k: (k, j))],
        out_specs=out_spec, out_shape=out_shape,
        scratch_shapes=[pltpu.VMEM((bm, bn), F32)],
        compiler_params=_params(("parallel", "parallel", "arbitrary")),
    )(a, b)


def _outproj_bwd(dx1, wo, tm, name):
    t, d = dx1.shape
    c2 = POOL_WIDTH + SG_WIDTH

    def body(dx_ref, wo_ref, dya_ref, dyb_ref, dyc_ref, dyct_ref, dxb_ref):
        dxb = dx_ref[...].astype(BF16)
        dxb_ref[...] = dxb
        dya_ref[...] = _dot_nt(dxb, wo_ref[0:POOL_WIDTH, :])
        dyb_ref[...] = _dot_nt(dxb, wo_ref[POOL_WIDTH:c2, :])
        dyc_ref[...] = _dot_nt(dxb, wo_ref[c2:, :]).astype(BF16)
        dyct_ref[...] = _dot_nt(wo_ref[c2:, :], dxb).astype(BF16)

    row = lambda w: pl.BlockSpec((tm, w), lambda i: (i, 0))
    return pl.pallas_call(
        body, name=name, grid=(t // tm,),
        in_specs=[row(d), pl.BlockSpec((d, d), lambda i: (0, 0))],
        out_specs=[row(POOL_WIDTH), row(SG_WIDTH), row(SB_WIDTH), pl.BlockSpec((SB_WIDTH, tm), lambda i: (0, i)), row(d)],
        out_shape=[jax.ShapeDtypeStruct((t, POOL_WIDTH), F32), jax.ShapeDtypeStruct((t, SG_WIDTH), F32),
                   jax.ShapeDtypeStruct((t, SB_WIDTH), BF16), jax.ShapeDtypeStruct((SB_WIDTH, t), BF16),
                   jax.ShapeDtypeStruct((t, d), BF16)],
        compiler_params=_params(("parallel",)),
    )(dx1, wo)


def _sba_bwd(qkv, ktb, dyc, dyct, cc, nvis, tq, name):
    t = qkv.shape[0]
    nb = t // tq
    idx = jnp.arange(tq)
    upper = (idx[None, :] > idx[:, None]).astype(BF16)
    lower = (idx[None, :] < idx[:, None]).astype(BF16)

    def body(q_ref, k_ref, v_ref, kt_ref, do_ref, dot_ref, c_ref, n_ref, up_ref, lo_ref, dqt_ref, dk_ref, dv_ref):
        i = pl.program_id(1)

        @pl.when(i == 0)
        def _():
            dk_ref[...] = jnp.zeros_like(dk_ref)
            dv_ref[...] = jnp.zeros_like(dv_ref)

        q = q_ref[...]
        do = do_ref[...]
        dot = dot_ref[...]
        up = up_ref[...]
        lo = lo_ref[...]
        lane_head = lax.broadcasted_iota(jnp.int32, (1, 128), 1) // SB_HD
        sub_head = lax.broadcasted_iota(jnp.int32, (128, 1), 0) // SB_HD
        causal = (lax.broadcasted_iota(jnp.int32, (tq, tq), 0) < lax.broadcasted_iota(jnp.int32, (tq, tq), 1))
        hms = [lane_head == h for h in range(2)]
        qh = [jnp.where(hm, q, jnp.zeros_like(q)) for hm in hms]
        qs = [x * 0.125 for x in qh]
        doh = [jnp.where(hm, do, jnp.zeros_like(do)) for hm in hms]

        def block(j, carry, diag):
            cgs, dqt = list(carry[:2]), carry[2]
            rows = pl.ds(pl.multiple_of(j * tq, tq), tq)
            kj = k_ref[rows, :]
            vj = v_ref[rows, :]
            kt = kt_ref[j]
            hs = range(2)
            z = [_dot_nt(kj, qh[h]) for h in hs]
            da = [_dot(jnp.where(hms[h], vj, jnp.zeros_like(vj)), dot) for h in hs]
            ls = [_log2_sigmoids(z[h] * SB_SCALE) for h in hs]
            lb = [ls[h][0] for h in hs]
            l1 = [jnp.where(causal, ls[h][1], 0.0) if diag else ls[h][1] for h in hs]
            after = [_tri_dot(up, l1[h]) for h in hs]
            a = [jnp.exp2(lb[h] + after[h] + c_ref[h, j]) for h in hs]
            if diag:
                a = [jnp.where(causal, a[h], 0.0) for h in hs]
            g = [a[h] * da[h] for h in hs]
            gsum = [_tri_dot(lo, g[h]) + cgs[h] for h in hs]
            dz = [g[h] - jnp.exp2(lb[h]) * (g[h] + gsum[h]) for h in hs]
            if diag:
                dz = [jnp.where(causal, dz[h], 0.0) for h in hs]
            dzb = [dz[h].astype(BF16) for h in hs]
            ab = [a[h].astype(BF16) for h in hs]
            for h in hs:
                dqt = dqt + _dot(jnp.where(sub_head == h, kt, jnp.zeros_like(kt)), dzb[h])
            dk_ref[rows, :] += _dot(dzb[0], qs[0]) + _dot(dzb[1], qs[1])
            dv_ref[rows, :] += _dot(ab[0], doh[0]) + _dot(ab[1], doh[1])
            return (gsum[0][tq - 1:tq, :] + g[0][tq - 1:tq, :], gsum[1][tq - 1:tq, :] + g[1][tq - 1:tq, :], dqt)

        n = jnp.clip(n_ref[pl.program_id(0), i].astype(jnp.int32), 1, i + 1)
        zero = jnp.zeros((1, tq), F32)
        carry = lax.fori_loop(i + 1 - n, i, lambda s, cr: block(s, cr, False), (zero, zero, jnp.zeros((128, tq), F32)))
        dqt_ref[...] = block(i, carry, True)[2] * 0.125

    return pl.pallas_call(
        body, name=name, grid=(SB_PAIRS, nb),
        in_specs=[pl.BlockSpec((tq, 128), lambda p, i: (i, p)),
                  pl.BlockSpec((t, 128), lambda p, i: (0, SB_PAIRS + p)),
                  pl.BlockSpec((t, 128), lambda p, i: (0, 2 * SB_PAIRS + p)),
                  pl.BlockSpec((None, nb, 128, tq), lambda p, i: (p, 0, 0, 0)),
                  pl.BlockSpec((tq, 128), lambda p, i: (i, p)),
                  pl.BlockSpec((128, tq), lambda p, i: (p, i)),
                  pl.BlockSpec((2, nb, 1, tq), lambda p, i: (p, 0, 0, i)),
                  pl.BlockSpec(memory_space=pltpu.SMEM),
                  pl.BlockSpec((tq, tq), lambda p, i: (0, 0)),
                  pl.BlockSpec((tq, tq), lambda p, i: (0, 0))],
        out_specs=[pl.BlockSpec((128, tq), lambda p, i: (p, i)),
                   pl.BlockSpec((t, 128), lambda p, i: (0, p)),
                   pl.BlockSpec((t, 128), lambda p, i: (0, p))],
        out_shape=[jax.ShapeDtypeStruct((SB_WIDTH, t), F32), jax.ShapeDtypeStruct((t, SB_WIDTH), F32),
                   jax.ShapeDtypeStruct((t, SB_WIDTH), F32)],
        compiler_params=_params(("arbitrary", "arbitrary")),
    )(qkv, qkv, qkv, ktb, dyc, dyct, cc, nvis, upper, lower)


def _sg_bwd(ab, dyb, gn, wm, wmt, bfull, tm, name):
    t = ab.shape[0]
    nt = t // tm
    sel = (jnp.arange(SG_WIDTH)[:, None] // SG_HD == jnp.arange(CHUNK)[None, :]).astype(F32)

    def body(u_ref, v_ref, dy_ref, gn_ref, wm_ref, wmt_ref, b_ref, sel_ref,
             dup_ref, dvp_ref, dgn_ref, dw_ref, db_ref, dbacc_ref):
        i = pl.program_id(0)

        @pl.when(i == 0)
        def _():
            dgn_ref[...] = jnp.zeros_like(dgn_ref)
            dw_ref[...] = jnp.zeros_like(dw_ref)
            dbacc_ref[...] = jnp.zeros_like(dbacc_ref)

        tril = (lax.broadcasted_iota(jnp.int32, (CHUNK, CHUNK), 0) >= lax.broadcasted_iota(jnp.int32, (CHUNK, CHUNK), 1))
        gn_ = gn_ref[...]
        for c in range(tm // CHUNK):
            rows = slice(c * CHUNK, (c + 1) * CHUNK)
            up, vp, dy = u_ref[rows, :], v_ref[rows, :], dy_ref[rows, :]
            u, v = _gelu(up), _gelu(vp)
            r = _rstd(v)
            vn = (v * r * gn_).astype(BF16)
            sv = b_ref[...]
            for h in range(SG_HEADS):
                sv = sv + jnp.where(_head_lanes(h), _dot(wm_ref[h], vn), 0.0)
            dup_ref[rows, :] = dy * sv * _gelu_grad(up)
            dsv = dy * u
            dbacc_ref[...] += dsv
            dvn = jnp.zeros((CHUNK, SG_WIDTH), F32)
            for h in range(SG_HEADS):
                dsv_h = jnp.where(_head_lanes(h), dsv, 0.0).astype(BF16)
                dvn = dvn + _dot(wmt_ref[h], dsv_h)
                dw_ref[h] += jnp.where(tril, _dot_nt(dsv_h, vn), 0.0)
            dv, dgn = _rms_bwd(v, r, gn_, dvn)
            dgn_ref[...] += dgn
            dvp_ref[rows, :] = dv * _gelu_grad(vp)

        @pl.when(i == nt - 1)
        def _():
            db_ref[...] = jnp.dot(dbacc_ref[...], sel_ref[...], preferred_element_type=F32,
                                  precision=lax.Precision.HIGHEST)

    const = lambda shape: pl.BlockSpec(shape, lambda i: tuple(0 for _ in shape))
    return pl.pallas_call(
        body, name=name, grid=(nt,),
        in_specs=[pl.BlockSpec((tm, SG_WIDTH), lambda i: (i, 1)), pl.BlockSpec((tm, SG_WIDTH), lambda i: (i, 2)),
                  pl.BlockSpec((tm, SG_WIDTH), lambda i: (i, 0)), const((1, SG_WIDTH)),
                  const((SG_HEADS, CHUNK, CHUNK)), const((SG_HEADS, CHUNK, CHUNK)), const((CHUNK, SG_WIDTH)),
                  const((SG_WIDTH, CHUNK))],
        out_specs=[pl.BlockSpec((tm, SG_WIDTH), lambda i: (i, 0)), pl.BlockSpec((tm, SG_WIDTH), lambda i: (i, 0)),
                   const((1, SG_WIDTH)), const((SG_HEADS, CHUNK, CHUNK)), const((CHUNK, CHUNK))],
        out_shape=[jax.ShapeDtypeStruct((t, SG_WIDTH), F32), jax.ShapeDtypeStruct((t, SG_WIDTH), F32),
                   jax.ShapeDtypeStruct((1, SG_WIDTH), F32), jax.ShapeDtypeStruct((SG_HEADS, CHUNK, CHUNK), F32),
                   jax.ShapeDtypeStruct((CHUNK, CHUNK), F32)],
        scratch_shapes=[pltpu.VMEM((CHUNK, SG_WIDTH), F32)],
        compiler_params=_params(("arbitrary",)),
    )(ab, ab, dyb, gn, wm, wmt, bfull, sel)


def _pool_bwd(ab, dya, wbd, scale, tm, name):
    t = ab.shape[0]
    nt = t // tm
    hb = tm // POOL_HALO
    nh = t // POOL_HALO

    def body(cur_ref, prev_ref, dy_ref, dyn_ref, w_ref, s_ref, da_ref, dw_ref, ds_ref):
        i = pl.program_id(0)

        @pl.when(i == 0)
        def _():
            dw_ref[...] = jnp.zeros_like(dw_ref)
            ds_ref[...] = jnp.zeros_like(ds_ref)

        prev = jnp.where(i == 0, 0.0, prev_ref[...])
        d = _pool_diff(cur_ref[...], prev, i * tm).astype(BF16)
        dy = dy_ref[...]
        ds_ref[...] += jnp.sum(dy * _dot(d, w_ref[...]), axis=0, keepdims=True)
        dyn = jnp.where(i == nt - 1, 0.0, dyn_ref[...])
        dys = (jnp.concatenate([dy, dyn], axis=0) * s_ref[...]).astype(BF16)
        dw_ref[...] += _dot_tn(d, dys[:tm])
        dd = _dot_nt(dys, w_ref[...])
        fwd = _pool_window_sums(dd / _pool_count(i * tm, tm + POOL_HALO), True)
        da_ref[...] = fwd[:tm] - dd[:tm]

    return pl.pallas_call(
        body, name=name, grid=(nt,),
        in_specs=[pl.BlockSpec((tm, POOL_WIDTH), lambda i: (i, 0)),
                  pl.BlockSpec((POOL_HALO, POOL_WIDTH), lambda i: (jnp.maximum(i * hb - 1, 0), 0)),
                  pl.BlockSpec((tm, POOL_WIDTH), lambda i: (i, 0)),
                  pl.BlockSpec((POOL_HALO, POOL_WIDTH), lambda i: (jnp.minimum((i + 1) * hb, nh - 1), 0)),
                  pl.BlockSpec((POOL_WIDTH, POOL_WIDTH), lambda i: (0, 0)),
                  pl.BlockSpec((1, POOL_WIDTH), lambda i: (0, 0))],
        out_specs=[pl.BlockSpec((tm, POOL_WIDTH), lambda i: (i, 0)),
                   pl.BlockSpec((POOL_WIDTH, POOL_WIDTH), lambda i: (0, 0)),
                   pl.BlockSpec((1, POOL_WIDTH), lambda i: (0, 0))],
        out_shape=[jax.ShapeDtypeStruct((t, POOL_WIDTH), F32), jax.ShapeDtypeStruct((POOL_WIDTH, POOL_WIDTH), F32),
                   jax.ShapeDtypeStruct((1, POOL_WIDTH), F32)],
        compiler_params=_params(("arbitrary",)),
    )(ab, ab, dya, dya, wbd, scale)


def _inproj_bwd(dx1, x, g, da, dup, dvp, dq, dk, dv, w, tm, name):
    t, d = x.shape
    n = w.shape[1]
    nt = t // tm

    def body(dx1_ref, x_ref, g_ref, da_ref, du_ref, dv_ref, dq_ref, dk_ref, dvv_ref, w_ref,
             dx_ref, h_ref, dp_ref, dg_ref):
        @pl.when(pl.program_id(0) == 0)
        def _():
            dg_ref[...] = jnp.zeros_like(dg_ref)

        dp = jnp.concatenate([da_ref[...], du_ref[...], dv_ref[...], dq_ref[...], dk_ref[...], dvv_ref[...]],
                             axis=1).astype(BF16)
        dp_ref[...] = dp
        xx = x_ref[...]
        r = _rstd(xx)
        h_ref[...] = (xx * r * g_ref[...]).astype(BF16)
        dx, dg = _rms_bwd(xx, r, g_ref[...], _dot_nt(dp, w_ref[...]))
        dx_ref[...] = dx1_ref[...] + dx
        dg_ref[...] += dg

    row = lambda w_: pl.BlockSpec((tm, w_), lambda i: (i, 0))
    return pl.pallas_call(
        body, name=name, grid=(nt,),
        in_specs=[row(d), row(d), pl.BlockSpec((1, d), lambda i: (0, 0)), row(POOL_WIDTH), row(SG_WIDTH),
                  row(SG_WIDTH), row(SB_WIDTH), row(SB_WIDTH), row(SB_WIDTH), pl.BlockSpec((d, n), lambda i: (0, 0))],
        out_specs=[row(d), row(d), row(n), pl.BlockSpec((1, d), lambda i: (0, 0))],
        out_shape=[jax.ShapeDtypeStruct((t, d), F32), jax.ShapeDtypeStruct((t, d), BF16),
                   jax.ShapeDtypeStruct((t, n), BF16), jax.ShapeDtypeStruct((1, d), F32)],
        compiler_params=_params(("arbitrary",)),
    )(dx1, x, g, da, dup, dvp, dq, dk, dv, w)


def _gather_layer(w_in, w_out, w_up, w_down, l, name):
    gi, go, gu, gd = _exchange([w[l].astype(BF16) for w in (w_in, w_out, w_up, w_down)], False, name)
    return gi.transpose(1, 0, 2).reshape(D_MODEL, IN_COLS), go.reshape(D_MODEL, D_MODEL), gu, gd


_SMALL_SHAPES = ((DEPTH, D_MODEL), (DEPTH, 4, POOL_GW, POOL_GW), (DEPTH, POOL_WIDTH), (DEPTH, SG_WIDTH),
                 (DEPTH, SG_HEADS, CHUNK, CHUNK), (DEPTH, SG_HEADS, CHUNK), (DEPTH, D_MODEL), (D_MODEL,))


def _pack_small(arrs):
    return jnp.concatenate([a.reshape(-1) for a in arrs]).reshape(-1, 128)


def _unpack_small(buf):
    flat = buf.reshape(-1)
    outs, off = [], 0
    for shp in _SMALL_SHAPES:
        size = 1
        for s in shp:
            size *= s
        outs.append(flat[off:off + size].reshape(shp))
        off += size
    return outs


def _tiles(t):
    return min(512, t), min(256, t // 4)


def _layer_fwd(xl, gathered, small_w, l):
    wi, wo, wu, wd = gathered
    n1, pw, ps, sn, sw, sb, n2 = small_w
    tm, tq = _tiles(xl.shape[0])
    nb = xl.shape[0] // tq
    wbd = jnp.zeros((4, POOL_GW, 4, POOL_GW), F32)
    for gi in range(4):
        wbd = wbd.at[gi, :, gi, :].set(pw[gi])
    wbd = wbd.reshape(POOL_WIDTH, POOL_WIDTH).astype(BF16)
    wm = (sw * jnp.tril(jnp.ones((CHUNK, CHUNK), F32))).astype(BF16)
    wmt = wm.transpose(0, 2, 1)
    bfull = jnp.repeat(sb.T, SG_HD, axis=1)
    g1, g2, psc, gn = n1[None, :], n2[None, :], ps[None, :], sn[None, :]

    ab, qkv = _inproj_fwd(xl, g1, wi, tm, f"inproj_fwd{l}")
    ya = _pool_fwd(ab, wbd, psc, tm, f"pool_fwd{l}")
    yb = _sg_fwd(ab, gn, wm, bfull, tm, f"sg_fwd{l}")
    vtb = qkv[:, 2 * SB_WIDTH:].reshape(nb, tq, SB_PAIRS, 128).transpose(2, 0, 3, 1)
    oct_, cc, nvis = _sba_fwd(qkv, vtb, tq, f"sba_fwd{l}")
    yc = oct_.T
    x1, u, x2 = _outproj_mlp_fwd(xl, ya, yb, yc, wo, g2, wu, wd, tm, f"mlp_fwd{l}")
    saved = dict(x=xl, ab=ab, qkv=qkv, cc=cc, nvis=nvis, ya=ya, yb=yb, yc=yc, x1=x1, u=u, wi=wi, wo=wo, wu=wu, wd=wd,
                 wbd=wbd, wm=wm, wmt=wmt, bfull=bfull, g1=g1, g2=g2, psc=psc, gn=gn)
    return x2, saved


def _layer_bwd(dx, s, l):
    tm, tq = _tiles(dx.shape[0])
    nb = dx.shape[0] // tq
    dx1, du, r, h2, dx2b, dn2 = _mlp_bwd(dx, s["x1"], s["g2"], s["u"], s["wu"], s["wd"], tm, f"mlp_bwd{l}")
    dw_up = _matmul_tn(h2, du, D_MODEL, FF_SHARD, tm, f"dw_up{l}", by_column_block=True)
    dw_down = _matmul_tn(r, dx2b, 1024, D_MODEL, tm, f"dw_down{l}").reshape(N_DEV, FF_SHARD, D_MODEL)
    dya, dyb, dyc, dyct, dx1b = _outproj_bwd(dx1, s["wo"], tm, f"outproj_bwd{l}")
    mix = jnp.concatenate([s["ya"], s["yb"], s["yc"]], axis=1)
    dw_out = _matmul_tn(mix, dx1b, D_MODEL, D_MODEL, tm, f"dw_out{l}").reshape(N_DEV, OUT_SHARD, D_MODEL)
    ktb = s["qkv"][:, SB_WIDTH:2 * SB_WIDTH].reshape(nb, tq, SB_PAIRS, 128).transpose(2, 0, 3, 1)
    dqt, dk, dv = _sba_bwd(s["qkv"], ktb, dyc, dyct, s["cc"], s["nvis"], tq, f"sba_bwd{l}")
    dup, dvp, dgn, dwm, dbm = _sg_bwd(s["ab"], dyb, s["gn"], s["wm"], s["wmt"], s["bfull"], tm, f"sg_bwd{l}")
    da, dwbd, dpsc = _pool_bwd(s["ab"], dya, s["wbd"], s["psc"], tm, f"pool_bwd{l}")
    dx, h1, dproj, dn1 = _inproj_bwd(dx1, s["x"], s["g1"], da, dup, dvp, dqt.T, dk, dv, s["wi"], tm, f"inproj_bwd{l}")
    dw_in = _matmul_tn(h1, dproj, D_MODEL, IN_COLS // 3, tm, f"dw_in{l}")
    dw_in = dw_in.reshape(D_MODEL, N_DEV, IN_SHARD).transpose(1, 0, 2)
    dpw = jnp.stack([dwbd[gi * POOL_GW:(gi + 1) * POOL_GW, gi * POOL_GW:(gi + 1) * POOL_GW] for gi in range(4)])
    return dx, (dw_in, dw_out, dw_up, dw_down), (dn1[0], dpw, dpsc[0], dgn[0], dwm, dbm[:, :SG_HEADS].T, dn2[0])


def kernel(x, norm1, w_in, pool_w, pool_scale, sg_norm, sg_w, sg_b, w_out, norm2, w_up, w_down, final_norm, loss_target, m_norm1, m_w_in, m_pool_w, m_pool_scale, m_sg_norm, m_sg_w, m_sg_b, m_w_out, m_norm2, m_w_up, m_w_down, m_final_norm, v_norm1, v_w_in, v_pool_w, v_pool_scale, v_sg_norm, v_sg_w, v_sg_b, v_w_out, v_norm2, v_w_up, v_w_down, v_final_norm):
    t = x.shape[1]
    tm, _ = _tiles(t)
    small_w = (norm1, pool_w, pool_scale, sg_norm, sg_w, sg_b, norm2)
    big_w = (w_in, w_out, w_up, w_down)
    big_m = (m_w_in, m_w_out, m_w_up, m_w_down)
    big_v = (v_w_in, v_w_out, v_w_up, v_w_down)

    xl = x.reshape(t, D_MODEL)
    saved = []
    for l in range(DEPTH):
        gathered = _gather_layer(*big_w, l, f"gather_weights{l}")
        xl, s = _layer_fwd(xl, gathered, tuple(w[l] for w in small_w), l)
        saved.append(s)
    loss_local, dx, dfinal = _loss_grad(xl, final_norm[None, :], loss_target.reshape(t, D_MODEL), tm, "loss_grad")
    loss = lax.psum(loss_local[0, 0], MESH_AXES)

    small_g = [None] * DEPTH
    big = [None] * 4
    for l in reversed(range(DEPTH)):
        dx, parts, small_g[l] = _layer_bwd(dx, saved[l], l)
        received = _exchange(list(parts), True, f"scatter_grads{l}")
        for k in range(4):
            big[k] = _reduce_adamw(received[k], big_w[k], big_m[k], big_v[k], l, big[k], f"adamw{k}_{l}")
    grad_x = dx.reshape(x.shape)

    small_local = [jnp.stack([small_g[l][k] for l in range(DEPTH)]) for k in range(7)] + [dfinal[0]]
    small_all = _exchange([_pack_small(small_local)], False, "gather_small_grads")[0]
    sm = _reduce_adamw(
        small_all,
        _pack_small([norm1, pool_w, pool_scale, sg_norm, sg_w, sg_b, norm2, final_norm])[None],
        _pack_small([m_norm1, m_pool_w, m_pool_scale, m_sg_norm, m_sg_w, m_sg_b, m_norm2, m_final_norm])[None],
        _pack_small([v_norm1, v_pool_w, v_pool_scale, v_sg_norm, v_sg_w, v_sg_b, v_norm2, v_final_norm])[None],
        0, None, "adamw_replicated")

    out = [loss, grad_x]
    for k in range(4):
        n1, pw, ps, sn, sw, sb, n2, fn = _unpack_small(sm[k][0])
        out += [n1, big[0][k], pw, ps, sn, sw, sb, big[1][k], n2, big[2][k], big[3][k], fn]
    return tuple(out)
```

```python
import functools

import jax
import jax.numpy as jnp
from jax import lax
from jax.experimental import pallas as pl
from jax.experimental.pallas import tpu as pltpu

F32 = jnp.float32
BF16 = jnp.bfloat16

D_MODEL = 1024
DEPTH = 2
POOL_WIDTH = 256
SG_WIDTH = 256
SB_WIDTH = 512
POOL_WINDOWS = (2, 4, 8, 16)
POOL_GW = 64
POOL_HALO = 16
CHUNK = 128
SG_HEADS = 4
SG_HD = 64
SB_HD = 64
SB_PAIRS = SB_WIDTH // 128
AB_COLS = POOL_WIDTH + 2 * SG_WIDTH
IN_COLS = AB_COLS + 3 * SB_WIDTH
D_FF = 4096
EPS = 1e-6
N_DEV = 8
FF_SHARD = D_FF // N_DEV
IN_SHARD = IN_COLS // N_DEV
OUT_SHARD = D_MODEL // N_DEV
ADAM_LR = 0.001
ADAM_B1 = 0.9
ADAM_B2 = 0.999
ADAM_EPS = 1e-08
ADAM_WD = 0.01
ADAM_STEP = 10
VMEM_LIMIT = 52 * 1024 * 1024
MESH_AXES = ("x", "y", "c")


def _dot(a, b):
    return jnp.dot(a, b, preferred_element_type=F32)


def _dot_nt(a, b):
    return lax.dot_general(a, b, (((1,), (1,)), ((), ())), preferred_element_type=F32)


def _dot_tn(a, b):
    return lax.dot_general(a, b, (((0,), (0,)), ((), ())), preferred_element_type=F32)


def _rstd(x):
    return lax.rsqrt(jnp.mean(x * x, axis=-1, keepdims=True) + EPS)


def _rms_bwd(x, r, g, dh):
    gq = dh * g
    dx = r * gq - x * (r * r * r) * jnp.mean(gq * x, axis=-1, keepdims=True)
    dg = jnp.sum(dh * x * r, axis=0, keepdims=True)
    return dx, dg


def _params(sem=None):
    kw = dict(vmem_limit_bytes=VMEM_LIMIT)
    if sem is not None:
        kw["dimension_semantics"] = sem
    return pltpu.CompilerParams(**kw)


def _row_tile(rows, cap):
    best = 8
    for t in range(8, min(rows, cap) + 1, 8):
        if rows % t == 0:
            best = t
    return best


def _peer(k):
    x, y, c = lax.axis_index("x"), lax.axis_index("y"), lax.axis_index("c")
    return (1 - x if k & 4 else x, 1 - y if k & 2 else y, 1 - c if k & 1 else c)


def _my_index():
    return 4 * lax.axis_index("x") + 2 * lax.axis_index("y") + lax.axis_index("c")


class Exchange:
    def __init__(self, arrs, scatter):
        self.arrs = list(arrs)
        self.scatter = list(scatter) if isinstance(scatter, (list, tuple)) else [scatter] * len(self.arrs)
        self.n = len(self.arrs)
        self.any_specs = [pl.BlockSpec(memory_space=pl.ANY)] * self.n
        self.out_shape = [jax.ShapeDtypeStruct((N_DEV,) + a.shape[-2:], a.dtype) for a in self.arrs]
        self.sems = [pltpu.SemaphoreType.DMA((self.n, N_DEV - 1)), pltpu.SemaphoreType.DMA((self.n, N_DEV - 1)),
                     pltpu.SemaphoreType.DMA((self.n,))]

    def _copies(self, ins, outs, sems):
        send_sems, recv_sems, local_sems = sems
        me = _my_index()
        local, remote = [], []
        for a in range(self.n):
            sc = self.scatter[a]
            local.append(pltpu.make_async_copy(ins[a].at[me] if sc else ins[a], outs[a].at[me], local_sems.at[a]))
            for k in range(1, N_DEV):
                px, py, pc = _peer(k)
                remote.append(pltpu.make_async_remote_copy(
                    src_ref=ins[a].at[4 * px + 2 * py + pc] if sc else ins[a], dst_ref=outs[a].at[me],
                    send_sem=send_sems.at[a, k - 1], recv_sem=recv_sems.at[a, k - 1],
                    device_id=(px, py, pc), device_id_type=pl.DeviceIdType.MESH))
        return local, remote

    def start(self, ins, outs, sems):
        local, remote = self._copies(ins, outs, sems)
        for cp in local + remote:
            cp.start()

    def wait(self, ins, outs, sems):
        local, remote = self._copies(ins, outs, sems)
        for cp in remote:
            cp.wait_recv()
        for cp in remote:
            cp.wait_send()
        for cp in local:
            cp.wait()

    def alone(self, name):
        n = self.n

        def body(*refs):
            ins, outs, sems = refs[:n], refs[n:2 * n], refs[2 * n:]
            self.start(ins, outs, sems)
            self.wait(ins, outs, sems)

        return pl.pallas_call(body, name=name, out_shape=self.out_shape, in_specs=self.any_specs,
                              out_specs=self.any_specs, scratch_shapes=self.sems)(*self.arrs)


def _exchange(arrs, scatter, name):
    return Exchange(arrs, scatter).alone(name)


def _call(body, name, grid, in_specs, out_specs, out_shape, scratch_shapes, semantics, args, ride=None):
    if ride is None:
        return pl.pallas_call(body, name=name, grid=grid, in_specs=in_specs, out_specs=out_specs, out_shape=out_shape,
                              scratch_shapes=scratch_shapes, compiler_params=_params(semantics))(*args)
    single = not isinstance(out_shape, (list, tuple))
    out_specs, out_shape = ([out_specs], [out_shape]) if single else (list(out_specs), list(out_shape))
    n_in, n_out, n_scr, n = len(in_specs), len(out_specs), len(scratch_shapes), ride.n

    def riding(*refs):
        ins, cins = refs[:n_in], refs[n_in:n_in + n]
        outs, couts = refs[n_in + n:n_in + n + n_out], refs[n_in + n + n_out:n_in + 2 * n + n_out]
        scr, sems = refs[n_in + 2 * n + n_out:n_in + 2 * n + n_out + n_scr], refs[n_in + 2 * n + n_out + n_scr:]
        first = functools.reduce(lambda p, q: p & q, [pl.program_id(a) == 0 for a in range(len(grid))])
        last = functools.reduce(lambda p, q: p & q, [pl.program_id(a) == g - 1 for a, g in enumerate(grid)])

        @pl.when(first)
        def _():
            ride.start(cins, couts, sems)

        body(*ins, *outs, *scr)

        @pl.when(last)
        def _():
            ride.wait(cins, couts, sems)

    res = pl.pallas_call(
        riding, name=name, grid=grid, in_specs=list(in_specs) + ride.any_specs, out_specs=out_specs + ride.any_specs,
        out_shape=out_shape + ride.out_shape, scratch_shapes=list(scratch_shapes) + ride.sems,
        compiler_params=_params(("arbitrary",) * len(grid)))(*args, *ride.arrs)
    own = res[0] if single else list(res[:n_out])
    return own, list(res[n_out:])


def _reduce_adamw(parts, w, m, v, l, prev, name):
    _, rows, n = parts.shape
    tr = _row_tile(rows, max(8, (1 << 18) // n))
    c1 = 1.0 - ADAM_B1 ** ADAM_STEP
    c2 = 1.0 - ADAM_B2 ** ADAM_STEP

    def body(p_ref, w_ref, m_ref, v_ref, *rest):
        g_ref, d_ref, nm_ref, nv_ref = rest[-4:]
        g = p_ref[0].astype(F32)
        for s in range(1, N_DEV):
            g = g + p_ref[s].astype(F32)
        nm = ADAM_B1 * m_ref[...] + (1.0 - ADAM_B1) * g
        nv = ADAM_B2 * v_ref[...] + (1.0 - ADAM_B2) * (g * g)
        m_hat = nm / c1
        v_hat = nv / c2
        g_ref[...] = g
        d_ref[...] = -ADAM_LR * (m_hat / (jnp.sqrt(v_hat) + ADAM_EPS) + ADAM_WD * w_ref[...])
        nm_ref[...] = nm
        nv_ref[...] = nv

    blk = pl.BlockSpec((None, tr, n), lambda i: (l, i, 0))
    out = jax.ShapeDtypeStruct(w.shape, F32)
    prev = list(prev) if prev is not None else []
    return pl.pallas_call(
        body, name=name, grid=(rows // tr,),
        in_specs=[pl.BlockSpec((N_DEV, tr, n), lambda i: (0, i, 0)), blk, blk, blk] + [pl.BlockSpec(memory_space=pl.ANY)] * len(prev),
        out_specs=[blk, blk, blk, blk], out_shape=[out, out, out, out],
        input_output_aliases={4 + k: k for k in range(len(prev))},
        compiler_params=_params(("parallel",)),
    )(parts, w, m, v, *prev)


def _inproj_fwd(x, g, w, tm, name):
    t, d = x.shape
    n = w.shape[1]

    def body(x_ref, g_ref, w_ref, ab_ref, qkv_ref):
        xx = x_ref[...]
        h = (xx * _rstd(xx) * g_ref[...]).astype(BF16)
        ab_ref[...] = _dot(h, w_ref[:, :AB_COLS])
        qkv_ref[...] = _dot(h, w_ref[:, AB_COLS:]).astype(BF16)

    return pl.pallas_call(
        body, name=name, grid=(t // tm,),
        in_specs=[pl.BlockSpec((tm, d), lambda i: (i, 0)), pl.BlockSpec((1, d), lambda i: (0, 0)),
                  pl.BlockSpec((d, n), lambda i: (0, 0))],
        out_specs=[pl.BlockSpec((tm, AB_COLS), lambda i: (i, 0)), pl.BlockSpec((tm, n - AB_COLS), lambda i: (i, 0))],
        out_shape=[jax.ShapeDtypeStruct((t, AB_COLS), F32), jax.ShapeDtypeStruct((t, n - AB_COLS), BF16)],
        compiler_params=_params(("parallel",)),
    )(x, g, w)


def _pool_window_sums(xx, forward):
    n = xx.shape[0]
    sh = (lambda k: n - k) if forward else (lambda k: k)
    s2 = xx + pltpu.roll(xx, sh(1), 0)
    s4 = s2 + pltpu.roll(s2, sh(2), 0)
    s8 = s4 + pltpu.roll(s4, sh(4), 0)
    s16 = s8 + pltpu.roll(s8, sh(8), 0)
    grp = lax.broadcasted_iota(jnp.int32, (1, POOL_WIDTH), 1) // POOL_GW
    return jnp.where(grp == 0, s2, jnp.where(grp == 1, s4, jnp.where(grp == 2, s8, s16)))


def _pool_count(t0, rows):
    grp = lax.broadcasted_iota(jnp.int32, (1, POOL_WIDTH), 1) // POOL_GW
    win = jnp.where(grp == 0, 2, jnp.where(grp == 1, 4, jnp.where(grp == 2, 8, 16)))
    tt = t0 + lax.broadcasted_iota(jnp.int32, (rows, 1), 0)
    return jnp.minimum(tt + 1, win).astype(F32)


def _pool_diff(cur, prev, t0):
    tm = cur.shape[0]
    sums = _pool_window_sums(jnp.concatenate([prev, cur], axis=0), False)[POOL_HALO:]
    return sums / _pool_count(t0, tm) - cur


def _pool_fwd(ab, wbd, scale, tm, name):
    t = ab.shape[0]
    hb = tm // POOL_HALO

    def body(cur_ref, prev_ref, w_ref, s_ref, y_ref):
        i = pl.program_id(0)
        prev = jnp.where(i == 0, 0.0, prev_ref[...])
        d = _pool_diff(cur_ref[...], prev, i * tm).astype(BF16)
        y_ref[...] = (_dot(d, w_ref[...]) * s_ref[...]).astype(BF16)

    return pl.pallas_call(
        body, name=name, grid=(t // tm,),
        in_specs=[pl.BlockSpec((tm, POOL_WIDTH), lambda i: (i, 0)),
                  pl.BlockSpec((POOL_HALO, POOL_WIDTH), lambda i: (jnp.maximum(i * hb - 1, 0), 0)),
                  pl.BlockSpec((POOL_WIDTH, POOL_WIDTH), lambda i: (0, 0)),
                  pl.BlockSpec((1, POOL_WIDTH), lambda i: (0, 0))],
        out_specs=pl.BlockSpec((tm, POOL_WIDTH), lambda i: (i, 0)),
        out_shape=jax.ShapeDtypeStruct((t, POOL_WIDTH), BF16),
        compiler_params=_params(("parallel",)),
    )(ab, ab, wbd, scale)


_GELU_K = 0.7978845608028654
_GELU_A = 0.044715


def _gelu(x):
    return 0.5 * x * (1.0 + jnp.tanh(_GELU_K * (x + _GELU_A * (x * x * x))))


def _gelu_grad(x):
    th = jnp.tanh(_GELU_K * (x + _GELU_A * (x * x * x)))
    return 0.5 * (1.0 + th) + 0.5 * x * (1.0 - th * th) * (_GELU_K * (1.0 + 3.0 * _GELU_A * (x * x)))


def _head_lanes(h):
    return lax.broadcasted_iota(jnp.int32, (1, SG_WIDTH), 1) // SG_HD == h


def _sg_fwd(ab, gn, wm, bfull, tm, name):
    t = ab.shape[0]

    def body(u_ref, v_ref, gn_ref, wm_ref, b_ref, y_ref):
        v = _gelu(v_ref[...])
        vn = (v * _rstd(v) * gn_ref[...]).astype(BF16)
        for c in range(tm // CHUNK):
            rows = slice(c * CHUNK, (c + 1) * CHUNK)
            vc = vn[rows]
            sv = b_ref[...]
            for h in range(SG_HEADS):
                sv = sv + jnp.where(_head_lanes(h), _dot(wm_ref[h], vc), 0.0)
            y_ref[rows, :] = (_gelu(u_ref[rows, :]) * sv).astype(BF16)

    return pl.pallas_call(
        body, name=name, grid=(t // tm,),
        in_specs=[pl.BlockSpec((tm, SG_WIDTH), lambda i: (i, 1)), pl.BlockSpec((tm, SG_WIDTH), lambda i: (i, 2)),
                  pl.BlockSpec((1, SG_WIDTH), lambda i: (0, 0)),
                  pl.BlockSpec((SG_HEADS, CHUNK, CHUNK), lambda i: (0, 0, 0)),
                  pl.BlockSpec((CHUNK, SG_WIDTH), lambda i: (0, 0))],
        out_specs=pl.BlockSpec((tm, SG_WIDTH), lambda i: (i, 0)),
        out_shape=jax.ShapeDtypeStruct((t, SG_WIDTH), BF16),
        compiler_params=_params(("parallel",)),
    )(ab, ab, gn, wm, bfull)


LOG2E = 1.4426950408889634
SB_SCALE = 0.125 * LOG2E
SB_DEAD_LOG2 = 152.0


def _log2_sigmoids(y):
    neg_abs = lax.bitcast_convert_type(lax.bitcast_convert_type(y, jnp.uint32) | jnp.uint32(0x80000000), F32)
    lb = jnp.minimum(y, 0.0) - jnp.log(1.0 + jnp.exp2(neg_abs)) * LOG2E
    return lb, lb - y


def _split(x):
    hi = x.astype(BF16)
    return hi, (x - hi.astype(F32)).astype(BF16)


def _tri_dot(tri, x):
    hi, lo = _split(x)
    return _dot(tri, hi) + _dot(tri, lo)


def _sba_fwd(qkv, vtb, tq, name, ride=None):
    t = qkv.shape[0]
    nb = t // tq
    upper = (jnp.arange(tq)[None, :] > jnp.arange(tq)[:, None]).astype(BF16)

    def body(q_ref, k_ref, vt_ref, up_ref, ot_ref, c_ref, n_ref):
        i = pl.program_id(1)
        q = q_ref[...]
        up = up_ref[...]
        lane_head = lax.broadcasted_iota(jnp.int32, (1, 128), 1) // SB_HD
        sub_head = lax.broadcasted_iota(jnp.int32, (128, 1), 0) // SB_HD
        causal = (lax.broadcasted_iota(jnp.int32, (tq, tq), 0) < lax.broadcasted_iota(jnp.int32, (tq, tq), 1))
        qh = [jnp.where(lane_head == h, q, jnp.zeros_like(q)) for h in range(2)]

        def block(j, carry, diag):
            cs, acc = list(carry[:2]), carry[2]
            kj = k_ref[pl.ds(pl.multiple_of(j * tq, tq), tq), :]
            vt = vt_ref[j]
            hs = range(2)
            z = [_dot_nt(kj, qh[h]) for h in hs]
            ls = [_log2_sigmoids(z[h] * SB_SCALE) for h in hs]
            lb = [ls[h][0] for h in hs]
            l1 = [jnp.where(causal, ls[h][1], 0.0) if diag else ls[h][1] for h in hs]
            after = [_tri_dot(up, l1[h]) for h in hs]
            a = [jnp.exp2(lb[h] + after[h] + cs[h]) for h in hs]
            if diag:
                a = [jnp.where(causal, a[h], 0.0) for h in hs]
            for h in hs:
                c_ref[h, j] = cs[h]
                acc = acc + _dot(jnp.where(sub_head == h, vt, jnp.zeros_like(vt)), a[h].astype(BF16))
                cs[h] = cs[h] + after[h][0:1, :] + l1[h][0:1, :]
            return cs[0], cs[1], acc

        def alive(c0, c1):
            return (jnp.max(jnp.maximum(c0, c1)) > -SB_DEAD_LOG2).astype(jnp.int32)

        def step(state):
            s, _, c0, c1, acc = state
            c0, c1, acc = block(i - 1 - s, (c0, c1, acc), False)
            return s + 1, alive(c0, c1), c0, c1, acc

        zero = jnp.zeros((1, tq), F32)
        c0, c1, acc = block(i, (zero, zero, jnp.zeros((128, tq), F32)), True)
        state = lax.while_loop(lambda st: (st[0] < i) & (st[1] > 0), step, (jnp.int32(0), alive(c0, c1), c0, c1, acc))
        ot_ref[...] = state[4].astype(BF16)
        n_ref[pl.program_id(0), i] = (state[0] + 1).astype(F32)

    return _call(
        body, name, (SB_PAIRS, nb),
        [pl.BlockSpec((tq, 128), lambda p, i: (i, p)),
         pl.BlockSpec((t, 128), lambda p, i: (0, SB_PAIRS + p)),
         pl.BlockSpec((None, nb, 128, tq), lambda p, i: (p, 0, 0, 0)),
         pl.BlockSpec((tq, tq), lambda p, i: (0, 0))],
        [pl.BlockSpec((128, tq), lambda p, i: (p, i)),
         pl.BlockSpec((2, nb, 1, tq), lambda p, i: (p, 0, 0, i)),
         pl.BlockSpec(memory_space=pltpu.SMEM)],
        [jax.ShapeDtypeStruct((SB_WIDTH, t), BF16),
         jax.ShapeDtypeStruct((2 * SB_PAIRS, nb, 1, t), F32),
         jax.ShapeDtypeStruct((SB_PAIRS, nb), F32)],
        [], ("arbitrary", "arbitrary"), (qkv, qkv, vtb, upper), ride)


def _outproj_mlp_fwd(x, ya, yb, yc, wo, g2, wup, wdn, tm, name, ride=None):
    t, d = x.shape
    nf = wup.shape[0]

    def body(x_ref, ya_ref, yb_ref, yc_ref, wo_ref, g_ref, wu_ref, wd_ref, x1_ref, u_ref, x2_ref, h_ref, acc_ref):
        j = pl.program_id(1)

        @pl.when(j == 0)
        def _():
            x1 = (x_ref[...] + _dot(ya_ref[...], wo_ref[0:POOL_WIDTH, :])
                  + _dot(yb_ref[...], wo_ref[POOL_WIDTH:POOL_WIDTH + SG_WIDTH, :])
                  + _dot(yc_ref[...], wo_ref[POOL_WIDTH + SG_WIDTH:, :]))
            x1_ref[...] = x1
            h_ref[...] = (x1 * _rstd(x1) * g_ref[...]).astype(BF16)
            acc_ref[...] = x1

        u = _dot(h_ref[...], wu_ref[...])
        u_ref[...] = u
        r = jnp.maximum(u, 0.0)
        acc_ref[...] += _dot((r * r).astype(BF16), wd_ref[...])

        @pl.when(j == nf - 1)
        def _():
            x2_ref[...] = acc_ref[...]

    row = lambda w: pl.BlockSpec((tm, w), lambda i, j: (i, 0))
    return _call(
        body, name, (t // tm, nf),
        [row(d), row(POOL_WIDTH), row(SG_WIDTH), row(SB_WIDTH),
         pl.BlockSpec((d, d), lambda i, j: (0, 0)), pl.BlockSpec((1, d), lambda i, j: (0, 0)),
         pl.BlockSpec((None, d, FF_SHARD), lambda i, j: (j, 0, 0)),
         pl.BlockSpec((None, FF_SHARD, d), lambda i, j: (j, 0, 0))],
        [row(d), pl.BlockSpec((tm, FF_SHARD), lambda i, j: (i, j)), row(d)],
        [jax.ShapeDtypeStruct((t, d), F32), jax.ShapeDtypeStruct((t, nf * FF_SHARD), F32),
         jax.ShapeDtypeStruct((t, d), F32)],
        [pltpu.VMEM((tm, d), BF16), pltpu.VMEM((tm, d), F32)],
        ("parallel", "arbitrary"), (x, ya, yb, yc, wo, g2, wup, wdn), ride)


def _loss_grad(x, g, target, tm, name):
    t, d = x.shape
    nt = t // tm

    def body(x_ref, g_ref, t_ref, loss_ref, dx_ref, dg_ref, sq_ref):
        i = pl.program_id(0)

        @pl.when(i == 0)
        def _():
            sq_ref[...] = jnp.zeros_like(sq_ref)
            dg_ref[...] = jnp.zeros_like(dg_ref)

        xx = x_ref[...]
        r = _rstd(xx)
        err = xx * r * g_ref[...] - t_ref[...]
        sq_ref[...] += jnp.sum(err * err, axis=0, keepdims=True)
        dx, dg = _rms_bwd(xx, r, g_ref[...], err * (1.0 / d))
        dx_ref[...] = dx
        dg_ref[...] += dg

        @pl.when(i == nt - 1)
        def _():
            loss_ref[...] = jnp.sum(sq_ref[...], axis=1, keepdims=True) * (0.5 / d)

    return pl.pallas_call(
        body, name=name, grid=(nt,),
        in_specs=[pl.BlockSpec((tm, d), lambda i: (i, 0)), pl.BlockSpec((1, d), lambda i: (0, 0)),
                  pl.BlockSpec((tm, d), lambda i: (i, 0))],
        out_specs=[pl.BlockSpec((1, 1), lambda i: (0, 0)), pl.BlockSpec((tm, d), lambda i: (i, 0)),
                   pl.BlockSpec((1, d), lambda i: (0, 0))],
        out_shape=[jax.ShapeDtypeStruct((1, 1), F32), jax.ShapeDtypeStruct((t, d), F32),
                   jax.ShapeDtypeStruct((1, d), F32)],
        scratch_shapes=[pltpu.VMEM((1, d), F32)],
        compiler_params=_params(("arbitrary",)),
    )(x, g, target)


def _mlp_bwd(dx2, x1, g2, u, wup, wdn, tm, name, ride=None):
    t, d = dx2.shape
    nf = wup.shape[0]
    nt = t // tm

    def body(dx2_ref, x1_ref, g_ref, u_ref, wu_ref, wd_ref, dx1_ref, du_ref, r_ref, h_ref, dxb_ref, dg_ref, acc_ref):
        i, j = pl.program_id(0), pl.program_id(1)

        @pl.when(j == 0)
        def _():
            x1 = x1_ref[...]
            h_ref[...] = (x1 * _rstd(x1) * g_ref[...]).astype(BF16)
            dxb_ref[...] = dx2_ref[...].astype(BF16)
            acc_ref[...] = jnp.zeros_like(acc_ref)

        @pl.when((i == 0) & (j == 0))
        def _():
            dg_ref[...] = jnp.zeros_like(dg_ref)

        dr = _dot_nt(dxb_ref[...], wd_ref[...])
        ru = jnp.maximum(u_ref[...], 0.0)
        du = (dr * (2.0 * ru)).astype(BF16)
        du_ref[...] = du
        r_ref[...] = (ru * ru).astype(BF16)
        acc_ref[...] += _dot_nt(du, wu_ref[...])

        @pl.when(j == nf - 1)
        def _():
            x1 = x1_ref[...]
            dx, dg = _rms_bwd(x1, _rstd(x1), g_ref[...], acc_ref[...])
            dx1_ref[...] = dx2_ref[...] + dx
            dg_ref[...] += dg

    row = lambda w: pl.BlockSpec((tm, w), lambda i, j: (i, 0))
    col = pl.BlockSpec((tm, FF_SHARD), lambda i, j: (i, j))
    return _call(
        body, name, (nt, nf),
        [row(d), row(d), pl.BlockSpec((1, d), lambda i, j: (0, 0)), col,
         pl.BlockSpec((None, d, FF_SHARD), lambda i, j: (j, 0, 0)),
         pl.BlockSpec((None, FF_SHARD, d), lambda i, j: (j, 0, 0))],
        [row(d), col, col, row(d), row(d), pl.BlockSpec((1, d), lambda i, j: (0, 0))],
        [jax.ShapeDtypeStruct((t, d), F32), jax.ShapeDtypeStruct((t, nf * FF_SHARD), BF16),
         jax.ShapeDtypeStruct((t, nf * FF_SHARD), BF16), jax.ShapeDtypeStruct((t, d), BF16),
         jax.ShapeDtypeStruct((t, d), BF16), jax.ShapeDtypeStruct((1, d), F32)],
        [pltpu.VMEM((tm, d), F32)], ("arbitrary", "arbitrary"), (dx2, x1, g2, u, wup, wdn), ride)


def _matmul_tn(a, b, bm, bn, bt, name, by_column_block=False):
    t, m = a.shape
    n = b.shape[1]
    nk = t // bt

    def body(a_ref, b_ref, o_ref, acc_ref):
        k = pl.program_id(2)

        @pl.when(k == 0)
        def _():
            acc_ref[...] = jnp.zeros_like(acc_ref)

        acc_ref[...] += _dot_tn(a_ref[...], b_ref[...])

        @pl.when(k == nk - 1)
        def _():
            o_ref[...] = acc_ref[...].astype(BF16)

    if by_column_block:
        out_spec = pl.BlockSpec((None, bm, bn), lambda i, j, k: (j, i, 0))
        out_shape = jax.ShapeDtypeStruct((n // bn, m, bn), BF16)
    else:
        out_spec = pl.BlockSpec((bm, bn), lambda i, j, k: (i, j))
        out_shape = jax.ShapeDtypeStruct((m, n), BF16)
    return pl.pallas_call(
        body, name=name, grid=(m // bm, n // bn, nk),
        in_specs=[pl.BlockSpec((bt, bm), lambda i, j, k: (k, i)), pl.BlockSpec((bt, bn), lambda i, j, k: (k, j))],
        out_specs=out_spec, out_shape=out_shape,
        scratch_shapes=[pltpu.VMEM((bm, bn), F32)],
        compiler_params=_params(("parallel", "parallel", "arbitrary")),
    )(a, b)


def _outproj_bwd(dx1, wo, tm, name):
    t, d = dx1.shape
    c2 = POOL_WIDTH + SG_WIDTH

    def body(dx_ref, wo_ref, dya_ref, dyb_ref, dyc_ref, dyct_ref, dxb_ref):
        dxb = dx_ref[...].astype(BF16)
        dxb_ref[...] = dxb
        dya_ref[...] = _dot_nt(dxb, wo_ref[0:POOL_WIDTH, :])
        dyb_ref[...] = _dot_nt(dxb, wo_ref[POOL_WIDTH:c2, :])
        dyc_ref[...] = _dot_nt(dxb, wo_ref[c2:, :]).astype(BF16)
        dyct_ref[...] = _dot_nt(wo_ref[c2:, :], dxb).astype(BF16)

    row = lambda w: pl.BlockSpec((tm, w), lambda i: (i, 0))
    return pl.pallas_call(
        body, name=name, grid=(t // tm,),
        in_specs=[row(d), pl.BlockSpec((d, d), lambda i: (0, 0))],
        out_specs=[row(POOL_WIDTH), row(SG_WIDTH), row(SB_WIDTH), pl.BlockSpec((SB_WIDTH, tm), lambda i: (0, i)), row(d)],
        out_shape=[jax.ShapeDtypeStruct((t, POOL_WIDTH), F32), jax.ShapeDtypeStruct((t, SG_WIDTH), F32),
                   jax.ShapeDtypeStruct((t, SB_WIDTH), BF16), jax.ShapeDtypeStruct((SB_WIDTH, t), BF16),
                   jax.ShapeDtypeStruct((t, d), BF16)],
        compiler_params=_params(("parallel",)),
    )(dx1, wo)


def _sba_bwd(qkv, ktb, dyc, dyct, cc, nvis, tq, name, ride=None):
    t = qkv.shape[0]
    nb = t // tq
    idx = jnp.arange(tq)
    upper = (idx[None, :] > idx[:, None]).astype(BF16)
    lower = (idx[None, :] < idx[:, None]).astype(BF16)

    def body(q_ref, k_ref, v_ref, kt_ref, do_ref, dot_ref, c_ref, n_ref, up_ref, lo_ref, dqt_ref, dk_ref, dv_ref):
        i = pl.program_id(1)

        @pl.when(i == 0)
        def _():
            dk_ref[...] = jnp.zeros_like(dk_ref)
            dv_ref[...] = jnp.zeros_like(dv_ref)

        q = q_ref[...]
        do = do_ref[...]
        dot = dot_ref[...]
        up = up_ref[...]
        lo = lo_ref[...]
        lane_head = lax.broadcasted_iota(jnp.int32, (1, 128), 1) // SB_HD
        sub_head = lax.broadcasted_iota(jnp.int32, (128, 1), 0) // SB_HD
        causal = (lax.broadcasted_iota(jnp.int32, (tq, tq), 0) < lax.broadcasted_iota(jnp.int32, (tq, tq), 1))
        hms = [lane_head == h for h in range(2)]
        qh = [jnp.where(hm, q, jnp.zeros_like(q)) for hm in hms]
        qs = [x * 0.125 for x in qh]
        doh = [jnp.where(hm, do, jnp.zeros_like(do)) for hm in hms]

        def block(j, carry, diag):
            cgs, dqt = list(carry[:2]), carry[2]
            rows = pl.ds(pl.multiple_of(j * tq, tq), tq)
            kj = k_ref[rows, :]
            vj = v_ref[rows, :]
            kt = kt_ref[j]
            hs = range(2)
            z = [_dot_nt(kj, qh[h]) for h in hs]
            da = [_dot(jnp.where(hms[h], vj, jnp.zeros_like(vj)), dot) for h in hs]
            ls = [_log2_sigmoids(z[h] * SB_SCALE) for h in hs]
            lb = [ls[h][0] for h in hs]
            l1 = [jnp.where(causal, ls[h][1], 0.0) if diag else ls[h][1] for h in hs]
            after = [_tri_dot(up, l1[h]) for h in hs]
            a = [jnp.exp2(lb[h] + after[h] + c_ref[h, j]) for h in hs]
            if diag:
                a = [jnp.where(causal, a[h], 0.0) for h in hs]
            g = [a[h] * da[h] for h in hs]
            gsum = [_tri_dot(lo, g[h]) + cgs[h] for h in hs]
            dz = [g[h] - jnp.exp2(lb[h]) * (g[h] + gsum[h]) for h in hs]
            if diag:
                dz = [jnp.where(causal, dz[h], 0.0) for h in hs]
            dzb = [dz[h].astype(BF16) for h in hs]
            ab = [a[h].astype(BF16) for h in hs]
            for h in hs:
                dqt = dqt + _dot(jnp.where(sub_head == h, kt, jnp.zeros_like(kt)), dzb[h])
            dk_ref[rows, :] += _dot(dzb[0], qs[0]) + _dot(dzb[1], qs[1])
            dv_ref[rows, :] += _dot(ab[0], doh[0]) + _dot(ab[1], doh[1])
            return (gsum[0][tq - 1:tq, :] + g[0][tq - 1:tq, :], gsum[1][tq - 1:tq, :] + g[1][tq - 1:tq, :], dqt)

        n = jnp.clip(n_ref[pl.program_id(0), i].astype(jnp.int32), 1, i + 1)
        zero = jnp.zeros((1, tq), F32)
        carry = lax.fori_loop(i + 1 - n, i, lambda s, cr: block(s, cr, False), (zero, zero, jnp.zeros((128, tq), F32)))
        dqt_ref[...] = block(i, carry, True)[2] * 0.125

    return _call(
        body, name, (SB_PAIRS, nb),
        [pl.BlockSpec((tq, 128), lambda p, i: (i, p)),
         pl.BlockSpec((t, 128), lambda p, i: (0, SB_PAIRS + p)),
         pl.BlockSpec((t, 128), lambda p, i: (0, 2 * SB_PAIRS + p)),
         pl.BlockSpec((None, nb, 128, tq), lambda p, i: (p, 0, 0, 0)),
         pl.BlockSpec((tq, 128), lambda p, i: (i, p)),
         pl.BlockSpec((128, tq), lambda p, i: (p, i)),
         pl.BlockSpec((2, nb, 1, tq), lambda p, i: (p, 0, 0, i)),
         pl.BlockSpec(memory_space=pltpu.SMEM),
         pl.BlockSpec((tq, tq), lambda p, i: (0, 0)),
         pl.BlockSpec((tq, tq), lambda p, i: (0, 0))],
        [pl.BlockSpec((128, tq), lambda p, i: (p, i)),
         pl.BlockSpec((t, 128), lambda p, i: (0, p)),
         pl.BlockSpec((t, 128), lambda p, i: (0, p))],
        [jax.ShapeDtypeStruct((SB_WIDTH, t), F32), jax.ShapeDtypeStruct((t, SB_WIDTH), F32),
         jax.ShapeDtypeStruct((t, SB_WIDTH), F32)],
        [], ("arbitrary", "arbitrary"), (qkv, qkv, qkv, ktb, dyc, dyct, cc, nvis, upper, lower), ride)


def _sg_bwd(ab, dyb, gn, wm, wmt, bfull, tm, name):
    t = ab.shape[0]
    nt = t // tm
    sel = (jnp.arange(SG_WIDTH)[:, None] // SG_HD == jnp.arange(CHUNK)[None, :]).astype(F32)

    def body(u_ref, v_ref, dy_ref, gn_ref, wm_ref, wmt_ref, b_ref, sel_ref,
             dup_ref, dvp_ref, dgn_ref, dw_ref, db_ref, dbacc_ref):
        i = pl.program_id(0)

        @pl.when(i == 0)
        def _():
            dgn_ref[...] = jnp.zeros_like(dgn_ref)
            dw_ref[...] = jnp.zeros_like(dw_ref)
            dbacc_ref[...] = jnp.zeros_like(dbacc_ref)

        tril = (lax.broadcasted_iota(jnp.int32, (CHUNK, CHUNK), 0) >= lax.broadcasted_iota(jnp.int32, (CHUNK, CHUNK), 1))
        gn_ = gn_ref[...]
        for c in range(tm // CHUNK):
            rows = slice(c * CHUNK, (c + 1) * CHUNK)
            up, vp, dy = u_ref[rows, :], v_ref[rows, :], dy_ref[rows, :]
            u, v = _gelu(up), _gelu(vp)
            r = _rstd(v)
            vn = (v * r * gn_).astype(BF16)
            sv = b_ref[...]
            for h in range(SG_HEADS):
                sv = sv + jnp.where(_head_lanes(h), _dot(wm_ref[h], vn), 0.0)
            dup_ref[rows, :] = dy * sv * _gelu_grad(up)
            dsv = dy * u
            dbacc_ref[...] += dsv
            dvn = jnp.zeros((CHUNK, SG_WIDTH), F32)
            for h in range(SG_HEADS):
                dsv_h = jnp.where(_head_lanes(h), dsv, 0.0).astype(BF16)
                dvn = dvn + _dot(wmt_ref[h], dsv_h)
                dw_ref[h] += jnp.where(tril, _dot_nt(dsv_h, vn), 0.0)
            dv, dgn = _rms_bwd(v, r, gn_, dvn)
            dgn_ref[...] += dgn
            dvp_ref[rows, :] = dv * _gelu_grad(vp)

        @pl.when(i == nt - 1)
        def _():
            db_ref[...] = jnp.dot(dbacc_ref[...], sel_ref[...], preferred_element_type=F32,
                                  precision=lax.Precision.HIGHEST)

    const = lambda shape: pl.BlockSpec(shape, lambda i: tuple(0 for _ in shape))
    return pl.pallas_call(
        body, name=name, grid=(nt,),
        in_specs=[pl.BlockSpec((tm, SG_WIDTH), lambda i: (i, 1)), pl.BlockSpec((tm, SG_WIDTH), lambda i: (i, 2)),
                  pl.BlockSpec((tm, SG_WIDTH), lambda i: (i, 0)), const((1, SG_WIDTH)),
                  const((SG_HEADS, CHUNK, CHUNK)), const((SG_HEADS, CHUNK, CHUNK)), const((CHUNK, SG_WIDTH)),
                  const((SG_WIDTH, CHUNK))],
        out_specs=[pl.BlockSpec((tm, SG_WIDTH), lambda i: (i, 0)), pl.BlockSpec((tm, SG_WIDTH), lambda i: (i, 0)),
                   const((1, SG_WIDTH)), const((SG_HEADS, CHUNK, CHUNK)), const((CHUNK, CHUNK))],
        out_shape=[jax.ShapeDtypeStruct((t, SG_WIDTH), F32), jax.ShapeDtypeStruct((t, SG_WIDTH), F32),
                   jax.ShapeDtypeStruct((1, SG_WIDTH), F32), jax.ShapeDtypeStruct((SG_HEADS, CHUNK, CHUNK), F32),
                   jax.ShapeDtypeStruct((CHUNK, CHUNK), F32)],
        scratch_shapes=[pltpu.VMEM((CHUNK, SG_WIDTH), F32)],
        compiler_params=_params(("arbitrary",)),
    )(ab, ab, dyb, gn, wm, wmt, bfull, sel)


def _pool_bwd(ab, dya, wbd, scale, tm, name):
    t = ab.shape[0]
    nt = t // tm
    hb = tm // POOL_HALO
    nh = t // POOL_HALO

    def body(cur_ref, prev_ref, dy_ref, dyn_ref, w_ref, s_ref, da_ref, dw_ref, ds_ref):
        i = pl.program_id(0)

        @pl.when(i == 0)
        def _():
            dw_ref[...] = jnp.zeros_like(dw_ref)
            ds_ref[...] = jnp.zeros_like(ds_ref)

        prev = jnp.where(i == 0, 0.0, prev_ref[...])
        d = _pool_diff(cur_ref[...], prev, i * tm).astype(BF16)
        dy = dy_ref[...]
        ds_ref[...] += jnp.sum(dy * _dot(d, w_ref[...]), axis=0, keepdims=True)
        dyn = jnp.where(i == nt - 1, 0.0, dyn_ref[...])
        dys = (jnp.concatenate([dy, dyn], axis=0) * s_ref[...]).astype(BF16)
        dw_ref[...] += _dot_tn(d, dys[:tm])
        dd = _dot_nt(dys, w_ref[...])
        fwd = _pool_window_sums(dd / _pool_count(i * tm, tm + POOL_HALO), True)
        da_ref[...] = fwd[:tm] - dd[:tm]

    return pl.pallas_call(
        body, name=name, grid=(nt,),
        in_specs=[pl.BlockSpec((tm, POOL_WIDTH), lambda i: (i, 0)),
                  pl.BlockSpec((POOL_HALO, POOL_WIDTH), lambda i: (jnp.maximum(i * hb - 1, 0), 0)),
                  pl.BlockSpec((tm, POOL_WIDTH), lambda i: (i, 0)),
                  pl.BlockSpec((POOL_HALO, POOL_WIDTH), lambda i: (jnp.minimum((i + 1) * hb, nh - 1), 0)),
                  pl.BlockSpec((POOL_WIDTH, POOL_WIDTH), lambda i: (0, 0)),
                  pl.BlockSpec((1, POOL_WIDTH), lambda i: (0, 0))],
        out_specs=[pl.BlockSpec((tm, POOL_WIDTH), lambda i: (i, 0)),
                   pl.BlockSpec((POOL_WIDTH, POOL_WIDTH), lambda i: (0, 0)),
                   pl.BlockSpec((1, POOL_WIDTH), lambda i: (0, 0))],
        out_shape=[jax.ShapeDtypeStruct((t, POOL_WIDTH), F32), jax.ShapeDtypeStruct((POOL_WIDTH, POOL_WIDTH), F32),
                   jax.ShapeDtypeStruct((1, POOL_WIDTH), F32)],
        compiler_params=_params(("arbitrary",)),
    )(ab, ab, dya, dya, wbd, scale)


def _inproj_bwd(dx1, x, g, da, dup, dvp, dq, dk, dv, w, tm, name):
    t, d = x.shape
    n = w.shape[1]
    nt = t // tm

    def body(dx1_ref, x_ref, g_ref, da_ref, du_ref, dv_ref, dq_ref, dk_ref, dvv_ref, w_ref,
             dx_ref, h_ref, dp_ref, dg_ref):
        @pl.when(pl.program_id(0) == 0)
        def _():
            dg_ref[...] = jnp.zeros_like(dg_ref)

        dp = jnp.concatenate([da_ref[...], du_ref[...], dv_ref[...], dq_ref[...], dk_ref[...], dvv_ref[...]],
                             axis=1).astype(BF16)
        dp_ref[...] = dp
        xx = x_ref[...]
        r = _rstd(xx)
        h_ref[...] = (xx * r * g_ref[...]).astype(BF16)
        dx, dg = _rms_bwd(xx, r, g_ref[...], _dot_nt(dp, w_ref[...]))
        dx_ref[...] = dx1_ref[...] + dx
        dg_ref[...] += dg

    row = lambda w_: pl.BlockSpec((tm, w_), lambda i: (i, 0))
    return pl.pallas_call(
        body, name=name, grid=(nt,),
        in_specs=[row(d), row(d), pl.BlockSpec((1, d), lambda i: (0, 0)), row(POOL_WIDTH), row(SG_WIDTH),
                  row(SG_WIDTH), row(SB_WIDTH), row(SB_WIDTH), row(SB_WIDTH), pl.BlockSpec((d, n), lambda i: (0, 0))],
        out_specs=[row(d), row(d), row(n), pl.BlockSpec((1, d), lambda i: (0, 0))],
        out_shape=[jax.ShapeDtypeStruct((t, d), F32), jax.ShapeDtypeStruct((t, d), BF16),
                   jax.ShapeDtypeStruct((t, n), BF16), jax.ShapeDtypeStruct((1, d), F32)],
        compiler_params=_params(("arbitrary",)),
    )(dx1, x, g, da, dup, dvp, dq, dk, dv, w)


def _full_w_in(gathered):
    return gathered.transpose(1, 0, 2).reshape(D_MODEL, IN_COLS)


_SMALL_SHAPES = ((DEPTH, D_MODEL), (DEPTH, 4, POOL_GW, POOL_GW), (DEPTH, POOL_WIDTH), (DEPTH, SG_WIDTH),
                 (DEPTH, SG_HEADS, CHUNK, CHUNK), (DEPTH, SG_HEADS, CHUNK), (DEPTH, D_MODEL), (D_MODEL,))


def _pack_small(arrs):
    return jnp.concatenate([a.reshape(-1) for a in arrs]).reshape(-1, 128)


def _unpack_small(buf):
    flat = buf.reshape(-1)
    outs, off = [], 0
    for shp in _SMALL_SHAPES:
        size = 1
        for s in shp:
            size *= s
        outs.append(flat[off:off + size].reshape(shp))
        off += size
    return outs


def _tiles(t):
    return min(512, t), min(256, t // 4), min(2048, t)


def _layer_fwd(xl, wi, rest, small_w, l, ride_attn=None, ride_mlp=None):
    n1, pw, ps, sn, sw, sb, n2 = small_w
    tm, tq, _ = _tiles(xl.shape[0])
    nb = xl.shape[0] // tq
    wbd = jnp.zeros((4, POOL_GW, 4, POOL_GW), F32)
    for gi in range(4):
        wbd = wbd.at[gi, :, gi, :].set(pw[gi])
    wbd = wbd.reshape(POOL_WIDTH, POOL_WIDTH).astype(BF16)
    wm = (sw * jnp.tril(jnp.ones((CHUNK, CHUNK), F32))).astype(BF16)
    wmt = wm.transpose(0, 2, 1)
    bfull = jnp.repeat(sb.T, SG_HD, axis=1)
    g1, g2, psc, gn = n1[None, :], n2[None, :], ps[None, :], sn[None, :]

    ab, qkv = _inproj_fwd(xl, g1, wi, tm, f"inproj_fwd{l}")
    ya = _pool_fwd(ab, wbd, psc, tm, f"pool_fwd{l}")
    yb = _sg_fwd(ab, gn, wm, bfull, tm, f"sg_fwd{l}")
    vtb = qkv[:, 2 * SB_WIDTH:].reshape(nb, tq, SB_PAIRS, 128).transpose(2, 0, 3, 1)
    if rest is None:
        (oct_, cc, nvis), (wo, wu, wd) = _sba_fwd(qkv, vtb, tq, f"sba_fwd{l}", ride_attn)
    else:
        oct_, cc, nvis = _sba_fwd(qkv, vtb, tq, f"sba_fwd{l}")
        wo, wu, wd = rest
    wo = wo.reshape(D_MODEL, D_MODEL)
    yc = oct_.T
    res = _outproj_mlp_fwd(xl, ya, yb, yc, wo, g2, wu, wd, tm, f"mlp_fwd{l}", ride_mlp)
    (x1, u, x2), rode = res if ride_mlp is not None else (res, None)
    saved = dict(x=xl, ab=ab, qkv=qkv, cc=cc, nvis=nvis, ya=ya, yb=yb, yc=yc, x1=x1, u=u, wi=wi, wo=wo, wu=wu, wd=wd,
                 wbd=wbd, wm=wm, wmt=wmt, bfull=bfull, g1=g1, g2=g2, psc=psc, gn=gn)
    return x2, saved, rode


def _layer_bwd(dx, s, l, ride_mlp=None, scatter_in_attn=False):
    tm, tq, tw = _tiles(dx.shape[0])
    nb = dx.shape[0] // tq
    res = _mlp_bwd(dx, s["x1"], s["g2"], s["u"], s["wu"], s["wd"], tm, f"mlp_bwd{l}", ride_mlp)
    (dx1, du, r, h2, dx2b, dn2), rode = res if ride_mlp is not None else (res, None)
    dw_up = _matmul_tn(h2, du, D_MODEL, FF_SHARD, tw, f"dw_up{l}", by_column_block=True)
    dw_down = _matmul_tn(r, dx2b, 1024, D_MODEL, tw, f"dw_down{l}").reshape(N_DEV, FF_SHARD, D_MODEL)
    dya, dyb, dyc, dyct, dx1b = _outproj_bwd(dx1, s["wo"], tm, f"outproj_bwd{l}")
    mix = jnp.concatenate([s["ya"], s["yb"], s["yc"]], axis=1)
    dw_out = _matmul_tn(mix, dx1b, D_MODEL, D_MODEL, tw, f"dw_out{l}").reshape(N_DEV, OUT_SHARD, D_MODEL)
    ktb = s["qkv"][:, SB_WIDTH:2 * SB_WIDTH].reshape(nb, tq, SB_PAIRS, 128).transpose(2, 0, 3, 1)
    if scatter_in_attn:
        (dqt, dk, dv), (dw_out, dw_up, dw_down) = _sba_bwd(s["qkv"], ktb, dyc, dyct, s["cc"], s["nvis"], tq, f"sba_bwd{l}",
                                                           Exchange([dw_out, dw_up, dw_down], True))
    else:
        dqt, dk, dv = _sba_bwd(s["qkv"], ktb, dyc, dyct, s["cc"], s["nvis"], tq, f"sba_bwd{l}")
    dup, dvp, dgn, dwm, dbm = _sg_bwd(s["ab"], dyb, s["gn"], s["wm"], s["wmt"], s["bfull"], tm, f"sg_bwd{l}")
    da, dwbd, dpsc = _pool_bwd(s["ab"], dya, s["wbd"], s["psc"], tm, f"pool_bwd{l}")
    dx, h1, dproj, dn1 = _inproj_bwd(dx1, s["x"], s["g1"], da, dup, dvp, dqt.T, dk, dv, s["wi"], tm, f"inproj_bwd{l}")
    dw_in = _matmul_tn(h1, dproj, D_MODEL, IN_COLS // 3, tw, f"dw_in{l}")
    dw_in = dw_in.reshape(D_MODEL, N_DEV, IN_SHARD).transpose(1, 0, 2)
    dpw = jnp.stack([dwbd[gi * POOL_GW:(gi + 1) * POOL_GW, gi * POOL_GW:(gi + 1) * POOL_GW] for gi in range(4)])
    small = (dn1[0], dpw, dpsc[0], dgn[0], dwm, dbm[:, :SG_HEADS].T, dn2[0])
    return dx, (dw_in, dw_out, dw_up, dw_down), small, rode


def kernel(x, norm1, w_in, pool_w, pool_scale, sg_norm, sg_w, sg_b, w_out, norm2, w_up, w_down, final_norm, loss_target, m_norm1, m_w_in, m_pool_w, m_pool_scale, m_sg_norm, m_sg_w, m_sg_b, m_w_out, m_norm2, m_w_up, m_w_down, m_final_norm, v_norm1, v_w_in, v_pool_w, v_pool_scale, v_sg_norm, v_sg_w, v_sg_b, v_w_out, v_norm2, v_w_up, v_w_down, v_final_norm):
    t = x.shape[1]
    tm = _tiles(t)[0]
    small_w = (norm1, pool_w, pool_scale, sg_norm, sg_w, sg_b, norm2)
    big_w = (w_in, w_out, w_up, w_down)
    big_m = (m_w_in, m_w_out, m_w_up, m_w_down)
    big_v = (v_w_in, v_w_out, v_w_up, v_w_down)
    shards = [[w[l].astype(BF16) for w in big_w] for l in range(DEPTH)]

    wi0 = _full_w_in(_exchange(shards[0][:1], False, "gather_w_in0")[0])
    x1, s0, g1 = _layer_fwd(x.reshape(t, D_MODEL), wi0, None, tuple(w[0] for w in small_w), 0,
                            ride_attn=Exchange(shards[0][1:], False), ride_mlp=Exchange(shards[1], False))
    x2, s1, _ = _layer_fwd(x1, _full_w_in(g1[0]), tuple(g1[1:]), tuple(w[1] for w in small_w), 1)
    loss_local, dx, dfinal = _loss_grad(x2, final_norm[None, :], loss_target.reshape(t, D_MODEL), tm, "loss_grad")
    loss = lax.psum(loss_local[0, 0], MESH_AXES)

    small_g = [None] * DEPTH
    dx, parts1, small_g[1], _ = _layer_bwd(dx, s1, 1)
    dx, parts0, small_g[0], recv1 = _layer_bwd(dx, s0, 0, ride_mlp=Exchange(parts1, True), scatter_in_attn=True)
    grad_x = dx.reshape(x.shape)
    small_local = [jnp.stack([small_g[l][k] for l in range(DEPTH)]) for k in range(7)] + [dfinal[0]]
    recv_in0, small_all = _exchange([parts0[0], _pack_small(small_local)], [True, False], "scatter_w_in0_gather_small")
    received = [[recv_in0] + list(parts0[1:]), recv1]

    big = [None] * 4
    for l in reversed(range(DEPTH)):
        for k in range(4):
            big[k] = _reduce_adamw(received[l][k], big_w[k], big_m[k], big_v[k], l, big[k], f"adamw{k}_{l}")

    sm = _reduce_adamw(
        small_all,
        _pack_small([norm1, pool_w, pool_scale, sg_norm, sg_w, sg_b, norm2, final_norm])[None],
        _pack_small([m_norm1, m_pool_w, m_pool_scale, m_sg_norm, m_sg_w, m_sg_b, m_norm2, m_final_norm])[None],
        _pack_small([v_norm1, v_pool_w, v_pool_scale, v_sg_norm, v_sg_w, v_sg_b, v_norm2, v_final_norm])[None],
        0, None, "adamw_replicated")

    out = [loss, grad_x]
    for k in range(4):
        n1, pw, ps, sn, sw, sb, n2, fn = _unpack_small(sm[k][0])
        out += [n1, big[0][k], pw, ps, sn, sw, sb, big[1][k], n2, big[2][k], big[3][k], fn]
    return tuple(out)
```

```python
import functools

import jax
import jax.numpy as jnp
from jax import lax
from jax.experimental import pallas as pl
from jax.experimental.pallas import tpu as pltpu

F32 = jnp.float32
BF16 = jnp.bfloat16

D_MODEL = 1024
DEPTH = 2
POOL_WIDTH = 256
SG_WIDTH = 256
SB_WIDTH = 512
POOL_WINDOWS = (2, 4, 8, 16)
POOL_GW = 64
POOL_HALO = 16
CHUNK = 128
SG_HEADS = 4
SG_HD = 64
SB_HD = 64
SB_PAIRS = SB_WIDTH // 128
AB_COLS = POOL_WIDTH + 2 * SG_WIDTH
IN_COLS = AB_COLS + 3 * SB_WIDTH
D_FF = 4096
EPS = 1e-6
N_DEV = 8
FF_SHARD = D_FF // N_DEV
IN_SHARD = IN_COLS // N_DEV
OUT_SHARD = D_MODEL // N_DEV
ADAM_LR = 0.001
ADAM_B1 = 0.9
ADAM_B2 = 0.999
ADAM_EPS = 1e-08
ADAM_WD = 0.01
ADAM_STEP = 10
VMEM_LIMIT = 52 * 1024 * 1024
MESH_AXES = ("x", "y", "c")


def _dot(a, b):
    return jnp.dot(a, b, preferred_element_type=F32)


def _dot_nt(a, b):
    return lax.dot_general(a, b, (((1,), (1,)), ((), ())), preferred_element_type=F32)


def _dot_tn(a, b):
    return lax.dot_general(a, b, (((0,), (0,)), ((), ())), preferred_element_type=F32)


def _rstd(x):
    return lax.rsqrt(jnp.mean(x * x, axis=-1, keepdims=True) + EPS)


def _rms_bwd(x, r, g, dh):
    gq = dh * g
    dx = r * gq - x * (r * r * r) * jnp.mean(gq * x, axis=-1, keepdims=True)
    dg = jnp.sum(dh * x * r, axis=0, keepdims=True)
    return dx, dg


def _params(sem=None):
    kw = dict(vmem_limit_bytes=VMEM_LIMIT)
    if sem is not None:
        kw["dimension_semantics"] = sem
    return pltpu.CompilerParams(**kw)


def _row_tile(rows, cap):
    best = 8
    for t in range(8, min(rows, cap) + 1, 8):
        if rows % t == 0:
            best = t
    return best


def _peer(k):
    x, y, c = lax.axis_index("x"), lax.axis_index("y"), lax.axis_index("c")
    return (1 - x if k & 4 else x, 1 - y if k & 2 else y, 1 - c if k & 1 else c)


def _my_index():
    return 4 * lax.axis_index("x") + 2 * lax.axis_index("y") + lax.axis_index("c")


class Exchange:
    def __init__(self, arrs, scatter):
        self.arrs = list(arrs)
        self.scatter = list(scatter) if isinstance(scatter, (list, tuple)) else [scatter] * len(self.arrs)
        self.n = len(self.arrs)
        self.any_specs = [pl.BlockSpec(memory_space=pl.ANY)] * self.n
        self.out_shape = [jax.ShapeDtypeStruct((N_DEV,) + a.shape[-2:], a.dtype) for a in self.arrs]
        self.sems = [pltpu.SemaphoreType.DMA((self.n, N_DEV - 1)), pltpu.SemaphoreType.DMA((self.n, N_DEV - 1)),
                     pltpu.SemaphoreType.DMA((self.n,))]

    def _copies(self, ins, outs, sems):
        send_sems, recv_sems, local_sems = sems
        me = _my_index()
        local, remote = [], []
        for a in range(self.n):
            sc = self.scatter[a]
            local.append(pltpu.make_async_copy(ins[a].at[me] if sc else ins[a], outs[a].at[me], local_sems.at[a]))
            for k in range(1, N_DEV):
                px, py, pc = _peer(k)
                remote.append(pltpu.make_async_remote_copy(
                    src_ref=ins[a].at[4 * px + 2 * py + pc] if sc else ins[a], dst_ref=outs[a].at[me],
                    send_sem=send_sems.at[a, k - 1], recv_sem=recv_sems.at[a, k - 1],
                    device_id=(px, py, pc), device_id_type=pl.DeviceIdType.MESH))
        return local, remote

    def start(self, ins, outs, sems):
        local, remote = self._copies(ins, outs, sems)
        for cp in local + remote:
            cp.start()

    def wait(self, ins, outs, sems):
        local, remote = self._copies(ins, outs, sems)
        for cp in remote:
            cp.wait_recv()
        for cp in remote:
            cp.wait_send()
        for cp in local:
            cp.wait()

    def alone(self, name):
        n = self.n

        def body(*refs):
            ins, outs, sems = refs[:n], refs[n:2 * n], refs[2 * n:]
            self.start(ins, outs, sems)
            self.wait(ins, outs, sems)

        return pl.pallas_call(body, name=name, out_shape=self.out_shape, in_specs=self.any_specs,
                              out_specs=self.any_specs, scratch_shapes=self.sems)(*self.arrs)


def _exchange(arrs, scatter, name):
    return Exchange(arrs, scatter).alone(name)


def _call(body, name, grid, in_specs, out_specs, out_shape, scratch_shapes, semantics, args, ride=None):
    if ride is None:
        return pl.pallas_call(body, name=name, grid=grid, in_specs=in_specs, out_specs=out_specs, out_shape=out_shape,
                              scratch_shapes=scratch_shapes, compiler_params=_params(semantics))(*args)
    single = not isinstance(out_shape, (list, tuple))
    out_specs, out_shape = ([out_specs], [out_shape]) if single else (list(out_specs), list(out_shape))
    n_in, n_out, n_scr, n = len(in_specs), len(out_specs), len(scratch_shapes), ride.n

    def riding(*refs):
        ins, cins = refs[:n_in], refs[n_in:n_in + n]
        outs, couts = refs[n_in + n:n_in + n + n_out], refs[n_in + n + n_out:n_in + 2 * n + n_out]
        scr, sems = refs[n_in + 2 * n + n_out:n_in + 2 * n + n_out + n_scr], refs[n_in + 2 * n + n_out + n_scr:]
        first = functools.reduce(lambda p, q: p & q, [pl.program_id(a) == 0 for a in range(len(grid))])
        last = functools.reduce(lambda p, q: p & q, [pl.program_id(a) == g - 1 for a, g in enumerate(grid)])

        @pl.when(first)
        def _():
            ride.start(cins, couts, sems)

        body(*ins, *outs, *scr)

        @pl.when(last)
        def _():
            ride.wait(cins, couts, sems)

    res = pl.pallas_call(
        riding, name=name, grid=grid, in_specs=list(in_specs) + ride.any_specs, out_specs=out_specs + ride.any_specs,
        out_shape=out_shape + ride.out_shape, scratch_shapes=list(scratch_shapes) + ride.sems,
        compiler_params=_params(("arbitrary",) * len(grid)))(*args, *ride.arrs)
    own = res[0] if single else list(res[:n_out])
    return own, list(res[n_out:])


def _reduce_adamw(parts, w, m, v, l, prev, name):
    _, rows, n = parts.shape
    tr = _row_tile(rows, max(8, (1 << 18) // n))
    c1 = 1.0 - ADAM_B1 ** ADAM_STEP
    c2 = 1.0 - ADAM_B2 ** ADAM_STEP

    def body(p_ref, w_ref, m_ref, v_ref, *rest):
        g_ref, d_ref, nm_ref, nv_ref = rest[-4:]
        g = p_ref[0].astype(F32)
        for s in range(1, N_DEV):
            g = g + p_ref[s].astype(F32)
        nm = ADAM_B1 * m_ref[...] + (1.0 - ADAM_B1) * g
        nv = ADAM_B2 * v_ref[...] + (1.0 - ADAM_B2) * (g * g)
        m_hat = nm / c1
        v_hat = nv / c2
        g_ref[...] = g
        d_ref[...] = -ADAM_LR * (m_hat / (jnp.sqrt(v_hat) + ADAM_EPS) + ADAM_WD * w_ref[...])
        nm_ref[...] = nm
        nv_ref[...] = nv

    blk = pl.BlockSpec((None, tr, n), lambda i: (l, i, 0))
    out = jax.ShapeDtypeStruct(w.shape, F32)
    prev = list(prev) if prev is not None else []
    return pl.pallas_call(
        body, name=name, grid=(rows // tr,),
        in_specs=[pl.BlockSpec((N_DEV, tr, n), lambda i: (0, i, 0)), blk, blk, blk] + [pl.BlockSpec(memory_space=pl.ANY)] * len(prev),
        out_specs=[blk, blk, blk, blk], out_shape=[out, out, out, out],
        input_output_aliases={4 + k: k for k in range(len(prev))},
        compiler_params=_params(("parallel",)),
    )(parts, w, m, v, *prev)


def _inproj_fwd(x, g, w, tm, tq, name):
    t, d = x.shape
    n = w.shape[1]
    nb = t // tq
    per = tm // tq

    def body(x_ref, g_ref, w_ref, ab_ref, qkv_ref, kt_ref, vt_ref):
        xx = x_ref[...]
        h = (xx * _rstd(xx) * g_ref[...]).astype(BF16)
        ab_ref[...] = _dot(h, w_ref[:, :AB_COLS])
        qkv = _dot(h, w_ref[:, AB_COLS:])
        qkv_ref[...] = qkv.astype(BF16)
        for which, out_ref in ((1, kt_ref), (2, vt_ref)):
            for p in range(SB_PAIRS):
                for b in range(per):
                    cols = which * SB_WIDTH + p * 128
                    out_ref[p, b] = qkv[b * tq:(b + 1) * tq, cols:cols + 128].T.astype(BF16)

    tb = pl.BlockSpec((SB_PAIRS, per, 128, tq), lambda i: (0, i, 0, 0))
    tshape = jax.ShapeDtypeStruct((SB_PAIRS, nb, 128, tq), BF16)
    return pl.pallas_call(
        body, name=name, grid=(t // tm,),
        in_specs=[pl.BlockSpec((tm, d), lambda i: (i, 0)), pl.BlockSpec((1, d), lambda i: (0, 0)),
                  pl.BlockSpec((d, n), lambda i: (0, 0))],
        out_specs=[pl.BlockSpec((tm, AB_COLS), lambda i: (i, 0)), pl.BlockSpec((tm, n - AB_COLS), lambda i: (i, 0)), tb, tb],
        out_shape=[jax.ShapeDtypeStruct((t, AB_COLS), F32), jax.ShapeDtypeStruct((t, n - AB_COLS), BF16), tshape, tshape],
        compiler_params=_params(("parallel",)),
    )(x, g, w)


def _pool_window_sums(xx, forward):
    n = xx.shape[0]
    sh = (lambda k: n - k) if forward else (lambda k: k)
    s2 = xx + pltpu.roll(xx, sh(1), 0)
    s4 = s2 + pltpu.roll(s2, sh(2), 0)
    s8 = s4 + pltpu.roll(s4, sh(4), 0)
    s16 = s8 + pltpu.roll(s8, sh(8), 0)
    grp = lax.broadcasted_iota(jnp.int32, (1, POOL_WIDTH), 1) // POOL_GW
    return jnp.where(grp == 0, s2, jnp.where(grp == 1, s4, jnp.where(grp == 2, s8, s16)))


def _pool_count(t0, rows):
    grp = lax.broadcasted_iota(jnp.int32, (1, POOL_WIDTH), 1) // POOL_GW
    win = jnp.where(grp == 0, 2, jnp.where(grp == 1, 4, jnp.where(grp == 2, 8, 16)))
    tt = t0 + lax.broadcasted_iota(jnp.int32, (rows, 1), 0)
    return jnp.minimum(tt + 1, win).astype(F32)


def _pool_diff(cur, prev, t0):
    tm = cur.shape[0]
    sums = _pool_window_sums(jnp.concatenate([prev, cur], axis=0), False)[POOL_HALO:]
    return sums / _pool_count(t0, tm) - cur


def _pool_fwd(ab, wbd, scale, tm, name):
    t = ab.shape[0]
    hb = tm // POOL_HALO

    def body(cur_ref, prev_ref, w_ref, s_ref, y_ref):
        i = pl.program_id(0)
        prev = jnp.where(i == 0, 0.0, prev_ref[...])
        d = _pool_diff(cur_ref[...], prev, i * tm).astype(BF16)
        y_ref[...] = (_dot(d, w_ref[...]) * s_ref[...]).astype(BF16)

    return pl.pallas_call(
        body, name=name, grid=(t // tm,),
        in_specs=[pl.BlockSpec((tm, POOL_WIDTH), lambda i: (i, 0)),
                  pl.BlockSpec((POOL_HALO, POOL_WIDTH), lambda i: (jnp.maximum(i * hb - 1, 0), 0)),
                  pl.BlockSpec((POOL_WIDTH, POOL_WIDTH), lambda i: (0, 0)),
                  pl.BlockSpec((1, POOL_WIDTH), lambda i: (0, 0))],
        out_specs=pl.BlockSpec((tm, POOL_WIDTH), lambda i: (i, 0)),
        out_shape=jax.ShapeDtypeStruct((t, POOL_WIDTH), BF16),
        compiler_params=_params(("parallel",)),
    )(ab, ab, wbd, scale)


_GELU_K = 0.7978845608028654
_GELU_A = 0.044715


def _gelu(x):
    return 0.5 * x * (1.0 + jnp.tanh(_GELU_K * (x + _GELU_A * (x * x * x))))


def _gelu_grad(x):
    th = jnp.tanh(_GELU_K * (x + _GELU_A * (x * x * x)))
    return 0.5 * (1.0 + th) + 0.5 * x * (1.0 - th * th) * (_GELU_K * (1.0 + 3.0 * _GELU_A * (x * x)))


def _head_lanes(h):
    return lax.broadcasted_iota(jnp.int32, (1, SG_WIDTH), 1) // SG_HD == h


def _sg_fwd(ab, gn, wm, bfull, tm, name):
    t = ab.shape[0]

    def body(u_ref, v_ref, gn_ref, wm_ref, b_ref, y_ref):
        v = _gelu(v_ref[...])
        vn = (v * _rstd(v) * gn_ref[...]).astype(BF16)
        for c in range(tm // CHUNK):
            rows = slice(c * CHUNK, (c + 1) * CHUNK)
            vc = vn[rows]
            sv = b_ref[...]
            for h in range(SG_HEADS):
                sv = sv + jnp.where(_head_lanes(h), _dot(wm_ref[h], vc), 0.0)
            y_ref[rows, :] = (_gelu(u_ref[rows, :]) * sv).astype(BF16)

    return pl.pallas_call(
        body, name=name, grid=(t // tm,),
        in_specs=[pl.BlockSpec((tm, SG_WIDTH), lambda i: (i, 1)), pl.BlockSpec((tm, SG_WIDTH), lambda i: (i, 2)),
                  pl.BlockSpec((1, SG_WIDTH), lambda i: (0, 0)),
                  pl.BlockSpec((SG_HEADS, CHUNK, CHUNK), lambda i: (0, 0, 0)),
                  pl.BlockSpec((CHUNK, SG_WIDTH), lambda i: (0, 0))],
        out_specs=pl.BlockSpec((tm, SG_WIDTH), lambda i: (i, 0)),
        out_shape=jax.ShapeDtypeStruct((t, SG_WIDTH), BF16),
        compiler_params=_params(("parallel",)),
    )(ab, ab, gn, wm, bfull)


LOG2E = 1.4426950408889634
SB_SCALE = 0.125 * LOG2E
SB_DEAD_LOG2 = 152.0


def _log2_sigmoids(y):
    neg_abs = lax.bitcast_convert_type(lax.bitcast_convert_type(y, jnp.uint32) | jnp.uint32(0x80000000), F32)
    lb = jnp.minimum(y, 0.0) - jnp.log(1.0 + jnp.exp2(neg_abs)) * LOG2E
    return lb, lb - y


def _split(x):
    hi = x.astype(BF16)
    return hi, (x - hi.astype(F32)).astype(BF16)


def _tri_dot(tri, x):
    hi, lo = _split(x)
    return _dot(tri, hi) + _dot(tri, lo)


def _sba_fwd(qkv, vtb, tq, name, ride=None):
    t = qkv.shape[0]
    nb = t // tq
    upper = (jnp.arange(tq)[None, :] > jnp.arange(tq)[:, None]).astype(BF16)

    def body(q_ref, k_ref, vt_ref, up_ref, ot_ref, c_ref, n_ref):
        i = pl.program_id(1)
        q = q_ref[...]
        up = up_ref[...]
        lane_head = lax.broadcasted_iota(jnp.int32, (1, 128), 1) // SB_HD
        sub_head = lax.broadcasted_iota(jnp.int32, (128, 1), 0) // SB_HD
        causal = (lax.broadcasted_iota(jnp.int32, (tq, tq), 0) < lax.broadcasted_iota(jnp.int32, (tq, tq), 1))
        qh = [jnp.where(lane_head == h, q, jnp.zeros_like(q)) for h in range(2)]

        def blocks(js, carry):
            cs, acc = list(carry[:2]), carry[2]
            kj = [k_ref[pl.ds(pl.multiple_of(j * tq, tq), tq), :] for j, _ in js]
            vt = [vt_ref[j] for j, _ in js]
            chains = [(b, h) for b in range(len(js)) for h in range(2)]
            z = [_dot_nt(kj[b], qh[h]) for b, h in chains]
            ls = [_log2_sigmoids(zz * SB_SCALE) for zz in z]
            lb = [x[0] for x in ls]
            l1 = [jnp.where(causal, x[1], 0.0) if js[b][1] else x[1] for x, (b, h) in zip(ls, chains)]
            after = [_tri_dot(up, x) for x in l1]
            a = []
            for n, (b, h) in enumerate(chains):
                c_ref[h, js[b][0]] = cs[h]
                an = jnp.exp2(lb[n] + after[n] + cs[h])
                a.append(jnp.where(causal, an, 0.0) if js[b][1] else an)
                cs[h] = cs[h] + after[n][0:1, :] + l1[n][0:1, :]
            for n, (b, h) in enumerate(chains):
                acc = acc + _dot(jnp.where(sub_head == h, vt[b], jnp.zeros_like(vt[b])), a[n].astype(BF16))
            return cs[0], cs[1], acc

        def alive(c0, c1):
            return (jnp.max(jnp.maximum(c0, c1)) > -SB_DEAD_LOG2).astype(jnp.int32)

        def step(state):
            s, _, c0, c1, acc = state
            c0, c1, acc = blocks([(i - 1 - s, False)], (c0, c1, acc))
            return s + 1, alive(c0, c1), c0, c1, acc

        zero = jnp.zeros((1, tq), F32)
        start = (zero, zero, jnp.zeros((128, tq), F32))
        c0, c1, acc = lax.cond(i > 0, lambda: blocks([(i, True), (i - 1, False)], start), lambda: blocks([(i, True)], start))
        first = jnp.minimum(i, 1)
        state = lax.while_loop(lambda st: (st[0] < i) & (st[1] > 0), step, (first, alive(c0, c1), c0, c1, acc))
        ot_ref[...] = state[4].astype(BF16)
        n_ref[pl.program_id(0), i] = (state[0] + 1).astype(F32)

    return _call(
        body, name, (SB_PAIRS, nb),
        [pl.BlockSpec((tq, 128), lambda p, i: (i, p)),
         pl.BlockSpec((t, 128), lambda p, i: (0, SB_PAIRS + p)),
         pl.BlockSpec((None, nb, 128, tq), lambda p, i: (p, 0, 0, 0)),
         pl.BlockSpec((tq, tq), lambda p, i: (0, 0))],
        [pl.BlockSpec((128, tq), lambda p, i: (p, i)),
         pl.BlockSpec((2, nb, 1, tq), lambda p, i: (p, 0, 0, i)),
         pl.BlockSpec(memory_space=pltpu.SMEM)],
        [jax.ShapeDtypeStruct((SB_WIDTH, t), BF16),
         jax.ShapeDtypeStruct((2 * SB_PAIRS, nb, 1, t), F32),
         jax.ShapeDtypeStruct((SB_PAIRS, nb), F32)],
        [], ("arbitrary", "arbitrary"), (qkv, qkv, vtb, upper), ride)


def _outproj_mlp_fwd(x, ya, yb, yct, wo, g2, wup, wdn, tm, name, ride=None):
    t, d = x.shape
    nf = wup.shape[0]

    def body(x_ref, ya_ref, yb_ref, yct_ref, wo_ref, g_ref, wu_ref, wd_ref, x1_ref, u_ref, x2_ref, h_ref, acc_ref):
        j = pl.program_id(1)

        @pl.when(j == 0)
        def _():
            x1 = (x_ref[...] + _dot(ya_ref[...], wo_ref[0:POOL_WIDTH, :])
                  + _dot(yb_ref[...], wo_ref[POOL_WIDTH:POOL_WIDTH + SG_WIDTH, :])
                  + _dot_tn(yct_ref[...], wo_ref[POOL_WIDTH + SG_WIDTH:, :]))
            x1_ref[...] = x1
            h_ref[...] = (x1 * _rstd(x1) * g_ref[...]).astype(BF16)
            acc_ref[...] = x1

        u = _dot(h_ref[...], wu_ref[...])
        u_ref[...] = u
        r = jnp.maximum(u, 0.0)
        acc_ref[...] += _dot((r * r).astype(BF16), wd_ref[...])

        @pl.when(j == nf - 1)
        def _():
            x2_ref[...] = acc_ref[...]

    row = lambda w: pl.BlockSpec((tm, w), lambda i, j: (i, 0))
    return _call(
        body, name, (t // tm, nf),
        [row(d), row(POOL_WIDTH), row(SG_WIDTH), pl.BlockSpec((SB_WIDTH, tm), lambda i, j: (0, i)),
         pl.BlockSpec((d, d), lambda i, j: (0, 0)), pl.BlockSpec((1, d), lambda i, j: (0, 0)),
         pl.BlockSpec((None, d, FF_SHARD), lambda i, j: (j, 0, 0)),
         pl.BlockSpec((None, FF_SHARD, d), lambda i, j: (j, 0, 0))],
        [row(d), pl.BlockSpec((tm, FF_SHARD), lambda i, j: (i, j)), row(d)],
        [jax.ShapeDtypeStruct((t, d), F32), jax.ShapeDtypeStruct((t, nf * FF_SHARD), F32),
         jax.ShapeDtypeStruct((t, d), F32)],
        [pltpu.VMEM((tm, d), BF16), pltpu.VMEM((tm, d), F32)],
        ("parallel", "arbitrary"), (x, ya, yb, yct, wo, g2, wup, wdn), ride)


def _loss_grad(x, g, target, tm, name):
    t, d = x.shape
    nt = t // tm

    def body(x_ref, g_ref, t_ref, loss_ref, dx_ref, dg_ref, sq_ref):
        i = pl.program_id(0)

        @pl.when(i == 0)
        def _():
            sq_ref[...] = jnp.zeros_like(sq_ref)
            dg_ref[...] = jnp.zeros_like(dg_ref)

        xx = x_ref[...]
        r = _rstd(xx)
        err = xx * r * g_ref[...] - t_ref[...]
        sq_ref[...] += jnp.sum(err * err, axis=0, keepdims=True)
        dx, dg = _rms_bwd(xx, r, g_ref[...], err * (1.0 / d))
        dx_ref[...] = dx
        dg_ref[...] += dg

        @pl.when(i == nt - 1)
        def _():
            loss_ref[...] = jnp.sum(sq_ref[...], axis=1, keepdims=True) * (0.5 / d)

    return pl.pallas_call(
        body, name=name, grid=(nt,),
        in_specs=[pl.BlockSpec((tm, d), lambda i: (i, 0)), pl.BlockSpec((1, d), lambda i: (0, 0)),
                  pl.BlockSpec((tm, d), lambda i: (i, 0))],
        out_specs=[pl.BlockSpec((1, 1), lambda i: (0, 0)), pl.BlockSpec((tm, d), lambda i: (i, 0)),
                   pl.BlockSpec((1, d), lambda i: (0, 0))],
        out_shape=[jax.ShapeDtypeStruct((1, 1), F32), jax.ShapeDtypeStruct((t, d), F32),
                   jax.ShapeDtypeStruct((1, d), F32)],
        scratch_shapes=[pltpu.VMEM((1, d), F32)],
        compiler_params=_params(("arbitrary",)),
    )(x, g, target)


def _mlp_bwd(dx2, x1, g2, u, wup, wdn, tm, name, ride=None):
    t, d = dx2.shape
    nf = wup.shape[0]
    nt = t // tm

    def body(dx2_ref, x1_ref, g_ref, u_ref, wu_ref, wd_ref, dx1_ref, du_ref, r_ref, h_ref, dxb_ref, dg_ref, acc_ref):
        i, j = pl.program_id(0), pl.program_id(1)

        @pl.when(j == 0)
        def _():
            x1 = x1_ref[...]
            h_ref[...] = (x1 * _rstd(x1) * g_ref[...]).astype(BF16)
            dxb_ref[...] = dx2_ref[...].astype(BF16)
            acc_ref[...] = jnp.zeros_like(acc_ref)

        @pl.when((i == 0) & (j == 0))
        def _():
            dg_ref[...] = jnp.zeros_like(dg_ref)

        dr = _dot_nt(dxb_ref[...], wd_ref[...])
        ru = jnp.maximum(u_ref[...], 0.0)
        du = (dr * (2.0 * ru)).astype(BF16)
        du_ref[...] = du
        r_ref[...] = (ru * ru).astype(BF16)
        acc_ref[...] += _dot_nt(du, wu_ref[...])

        @pl.when(j == nf - 1)
        def _():
            x1 = x1_ref[...]
            dx, dg = _rms_bwd(x1, _rstd(x1), g_ref[...], acc_ref[...])
            dx1_ref[...] = dx2_ref[...] + dx
            dg_ref[...] += dg

    row = lambda w: pl.BlockSpec((tm, w), lambda i, j: (i, 0))
    col = pl.BlockSpec((tm, FF_SHARD), lambda i, j: (i, j))
    return _call(
        body, name, (nt, nf),
        [row(d), row(d), pl.BlockSpec((1, d), lambda i, j: (0, 0)), col,
         pl.BlockSpec((None, d, FF_SHARD), lambda i, j: (j, 0, 0)),
         pl.BlockSpec((None, FF_SHARD, d), lambda i, j: (j, 0, 0))],
        [row(d), col, col, row(d), row(d), pl.BlockSpec((1, d), lambda i, j: (0, 0))],
        [jax.ShapeDtypeStruct((t, d), F32), jax.ShapeDtypeStruct((t, nf * FF_SHARD), BF16),
         jax.ShapeDtypeStruct((t, nf * FF_SHARD), BF16), jax.ShapeDtypeStruct((t, d), BF16),
         jax.ShapeDtypeStruct((t, d), BF16), jax.ShapeDtypeStruct((1, d), F32)],
        [pltpu.VMEM((tm, d), F32)], ("arbitrary", "arbitrary"), (dx2, x1, g2, u, wup, wdn), ride)


def _matmul_tn(a, b, bm, bn, bt, name, by_column_block=False, a_transposed=False):
    m, t = a.shape if a_transposed else a.shape[::-1]
    n = b.shape[1]
    nk = t // bt

    def body(a_ref, b_ref, o_ref, acc_ref):
        k = pl.program_id(2)

        @pl.when(k == 0)
        def _():
            acc_ref[...] = jnp.zeros_like(acc_ref)

        acc_ref[...] += _dot(a_ref[...], b_ref[...]) if a_transposed else _dot_tn(a_ref[...], b_ref[...])

        @pl.when(k == nk - 1)
        def _():
            o_ref[...] = acc_ref[...].astype(BF16)

    if by_column_block:
        out_spec = pl.BlockSpec((None, bm, bn), lambda i, j, k: (j, i, 0))
        out_shape = jax.ShapeDtypeStruct((n // bn, m, bn), BF16)
    else:
        out_spec = pl.BlockSpec((bm, bn), lambda i, j, k: (i, j))
        out_shape = jax.ShapeDtypeStruct((m, n), BF16)
    return pl.pallas_call(
        body, name=name, grid=(m // bm, n // bn, nk),
        in_specs=[pl.BlockSpec((bm, bt), lambda i, j, k: (i, k)) if a_transposed else pl.BlockSpec((bt, bm), lambda i, j, k: (k, i)),
                  pl.BlockSpec((bt, bn), lambda i, j, k: (k, j))],
        out_specs=out_spec, out_shape=out_shape,
        scratch_shapes=[pltpu.VMEM((bm, bn), F32)],
        compiler_params=_params(("parallel", "parallel", "arbitrary")),
    )(a, b)


def _outproj_bwd(dx1, wo, tm, name):
    t, d = dx1.shape
    c2 = POOL_WIDTH + SG_WIDTH

    def body(dx_ref, wo_ref, dya_ref, dyb_ref, dyc_ref, dyct_ref, dxb_ref):
        dxb = dx_ref[...].astype(BF16)
        dxb_ref[...] = dxb
        dya_ref[...] = _dot_nt(dxb, wo_ref[0:POOL_WIDTH, :])
        dyb_ref[...] = _dot_nt(dxb, wo_ref[POOL_WIDTH:c2, :])
        dyc_ref[...] = _dot_nt(dxb, wo_ref[c2:, :]).astype(BF16)
        dyct_ref[...] = _dot_nt(wo_ref[c2:, :], dxb).astype(BF16)

    row = lambda w: pl.BlockSpec((tm, w), lambda i: (i, 0))
    return pl.pallas_call(
        body, name=name, grid=(t // tm,),
        in_specs=[row(d), pl.BlockSpec((d, d), lambda i: (0, 0))],
        out_specs=[row(POOL_WIDTH), row(SG_WIDTH), row(SB_WIDTH), pl.BlockSpec((SB_WIDTH, tm), lambda i: (0, i)), row(d)],
        out_shape=[jax.ShapeDtypeStruct((t, POOL_WIDTH), F32), jax.ShapeDtypeStruct((t, SG_WIDTH), F32),
                   jax.ShapeDtypeStruct((t, SB_WIDTH), BF16), jax.ShapeDtypeStruct((SB_WIDTH, t), BF16),
                   jax.ShapeDtypeStruct((t, d), BF16)],
        compiler_params=_params(("parallel",)),
    )(dx1, wo)


def _sba_bwd(qkv, ktb, dyc, dyct, cc, nvis, tq, name, ride=None):
    t = qkv.shape[0]
    nb = t // tq
    idx = jnp.arange(tq)
    upper = (idx[None, :] > idx[:, None]).astype(BF16)
    lower = (idx[None, :] < idx[:, None]).astype(BF16)

    def body(q_ref, k_ref, v_ref, kt_ref, do_ref, dot_ref, c_ref, n_ref, up_ref, lo_ref, dqt_ref, dk_ref, dv_ref):
        i = pl.program_id(1)

        @pl.when(i == 0)
        def _():
            dk_ref[...] = jnp.zeros_like(dk_ref)
            dv_ref[...] = jnp.zeros_like(dv_ref)

        q = q_ref[...]
        do = do_ref[...]
        dot = dot_ref[...]
        up = up_ref[...]
        lo = lo_ref[...]
        lane_head = lax.broadcasted_iota(jnp.int32, (1, 128), 1) // SB_HD
        sub_head = lax.broadcasted_iota(jnp.int32, (128, 1), 0) // SB_HD
        causal = (lax.broadcasted_iota(jnp.int32, (tq, tq), 0) < lax.broadcasted_iota(jnp.int32, (tq, tq), 1))
        hms = [lane_head == h for h in range(2)]
        qh = [jnp.where(hm, q, jnp.zeros_like(q)) for hm in hms]
        qs = [x * 0.125 for x in qh]
        doh = [jnp.where(hm, do, jnp.zeros_like(do)) for hm in hms]

        def blocks(js, carry):
            cgs, dqt = list(carry[:2]), carry[2]
            rows = [pl.ds(pl.multiple_of(j * tq, tq), tq) for j, _ in js]
            kj = [k_ref[r, :] for r in rows]
            vj = [v_ref[r, :] for r in rows]
            kt = [kt_ref[j] for j, _ in js]
            chains = [(b, h) for b in range(len(js)) for h in range(2)]
            z = [_dot_nt(kj[b], qh[h]) for b, h in chains]
            da = [_dot(jnp.where(hms[h], vj[b], jnp.zeros_like(vj[b])), dot) for b, h in chains]
            ls = [_log2_sigmoids(zz * SB_SCALE) for zz in z]
            lb = [x[0] for x in ls]
            l1 = [jnp.where(causal, x[1], 0.0) if js[b][1] else x[1] for x, (b, h) in zip(ls, chains)]
            after = [_tri_dot(up, x) for x in l1]
            a = [jnp.exp2(lb[n] + after[n] + c_ref[h, js[b][0]]) for n, (b, h) in enumerate(chains)]
            a = [jnp.where(causal, a[n], 0.0) if js[b][1] else a[n] for n, (b, h) in enumerate(chains)]
            g = [a[n] * da[n] for n in range(len(chains))]
            gloc = [_tri_dot(lo, x) for x in g]
            dzb = []
            for n, (b, h) in enumerate(chains):
                gsum = gloc[n] + cgs[h]
                dz = g[n] - jnp.exp2(lb[n]) * (g[n] + gsum)
                dzb.append((jnp.where(causal, dz, 0.0) if js[b][1] else dz).astype(BF16))
                cgs[h] = gsum[tq - 1:tq, :] + g[n][tq - 1:tq, :]
            ab = [x.astype(BF16) for x in a]
            for n, (b, h) in enumerate(chains):
                dqt = dqt + _dot(jnp.where(sub_head == h, kt[b], jnp.zeros_like(kt[b])), dzb[n])
            for b in range(len(js)):
                dk_ref[rows[b], :] += _dot(dzb[2 * b], qs[0]) + _dot(dzb[2 * b + 1], qs[1])
                dv_ref[rows[b], :] += _dot(ab[2 * b], doh[0]) + _dot(ab[2 * b + 1], doh[1])
            return cgs[0], cgs[1], dqt

        n = jnp.clip(n_ref[pl.program_id(0), i].astype(jnp.int32), 1, i + 1)
        zero = jnp.zeros((1, tq), F32)
        carry = lax.fori_loop(jnp.minimum(i + 1 - n, i - 1), i - 1, lambda s, cr: blocks([(s, False)], cr),
                              (zero, zero, jnp.zeros((128, tq), F32)))
        carry = lax.cond(i > 0, lambda: blocks([(i - 1, False), (i, True)], carry), lambda: blocks([(i, True)], carry))
        dqt_ref[...] = carry[2] * 0.125

    return _call(
        body, name, (SB_PAIRS, nb),
        [pl.BlockSpec((tq, 128), lambda p, i: (i, p)),
         pl.BlockSpec((t, 128), lambda p, i: (0, SB_PAIRS + p)),
         pl.BlockSpec((t, 128), lambda p, i: (0, 2 * SB_PAIRS + p)),
         pl.BlockSpec((None, nb, 128, tq), lambda p, i: (p, 0, 0, 0)),
         pl.BlockSpec((tq, 128), lambda p, i: (i, p)),
         pl.BlockSpec((128, tq), lambda p, i: (p, i)),
         pl.BlockSpec((2, nb, 1, tq), lambda p, i: (p, 0, 0, i)),
         pl.BlockSpec(memory_space=pltpu.SMEM),
         pl.BlockSpec((tq, tq), lambda p, i: (0, 0)),
         pl.BlockSpec((tq, tq), lambda p, i: (0, 0))],
        [pl.BlockSpec((128, tq), lambda p, i: (p, i)),
         pl.BlockSpec((t, 128), lambda p, i: (0, p)),
         pl.BlockSpec((t, 128), lambda p, i: (0, p))],
        [jax.ShapeDtypeStruct((SB_WIDTH, t), F32), jax.ShapeDtypeStruct((t, SB_WIDTH), F32),
         jax.ShapeDtypeStruct((t, SB_WIDTH), F32)],
        [], ("arbitrary", "arbitrary"), (qkv, qkv, qkv, ktb, dyc, dyct, cc, nvis, upper, lower), ride)


def _sg_bwd(ab, dyb, gn, wm, wmt, bfull, tm, name):
    t = ab.shape[0]
    nt = t // tm
    sel = (jnp.arange(SG_WIDTH)[:, None] // SG_HD == jnp.arange(CHUNK)[None, :]).astype(F32)

    def body(u_ref, v_ref, dy_ref, gn_ref, wm_ref, wmt_ref, b_ref, sel_ref,
             dup_ref, dvp_ref, dgn_ref, dw_ref, db_ref, dbacc_ref):
        i = pl.program_id(0)

        @pl.when(i == 0)
        def _():
            dgn_ref[...] = jnp.zeros_like(dgn_ref)
            dw_ref[...] = jnp.zeros_like(dw_ref)
            dbacc_ref[...] = jnp.zeros_like(dbacc_ref)

        tril = (lax.broadcasted_iota(jnp.int32, (CHUNK, CHUNK), 0) >= lax.broadcasted_iota(jnp.int32, (CHUNK, CHUNK), 1))
        gn_ = gn_ref[...]
        for c in range(tm // CHUNK):
            rows = slice(c * CHUNK, (c + 1) * CHUNK)
            up, vp, dy = u_ref[rows, :], v_ref[rows, :], dy_ref[rows, :]
            u, v = _gelu(up), _gelu(vp)
            r = _rstd(v)
            vn = (v * r * gn_).astype(BF16)
            sv = b_ref[...]
            for h in range(SG_HEADS):
                sv = sv + jnp.where(_head_lanes(h), _dot(wm_ref[h], vn), 0.0)
            dup_ref[rows, :] = dy * sv * _gelu_grad(up)
            dsv = dy * u
            dbacc_ref[...] += dsv
            dvn = jnp.zeros((CHUNK, SG_WIDTH), F32)
            for h in range(SG_HEADS):
                dsv_h = jnp.where(_head_lanes(h), dsv, 0.0).astype(BF16)
                dvn = dvn + _dot(wmt_ref[h], dsv_h)
                dw_ref[h] += jnp.where(tril, _dot_nt(dsv_h, vn), 0.0)
            dv, dgn = _rms_bwd(v, r, gn_, dvn)
            dgn_ref[...] += dgn
            dvp_ref[rows, :] = dv * _gelu_grad(vp)

        @pl.when(i == nt - 1)
        def _():
            db_ref[...] = jnp.dot(dbacc_ref[...], sel_ref[...], preferred_element_type=F32,
                                  precision=lax.Precision.HIGHEST)

    const = lambda shape: pl.BlockSpec(shape, lambda i: tuple(0 for _ in shape))
    return pl.pallas_call(
        body, name=name, grid=(nt,),
        in_specs=[pl.BlockSpec((tm, SG_WIDTH), lambda i: (i, 1)), pl.BlockSpec((tm, SG_WIDTH), lambda i: (i, 2)),
                  pl.BlockSpec((tm, SG_WIDTH), lambda i: (i, 0)), const((1, SG_WIDTH)),
                  const((SG_HEADS, CHUNK, CHUNK)), const((SG_HEADS, CHUNK, CHUNK)), const((CHUNK, SG_WIDTH)),
                  const((SG_WIDTH, CHUNK))],
        out_specs=[pl.BlockSpec((tm, SG_WIDTH), lambda i: (i, 0)), pl.BlockSpec((tm, SG_WIDTH), lambda i: (i, 0)),
                   const((1, SG_WIDTH)), const((SG_HEADS, CHUNK, CHUNK)), const((CHUNK, CHUNK))],
        out_shape=[jax.ShapeDtypeStruct((t, SG_WIDTH), F32), jax.ShapeDtypeStruct((t, SG_WIDTH), F32),
                   jax.ShapeDtypeStruct((1, SG_WIDTH), F32), jax.ShapeDtypeStruct((SG_HEADS, CHUNK, CHUNK), F32),
                   jax.ShapeDtypeStruct((CHUNK, CHUNK), F32)],
        scratch_shapes=[pltpu.VMEM((CHUNK, SG_WIDTH), F32)],
        compiler_params=_params(("arbitrary",)),
    )(ab, ab, dyb, gn, wm, wmt, bfull, sel)


def _pool_bwd(ab, dya, wbd, scale, tm, name):
    t = ab.shape[0]
    nt = t // tm
    hb = tm // POOL_HALO
    nh = t // POOL_HALO

    def body(cur_ref, prev_ref, dy_ref, dyn_ref, w_ref, s_ref, da_ref, dw_ref, ds_ref):
        i = pl.program_id(0)

        @pl.when(i == 0)
        def _():
            dw_ref[...] = jnp.zeros_like(dw_ref)
            ds_ref[...] = jnp.zeros_like(ds_ref)

        prev = jnp.where(i == 0, 0.0, prev_ref[...])
        d = _pool_diff(cur_ref[...], prev, i * tm).astype(BF16)
        dy = dy_ref[...]
        ds_ref[...] += jnp.sum(dy * _dot(d, w_ref[...]), axis=0, keepdims=True)
        dyn = jnp.where(i == nt - 1, 0.0, dyn_ref[...])
        dys = (jnp.concatenate([dy, dyn], axis=0) * s_ref[...]).astype(BF16)
        dw_ref[...] += _dot_tn(d, dys[:tm])
        dd = _dot_nt(dys, w_ref[...])
        fwd = _pool_window_sums(dd / _pool_count(i * tm, tm + POOL_HALO), True)
        da_ref[...] = fwd[:tm] - dd[:tm]

    return pl.pallas_call(
        body, name=name, grid=(nt,),
        in_specs=[pl.BlockSpec((tm, POOL_WIDTH), lambda i: (i, 0)),
                  pl.BlockSpec((POOL_HALO, POOL_WIDTH), lambda i: (jnp.maximum(i * hb - 1, 0), 0)),
                  pl.BlockSpec((tm, POOL_WIDTH), lambda i: (i, 0)),
                  pl.BlockSpec((POOL_HALO, POOL_WIDTH), lambda i: (jnp.minimum((i + 1) * hb, nh - 1), 0)),
                  pl.BlockSpec((POOL_WIDTH, POOL_WIDTH), lambda i: (0, 0)),
                  pl.BlockSpec((1, POOL_WIDTH), lambda i: (0, 0))],
        out_specs=[pl.BlockSpec((tm, POOL_WIDTH), lambda i: (i, 0)),
                   pl.BlockSpec((POOL_WIDTH, POOL_WIDTH), lambda i: (0, 0)),
                   pl.BlockSpec((1, POOL_WIDTH), lambda i: (0, 0))],
        out_shape=[jax.ShapeDtypeStruct((t, POOL_WIDTH), F32), jax.ShapeDtypeStruct((POOL_WIDTH, POOL_WIDTH), F32),
                   jax.ShapeDtypeStruct((1, POOL_WIDTH), F32)],
        compiler_params=_params(("arbitrary",)),
    )(ab, ab, dya, dya, wbd, scale)


def _inproj_bwd(dx1, x, g, da, dup, dvp, dqt, dk, dv, w, tm, name):
    t, d = x.shape
    n = w.shape[1]
    nt = t // tm

    def body(dx1_ref, x_ref, g_ref, da_ref, du_ref, dv_ref, dqt_ref, dk_ref, dvv_ref, w_ref,
             dx_ref, h_ref, dp_ref, dg_ref):
        @pl.when(pl.program_id(0) == 0)
        def _():
            dg_ref[...] = jnp.zeros_like(dg_ref)

        dp = jnp.concatenate([da_ref[...], du_ref[...], dv_ref[...], dqt_ref[...].T, dk_ref[...], dvv_ref[...]],
                             axis=1).astype(BF16)
        dp_ref[...] = dp
        xx = x_ref[...]
        r = _rstd(xx)
        h_ref[...] = (xx * r * g_ref[...]).astype(BF16)
        dx, dg = _rms_bwd(xx, r, g_ref[...], _dot_nt(dp, w_ref[...]))
        dx_ref[...] = dx1_ref[...] + dx
        dg_ref[...] += dg

    row = lambda w_: pl.BlockSpec((tm, w_), lambda i: (i, 0))
    return pl.pallas_call(
        body, name=name, grid=(nt,),
        in_specs=[row(d), row(d), pl.BlockSpec((1, d), lambda i: (0, 0)), row(POOL_WIDTH), row(SG_WIDTH),
                  row(SG_WIDTH), pl.BlockSpec((SB_WIDTH, tm), lambda i: (0, i)), row(SB_WIDTH), row(SB_WIDTH),
                  pl.BlockSpec((d, n), lambda i: (0, 0))],
        out_specs=[row(d), row(d), row(n), pl.BlockSpec((1, d), lambda i: (0, 0))],
        out_shape=[jax.ShapeDtypeStruct((t, d), F32), jax.ShapeDtypeStruct((t, d), BF16),
                   jax.ShapeDtypeStruct((t, n), BF16), jax.ShapeDtypeStruct((1, d), F32)],
        compiler_params=_params(("arbitrary",)),
    )(dx1, x, g, da, dup, dvp, dqt, dk, dv, w)


def _full_w_in(gathered):
    return gathered.transpose(1, 0, 2).reshape(D_MODEL, IN_COLS)


_SMALL_SHAPES = ((D_MODEL,), (4, POOL_GW, POOL_GW), (POOL_WIDTH,), (SG_WIDTH,), (SG_HEADS, CHUNK, CHUNK),
                 (SG_HEADS, CHUNK), (D_MODEL,))
_SMALL_SIZES = tuple(functools.reduce(lambda p, q: p * q, shp) for shp in _SMALL_SHAPES)
_SMALL_ROWS = sum(_SMALL_SIZES) // 128


def _pack_small_layer(arrs):
    return jnp.concatenate([a.reshape(-1) for a in arrs]).reshape(_SMALL_ROWS, 128)


def _pack_small(arrs, final):
    return jnp.concatenate([_pack_small_layer([a[l] for a in arrs]) for l in range(DEPTH)] + [final.reshape(-1, 128)])


def _unpack_small(buf):
    per_layer = []
    for l in range(DEPTH):
        flat, off, outs = buf[l * _SMALL_ROWS:(l + 1) * _SMALL_ROWS].reshape(-1), 0, []
        for shp, size in zip(_SMALL_SHAPES, _SMALL_SIZES):
            outs.append(flat[off:off + size].reshape(shp))
            off += size
        per_layer.append(outs)
    return [jnp.stack([per_layer[l][k] for l in range(DEPTH)]) for k in range(len(_SMALL_SHAPES))] + \
           [buf[DEPTH * _SMALL_ROWS:].reshape(-1)]


def _tiles(t):
    return min(512, t), min(256, t // 4), min(2048, t)


def _layer_fwd(xl, wi, rest, small_w, l, ride_attn=None, ride_mlp=None):
    n1, pw, ps, sn, sw, sb, n2 = small_w
    tm, tq, _ = _tiles(xl.shape[0])
    nb = xl.shape[0] // tq
    wbd = jnp.zeros((4, POOL_GW, 4, POOL_GW), F32)
    for gi in range(4):
        wbd = wbd.at[gi, :, gi, :].set(pw[gi])
    wbd = wbd.reshape(POOL_WIDTH, POOL_WIDTH).astype(BF16)
    wm = (sw * jnp.tril(jnp.ones((CHUNK, CHUNK), F32))).astype(BF16)
    wmt = wm.transpose(0, 2, 1)
    bfull = jnp.repeat(sb.T, SG_HD, axis=1)
    g1, g2, psc, gn = n1[None, :], n2[None, :], ps[None, :], sn[None, :]

    ab, qkv, ktb, vtb = _inproj_fwd(xl, g1, wi, tm, tq, f"inproj_fwd{l}")
    ya = _pool_fwd(ab, wbd, psc, tm, f"pool_fwd{l}")
    yb = _sg_fwd(ab, gn, wm, bfull, tm, f"sg_fwd{l}")
    if rest is None:
        (yct, cc, nvis), (wo, wu, wd) = _sba_fwd(qkv, vtb, tq, f"sba_fwd{l}", ride_attn)
    else:
        yct, cc, nvis = _sba_fwd(qkv, vtb, tq, f"sba_fwd{l}")
        wo, wu, wd = rest
    wo = wo.reshape(D_MODEL, D_MODEL)
    res = _outproj_mlp_fwd(xl, ya, yb, yct, wo, g2, wu, wd, tm, f"mlp_fwd{l}", ride_mlp)
    (x1, u, x2), rode = res if ride_mlp is not None else (res, None)
    saved = dict(x=xl, ab=ab, qkv=qkv, ktb=ktb, cc=cc, nvis=nvis, ya=ya, yb=yb, yct=yct, x1=x1, u=u, wi=wi, wo=wo, wu=wu,
                 wd=wd, wbd=wbd, wm=wm, wmt=wmt, bfull=bfull, g1=g1, g2=g2, psc=psc, gn=gn)
    return x2, saved, rode


def _layer_bwd(dx, s, l, ride_mlp=None, scatter_in_attn=False):
    tm, tq, tw = _tiles(dx.shape[0])
    ktb = s["ktb"]
    res = _mlp_bwd(dx, s["x1"], s["g2"], s["u"], s["wu"], s["wd"], tm, f"mlp_bwd{l}", ride_mlp)
    (dx1, du, r, h2, dx2b, dn2), rode = res if ride_mlp is not None else (res, None)
    dw_up = _matmul_tn(h2, du, D_MODEL, FF_SHARD, tw, f"dw_up{l}", by_column_block=True)
    dw_down = _matmul_tn(r, dx2b, 1024, D_MODEL, tw, f"dw_down{l}").reshape(N_DEV, FF_SHARD, D_MODEL)
    dya, dyb, dyc, dyct, dx1b = _outproj_bwd(dx1, s["wo"], tm, f"outproj_bwd{l}")
    dw_out = jnp.concatenate([
        _matmul_tn(jnp.concatenate([s["ya"], s["yb"]], axis=1), dx1b, POOL_WIDTH + SG_WIDTH, D_MODEL, tw, f"dw_out_ab{l}"),
        _matmul_tn(s["yct"], dx1b, SB_WIDTH, D_MODEL, tw, f"dw_out_c{l}", a_transposed=True)]
    ).reshape(N_DEV, OUT_SHARD, D_MODEL)
    if scatter_in_attn:
        (dqt, dk, dv), (dw_out, dw_up, dw_down) = _sba_bwd(s["qkv"], ktb, dyc, dyct, s["cc"], s["nvis"], tq, f"sba_bwd{l}",
                                                           Exchange([dw_out, dw_up, dw_down], True))
    else:
        dqt, dk, dv = _sba_bwd(s["qkv"], ktb, dyc, dyct, s["cc"], s["nvis"], tq, f"sba_bwd{l}")
    dup, dvp, dgn, dwm, dbm = _sg_bwd(s["ab"], dyb, s["gn"], s["wm"], s["wmt"], s["bfull"], tm, f"sg_bwd{l}")
    da, dwbd, dpsc = _pool_bwd(s["ab"], dya, s["wbd"], s["psc"], tm, f"pool_bwd{l}")
    dx, h1, dproj, dn1 = _inproj_bwd(dx1, s["x"], s["g1"], da, dup, dvp, dqt, dk, dv, s["wi"], tm, f"inproj_bwd{l}")
    dw_in = _matmul_tn(h1, dproj, D_MODEL, IN_COLS // 3, tw, f"dw_in{l}")
    dw_in = dw_in.reshape(D_MODEL, N_DEV, IN_SHARD).transpose(1, 0, 2)
    dpw = jnp.stack([dwbd[gi * POOL_GW:(gi + 1) * POOL_GW, gi * POOL_GW:(gi + 1) * POOL_GW] for gi in range(4)])
    small = (dn1[0], dpw, dpsc[0], dgn[0], dwm, dbm[:, :SG_HEADS].T, dn2[0])
    return dx, (dw_in, dw_out, dw_up, dw_down), small, rode


def kernel(x, norm1, w_in, pool_w, pool_scale, sg_norm, sg_w, sg_b, w_out, norm2, w_up, w_down, final_norm, loss_target, m_norm1, m_w_in, m_pool_w, m_pool_scale, m_sg_norm, m_sg_w, m_sg_b, m_w_out, m_norm2, m_w_up, m_w_down, m_final_norm, v_norm1, v_w_in, v_pool_w, v_pool_scale, v_sg_norm, v_sg_w, v_sg_b, v_w_out, v_norm2, v_w_up, v_w_down, v_final_norm):
    t = x.shape[1]
    tm = _tiles(t)[0]
    small_w = (norm1, pool_w, pool_scale, sg_norm, sg_w, sg_b, norm2)
    big_w = (w_in, w_out, w_up, w_down)
    big_m = (m_w_in, m_w_out, m_w_up, m_w_down)
    big_v = (v_w_in, v_w_out, v_w_up, v_w_down)
    shards = [[w[l].astype(BF16) for w in big_w] for l in range(DEPTH)]

    wi0 = _full_w_in(_exchange(shards[0][:1], False, "gather_w_in0")[0])
    x1, s0, g1 = _layer_fwd(x.reshape(t, D_MODEL), wi0, None, tuple(w[0] for w in small_w), 0,
                            ride_attn=Exchange(shards[0][1:], False), ride_mlp=Exchange(shards[1], False))
    x2, s1, _ = _layer_fwd(x1, _full_w_in(g1[0]), tuple(g1[1:]), tuple(w[1] for w in small_w), 1)
    loss_local, dx, dfinal = _loss_grad(x2, final_norm[None, :], loss_target.reshape(t, D_MODEL), tm, "loss_grad")
    loss = lax.psum(loss_local[0, 0], MESH_AXES)

    dx, parts1, small1, _ = _layer_bwd(dx, s1, 1)
    early = jnp.concatenate([_pack_small_layer(small1), dfinal.reshape(-1, 128)])
    dx, parts0, small0, recv1 = _layer_bwd(dx, s0, 0, ride_mlp=Exchange(list(parts1) + [early], [True] * 4 + [False]),
                                           scatter_in_attn=True)
    grad_x = dx.reshape(x.shape)
    recv_in0, late = _exchange([parts0[0], _pack_small_layer(small0)], [True, False], "scatter_w_in0_gather_small0")
    small_all = jnp.concatenate([late, recv1[4]], axis=1)
    received = [[recv_in0] + list(parts0[1:]), recv1[:4]]

    big = [None] * 4
    for l in reversed(range(DEPTH)):
        for k in range(4):
            big[k] = _reduce_adamw(received[l][k], big_w[k], big_m[k], big_v[k], l, big[k], f"adamw{k}_{l}")

    sm = _reduce_adamw(
        small_all,
        _pack_small(small_w, final_norm)[None],
        _pack_small([m_norm1, m_pool_w, m_pool_scale, m_sg_norm, m_sg_w, m_sg_b, m_norm2], m_final_norm)[None],
        _pack_small([v_norm1, v_pool_w, v_pool_scale, v_sg_norm, v_sg_w, v_sg_b, v_norm2], v_final_norm)[None],
        0, None, "adamw_replicated")

    out = [loss, grad_x]
    for k in range(4):
        n1, pw, ps, sn, sw, sb, n2, fn = _unpack_small(sm[k][0])
        out += [n1, big[0][k], pw, ps, sn, sw, sb, big[1][k], n2, big[2][k], big[3][k], fn]
    return tuple(out)
```

```python
import functools

import jax
import jax.numpy as jnp
from jax import lax
from jax.experimental import pallas as pl
from jax.experimental.pallas import tpu as pltpu

F32 = jnp.float32
BF16 = jnp.bfloat16

D_MODEL = 1024
DEPTH = 2
POOL_WIDTH = 256
SG_WIDTH = 256
SB_WIDTH = 512
POOL_WINDOWS = (2, 4, 8, 16)
POOL_GW = 64
POOL_HALO = 16
CHUNK = 128
SG_HEADS = 4
SG_HD = 64
SB_HD = 64
SB_PAIRS = SB_WIDTH // 128
AB_COLS = POOL_WIDTH + 2 * SG_WIDTH
IN_COLS = AB_COLS + 3 * SB_WIDTH
D_FF = 4096
EPS = 1e-6
N_DEV = 8
FF_SHARD = D_FF // N_DEV
IN_SHARD = IN_COLS // N_DEV
OUT_SHARD = D_MODEL // N_DEV
ADAM_LR = 0.001
ADAM_B1 = 0.9
ADAM_B2 = 0.999
ADAM_EPS = 1e-08
ADAM_WD = 0.01
ADAM_STEP = 10
VMEM_LIMIT = 56 * 1024 * 1024
MLP_ROWS = 1024
MESH_AXES = ("x", "y", "c")


def _dot(a, b):
    return jnp.dot(a, b, preferred_element_type=F32)


def _dot_nt(a, b):
    return lax.dot_general(a, b, (((1,), (1,)), ((), ())), preferred_element_type=F32)


def _dot_tn(a, b):
    return lax.dot_general(a, b, (((0,), (0,)), ((), ())), preferred_element_type=F32)


def _rstd(x):
    return lax.rsqrt(jnp.mean(x * x, axis=-1, keepdims=True) + EPS)


def _rms_bwd(x, r, g, dh):
    gq = dh * g
    dx = r * gq - x * (r * r * r) * jnp.mean(gq * x, axis=-1, keepdims=True)
    dg = jnp.sum(dh * x * r, axis=0, keepdims=True)
    return dx, dg


def _params(sem=None):
    kw = dict(vmem_limit_bytes=VMEM_LIMIT)
    if sem is not None:
        kw["dimension_semantics"] = sem
    return pltpu.CompilerParams(**kw)


def _row_tile(rows, cap):
    best = 8
    for t in range(8, min(rows, cap) + 1, 8):
        if rows % t == 0:
            best = t
    return best


def _peer(k):
    x, y, c = lax.axis_index("x"), lax.axis_index("y"), lax.axis_index("c")
    return (1 - x if k & 4 else x, 1 - y if k & 2 else y, 1 - c if k & 1 else c)


def _my_index():
    return 4 * lax.axis_index("x") + 2 * lax.axis_index("y") + lax.axis_index("c")


class Exchange:
    def __init__(self, arrs, scatter):
        self.arrs = list(arrs)
        self.scatter = list(scatter) if isinstance(scatter, (list, tuple)) else [scatter] * len(self.arrs)
        self.n = len(self.arrs)
        self.any_specs = [pl.BlockSpec(memory_space=pl.ANY)] * self.n
        self.out_shape = [jax.ShapeDtypeStruct((N_DEV,) + a.shape[-2:], a.dtype) for a in self.arrs]
        self.sems = [pltpu.SemaphoreType.DMA((self.n, N_DEV - 1)), pltpu.SemaphoreType.DMA((self.n, N_DEV - 1)),
                     pltpu.SemaphoreType.DMA((self.n,))]

    def _copies(self, ins, outs, sems):
        send_sems, recv_sems, local_sems = sems
        me = _my_index()
        local, remote = [], []
        for a in range(self.n):
            sc = self.scatter[a]
            local.append(pltpu.make_async_copy(ins[a].at[me] if sc else ins[a], outs[a].at[me], local_sems.at[a]))
            for k in range(1, N_DEV):
                px, py, pc = _peer(k)
                remote.append(pltpu.make_async_remote_copy(
                    src_ref=ins[a].at[4 * px + 2 * py + pc] if sc else ins[a], dst_ref=outs[a].at[me],
                    send_sem=send_sems.at[a, k - 1], recv_sem=recv_sems.at[a, k - 1],
                    device_id=(px, py, pc), device_id_type=pl.DeviceIdType.MESH))
        return local, remote

    def start(self, ins, outs, sems):
        local, remote = self._copies(ins, outs, sems)
        for cp in local + remote:
            cp.start()

    def wait(self, ins, outs, sems):
        local, remote = self._copies(ins, outs, sems)
        for cp in remote:
            cp.wait_recv()
        for cp in remote:
            cp.wait_send()
        for cp in local:
            cp.wait()

    def alone(self, name):
        n = self.n

        def body(*refs):
            ins, outs, sems = refs[:n], refs[n:2 * n], refs[2 * n:]
            self.start(ins, outs, sems)
            self.wait(ins, outs, sems)

        return pl.pallas_call(body, name=name, out_shape=self.out_shape, in_specs=self.any_specs,
                              out_specs=self.any_specs, scratch_shapes=self.sems)(*self.arrs)


def _exchange(arrs, scatter, name):
    return Exchange(arrs, scatter).alone(name)


def _call(body, name, grid, in_specs, out_specs, out_shape, scratch_shapes, semantics, args, ride=None):
    if ride is None:
        return pl.pallas_call(body, name=name, grid=grid, in_specs=in_specs, out_specs=out_specs, out_shape=out_shape,
                              scratch_shapes=scratch_shapes, compiler_params=_params(semantics))(*args)
    single = not isinstance(out_shape, (list, tuple))
    out_specs, out_shape = ([out_specs], [out_shape]) if single else (list(out_specs), list(out_shape))
    n_in, n_out, n_scr, n = len(in_specs), len(out_specs), len(scratch_shapes), ride.n

    def riding(*refs):
        ins, cins = refs[:n_in], refs[n_in:n_in + n]
        outs, couts = refs[n_in + n:n_in + n + n_out], refs[n_in + n + n_out:n_in + 2 * n + n_out]
        scr, sems = refs[n_in + 2 * n + n_out:n_in + 2 * n + n_out + n_scr], refs[n_in + 2 * n + n_out + n_scr:]
        first = functools.reduce(lambda p, q: p & q, [pl.program_id(a) == 0 for a in range(len(grid))])
        last = functools.reduce(lambda p, q: p & q, [pl.program_id(a) == g - 1 for a, g in enumerate(grid)])

        @pl.when(first)
        def _():
            ride.start(cins, couts, sems)

        body(*ins, *outs, *scr)

        @pl.when(last)
        def _():
            ride.wait(cins, couts, sems)

    res = pl.pallas_call(
        riding, name=name, grid=grid, in_specs=list(in_specs) + ride.any_specs, out_specs=out_specs + ride.any_specs,
        out_shape=out_shape + ride.out_shape, scratch_shapes=list(scratch_shapes) + ride.sems,
        compiler_params=_params(("arbitrary",) * len(grid)))(*args, *ride.arrs)
    own = res[0] if single else list(res[:n_out])
    return own, list(res[n_out:])


def _reduce_adamw(parts, w, m, v, l, prev, name):
    _, rows, n = parts.shape
    tr = _row_tile(rows, max(8, (1 << 18) // n))
    c1 = 1.0 - ADAM_B1 ** ADAM_STEP
    c2 = 1.0 - ADAM_B2 ** ADAM_STEP

    def body(p_ref, w_ref, m_ref, v_ref, *rest):
        g_ref, d_ref, nm_ref, nv_ref = rest[-4:]
        g = p_ref[0].astype(F32)
        for s in range(1, N_DEV):
            g = g + p_ref[s].astype(F32)
        nm = ADAM_B1 * m_ref[...] + (1.0 - ADAM_B1) * g
        nv = ADAM_B2 * v_ref[...] + (1.0 - ADAM_B2) * (g * g)
        m_hat = nm / c1
        v_hat = nv / c2
        g_ref[...] = g
        d_ref[...] = -ADAM_LR * (m_hat / (jnp.sqrt(v_hat) + ADAM_EPS) + ADAM_WD * w_ref[...])
        nm_ref[...] = nm
        nv_ref[...] = nv

    blk = pl.BlockSpec((None, tr, n), lambda i: (l, i, 0))
    out = jax.ShapeDtypeStruct(w.shape, F32)
    prev = list(prev) if prev is not None else []
    return pl.pallas_call(
        body, name=name, grid=(rows // tr,),
        in_specs=[pl.BlockSpec((N_DEV, tr, n), lambda i: (0, i, 0)), blk, blk, blk] + [pl.BlockSpec(memory_space=pl.ANY)] * len(prev),
        out_specs=[blk, blk, blk, blk], out_shape=[out, out, out, out],
        input_output_aliases={4 + k: k for k in range(len(prev))},
        compiler_params=_params(("parallel",)),
    )(parts, w, m, v, *prev)


def _inproj_fwd(x, g, w, tm, tq, name):
    t, d = x.shape
    n = w.shape[1]
    nb = t // tq
    per = tm // tq

    def body(x_ref, g_ref, w_ref, ab_ref, qkv_ref, kt_ref, vt_ref):
        xx = x_ref[...]
        h = (xx * _rstd(xx) * g_ref[...]).astype(BF16)
        ab_ref[...] = _dot(h, w_ref[:, :AB_COLS])
        qkv = _dot(h, w_ref[:, AB_COLS:])
        qkv_ref[...] = qkv.astype(BF16)
        for which, out_ref in ((1, kt_ref), (2, vt_ref)):
            for p in range(SB_PAIRS):
                for b in range(per):
                    cols = which * SB_WIDTH + p * 128
                    out_ref[p, b] = qkv[b * tq:(b + 1) * tq, cols:cols + 128].T.astype(BF16)

    tb = pl.BlockSpec((SB_PAIRS, per, 128, tq), lambda i: (0, i, 0, 0))
    tshape = jax.ShapeDtypeStruct((SB_PAIRS, nb, 128, tq), BF16)
    return pl.pallas_call(
        body, name=name, grid=(t // tm,),
        in_specs=[pl.BlockSpec((tm, d), lambda i: (i, 0)), pl.BlockSpec((1, d), lambda i: (0, 0)),
                  pl.BlockSpec((d, n), lambda i: (0, 0))],
        out_specs=[pl.BlockSpec((tm, AB_COLS), lambda i: (i, 0)), pl.BlockSpec((tm, n - AB_COLS), lambda i: (i, 0)), tb, tb],
        out_shape=[jax.ShapeDtypeStruct((t, AB_COLS), F32), jax.ShapeDtypeStruct((t, n - AB_COLS), BF16), tshape, tshape],
        compiler_params=_params(("parallel",)),
    )(x, g, w)


def _pool_window_sums(xx, forward):
    n = xx.shape[0]
    sh = (lambda k: n - k) if forward else (lambda k: k)
    s2 = xx + pltpu.roll(xx, sh(1), 0)
    s4 = s2 + pltpu.roll(s2, sh(2), 0)
    s8 = s4 + pltpu.roll(s4, sh(4), 0)
    s16 = s8 + pltpu.roll(s8, sh(8), 0)
    grp = lax.broadcasted_iota(jnp.int32, (1, POOL_WIDTH), 1) // POOL_GW
    return jnp.where(grp == 0, s2, jnp.where(grp == 1, s4, jnp.where(grp == 2, s8, s16)))


def _pool_count(t0, rows):
    grp = lax.broadcasted_iota(jnp.int32, (1, POOL_WIDTH), 1) // POOL_GW
    win = jnp.where(grp == 0, 2, jnp.where(grp == 1, 4, jnp.where(grp == 2, 8, 16)))
    tt = t0 + lax.broadcasted_iota(jnp.int32, (rows, 1), 0)
    return jnp.minimum(tt + 1, win).astype(F32)


def _pool_diff(cur, prev, t0):
    tm = cur.shape[0]
    sums = _pool_window_sums(jnp.concatenate([prev, cur], axis=0), False)[POOL_HALO:]
    return sums / _pool_count(t0, tm) - cur


def _pool_fwd(ab, wbd, scale, tm, name):
    t = ab.shape[0]
    hb = tm // POOL_HALO

    def body(cur_ref, prev_ref, w_ref, s_ref, y_ref):
        i = pl.program_id(0)
        prev = jnp.where(i == 0, 0.0, prev_ref[...])
        d = _pool_diff(cur_ref[...], prev, i * tm).astype(BF16)
        y_ref[...] = (_dot(d, w_ref[...]) * s_ref[...]).astype(BF16)

    return pl.pallas_call(
        body, name=name, grid=(t // tm,),
        in_specs=[pl.BlockSpec((tm, POOL_WIDTH), lambda i: (i, 0)),
                  pl.BlockSpec((POOL_HALO, POOL_WIDTH), lambda i: (jnp.maximum(i * hb - 1, 0), 0)),
                  pl.BlockSpec((POOL_WIDTH, POOL_WIDTH), lambda i: (0, 0)),
                  pl.BlockSpec((1, POOL_WIDTH), lambda i: (0, 0))],
        out_specs=pl.BlockSpec((tm, POOL_WIDTH), lambda i: (i, 0)),
        out_shape=jax.ShapeDtypeStruct((t, POOL_WIDTH), BF16),
        compiler_params=_params(("parallel",)),
    )(ab, ab, wbd, scale)


_GELU_K = 0.7978845608028654
_GELU_A = 0.044715


def _gelu(x):
    return 0.5 * x * (1.0 + jnp.tanh(_GELU_K * (x + _GELU_A * (x * x * x))))


def _gelu_grad(x):
    th = jnp.tanh(_GELU_K * (x + _GELU_A * (x * x * x)))
    return 0.5 * (1.0 + th) + 0.5 * x * (1.0 - th * th) * (_GELU_K * (1.0 + 3.0 * _GELU_A * (x * x)))


def _head_lanes(h):
    return lax.broadcasted_iota(jnp.int32, (1, SG_WIDTH), 1) // SG_HD == h


def _sg_fwd(ab, gn, wm, bfull, tm, name):
    t = ab.shape[0]

    def body(u_ref, v_ref, gn_ref, wm_ref, b_ref, y_ref):
        v = _gelu(v_ref[...])
        vn = (v * _rstd(v) * gn_ref[...]).astype(BF16)
        for c in range(tm // CHUNK):
            rows = slice(c * CHUNK, (c + 1) * CHUNK)
            vc = vn[rows]
            sv = b_ref[...]
            for h in range(SG_HEADS):
                sv = sv + jnp.where(_head_lanes(h), _dot(wm_ref[h], vc), 0.0)
            y_ref[rows, :] = (_gelu(u_ref[rows, :]) * sv).astype(BF16)

    return pl.pallas_call(
        body, name=name, grid=(t // tm,),
        in_specs=[pl.BlockSpec((tm, SG_WIDTH), lambda i: (i, 1)), pl.BlockSpec((tm, SG_WIDTH), lambda i: (i, 2)),
                  pl.BlockSpec((1, SG_WIDTH), lambda i: (0, 0)),
                  pl.BlockSpec((SG_HEADS, CHUNK, CHUNK), lambda i: (0, 0, 0)),
                  pl.BlockSpec((CHUNK, SG_WIDTH), lambda i: (0, 0))],
        out_specs=pl.BlockSpec((tm, SG_WIDTH), lambda i: (i, 0)),
        out_shape=jax.ShapeDtypeStruct((t, SG_WIDTH), BF16),
        compiler_params=_params(("parallel",)),
    )(ab, ab, gn, wm, bfull)


LOG2E = 1.4426950408889634
SB_SCALE = 0.125 * LOG2E
SB_DEAD_LOG2 = 152.0


def _log2_sigmoids(y):
    neg_abs = lax.bitcast_convert_type(lax.bitcast_convert_type(y, jnp.uint32) | jnp.uint32(0x80000000), F32)
    lb = jnp.minimum(y, 0.0) - jnp.log(1.0 + jnp.exp2(neg_abs)) * LOG2E
    return lb, lb - y


def _split(x):
    hi = x.astype(BF16)
    return hi, (x - hi.astype(F32)).astype(BF16)


def _tri_dot(tri, x):
    hi, lo = _split(x)
    return _dot(tri, hi) + _dot(tri, lo)


def _sba_fwd(qkv, vtb, tq, name, ride=None):
    t = qkv.shape[0]
    nb = t // tq
    upper = (jnp.arange(tq)[None, :] > jnp.arange(tq)[:, None]).astype(BF16)

    def body(q_ref, k_ref, vt_ref, up_ref, ot_ref, c_ref, n_ref):
        i = pl.program_id(1)
        q = q_ref[...]
        up = up_ref[...]
        lane_head = lax.broadcasted_iota(jnp.int32, (1, 128), 1) // SB_HD
        sub_head = lax.broadcasted_iota(jnp.int32, (128, 1), 0) // SB_HD
        causal = (lax.broadcasted_iota(jnp.int32, (tq, tq), 0) < lax.broadcasted_iota(jnp.int32, (tq, tq), 1))
        qh = [jnp.where(lane_head == h, q, jnp.zeros_like(q)) for h in range(2)]

        def blocks(js, carry):
            cs, acc = list(carry[:2]), carry[2]
            kj = [k_ref[pl.ds(pl.multiple_of(j * tq, tq), tq), :] for j, _ in js]
            vt = [vt_ref[j] for j, _ in js]
            chains = [(b, h) for b in range(len(js)) for h in range(2)]
            z = [_dot_nt(kj[b], qh[h]) for b, h in chains]
            ls = [_log2_sigmoids(zz * SB_SCALE) for zz in z]
            lb = [x[0] for x in ls]
            l1 = [jnp.where(causal, x[1], 0.0) if js[b][1] else x[1] for x, (b, h) in zip(ls, chains)]
            after = [_tri_dot(up, x) for x in l1]
            a = []
            for n, (b, h) in enumerate(chains):
                c_ref[h, js[b][0]] = cs[h]
                an = jnp.exp2(lb[n] + after[n] + cs[h])
                a.append(jnp.where(causal, an, 0.0) if js[b][1] else an)
                cs[h] = cs[h] + after[n][0:1, :] + l1[n][0:1, :]
            for n, (b, h) in enumerate(chains):
                acc = acc + _dot(jnp.where(sub_head == h, vt[b], jnp.zeros_like(vt[b])), a[n].astype(BF16))
            return cs[0], cs[1], acc

        def alive(c0, c1):
            return (jnp.max(jnp.maximum(c0, c1)) > -SB_DEAD_LOG2).astype(jnp.int32)

        def step(state):
            s, _, c0, c1, acc = state
            c0, c1, acc = blocks([(i - 1 - s, False)], (c0, c1, acc))
            return s + 1, alive(c0, c1), c0, c1, acc

        zero = jnp.zeros((1, tq), F32)
        start = (zero, zero, jnp.zeros((128, tq), F32))
        c0, c1, acc = lax.cond(i > 0, lambda: blocks([(i, True), (i - 1, False)], start), lambda: blocks([(i, True)], start))
        first = jnp.minimum(i, 1)
        state = lax.while_loop(lambda st: (st[0] < i) & (st[1] > 0), step, (first, alive(c0, c1), c0, c1, acc))
        ot_ref[...] = state[4].astype(BF16)
        n_ref[pl.program_id(0), i] = (state[0] + 1).astype(F32)

    return _call(
        body, name, (SB_PAIRS, nb),
        [pl.BlockSpec((tq, 128), lambda p, i: (i, p)),
         pl.BlockSpec((t, 128), lambda p, i: (0, SB_PAIRS + p)),
         pl.BlockSpec((None, nb, 128, tq), lambda p, i: (p, 0, 0, 0)),
         pl.BlockSpec((tq, tq), lambda p, i: (0, 0))],
        [pl.BlockSpec((128, tq), lambda p, i: (p, i)),
         pl.BlockSpec((2, nb, 1, tq), lambda p, i: (p, 0, 0, i)),
         pl.BlockSpec(memory_space=pltpu.SMEM)],
        [jax.ShapeDtypeStruct((SB_WIDTH, t), BF16),
         jax.ShapeDtypeStruct((2 * SB_PAIRS, nb, 1, t), F32),
         jax.ShapeDtypeStruct((SB_PAIRS, nb), F32)],
        [], ("arbitrary", "arbitrary"), (qkv, qkv, vtb, upper), ride)


def _outproj_mlp_fwd(x, ya, yb, yct, wo, g2, wup, wdn, tm, name, ride=None):
    t, d = x.shape
    nf = wup.shape[0]

    def body(x_ref, ya_ref, yb_ref, yct_ref, wo_ref, g_ref, wu_ref, wd_ref, x1_ref, u_ref, x2_ref, h_ref, acc_ref):
        j = pl.program_id(1)

        @pl.when(j == 0)
        def _():
            x1 = (x_ref[...] + _dot(ya_ref[...], wo_ref[0:POOL_WIDTH, :])
                  + _dot(yb_ref[...], wo_ref[POOL_WIDTH:POOL_WIDTH + SG_WIDTH, :])
                  + _dot_tn(yct_ref[...], wo_ref[POOL_WIDTH + SG_WIDTH:, :]))
            x1_ref[...] = x1
            h_ref[...] = (x1 * _rstd(x1) * g_ref[...]).astype(BF16)
            acc_ref[...] = x1

        u = _dot(h_ref[...], wu_ref[...])
        u_ref[...] = u
        r = jnp.maximum(u, 0.0)
        acc_ref[...] += _dot((r * r).astype(BF16), wd_ref[...])

        @pl.when(j == nf - 1)
        def _():
            x2_ref[...] = acc_ref[...]

    row = lambda w: pl.BlockSpec((tm, w), lambda i, j: (i, 0))
    return _call(
        body, name, (t // tm, nf),
        [row(d), row(POOL_WIDTH), row(SG_WIDTH), pl.BlockSpec((SB_WIDTH, tm), lambda i, j: (0, i)),
         pl.BlockSpec((d, d), lambda i, j: (0, 0)), pl.BlockSpec((1, d), lambda i, j: (0, 0)),
         pl.BlockSpec((None, d, FF_SHARD), lambda i, j: (j, 0, 0)),
         pl.BlockSpec((None, FF_SHARD, d), lambda i, j: (j, 0, 0))],
        [row(d), pl.BlockSpec((tm, FF_SHARD), lambda i, j: (i, j)), row(d)],
        [jax.ShapeDtypeStruct((t, d), F32), jax.ShapeDtypeStruct((t, nf * FF_SHARD), F32),
         jax.ShapeDtypeStruct((t, d), F32)],
        [pltpu.VMEM((tm, d), BF16), pltpu.VMEM((tm, d), F32)],
        ("parallel", "arbitrary"), (x, ya, yb, yct, wo, g2, wup, wdn), ride)


def _loss_grad(x, g, target, tm, name):
    t, d = x.shape
    nt = t // tm

    def body(x_ref, g_ref, t_ref, loss_ref, dx_ref, dg_ref, sq_ref):
        i = pl.program_id(0)

        @pl.when(i == 0)
        def _():
            sq_ref[...] = jnp.zeros_like(sq_ref)
            dg_ref[...] = jnp.zeros_like(dg_ref)

        xx = x_ref[...]
        r = _rstd(xx)
        err = xx * r * g_ref[...] - t_ref[...]
        sq_ref[...] += jnp.sum(err * err, axis=0, keepdims=True)
        dx, dg = _rms_bwd(xx, r, g_ref[...], err * (1.0 / d))
        dx_ref[...] = dx
        dg_ref[...] += dg

        @pl.when(i == nt - 1)
        def _():
            loss_ref[...] = jnp.sum(sq_ref[...], axis=1, keepdims=True) * (0.5 / d)

    return pl.pallas_call(
        body, name=name, grid=(nt,),
        in_specs=[pl.BlockSpec((tm, d), lambda i: (i, 0)), pl.BlockSpec((1, d), lambda i: (0, 0)),
                  pl.BlockSpec((tm, d), lambda i: (i, 0))],
        out_specs=[pl.BlockSpec((1, 1), lambda i: (0, 0)), pl.BlockSpec((tm, d), lambda i: (i, 0)),
                   pl.BlockSpec((1, d), lambda i: (0, 0))],
        out_shape=[jax.ShapeDtypeStruct((1, 1), F32), jax.ShapeDtypeStruct((t, d), F32),
                   jax.ShapeDtypeStruct((1, d), F32)],
        scratch_shapes=[pltpu.VMEM((1, d), F32)],
        compiler_params=_params(("arbitrary",)),
    )(x, g, target)


def _mlp_bwd(dx2, x1, g2, u, wup, wdn, tm, name, ride=None):
    t, d = dx2.shape
    nf = wup.shape[0]
    nt = t // tm

    def body(dx2_ref, x1_ref, g_ref, u_ref, wu_ref, wd_ref, dx1_ref, du_ref, r_ref, h_ref, dxb_ref, dg_ref, acc_ref):
        i, j = pl.program_id(0), pl.program_id(1)

        @pl.when(j == 0)
        def _():
            x1 = x1_ref[...]
            h_ref[...] = (x1 * _rstd(x1) * g_ref[...]).astype(BF16)
            dxb_ref[...] = dx2_ref[...].astype(BF16)
            acc_ref[...] = jnp.zeros_like(acc_ref)

        @pl.when((i == 0) & (j == 0))
        def _():
            dg_ref[...] = jnp.zeros_like(dg_ref)

        dr = _dot_nt(dxb_ref[...], wd_ref[...])
        ru = jnp.maximum(u_ref[...], 0.0)
        du = (dr * (2.0 * ru)).astype(BF16)
        du_ref[...] = du
        r_ref[...] = (ru * ru).astype(BF16)
        acc_ref[...] += _dot_nt(du, wu_ref[...])

        @pl.when(j == nf - 1)
        def _():
            x1 = x1_ref[...]
            dx, dg = _rms_bwd(x1, _rstd(x1), g_ref[...], acc_ref[...])
            dx1_ref[...] = dx2_ref[...] + dx
            dg_ref[...] += dg

    row = lambda w: pl.BlockSpec((tm, w), lambda i, j: (i, 0))
    col = pl.BlockSpec((tm, FF_SHARD), lambda i, j: (i, j))
    return _call(
        body, name, (nt, nf),
        [row(d), row(d), pl.BlockSpec((1, d), lambda i, j: (0, 0)), col,
         pl.BlockSpec((None, d, FF_SHARD), lambda i, j: (j, 0, 0)),
         pl.BlockSpec((None, FF_SHARD, d), lambda i, j: (j, 0, 0))],
        [row(d), col, col, row(d), row(d), pl.BlockSpec((1, d), lambda i, j: (0, 0))],
        [jax.ShapeDtypeStruct((t, d), F32), jax.ShapeDtypeStruct((t, nf * FF_SHARD), BF16),
         jax.ShapeDtypeStruct((t, nf * FF_SHARD), BF16), jax.ShapeDtypeStruct((t, d), BF16),
         jax.ShapeDtypeStruct((t, d), BF16), jax.ShapeDtypeStruct((1, d), F32)],
        [pltpu.VMEM((tm, d), F32)], ("arbitrary", "arbitrary"), (dx2, x1, g2, u, wup, wdn), ride)


def _matmul_tn(a, b, bm, bn, bt, name, by_column_block=False, a_transposed=False):
    m, t = a.shape if a_transposed else a.shape[::-1]
    n = b.shape[1]
    nk = t // bt

    def body(a_ref, b_ref, o_ref, acc_ref):
        k = pl.program_id(2)

        @pl.when(k == 0)
        def _():
            acc_ref[...] = jnp.zeros_like(acc_ref)

        acc_ref[...] += _dot(a_ref[...], b_ref[...]) if a_transposed else _dot_tn(a_ref[...], b_ref[...])

        @pl.when(k == nk - 1)
        def _():
            o_ref[...] = acc_ref[...].astype(BF16)

    if by_column_block:
        out_spec = pl.BlockSpec((None, bm, bn), lambda i, j, k: (j, i, 0))
        out_shape = jax.ShapeDtypeStruct((n // bn, m, bn), BF16)
    else:
        out_spec = pl.BlockSpec((bm, bn), lambda i, j, k: (i, j))
        out_shape = jax.ShapeDtypeStruct((m, n), BF16)
    return pl.pallas_call(
        body, name=name, grid=(m // bm, n // bn, nk),
        in_specs=[pl.BlockSpec((bm, bt), lambda i, j, k: (i, k)) if a_transposed else pl.BlockSpec((bt, bm), lambda i, j, k: (k, i)),
                  pl.BlockSpec((bt, bn), lambda i, j, k: (k, j))],
        out_specs=out_spec, out_shape=out_shape,
        scratch_shapes=[pltpu.VMEM((bm, bn), F32)],
        compiler_params=_params(("parallel", "parallel", "arbitrary")),
    )(a, b)


def _outproj_bwd(dx1, wo, tm, name):
    t, d = dx1.shape
    c2 = POOL_WIDTH + SG_WIDTH

    def body(dx_ref, wo_ref, dya_ref, dyb_ref, dyc_ref, dyct_ref, dxb_ref):
        dxb = dx_ref[...].astype(BF16)
        dxb_ref[...] = dxb
        dya_ref[...] = _dot_nt(dxb, wo_ref[0:POOL_WIDTH, :])
        dyb_ref[...] = _dot_nt(dxb, wo_ref[POOL_WIDTH:c2, :])
        dyc_ref[...] = _dot_nt(dxb, wo_ref[c2:, :]).astype(BF16)
        dyct_ref[...] = _dot_nt(wo_ref[c2:, :], dxb).astype(BF16)

    row = lambda w: pl.BlockSpec((tm, w), lambda i: (i, 0))
    return pl.pallas_call(
        body, name=name, grid=(t // tm,),
        in_specs=[row(d), pl.BlockSpec((d, d), lambda i: (0, 0))],
        out_specs=[row(POOL_WIDTH), row(SG_WIDTH), row(SB_WIDTH), pl.BlockSpec((SB_WIDTH, tm), lambda i: (0, i)), row(d)],
        out_shape=[jax.ShapeDtypeStruct((t, POOL_WIDTH), F32), jax.ShapeDtypeStruct((t, SG_WIDTH), F32),
                   jax.ShapeDtypeStruct((t, SB_WIDTH), BF16), jax.ShapeDtypeStruct((SB_WIDTH, t), BF16),
                   jax.ShapeDtypeStruct((t, d), BF16)],
        compiler_params=_params(("parallel",)),
    )(dx1, wo)


def _sba_bwd(qkv, ktb, dyc, dyct, cc, nvis, tq, name, ride=None):
    t = qkv.shape[0]
    nb = t // tq
    idx = jnp.arange(tq)
    upper = (idx[None, :] > idx[:, None]).astype(BF16)
    lower = (idx[None, :] < idx[:, None]).astype(BF16)

    def body(q_ref, k_ref, v_ref, kt_ref, do_ref, dot_ref, c_ref, n_ref, up_ref, lo_ref, dqt_ref, dk_ref, dv_ref):
        i = pl.program_id(1)

        @pl.when(i == 0)
        def _():
            dk_ref[...] = jnp.zeros_like(dk_ref)
            dv_ref[...] = jnp.zeros_like(dv_ref)

        q = q_ref[...]
        do = do_ref[...]
        dot = dot_ref[...]
        up = up_ref[...]
        lo = lo_ref[...]
        lane_head = lax.broadcasted_iota(jnp.int32, (1, 128), 1) // SB_HD
        sub_head = lax.broadcasted_iota(jnp.int32, (128, 1), 0) // SB_HD
        causal = (lax.broadcasted_iota(jnp.int32, (tq, tq), 0) < lax.broadcasted_iota(jnp.int32, (tq, tq), 1))
        hms = [lane_head == h for h in range(2)]
        qh = [jnp.where(hm, q, jnp.zeros_like(q)) for hm in hms]
        qs = [x * 0.125 for x in qh]
        doh = [jnp.where(hm, do, jnp.zeros_like(do)) for hm in hms]

        def blocks(js, carry):
            cgs, dqt = list(carry[:2]), carry[2]
            rows = [pl.ds(pl.multiple_of(j * tq, tq), tq) for j, _ in js]
            kj = [k_ref[r, :] for r in rows]
            vj = [v_ref[r, :] for r in rows]
            kt = [kt_ref[j] for j, _ in js]
            chains = [(b, h) for b in range(len(js)) for h in range(2)]
            z = [_dot_nt(kj[b], qh[h]) for b, h in chains]
            da = [_dot(jnp.where(hms[h], vj[b], jnp.zeros_like(vj[b])), dot) for b, h in chains]
            ls = [_log2_sigmoids(zz * SB_SCALE) for zz in z]
            lb = [x[0] for x in ls]
            l1 = [jnp.where(causal, x[1], 0.0) if js[b][1] else x[1] for x, (b, h) in zip(ls, chains)]
            after = [_tri_dot(up, x) for x in l1]
            a = [jnp.exp2(lb[n] + after[n] + c_ref[h, js[b][0]]) for n, (b, h) in enumerate(chains)]
            a = [jnp.where(causal, a[n], 0.0) if js[b][1] else a[n] for n, (b, h) in enumerate(chains)]
            g = [a[n] * da[n] for n in range(len(chains))]
            gloc = [_tri_dot(lo, x) for x in g]
            dzb = []
            for n, (b, h) in enumerate(chains):
                gsum = gloc[n] + cgs[h]
                dz = g[n] - jnp.exp2(lb[n]) * (g[n] + gsum)
                dzb.append((jnp.where(causal, dz, 0.0) if js[b][1] else dz).astype(BF16))
                cgs[h] = gsum[tq - 1:tq, :] + g[n][tq - 1:tq, :]
            ab = [x.astype(BF16) for x in a]
            for n, (b, h) in enumerate(chains):
                dqt = dqt + _dot(jnp.where(sub_head == h, kt[b], jnp.zeros_like(kt[b])), dzb[n])
            for b in range(len(js)):
                dk_ref[rows[b], :] += _dot(dzb[2 * b], qs[0]) + _dot(dzb[2 * b + 1], qs[1])
                dv_ref[rows[b], :] += _dot(ab[2 * b], doh[0]) + _dot(ab[2 * b + 1], doh[1])
            return cgs[0], cgs[1], dqt

        n = jnp.clip(n_ref[pl.program_id(0), i].astype(jnp.int32), 1, i + 1)
        zero = jnp.zeros((1, tq), F32)
        carry = lax.fori_loop(jnp.minimum(i + 1 - n, i - 1), i - 1, lambda s, cr: blocks([(s, False)], cr),
                              (zero, zero, jnp.zeros((128, tq), F32)))
        carry = lax.cond(i > 0, lambda: blocks([(i - 1, False), (i, True)], carry), lambda: blocks([(i, True)], carry))
        dqt_ref[...] = carry[2] * 0.125

    return _call(
        body, name, (SB_PAIRS, nb),
        [pl.BlockSpec((tq, 128), lambda p, i: (i, p)),
         pl.BlockSpec((t, 128), lambda p, i: (0, SB_PAIRS + p)),
         pl.BlockSpec((t, 128), lambda p, i: (0, 2 * SB_PAIRS + p)),
         pl.BlockSpec((None, nb, 128, tq), lambda p, i: (p, 0, 0, 0)),
         pl.BlockSpec((tq, 128), lambda p, i: (i, p)),
         pl.BlockSpec((128, tq), lambda p, i: (p, i)),
         pl.BlockSpec((2, nb, 1, tq), lambda p, i: (p, 0, 0, i)),
         pl.BlockSpec(memory_space=pltpu.SMEM),
         pl.BlockSpec((tq, tq), lambda p, i: (0, 0)),
         pl.BlockSpec((tq, tq), lambda p, i: (0, 0))],
        [pl.BlockSpec((128, tq), lambda p, i: (p, i)),
         pl.BlockSpec((t, 128), lambda p, i: (0, p)),
         pl.BlockSpec((t, 128), lambda p, i: (0, p))],
        [jax.ShapeDtypeStruct((SB_WIDTH, t), F32), jax.ShapeDtypeStruct((t, SB_WIDTH), F32),
         jax.ShapeDtypeStruct((t, SB_WIDTH), F32)],
        [], ("arbitrary", "arbitrary"), (qkv, qkv, qkv, ktb, dyc, dyct, cc, nvis, upper, lower), ride)


def _sg_bwd(ab, dyb, gn, wm, wmt, bfull, tm, name):
    t = ab.shape[0]
    nt = t // tm
    sel = (jnp.arange(SG_WIDTH)[:, None] // SG_HD == jnp.arange(CHUNK)[None, :]).astype(F32)

    def body(u_ref, v_ref, dy_ref, gn_ref, wm_ref, wmt_ref, b_ref, sel_ref,
             dup_ref, dvp_ref, dgn_ref, dw_ref, db_ref, dbacc_ref):
        i = pl.program_id(0)

        @pl.when(i == 0)
        def _():
            dgn_ref[...] = jnp.zeros_like(dgn_ref)
            dw_ref[...] = jnp.zeros_like(dw_ref)
            dbacc_ref[...] = jnp.zeros_like(dbacc_ref)

        tril = (lax.broadcasted_iota(jnp.int32, (CHUNK, CHUNK), 0) >= lax.broadcasted_iota(jnp.int32, (CHUNK, CHUNK), 1))
        gn_ = gn_ref[...]
        for c in range(tm // CHUNK):
            rows = slice(c * CHUNK, (c + 1) * CHUNK)
            up, vp, dy = u_ref[rows, :], v_ref[rows, :], dy_ref[rows, :]
            u, v = _gelu(up), _gelu(vp)
            r = _rstd(v)
            vn = (v * r * gn_).astype(BF16)
            sv = b_ref[...]
            for h in range(SG_HEADS):
                sv = sv + jnp.where(_head_lanes(h), _dot(wm_ref[h], vn), 0.0)
            dup_ref[rows, :] = dy * sv * _gelu_grad(up)
            dsv = dy * u
            dbacc_ref[...] += dsv
            dvn = jnp.zeros((CHUNK, SG_WIDTH), F32)
            for h in range(SG_HEADS):
                dsv_h = jnp.where(_head_lanes(h), dsv, 0.0).astype(BF16)
                dvn = dvn + _dot(wmt_ref[h], dsv_h)
                dw_ref[h] += jnp.where(tril, _dot_nt(dsv_h, vn), 0.0)
            dv, dgn = _rms_bwd(v, r, gn_, dvn)
            dgn_ref[...] += dgn
            dvp_ref[rows, :] = dv * _gelu_grad(vp)

        @pl.when(i == nt - 1)
        def _():
            db_ref[...] = jnp.dot(dbacc_ref[...], sel_ref[...], preferred_element_type=F32,
                                  precision=lax.Precision.HIGHEST)

    const = lambda shape: pl.BlockSpec(shape, lambda i: tuple(0 for _ in shape))
    return pl.pallas_call(
        body, name=name, grid=(nt,),
        in_specs=[pl.BlockSpec((tm, SG_WIDTH), lambda i: (i, 1)), pl.BlockSpec((tm, SG_WIDTH), lambda i: (i, 2)),
                  pl.BlockSpec((tm, SG_WIDTH), lambda i: (i, 0)), const((1, SG_WIDTH)),
                  const((SG_HEADS, CHUNK, CHUNK)), const((SG_HEADS, CHUNK, CHUNK)), const((CHUNK, SG_WIDTH)),
                  const((SG_WIDTH, CHUNK))],
        out_specs=[pl.BlockSpec((tm, SG_WIDTH), lambda i: (i, 0)), pl.BlockSpec((tm, SG_WIDTH), lambda i: (i, 0)),
                   const((1, SG_WIDTH)), const((SG_HEADS, CHUNK, CHUNK)), const((CHUNK, CHUNK))],
        out_shape=[jax.ShapeDtypeStruct((t, SG_WIDTH), F32), jax.ShapeDtypeStruct((t, SG_WIDTH), F32),
                   jax.ShapeDtypeStruct((1, SG_WIDTH), F32), jax.ShapeDtypeStruct((SG_HEADS, CHUNK, CHUNK), F32),
                   jax.ShapeDtypeStruct((CHUNK, CHUNK), F32)],
        scratch_shapes=[pltpu.VMEM((CHUNK, SG_WIDTH), F32)],
        compiler_params=_params(("arbitrary",)),
    )(ab, ab, dyb, gn, wm, wmt, bfull, sel)


def _pool_bwd(ab, dya, wbd, scale, tm, name):
    t = ab.shape[0]
    nt = t // tm
    hb = tm // POOL_HALO
    nh = t // POOL_HALO

    def body(cur_ref, prev_ref, dy_ref, dyn_ref, w_ref, s_ref, da_ref, dw_ref, ds_ref):
        i = pl.program_id(0)

        @pl.when(i == 0)
        def _():
            dw_ref[...] = jnp.zeros_like(dw_ref)
            ds_ref[...] = jnp.zeros_like(ds_ref)

        prev = jnp.where(i == 0, 0.0, prev_ref[...])
        d = _pool_diff(cur_ref[...], prev, i * tm).astype(BF16)
        dy = dy_ref[...]
        ds_ref[...] += jnp.sum(dy * _dot(d, w_ref[...]), axis=0, keepdims=True)
        dyn = jnp.where(i == nt - 1, 0.0, dyn_ref[...])
        dys = (jnp.concatenate([dy, dyn], axis=0) * s_ref[...]).astype(BF16)
        dw_ref[...] += _dot_tn(d, dys[:tm])
        dd = _dot_nt(dys, w_ref[...])
        fwd = _pool_window_sums(dd / _pool_count(i * tm, tm + POOL_HALO), True)
        da_ref[...] = fwd[:tm] - dd[:tm]

    return pl.pallas_call(
        body, name=name, grid=(nt,),
        in_specs=[pl.BlockSpec((tm, POOL_WIDTH), lambda i: (i, 0)),
                  pl.BlockSpec((POOL_HALO, POOL_WIDTH), lambda i: (jnp.maximum(i * hb - 1, 0), 0)),
                  pl.BlockSpec((tm, POOL_WIDTH), lambda i: (i, 0)),
                  pl.BlockSpec((POOL_HALO, POOL_WIDTH), lambda i: (jnp.minimum((i + 1) * hb, nh - 1), 0)),
                  pl.BlockSpec((POOL_WIDTH, POOL_WIDTH), lambda i: (0, 0)),
                  pl.BlockSpec((1, POOL_WIDTH), lambda i: (0, 0))],
        out_specs=[pl.BlockSpec((tm, POOL_WIDTH), lambda i: (i, 0)),
                   pl.BlockSpec((POOL_WIDTH, POOL_WIDTH), lambda i: (0, 0)),
                   pl.BlockSpec((1, POOL_WIDTH), lambda i: (0, 0))],
        out_shape=[jax.ShapeDtypeStruct((t, POOL_WIDTH), F32), jax.ShapeDtypeStruct((POOL_WIDTH, POOL_WIDTH), F32),
                   jax.ShapeDtypeStruct((1, POOL_WIDTH), F32)],
        compiler_params=_params(("arbitrary",)),
    )(ab, ab, dya, dya, wbd, scale)


def _inproj_bwd(dx1, x, g, da, dup, dvp, dqt, dk, dv, w, tm, name):
    t, d = x.shape
    n = w.shape[1]
    nt = t // tm

    def body(dx1_ref, x_ref, g_ref, da_ref, du_ref, dv_ref, dqt_ref, dk_ref, dvv_ref, w_ref,
             dx_ref, h_ref, dp_ref, dg_ref):
        @pl.when(pl.program_id(0) == 0)
        def _():
            dg_ref[...] = jnp.zeros_like(dg_ref)

        dp = jnp.concatenate([da_ref[...], du_ref[...], dv_ref[...], dqt_ref[...].T, dk_ref[...], dvv_ref[...]],
                             axis=1).astype(BF16)
        dp_ref[...] = dp
        xx = x_ref[...]
        r = _rstd(xx)
        h_ref[...] = (xx * r * g_ref[...]).astype(BF16)
        dx, dg = _rms_bwd(xx, r, g_ref[...], _dot_nt(dp, w_ref[...]))
        dx_ref[...] = dx1_ref[...] + dx
        dg_ref[...] += dg

    row = lambda w_: pl.BlockSpec((tm, w_), lambda i: (i, 0))
    return pl.pallas_call(
        body, name=name, grid=(nt,),
        in_specs=[row(d), row(d), pl.BlockSpec((1, d), lambda i: (0, 0)), row(POOL_WIDTH), row(SG_WIDTH),
                  row(SG_WIDTH), pl.BlockSpec((SB_WIDTH, tm), lambda i: (0, i)), row(SB_WIDTH), row(SB_WIDTH),
                  pl.BlockSpec((d, n), lambda i: (0, 0))],
        out_specs=[row(d), row(d), row(n), pl.BlockSpec((1, d), lambda i: (0, 0))],
        out_shape=[jax.ShapeDtypeStruct((t, d), F32), jax.ShapeDtypeStruct((t, d), BF16),
                   jax.ShapeDtypeStruct((t, n), BF16), jax.ShapeDtypeStruct((1, d), F32)],
        compiler_params=_params(("arbitrary",)),
    )(dx1, x, g, da, dup, dvp, dqt, dk, dv, w)


def _full_w_in(gathered):
    return gathered.transpose(1, 0, 2).reshape(D_MODEL, IN_COLS)


_SMALL_SHAPES = ((D_MODEL,), (4, POOL_GW, POOL_GW), (POOL_WIDTH,), (SG_WIDTH,), (SG_HEADS, CHUNK, CHUNK),
                 (SG_HEADS, CHUNK), (D_MODEL,))
_SMALL_SIZES = tuple(functools.reduce(lambda p, q: p * q, shp) for shp in _SMALL_SHAPES)
_SMALL_ROWS = sum(_SMALL_SIZES) // 128


def _pack_small_layer(arrs):
    return jnp.concatenate([a.reshape(-1) for a in arrs]).reshape(_SMALL_ROWS, 128)


def _pack_small(arrs, final):
    return jnp.concatenate([_pack_small_layer([a[l] for a in arrs]) for l in range(DEPTH)] + [final.reshape(-1, 128)])


def _unpack_small(buf):
    per_layer = []
    for l in range(DEPTH):
        flat, off, outs = buf[l * _SMALL_ROWS:(l + 1) * _SMALL_ROWS].reshape(-1), 0, []
        for shp, size in zip(_SMALL_SHAPES, _SMALL_SIZES):
            outs.append(flat[off:off + size].reshape(shp))
            off += size
        per_layer.append(outs)
    return [jnp.stack([per_layer[l][k] for l in range(DEPTH)]) for k in range(len(_SMALL_SHAPES))] + \
           [buf[DEPTH * _SMALL_ROWS:].reshape(-1)]


def _tiles(t):
    return min(512, t), min(256, t // 4), min(2048, t)


def _layer_fwd(xl, wi, rest, small_w, l, ride_attn=None, ride_mlp=None):
    n1, pw, ps, sn, sw, sb, n2 = small_w
    tm, tq, _ = _tiles(xl.shape[0])
    nb = xl.shape[0] // tq
    wbd = jnp.zeros((4, POOL_GW, 4, POOL_GW), F32)
    for gi in range(4):
        wbd = wbd.at[gi, :, gi, :].set(pw[gi])
    wbd = wbd.reshape(POOL_WIDTH, POOL_WIDTH).astype(BF16)
    wm = (sw * jnp.tril(jnp.ones((CHUNK, CHUNK), F32))).astype(BF16)
    wmt = wm.transpose(0, 2, 1)
    bfull = jnp.repeat(sb.T, SG_HD, axis=1)
    g1, g2, psc, gn = n1[None, :], n2[None, :], ps[None, :], sn[None, :]

    ab, qkv, ktb, vtb = _inproj_fwd(xl, g1, wi, tm, tq, f"inproj_fwd{l}")
    ya = _pool_fwd(ab, wbd, psc, tm, f"pool_fwd{l}")
    yb = _sg_fwd(ab, gn, wm, bfull, tm, f"sg_fwd{l}")
    if rest is None:
        (yct, cc, nvis), (wo, wu, wd) = _sba_fwd(qkv, vtb, tq, f"sba_fwd{l}", ride_attn)
    else:
        yct, cc, nvis = _sba_fwd(qkv, vtb, tq, f"sba_fwd{l}")
        wo, wu, wd = rest
    wo = wo.reshape(D_MODEL, D_MODEL)
    res = _outproj_mlp_fwd(xl, ya, yb, yct, wo, g2, wu, wd, min(MLP_ROWS, xl.shape[0]), f"mlp_fwd{l}", ride_mlp)
    (x1, u, x2), rode = res if ride_mlp is not None else (res, None)
    saved = dict(x=xl, ab=ab, qkv=qkv, ktb=ktb, cc=cc, nvis=nvis, ya=ya, yb=yb, yct=yct, x1=x1, u=u, wi=wi, wo=wo, wu=wu,
                 wd=wd, wbd=wbd, wm=wm, wmt=wmt, bfull=bfull, g1=g1, g2=g2, psc=psc, gn=gn)
    return x2, saved, rode


def _layer_bwd(dx, s, l, ride_mlp=None, scatter_in_attn=False):
    tm, tq, tw = _tiles(dx.shape[0])
    ktb = s["ktb"]
    res = _mlp_bwd(dx, s["x1"], s["g2"], s["u"], s["wu"], s["wd"], min(MLP_ROWS, dx.shape[0]), f"mlp_bwd{l}", ride_mlp)
    (dx1, du, r, h2, dx2b, dn2), rode = res if ride_mlp is not None else (res, None)
    dw_up = _matmul_tn(h2, du, D_MODEL, FF_SHARD, tw, f"dw_up{l}", by_column_block=True)
    dw_down = _matmul_tn(r, dx2b, 1024, D_MODEL, tw, f"dw_down{l}").reshape(N_DEV, FF_SHARD, D_MODEL)
    dya, dyb, dyc, dyct, dx1b = _outproj_bwd(dx1, s["wo"], tm, f"outproj_bwd{l}")
    dw_out = jnp.concatenate([
        _matmul_tn(jnp.concatenate([s["ya"], s["yb"]], axis=1), dx1b, POOL_WIDTH + SG_WIDTH, D_MODEL, tw, f"dw_out_ab{l}"),
        _matmul_tn(s["yct"], dx1b, SB_WIDTH, D_MODEL, tw, f"dw_out_c{l}", a_transposed=True)]
    ).reshape(N_DEV, OUT_SHARD, D_MODEL)
    if scatter_in_attn:
        (dqt, dk, dv), (dw_out, dw_up, dw_down) = _sba_bwd(s["qkv"], ktb, dyc, dyct, s["cc"], s["nvis"], tq, f"sba_bwd{l}",
                                                           Exchange([dw_out, dw_up, dw_down], True))
    else:
        dqt, dk, dv = _sba_bwd(s["qkv"], ktb, dyc, dyct, s["cc"], s["nvis"], tq, f"sba_bwd{l}")
    dup, dvp, dgn, dwm, dbm = _sg_bwd(s["ab"], dyb, s["gn"], s["wm"], s["wmt"], s["bfull"], tm, f"sg_bwd{l}")
    da, dwbd, dpsc = _pool_bwd(s["ab"], dya, s["wbd"], s["psc"], tm, f"pool_bwd{l}")
    dx, h1, dproj, dn1 = _inproj_bwd(dx1, s["x"], s["g1"], da, dup, dvp, dqt, dk, dv, s["wi"], tm, f"inproj_bwd{l}")
    dw_in = _matmul_tn(h1, dproj, D_MODEL, IN_COLS // 3, tw, f"dw_in{l}")
    dw_in = dw_in.reshape(D_MODEL, N_DEV, IN_SHARD).transpose(1, 0, 2)
    dpw = jnp.stack([dwbd[gi * POOL_GW:(gi + 1) * POOL_GW, gi * POOL_GW:(gi + 1) * POOL_GW] for gi in range(4)])
    small = (dn1[0], dpw, dpsc[0], dgn[0], dwm, dbm[:, :SG_HEADS].T, dn2[0])
    return dx, (dw_in, dw_out, dw_up, dw_down), small, rode


def kernel(x, norm1, w_in, pool_w, pool_scale, sg_norm, sg_w, sg_b, w_out, norm2, w_up, w_down, final_norm, loss_target, m_norm1, m_w_in, m_pool_w, m_pool_scale, m_sg_norm, m_sg_w, m_sg_b, m_w_out, m_norm2, m_w_up, m_w_down, m_final_norm, v_norm1, v_w_in, v_pool_w, v_pool_scale, v_sg_norm, v_sg_w, v_sg_b, v_w_out, v_norm2, v_w_up, v_w_down, v_final_norm):
    t = x.shape[1]
    tm = _tiles(t)[0]
    small_w = (norm1, pool_w, pool_scale, sg_norm, sg_w, sg_b, norm2)
    big_w = (w_in, w_out, w_up, w_down)
    big_m = (m_w_in, m_w_out, m_w_up, m_w_down)
    big_v = (v_w_in, v_w_out, v_w_up, v_w_down)
    shards = [[w[l].astype(BF16) for w in big_w] for l in range(DEPTH)]

    wi0 = _full_w_in(_exchange(shards[0][:1], False, "gather_w_in0")[0])
    x1, s0, g1 = _layer_fwd(x.reshape(t, D_MODEL), wi0, None, tuple(w[0] for w in small_w), 0,
                            ride_attn=Exchange(shards[0][1:], False), ride_mlp=Exchange(shards[1], False))
    x2, s1, _ = _layer_fwd(x1, _full_w_in(g1[0]), tuple(g1[1:]), tuple(w[1] for w in small_w), 1)
    loss_local, dx, dfinal = _loss_grad(x2, final_norm[None, :], loss_target.reshape(t, D_MODEL), tm, "loss_grad")
    loss = lax.psum(loss_local[0, 0], MESH_AXES)

    dx, parts1, small1, _ = _layer_bwd(dx, s1, 1)
    early = jnp.concatenate([_pack_small_layer(small1), dfinal.reshape(-1, 128)])
    dx, parts0, small0, recv1 = _layer_bwd(dx, s0, 0, ride_mlp=Exchange(list(parts1) + [early], [True] * 4 + [False]),
                                           scatter_in_attn=True)
    grad_x = dx.reshape(x.shape)
    recv_in0, late = _exchange([parts0[0], _pack_small_layer(small0)], [True, False], "scatter_w_in0_gather_small0")
    small_all = jnp.concatenate([late, recv1[4]], axis=1)
    received = [[recv_in0] + list(parts0[1:]), recv1[:4]]

    big = [None] * 4
    for l in reversed(range(DEPTH)):
        for k in range(4):
            big[k] = _reduce_adamw(received[l][k], big_w[k], big_m[k], big_v[k], l, big[k], f"adamw{k}_{l}")

    sm = _reduce_adamw(
        small_all,
        _pack_small(small_w, final_norm)[None],
        _pack_small([m_norm1, m_pool_w, m_pool_scale, m_sg_norm, m_sg_w, m_sg_b, m_norm2], m_final_norm)[None],
        _pack_small([v_norm1, v_pool_w, v_pool_scale, v_sg_norm, v_sg_w, v_sg_b, v_norm2], v_final_norm)[None],
        0, None, "adamw_replicated")

    out = [loss, grad_x]
    for k in range(4):
        n1, pw, ps, sn, sw, sb, n2, fn = _unpack_small(sm[k][0])
        out += [n1, big[0][k], pw, ps, sn, sw, sb, big[1][k], n2, big[2][k], big[3][k], fn]
    return tuple(out)
```

```python
import functools

import jax
import jax.numpy as jnp
from jax import lax
from jax.experimental import pallas as pl
from jax.experimental.pallas import tpu as pltpu

F32 = jnp.float32
BF16 = jnp.bfloat16

D_MODEL = 1024
DEPTH = 2
POOL_WIDTH = 256
SG_WIDTH = 256
SB_WIDTH = 512
POOL_WINDOWS = (2, 4, 8, 16)
POOL_GW = 64
POOL_HALO = 16
CHUNK = 128
SG_HEADS = 4
SG_HD = 64
SB_HD = 64
SB_PAIRS = SB_WIDTH // 128
AB_COLS = POOL_WIDTH + 2 * SG_WIDTH
IN_COLS = AB_COLS + 3 * SB_WIDTH
D_FF = 4096
EPS = 1e-6
N_DEV = 8
FF_SHARD = D_FF // N_DEV
IN_SHARD = IN_COLS // N_DEV
OUT_SHARD = D_MODEL // N_DEV
ADAM_LR = 0.001
ADAM_B1 = 0.9
ADAM_B2 = 0.999
ADAM_EPS = 1e-08
ADAM_WD = 0.01
ADAM_STEP = 10
VMEM_LIMIT = 56 * 1024 * 1024
MLP_ROWS = 1024
MESH_AXES = ("x", "y", "c")


def _dot(a, b):
    return jnp.dot(a, b, preferred_element_type=F32)


def _dot_nt(a, b):
    return lax.dot_general(a, b, (((1,), (1,)), ((), ())), preferred_element_type=F32)


def _dot_tn(a, b):
    return lax.dot_general(a, b, (((0,), (0,)), ((), ())), preferred_element_type=F32)


def _rstd(x):
    return lax.rsqrt(jnp.mean(x * x, axis=-1, keepdims=True) + EPS)


def _rms_bwd(x, r, g, dh):
    gq = dh * g
    dx = r * gq - x * (r * r * r) * jnp.mean(gq * x, axis=-1, keepdims=True)
    dg = jnp.sum(dh * x * r, axis=0, keepdims=True)
    return dx, dg


def _params(sem=None):
    kw = dict(vmem_limit_bytes=VMEM_LIMIT)
    if sem is not None:
        kw["dimension_semantics"] = sem
    return pltpu.CompilerParams(**kw)


def _row_tile(rows, cap):
    best = 8
    for t in range(8, min(rows, cap) + 1, 8):
        if rows % t == 0:
            best = t
    return best


def _peer(k):
    x, y, c = lax.axis_index("x"), lax.axis_index("y"), lax.axis_index("c")
    return (1 - x if k & 4 else x, 1 - y if k & 2 else y, 1 - c if k & 1 else c)


def _my_index():
    return 4 * lax.axis_index("x") + 2 * lax.axis_index("y") + lax.axis_index("c")


class Exchange:
    def __init__(self, arrs, scatter):
        self.arrs = list(arrs)
        self.scatter = list(scatter) if isinstance(scatter, (list, tuple)) else [scatter] * len(self.arrs)
        self.n = len(self.arrs)
        self.any_specs = [pl.BlockSpec(memory_space=pl.ANY)] * self.n
        self.out_shape = [jax.ShapeDtypeStruct((N_DEV,) + a.shape[-2:], a.dtype) for a in self.arrs]
        self.sems = [pltpu.SemaphoreType.DMA((self.n, N_DEV - 1)), pltpu.SemaphoreType.DMA((self.n, N_DEV - 1)),
                     pltpu.SemaphoreType.DMA((self.n,))]

    def _copies(self, ins, outs, sems):
        send_sems, recv_sems, local_sems = sems
        me = _my_index()
        local, remote = [], []
        for a in range(self.n):
            sc = self.scatter[a]
            local.append(pltpu.make_async_copy(ins[a].at[me] if sc else ins[a], outs[a].at[me], local_sems.at[a]))
            for k in range(1, N_DEV):
                px, py, pc = _peer(k)
                remote.append(pltpu.make_async_remote_copy(
                    src_ref=ins[a].at[4 * px + 2 * py + pc] if sc else ins[a], dst_ref=outs[a].at[me],
                    send_sem=send_sems.at[a, k - 1], recv_sem=recv_sems.at[a, k - 1],
                    device_id=(px, py, pc), device_id_type=pl.DeviceIdType.MESH))
        return local, remote

    def start(self, ins, outs, sems):
        local, remote = self._copies(ins, outs, sems)
        for cp in local + remote:
            cp.start()

    def wait(self, ins, outs, sems):
        local, remote = self._copies(ins, outs, sems)
        for cp in remote:
            cp.wait_recv()
        for cp in remote:
            cp.wait_send()
        for cp in local:
            cp.wait()

    def alone(self, name):
        n = self.n

        def body(*refs):
            ins, outs, sems = refs[:n], refs[n:2 * n], refs[2 * n:]
            self.start(ins, outs, sems)
            self.wait(ins, outs, sems)

        return pl.pallas_call(body, name=name, out_shape=self.out_shape, in_specs=self.any_specs,
                              out_specs=self.any_specs, scratch_shapes=self.sems)(*self.arrs)


def _exchange(arrs, scatter, name):
    return Exchange(arrs, scatter).alone(name)


def _call(body, name, grid, in_specs, out_specs, out_shape, scratch_shapes, semantics, args, ride=None):
    if ride is None:
        return pl.pallas_call(body, name=name, grid=grid, in_specs=in_specs, out_specs=out_specs, out_shape=out_shape,
                              scratch_shapes=scratch_shapes, compiler_params=_params(semantics))(*args)
    single = not isinstance(out_shape, (list, tuple))
    out_specs, out_shape = ([out_specs], [out_shape]) if single else (list(out_specs), list(out_shape))
    n_in, n_out, n_scr, n = len(in_specs), len(out_specs), len(scratch_shapes), ride.n

    def riding(*refs):
        ins, cins = refs[:n_in], refs[n_in:n_in + n]
        outs, couts = refs[n_in + n:n_in + n + n_out], refs[n_in + n + n_out:n_in + 2 * n + n_out]
        scr, sems = refs[n_in + 2 * n + n_out:n_in + 2 * n + n_out + n_scr], refs[n_in + 2 * n + n_out + n_scr:]
        first = functools.reduce(lambda p, q: p & q, [pl.program_id(a) == 0 for a in range(len(grid))])
        last = functools.reduce(lambda p, q: p & q, [pl.program_id(a) == g - 1 for a, g in enumerate(grid)])

        @pl.when(first)
        def _():
            ride.start(cins, couts, sems)

        body(*ins, *outs, *scr)

        @pl.when(last)
        def _():
            ride.wait(cins, couts, sems)

    res = pl.pallas_call(
        riding, name=name, grid=grid, in_specs=list(in_specs) + ride.any_specs, out_specs=out_specs + ride.any_specs,
        out_shape=out_shape + ride.out_shape, scratch_shapes=list(scratch_shapes) + ride.sems,
        compiler_params=_params(("arbitrary",) * len(grid)))(*args, *ride.arrs)
    own = res[0] if single else list(res[:n_out])
    return own, list(res[n_out:])


def _reduce_adamw(parts, w, m, v, l, prev, name):
    _, rows, n = parts.shape
    tr = _row_tile(rows, max(8, (1 << 18) // n))
    c1 = 1.0 - ADAM_B1 ** ADAM_STEP
    c2 = 1.0 - ADAM_B2 ** ADAM_STEP

    def body(p_ref, w_ref, m_ref, v_ref, *rest):
        g_ref, d_ref, nm_ref, nv_ref = rest[-4:]
        g = p_ref[0].astype(F32)
        for s in range(1, N_DEV):
            g = g + p_ref[s].astype(F32)
        nm = ADAM_B1 * m_ref[...] + (1.0 - ADAM_B1) * g
        nv = ADAM_B2 * v_ref[...] + (1.0 - ADAM_B2) * (g * g)
        m_hat = nm / c1
        v_hat = nv / c2
        g_ref[...] = g
        d_ref[...] = -ADAM_LR * (m_hat / (jnp.sqrt(v_hat) + ADAM_EPS) + ADAM_WD * w_ref[...])
        nm_ref[...] = nm
        nv_ref[...] = nv

    blk = pl.BlockSpec((None, tr, n), lambda i: (l, i, 0))
    out = jax.ShapeDtypeStruct(w.shape, F32)
    prev = list(prev) if prev is not None else []
    return pl.pallas_call(
        body, name=name, grid=(rows // tr,),
        in_specs=[pl.BlockSpec((N_DEV, tr, n), lambda i: (0, i, 0)), blk, blk, blk] + [pl.BlockSpec(memory_space=pl.ANY)] * len(prev),
        out_specs=[blk, blk, blk, blk], out_shape=[out, out, out, out],
        input_output_aliases={4 + k: k for k in range(len(prev))},
        compiler_params=_params(("parallel",)),
    )(parts, w, m, v, *prev)


def _inproj_fwd(x, g, w, tm, tq, name, ride=None):
    t, d = x.shape
    n = w.shape[1]
    nb = t // tq
    per = tm // tq

    def body(x_ref, g_ref, w_ref, ab_ref, qkv_ref, kt_ref, vt_ref):
        xx = x_ref[...]
        h = (xx * _rstd(xx) * g_ref[...]).astype(BF16)
        ab_ref[...] = _dot(h, w_ref[:, :AB_COLS])
        qkv = _dot(h, w_ref[:, AB_COLS:])
        qkv_ref[...] = qkv.astype(BF16)
        for which, out_ref in ((1, kt_ref), (2, vt_ref)):
            for p in range(SB_PAIRS):
                for b in range(per):
                    cols = which * SB_WIDTH + p * 128
                    out_ref[p, b] = qkv[b * tq:(b + 1) * tq, cols:cols + 128].T.astype(BF16)

    tb = pl.BlockSpec((SB_PAIRS, per, 128, tq), lambda i: (0, i, 0, 0))
    tshape = jax.ShapeDtypeStruct((SB_PAIRS, nb, 128, tq), BF16)
    return _call(
        body, name, (t // tm,),
        [pl.BlockSpec((tm, d), lambda i: (i, 0)), pl.BlockSpec((1, d), lambda i: (0, 0)),
         pl.BlockSpec((d, n), lambda i: (0, 0))],
        [pl.BlockSpec((tm, AB_COLS), lambda i: (i, 0)), pl.BlockSpec((tm, n - AB_COLS), lambda i: (i, 0)), tb, tb],
        [jax.ShapeDtypeStruct((t, AB_COLS), F32), jax.ShapeDtypeStruct((t, n - AB_COLS), BF16), tshape, tshape],
        [], ("parallel",), (x, g, w), ride)


def _pool_window_sums(xx, forward):
    n = xx.shape[0]
    sh = (lambda k: n - k) if forward else (lambda k: k)
    s2 = xx + pltpu.roll(xx, sh(1), 0)
    s4 = s2 + pltpu.roll(s2, sh(2), 0)
    s8 = s4 + pltpu.roll(s4, sh(4), 0)
    s16 = s8 + pltpu.roll(s8, sh(8), 0)
    grp = lax.broadcasted_iota(jnp.int32, (1, POOL_WIDTH), 1) // POOL_GW
    return jnp.where(grp == 0, s2, jnp.where(grp == 1, s4, jnp.where(grp == 2, s8, s16)))


def _pool_count(t0, rows):
    grp = lax.broadcasted_iota(jnp.int32, (1, POOL_WIDTH), 1) // POOL_GW
    win = jnp.where(grp == 0, 2, jnp.where(grp == 1, 4, jnp.where(grp == 2, 8, 16)))
    tt = t0 + lax.broadcasted_iota(jnp.int32, (rows, 1), 0)
    return jnp.minimum(tt + 1, win).astype(F32)


def _pool_diff(cur, prev, t0):
    tm = cur.shape[0]
    sums = _pool_window_sums(jnp.concatenate([prev, cur], axis=0), False)[POOL_HALO:]
    return sums / _pool_count(t0, tm) - cur


def _pool_fwd(ab, wbd, scale, tm, name):
    t = ab.shape[0]
    hb = tm // POOL_HALO

    def body(cur_ref, prev_ref, w_ref, s_ref, y_ref):
        i = pl.program_id(0)
        prev = jnp.where(i == 0, 0.0, prev_ref[...])
        d = _pool_diff(cur_ref[...], prev, i * tm).astype(BF16)
        y_ref[...] = (_dot(d, w_ref[...]) * s_ref[...]).astype(BF16)

    return pl.pallas_call(
        body, name=name, grid=(t // tm,),
        in_specs=[pl.BlockSpec((tm, POOL_WIDTH), lambda i: (i, 0)),
                  pl.BlockSpec((POOL_HALO, POOL_WIDTH), lambda i: (jnp.maximum(i * hb - 1, 0), 0)),
                  pl.BlockSpec((POOL_WIDTH, POOL_WIDTH), lambda i: (0, 0)),
                  pl.BlockSpec((1, POOL_WIDTH), lambda i: (0, 0))],
        out_specs=pl.BlockSpec((tm, POOL_WIDTH), lambda i: (i, 0)),
        out_shape=jax.ShapeDtypeStruct((t, POOL_WIDTH), BF16),
        compiler_params=_params(("parallel",)),
    )(ab, ab, wbd, scale)


_GELU_K = 0.7978845608028654
_GELU_A = 0.044715


def _gelu(x):
    return 0.5 * x * (1.0 + jnp.tanh(_GELU_K * (x + _GELU_A * (x * x * x))))


def _gelu_grad(x):
    th = jnp.tanh(_GELU_K * (x + _GELU_A * (x * x * x)))
    return 0.5 * (1.0 + th) + 0.5 * x * (1.0 - th * th) * (_GELU_K * (1.0 + 3.0 * _GELU_A * (x * x)))


def _head_lanes(h):
    return lax.broadcasted_iota(jnp.int32, (1, SG_WIDTH), 1) // SG_HD == h


def _sg_fwd(ab, gn, wm, bfull, tm, name):
    t = ab.shape[0]

    def body(u_ref, v_ref, gn_ref, wm_ref, b_ref, y_ref):
        v = _gelu(v_ref[...])
        vn = (v * _rstd(v) * gn_ref[...]).astype(BF16)
        for c in range(tm // CHUNK):
            rows = slice(c * CHUNK, (c + 1) * CHUNK)
            vc = vn[rows]
            sv = b_ref[...]
            for h in range(SG_HEADS):
                sv = sv + jnp.where(_head_lanes(h), _dot(wm_ref[h], vc), 0.0)
            y_ref[rows, :] = (_gelu(u_ref[rows, :]) * sv).astype(BF16)

    return pl.pallas_call(
        body, name=name, grid=(t // tm,),
        in_specs=[pl.BlockSpec((tm, SG_WIDTH), lambda i: (i, 1)), pl.BlockSpec((tm, SG_WIDTH), lambda i: (i, 2)),
                  pl.BlockSpec((1, SG_WIDTH), lambda i: (0, 0)),
                  pl.BlockSpec((SG_HEADS, CHUNK, CHUNK), lambda i: (0, 0, 0)),
                  pl.BlockSpec((CHUNK, SG_WIDTH), lambda i: (0, 0))],
        out_specs=pl.BlockSpec((tm, SG_WIDTH), lambda i: (i, 0)),
        out_shape=jax.ShapeDtypeStruct((t, SG_WIDTH), BF16),
        compiler_params=_params(("parallel",)),
    )(ab, ab, gn, wm, bfull)


LOG2E = 1.4426950408889634
SB_SCALE = 0.125 * LOG2E
SB_DEAD_LOG2 = 152.0


def _log2_sigmoids(y):
    neg_abs = lax.bitcast_convert_type(lax.bitcast_convert_type(y, jnp.uint32) | jnp.uint32(0x80000000), F32)
    lb = jnp.minimum(y, 0.0) - jnp.log(1.0 + jnp.exp2(neg_abs)) * LOG2E
    return lb, lb - y


def _split(x):
    hi = x.astype(BF16)
    return hi, (x - hi.astype(F32)).astype(BF16)


def _tri_dot(tri, x):
    hi, lo = _split(x)
    return _dot(tri, hi) + _dot(tri, lo)


def _sba_fwd(qkv, vtb, tq, name, ride=None):
    t = qkv.shape[0]
    nb = t // tq
    upper = (jnp.arange(tq)[None, :] > jnp.arange(tq)[:, None]).astype(BF16)

    def body(q_ref, k_ref, vt_ref, up_ref, ot_ref, c_ref, n_ref):
        i = pl.program_id(1)
        q = q_ref[...]
        up = up_ref[...]
        lane_head = lax.broadcasted_iota(jnp.int32, (1, 128), 1) // SB_HD
        sub_head = lax.broadcasted_iota(jnp.int32, (128, 1), 0) // SB_HD
        causal = (lax.broadcasted_iota(jnp.int32, (tq, tq), 0) < lax.broadcasted_iota(jnp.int32, (tq, tq), 1))
        qh = [jnp.where(lane_head == h, q, jnp.zeros_like(q)) for h in range(2)]

        def blocks(js, carry):
            cs, acc = list(carry[:2]), carry[2]
            kj = [k_ref[pl.ds(pl.multiple_of(j * tq, tq), tq), :] for j, _ in js]
            vt = [vt_ref[j] for j, _ in js]
            chains = [(b, h) for b in range(len(js)) for h in range(2)]
            z = [_dot_nt(kj[b], qh[h]) for b, h in chains]
            ls = [_log2_sigmoids(zz * SB_SCALE) for zz in z]
            lb = [x[0] for x in ls]
            l1 = [jnp.where(causal, x[1], 0.0) if js[b][1] else x[1] for x, (b, h) in zip(ls, chains)]
            after = [_tri_dot(up, x) for x in l1]
            a = []
            for n, (b, h) in enumerate(chains):
                c_ref[h, js[b][0]] = cs[h]
                an = jnp.exp2(lb[n] + after[n] + cs[h])
                a.append(jnp.where(causal, an, 0.0) if js[b][1] else an)
                cs[h] = cs[h] + after[n][0:1, :] + l1[n][0:1, :]
            for n, (b, h) in enumerate(chains):
                acc = acc + _dot(jnp.where(sub_head == h, vt[b], jnp.zeros_like(vt[b])), a[n].astype(BF16))
            return cs[0], cs[1], acc

        def alive(c0, c1):
            return (jnp.max(jnp.maximum(c0, c1)) > -SB_DEAD_LOG2).astype(jnp.int32)

        def step(state):
            s, _, c0, c1, acc = state
            c0, c1, acc = blocks([(i - 1 - s, False)], (c0, c1, acc))
            return s + 1, alive(c0, c1), c0, c1, acc

        zero = jnp.zeros((1, tq), F32)
        start = (zero, zero, jnp.zeros((128, tq), F32))
        c0, c1, acc = lax.cond(i > 0, lambda: blocks([(i, True), (i - 1, False)], start), lambda: blocks([(i, True)], start))
        first = jnp.minimum(i, 1)
        state = lax.while_loop(lambda st: (st[0] < i) & (st[1] > 0), step, (first, alive(c0, c1), c0, c1, acc))
        ot_ref[...] = state[4].astype(BF16)
        n_ref[pl.program_id(0), i] = (state[0] + 1).astype(F32)

    return _call(
        body, name, (SB_PAIRS, nb),
        [pl.BlockSpec((tq, 128), lambda p, i: (i, p)),
         pl.BlockSpec((t, 128), lambda p, i: (0, SB_PAIRS + p)),
         pl.BlockSpec((None, nb, 128, tq), lambda p, i: (p, 0, 0, 0)),
         pl.BlockSpec((tq, tq), lambda p, i: (0, 0))],
        [pl.BlockSpec((128, tq), lambda p, i: (p, i)),
         pl.BlockSpec((2, nb, 1, tq), lambda p, i: (p, 0, 0, i)),
         pl.BlockSpec(memory_space=pltpu.SMEM)],
        [jax.ShapeDtypeStruct((SB_WIDTH, t), BF16),
         jax.ShapeDtypeStruct((2 * SB_PAIRS, nb, 1, t), F32),
         jax.ShapeDtypeStruct((SB_PAIRS, nb), F32)],
        [], ("arbitrary", "arbitrary"), (qkv, qkv, vtb, upper), ride)


def _outproj_mlp_fwd(x, ya, yb, yct, wo, g2, wup, wdn, tm, name, ride=None):
    t, d = x.shape
    nf = wup.shape[0]

    def body(x_ref, ya_ref, yb_ref, yct_ref, wo_ref, g_ref, wu_ref, wd_ref, x1_ref, u_ref, x2_ref, h_ref, acc_ref):
        j = pl.program_id(1)

        @pl.when(j == 0)
        def _():
            x1 = (x_ref[...] + _dot(ya_ref[...], wo_ref[0:POOL_WIDTH, :])
                  + _dot(yb_ref[...], wo_ref[POOL_WIDTH:POOL_WIDTH + SG_WIDTH, :])
                  + _dot_tn(yct_ref[...], wo_ref[POOL_WIDTH + SG_WIDTH:, :]))
            x1_ref[...] = x1
            h_ref[...] = (x1 * _rstd(x1) * g_ref[...]).astype(BF16)
            acc_ref[...] = x1

        u = _dot(h_ref[...], wu_ref[...])
        u_ref[...] = u
        r = jnp.maximum(u, 0.0)
        acc_ref[...] += _dot((r * r).astype(BF16), wd_ref[...])

        @pl.when(j == nf - 1)
        def _():
            x2_ref[...] = acc_ref[...]

    row = lambda w: pl.BlockSpec((tm, w), lambda i, j: (i, 0))
    return _call(
        body, name, (t // tm, nf),
        [row(d), row(POOL_WIDTH), row(SG_WIDTH), pl.BlockSpec((SB_WIDTH, tm), lambda i, j: (0, i)),
         pl.BlockSpec((d, d), lambda i, j: (0, 0)), pl.BlockSpec((1, d), lambda i, j: (0, 0)),
         pl.BlockSpec((None, d, FF_SHARD), lambda i, j: (j, 0, 0)),
         pl.BlockSpec((None, FF_SHARD, d), lambda i, j: (j, 0, 0))],
        [row(d), pl.BlockSpec((tm, FF_SHARD), lambda i, j: (i, j)), row(d)],
        [jax.ShapeDtypeStruct((t, d), F32), jax.ShapeDtypeStruct((t, nf * FF_SHARD), F32),
         jax.ShapeDtypeStruct((t, d), F32)],
        [pltpu.VMEM((tm, d), BF16), pltpu.VMEM((tm, d), F32)],
        ("parallel", "arbitrary"), (x, ya, yb, yct, wo, g2, wup, wdn), ride)


def _loss_grad(x, g, target, tm, name):
    t, d = x.shape
    nt = t // tm

    def body(x_ref, g_ref, t_ref, loss_ref, dx_ref, dg_ref, sq_ref):
        i = pl.program_id(0)

        @pl.when(i == 0)
        def _():
            sq_ref[...] = jnp.zeros_like(sq_ref)
            dg_ref[...] = jnp.zeros_like(dg_ref)

        xx = x_ref[...]
        r = _rstd(xx)
        err = xx * r * g_ref[...] - t_ref[...]
        sq_ref[...] += jnp.sum(err * err, axis=0, keepdims=True)
        dx, dg = _rms_bwd(xx, r, g_ref[...], err * (1.0 / d))
        dx_ref[...] = dx
        dg_ref[...] += dg

        @pl.when(i == nt - 1)
        def _():
            loss_ref[...] = jnp.sum(sq_ref[...], axis=1, keepdims=True) * (0.5 / d)

    return pl.pallas_call(
        body, name=name, grid=(nt,),
        in_specs=[pl.BlockSpec((tm, d), lambda i: (i, 0)), pl.BlockSpec((1, d), lambda i: (0, 0)),
                  pl.BlockSpec((tm, d), lambda i: (i, 0))],
        out_specs=[pl.BlockSpec((1, 1), lambda i: (0, 0)), pl.BlockSpec((tm, d), lambda i: (i, 0)),
                   pl.BlockSpec((1, d), lambda i: (0, 0))],
        out_shape=[jax.ShapeDtypeStruct((1, 1), F32), jax.ShapeDtypeStruct((t, d), F32),
                   jax.ShapeDtypeStruct((1, d), F32)],
        scratch_shapes=[pltpu.VMEM((1, d), F32)],
        compiler_params=_params(("arbitrary",)),
    )(x, g, target)


def _mlp_bwd(dx2, x1, g2, u, wup, wdn, tm, name, ride=None):
    t, d = dx2.shape
    nf = wup.shape[0]
    nt = t // tm

    def body(dx2_ref, x1_ref, g_ref, u_ref, wu_ref, wd_ref, dx1_ref, du_ref, r_ref, h_ref, dxb_ref, dg_ref, acc_ref):
        i, j = pl.program_id(0), pl.program_id(1)

        @pl.when(j == 0)
        def _():
            x1 = x1_ref[...]
            h_ref[...] = (x1 * _rstd(x1) * g_ref[...]).astype(BF16)
            dxb_ref[...] = dx2_ref[...].astype(BF16)
            acc_ref[...] = jnp.zeros_like(acc_ref)

        @pl.when((i == 0) & (j == 0))
        def _():
            dg_ref[...] = jnp.zeros_like(dg_ref)

        dr = _dot_nt(dxb_ref[...], wd_ref[...])
        ru = jnp.maximum(u_ref[...], 0.0)
        du = (dr * (2.0 * ru)).astype(BF16)
        du_ref[...] = du
        r_ref[...] = (ru * ru).astype(BF16)
        acc_ref[...] += _dot_nt(du, wu_ref[...])

        @pl.when(j == nf - 1)
        def _():
            x1 = x1_ref[...]
            dx, dg = _rms_bwd(x1, _rstd(x1), g_ref[...], acc_ref[...])
            dx1_ref[...] = dx2_ref[...] + dx
            dg_ref[...] += dg

    row = lambda w: pl.BlockSpec((tm, w), lambda i, j: (i, 0))
    col = pl.BlockSpec((tm, FF_SHARD), lambda i, j: (i, j))
    return _call(
        body, name, (nt, nf),
        [row(d), row(d), pl.BlockSpec((1, d), lambda i, j: (0, 0)), col,
         pl.BlockSpec((None, d, FF_SHARD), lambda i, j: (j, 0, 0)),
         pl.BlockSpec((None, FF_SHARD, d), lambda i, j: (j, 0, 0))],
        [row(d), col, col, row(d), row(d), pl.BlockSpec((1, d), lambda i, j: (0, 0))],
        [jax.ShapeDtypeStruct((t, d), F32), jax.ShapeDtypeStruct((t, nf * FF_SHARD), BF16),
         jax.ShapeDtypeStruct((t, nf * FF_SHARD), BF16), jax.ShapeDtypeStruct((t, d), BF16),
         jax.ShapeDtypeStruct((t, d), BF16), jax.ShapeDtypeStruct((1, d), F32)],
        [pltpu.VMEM((tm, d), F32)], ("arbitrary", "arbitrary"), (dx2, x1, g2, u, wup, wdn), ride)


def _matmul_tn(a, b, bm, bn, bt, name, by_column_block=False, a_transposed=False):
    m, t = a.shape if a_transposed else a.shape[::-1]
    n = b.shape[1]
    nk = t // bt

    def body(a_ref, b_ref, o_ref, acc_ref):
        k = pl.program_id(2)

        @pl.when(k == 0)
        def _():
            acc_ref[...] = jnp.zeros_like(acc_ref)

        acc_ref[...] += _dot(a_ref[...], b_ref[...]) if a_transposed else _dot_tn(a_ref[...], b_ref[...])

        @pl.when(k == nk - 1)
        def _():
            o_ref[...] = acc_ref[...].astype(BF16)

    if by_column_block:
        out_spec = pl.BlockSpec((None, bm, bn), lambda i, j, k: (j, i, 0))
        out_shape = jax.ShapeDtypeStruct((n // bn, m, bn), BF16)
    else:
        out_spec = pl.BlockSpec((bm, bn), lambda i, j, k: (i, j))
        out_shape = jax.ShapeDtypeStruct((m, n), BF16)
    return pl.pallas_call(
        body, name=name, grid=(m // bm, n // bn, nk),
        in_specs=[pl.BlockSpec((bm, bt), lambda i, j, k: (i, k)) if a_transposed else pl.BlockSpec((bt, bm), lambda i, j, k: (k, i)),
                  pl.BlockSpec((bt, bn), lambda i, j, k: (k, j))],
        out_specs=out_spec, out_shape=out_shape,
        scratch_shapes=[pltpu.VMEM((bm, bn), F32)],
        compiler_params=_params(("parallel", "parallel", "arbitrary")),
    )(a, b)


def _outproj_bwd(dx1, wo, tm, name):
    t, d = dx1.shape
    c2 = POOL_WIDTH + SG_WIDTH

    def body(dx_ref, wo_ref, dya_ref, dyb_ref, dyc_ref, dyct_ref, dxb_ref):
        dxb = dx_ref[...].astype(BF16)
        dxb_ref[...] = dxb
        dya_ref[...] = _dot_nt(dxb, wo_ref[0:POOL_WIDTH, :])
        dyb_ref[...] = _dot_nt(dxb, wo_ref[POOL_WIDTH:c2, :])
        dyc_ref[...] = _dot_nt(dxb, wo_ref[c2:, :]).astype(BF16)
        dyct_ref[...] = _dot_nt(wo_ref[c2:, :], dxb).astype(BF16)

    row = lambda w: pl.BlockSpec((tm, w), lambda i: (i, 0))
    return pl.pallas_call(
        body, name=name, grid=(t // tm,),
        in_specs=[row(d), pl.BlockSpec((d, d), lambda i: (0, 0))],
        out_specs=[row(POOL_WIDTH), row(SG_WIDTH), row(SB_WIDTH), pl.BlockSpec((SB_WIDTH, tm), lambda i: (0, i)), row(d)],
        out_shape=[jax.ShapeDtypeStruct((t, POOL_WIDTH), F32), jax.ShapeDtypeStruct((t, SG_WIDTH), F32),
                   jax.ShapeDtypeStruct((t, SB_WIDTH), BF16), jax.ShapeDtypeStruct((SB_WIDTH, t), BF16),
                   jax.ShapeDtypeStruct((t, d), BF16)],
        compiler_params=_params(("parallel",)),
    )(dx1, wo)


def _sba_bwd(qkv, ktb, dyc, dyct, cc, nvis, tq, name, ride=None):
    t = qkv.shape[0]
    nb = t // tq
    idx = jnp.arange(tq)
    upper = (idx[None, :] > idx[:, None]).astype(BF16)
    lower = (idx[None, :] < idx[:, None]).astype(BF16)

    def body(q_ref, k_ref, v_ref, kt_ref, do_ref, dot_ref, c_ref, n_ref, up_ref, lo_ref, dqt_ref, dk_ref, dv_ref):
        i = pl.program_id(1)

        @pl.when(i == 0)
        def _():
            dk_ref[...] = jnp.zeros_like(dk_ref)
            dv_ref[...] = jnp.zeros_like(dv_ref)

        q = q_ref[...]
        do = do_ref[...]
        dot = dot_ref[...]
        up = up_ref[...]
        lo = lo_ref[...]
        lane_head = lax.broadcasted_iota(jnp.int32, (1, 128), 1) // SB_HD
        sub_head = lax.broadcasted_iota(jnp.int32, (128, 1), 0) // SB_HD
        causal = (lax.broadcasted_iota(jnp.int32, (tq, tq), 0) < lax.broadcasted_iota(jnp.int32, (tq, tq), 1))
        hms = [lane_head == h for h in range(2)]
        qh = [jnp.where(hm, q, jnp.zeros_like(q)) for hm in hms]
        qs = [x * 0.125 for x in qh]
        doh = [jnp.where(hm, do, jnp.zeros_like(do)) for hm in hms]

        def blocks(js, carry):
            cgs, dqt = list(carry[:2]), carry[2]
            rows = [pl.ds(pl.multiple_of(j * tq, tq), tq) for j, _ in js]
            kj = [k_ref[r, :] for r in rows]
            vj = [v_ref[r, :] for r in rows]
            kt = [kt_ref[j] for j, _ in js]
            chains = [(b, h) for b in range(len(js)) for h in range(2)]
            z = [_dot_nt(kj[b], qh[h]) for b, h in chains]
            da = [_dot(jnp.where(hms[h], vj[b], jnp.zeros_like(vj[b])), dot) for b, h in chains]
            ls = [_log2_sigmoids(zz * SB_SCALE) for zz in z]
            lb = [x[0] for x in ls]
            l1 = [jnp.where(causal, x[1], 0.0) if js[b][1] else x[1] for x, (b, h) in zip(ls, chains)]
            after = [_tri_dot(up, x) for x in l1]
            a = [jnp.exp2(lb[n] + after[n] + c_ref[h, js[b][0]]) for n, (b, h) in enumerate(chains)]
            a = [jnp.where(causal, a[n], 0.0) if js[b][1] else a[n] for n, (b, h) in enumerate(chains)]
            g = [a[n] * da[n] for n in range(len(chains))]
            gloc = [_tri_dot(lo, x) for x in g]
            dzb = []
            for n, (b, h) in enumerate(chains):
                gsum = gloc[n] + cgs[h]
                dz = g[n] - jnp.exp2(lb[n]) * (g[n] + gsum)
                dzb.append((jnp.where(causal, dz, 0.0) if js[b][1] else dz).astype(BF16))
                cgs[h] = gsum[tq - 1:tq, :] + g[n][tq - 1:tq, :]
            ab = [x.astype(BF16) for x in a]
            for n, (b, h) in enumerate(chains):
                dqt = dqt + _dot(jnp.where(sub_head == h, kt[b], jnp.zeros_like(kt[b])), dzb[n])
            for b in range(len(js)):
                dk_ref[rows[b], :] += _dot(dzb[2 * b], qs[0]) + _dot(dzb[2 * b + 1], qs[1])
                dv_ref[rows[b], :] += _dot(ab[2 * b], doh[0]) + _dot(ab[2 * b + 1], doh[1])
            return cgs[0], cgs[1], dqt

        n = jnp.clip(n_ref[pl.program_id(0), i].astype(jnp.int32), 1, i + 1)
        zero = jnp.zeros((1, tq), F32)
        carry = lax.fori_loop(jnp.minimum(i + 1 - n, i - 1), i - 1, lambda s, cr: blocks([(s, False)], cr),
                              (zero, zero, jnp.zeros((128, tq), F32)))
        carry = lax.cond(i > 0, lambda: blocks([(i - 1, False), (i, True)], carry), lambda: blocks([(i, True)], carry))
        dqt_ref[...] = carry[2] * 0.125

    return _call(
        body, name, (SB_PAIRS, nb),
        [pl.BlockSpec((tq, 128), lambda p, i: (i, p)),
         pl.BlockSpec((t, 128), lambda p, i: (0, SB_PAIRS + p)),
         pl.BlockSpec((t, 128), lambda p, i: (0, 2 * SB_PAIRS + p)),
         pl.BlockSpec((None, nb, 128, tq), lambda p, i: (p, 0, 0, 0)),
         pl.BlockSpec((tq, 128), lambda p, i: (i, p)),
         pl.BlockSpec((128, tq), lambda p, i: (p, i)),
         pl.BlockSpec((2, nb, 1, tq), lambda p, i: (p, 0, 0, i)),
         pl.BlockSpec(memory_space=pltpu.SMEM),
         pl.BlockSpec((tq, tq), lambda p, i: (0, 0)),
         pl.BlockSpec((tq, tq), lambda p, i: (0, 0))],
        [pl.BlockSpec((128, tq), lambda p, i: (p, i)),
         pl.BlockSpec((t, 128), lambda p, i: (0, p)),
         pl.BlockSpec((t, 128), lambda p, i: (0, p))],
        [jax.ShapeDtypeStruct((SB_WIDTH, t), F32), jax.ShapeDtypeStruct((t, SB_WIDTH), F32),
         jax.ShapeDtypeStruct((t, SB_WIDTH), F32)],
        [], ("arbitrary", "arbitrary"), (qkv, qkv, qkv, ktb, dyc, dyct, cc, nvis, upper, lower), ride)


def _sg_bwd(ab, dyb, gn, wm, wmt, bfull, tm, name):
    t = ab.shape[0]
    nt = t // tm
    sel = (jnp.arange(SG_WIDTH)[:, None] // SG_HD == jnp.arange(CHUNK)[None, :]).astype(F32)

    def body(u_ref, v_ref, dy_ref, gn_ref, wm_ref, wmt_ref, b_ref, sel_ref,
             dup_ref, dvp_ref, dgn_ref, dw_ref, db_ref, dbacc_ref):
        i = pl.program_id(0)

        @pl.when(i == 0)
        def _():
            dgn_ref[...] = jnp.zeros_like(dgn_ref)
            dw_ref[...] = jnp.zeros_like(dw_ref)
            dbacc_ref[...] = jnp.zeros_like(dbacc_ref)

        tril = (lax.broadcasted_iota(jnp.int32, (CHUNK, CHUNK), 0) >= lax.broadcasted_iota(jnp.int32, (CHUNK, CHUNK), 1))
        gn_ = gn_ref[...]
        for c in range(tm // CHUNK):
            rows = slice(c * CHUNK, (c + 1) * CHUNK)
            up, vp, dy = u_ref[rows, :], v_ref[rows, :], dy_ref[rows, :]
            u, v = _gelu(up), _gelu(vp)
            r = _rstd(v)
            vn = (v * r * gn_).astype(BF16)
            sv = b_ref[...]
            for h in range(SG_HEADS):
                sv = sv + jnp.where(_head_lanes(h), _dot(wm_ref[h], vn), 0.0)
            dup_ref[rows, :] = dy * sv * _gelu_grad(up)
            dsv = dy * u
            dbacc_ref[...] += dsv
            dvn = jnp.zeros((CHUNK, SG_WIDTH), F32)
            for h in range(SG_HEADS):
                dsv_h = jnp.where(_head_lanes(h), dsv, 0.0).astype(BF16)
                dvn = dvn + _dot(wmt_ref[h], dsv_h)
                dw_ref[h] += jnp.where(tril, _dot_nt(dsv_h, vn), 0.0)
            dv, dgn = _rms_bwd(v, r, gn_, dvn)
            dgn_ref[...] += dgn
            dvp_ref[rows, :] = dv * _gelu_grad(vp)

        @pl.when(i == nt - 1)
        def _():
            db_ref[...] = jnp.dot(dbacc_ref[...], sel_ref[...], preferred_element_type=F32,
                                  precision=lax.Precision.HIGHEST)

    const = lambda shape: pl.BlockSpec(shape, lambda i: tuple(0 for _ in shape))
    return pl.pallas_call(
        body, name=name, grid=(nt,),
        in_specs=[pl.BlockSpec((tm, SG_WIDTH), lambda i: (i, 1)), pl.BlockSpec((tm, SG_WIDTH), lambda i: (i, 2)),
                  pl.BlockSpec((tm, SG_WIDTH), lambda i: (i, 0)), const((1, SG_WIDTH)),
                  const((SG_HEADS, CHUNK, CHUNK)), const((SG_HEADS, CHUNK, CHUNK)), const((CHUNK, SG_WIDTH)),
                  const((SG_WIDTH, CHUNK))],
        out_specs=[pl.BlockSpec((tm, SG_WIDTH), lambda i: (i, 0)), pl.BlockSpec((tm, SG_WIDTH), lambda i: (i, 0)),
                   const((1, SG_WIDTH)), const((SG_HEADS, CHUNK, CHUNK)), const((CHUNK, CHUNK))],
        out_shape=[jax.ShapeDtypeStruct((t, SG_WIDTH), F32), jax.ShapeDtypeStruct((t, SG_WIDTH), F32),
                   jax.ShapeDtypeStruct((1, SG_WIDTH), F32), jax.ShapeDtypeStruct((SG_HEADS, CHUNK, CHUNK), F32),
                   jax.ShapeDtypeStruct((CHUNK, CHUNK), F32)],
        scratch_shapes=[pltpu.VMEM((CHUNK, SG_WIDTH), F32)],
        compiler_params=_params(("arbitrary",)),
    )(ab, ab, dyb, gn, wm, wmt, bfull, sel)


def _pool_bwd(ab, dya, wbd, scale, tm, name):
    t = ab.shape[0]
    nt = t // tm
    hb = tm // POOL_HALO
    nh = t // POOL_HALO

    def body(cur_ref, prev_ref, dy_ref, dyn_ref, w_ref, s_ref, da_ref, dw_ref, ds_ref):
        i = pl.program_id(0)

        @pl.when(i == 0)
        def _():
            dw_ref[...] = jnp.zeros_like(dw_ref)
            ds_ref[...] = jnp.zeros_like(ds_ref)

        prev = jnp.where(i == 0, 0.0, prev_ref[...])
        d = _pool_diff(cur_ref[...], prev, i * tm).astype(BF16)
        dy = dy_ref[...]
        ds_ref[...] += jnp.sum(dy * _dot(d, w_ref[...]), axis=0, keepdims=True)
        dyn = jnp.where(i == nt - 1, 0.0, dyn_ref[...])
        dys = (jnp.concatenate([dy, dyn], axis=0) * s_ref[...]).astype(BF16)
        dw_ref[...] += _dot_tn(d, dys[:tm])
        dd = _dot_nt(dys, w_ref[...])
        fwd = _pool_window_sums(dd / _pool_count(i * tm, tm + POOL_HALO), True)
        da_ref[...] = fwd[:tm] - dd[:tm]

    return pl.pallas_call(
        body, name=name, grid=(nt,),
        in_specs=[pl.BlockSpec((tm, POOL_WIDTH), lambda i: (i, 0)),
                  pl.BlockSpec((POOL_HALO, POOL_WIDTH), lambda i: (jnp.maximum(i * hb - 1, 0), 0)),
                  pl.BlockSpec((tm, POOL_WIDTH), lambda i: (i, 0)),
                  pl.BlockSpec((POOL_HALO, POOL_WIDTH), lambda i: (jnp.minimum((i + 1) * hb, nh - 1), 0)),
                  pl.BlockSpec((POOL_WIDTH, POOL_WIDTH), lambda i: (0, 0)),
                  pl.BlockSpec((1, POOL_WIDTH), lambda i: (0, 0))],
        out_specs=[pl.BlockSpec((tm, POOL_WIDTH), lambda i: (i, 0)),
                   pl.BlockSpec((POOL_WIDTH, POOL_WIDTH), lambda i: (0, 0)),
                   pl.BlockSpec((1, POOL_WIDTH), lambda i: (0, 0))],
        out_shape=[jax.ShapeDtypeStruct((t, POOL_WIDTH), F32), jax.ShapeDtypeStruct((POOL_WIDTH, POOL_WIDTH), F32),
                   jax.ShapeDtypeStruct((1, POOL_WIDTH), F32)],
        compiler_params=_params(("arbitrary",)),
    )(ab, ab, dya, dya, wbd, scale)


def _inproj_bwd(dx1, x, g, da, dup, dvp, dqt, dk, dv, w, tm, name):
    t, d = x.shape
    n = w.shape[1]
    nt = t // tm

    def body(dx1_ref, x_ref, g_ref, da_ref, du_ref, dv_ref, dqt_ref, dk_ref, dvv_ref, w_ref,
             dx_ref, h_ref, dp_ref, dg_ref):
        @pl.when(pl.program_id(0) == 0)
        def _():
            dg_ref[...] = jnp.zeros_like(dg_ref)

        dp = jnp.concatenate([da_ref[...], du_ref[...], dv_ref[...], dqt_ref[...].T, dk_ref[...], dvv_ref[...]],
                             axis=1).astype(BF16)
        dp_ref[...] = dp
        xx = x_ref[...]
        r = _rstd(xx)
        h_ref[...] = (xx * r * g_ref[...]).astype(BF16)
        dx, dg = _rms_bwd(xx, r, g_ref[...], _dot_nt(dp, w_ref[...]))
        dx_ref[...] = dx1_ref[...] + dx
        dg_ref[...] += dg

    row = lambda w_: pl.BlockSpec((tm, w_), lambda i: (i, 0))
    return pl.pallas_call(
        body, name=name, grid=(nt,),
        in_specs=[row(d), row(d), pl.BlockSpec((1, d), lambda i: (0, 0)), row(POOL_WIDTH), row(SG_WIDTH),
                  row(SG_WIDTH), pl.BlockSpec((SB_WIDTH, tm), lambda i: (0, i)), row(SB_WIDTH), row(SB_WIDTH),
                  pl.BlockSpec((d, n), lambda i: (0, 0))],
        out_specs=[row(d), row(d), row(n), pl.BlockSpec((1, d), lambda i: (0, 0))],
        out_shape=[jax.ShapeDtypeStruct((t, d), F32), jax.ShapeDtypeStruct((t, d), BF16),
                   jax.ShapeDtypeStruct((t, n), BF16), jax.ShapeDtypeStruct((1, d), F32)],
        compiler_params=_params(("arbitrary",)),
    )(dx1, x, g, da, dup, dvp, dqt, dk, dv, w)


def _full_w_in(gathered):
    return gathered.transpose(1, 0, 2).reshape(D_MODEL, IN_COLS)


_SMALL_SHAPES = ((D_MODEL,), (4, POOL_GW, POOL_GW), (POOL_WIDTH,), (SG_WIDTH,), (SG_HEADS, CHUNK, CHUNK),
                 (SG_HEADS, CHUNK), (D_MODEL,))
_SMALL_SIZES = tuple(functools.reduce(lambda p, q: p * q, shp) for shp in _SMALL_SHAPES)
_SMALL_ROWS = sum(_SMALL_SIZES) // 128


def _pack_small_layer(arrs):
    return jnp.concatenate([a.reshape(-1) for a in arrs]).reshape(_SMALL_ROWS, 128)


def _pack_small(arrs, final):
    return jnp.concatenate([_pack_small_layer([a[l] for a in arrs]) for l in range(DEPTH)] + [final.reshape(-1, 128)])


def _unpack_small(buf):
    per_layer = []
    for l in range(DEPTH):
        flat, off, outs = buf[l * _SMALL_ROWS:(l + 1) * _SMALL_ROWS].reshape(-1), 0, []
        for shp, size in zip(_SMALL_SHAPES, _SMALL_SIZES):
            outs.append(flat[off:off + size].reshape(shp))
            off += size
        per_layer.append(outs)
    return [jnp.stack([per_layer[l][k] for l in range(DEPTH)]) for k in range(len(_SMALL_SHAPES))] + \
           [buf[DEPTH * _SMALL_ROWS:].reshape(-1)]


def _tiles(t):
    return min(512, t), min(256, t // 4), min(2048, t)


def _layer_fwd(xl, wi, wo, wud, small_w, l, ride_mlp=None):
    n1, pw, ps, sn, sw, sb, n2 = small_w
    tm, tq, _ = _tiles(xl.shape[0])
    nb = xl.shape[0] // tq
    wbd = jnp.zeros((4, POOL_GW, 4, POOL_GW), F32)
    for gi in range(4):
        wbd = wbd.at[gi, :, gi, :].set(pw[gi])
    wbd = wbd.reshape(POOL_WIDTH, POOL_WIDTH).astype(BF16)
    wm = (sw * jnp.tril(jnp.ones((CHUNK, CHUNK), F32))).astype(BF16)
    wmt = wm.transpose(0, 2, 1)
    bfull = jnp.repeat(sb.T, SG_HD, axis=1)
    g1, g2, psc, gn = n1[None, :], n2[None, :], ps[None, :], sn[None, :]

    if isinstance(wo, Exchange):
        (ab, qkv, ktb, vtb), (wo,) = _inproj_fwd(xl, g1, wi, tm, tq, f"inproj_fwd{l}", wo)
    else:
        ab, qkv, ktb, vtb = _inproj_fwd(xl, g1, wi, tm, tq, f"inproj_fwd{l}")
    ya = _pool_fwd(ab, wbd, psc, tm, f"pool_fwd{l}")
    yb = _sg_fwd(ab, gn, wm, bfull, tm, f"sg_fwd{l}")
    if isinstance(wud, Exchange):
        (yct, cc, nvis), (wu, wd) = _sba_fwd(qkv, vtb, tq, f"sba_fwd{l}", wud)
    else:
        (yct, cc, nvis), (wu, wd) = _sba_fwd(qkv, vtb, tq, f"sba_fwd{l}"), wud
    wo = wo.reshape(D_MODEL, D_MODEL)
    res = _outproj_mlp_fwd(xl, ya, yb, yct, wo, g2, wu, wd, min(MLP_ROWS, xl.shape[0]), f"mlp_fwd{l}", ride_mlp)
    (x1, u, x2), rode = res if ride_mlp is not None else (res, None)
    saved = dict(x=xl, ab=ab, qkv=qkv, ktb=ktb, cc=cc, nvis=nvis, ya=ya, yb=yb, yct=yct, x1=x1, u=u, wi=wi, wo=wo, wu=wu,
                 wd=wd, wbd=wbd, wm=wm, wmt=wmt, bfull=bfull, g1=g1, g2=g2, psc=psc, gn=gn)
    return x2, saved, rode


def _layer_bwd(dx, s, l, ride_mlp=None, scatter_in_attn=True):
    tm, tq, tw = _tiles(dx.shape[0])
    ktb = s["ktb"]
    res = _mlp_bwd(dx, s["x1"], s["g2"], s["u"], s["wu"], s["wd"], min(MLP_ROWS, dx.shape[0]), f"mlp_bwd{l}", ride_mlp)
    (dx1, du, r, h2, dx2b, dn2), rode = res if ride_mlp is not None else (res, None)
    dw_up = _matmul_tn(h2, du, D_MODEL, FF_SHARD, tw, f"dw_up{l}", by_column_block=True)
    dw_down = _matmul_tn(r, dx2b, 1024, D_MODEL, tw, f"dw_down{l}").reshape(N_DEV, FF_SHARD, D_MODEL)
    dya, dyb, dyc, dyct, dx1b = _outproj_bwd(dx1, s["wo"], tm, f"outproj_bwd{l}")
    dw_out = jnp.concatenate([
        _matmul_tn(jnp.concatenate([s["ya"], s["yb"]], axis=1), dx1b, POOL_WIDTH + SG_WIDTH, D_MODEL, tw, f"dw_out_ab{l}"),
        _matmul_tn(s["yct"], dx1b, SB_WIDTH, D_MODEL, tw, f"dw_out_c{l}", a_transposed=True)]
    ).reshape(N_DEV, OUT_SHARD, D_MODEL)
    if scatter_in_attn:
        (dqt, dk, dv), (dw_out, dw_up, dw_down) = _sba_bwd(s["qkv"], ktb, dyc, dyct, s["cc"], s["nvis"], tq, f"sba_bwd{l}",
                                                           Exchange([dw_out, dw_up, dw_down], True))
    else:
        dqt, dk, dv = _sba_bwd(s["qkv"], ktb, dyc, dyct, s["cc"], s["nvis"], tq, f"sba_bwd{l}")
    dup, dvp, dgn, dwm, dbm = _sg_bwd(s["ab"], dyb, s["gn"], s["wm"], s["wmt"], s["bfull"], tm, f"sg_bwd{l}")
    da, dwbd, dpsc = _pool_bwd(s["ab"], dya, s["wbd"], s["psc"], tm, f"pool_bwd{l}")
    dx, h1, dproj, dn1 = _inproj_bwd(dx1, s["x"], s["g1"], da, dup, dvp, dqt, dk, dv, s["wi"], tm, f"inproj_bwd{l}")
    dw_in = _matmul_tn(h1, dproj, D_MODEL, IN_COLS // 3, tw, f"dw_in{l}")
    dw_in = dw_in.reshape(D_MODEL, N_DEV, IN_SHARD).transpose(1, 0, 2)
    dpw = jnp.stack([dwbd[gi * POOL_GW:(gi + 1) * POOL_GW, gi * POOL_GW:(gi + 1) * POOL_GW] for gi in range(4)])
    small = (dn1[0], dpw, dpsc[0], dgn[0], dwm, dbm[:, :SG_HEADS].T, dn2[0])
    return dx, (dw_in, dw_out, dw_up, dw_down), small, rode


def kernel(x, norm1, w_in, pool_w, pool_scale, sg_norm, sg_w, sg_b, w_out, norm2, w_up, w_down, final_norm, loss_target, m_norm1, m_w_in, m_pool_w, m_pool_scale, m_sg_norm, m_sg_w, m_sg_b, m_w_out, m_norm2, m_w_up, m_w_down, m_final_norm, v_norm1, v_w_in, v_pool_w, v_pool_scale, v_sg_norm, v_sg_w, v_sg_b, v_w_out, v_norm2, v_w_up, v_w_down, v_final_norm):
    t = x.shape[1]
    tm = _tiles(t)[0]
    small_w = (norm1, pool_w, pool_scale, sg_norm, sg_w, sg_b, norm2)
    big_w = (w_in, w_out, w_up, w_down)
    big_m = (m_w_in, m_w_out, m_w_up, m_w_down)
    big_v = (v_w_in, v_w_out, v_w_up, v_w_down)
    shards = [[w[l].astype(BF16) for w in big_w] for l in range(DEPTH)]

    wi0 = _full_w_in(_exchange(shards[0][:1], False, "gather_w_in0")[0])
    x1, s0, g1 = _layer_fwd(x.reshape(t, D_MODEL), wi0, Exchange(shards[0][1:2], False), Exchange(shards[0][2:], False),
                            tuple(w[0] for w in small_w), 0, ride_mlp=Exchange(shards[1][:2], False))
    x2, s1, _ = _layer_fwd(x1, _full_w_in(g1[0]), g1[1], Exchange(shards[1][2:], False), tuple(w[1] for w in small_w), 1)
    loss_local, dx, dfinal = _loss_grad(x2, final_norm[None, :], loss_target.reshape(t, D_MODEL), tm, "loss_grad")
    loss = lax.psum(loss_local[0, 0], MESH_AXES)

    dx, parts1, small1, _ = _layer_bwd(dx, s1, 1)
    early = jnp.concatenate([_pack_small_layer(small1), dfinal.reshape(-1, 128)])
    dx, parts0, small0, (recv_in1, early) = _layer_bwd(dx, s0, 0, ride_mlp=Exchange([parts1[0], early], [True, False]))
    grad_x = dx.reshape(x.shape)
    recv_in0, late = _exchange([parts0[0], _pack_small_layer(small0)], [True, False], "scatter_w_in0_gather_small0")
    small_all = jnp.concatenate([late, early], axis=1)
    received = [[recv_in0] + list(parts0[1:]), [recv_in1] + list(parts1[1:])]

    big = [None] * 4
    for l in reversed(range(DEPTH)):
        for k in range(4):
            big[k] = _reduce_adamw(received[l][k], big_w[k], big_m[k], big_v[k], l, big[k], f"adamw{k}_{l}")

    sm = _reduce_adamw(
        small_all,
        _pack_small(small_w, final_norm)[None],
        _pack_small([m_norm1, m_pool_w, m_pool_scale, m_sg_norm, m_sg_w, m_sg_b, m_norm2], m_final_norm)[None],
        _pack_small([v_norm1, v_pool_w, v_pool_scale, v_sg_norm, v_sg_w, v_sg_b, v_norm2], v_final_norm)[None],
        0, None, "adamw_replicated")

    out = [loss, grad_x]
    for k in range(4):
        n1, pw, ps, sn, sw, sb, n2, fn = _unpack_small(sm[k][0])
        out += [n1, big[0][k], pw, ps, sn, sw, sb, big[1][k], n2, big[2][k], big[3][k], fn]
    return tuple(out)
```

```python
import functools

import jax
import jax.numpy as jnp
from jax import lax
from jax.experimental import pallas as pl
from jax.experimental.pallas import tpu as pltpu

F32 = jnp.float32
BF16 = jnp.bfloat16

D_MODEL = 1024
DEPTH = 2
POOL_WIDTH = 256
SG_WIDTH = 256
SB_WIDTH = 512
POOL_WINDOWS = (2, 4, 8, 16)
POOL_GW = 64
POOL_HALO = 16
CHUNK = 128
SG_HEADS = 4
SG_HD = 64
SB_HD = 64
SB_PAIRS = SB_WIDTH // 128
AB_COLS = POOL_WIDTH + 2 * SG_WIDTH
IN_COLS = AB_COLS + 3 * SB_WIDTH
D_FF = 4096
EPS = 1e-6
N_DEV = 8
FF_SHARD = D_FF // N_DEV
IN_SHARD = IN_COLS // N_DEV
OUT_SHARD = D_MODEL // N_DEV
ADAM_LR = 0.001
ADAM_B1 = 0.9
ADAM_B2 = 0.999
ADAM_EPS = 1e-08
ADAM_WD = 0.01
ADAM_STEP = 10
VMEM_LIMIT = 56 * 1024 * 1024
MLP_ROWS = 1024
MESH_AXES = ("x", "y", "c")


def _dot(a, b):
    return jnp.dot(a, b, preferred_element_type=F32)


def _dot_nt(a, b):
    return lax.dot_general(a, b, (((1,), (1,)), ((), ())), preferred_element_type=F32)


def _dot_tn(a, b):
    return lax.dot_general(a, b, (((0,), (0,)), ((), ())), preferred_element_type=F32)


def _rstd(x):
    return lax.rsqrt(jnp.mean(x * x, axis=-1, keepdims=True) + EPS)


def _rms_bwd(x, r, g, dh):
    gq = dh * g
    dx = r * gq - x * (r * r * r) * jnp.mean(gq * x, axis=-1, keepdims=True)
    dg = jnp.sum(dh * x * r, axis=0, keepdims=True)
    return dx, dg


def _params(sem=None):
    kw = dict(vmem_limit_bytes=VMEM_LIMIT)
    if sem is not None:
        kw["dimension_semantics"] = sem
    return pltpu.CompilerParams(**kw)


def _row_tile(rows, cap):
    best = 8
    for t in range(8, min(rows, cap) + 1, 8):
        if rows % t == 0:
            best = t
    return best


def _peer(k):
    x, y, c = lax.axis_index("x"), lax.axis_index("y"), lax.axis_index("c")
    return (1 - x if k & 4 else x, 1 - y if k & 2 else y, 1 - c if k & 1 else c)


def _my_index():
    return 4 * lax.axis_index("x") + 2 * lax.axis_index("y") + lax.axis_index("c")


class Exchange:
    def __init__(self, arrs, scatter):
        self.arrs = list(arrs)
        self.scatter = list(scatter) if isinstance(scatter, (list, tuple)) else [scatter] * len(self.arrs)
        self.n = len(self.arrs)
        self.any_specs = [pl.BlockSpec(memory_space=pl.ANY)] * self.n
        self.out_shape = [jax.ShapeDtypeStruct((N_DEV,) + a.shape[-2:], a.dtype) for a in self.arrs]
        self.sems = [pltpu.SemaphoreType.DMA((self.n, N_DEV - 1)), pltpu.SemaphoreType.DMA((self.n, N_DEV - 1)),
                     pltpu.SemaphoreType.DMA((self.n,))]

    def _copies(self, ins, outs, sems):
        send_sems, recv_sems, local_sems = sems
        me = _my_index()
        local, remote = [], []
        for a in range(self.n):
            sc = self.scatter[a]
            local.append(pltpu.make_async_copy(ins[a].at[me] if sc else ins[a], outs[a].at[me], local_sems.at[a]))
            for k in range(1, N_DEV):
                px, py, pc = _peer(k)
                remote.append(pltpu.make_async_remote_copy(
                    src_ref=ins[a].at[4 * px + 2 * py + pc] if sc else ins[a], dst_ref=outs[a].at[me],
                    send_sem=send_sems.at[a, k - 1], recv_sem=recv_sems.at[a, k - 1],
                    device_id=(px, py, pc), device_id_type=pl.DeviceIdType.MESH))
        return local, remote

    def start(self, ins, outs, sems):
        local, remote = self._copies(ins, outs, sems)
        for cp in local + remote:
            cp.start()

    def wait(self, ins, outs, sems):
        local, remote = self._copies(ins, outs, sems)
        for cp in remote:
            cp.wait_recv()
        for cp in remote:
            cp.wait_send()
        for cp in local:
            cp.wait()

    def alone(self, name):
        n = self.n

        def body(*refs):
            ins, outs, sems = refs[:n], refs[n:2 * n], refs[2 * n:]
            self.start(ins, outs, sems)
            self.wait(ins, outs, sems)

        return pl.pallas_call(body, name=name, out_shape=self.out_shape, in_specs=self.any_specs,
                              out_specs=self.any_specs, scratch_shapes=self.sems)(*self.arrs)


def _exchange(arrs, scatter, name):
    return Exchange(arrs, scatter).alone(name)


def _call(body, name, grid, in_specs, out_specs, out_shape, scratch_shapes, semantics, args, ride=None):
    if ride is None:
        return pl.pallas_call(body, name=name, grid=grid, in_specs=in_specs, out_specs=out_specs, out_shape=out_shape,
                              scratch_shapes=scratch_shapes, compiler_params=_params(semantics))(*args)
    single = not isinstance(out_shape, (list, tuple))
    out_specs, out_shape = ([out_specs], [out_shape]) if single else (list(out_specs), list(out_shape))
    n_in, n_out, n_scr, n = len(in_specs), len(out_specs), len(scratch_shapes), ride.n

    def riding(*refs):
        ins, cins = refs[:n_in], refs[n_in:n_in + n]
        outs, couts = refs[n_in + n:n_in + n + n_out], refs[n_in + n + n_out:n_in + 2 * n + n_out]
        scr, sems = refs[n_in + 2 * n + n_out:n_in + 2 * n + n_out + n_scr], refs[n_in + 2 * n + n_out + n_scr:]
        first = functools.reduce(lambda p, q: p & q, [pl.program_id(a) == 0 for a in range(len(grid))])
        last = functools.reduce(lambda p, q: p & q, [pl.program_id(a) == g - 1 for a, g in enumerate(grid)])

        @pl.when(first)
        def _():
            ride.start(cins, couts, sems)

        body(*ins, *outs, *scr)

        @pl.when(last)
        def _():
            ride.wait(cins, couts, sems)

    res = pl.pallas_call(
        riding, name=name, grid=grid, in_specs=list(in_specs) + ride.any_specs, out_specs=out_specs + ride.any_specs,
        out_shape=out_shape + ride.out_shape, scratch_shapes=list(scratch_shapes) + ride.sems,
        compiler_params=_params(("arbitrary",) * len(grid)))(*args, *ride.arrs)
    own = res[0] if single else list(res[:n_out])
    return own, list(res[n_out:])


def _reduce_adamw(parts, w, m, v, l, prev, name):
    _, rows, n = parts.shape
    tr = _row_tile(rows, max(8, (1 << 18) // n))
    c1 = 1.0 - ADAM_B1 ** ADAM_STEP
    c2 = 1.0 - ADAM_B2 ** ADAM_STEP

    def body(p_ref, w_ref, m_ref, v_ref, *rest):
        g_ref, d_ref, nm_ref, nv_ref = rest[-4:]
        g = p_ref[0].astype(F32)
        for s in range(1, N_DEV):
            g = g + p_ref[s].astype(F32)
        nm = ADAM_B1 * m_ref[...] + (1.0 - ADAM_B1) * g
        nv = ADAM_B2 * v_ref[...] + (1.0 - ADAM_B2) * (g * g)
        m_hat = nm / c1
        v_hat = nv / c2
        g_ref[...] = g
        d_ref[...] = -ADAM_LR * (m_hat / (jnp.sqrt(v_hat) + ADAM_EPS) + ADAM_WD * w_ref[...])
        nm_ref[...] = nm
        nv_ref[...] = nv

    blk = pl.BlockSpec((None, tr, n), lambda i: (l, i, 0))
    out = jax.ShapeDtypeStruct(w.shape, F32)
    prev = list(prev) if prev is not None else []
    return pl.pallas_call(
        body, name=name, grid=(rows // tr,),
        in_specs=[pl.BlockSpec((N_DEV, tr, n), lambda i: (0, i, 0)), blk, blk, blk] + [pl.BlockSpec(memory_space=pl.ANY)] * len(prev),
        out_specs=[blk, blk, blk, blk], out_shape=[out, out, out, out],
        input_output_aliases={4 + k: k for k in range(len(prev))},
        compiler_params=_params(("parallel",)),
    )(parts, w, m, v, *prev)


def _inproj_fwd(x, g, w, tm, tq, name, ride=None):
    t, d = x.shape
    n = w.shape[1]
    nb = t // tq
    per = tm // tq

    def body(x_ref, g_ref, w_ref, ab_ref, qkv_ref, kt_ref, vt_ref):
        xx = x_ref[...]
        h = (xx * _rstd(xx) * g_ref[...]).astype(BF16)
        ab_ref[...] = _dot(h, w_ref[:, :AB_COLS])
        qkv = _dot(h, w_ref[:, AB_COLS:])
        qkv_ref[...] = qkv.astype(BF16)
        for which, out_ref in ((1, kt_ref), (2, vt_ref)):
            for p in range(SB_PAIRS):
                for b in range(per):
                    cols = which * SB_WIDTH + p * 128
                    out_ref[p, b] = qkv[b * tq:(b + 1) * tq, cols:cols + 128].T.astype(BF16)

    tb = pl.BlockSpec((SB_PAIRS, per, 128, tq), lambda i: (0, i, 0, 0))
    tshape = jax.ShapeDtypeStruct((SB_PAIRS, nb, 128, tq), BF16)
    return _call(
        body, name, (t // tm,),
        [pl.BlockSpec((tm, d), lambda i: (i, 0)), pl.BlockSpec((1, d), lambda i: (0, 0)),
         pl.BlockSpec((d, n), lambda i: (0, 0))],
        [pl.BlockSpec((tm, AB_COLS), lambda i: (i, 0)), pl.BlockSpec((tm, n - AB_COLS), lambda i: (i, 0)), tb, tb],
        [jax.ShapeDtypeStruct((t, AB_COLS), F32), jax.ShapeDtypeStruct((t, n - AB_COLS), BF16), tshape, tshape],
        [], ("parallel",), (x, g, w), ride)


def _pool_window_sums(xx, forward):
    n = xx.shape[0]
    sh = (lambda k: n - k) if forward else (lambda k: k)
    s2 = xx + pltpu.roll(xx, sh(1), 0)
    s4 = s2 + pltpu.roll(s2, sh(2), 0)
    s8 = s4 + pltpu.roll(s4, sh(4), 0)
    s16 = s8 + pltpu.roll(s8, sh(8), 0)
    grp = lax.broadcasted_iota(jnp.int32, (1, POOL_WIDTH), 1) // POOL_GW
    return jnp.where(grp == 0, s2, jnp.where(grp == 1, s4, jnp.where(grp == 2, s8, s16)))


def _pool_count(t0, rows):
    grp = lax.broadcasted_iota(jnp.int32, (1, POOL_WIDTH), 1) // POOL_GW
    win = jnp.where(grp == 0, 2, jnp.where(grp == 1, 4, jnp.where(grp == 2, 8, 16)))
    tt = t0 + lax.broadcasted_iota(jnp.int32, (rows, 1), 0)
    return jnp.minimum(tt + 1, win).astype(F32)


def _pool_diff(cur, prev, t0):
    tm = cur.shape[0]
    sums = _pool_window_sums(jnp.concatenate([prev, cur], axis=0), False)[POOL_HALO:]
    return sums / _pool_count(t0, tm) - cur


def _pool_fwd(ab, wbd, scale, tm, name):
    t = ab.shape[0]
    hb = tm // POOL_HALO

    def body(cur_ref, prev_ref, w_ref, s_ref, y_ref):
        i = pl.program_id(0)
        prev = jnp.where(i == 0, 0.0, prev_ref[...])
        d = _pool_diff(cur_ref[...], prev, i * tm).astype(BF16)
        y_ref[...] = (_dot(d, w_ref[...]) * s_ref[...]).astype(BF16)

    return pl.pallas_call(
        body, name=name, grid=(t // tm,),
        in_specs=[pl.BlockSpec((tm, POOL_WIDTH), lambda i: (i, 0)),
                  pl.BlockSpec((POOL_HALO, POOL_WIDTH), lambda i: (jnp.maximum(i * hb - 1, 0), 0)),
                  pl.BlockSpec((POOL_WIDTH, POOL_WIDTH), lambda i: (0, 0)),
                  pl.BlockSpec((1, POOL_WIDTH), lambda i: (0, 0))],
        out_specs=pl.BlockSpec((tm, POOL_WIDTH), lambda i: (i, 0)),
        out_shape=jax.ShapeDtypeStruct((t, POOL_WIDTH), BF16),
        compiler_params=_params(("parallel",)),
    )(ab, ab, wbd, scale)


_GELU_K = 0.7978845608028654
_GELU_A = 0.044715


def _gelu(x):
    return 0.5 * x * (1.0 + jnp.tanh(_GELU_K * (x + _GELU_A * (x * x * x))))


def _gelu_grad(x):
    th = jnp.tanh(_GELU_K * (x + _GELU_A * (x * x * x)))
    return 0.5 * (1.0 + th) + 0.5 * x * (1.0 - th * th) * (_GELU_K * (1.0 + 3.0 * _GELU_A * (x * x)))


def _head_lanes(h):
    return lax.broadcasted_iota(jnp.int32, (1, SG_WIDTH), 1) // SG_HD == h


def _sg_fwd(ab, gn, wm, bfull, tm, name):
    t = ab.shape[0]

    def body(u_ref, v_ref, gn_ref, wm_ref, b_ref, y_ref):
        v = _gelu(v_ref[...])
        vn = (v * _rstd(v) * gn_ref[...]).astype(BF16)
        for c in range(tm // CHUNK):
            rows = slice(c * CHUNK, (c + 1) * CHUNK)
            vc = vn[rows]
            sv = b_ref[...]
            for h in range(SG_HEADS):
                sv = sv + jnp.where(_head_lanes(h), _dot(wm_ref[h], vc), 0.0)
            y_ref[rows, :] = (_gelu(u_ref[rows, :]) * sv).astype(BF16)

    return pl.pallas_call(
        body, name=name, grid=(t // tm,),
        in_specs=[pl.BlockSpec((tm, SG_WIDTH), lambda i: (i, 1)), pl.BlockSpec((tm, SG_WIDTH), lambda i: (i, 2)),
                  pl.BlockSpec((1, SG_WIDTH), lambda i: (0, 0)),
                  pl.BlockSpec((SG_HEADS, CHUNK, CHUNK), lambda i: (0, 0, 0)),
                  pl.BlockSpec((CHUNK, SG_WIDTH), lambda i: (0, 0))],
        out_specs=pl.BlockSpec((tm, SG_WIDTH), lambda i: (i, 0)),
        out_shape=jax.ShapeDtypeStruct((t, SG_WIDTH), BF16),
        compiler_params=_params(("parallel",)),
    )(ab, ab, gn, wm, bfull)


LOG2E = 1.4426950408889634
SB_SCALE = 0.125 * LOG2E
SB_DEAD_LOG2 = 152.0
SB_QUERY_BLOCKS_PER_STEP = 2


def _log2_sigmoids(y):
    neg_abs = lax.bitcast_convert_type(lax.bitcast_convert_type(y, jnp.uint32) | jnp.uint32(0x80000000), F32)
    lb = jnp.minimum(y, 0.0) - jnp.log(1.0 + jnp.exp2(neg_abs)) * LOG2E
    return lb, lb - y


def _split(x):
    hi = x.astype(BF16)
    return hi, (x - hi.astype(F32)).astype(BF16)


def _tri_dot(tri, x):
    hi, lo = _split(x)
    return _dot(tri, hi) + _dot(tri, lo)


def _sba_fwd(qkv, vtb, tq, name, ride=None):
    t = qkv.shape[0]
    nb = t // tq
    qb = SB_QUERY_BLOCKS_PER_STEP
    upper = (jnp.arange(tq)[None, :] > jnp.arange(tq)[:, None]).astype(BF16)

    def body(q_ref, k_ref, vt_ref, up_ref, ot_ref, c_ref, n_ref):
        for sub in range(qb):
            query_block(pl.program_id(1) * qb + sub, slice(sub * tq, (sub + 1) * tq), q_ref, k_ref, vt_ref, up_ref,
                        ot_ref, c_ref, n_ref)

    def query_block(i, mine, q_ref, k_ref, vt_ref, up_ref, ot_ref, c_ref, n_ref):
        q = q_ref[mine, :]
        up = up_ref[...]
        lane_head = lax.broadcasted_iota(jnp.int32, (1, 128), 1) // SB_HD
        sub_head = lax.broadcasted_iota(jnp.int32, (128, 1), 0) // SB_HD
        causal = (lax.broadcasted_iota(jnp.int32, (tq, tq), 0) < lax.broadcasted_iota(jnp.int32, (tq, tq), 1))
        qh = [jnp.where(lane_head == h, q, jnp.zeros_like(q)) for h in range(2)]

        def blocks(js, carry):
            cs, acc = list(carry[:2]), carry[2]
            kj = [k_ref[pl.ds(pl.multiple_of(j * tq, tq), tq), :] for j, _ in js]
            vt = [vt_ref[j] for j, _ in js]
            chains = [(b, h) for b in range(len(js)) for h in range(2)]
            z = [_dot_nt(kj[b], qh[h]) for b, h in chains]
            ls = [_log2_sigmoids(zz * SB_SCALE) for zz in z]
            lb = [x[0] for x in ls]
            l1 = [jnp.where(causal, x[1], 0.0) if js[b][1] else x[1] for x, (b, h) in zip(ls, chains)]
            after = [_tri_dot(up, x) for x in l1]
            a = []
            for n, (b, h) in enumerate(chains):
                c_ref[h, js[b][0], :, mine] = cs[h]
                an = jnp.exp2(lb[n] + after[n] + cs[h])
                a.append(jnp.where(causal, an, 0.0) if js[b][1] else an)
                cs[h] = cs[h] + after[n][0:1, :] + l1[n][0:1, :]
            for n, (b, h) in enumerate(chains):
                acc = acc + _dot(jnp.where(sub_head == h, vt[b], jnp.zeros_like(vt[b])), a[n].astype(BF16))
            return cs[0], cs[1], acc

        def alive(c0, c1):
            return (jnp.max(jnp.maximum(c0, c1)) > -SB_DEAD_LOG2).astype(jnp.int32)

        def step(state):
            s, _, c0, c1, acc = state
            c0, c1, acc = blocks([(i - 1 - s, False)], (c0, c1, acc))
            return s + 1, alive(c0, c1), c0, c1, acc

        zero = jnp.zeros((1, tq), F32)
        start = (zero, zero, jnp.zeros((128, tq), F32))
        c0, c1, acc = lax.cond(i > 0, lambda: blocks([(i, True), (i - 1, False)], start), lambda: blocks([(i, True)], start))
        first = jnp.minimum(i, 1)
        state = lax.while_loop(lambda st: (st[0] < i) & (st[1] > 0), step, (first, alive(c0, c1), c0, c1, acc))
        ot_ref[:, mine] = state[4].astype(BF16)
        n_ref[pl.program_id(0), i] = (state[0] + 1).astype(F32)

    return _call(
        body, name, (SB_PAIRS, nb // qb),
        [pl.BlockSpec((qb * tq, 128), lambda p, i: (i, p)),
         pl.BlockSpec((t, 128), lambda p, i: (0, SB_PAIRS + p)),
         pl.BlockSpec((None, nb, 128, tq), lambda p, i: (p, 0, 0, 0)),
         pl.BlockSpec((tq, tq), lambda p, i: (0, 0))],
        [pl.BlockSpec((128, qb * tq), lambda p, i: (p, i)),
         pl.BlockSpec((2, nb, 1, qb * tq), lambda p, i: (p, 0, 0, i)),
         pl.BlockSpec(memory_space=pltpu.SMEM)],
        [jax.ShapeDtypeStruct((SB_WIDTH, t), BF16),
         jax.ShapeDtypeStruct((2 * SB_PAIRS, nb, 1, t), F32),
         jax.ShapeDtypeStruct((SB_PAIRS, nb), F32)],
        [], ("arbitrary", "arbitrary"), (qkv, qkv, vtb, upper), ride)


def _outproj_mlp_fwd(x, ya, yb, yct, wo, g2, wup, wdn, tm, name, ride=None):
    t, d = x.shape
    nf = wup.shape[0]

    def body(x_ref, ya_ref, yb_ref, yct_ref, wo_ref, g_ref, wu_ref, wd_ref, x1_ref, u_ref, x2_ref, h_ref, acc_ref):
        j = pl.program_id(1)

        @pl.when(j == 0)
        def _():
            x1 = (x_ref[...] + _dot(ya_ref[...], wo_ref[0:POOL_WIDTH, :])
                  + _dot(yb_ref[...], wo_ref[POOL_WIDTH:POOL_WIDTH + SG_WIDTH, :])
                  + _dot_tn(yct_ref[...], wo_ref[POOL_WIDTH + SG_WIDTH:, :]))
            x1_ref[...] = x1
            h_ref[...] = (x1 * _rstd(x1) * g_ref[...]).astype(BF16)
            acc_ref[...] = x1

        u = _dot(h_ref[...], wu_ref[...])
        u_ref[...] = u
        r = jnp.maximum(u, 0.0)
        acc_ref[...] += _dot((r * r).astype(BF16), wd_ref[...])

        @pl.when(j == nf - 1)
        def _():
            x2_ref[...] = acc_ref[...]

    row = lambda w: pl.BlockSpec((tm, w), lambda i, j: (i, 0))
    return _call(
        body, name, (t // tm, nf),
        [row(d), row(POOL_WIDTH), row(SG_WIDTH), pl.BlockSpec((SB_WIDTH, tm), lambda i, j: (0, i)),
         pl.BlockSpec((d, d), lambda i, j: (0, 0)), pl.BlockSpec((1, d), lambda i, j: (0, 0)),
         pl.BlockSpec((None, d, FF_SHARD), lambda i, j: (j, 0, 0)),
         pl.BlockSpec((None, FF_SHARD, d), lambda i, j: (j, 0, 0))],
        [row(d), pl.BlockSpec((tm, FF_SHARD), lambda i, j: (i, j)), row(d)],
        [jax.ShapeDtypeStruct((t, d), F32), jax.ShapeDtypeStruct((t, nf * FF_SHARD), F32),
         jax.ShapeDtypeStruct((t, d), F32)],
        [pltpu.VMEM((tm, d), BF16), pltpu.VMEM((tm, d), F32)],
        ("parallel", "arbitrary"), (x, ya, yb, yct, wo, g2, wup, wdn), ride)


def _loss_grad(x, g, target, tm, name):
    t, d = x.shape
    nt = t // tm

    def body(x_ref, g_ref, t_ref, loss_ref, dx_ref, dg_ref, sq_ref):
        i = pl.program_id(0)

        @pl.when(i == 0)
        def _():
            sq_ref[...] = jnp.zeros_like(sq_ref)
            dg_ref[...] = jnp.zeros_like(dg_ref)

        xx = x_ref[...]
        r = _rstd(xx)
        err = xx * r * g_ref[...] - t_ref[...]
        sq_ref[...] += jnp.sum(err * err, axis=0, keepdims=True)
        dx, dg = _rms_bwd(xx, r, g_ref[...], err * (1.0 / d))
        dx_ref[...] = dx
        dg_ref[...] += dg

        @pl.when(i == nt - 1)
        def _():
            loss_ref[...] = jnp.sum(sq_ref[...], axis=1, keepdims=True) * (0.5 / d)

    return pl.pallas_call(
        body, name=name, grid=(nt,),
        in_specs=[pl.BlockSpec((tm, d), lambda i: (i, 0)), pl.BlockSpec((1, d), lambda i: (0, 0)),
                  pl.BlockSpec((tm, d), lambda i: (i, 0))],
        out_specs=[pl.BlockSpec((1, 1), lambda i: (0, 0)), pl.BlockSpec((tm, d), lambda i: (i, 0)),
                   pl.BlockSpec((1, d), lambda i: (0, 0))],
        out_shape=[jax.ShapeDtypeStruct((1, 1), F32), jax.ShapeDtypeStruct((t, d), F32),
                   jax.ShapeDtypeStruct((1, d), F32)],
        scratch_shapes=[pltpu.VMEM((1, d), F32)],
        compiler_params=_params(("arbitrary",)),
    )(x, g, target)


def _mlp_bwd(dx2, x1, g2, u, wup, wdn, tm, name, ride=None):
    t, d = dx2.shape
    nf = wup.shape[0]
    nt = t // tm

    def body(dx2_ref, x1_ref, g_ref, u_ref, wu_ref, wd_ref, dx1_ref, du_ref, r_ref, h_ref, dxb_ref, dg_ref, acc_ref):
        i, j = pl.program_id(0), pl.program_id(1)

        @pl.when(j == 0)
        def _():
            x1 = x1_ref[...]
            h_ref[...] = (x1 * _rstd(x1) * g_ref[...]).astype(BF16)
            dxb_ref[...] = dx2_ref[...].astype(BF16)
            acc_ref[...] = jnp.zeros_like(acc_ref)

        @pl.when((i == 0) & (j == 0))
        def _():
            dg_ref[...] = jnp.zeros_like(dg_ref)

        dr = _dot_nt(dxb_ref[...], wd_ref[...])
        ru = jnp.maximum(u_ref[...], 0.0)
        du = (dr * (2.0 * ru)).astype(BF16)
        du_ref[...] = du
        r_ref[...] = (ru * ru).astype(BF16)
        acc_ref[...] += _dot_nt(du, wu_ref[...])

        @pl.when(j == nf - 1)
        def _():
            x1 = x1_ref[...]
            dx, dg = _rms_bwd(x1, _rstd(x1), g_ref[...], acc_ref[...])
            dx1_ref[...] = dx2_ref[...] + dx
            dg_ref[...] += dg

    row = lambda w: pl.BlockSpec((tm, w), lambda i, j: (i, 0))
    col = pl.BlockSpec((tm, FF_SHARD), lambda i, j: (i, j))
    return _call(
        body, name, (nt, nf),
        [row(d), row(d), pl.BlockSpec((1, d), lambda i, j: (0, 0)), col,
         pl.BlockSpec((None, d, FF_SHARD), lambda i, j: (j, 0, 0)),
         pl.BlockSpec((None, FF_SHARD, d), lambda i, j: (j, 0, 0))],
        [row(d), col, col, row(d), row(d), pl.BlockSpec((1, d), lambda i, j: (0, 0))],
        [jax.ShapeDtypeStruct((t, d), F32), jax.ShapeDtypeStruct((t, nf * FF_SHARD), BF16),
         jax.ShapeDtypeStruct((t, nf * FF_SHARD), BF16), jax.ShapeDtypeStruct((t, d), BF16),
         jax.ShapeDtypeStruct((t, d), BF16), jax.ShapeDtypeStruct((1, d), F32)],
        [pltpu.VMEM((tm, d), F32)], ("arbitrary", "arbitrary"), (dx2, x1, g2, u, wup, wdn), ride)


def _matmul_tn(a, b, bm, bn, bt, name, by_column_block=0, a_transposed=False):
    m, t = a.shape if a_transposed else a.shape[::-1]
    n = b.shape[1]
    nk = t // bt

    def body(a_ref, b_ref, o_ref, acc_ref):
        k = pl.program_id(2)

        @pl.when(k == 0)
        def _():
            acc_ref[...] = jnp.zeros_like(acc_ref)

        acc_ref[...] += _dot(a_ref[...], b_ref[...]) if a_transposed else _dot_tn(a_ref[...], b_ref[...])

        @pl.when(k == nk - 1)
        def _():
            if by_column_block:
                for c in range(bn // by_column_block):
                    o_ref[c] = acc_ref[:, c * by_column_block:(c + 1) * by_column_block].astype(BF16)
            else:
                o_ref[...] = acc_ref[...].astype(BF16)

    if by_column_block:
        out_spec = pl.BlockSpec((bn // by_column_block, bm, by_column_block), lambda i, j, k: (j, i, 0))
        out_shape = jax.ShapeDtypeStruct((n // by_column_block, m, by_column_block), BF16)
    else:
        out_spec = pl.BlockSpec((bm, bn), lambda i, j, k: (i, j))
        out_shape = jax.ShapeDtypeStruct((m, n), BF16)
    return pl.pallas_call(
        body, name=name, grid=(m // bm, n // bn, nk),
        in_specs=[pl.BlockSpec((bm, bt), lambda i, j, k: (i, k)) if a_transposed else pl.BlockSpec((bt, bm), lambda i, j, k: (k, i)),
                  pl.BlockSpec((bt, bn), lambda i, j, k: (k, j))],
        out_specs=out_spec, out_shape=out_shape,
        scratch_shapes=[pltpu.VMEM((bm, bn), F32)],
        compiler_params=_params(("parallel", "parallel", "arbitrary")),
    )(a, b)


def _outproj_bwd(dx1, wo, tm, name):
    t, d = dx1.shape
    c2 = POOL_WIDTH + SG_WIDTH

    def body(dx_ref, wo_ref, dya_ref, dyb_ref, dyc_ref, dyct_ref, dxb_ref):
        dxb = dx_ref[...].astype(BF16)
        dxb_ref[...] = dxb
        dya_ref[...] = _dot_nt(dxb, wo_ref[0:POOL_WIDTH, :])
        dyb_ref[...] = _dot_nt(dxb, wo_ref[POOL_WIDTH:c2, :])
        dyc_ref[...] = _dot_nt(dxb, wo_ref[c2:, :]).astype(BF16)
        dyct_ref[...] = _dot_nt(wo_ref[c2:, :], dxb).astype(BF16)

    row = lambda w: pl.BlockSpec((tm, w), lambda i: (i, 0))
    return pl.pallas_call(
        body, name=name, grid=(t // tm,),
        in_specs=[row(d), pl.BlockSpec((d, d), lambda i: (0, 0))],
        out_specs=[row(POOL_WIDTH), row(SG_WIDTH), row(SB_WIDTH), pl.BlockSpec((SB_WIDTH, tm), lambda i: (0, i)), row(d)],
        out_shape=[jax.ShapeDtypeStruct((t, POOL_WIDTH), F32), jax.ShapeDtypeStruct((t, SG_WIDTH), F32),
                   jax.ShapeDtypeStruct((t, SB_WIDTH), BF16), jax.ShapeDtypeStruct((SB_WIDTH, t), BF16),
                   jax.ShapeDtypeStruct((t, d), BF16)],
        compiler_params=_params(("parallel",)),
    )(dx1, wo)


def _sba_bwd(qkv, ktb, dyc, dyct, cc, nvis, tq, name, ride=None):
    t = qkv.shape[0]
    nb = t // tq
    idx = jnp.arange(tq)
    upper = (idx[None, :] > idx[:, None]).astype(BF16)
    lower = (idx[None, :] < idx[:, None]).astype(BF16)

    qb = SB_QUERY_BLOCKS_PER_STEP

    def body(q_ref, k_ref, v_ref, kt_ref, do_ref, dot_ref, c_ref, n_ref, up_ref, lo_ref, dqt_ref, dk_ref, dv_ref):
        @pl.when(pl.program_id(1) == 0)
        def _():
            dk_ref[...] = jnp.zeros_like(dk_ref)
            dv_ref[...] = jnp.zeros_like(dv_ref)

        for sub in range(qb):
            query_block(pl.program_id(1) * qb + sub, slice(sub * tq, (sub + 1) * tq), q_ref, k_ref, v_ref, kt_ref,
                        do_ref, dot_ref, c_ref, n_ref, up_ref, lo_ref, dqt_ref, dk_ref, dv_ref)

    def query_block(i, mine, q_ref, k_ref, v_ref, kt_ref, do_ref, dot_ref, c_ref, n_ref, up_ref, lo_ref,
                    dqt_ref, dk_ref, dv_ref):
        q = q_ref[mine, :]
        do = do_ref[mine, :]
        dot = dot_ref[:, mine]
        up = up_ref[...]
        lo = lo_ref[...]
        lane_head = lax.broadcasted_iota(jnp.int32, (1, 128), 1) // SB_HD
        sub_head = lax.broadcasted_iota(jnp.int32, (128, 1), 0) // SB_HD
        causal = (lax.broadcasted_iota(jnp.int32, (tq, tq), 0) < lax.broadcasted_iota(jnp.int32, (tq, tq), 1))
        hms = [lane_head == h for h in range(2)]
        qh = [jnp.where(hm, q, jnp.zeros_like(q)) for hm in hms]
        qs = [x * 0.125 for x in qh]
        doh = [jnp.where(hm, do, jnp.zeros_like(do)) for hm in hms]

        def blocks(js, carry):
            cgs, dqt = list(carry[:2]), carry[2]
            rows = [pl.ds(pl.multiple_of(j * tq, tq), tq) for j, _ in js]
            kj = [k_ref[r, :] for r in rows]
            vj = [v_ref[r, :] for r in rows]
            kt = [kt_ref[j] for j, _ in js]
            chains = [(b, h) for b in range(len(js)) for h in range(2)]
            z = [_dot_nt(kj[b], qh[h]) for b, h in chains]
            da = [_dot(jnp.where(hms[h], vj[b], jnp.zeros_like(vj[b])), dot) for b, h in chains]
            ls = [_log2_sigmoids(zz * SB_SCALE) for zz in z]
            lb = [x[0] for x in ls]
            l1 = [jnp.where(causal, x[1], 0.0) if js[b][1] else x[1] for x, (b, h) in zip(ls, chains)]
            after = [_tri_dot(up, x) for x in l1]
            a = [jnp.exp2(lb[n] + after[n] + c_ref[h, js[b][0], :, mine]) for n, (b, h) in enumerate(chains)]
            a = [jnp.where(causal, a[n], 0.0) if js[b][1] else a[n] for n, (b, h) in enumerate(chains)]
            g = [a[n] * da[n] for n in range(len(chains))]
            gloc = [_tri_dot(lo, x) for x in g]
            dzb = []
            for n, (b, h) in enumerate(chains):
                gsum = gloc[n] + cgs[h]
                dz = g[n] - jnp.exp2(lb[n]) * (g[n] + gsum)
                dzb.append((jnp.where(causal, dz, 0.0) if js[b][1] else dz).astype(BF16))
                cgs[h] = gsum[tq - 1:tq, :] + g[n][tq - 1:tq, :]
            ab = [x.astype(BF16) for x in a]
            for n, (b, h) in enumerate(chains):
                dqt = dqt + _dot(jnp.where(sub_head == h, kt[b], jnp.zeros_like(kt[b])), dzb[n])
            for b in range(len(js)):
                dk_ref[rows[b], :] += _dot(dzb[2 * b], qs[0]) + _dot(dzb[2 * b + 1], qs[1])
                dv_ref[rows[b], :] += _dot(ab[2 * b], doh[0]) + _dot(ab[2 * b + 1], doh[1])
            return cgs[0], cgs[1], dqt

        n = jnp.clip(n_ref[pl.program_id(0), i].astype(jnp.int32), 1, i + 1)
        zero = jnp.zeros((1, tq), F32)
        carry = lax.fori_loop(jnp.minimum(i + 1 - n, i - 1), i - 1, lambda s, cr: blocks([(s, False)], cr),
                              (zero, zero, jnp.zeros((128, tq), F32)))
        carry = lax.cond(i > 0, lambda: blocks([(i - 1, False), (i, True)], carry), lambda: blocks([(i, True)], carry))
        dqt_ref[:, mine] = carry[2] * 0.125

    return _call(
        body, name, (SB_PAIRS, nb // qb),
        [pl.BlockSpec((qb * tq, 128), lambda p, i: (i, p)),
         pl.BlockSpec((t, 128), lambda p, i: (0, SB_PAIRS + p)),
         pl.BlockSpec((t, 128), lambda p, i: (0, 2 * SB_PAIRS + p)),
         pl.BlockSpec((None, nb, 128, tq), lambda p, i: (p, 0, 0, 0)),
         pl.BlockSpec((qb * tq, 128), lambda p, i: (i, p)),
         pl.BlockSpec((128, qb * tq), lambda p, i: (p, i)),
         pl.BlockSpec((2, nb, 1, qb * tq), lambda p, i: (p, 0, 0, i)),
         pl.BlockSpec(memory_space=pltpu.SMEM),
         pl.BlockSpec((tq, tq), lambda p, i: (0, 0)),
         pl.BlockSpec((tq, tq), lambda p, i: (0, 0))],
        [pl.BlockSpec((128, qb * tq), lambda p, i: (p, i)),
         pl.BlockSpec((t, 128), lambda p, i: (0, p)),
         pl.BlockSpec((t, 128), lambda p, i: (0, p))],
        [jax.ShapeDtypeStruct((SB_WIDTH, t), F32), jax.ShapeDtypeStruct((t, SB_WIDTH), F32),
         jax.ShapeDtypeStruct((t, SB_WIDTH), F32)],
        [], ("arbitrary", "arbitrary"), (qkv, qkv, qkv, ktb, dyc, dyct, cc, nvis, upper, lower), ride)


def _sg_bwd(ab, dyb, gn, wm, wmt, bfull, tm, name):
    t = ab.shape[0]
    nt = t // tm
    sel = (jnp.arange(SG_WIDTH)[:, None] // SG_HD == jnp.arange(CHUNK)[None, :]).astype(F32)

    def body(u_ref, v_ref, dy_ref, gn_ref, wm_ref, wmt_ref, b_ref, sel_ref,
             dup_ref, dvp_ref, dgn_ref, dw_ref, db_ref, dbacc_ref):
        i = pl.program_id(0)

        @pl.when(i == 0)
        def _():
            dgn_ref[...] = jnp.zeros_like(dgn_ref)
            dw_ref[...] = jnp.zeros_like(dw_ref)
            dbacc_ref[...] = jnp.zeros_like(dbacc_ref)

        tril = (lax.broadcasted_iota(jnp.int32, (CHUNK, CHUNK), 0) >= lax.broadcasted_iota(jnp.int32, (CHUNK, CHUNK), 1))
        gn_ = gn_ref[...]
        for c in range(tm // CHUNK):
            rows = slice(c * CHUNK, (c + 1) * CHUNK)
            up, vp, dy = u_ref[rows, :], v_ref[rows, :], dy_ref[rows, :]
            u, v = _gelu(up), _gelu(vp)
            r = _rstd(v)
            vn = (v * r * gn_).astype(BF16)
            sv = b_ref[...]
            for h in range(SG_HEADS):
                sv = sv + jnp.where(_head_lanes(h), _dot(wm_ref[h], vn), 0.0)
            dup_ref[rows, :] = dy * sv * _gelu_grad(up)
            dsv = dy * u
            dbacc_ref[...] += dsv
            dvn = jnp.zeros((CHUNK, SG_WIDTH), F32)
            for h in range(SG_HEADS):
                dsv_h = jnp.where(_head_lanes(h), dsv, 0.0).astype(BF16)
                dvn = dvn + _dot(wmt_ref[h], dsv_h)
                dw_ref[h] += jnp.where(tril, _dot_nt(dsv_h, vn), 0.0)
            dv, dgn = _rms_bwd(v, r, gn_, dvn)
            dgn_ref[...] += dgn
            dvp_ref[rows, :] = dv * _gelu_grad(vp)

        @pl.when(i == nt - 1)
        def _():
            db_ref[...] = jnp.dot(dbacc_ref[...], sel_ref[...], preferred_element_type=F32,
                                  precision=lax.Precision.HIGHEST)

    const = lambda shape: pl.BlockSpec(shape, lambda i: tuple(0 for _ in shape))
    return pl.pallas_call(
        body, name=name, grid=(nt,),
        in_specs=[pl.BlockSpec((tm, SG_WIDTH), lambda i: (i, 1)), pl.BlockSpec((tm, SG_WIDTH), lambda i: (i, 2)),
                  pl.BlockSpec((tm, SG_WIDTH), lambda i: (i, 0)), const((1, SG_WIDTH)),
                  const((SG_HEADS, CHUNK, CHUNK)), const((SG_HEADS, CHUNK, CHUNK)), const((CHUNK, SG_WIDTH)),
                  const((SG_WIDTH, CHUNK))],
        out_specs=[pl.BlockSpec((tm, SG_WIDTH), lambda i: (i, 0)), pl.BlockSpec((tm, SG_WIDTH), lambda i: (i, 0)),
                   const((1, SG_WIDTH)), const((SG_HEADS, CHUNK, CHUNK)), const((CHUNK, CHUNK))],
        out_shape=[jax.ShapeDtypeStruct((t, SG_WIDTH), F32), jax.ShapeDtypeStruct((t, SG_WIDTH), F32),
                   jax.ShapeDtypeStruct((1, SG_WIDTH), F32), jax.ShapeDtypeStruct((SG_HEADS, CHUNK, CHUNK), F32),
                   jax.ShapeDtypeStruct((CHUNK, CHUNK), F32)],
        scratch_shapes=[pltpu.VMEM((CHUNK, SG_WIDTH), F32)],
        compiler_params=_params(("arbitrary",)),
    )(ab, ab, dyb, gn, wm, wmt, bfull, sel)


def _pool_bwd(ab, dya, wbd, scale, tm, name):
    t = ab.shape[0]
    nt = t // tm
    hb = tm // POOL_HALO
    nh = t // POOL_HALO

    def body(cur_ref, prev_ref, dy_ref, dyn_ref, w_ref, s_ref, da_ref, dw_ref, ds_ref):
        i = pl.program_id(0)

        @pl.when(i == 0)
        def _():
            dw_ref[...] = jnp.zeros_like(dw_ref)
            ds_ref[...] = jnp.zeros_like(ds_ref)

        prev = jnp.where(i == 0, 0.0, prev_ref[...])
        d = _pool_diff(cur_ref[...], prev, i * tm).astype(BF16)
        dy = dy_ref[...]
        ds_ref[...] += jnp.sum(dy * _dot(d, w_ref[...]), axis=0, keepdims=True)
        dyn = jnp.where(i == nt - 1, 0.0, dyn_ref[...])
        dys = (jnp.concatenate([dy, dyn], axis=0) * s_ref[...]).astype(BF16)
        dw_ref[...] += _dot_tn(d, dys[:tm])
        dd = _dot_nt(dys, w_ref[...])
        fwd = _pool_window_sums(dd / _pool_count(i * tm, tm + POOL_HALO), True)
        da_ref[...] = fwd[:tm] - dd[:tm]

    return pl.pallas_call(
        body, name=name, grid=(nt,),
        in_specs=[pl.BlockSpec((tm, POOL_WIDTH), lambda i: (i, 0)),
                  pl.BlockSpec((POOL_HALO, POOL_WIDTH), lambda i: (jnp.maximum(i * hb - 1, 0), 0)),
                  pl.BlockSpec((tm, POOL_WIDTH), lambda i: (i, 0)),
                  pl.BlockSpec((POOL_HALO, POOL_WIDTH), lambda i: (jnp.minimum((i + 1) * hb, nh - 1), 0)),
                  pl.BlockSpec((POOL_WIDTH, POOL_WIDTH), lambda i: (0, 0)),
                  pl.BlockSpec((1, POOL_WIDTH), lambda i: (0, 0))],
        out_specs=[pl.BlockSpec((tm, POOL_WIDTH), lambda i: (i, 0)),
                   pl.BlockSpec((POOL_WIDTH, POOL_WIDTH), lambda i: (0, 0)),
                   pl.BlockSpec((1, POOL_WIDTH), lambda i: (0, 0))],
        out_shape=[jax.ShapeDtypeStruct((t, POOL_WIDTH), F32), jax.ShapeDtypeStruct((POOL_WIDTH, POOL_WIDTH), F32),
                   jax.ShapeDtypeStruct((1, POOL_WIDTH), F32)],
        compiler_params=_params(("arbitrary",)),
    )(ab, ab, dya, dya, wbd, scale)


def _inproj_bwd(dx1, x, g, da, dup, dvp, dqt, dk, dv, w, tm, name, ride=None):
    t, d = x.shape
    n = w.shape[1]
    nt = t // tm

    def body(dx1_ref, x_ref, g_ref, da_ref, du_ref, dv_ref, dqt_ref, dk_ref, dvv_ref, w_ref,
             dx_ref, h_ref, dp_ref, dg_ref):
        @pl.when(pl.program_id(0) == 0)
        def _():
            dg_ref[...] = jnp.zeros_like(dg_ref)

        dp = jnp.concatenate([da_ref[...], du_ref[...], dv_ref[...], dqt_ref[...].T, dk_ref[...], dvv_ref[...]],
                             axis=1).astype(BF16)
        dp_ref[...] = dp
        xx = x_ref[...]
        r = _rstd(xx)
        h_ref[...] = (xx * r * g_ref[...]).astype(BF16)
        dx, dg = _rms_bwd(xx, r, g_ref[...], _dot_nt(dp, w_ref[...]))
        dx_ref[...] = dx1_ref[...] + dx
        dg_ref[...] += dg

    row = lambda w_: pl.BlockSpec((tm, w_), lambda i: (i, 0))
    return _call(
        body, name, (nt,),
        [row(d), row(d), pl.BlockSpec((1, d), lambda i: (0, 0)), row(POOL_WIDTH), row(SG_WIDTH),
         row(SG_WIDTH), pl.BlockSpec((SB_WIDTH, tm), lambda i: (0, i)), row(SB_WIDTH), row(SB_WIDTH),
         pl.BlockSpec((d, n), lambda i: (0, 0))],
        [row(d), row(d), row(n), pl.BlockSpec((1, d), lambda i: (0, 0))],
        [jax.ShapeDtypeStruct((t, d), F32), jax.ShapeDtypeStruct((t, d), BF16),
         jax.ShapeDtypeStruct((t, n), BF16), jax.ShapeDtypeStruct((1, d), F32)],
        [], ("arbitrary",), (dx1, x, g, da, dup, dvp, dqt, dk, dv, w), ride)


def _full_w_in(gathered):
    return gathered.transpose(1, 0, 2).reshape(D_MODEL, IN_COLS)


_SMALL_SHAPES = ((D_MODEL,), (4, POOL_GW, POOL_GW), (POOL_WIDTH,), (SG_WIDTH,), (SG_HEADS, CHUNK, CHUNK),
                 (SG_HEADS, CHUNK), (D_MODEL,))
_SMALL_SIZES = tuple(functools.reduce(lambda p, q: p * q, shp) for shp in _SMALL_SHAPES)
_SMALL_ROWS = sum(_SMALL_SIZES) // 128
_NORM1_ROWS = D_MODEL // 128


def _pack_small_layer(arrs):
    return jnp.concatenate([a.reshape(-1) for a in arrs]).reshape(_SMALL_ROWS, 128)


def _pack_small(arrs, final):
    return jnp.concatenate([_pack_small_layer([a[l] for a in arrs]) for l in range(DEPTH)] + [final.reshape(-1, 128)])


def _unpack_small(buf):
    per_layer = []
    for l in range(DEPTH):
        flat, off, outs = buf[l * _SMALL_ROWS:(l + 1) * _SMALL_ROWS].reshape(-1), 0, []
        for shp, size in zip(_SMALL_SHAPES, _SMALL_SIZES):
            outs.append(flat[off:off + size].reshape(shp))
            off += size
        per_layer.append(outs)
    return [jnp.stack([per_layer[l][k] for l in range(DEPTH)]) for k in range(len(_SMALL_SHAPES))] + \
           [buf[DEPTH * _SMALL_ROWS:].reshape(-1)]


def _tiles(t):
    return min(512, t), min(256, t // 4), min(2048, t)


def _layer_fwd(xl, wi, wo, wud, small_w, l, ride_mlp=None):
    n1, pw, ps, sn, sw, sb, n2 = small_w
    tm, tq, _ = _tiles(xl.shape[0])
    nb = xl.shape[0] // tq
    wbd = jnp.zeros((4, POOL_GW, 4, POOL_GW), F32)
    for gi in range(4):
        wbd = wbd.at[gi, :, gi, :].set(pw[gi])
    wbd = wbd.reshape(POOL_WIDTH, POOL_WIDTH).astype(BF16)
    wm = (sw * jnp.tril(jnp.ones((CHUNK, CHUNK), F32))).astype(BF16)
    wmt = wm.transpose(0, 2, 1)
    bfull = jnp.repeat(sb.T, SG_HD, axis=1)
    g1, g2, psc, gn = n1[None, :], n2[None, :], ps[None, :], sn[None, :]

    if isinstance(wo, Exchange):
        (ab, qkv, ktb, vtb), (wo,) = _inproj_fwd(xl, g1, wi, tm, tq, f"inproj_fwd{l}", wo)
    else:
        ab, qkv, ktb, vtb = _inproj_fwd(xl, g1, wi, tm, tq, f"inproj_fwd{l}")
    ya = _pool_fwd(ab, wbd, psc, tm, f"pool_fwd{l}")
    yb = _sg_fwd(ab, gn, wm, bfull, tm, f"sg_fwd{l}")
    if isinstance(wud, Exchange):
        (yct, cc, nvis), (wu, wd) = _sba_fwd(qkv, vtb, tq, f"sba_fwd{l}", wud)
    else:
        (yct, cc, nvis), (wu, wd) = _sba_fwd(qkv, vtb, tq, f"sba_fwd{l}"), wud
    wo = wo.reshape(D_MODEL, D_MODEL)
    res = _outproj_mlp_fwd(xl, ya, yb, yct, wo, g2, wu, wd, min(MLP_ROWS, xl.shape[0]), f"mlp_fwd{l}", ride_mlp)
    (x1, u, x2), rode = res if ride_mlp is not None else (res, None)
    saved = dict(x=xl, ab=ab, qkv=qkv, ktb=ktb, cc=cc, nvis=nvis, ya=ya, yb=yb, yct=yct, x1=x1, u=u, wi=wi, wo=wo, wu=wu,
                 wd=wd, wbd=wbd, wm=wm, wmt=wmt, bfull=bfull, g1=g1, g2=g2, psc=psc, gn=gn)
    return x2, saved, rode


def _layer_bwd(dx, s, l, ride_mlp=None, scatter_in_attn=True, gather_small=False):
    tm, tq, tw = _tiles(dx.shape[0])
    ktb = s["ktb"]
    res = _mlp_bwd(dx, s["x1"], s["g2"], s["u"], s["wu"], s["wd"], min(MLP_ROWS, dx.shape[0]), f"mlp_bwd{l}", ride_mlp)
    (dx1, du, r, h2, dx2b, dn2), rode = res if ride_mlp is not None else (res, None)
    dw_up = _matmul_tn(h2, du, D_MODEL, 2 * FF_SHARD, tw, f"dw_up{l}", by_column_block=FF_SHARD)
    dw_down = _matmul_tn(r, dx2b, 1024, D_MODEL, tw, f"dw_down{l}").reshape(N_DEV, FF_SHARD, D_MODEL)
    dya, dyb, dyc, dyct, dx1b = _outproj_bwd(dx1, s["wo"], tm, f"outproj_bwd{l}")
    dw_out = jnp.concatenate([
        _matmul_tn(jnp.concatenate([s["ya"], s["yb"]], axis=1), dx1b, POOL_WIDTH + SG_WIDTH, D_MODEL, tw, f"dw_out_ab{l}"),
        _matmul_tn(s["yct"], dx1b, SB_WIDTH, D_MODEL, tw, f"dw_out_c{l}", a_transposed=True)]
    ).reshape(N_DEV, OUT_SHARD, D_MODEL)
    if scatter_in_attn:
        (dqt, dk, dv), (dw_out, dw_up, dw_down) = _sba_bwd(s["qkv"], ktb, dyc, dyct, s["cc"], s["nvis"], tq, f"sba_bwd{l}",
                                                           Exchange([dw_out, dw_up, dw_down], True))
    else:
        dqt, dk, dv = _sba_bwd(s["qkv"], ktb, dyc, dyct, s["cc"], s["nvis"], tq, f"sba_bwd{l}")
    dup, dvp, dgn, dwm, dbm = _sg_bwd(s["ab"], dyb, s["gn"], s["wm"], s["wmt"], s["bfull"], tm, f"sg_bwd{l}")
    da, dwbd, dpsc = _pool_bwd(s["ab"], dya, s["wbd"], s["psc"], tm, f"pool_bwd{l}")
    dpw = jnp.stack([dwbd[gi * POOL_GW:(gi + 1) * POOL_GW, gi * POOL_GW:(gi + 1) * POOL_GW] for gi in range(4)])
    small = _pack_small_layer([jnp.zeros((D_MODEL,), F32), dpw, dpsc[0], dgn[0], dwm, dbm[:, :SG_HEADS].T, dn2[0]])[_NORM1_ROWS:]
    res = _inproj_bwd(dx1, s["x"], s["g1"], da, dup, dvp, dqt, dk, dv, s["wi"], tm, f"inproj_bwd{l}",
                      Exchange([small], False) if gather_small else None)
    (dx, h1, dproj, dn1), small = (res[0], res[1][0]) if gather_small else (res, small)
    dw_in = _matmul_tn(h1, dproj, D_MODEL, IN_COLS // 3, tw, f"dw_in{l}")
    dw_in = dw_in.reshape(D_MODEL, N_DEV, IN_SHARD).transpose(1, 0, 2)
    return dx, (dw_in, dw_out, dw_up, dw_down), (dn1.reshape(_NORM1_ROWS, 128), small), rode


def kernel(x, norm1, w_in, pool_w, pool_scale, sg_norm, sg_w, sg_b, w_out, norm2, w_up, w_down, final_norm, loss_target, m_norm1, m_w_in, m_pool_w, m_pool_scale, m_sg_norm, m_sg_w, m_sg_b, m_w_out, m_norm2, m_w_up, m_w_down, m_final_norm, v_norm1, v_w_in, v_pool_w, v_pool_scale, v_sg_norm, v_sg_w, v_sg_b, v_w_out, v_norm2, v_w_up, v_w_down, v_final_norm):
    t = x.shape[1]
    tm = _tiles(t)[0]
    small_w = (norm1, pool_w, pool_scale, sg_norm, sg_w, sg_b, norm2)
    big_w = (w_in, w_out, w_up, w_down)
    big_m = (m_w_in, m_w_out, m_w_up, m_w_down)
    big_v = (v_w_in, v_w_out, v_w_up, v_w_down)
    shards = [[w[l].astype(BF16) for w in big_w] for l in range(DEPTH)]

    wi0 = _full_w_in(_exchange(shards[0][:1], False, "gather_w_in0")[0])
    x1, s0, g1 = _layer_fwd(x.reshape(t, D_MODEL), wi0, Exchange(shards[0][1:2], False), Exchange(shards[0][2:], False),
                            tuple(w[0] for w in small_w), 0, ride_mlp=Exchange(shards[1][:2], False))
    x2, s1, _ = _layer_fwd(x1, _full_w_in(g1[0]), g1[1], Exchange(shards[1][2:], False), tuple(w[1] for w in small_w), 1)
    loss_local, dx, dfinal = _loss_grad(x2, final_norm[None, :], loss_target.reshape(t, D_MODEL), tm, "loss_grad")
    loss = lax.psum(loss_local[0, 0], MESH_AXES)

    dx, parts1, small1, _ = _layer_bwd(dx, s1, 1)
    early = jnp.concatenate(list(small1) + [dfinal.reshape(-1, 128)])
    dx, parts0, (dn1, small0), (recv_in1, early) = _layer_bwd(
        dx, s0, 0, ride_mlp=Exchange([parts1[0], early], [True, False]), gather_small=True)
    grad_x = dx.reshape(x.shape)
    recv_in0, dn1 = _exchange([parts0[0], dn1], [True, False], "scatter_w_in0_gather_norm1_0")
    small_all = jnp.concatenate([dn1, small0, early], axis=1)
    received = [[recv_in0] + list(parts0[1:]), [recv_in1] + list(parts1[1:])]

    big = [None] * 4
    for l in reversed(range(DEPTH)):
        for k in range(4):
            big[k] = _reduce_adamw(received[l][k], big_w[k], big_m[k], big_v[k], l, big[k], f"adamw{k}_{l}")

    sm = _reduce_adamw(
        small_all,
        _pack_small(small_w, final_norm)[None],
        _pack_small([m_norm1, m_pool_w, m_pool_scale, m_sg_norm, m_sg_w, m_sg_b, m_norm2], m_final_norm)[None],
        _pack_small([v_norm1, v_pool_w, v_pool_scale, v_sg_norm, v_sg_w, v_sg_b, v_norm2], v_final_norm)[None],
        0, None, "adamw_replicated")

    out = [loss, grad_x]
    for k in range(4):
        n1, pw, ps, sn, sw, sb, n2, fn = _unpack_small(sm[k][0])
        out += [n1, big[0][k], pw, ps, sn, sw, sb, big[1][k], n2, big[2][k], big[3][k], fn]
    return tuple(out)
```

```python
import functools

import jax
import jax.numpy as jnp
from jax import lax
from jax.experimental import pallas as pl
from jax.experimental.pallas import tpu as pltpu

F32 = jnp.float32
BF16 = jnp.bfloat16

D_MODEL = 1024
DEPTH = 2
POOL_WIDTH = 256
SG_WIDTH = 256
SB_WIDTH = 512
POOL_WINDOWS = (2, 4, 8, 16)
POOL_GW = 64
POOL_HALO = 16
CHUNK = 128
SG_HEADS = 4
SG_HD = 64
SB_HD = 64
SB_PAIRS = SB_WIDTH // 128
AB_COLS = POOL_WIDTH + 2 * SG_WIDTH
IN_COLS = AB_COLS + 3 * SB_WIDTH
D_FF = 4096
EPS = 1e-6
N_DEV = 8
FF_SHARD = D_FF // N_DEV
IN_SHARD = IN_COLS // N_DEV
OUT_SHARD = D_MODEL // N_DEV
ADAM_LR = 0.001
ADAM_B1 = 0.9
ADAM_B2 = 0.999
ADAM_EPS = 1e-08
ADAM_WD = 0.01
ADAM_STEP = 10
VMEM_LIMIT = 56 * 1024 * 1024
MLP_ROWS = 1024
MESH_AXES = ("x", "y", "c")


def _dot(a, b):
    return jnp.dot(a, b, preferred_element_type=F32)


def _dot_nt(a, b):
    return lax.dot_general(a, b, (((1,), (1,)), ((), ())), preferred_element_type=F32)


def _dot_tn(a, b):
    return lax.dot_general(a, b, (((0,), (0,)), ((), ())), preferred_element_type=F32)


def _rstd(x):
    return lax.rsqrt(jnp.mean(x * x, axis=-1, keepdims=True) + EPS)


def _rms_bwd(x, r, g, dh):
    gq = dh * g
    dx = r * gq - x * (r * r * r) * jnp.mean(gq * x, axis=-1, keepdims=True)
    dg = jnp.sum(dh * x * r, axis=0, keepdims=True)
    return dx, dg


def _params(sem=None):
    kw = dict(vmem_limit_bytes=VMEM_LIMIT)
    if sem is not None:
        kw["dimension_semantics"] = sem
    return pltpu.CompilerParams(**kw)


def _row_tile(rows, cap):
    best = 8
    for t in range(8, min(rows, cap) + 1, 8):
        if rows % t == 0:
            best = t
    return best


def _peer(k):
    x, y, c = lax.axis_index("x"), lax.axis_index("y"), lax.axis_index("c")
    return (1 - x if k & 4 else x, 1 - y if k & 2 else y, 1 - c if k & 1 else c)


def _my_index():
    return 4 * lax.axis_index("x") + 2 * lax.axis_index("y") + lax.axis_index("c")


class Exchange:
    def __init__(self, arrs, scatter):
        self.arrs = list(arrs)
        self.scatter = list(scatter) if isinstance(scatter, (list, tuple)) else [scatter] * len(self.arrs)
        self.n = len(self.arrs)
        self.any_specs = [pl.BlockSpec(memory_space=pl.ANY)] * self.n
        self.out_shape = [jax.ShapeDtypeStruct((N_DEV,) + a.shape[-2:], a.dtype) for a in self.arrs]
        self.sems = [pltpu.SemaphoreType.DMA((self.n, N_DEV - 1)), pltpu.SemaphoreType.DMA((self.n, N_DEV - 1)),
                     pltpu.SemaphoreType.DMA((self.n,))]

    def _copies(self, ins, outs, sems):
        send_sems, recv_sems, local_sems = sems
        me = _my_index()
        local, remote = [], []
        for a in range(self.n):
            sc = self.scatter[a]
            local.append(pltpu.make_async_copy(ins[a].at[me] if sc else ins[a], outs[a].at[me], local_sems.at[a]))
            for k in range(1, N_DEV):
                px, py, pc = _peer(k)
                remote.append(pltpu.make_async_remote_copy(
                    src_ref=ins[a].at[4 * px + 2 * py + pc] if sc else ins[a], dst_ref=outs[a].at[me],
                    send_sem=send_sems.at[a, k - 1], recv_sem=recv_sems.at[a, k - 1],
                    device_id=(px, py, pc), device_id_type=pl.DeviceIdType.MESH))
        return local, remote

    def start(self, ins, outs, sems):
        local, remote = self._copies(ins, outs, sems)
        for cp in local + remote:
            cp.start()

    def wait(self, ins, outs, sems):
        local, remote = self._copies(ins, outs, sems)
        for cp in remote:
            cp.wait_recv()
        for cp in remote:
            cp.wait_send()
        for cp in local:
            cp.wait()

    def alone(self, name):
        n = self.n

        def body(*refs):
            ins, outs, sems = refs[:n], refs[n:2 * n], refs[2 * n:]
            self.start(ins, outs, sems)
            self.wait(ins, outs, sems)

        return pl.pallas_call(body, name=name, out_shape=self.out_shape, in_specs=self.any_specs,
                              out_specs=self.any_specs, scratch_shapes=self.sems)(*self.arrs)


def _exchange(arrs, scatter, name):
    return Exchange(arrs, scatter).alone(name)


def _call(body, name, grid, in_specs, out_specs, out_shape, scratch_shapes, semantics, args, ride=None):
    if ride is None:
        return pl.pallas_call(body, name=name, grid=grid, in_specs=in_specs, out_specs=out_specs, out_shape=out_shape,
                              scratch_shapes=scratch_shapes, compiler_params=_params(semantics))(*args)
    single = not isinstance(out_shape, (list, tuple))
    out_specs, out_shape = ([out_specs], [out_shape]) if single else (list(out_specs), list(out_shape))
    n_in, n_out, n_scr, n = len(in_specs), len(out_specs), len(scratch_shapes), ride.n

    def riding(*refs):
        ins, cins = refs[:n_in], refs[n_in:n_in + n]
        outs, couts = refs[n_in + n:n_in + n + n_out], refs[n_in + n + n_out:n_in + 2 * n + n_out]
        scr, sems = refs[n_in + 2 * n + n_out:n_in + 2 * n + n_out + n_scr], refs[n_in + 2 * n + n_out + n_scr:]
        first = functools.reduce(lambda p, q: p & q, [pl.program_id(a) == 0 for a in range(len(grid))])
        last = functools.reduce(lambda p, q: p & q, [pl.program_id(a) == g - 1 for a, g in enumerate(grid)])

        @pl.when(first)
        def _():
            ride.start(cins, couts, sems)

        body(*ins, *outs, *scr)

        @pl.when(last)
        def _():
            ride.wait(cins, couts, sems)

    res = pl.pallas_call(
        riding, name=name, grid=grid, in_specs=list(in_specs) + ride.any_specs, out_specs=out_specs + ride.any_specs,
        out_shape=out_shape + ride.out_shape, scratch_shapes=list(scratch_shapes) + ride.sems,
        compiler_params=_params(("arbitrary",) * len(grid)))(*args, *ride.arrs)
    own = res[0] if single else list(res[:n_out])
    return own, list(res[n_out:])


def _reduce_adamw(parts, w, m, v, l, prev, name):
    _, rows, n = parts.shape
    tr = _row_tile(rows, max(8, (1 << 18) // n))
    c1 = 1.0 - ADAM_B1 ** ADAM_STEP
    c2 = 1.0 - ADAM_B2 ** ADAM_STEP

    def body(p_ref, w_ref, m_ref, v_ref, *rest):
        g_ref, d_ref, nm_ref, nv_ref = rest[-4:]
        g = p_ref[0].astype(F32)
        for s in range(1, N_DEV):
            g = g + p_ref[s].astype(F32)
        nm = ADAM_B1 * m_ref[...] + (1.0 - ADAM_B1) * g
        nv = ADAM_B2 * v_ref[...] + (1.0 - ADAM_B2) * (g * g)
        m_hat = nm / c1
        v_hat = nv / c2
        g_ref[...] = g
        d_ref[...] = -ADAM_LR * (m_hat / (jnp.sqrt(v_hat) + ADAM_EPS) + ADAM_WD * w_ref[...])
        nm_ref[...] = nm
        nv_ref[...] = nv

    blk = pl.BlockSpec((None, tr, n), lambda i: (l, i, 0))
    out = jax.ShapeDtypeStruct(w.shape, F32)
    prev = list(prev) if prev is not None else []
    return pl.pallas_call(
        body, name=name, grid=(rows // tr,),
        in_specs=[pl.BlockSpec((N_DEV, tr, n), lambda i: (0, i, 0)), blk, blk, blk] + [pl.BlockSpec(memory_space=pl.ANY)] * len(prev),
        out_specs=[blk, blk, blk, blk], out_shape=[out, out, out, out],
        input_output_aliases={4 + k: k for k in range(len(prev))},
        compiler_params=_params(("parallel",)),
    )(parts, w, m, v, *prev)


def _inproj_fwd(x, g, w, tm, tq, name, ride=None):
    t, d = x.shape
    n = w.shape[1]
    nb = t // tq
    per = tm // tq

    def body(x_ref, g_ref, w_ref, ab_ref, qkv_ref, kt_ref, vt_ref):
        xx = x_ref[...]
        h = (xx * _rstd(xx) * g_ref[...]).astype(BF16)
        ab_ref[...] = _dot(h, w_ref[:, :AB_COLS])
        qkv = _dot(h, w_ref[:, AB_COLS:])
        qkv_ref[...] = qkv.astype(BF16)
        for which, out_ref in ((1, kt_ref), (2, vt_ref)):
            for p in range(SB_PAIRS):
                for b in range(per):
                    cols = which * SB_WIDTH + p * 128
                    out_ref[p, b] = qkv[b * tq:(b + 1) * tq, cols:cols + 128].T.astype(BF16)

    tb = pl.BlockSpec((SB_PAIRS, per, 128, tq), lambda i: (0, i, 0, 0))
    tshape = jax.ShapeDtypeStruct((SB_PAIRS, nb, 128, tq), BF16)
    return _call(
        body, name, (t // tm,),
        [pl.BlockSpec((tm, d), lambda i: (i, 0)), pl.BlockSpec((1, d), lambda i: (0, 0)),
         pl.BlockSpec((d, n), lambda i: (0, 0))],
        [pl.BlockSpec((tm, AB_COLS), lambda i: (i, 0)), pl.BlockSpec((tm, n - AB_COLS), lambda i: (i, 0)), tb, tb],
        [jax.ShapeDtypeStruct((t, AB_COLS), F32), jax.ShapeDtypeStruct((t, n - AB_COLS), BF16), tshape, tshape],
        [], ("parallel",), (x, g, w), ride)


def _pool_window_sums(xx, forward):
    n = xx.shape[0]
    sh = (lambda k: n - k) if forward else (lambda k: k)
    s2 = xx + pltpu.roll(xx, sh(1), 0)
    s4 = s2 + pltpu.roll(s2, sh(2), 0)
    s8 = s4 + pltpu.roll(s4, sh(4), 0)
    s16 = s8 + pltpu.roll(s8, sh(8), 0)
    grp = lax.broadcasted_iota(jnp.int32, (1, POOL_WIDTH), 1) // POOL_GW
    return jnp.where(grp == 0, s2, jnp.where(grp == 1, s4, jnp.where(grp == 2, s8, s16)))


def _pool_count(t0, rows):
    grp = lax.broadcasted_iota(jnp.int32, (1, POOL_WIDTH), 1) // POOL_GW
    win = jnp.where(grp == 0, 2, jnp.where(grp == 1, 4, jnp.where(grp == 2, 8, 16)))
    tt = t0 + lax.broadcasted_iota(jnp.int32, (rows, 1), 0)
    return jnp.minimum(tt + 1, win).astype(F32)


def _pool_diff(cur, prev, t0):
    tm = cur.shape[0]
    sums = _pool_window_sums(jnp.concatenate([prev, cur], axis=0), False)[POOL_HALO:]
    return sums / _pool_count(t0, tm) - cur


def _pool_fwd(ab, wbd, scale, tm, name):
    t = ab.shape[0]
    hb = tm // POOL_HALO

    def body(cur_ref, prev_ref, w_ref, s_ref, y_ref):
        i = pl.program_id(0)
        prev = jnp.where(i == 0, 0.0, prev_ref[...])
        d = _pool_diff(cur_ref[...], prev, i * tm).astype(BF16)
        y_ref[...] = (_dot(d, w_ref[...]) * s_ref[...]).astype(BF16)

    return pl.pallas_call(
        body, name=name, grid=(t // tm,),
        in_specs=[pl.BlockSpec((tm, POOL_WIDTH), lambda i: (i, 0)),
                  pl.BlockSpec((POOL_HALO, POOL_WIDTH), lambda i: (jnp.maximum(i * hb - 1, 0), 0)),
                  pl.BlockSpec((POOL_WIDTH, POOL_WIDTH), lambda i: (0, 0)),
                  pl.BlockSpec((1, POOL_WIDTH), lambda i: (0, 0))],
        out_specs=pl.BlockSpec((tm, POOL_WIDTH), lambda i: (i, 0)),
        out_shape=jax.ShapeDtypeStruct((t, POOL_WIDTH), BF16),
        compiler_params=_params(("parallel",)),
    )(ab, ab, wbd, scale)


_GELU_K = 0.7978845608028654
_GELU_A = 0.044715


def _gelu(x):
    return 0.5 * x * (1.0 + jnp.tanh(_GELU_K * (x + _GELU_A * (x * x * x))))


def _gelu_grad(x):
    th = jnp.tanh(_GELU_K * (x + _GELU_A * (x * x * x)))
    return 0.5 * (1.0 + th) + 0.5 * x * (1.0 - th * th) * (_GELU_K * (1.0 + 3.0 * _GELU_A * (x * x)))


def _head_lanes(h):
    return lax.broadcasted_iota(jnp.int32, (1, SG_WIDTH), 1) // SG_HD == h


def _sg_fwd(ab, gn, wm, bfull, tm, name):
    t = ab.shape[0]

    def body(u_ref, v_ref, gn_ref, wm_ref, b_ref, y_ref):
        v = _gelu(v_ref[...])
        vn = (v * _rstd(v) * gn_ref[...]).astype(BF16)
        for c in range(tm // CHUNK):
            rows = slice(c * CHUNK, (c + 1) * CHUNK)
            vc = vn[rows]
            sv = b_ref[...]
            for h in range(SG_HEADS):
                sv = sv + jnp.where(_head_lanes(h), _dot(wm_ref[h], vc), 0.0)
            y_ref[rows, :] = (_gelu(u_ref[rows, :]) * sv).astype(BF16)

    return pl.pallas_call(
        body, name=name, grid=(t // tm,),
        in_specs=[pl.BlockSpec((tm, SG_WIDTH), lambda i: (i, 1)), pl.BlockSpec((tm, SG_WIDTH), lambda i: (i, 2)),
                  pl.BlockSpec((1, SG_WIDTH), lambda i: (0, 0)),
                  pl.BlockSpec((SG_HEADS, CHUNK, CHUNK), lambda i: (0, 0, 0)),
                  pl.BlockSpec((CHUNK, SG_WIDTH), lambda i: (0, 0))],
        out_specs=pl.BlockSpec((tm, SG_WIDTH), lambda i: (i, 0)),
        out_shape=jax.ShapeDtypeStruct((t, SG_WIDTH), BF16),
        compiler_params=_params(("parallel",)),
    )(ab, ab, gn, wm, bfull)


LOG2E = 1.4426950408889634
SB_SCALE = 0.125 * LOG2E
SB_DEAD_LOG2 = 152.0
SB_QUERY_BLOCKS_PER_STEP = 2


def _log2_sigmoids(y):
    neg_abs = lax.bitcast_convert_type(lax.bitcast_convert_type(y, jnp.uint32) | jnp.uint32(0x80000000), F32)
    lb = jnp.minimum(y, 0.0) - jnp.log(1.0 + jnp.exp2(neg_abs)) * LOG2E
    return lb, lb - y


def _split(x):
    hi = x.astype(BF16)
    return hi, (x - hi.astype(F32)).astype(BF16)


def _tri_dot(tri, x):
    hi, lo = _split(x)
    return _dot(tri, hi) + _dot(tri, lo)


def _sba_fwd(qkv, vtb, tq, name, ride=None):
    t = qkv.shape[0]
    nb = t // tq
    qb = SB_QUERY_BLOCKS_PER_STEP
    upper = (jnp.arange(tq)[None, :] > jnp.arange(tq)[:, None]).astype(BF16)

    def body(q_ref, k_ref, vt_ref, up_ref, ot_ref, c_ref, n_ref):
        for sub in range(qb):
            query_block(pl.program_id(1) * qb + sub, slice(sub * tq, (sub + 1) * tq), q_ref, k_ref, vt_ref, up_ref,
                        ot_ref, c_ref, n_ref)

    def query_block(i, mine, q_ref, k_ref, vt_ref, up_ref, ot_ref, c_ref, n_ref):
        q = q_ref[mine, :]
        up = up_ref[...]
        lane_head = lax.broadcasted_iota(jnp.int32, (1, 128), 1) // SB_HD
        sub_head = lax.broadcasted_iota(jnp.int32, (128, 1), 0) // SB_HD
        causal = (lax.broadcasted_iota(jnp.int32, (tq, tq), 0) < lax.broadcasted_iota(jnp.int32, (tq, tq), 1))
        qh = [jnp.where(lane_head == h, q, jnp.zeros_like(q)) for h in range(2)]

        def blocks(js, carry):
            cs, acc = list(carry[:2]), carry[2]
            kj = [k_ref[pl.ds(pl.multiple_of(j * tq, tq), tq), :] for j, _ in js]
            vt = [vt_ref[j] for j, _ in js]
            chains = [(b, h) for b in range(len(js)) for h in range(2)]
            z = [_dot_nt(kj[b], qh[h]) for b, h in chains]
            ls = [_log2_sigmoids(zz * SB_SCALE) for zz in z]
            lb = [x[0] for x in ls]
            l1 = [jnp.where(causal, x[1], 0.0) if js[b][1] else x[1] for x, (b, h) in zip(ls, chains)]
            after = [_tri_dot(up, x) for x in l1]
            a = []
            for n, (b, h) in enumerate(chains):
                c_ref[h, js[b][0], :, mine] = cs[h]
                an = jnp.exp2(lb[n] + after[n] + cs[h])
                a.append(jnp.where(causal, an, 0.0) if js[b][1] else an)
                cs[h] = cs[h] + after[n][0:1, :] + l1[n][0:1, :]
            alive = (jnp.max(jnp.maximum(cs[0], cs[1])) > -SB_DEAD_LOG2).astype(jnp.int32)
            for n, (b, h) in enumerate(chains):
                acc = acc + _dot(jnp.where(sub_head == h, vt[b], jnp.zeros_like(vt[b])), a[n].astype(BF16))
            return alive, cs[0], cs[1], acc

        def step(state):
            s, _, c0, c1, acc = state
            return (s + 1,) + blocks([(i - 1 - s, False)], (c0, c1, acc))

        zero = jnp.zeros((1, tq), F32)
        start = (zero, zero, jnp.zeros((128, tq), F32))
        joint = lax.cond(i > 0, lambda: blocks([(i, True), (i - 1, False)], start), lambda: blocks([(i, True)], start))
        state = lax.while_loop(lambda st: (st[0] < i) & (st[1] > 0), step, (jnp.minimum(i, 1),) + tuple(joint))
        ot_ref[:, mine] = state[4].astype(BF16)
        n_ref[pl.program_id(0), i] = (state[0] + 1).astype(F32)

    return _call(
        body, name, (SB_PAIRS, nb // qb),
        [pl.BlockSpec((qb * tq, 128), lambda p, i: (i, p)),
         pl.BlockSpec((t, 128), lambda p, i: (0, SB_PAIRS + p)),
         pl.BlockSpec((None, nb, 128, tq), lambda p, i: (p, 0, 0, 0)),
         pl.BlockSpec((tq, tq), lambda p, i: (0, 0))],
        [pl.BlockSpec((128, qb * tq), lambda p, i: (p, i)),
         pl.BlockSpec((2, nb, 1, qb * tq), lambda p, i: (p, 0, 0, i)),
         pl.BlockSpec(memory_space=pltpu.SMEM)],
        [jax.ShapeDtypeStruct((SB_WIDTH, t), BF16),
         jax.ShapeDtypeStruct((2 * SB_PAIRS, nb, 1, t), F32),
         jax.ShapeDtypeStruct((SB_PAIRS, nb), F32)],
        [], ("arbitrary", "arbitrary"), (qkv, qkv, vtb, upper), ride)


def _outproj_mlp_fwd(x, ya, yb, yct, wo, g2, wup, wdn, tm, name, ride=None):
    t, d = x.shape
    nf = wup.shape[0]

    def body(x_ref, ya_ref, yb_ref, yct_ref, wo_ref, g_ref, wu_ref, wd_ref, x1_ref, u_ref, x2_ref, h_ref, acc_ref):
        j = pl.program_id(1)

        @pl.when(j == 0)
        def _():
            x1 = (x_ref[...] + _dot(ya_ref[...], wo_ref[0:POOL_WIDTH, :])
                  + _dot(yb_ref[...], wo_ref[POOL_WIDTH:POOL_WIDTH + SG_WIDTH, :])
                  + _dot_tn(yct_ref[...], wo_ref[POOL_WIDTH + SG_WIDTH:, :]))
            x1_ref[...] = x1
            h_ref[...] = (x1 * _rstd(x1) * g_ref[...]).astype(BF16)
            acc_ref[...] = x1

        u = _dot(h_ref[...], wu_ref[...])
        u_ref[...] = u
        r = jnp.maximum(u, 0.0)
        acc_ref[...] += _dot((r * r).astype(BF16), wd_ref[...])

        @pl.when(j == nf - 1)
        def _():
            x2_ref[...] = acc_ref[...]

    row = lambda w: pl.BlockSpec((tm, w), lambda i, j: (i, 0))
    return _call(
        body, name, (t // tm, nf),
        [row(d), row(POOL_WIDTH), row(SG_WIDTH), pl.BlockSpec((SB_WIDTH, tm), lambda i, j: (0, i)),
         pl.BlockSpec((d, d), lambda i, j: (0, 0)), pl.BlockSpec((1, d), lambda i, j: (0, 0)),
         pl.BlockSpec((None, d, FF_SHARD), lambda i, j: (j, 0, 0)),
         pl.BlockSpec((None, FF_SHARD, d), lambda i, j: (j, 0, 0))],
        [row(d), pl.BlockSpec((tm, FF_SHARD), lambda i, j: (i, j)), row(d)],
        [jax.ShapeDtypeStruct((t, d), F32), jax.ShapeDtypeStruct((t, nf * FF_SHARD), F32),
         jax.ShapeDtypeStruct((t, d), F32)],
        [pltpu.VMEM((tm, d), BF16), pltpu.VMEM((tm, d), F32)],
        ("parallel", "arbitrary"), (x, ya, yb, yct, wo, g2, wup, wdn), ride)


def _loss_grad(x, g, target, tm, name):
    t, d = x.shape
    nt = t // tm

    def body(x_ref, g_ref, t_ref, loss_ref, dx_ref, dg_ref, sq_ref):
        i = pl.program_id(0)

        @pl.when(i == 0)
        def _():
            sq_ref[...] = jnp.zeros_like(sq_ref)
            dg_ref[...] = jnp.zeros_like(dg_ref)

        xx = x_ref[...]
        r = _rstd(xx)
        err = xx * r * g_ref[...] - t_ref[...]
        sq_ref[...] += jnp.sum(err * err, axis=0, keepdims=True)
        dx, dg = _rms_bwd(xx, r, g_ref[...], err * (1.0 / d))
        dx_ref[...] = dx
        dg_ref[...] += dg

        @pl.when(i == nt - 1)
        def _():
            loss_ref[...] = jnp.sum(sq_ref[...], axis=1, keepdims=True) * (0.5 / d)

    return pl.pallas_call(
        body, name=name, grid=(nt,),
        in_specs=[pl.BlockSpec((tm, d), lambda i: (i, 0)), pl.BlockSpec((1, d), lambda i: (0, 0)),
                  pl.BlockSpec((tm, d), lambda i: (i, 0))],
        out_specs=[pl.BlockSpec((1, 1), lambda i: (0, 0)), pl.BlockSpec((tm, d), lambda i: (i, 0)),
                   pl.BlockSpec((1, d), lambda i: (0, 0))],
        out_shape=[jax.ShapeDtypeStruct((1, 1), F32), jax.ShapeDtypeStruct((t, d), F32),
                   jax.ShapeDtypeStruct((1, d), F32)],
        scratch_shapes=[pltpu.VMEM((1, d), F32)],
        compiler_params=_params(("arbitrary",)),
    )(x, g, target)


def _mlp_bwd(dx2, x1, g2, u, wup, wdn, tm, name, ride=None):
    t, d = dx2.shape
    nf = wup.shape[0]
    nt = t // tm

    def body(dx2_ref, x1_ref, g_ref, u_ref, wu_ref, wd_ref, dx1_ref, du_ref, r_ref, h_ref, dxb_ref, dg_ref, acc_ref):
        i, j = pl.program_id(0), pl.program_id(1)

        @pl.when(j == 0)
        def _():
            x1 = x1_ref[...]
            h_ref[...] = (x1 * _rstd(x1) * g_ref[...]).astype(BF16)
            dxb_ref[...] = dx2_ref[...].astype(BF16)
            acc_ref[...] = jnp.zeros_like(acc_ref)

        @pl.when((i == 0) & (j == 0))
        def _():
            dg_ref[...] = jnp.zeros_like(dg_ref)

        dr = _dot_nt(dxb_ref[...], wd_ref[...])
        ru = jnp.maximum(u_ref[...], 0.0)
        du = (dr * (2.0 * ru)).astype(BF16)
        du_ref[...] = du
        r_ref[...] = (ru * ru).astype(BF16)
        acc_ref[...] += _dot_nt(du, wu_ref[...])

        @pl.when(j == nf - 1)
        def _():
            x1 = x1_ref[...]
            dx, dg = _rms_bwd(x1, _rstd(x1), g_ref[...], acc_ref[...])
            dx1_ref[...] = dx2_ref[...] + dx
            dg_ref[...] += dg

    row = lambda w: pl.BlockSpec((tm, w), lambda i, j: (i, 0))
    col = pl.BlockSpec((tm, FF_SHARD), lambda i, j: (i, j))
    return _call(
        body, name, (nt, nf),
        [row(d), row(d), pl.BlockSpec((1, d), lambda i, j: (0, 0)), col,
         pl.BlockSpec((None, d, FF_SHARD), lambda i, j: (j, 0, 0)),
         pl.BlockSpec((None, FF_SHARD, d), lambda i, j: (j, 0, 0))],
        [row(d), col, col, row(d), row(d), pl.BlockSpec((1, d), lambda i, j: (0, 0))],
        [jax.ShapeDtypeStruct((t, d), F32), jax.ShapeDtypeStruct((t, nf * FF_SHARD), BF16),
         jax.ShapeDtypeStruct((t, nf * FF_SHARD), BF16), jax.ShapeDtypeStruct((t, d), BF16),
         jax.ShapeDtypeStruct((t, d), BF16), jax.ShapeDtypeStruct((1, d), F32)],
        [pltpu.VMEM((tm, d), F32)], ("arbitrary", "arbitrary"), (dx2, x1, g2, u, wup, wdn), ride)


def _matmul_tn(a, b, bm, bn, bt, name, by_column_block=0, a_transposed=False, a_blocks=None, ride=None):
    m, t = a.shape if a_transposed else a.shape[::-1]
    first, count = a_blocks if a_blocks is not None else (0, m // bm)
    m = count * bm
    n = b.shape[1]
    nk = t // bt

    def body(a_ref, b_ref, o_ref, acc_ref):
        k = pl.program_id(2)

        @pl.when(k == 0)
        def _():
            acc_ref[...] = jnp.zeros_like(acc_ref)

        acc_ref[...] += _dot(a_ref[...], b_ref[...]) if a_transposed else _dot_tn(a_ref[...], b_ref[...])

        @pl.when(k == nk - 1)
        def _():
            if by_column_block:
                for c in range(bn // by_column_block):
                    o_ref[c] = acc_ref[:, c * by_column_block:(c + 1) * by_column_block].astype(BF16)
            else:
                o_ref[...] = acc_ref[...].astype(BF16)

    if by_column_block:
        out_spec = pl.BlockSpec((bn // by_column_block, bm, by_column_block), lambda i, j, k: (j, i, 0))
        out_shape = jax.ShapeDtypeStruct((n // by_column_block, m, by_column_block), BF16)
    else:
        out_spec = pl.BlockSpec((bm, bn), lambda i, j, k: (i, j))
        out_shape = jax.ShapeDtypeStruct((m, n), BF16)
    a_spec = (pl.BlockSpec((bm, bt), lambda i, j, k: (i + first, k)) if a_transposed
              else pl.BlockSpec((bt, bm), lambda i, j, k: (k, i + first)))
    return _call(body, name, (count, n // bn, nk), [a_spec, pl.BlockSpec((bt, bn), lambda i, j, k: (k, j))],
                 out_spec, out_shape, [pltpu.VMEM((bm, bn), F32)], ("parallel", "parallel", "arbitrary"), (a, b), ride)


def _outproj_bwd(dx1, wo, tm, name):
    t, d = dx1.shape
    c2 = POOL_WIDTH + SG_WIDTH

    def body(dx_ref, wo_ref, dya_ref, dyb_ref, dyc_ref, dyct_ref, dxb_ref):
        dxb = dx_ref[...].astype(BF16)
        dxb_ref[...] = dxb
        dya_ref[...] = _dot_nt(dxb, wo_ref[0:POOL_WIDTH, :])
        dyb_ref[...] = _dot_nt(dxb, wo_ref[POOL_WIDTH:c2, :])
        dyc_ref[...] = _dot_nt(dxb, wo_ref[c2:, :]).astype(BF16)
        dyct_ref[...] = _dot_nt(wo_ref[c2:, :], dxb).astype(BF16)

    row = lambda w: pl.BlockSpec((tm, w), lambda i: (i, 0))
    return pl.pallas_call(
        body, name=name, grid=(t // tm,),
        in_specs=[row(d), pl.BlockSpec((d, d), lambda i: (0, 0))],
        out_specs=[row(POOL_WIDTH), row(SG_WIDTH), row(SB_WIDTH), pl.BlockSpec((SB_WIDTH, tm), lambda i: (0, i)), row(d)],
        out_shape=[jax.ShapeDtypeStruct((t, POOL_WIDTH), F32), jax.ShapeDtypeStruct((t, SG_WIDTH), F32),
                   jax.ShapeDtypeStruct((t, SB_WIDTH), BF16), jax.ShapeDtypeStruct((SB_WIDTH, t), BF16),
                   jax.ShapeDtypeStruct((t, d), BF16)],
        compiler_params=_params(("parallel",)),
    )(dx1, wo)


def _sba_bwd(qkv, ktb, dyc, dyct, cc, nvis, tq, name, ride=None):
    t = qkv.shape[0]
    nb = t // tq
    idx = jnp.arange(tq)
    upper = (idx[None, :] > idx[:, None]).astype(BF16)
    lower = (idx[None, :] < idx[:, None]).astype(BF16)

    qb = SB_QUERY_BLOCKS_PER_STEP

    def body(q_ref, k_ref, v_ref, kt_ref, do_ref, dot_ref, c_ref, n_ref, up_ref, lo_ref, dqt_ref, dk_ref, dv_ref):
        @pl.when(pl.program_id(1) == 0)
        def _():
            dk_ref[...] = jnp.zeros_like(dk_ref)
            dv_ref[...] = jnp.zeros_like(dv_ref)

        for sub in range(qb):
            query_block(pl.program_id(1) * qb + sub, slice(sub * tq, (sub + 1) * tq), q_ref, k_ref, v_ref, kt_ref,
                        do_ref, dot_ref, c_ref, n_ref, up_ref, lo_ref, dqt_ref, dk_ref, dv_ref)

    def query_block(i, mine, q_ref, k_ref, v_ref, kt_ref, do_ref, dot_ref, c_ref, n_ref, up_ref, lo_ref,
                    dqt_ref, dk_ref, dv_ref):
        q = q_ref[mine, :]
        do = do_ref[mine, :]
        dot = dot_ref[:, mine]
        up = up_ref[...]
        lo = lo_ref[...]
        lane_head = lax.broadcasted_iota(jnp.int32, (1, 128), 1) // SB_HD
        sub_head = lax.broadcasted_iota(jnp.int32, (128, 1), 0) // SB_HD
        causal = (lax.broadcasted_iota(jnp.int32, (tq, tq), 0) < lax.broadcasted_iota(jnp.int32, (tq, tq), 1))
        hms = [lane_head == h for h in range(2)]
        qh = [jnp.where(hm, q, jnp.zeros_like(q)) for hm in hms]
        qs = [x * 0.125 for x in qh]
        doh = [jnp.where(hm, do, jnp.zeros_like(do)) for hm in hms]

        def blocks(js, carry):
            cgs, dqt = list(carry[:2]), carry[2]
            rows = [pl.ds(pl.multiple_of(j * tq, tq), tq) for j, _ in js]
            kj = [k_ref[r, :] for r in rows]
            vj = [v_ref[r, :] for r in rows]
            kt = [kt_ref[j] for j, _ in js]
            chains = [(b, h) for b in range(len(js)) for h in range(2)]
            z = [_dot_nt(kj[b], qh[h]) for b, h in chains]
            da = [_dot(jnp.where(hms[h], vj[b], jnp.zeros_like(vj[b])), dot) for b, h in chains]
            ls = [_log2_sigmoids(zz * SB_SCALE) for zz in z]
            lb = [x[0] for x in ls]
            l1 = [jnp.where(causal, x[1], 0.0) if js[b][1] else x[1] for x, (b, h) in zip(ls, chains)]
            after = [_tri_dot(up, x) for x in l1]
            a = [jnp.exp2(lb[n] + after[n] + c_ref[h, js[b][0], :, mine]) for n, (b, h) in enumerate(chains)]
            a = [jnp.where(causal, a[n], 0.0) if js[b][1] else a[n] for n, (b, h) in enumerate(chains)]
            g = [a[n] * da[n] for n in range(len(chains))]
            gloc = [_tri_dot(lo, x) for x in g]
            dzb = []
            for n, (b, h) in enumerate(chains):
                gsum = gloc[n] + cgs[h]
                dz = g[n] - jnp.exp2(lb[n]) * (g[n] + gsum)
                dzb.append((jnp.where(causal, dz, 0.0) if js[b][1] else dz).astype(BF16))
                cgs[h] = gsum[tq - 1:tq, :] + g[n][tq - 1:tq, :]
            ab = [x.astype(BF16) for x in a]
            for n, (b, h) in enumerate(chains):
                dqt = dqt + _dot(jnp.where(sub_head == h, kt[b], jnp.zeros_like(kt[b])), dzb[n])
            for b in range(len(js)):
                dk_ref[rows[b], :] += _dot(dzb[2 * b], qs[0]) + _dot(dzb[2 * b + 1], qs[1])
                dv_ref[rows[b], :] += _dot(ab[2 * b], doh[0]) + _dot(ab[2 * b + 1], doh[1])
            return cgs[0], cgs[1], dqt

        n = jnp.clip(n_ref[pl.program_id(0), i].astype(jnp.int32), 1, i + 1)
        zero = jnp.zeros((1, tq), F32)
        carry = lax.fori_loop(jnp.minimum(i + 1 - n, i - 1), i - 1, lambda s, cr: blocks([(s, False)], cr),
                              (zero, zero, jnp.zeros((128, tq), F32)))
        carry = lax.cond(i > 0, lambda: blocks([(i - 1, False), (i, True)], carry), lambda: blocks([(i, True)], carry))
        dqt_ref[:, mine] = carry[2] * 0.125

    return _call(
        body, name, (SB_PAIRS, nb // qb),
        [pl.BlockSpec((qb * tq, 128), lambda p, i: (i, p)),
         pl.BlockSpec((t, 128), lambda p, i: (0, SB_PAIRS + p)),
         pl.BlockSpec((t, 128), lambda p, i: (0, 2 * SB_PAIRS + p)),
         pl.BlockSpec((None, nb, 128, tq), lambda p, i: (p, 0, 0, 0)),
         pl.BlockSpec((qb * tq, 128), lambda p, i: (i, p)),
         pl.BlockSpec((128, qb * tq), lambda p, i: (p, i)),
         pl.BlockSpec((2, nb, 1, qb * tq), lambda p, i: (p, 0, 0, i)),
         pl.BlockSpec(memory_space=pltpu.SMEM),
         pl.BlockSpec((tq, tq), lambda p, i: (0, 0)),
         pl.BlockSpec((tq, tq), lambda p, i: (0, 0))],
        [pl.BlockSpec((128, qb * tq), lambda p, i: (p, i)),
         pl.BlockSpec((t, 128), lambda p, i: (0, p)),
         pl.BlockSpec((t, 128), lambda p, i: (0, p))],
        [jax.ShapeDtypeStruct((SB_WIDTH, t), F32), jax.ShapeDtypeStruct((t, SB_WIDTH), F32),
         jax.ShapeDtypeStruct((t, SB_WIDTH), F32)],
        [], ("arbitrary", "arbitrary"), (qkv, qkv, qkv, ktb, dyc, dyct, cc, nvis, upper, lower), ride)


def _sg_bwd(ab, dyb, gn, wm, wmt, bfull, tm, name):
    t = ab.shape[0]
    nt = t // tm
    sel = (jnp.arange(SG_WIDTH)[:, None] // SG_HD == jnp.arange(CHUNK)[None, :]).astype(F32)

    def body(u_ref, v_ref, dy_ref, gn_ref, wm_ref, wmt_ref, b_ref, sel_ref,
             dup_ref, dvp_ref, dgn_ref, dw_ref, db_ref, dbacc_ref):
        i = pl.program_id(0)

        @pl.when(i == 0)
        def _():
            dgn_ref[...] = jnp.zeros_like(dgn_ref)
            dw_ref[...] = jnp.zeros_like(dw_ref)
            dbacc_ref[...] = jnp.zeros_like(dbacc_ref)

        tril = (lax.broadcasted_iota(jnp.int32, (CHUNK, CHUNK), 0) >= lax.broadcasted_iota(jnp.int32, (CHUNK, CHUNK), 1))
        gn_ = gn_ref[...]
        for c in range(tm // CHUNK):
            rows = slice(c * CHUNK, (c + 1) * CHUNK)
            up, vp, dy = u_ref[rows, :], v_ref[rows, :], dy_ref[rows, :]
            u, v = _gelu(up), _gelu(vp)
            r = _rstd(v)
            vn = (v * r * gn_).astype(BF16)
            sv = b_ref[...]
            for h in range(SG_HEADS):
                sv = sv + jnp.where(_head_lanes(h), _dot(wm_ref[h], vn), 0.0)
            dup_ref[rows, :] = dy * sv * _gelu_grad(up)
            dsv = dy * u
            dbacc_ref[...] += dsv
            dvn = jnp.zeros((CHUNK, SG_WIDTH), F32)
            for h in range(SG_HEADS):
                dsv_h = jnp.where(_head_lanes(h), dsv, 0.0).astype(BF16)
                dvn = dvn + _dot(wmt_ref[h], dsv_h)
                dw_ref[h] += jnp.where(tril, _dot_nt(dsv_h, vn), 0.0)
            dv, dgn = _rms_bwd(v, r, gn_, dvn)
            dgn_ref[...] += dgn
            dvp_ref[rows, :] = dv * _gelu_grad(vp)

        @pl.when(i == nt - 1)
        def _():
            db_ref[...] = jnp.dot(dbacc_ref[...], sel_ref[...], preferred_element_type=F32,
                                  precision=lax.Precision.HIGHEST)

    const = lambda shape: pl.BlockSpec(shape, lambda i: tuple(0 for _ in shape))
    return pl.pallas_call(
        body, name=name, grid=(nt,),
        in_specs=[pl.BlockSpec((tm, SG_WIDTH), lambda i: (i, 1)), pl.BlockSpec((tm, SG_WIDTH), lambda i: (i, 2)),
                  pl.BlockSpec((tm, SG_WIDTH), lambda i: (i, 0)), const((1, SG_WIDTH)),
                  const((SG_HEADS, CHUNK, CHUNK)), const((SG_HEADS, CHUNK, CHUNK)), const((CHUNK, SG_WIDTH)),
                  const((SG_WIDTH, CHUNK))],
        out_specs=[pl.BlockSpec((tm, SG_WIDTH), lambda i: (i, 0)), pl.BlockSpec((tm, SG_WIDTH), lambda i: (i, 0)),
                   const((1, SG_WIDTH)), const((SG_HEADS, CHUNK, CHUNK)), const((CHUNK, CHUNK))],
        out_shape=[jax.ShapeDtypeStruct((t, SG_WIDTH), F32), jax.ShapeDtypeStruct((t, SG_WIDTH), F32),
                   jax.ShapeDtypeStruct((1, SG_WIDTH), F32), jax.ShapeDtypeStruct((SG_HEADS, CHUNK, CHUNK), F32),
                   jax.ShapeDtypeStruct((CHUNK, CHUNK), F32)],
        scratch_shapes=[pltpu.VMEM((CHUNK, SG_WIDTH), F32)],
        compiler_params=_params(("arbitrary",)),
    )(ab, ab, dyb, gn, wm, wmt, bfull, sel)


def _pool_bwd(ab, dya, wbd, scale, tm, name):
    t = ab.shape[0]
    nt = t // tm
    hb = tm // POOL_HALO
    nh = t // POOL_HALO

    def body(cur_ref, prev_ref, dy_ref, dyn_ref, w_ref, s_ref, da_ref, dw_ref, ds_ref):
        i = pl.program_id(0)

        @pl.when(i == 0)
        def _():
            dw_ref[...] = jnp.zeros_like(dw_ref)
            ds_ref[...] = jnp.zeros_like(ds_ref)

        prev = jnp.where(i == 0, 0.0, prev_ref[...])
        d = _pool_diff(cur_ref[...], prev, i * tm).astype(BF16)
        dy = dy_ref[...]
        ds_ref[...] += jnp.sum(dy * _dot(d, w_ref[...]), axis=0, keepdims=True)
        dyn = jnp.where(i == nt - 1, 0.0, dyn_ref[...])
        dys = (jnp.concatenate([dy, dyn], axis=0) * s_ref[...]).astype(BF16)
        dw_ref[...] += _dot_tn(d, dys[:tm])
        dd = _dot_nt(dys, w_ref[...])
        fwd = _pool_window_sums(dd / _pool_count(i * tm, tm + POOL_HALO), True)
        da_ref[...] = fwd[:tm] - dd[:tm]

    return pl.pallas_call(
        body, name=name, grid=(nt,),
        in_specs=[pl.BlockSpec((tm, POOL_WIDTH), lambda i: (i, 0)),
                  pl.BlockSpec((POOL_HALO, POOL_WIDTH), lambda i: (jnp.maximum(i * hb - 1, 0), 0)),
                  pl.BlockSpec((tm, POOL_WIDTH), lambda i: (i, 0)),
                  pl.BlockSpec((POOL_HALO, POOL_WIDTH), lambda i: (jnp.minimum((i + 1) * hb, nh - 1), 0)),
                  pl.BlockSpec((POOL_WIDTH, POOL_WIDTH), lambda i: (0, 0)),
                  pl.BlockSpec((1, POOL_WIDTH), lambda i: (0, 0))],
        out_specs=[pl.BlockSpec((tm, POOL_WIDTH), lambda i: (i, 0)),
                   pl.BlockSpec((POOL_WIDTH, POOL_WIDTH), lambda i: (0, 0)),
                   pl.BlockSpec((1, POOL_WIDTH), lambda i: (0, 0))],
        out_shape=[jax.ShapeDtypeStruct((t, POOL_WIDTH), F32), jax.ShapeDtypeStruct((POOL_WIDTH, POOL_WIDTH), F32),
                   jax.ShapeDtypeStruct((1, POOL_WIDTH), F32)],
        compiler_params=_params(("arbitrary",)),
    )(ab, ab, dya, dya, wbd, scale)


def _inproj_bwd(dx1, x, g, da, dup, dvp, dqt, dk, dv, w, tm, name, ride=None):
    t, d = x.shape
    n = w.shape[1]
    nt = t // tm

    def body(dx1_ref, x_ref, g_ref, da_ref, du_ref, dv_ref, dqt_ref, dk_ref, dvv_ref, w_ref,
             dx_ref, h_ref, dp_ref, dg_ref):
        @pl.when(pl.program_id(0) == 0)
        def _():
            dg_ref[...] = jnp.zeros_like(dg_ref)

        dp = jnp.concatenate([da_ref[...], du_ref[...], dv_ref[...], dqt_ref[...].T, dk_ref[...], dvv_ref[...]],
                             axis=1).astype(BF16)
        dp_ref[...] = dp
        xx = x_ref[...]
        r = _rstd(xx)
        h_ref[...] = (xx * r * g_ref[...]).astype(BF16)
        dx, dg = _rms_bwd(xx, r, g_ref[...], _dot_nt(dp, w_ref[...]))
        dx_ref[...] = dx1_ref[...] + dx
        dg_ref[...] += dg

    row = lambda w_: pl.BlockSpec((tm, w_), lambda i: (i, 0))
    return _call(
        body, name, (nt,),
        [row(d), row(d), pl.BlockSpec((1, d), lambda i: (0, 0)), row(POOL_WIDTH), row(SG_WIDTH),
         row(SG_WIDTH), pl.BlockSpec((SB_WIDTH, tm), lambda i: (0, i)), row(SB_WIDTH), row(SB_WIDTH),
         pl.BlockSpec((d, n), lambda i: (0, 0))],
        [row(d), row(d), row(n), pl.BlockSpec((1, d), lambda i: (0, 0))],
        [jax.ShapeDtypeStruct((t, d), F32), jax.ShapeDtypeStruct((t, d), BF16),
         jax.ShapeDtypeStruct((t, n), BF16), jax.ShapeDtypeStruct((1, d), F32)],
        [], ("arbitrary",), (dx1, x, g, da, dup, dvp, dqt, dk, dv, w), ride)


def _full_w_in(gathered):
    return gathered.transpose(1, 0, 2).reshape(D_MODEL, IN_COLS)


_SMALL_SHAPES = ((D_MODEL,), (4, POOL_GW, POOL_GW), (POOL_WIDTH,), (SG_WIDTH,), (SG_HEADS, CHUNK, CHUNK),
                 (SG_HEADS, CHUNK), (D_MODEL,))
_SMALL_SIZES = tuple(functools.reduce(lambda p, q: p * q, shp) for shp in _SMALL_SHAPES)
_SMALL_ROWS = sum(_SMALL_SIZES) // 128
_NORM1_ROWS = D_MODEL // 128


def _pack_small_layer(arrs):
    return jnp.concatenate([a.reshape(-1) for a in arrs]).reshape(_SMALL_ROWS, 128)


def _pack_small(arrs, final):
    return jnp.concatenate([_pack_small_layer([a[l] for a in arrs]) for l in range(DEPTH)] + [final.reshape(-1, 128)])


def _unpack_small(buf):
    per_layer = []
    for l in range(DEPTH):
        flat, off, outs = buf[l * _SMALL_ROWS:(l + 1) * _SMALL_ROWS].reshape(-1), 0, []
        for shp, size in zip(_SMALL_SHAPES, _SMALL_SIZES):
            outs.append(flat[off:off + size].reshape(shp))
            off += size
        per_layer.append(outs)
    return [jnp.stack([per_layer[l][k] for l in range(DEPTH)]) for k in range(len(_SMALL_SHAPES))] + \
           [buf[DEPTH * _SMALL_ROWS:].reshape(-1)]


def _tiles(t):
    return min(512, t), min(256, t // 4), min(2048, t)


def _layer_fwd(xl, wi, wo, wud, small_w, l, ride_mlp=None):
    n1, pw, ps, sn, sw, sb, n2 = small_w
    tm, tq, _ = _tiles(xl.shape[0])
    nb = xl.shape[0] // tq
    wbd = jnp.zeros((4, POOL_GW, 4, POOL_GW), F32)
    for gi in range(4):
        wbd = wbd.at[gi, :, gi, :].set(pw[gi])
    wbd = wbd.reshape(POOL_WIDTH, POOL_WIDTH).astype(BF16)
    wm = (sw * jnp.tril(jnp.ones((CHUNK, CHUNK), F32))).astype(BF16)
    wmt = wm.transpose(0, 2, 1)
    bfull = jnp.repeat(sb.T, SG_HD, axis=1)
    g1, g2, psc, gn = n1[None, :], n2[None, :], ps[None, :], sn[None, :]

    if isinstance(wo, Exchange):
        (ab, qkv, ktb, vtb), (wo,) = _inproj_fwd(xl, g1, wi, tm, tq, f"inproj_fwd{l}", wo)
    else:
        ab, qkv, ktb, vtb = _inproj_fwd(xl, g1, wi, tm, tq, f"inproj_fwd{l}")
    ya = _pool_fwd(ab, wbd, psc, tm, f"pool_fwd{l}")
    yb = _sg_fwd(ab, gn, wm, bfull, tm, f"sg_fwd{l}")
    if isinstance(wud, Exchange):
        (yct, cc, nvis), (wu, wd) = _sba_fwd(qkv, vtb, tq, f"sba_fwd{l}", wud)
    else:
        (yct, cc, nvis), (wu, wd) = _sba_fwd(qkv, vtb, tq, f"sba_fwd{l}"), wud
    wo = wo.reshape(D_MODEL, D_MODEL)
    res = _outproj_mlp_fwd(xl, ya, yb, yct, wo, g2, wu, wd, min(MLP_ROWS, xl.shape[0]), f"mlp_fwd{l}", ride_mlp)
    (x1, u, x2), rode = res if ride_mlp is not None else (res, None)
    saved = dict(x=xl, ab=ab, qkv=qkv, ktb=ktb, cc=cc, nvis=nvis, ya=ya, yb=yb, yct=yct, x1=x1, u=u, wi=wi, wo=wo, wu=wu,
                 wd=wd, wbd=wbd, wm=wm, wmt=wmt, bfull=bfull, g1=g1, g2=g2, psc=psc, gn=gn)
    return x2, saved, rode


def _layer_bwd(dx, s, l, ride_mlp=None, scatter_in_attn=True, gather_small=False, split_w_in=False):
    tm, tq, tw = _tiles(dx.shape[0])
    ktb = s["ktb"]
    res = _mlp_bwd(dx, s["x1"], s["g2"], s["u"], s["wu"], s["wd"], min(MLP_ROWS, dx.shape[0]), f"mlp_bwd{l}", ride_mlp)
    (dx1, du, r, h2, dx2b, dn2), rode = res if ride_mlp is not None else (res, None)
    dw_up = _matmul_tn(h2, du, D_MODEL, 2 * FF_SHARD, tw, f"dw_up{l}", by_column_block=FF_SHARD)
    dw_down = _matmul_tn(r, dx2b, 1024, D_MODEL, tw, f"dw_down{l}").reshape(N_DEV, FF_SHARD, D_MODEL)
    dya, dyb, dyc, dyct, dx1b = _outproj_bwd(dx1, s["wo"], tm, f"outproj_bwd{l}")
    dw_out = jnp.concatenate([
        _matmul_tn(jnp.concatenate([s["ya"], s["yb"]], axis=1), dx1b, POOL_WIDTH + SG_WIDTH, D_MODEL, tw, f"dw_out_ab{l}"),
        _matmul_tn(s["yct"], dx1b, SB_WIDTH, D_MODEL, tw, f"dw_out_c{l}", a_transposed=True)]
    ).reshape(N_DEV, OUT_SHARD, D_MODEL)
    if scatter_in_attn:
        (dqt, dk, dv), (dw_out, dw_up, dw_down) = _sba_bwd(s["qkv"], ktb, dyc, dyct, s["cc"], s["nvis"], tq, f"sba_bwd{l}",
                                                           Exchange([dw_out, dw_up, dw_down], True))
    else:
        dqt, dk, dv = _sba_bwd(s["qkv"], ktb, dyc, dyct, s["cc"], s["nvis"], tq, f"sba_bwd{l}")
    dup, dvp, dgn, dwm, dbm = _sg_bwd(s["ab"], dyb, s["gn"], s["wm"], s["wmt"], s["bfull"], tm, f"sg_bwd{l}")
    da, dwbd, dpsc = _pool_bwd(s["ab"], dya, s["wbd"], s["psc"], tm, f"pool_bwd{l}")
    dpw = jnp.stack([dwbd[gi * POOL_GW:(gi + 1) * POOL_GW, gi * POOL_GW:(gi + 1) * POOL_GW] for gi in range(4)])
    small = _pack_small_layer([jnp.zeros((D_MODEL,), F32), dpw, dpsc[0], dgn[0], dwm, dbm[:, :SG_HEADS].T, dn2[0]])[_NORM1_ROWS:]
    res = _inproj_bwd(dx1, s["x"], s["g1"], da, dup, dvp, dqt, dk, dv, s["wi"], tm, f"inproj_bwd{l}",
                      Exchange([small], False) if gather_small else None)
    (dx, h1, dproj, dn1), small = (res[0], res[1][0]) if gather_small else (res, small)
    by_shard = lambda g: g.reshape(-1, N_DEV, IN_SHARD).transpose(1, 0, 2)
    if split_w_in:
        half = D_MODEL // 2
        upper = by_shard(_matmul_tn(h1, dproj, half, IN_COLS // 3, tw, f"dw_in_upper{l}", a_blocks=(0, 1)))
        lower, (upper,) = _matmul_tn(h1, dproj, half, IN_COLS // 3, tw, f"dw_in_lower{l}", a_blocks=(1, 1),
                                     ride=Exchange([upper], True))
        dw_in = (upper, by_shard(lower))
    else:
        dw_in = by_shard(_matmul_tn(h1, dproj, D_MODEL, IN_COLS // 3, tw, f"dw_in{l}"))
    return dx, (dw_in, dw_out, dw_up, dw_down), (dn1.reshape(_NORM1_ROWS, 128), small), rode


def kernel(x, norm1, w_in, pool_w, pool_scale, sg_norm, sg_w, sg_b, w_out, norm2, w_up, w_down, final_norm, loss_target, m_norm1, m_w_in, m_pool_w, m_pool_scale, m_sg_norm, m_sg_w, m_sg_b, m_w_out, m_norm2, m_w_up, m_w_down, m_final_norm, v_norm1, v_w_in, v_pool_w, v_pool_scale, v_sg_norm, v_sg_w, v_sg_b, v_w_out, v_norm2, v_w_up, v_w_down, v_final_norm):
    t = x.shape[1]
    tm = _tiles(t)[0]
    small_w = (norm1, pool_w, pool_scale, sg_norm, sg_w, sg_b, norm2)
    big_w = (w_in, w_out, w_up, w_down)
    big_m = (m_w_in, m_w_out, m_w_up, m_w_down)
    big_v = (v_w_in, v_w_out, v_w_up, v_w_down)
    shards = [[w[l].astype(BF16) for w in big_w] for l in range(DEPTH)]

    wi0 = _full_w_in(_exchange(shards[0][:1], False, "gather_w_in0")[0])
    x1, s0, g1 = _layer_fwd(x.reshape(t, D_MODEL), wi0, Exchange(shards[0][1:2], False), Exchange(shards[0][2:], False),
                            tuple(w[0] for w in small_w), 0, ride_mlp=Exchange(shards[1][:2], False))
    x2, s1, _ = _layer_fwd(x1, _full_w_in(g1[0]), g1[1], Exchange(shards[1][2:], False), tuple(w[1] for w in small_w), 1)
    loss_local, dx, dfinal = _loss_grad(x2, final_norm[None, :], loss_target.reshape(t, D_MODEL), tm, "loss_grad")
    loss = lax.psum(loss_local[0, 0], MESH_AXES)

    dx, parts1, small1, _ = _layer_bwd(dx, s1, 1)
    early = jnp.concatenate(list(small1) + [dfinal.reshape(-1, 128)])
    dx, parts0, (dn1, small0), (recv_in1, early) = _layer_bwd(
        dx, s0, 0, ride_mlp=Exchange([parts1[0], early], [True, False]), gather_small=True, split_w_in=True)
    grad_x = dx.reshape(x.shape)
    recv_upper, lower = parts0[0]
    recv_lower, dn1 = _exchange([lower, dn1], [True, False], "scatter_w_in0_lower_gather_norm1_0")
    recv_in0 = jnp.concatenate([recv_upper, recv_lower], axis=1)
    small_all = jnp.concatenate([dn1, small0, early], axis=1)
    received = [[recv_in0] + list(parts0[1:]), [recv_in1] + list(parts1[1:])]

    big = [None] * 4
    for l in reversed(range(DEPTH)):
        for k in range(4):
            big[k] = _reduce_adamw(received[l][k], big_w[k], big_m[k], big_v[k], l, big[k], f"adamw{k}_{l}")

    sm = _reduce_adamw(
        small_all,
        _pack_small(small_w, final_norm)[None],
        _pack_small([m_norm1, m_pool_w, m_pool_scale, m_sg_norm, m_sg_w, m_sg_b, m_norm2], m_final_norm)[None],
        _pack_small([v_norm1, v_pool_w, v_pool_scale, v_sg_norm, v_sg_w, v_sg_b, v_norm2], v_final_norm)[None],
        0, None, "adamw_replicated")

    out = [loss, grad_x]
    for k in range(4):
        n1, pw, ps, sn, sw, sb, n2, fn = _unpack_small(sm[k][0])
        out += [n1, big[0][k], pw, ps, sn, sw, sb, big[1][k], n2, big[2][k], big[3][k], fn]
    return tuple(out)
```

```python
import functools

import jax
import jax.numpy as jnp
from jax import lax
from jax.experimental import pallas as pl
from jax.experimental.pallas import tpu as pltpu

F32 = jnp.float32
BF16 = jnp.bfloat16

D_MODEL = 1024
DEPTH = 2
POOL_WIDTH = 256
SG_WIDTH = 256
SB_WIDTH = 512
POOL_WINDOWS = (2, 4, 8, 16)
POOL_GW = 64
POOL_HALO = 16
CHUNK = 128
SG_HEADS = 4
SG_HD = 64
SB_HD = 64
SB_PAIRS = SB_WIDTH // 128
AB_COLS = POOL_WIDTH + 2 * SG_WIDTH
IN_COLS = AB_COLS + 3 * SB_WIDTH
D_FF = 4096
EPS = 1e-6
N_DEV = 8
FF_SHARD = D_FF // N_DEV
IN_SHARD = IN_COLS // N_DEV
OUT_SHARD = D_MODEL // N_DEV
ADAM_LR = 0.001
ADAM_B1 = 0.9
ADAM_B2 = 0.999
ADAM_EPS = 1e-08
ADAM_WD = 0.01
ADAM_STEP = 10
VMEM_LIMIT = 56 * 1024 * 1024
MLP_ROWS = 1024
MESH_AXES = ("x", "y", "c")


def _dot(a, b):
    return jnp.dot(a, b, preferred_element_type=F32)


def _dot_nt(a, b):
    return lax.dot_general(a, b, (((1,), (1,)), ((), ())), preferred_element_type=F32)


def _dot_tn(a, b):
    return lax.dot_general(a, b, (((0,), (0,)), ((), ())), preferred_element_type=F32)


def _rstd(x):
    return lax.rsqrt(jnp.mean(x * x, axis=-1, keepdims=True) + EPS)


def _rms_bwd(x, r, g, dh):
    gq = dh * g
    dx = r * gq - x * (r * r * r) * jnp.mean(gq * x, axis=-1, keepdims=True)
    dg = jnp.sum(dh * x * r, axis=0, keepdims=True)
    return dx, dg


def _params(sem=None):
    kw = dict(vmem_limit_bytes=VMEM_LIMIT)
    if sem is not None:
        kw["dimension_semantics"] = sem
    return pltpu.CompilerParams(**kw)


def _row_tile(rows, cap):
    best = 8
    for t in range(8, min(rows, cap) + 1, 8):
        if rows % t == 0:
            best = t
    return best


def _peer(k):
    x, y, c = lax.axis_index("x"), lax.axis_index("y"), lax.axis_index("c")
    return (1 - x if k & 4 else x, 1 - y if k & 2 else y, 1 - c if k & 1 else c)


def _my_index():
    return 4 * lax.axis_index("x") + 2 * lax.axis_index("y") + lax.axis_index("c")


class Exchange:
    def __init__(self, arrs, scatter):
        self.arrs = list(arrs)
        self.scatter = list(scatter) if isinstance(scatter, (list, tuple)) else [scatter] * len(self.arrs)
        self.n = len(self.arrs)
        self.any_specs = [pl.BlockSpec(memory_space=pl.ANY)] * self.n
        self.out_shape = [jax.ShapeDtypeStruct((N_DEV,) + a.shape[-2:], a.dtype) for a in self.arrs]
        self.sems = [pltpu.SemaphoreType.DMA((self.n, N_DEV - 1)), pltpu.SemaphoreType.DMA((self.n, N_DEV - 1)),
                     pltpu.SemaphoreType.DMA((self.n,))]

    def _copies(self, ins, outs, sems):
        send_sems, recv_sems, local_sems = sems
        me = _my_index()
        local, remote = [], []
        for a in range(self.n):
            sc = self.scatter[a]
            local.append(pltpu.make_async_copy(ins[a].at[me] if sc else ins[a], outs[a].at[me], local_sems.at[a]))
            for k in range(1, N_DEV):
                px, py, pc = _peer(k)
                remote.append(pltpu.make_async_remote_copy(
                    src_ref=ins[a].at[4 * px + 2 * py + pc] if sc else ins[a], dst_ref=outs[a].at[me],
                    send_sem=send_sems.at[a, k - 1], recv_sem=recv_sems.at[a, k - 1],
                    device_id=(px, py, pc), device_id_type=pl.DeviceIdType.MESH))
        return local, remote

    def start(self, ins, outs, sems):
        local, remote = self._copies(ins, outs, sems)
        for cp in local + remote:
            cp.start()

    def wait(self, ins, outs, sems):
        local, remote = self._copies(ins, outs, sems)
        for cp in remote:
            cp.wait_recv()
        for cp in remote:
            cp.wait_send()
        for cp in local:
            cp.wait()

    def alone(self, name):
        n = self.n

        def body(*refs):
            ins, outs, sems = refs[:n], refs[n:2 * n], refs[2 * n:]
            self.start(ins, outs, sems)
            self.wait(ins, outs, sems)

        return pl.pallas_call(body, name=name, out_shape=self.out_shape, in_specs=self.any_specs,
                              out_specs=self.any_specs, scratch_shapes=self.sems)(*self.arrs)


def _exchange(arrs, scatter, name):
    return Exchange(arrs, scatter).alone(name)


def _call(body, name, grid, in_specs, out_specs, out_shape, scratch_shapes, semantics, args, ride=None):
    if ride is None:
        return pl.pallas_call(body, name=name, grid=grid, in_specs=in_specs, out_specs=out_specs, out_shape=out_shape,
                              scratch_shapes=scratch_shapes, compiler_params=_params(semantics))(*args)
    single = not isinstance(out_shape, (list, tuple))
    out_specs, out_shape = ([out_specs], [out_shape]) if single else (list(out_specs), list(out_shape))
    n_in, n_out, n_scr, n = len(in_specs), len(out_specs), len(scratch_shapes), ride.n

    def riding(*refs):
        ins, cins = refs[:n_in], refs[n_in:n_in + n]
        outs, couts = refs[n_in + n:n_in + n + n_out], refs[n_in + n + n_out:n_in + 2 * n + n_out]
        scr, sems = refs[n_in + 2 * n + n_out:n_in + 2 * n + n_out + n_scr], refs[n_in + 2 * n + n_out + n_scr:]
        first = functools.reduce(lambda p, q: p & q, [pl.program_id(a) == 0 for a in range(len(grid))])
        last = functools.reduce(lambda p, q: p & q, [pl.program_id(a) == g - 1 for a, g in enumerate(grid)])

        @pl.when(first)
        def _():
            ride.start(cins, couts, sems)

        body(*ins, *outs, *scr)

        @pl.when(last)
        def _():
            ride.wait(cins, couts, sems)

    res = pl.pallas_call(
        riding, name=name, grid=grid, in_specs=list(in_specs) + ride.any_specs, out_specs=out_specs + ride.any_specs,
        out_shape=out_shape + ride.out_shape, scratch_shapes=list(scratch_shapes) + ride.sems,
        compiler_params=_params(("arbitrary",) * len(grid)))(*args, *ride.arrs)
    own = res[0] if single else list(res[:n_out])
    return own, list(res[n_out:])


def _reduce_adamw(parts, w, m, v, l, prev, name):
    _, rows, n = parts.shape
    tr = _row_tile(rows, max(8, (1 << 18) // n))
    c1 = 1.0 - ADAM_B1 ** ADAM_STEP
    c2 = 1.0 - ADAM_B2 ** ADAM_STEP

    def body(p_ref, w_ref, m_ref, v_ref, *rest):
        g_ref, d_ref, nm_ref, nv_ref = rest[-4:]
        g = p_ref[0].astype(F32)
        for s in range(1, N_DEV):
            g = g + p_ref[s].astype(F32)
        nm = ADAM_B1 * m_ref[...] + (1.0 - ADAM_B1) * g
        nv = ADAM_B2 * v_ref[...] + (1.0 - ADAM_B2) * (g * g)
        m_hat = nm / c1
        v_hat = nv / c2
        g_ref[...] = g
        d_ref[...] = -ADAM_LR * (m_hat / (jnp.sqrt(v_hat) + ADAM_EPS) + ADAM_WD * w_ref[...])
        nm_ref[...] = nm
        nv_ref[...] = nv

    blk = pl.BlockSpec((None, tr, n), lambda i: (l, i, 0))
    out = jax.ShapeDtypeStruct(w.shape, F32)
    prev = list(prev) if prev is not None else []
    return pl.pallas_call(
        body, name=name, grid=(rows // tr,),
        in_specs=[pl.BlockSpec((N_DEV, tr, n), lambda i: (0, i, 0)), blk, blk, blk] + [pl.BlockSpec(memory_space=pl.ANY)] * len(prev),
        out_specs=[blk, blk, blk, blk], out_shape=[out, out, out, out],
        input_output_aliases={4 + k: k for k in range(len(prev))},
        compiler_params=_params(("parallel",)),
    )(parts, w, m, v, *prev)


def _inproj_fwd(x, g, w, tm, tq, name, ride=None):
    t, d = x.shape
    n = w.shape[1]
    nb = t // tq
    per = tm // tq

    def body(x_ref, g_ref, w_ref, ab_ref, qkv_ref, kt_ref, vt_ref):
        xx = x_ref[...]
        h = (xx * _rstd(xx) * g_ref[...]).astype(BF16)
        ab_ref[...] = _dot(h, w_ref[:, :AB_COLS])
        qkv = _dot(h, w_ref[:, AB_COLS:])
        qkv_ref[...] = qkv.astype(BF16)
        for which, out_ref in ((1, kt_ref), (2, vt_ref)):
            for p in range(SB_PAIRS):
                for b in range(per):
                    cols = which * SB_WIDTH + p * 128
                    out_ref[p, b] = qkv[b * tq:(b + 1) * tq, cols:cols + 128].T.astype(BF16)

    tb = pl.BlockSpec((SB_PAIRS, per, 128, tq), lambda i: (0, i, 0, 0))
    tshape = jax.ShapeDtypeStruct((SB_PAIRS, nb, 128, tq), BF16)
    return _call(
        body, name, (t // tm,),
        [pl.BlockSpec((tm, d), lambda i: (i, 0)), pl.BlockSpec((1, d), lambda i: (0, 0)),
         pl.BlockSpec((d, n), lambda i: (0, 0))],
        [pl.BlockSpec((tm, AB_COLS), lambda i: (i, 0)), pl.BlockSpec((tm, n - AB_COLS), lambda i: (i, 0)), tb, tb],
        [jax.ShapeDtypeStruct((t, AB_COLS), F32), jax.ShapeDtypeStruct((t, n - AB_COLS), BF16), tshape, tshape],
        [], ("parallel",), (x, g, w), ride)


def _pool_window_sums(xx, forward):
    n = xx.shape[0]
    sh = (lambda k: n - k) if forward else (lambda k: k)
    s2 = xx + pltpu.roll(xx, sh(1), 0)
    s4 = s2 + pltpu.roll(s2, sh(2), 0)
    s8 = s4 + pltpu.roll(s4, sh(4), 0)
    s16 = s8 + pltpu.roll(s8, sh(8), 0)
    grp = lax.broadcasted_iota(jnp.int32, (1, POOL_WIDTH), 1) // POOL_GW
    return jnp.where(grp == 0, s2, jnp.where(grp == 1, s4, jnp.where(grp == 2, s8, s16)))


def _pool_count(t0, rows):
    grp = lax.broadcasted_iota(jnp.int32, (1, POOL_WIDTH), 1) // POOL_GW
    win = jnp.where(grp == 0, 2, jnp.where(grp == 1, 4, jnp.where(grp == 2, 8, 16)))
    tt = t0 + lax.broadcasted_iota(jnp.int32, (rows, 1), 0)
    return jnp.minimum(tt + 1, win).astype(F32)


def _pool_diff(cur, prev, t0):
    tm = cur.shape[0]
    sums = _pool_window_sums(jnp.concatenate([prev, cur], axis=0), False)[POOL_HALO:]
    return sums / _pool_count(t0, tm) - cur


def _pool_fwd(ab, wbd, scale, tm, name):
    t = ab.shape[0]
    hb = tm // POOL_HALO

    def body(cur_ref, prev_ref, w_ref, s_ref, y_ref):
        i = pl.program_id(0)
        prev = jnp.where(i == 0, 0.0, prev_ref[...])
        d = _pool_diff(cur_ref[...], prev, i * tm).astype(BF16)
        y_ref[...] = (_dot(d, w_ref[...]) * s_ref[...]).astype(BF16)

    return pl.pallas_call(
        body, name=name, grid=(t // tm,),
        in_specs=[pl.BlockSpec((tm, POOL_WIDTH), lambda i: (i, 0)),
                  pl.BlockSpec((POOL_HALO, POOL_WIDTH), lambda i: (jnp.maximum(i * hb - 1, 0), 0)),
                  pl.BlockSpec((POOL_WIDTH, POOL_WIDTH), lambda i: (0, 0)),
                  pl.BlockSpec((1, POOL_WIDTH), lambda i: (0, 0))],
        out_specs=pl.BlockSpec((tm, POOL_WIDTH), lambda i: (i, 0)),
        out_shape=jax.ShapeDtypeStruct((t, POOL_WIDTH), BF16),
        compiler_params=_params(("parallel",)),
    )(ab, ab, wbd, scale)


_GELU_K = 0.7978845608028654
_GELU_A = 0.044715


def _gelu(x):
    return 0.5 * x * (1.0 + jnp.tanh(_GELU_K * (x + _GELU_A * (x * x * x))))


def _gelu_grad(x):
    th = jnp.tanh(_GELU_K * (x + _GELU_A * (x * x * x)))
    return 0.5 * (1.0 + th) + 0.5 * x * (1.0 - th * th) * (_GELU_K * (1.0 + 3.0 * _GELU_A * (x * x)))


def _head_lanes(h):
    return lax.broadcasted_iota(jnp.int32, (1, SG_WIDTH), 1) // SG_HD == h


def _sg_fwd(ab, gn, wm, bfull, tm, name):
    t = ab.shape[0]

    def body(u_ref, v_ref, gn_ref, wm_ref, b_ref, y_ref):
        v = _gelu(v_ref[...])
        vn = (v * _rstd(v) * gn_ref[...]).astype(BF16)
        for c in range(tm // CHUNK):
            rows = slice(c * CHUNK, (c + 1) * CHUNK)
            vc = vn[rows]
            sv = b_ref[...]
            for h in range(SG_HEADS):
                sv = sv + jnp.where(_head_lanes(h), _dot(wm_ref[h], vc), 0.0)
            y_ref[rows, :] = (_gelu(u_ref[rows, :]) * sv).astype(BF16)

    return pl.pallas_call(
        body, name=name, grid=(t // tm,),
        in_specs=[pl.BlockSpec((tm, SG_WIDTH), lambda i: (i, 1)), pl.BlockSpec((tm, SG_WIDTH), lambda i: (i, 2)),
                  pl.BlockSpec((1, SG_WIDTH), lambda i: (0, 0)),
                  pl.BlockSpec((SG_HEADS, CHUNK, CHUNK), lambda i: (0, 0, 0)),
                  pl.BlockSpec((CHUNK, SG_WIDTH), lambda i: (0, 0))],
        out_specs=pl.BlockSpec((tm, SG_WIDTH), lambda i: (i, 0)),
        out_shape=jax.ShapeDtypeStruct((t, SG_WIDTH), BF16),
        compiler_params=_params(("parallel",)),
    )(ab, ab, gn, wm, bfull)


LOG2E = 1.4426950408889634
SB_SCALE = 0.125 * LOG2E
SB_DEAD_LOG2 = 152.0
SB_QUERY_BLOCKS_PER_STEP = 2


def _log2_sigmoids(y):
    neg_abs = lax.bitcast_convert_type(lax.bitcast_convert_type(y, jnp.uint32) | jnp.uint32(0x80000000), F32)
    lb = jnp.minimum(y, 0.0) - jnp.log(1.0 + jnp.exp2(neg_abs)) * LOG2E
    return lb, lb - y


def _split(x):
    hi = x.astype(BF16)
    return hi, (x - hi.astype(F32)).astype(BF16)


def _tri_dot(tri, x):
    hi, lo = _split(x)
    return _dot(tri, hi) + _dot(tri, lo)


def _sba_fwd(qkv, vtb, tq, name, ride=None):
    t = qkv.shape[0]
    nb = t // tq
    qb = SB_QUERY_BLOCKS_PER_STEP
    upper = (jnp.arange(tq)[None, :] > jnp.arange(tq)[:, None]).astype(BF16)

    def body(q_ref, k_ref, vt_ref, up_ref, ot_ref, c_ref, n_ref):
        for sub in range(qb):
            query_block(pl.program_id(1) * qb + sub, slice(sub * tq, (sub + 1) * tq), q_ref, k_ref, vt_ref, up_ref,
                        ot_ref, c_ref, n_ref)

    def query_block(i, mine, q_ref, k_ref, vt_ref, up_ref, ot_ref, c_ref, n_ref):
        q = q_ref[mine, :]
        up = up_ref[...]
        lane_head = lax.broadcasted_iota(jnp.int32, (1, 128), 1) // SB_HD
        sub_head = lax.broadcasted_iota(jnp.int32, (128, 1), 0) // SB_HD
        causal = (lax.broadcasted_iota(jnp.int32, (tq, tq), 0) < lax.broadcasted_iota(jnp.int32, (tq, tq), 1))
        qh = [jnp.where(lane_head == h, q, jnp.zeros_like(q)) for h in range(2)]

        def blocks(js, carry):
            cs, acc = list(carry[:2]), carry[2]
            kj = [k_ref[pl.ds(pl.multiple_of(j * tq, tq), tq), :] for j, _ in js]
            vt = [vt_ref[j] for j, _ in js]
            chains = [(b, h) for b in range(len(js)) for h in range(2)]
            z = [_dot_nt(kj[b], qh[h]) for b, h in chains]
            ls = [_log2_sigmoids(zz * SB_SCALE) for zz in z]
            lb = [x[0] for x in ls]
            l1 = [jnp.where(causal, x[1], 0.0) if js[b][1] else x[1] for x, (b, h) in zip(ls, chains)]
            after = [_tri_dot(up, x) for x in l1]
            a = []
            for n, (b, h) in enumerate(chains):
                c_ref[h, js[b][0], :, mine] = cs[h]
                an = jnp.exp2(lb[n] + after[n] + cs[h])
                a.append(jnp.where(causal, an, 0.0) if js[b][1] else an)
                cs[h] = cs[h] + after[n][0:1, :] + l1[n][0:1, :]
            alive = (jnp.max(jnp.maximum(cs[0], cs[1])) > -SB_DEAD_LOG2).astype(jnp.int32)
            for n, (b, h) in enumerate(chains):
                acc = acc + _dot(jnp.where(sub_head == h, vt[b], jnp.zeros_like(vt[b])), a[n].astype(BF16))
            return alive, cs[0], cs[1], acc

        def step(state):
            s, _, c0, c1, acc = state
            return (s + 1,) + blocks([(i - 1 - s, False)], (c0, c1, acc))

        zero = jnp.zeros((1, tq), F32)
        start = (zero, zero, jnp.zeros((128, tq), F32))
        joint = lax.cond(i > 0, lambda: blocks([(i, True), (i - 1, False)], start), lambda: blocks([(i, True)], start))
        state = lax.while_loop(lambda st: (st[0] < i) & (st[1] > 0), step, (jnp.minimum(i, 1),) + tuple(joint))
        ot_ref[:, mine] = state[4].astype(BF16)
        n_ref[pl.program_id(0), i] = (state[0] + 1).astype(F32)

    return _call(
        body, name, (SB_PAIRS, nb // qb),
        [pl.BlockSpec((qb * tq, 128), lambda p, i: (i, p)),
         pl.BlockSpec((t, 128), lambda p, i: (0, SB_PAIRS + p)),
         pl.BlockSpec((None, nb, 128, tq), lambda p, i: (p, 0, 0, 0)),
         pl.BlockSpec((tq, tq), lambda p, i: (0, 0))],
        [pl.BlockSpec((128, qb * tq), lambda p, i: (p, i)),
         pl.BlockSpec((2, nb, 1, qb * tq), lambda p, i: (p, 0, 0, i)),
         pl.BlockSpec(memory_space=pltpu.SMEM)],
        [jax.ShapeDtypeStruct((SB_WIDTH, t), BF16),
         jax.ShapeDtypeStruct((2 * SB_PAIRS, nb, 1, t), F32),
         jax.ShapeDtypeStruct((SB_PAIRS, nb), F32)],
        [], ("arbitrary", "arbitrary"), (qkv, qkv, vtb, upper), ride)


def _outproj_mlp_fwd(x, ya, yb, yct, wo, g2, wup, wdn, tm, name, ride=None):
    t, d = x.shape
    nf = wup.shape[0]

    def body(x_ref, ya_ref, yb_ref, yct_ref, wo_ref, g_ref, wu_ref, wd_ref, x1_ref, u_ref, x2_ref, h_ref, acc_ref):
        j = pl.program_id(1)

        @pl.when(j == 0)
        def _():
            x1 = (x_ref[...] + _dot(ya_ref[...], wo_ref[0:POOL_WIDTH, :])
                  + _dot(yb_ref[...], wo_ref[POOL_WIDTH:POOL_WIDTH + SG_WIDTH, :])
                  + _dot_tn(yct_ref[...], wo_ref[POOL_WIDTH + SG_WIDTH:, :]))
            x1_ref[...] = x1
            h_ref[...] = (x1 * _rstd(x1) * g_ref[...]).astype(BF16)
            acc_ref[...] = x1

        u = _dot(h_ref[...], wu_ref[...])
        u_ref[...] = u
        r = jnp.maximum(u, 0.0)
        acc_ref[...] += _dot((r * r).astype(BF16), wd_ref[...])

        @pl.when(j == nf - 1)
        def _():
            x2_ref[...] = acc_ref[...]

    row = lambda w: pl.BlockSpec((tm, w), lambda i, j: (i, 0))
    return _call(
        body, name, (t // tm, nf),
        [row(d), row(POOL_WIDTH), row(SG_WIDTH), pl.BlockSpec((SB_WIDTH, tm), lambda i, j: (0, i)),
         pl.BlockSpec((d, d), lambda i, j: (0, 0)), pl.BlockSpec((1, d), lambda i, j: (0, 0)),
         pl.BlockSpec((None, d, FF_SHARD), lambda i, j: (j, 0, 0)),
         pl.BlockSpec((None, FF_SHARD, d), lambda i, j: (j, 0, 0))],
        [row(d), pl.BlockSpec((tm, FF_SHARD), lambda i, j: (i, j)), row(d)],
        [jax.ShapeDtypeStruct((t, d), F32), jax.ShapeDtypeStruct((t, nf * FF_SHARD), F32),
         jax.ShapeDtypeStruct((t, d), F32)],
        [pltpu.VMEM((tm, d), BF16), pltpu.VMEM((tm, d), F32)],
        ("parallel", "arbitrary"), (x, ya, yb, yct, wo, g2, wup, wdn), ride)


def _loss_grad(x, g, target, tm, name):
    t, d = x.shape
    nt = t // tm

    def body(x_ref, g_ref, t_ref, loss_ref, dx_ref, dg_ref, sq_ref):
        i = pl.program_id(0)

        @pl.when(i == 0)
        def _():
            sq_ref[...] = jnp.zeros_like(sq_ref)
            dg_ref[...] = jnp.zeros_like(dg_ref)

        xx = x_ref[...]
        r = _rstd(xx)
        err = xx * r * g_ref[...] - t_ref[...]
        sq_ref[...] += jnp.sum(err * err, axis=0, keepdims=True)
        dx, dg = _rms_bwd(xx, r, g_ref[...], err * (1.0 / d))
        dx_ref[...] = dx
        dg_ref[...] += dg

        @pl.when(i == nt - 1)
        def _():
            loss_ref[...] = jnp.sum(sq_ref[...], axis=1, keepdims=True) * (0.5 / d)

    return pl.pallas_call(
        body, name=name, grid=(nt,),
        in_specs=[pl.BlockSpec((tm, d), lambda i: (i, 0)), pl.BlockSpec((1, d), lambda i: (0, 0)),
                  pl.BlockSpec((tm, d), lambda i: (i, 0))],
        out_specs=[pl.BlockSpec((1, 1), lambda i: (0, 0)), pl.BlockSpec((tm, d), lambda i: (i, 0)),
                   pl.BlockSpec((1, d), lambda i: (0, 0))],
        out_shape=[jax.ShapeDtypeStruct((1, 1), F32), jax.ShapeDtypeStruct((t, d), F32),
                   jax.ShapeDtypeStruct((1, d), F32)],
        scratch_shapes=[pltpu.VMEM((1, d), F32)],
        compiler_params=_params(("arbitrary",)),
    )(x, g, target)


def _mlp_bwd(dx2, x1, g2, u, wup, wdn, tm, name, ride=None):
    t, d = dx2.shape
    nf = wup.shape[0]
    nt = t // tm

    def body(dx2_ref, x1_ref, g_ref, u_ref, wu_ref, wd_ref, dx1_ref, du_ref, r_ref, h_ref, dxb_ref, dg_ref, acc_ref):
        i, j = pl.program_id(0), pl.program_id(1)

        @pl.when(j == 0)
        def _():
            x1 = x1_ref[...]
            h_ref[...] = (x1 * _rstd(x1) * g_ref[...]).astype(BF16)
            dxb_ref[...] = dx2_ref[...].astype(BF16)
            acc_ref[...] = jnp.zeros_like(acc_ref)

        @pl.when((i == 0) & (j == 0))
        def _():
            dg_ref[...] = jnp.zeros_like(dg_ref)

        dr = _dot_nt(dxb_ref[...], wd_ref[...])
        ru = jnp.maximum(u_ref[...], 0.0)
        du = (dr * (2.0 * ru)).astype(BF16)
        du_ref[...] = du
        r_ref[...] = (ru * ru).astype(BF16)
        acc_ref[...] += _dot_nt(du, wu_ref[...])

        @pl.when(j == nf - 1)
        def _():
            x1 = x1_ref[...]
            dx, dg = _rms_bwd(x1, _rstd(x1), g_ref[...], acc_ref[...])
            dx1_ref[...] = dx2_ref[...] + dx
            dg_ref[...] += dg

    row = lambda w: pl.BlockSpec((tm, w), lambda i, j: (i, 0))
    col = pl.BlockSpec((tm, FF_SHARD), lambda i, j: (i, j))
    return _call(
        body, name, (nt, nf),
        [row(d), row(d), pl.BlockSpec((1, d), lambda i, j: (0, 0)), col,
         pl.BlockSpec((None, d, FF_SHARD), lambda i, j: (j, 0, 0)),
         pl.BlockSpec((None, FF_SHARD, d), lambda i, j: (j, 0, 0))],
        [row(d), col, col, row(d), row(d), pl.BlockSpec((1, d), lambda i, j: (0, 0))],
        [jax.ShapeDtypeStruct((t, d), F32), jax.ShapeDtypeStruct((t, nf * FF_SHARD), BF16),
         jax.ShapeDtypeStruct((t, nf * FF_SHARD), BF16), jax.ShapeDtypeStruct((t, d), BF16),
         jax.ShapeDtypeStruct((t, d), BF16), jax.ShapeDtypeStruct((1, d), F32)],
        [pltpu.VMEM((tm, d), F32)], ("arbitrary", "arbitrary"), (dx2, x1, g2, u, wup, wdn), ride)


def _matmul_tn(a, b, bm, bn, bt, name, by_column_block=0, a_transposed=False):
    m, t = a.shape if a_transposed else a.shape[::-1]
    n = b.shape[1]
    nk = t // bt

    def body(a_ref, b_ref, o_ref, acc_ref):
        k = pl.program_id(2)

        @pl.when(k == 0)
        def _():
            acc_ref[...] = jnp.zeros_like(acc_ref)

        acc_ref[...] += _dot(a_ref[...], b_ref[...]) if a_transposed else _dot_tn(a_ref[...], b_ref[...])

        @pl.when(k == nk - 1)
        def _():
            if by_column_block:
                for c in range(bn // by_column_block):
                    o_ref[c] = acc_ref[:, c * by_column_block:(c + 1) * by_column_block].astype(BF16)
            else:
                o_ref[...] = acc_ref[...].astype(BF16)

    if by_column_block:
        out_spec = pl.BlockSpec((bn // by_column_block, bm, by_column_block), lambda i, j, k: (j, i, 0))
        out_shape = jax.ShapeDtypeStruct((n // by_column_block, m, by_column_block), BF16)
    else:
        out_spec = pl.BlockSpec((bm, bn), lambda i, j, k: (i, j))
        out_shape = jax.ShapeDtypeStruct((m, n), BF16)
    a_spec = (pl.BlockSpec((bm, bt), lambda i, j, k: (i, k)) if a_transposed
              else pl.BlockSpec((bt, bm), lambda i, j, k: (k, i)))
    return _call(body, name, (m // bm, n // bn, nk), [a_spec, pl.BlockSpec((bt, bn), lambda i, j, k: (k, j))],
                 out_spec, out_shape, [pltpu.VMEM((bm, bn), F32)], ("parallel", "parallel", "arbitrary"), (a, b))


def _outproj_bwd(dx1, wo, tm, name):
    t, d = dx1.shape
    c2 = POOL_WIDTH + SG_WIDTH

    def body(dx_ref, wo_ref, dya_ref, dyb_ref, dyc_ref, dyct_ref, dxb_ref):
        dxb = dx_ref[...].astype(BF16)
        dxb_ref[...] = dxb
        dya_ref[...] = _dot_nt(dxb, wo_ref[0:POOL_WIDTH, :])
        dyb_ref[...] = _dot_nt(dxb, wo_ref[POOL_WIDTH:c2, :])
        dyc_ref[...] = _dot_nt(dxb, wo_ref[c2:, :]).astype(BF16)
        dyct_ref[...] = _dot_nt(wo_ref[c2:, :], dxb).astype(BF16)

    row = lambda w: pl.BlockSpec((tm, w), lambda i: (i, 0))
    return pl.pallas_call(
        body, name=name, grid=(t // tm,),
        in_specs=[row(d), pl.BlockSpec((d, d), lambda i: (0, 0))],
        out_specs=[row(POOL_WIDTH), row(SG_WIDTH), row(SB_WIDTH), pl.BlockSpec((SB_WIDTH, tm), lambda i: (0, i)), row(d)],
        out_shape=[jax.ShapeDtypeStruct((t, POOL_WIDTH), F32), jax.ShapeDtypeStruct((t, SG_WIDTH), F32),
                   jax.ShapeDtypeStruct((t, SB_WIDTH), BF16), jax.ShapeDtypeStruct((SB_WIDTH, t), BF16),
                   jax.ShapeDtypeStruct((t, d), BF16)],
        compiler_params=_params(("parallel",)),
    )(dx1, wo)


def _sba_bwd(qkv, ktb, dyc, dyct, cc, nvis, tq, name, ride=None):
    t = qkv.shape[0]
    nb = t // tq
    idx = jnp.arange(tq)
    upper = (idx[None, :] > idx[:, None]).astype(BF16)
    lower = (idx[None, :] < idx[:, None]).astype(BF16)

    qb = SB_QUERY_BLOCKS_PER_STEP

    def body(q_ref, k_ref, v_ref, kt_ref, do_ref, dot_ref, c_ref, n_ref, up_ref, lo_ref, dqt_ref, dk_ref, dv_ref):
        @pl.when(pl.program_id(1) == 0)
        def _():
            dk_ref[...] = jnp.zeros_like(dk_ref)
            dv_ref[...] = jnp.zeros_like(dv_ref)

        for sub in range(qb):
            query_block(pl.program_id(1) * qb + sub, slice(sub * tq, (sub + 1) * tq), q_ref, k_ref, v_ref, kt_ref,
                        do_ref, dot_ref, c_ref, n_ref, up_ref, lo_ref, dqt_ref, dk_ref, dv_ref)

    def query_block(i, mine, q_ref, k_ref, v_ref, kt_ref, do_ref, dot_ref, c_ref, n_ref, up_ref, lo_ref,
                    dqt_ref, dk_ref, dv_ref):
        q = q_ref[mine, :]
        do = do_ref[mine, :]
        dot = dot_ref[:, mine]
        up = up_ref[...]
        lo = lo_ref[...]
        lane_head = lax.broadcasted_iota(jnp.int32, (1, 128), 1) // SB_HD
        sub_head = lax.broadcasted_iota(jnp.int32, (128, 1), 0) // SB_HD
        causal = (lax.broadcasted_iota(jnp.int32, (tq, tq), 0) < lax.broadcasted_iota(jnp.int32, (tq, tq), 1))
        hms = [lane_head == h for h in range(2)]
        qh = [jnp.where(hm, q, jnp.zeros_like(q)) for hm in hms]
        qs = [x * 0.125 for x in qh]
        doh = [jnp.where(hm, do, jnp.zeros_like(do)) for hm in hms]

        def blocks(js, carry):
            cgs, dqt = list(carry[:2]), carry[2]
            rows = [pl.ds(pl.multiple_of(j * tq, tq), tq) for j, _ in js]
            kj = [k_ref[r, :] for r in rows]
            vj = [v_ref[r, :] for r in rows]
            kt = [kt_ref[j] for j, _ in js]
            chains = [(b, h) for b in range(len(js)) for h in range(2)]
            z = [_dot_nt(kj[b], qh[h]) for b, h in chains]
            da = [_dot(jnp.where(hms[h], vj[b], jnp.zeros_like(vj[b])), dot) for b, h in chains]
            ls = [_log2_sigmoids(zz * SB_SCALE) for zz in z]
            lb = [x[0] for x in ls]
            l1 = [jnp.where(causal, x[1], 0.0) if js[b][1] else x[1] for x, (b, h) in zip(ls, chains)]
            after = [_tri_dot(up, x) for x in l1]
            a = [jnp.exp2(lb[n] + after[n] + c_ref[h, js[b][0], :, mine]) for n, (b, h) in enumerate(chains)]
            a = [jnp.where(causal, a[n], 0.0) if js[b][1] else a[n] for n, (b, h) in enumerate(chains)]
            g = [a[n] * da[n] for n in range(len(chains))]
            gloc = [_dot(lo, x.astype(BF16)) for x in g]
            dzb = []
            for n, (b, h) in enumerate(chains):
                gsum = gloc[n] + cgs[h]
                dz = g[n] - jnp.exp2(lb[n]) * (g[n] + gsum)
                dzb.append((jnp.where(causal, dz, 0.0) if js[b][1] else dz).astype(BF16))
                cgs[h] = gsum[tq - 1:tq, :] + g[n][tq - 1:tq, :]
            ab = [x.astype(BF16) for x in a]
            for n, (b, h) in enumerate(chains):
                dqt = dqt + _dot(jnp.where(sub_head == h, kt[b], jnp.zeros_like(kt[b])), dzb[n])
            for b in range(len(js)):
                dk_ref[rows[b], :] += _dot(dzb[2 * b], qs[0]) + _dot(dzb[2 * b + 1], qs[1])
                dv_ref[rows[b], :] += _dot(ab[2 * b], doh[0]) + _dot(ab[2 * b + 1], doh[1])
            return cgs[0], cgs[1], dqt

        n = jnp.clip(n_ref[pl.program_id(0), i].astype(jnp.int32), 1, i + 1)
        zero = jnp.zeros((1, tq), F32)
        carry = lax.fori_loop(jnp.minimum(i + 1 - n, i - 1), i - 1, lambda s, cr: blocks([(s, False)], cr),
                              (zero, zero, jnp.zeros((128, tq), F32)))
        carry = lax.cond(i > 0, lambda: blocks([(i - 1, False), (i, True)], carry), lambda: blocks([(i, True)], carry))
        dqt_ref[:, mine] = carry[2] * 0.125

    return _call(
        body, name, (SB_PAIRS, nb // qb),
        [pl.BlockSpec((qb * tq, 128), lambda p, i: (i, p)),
         pl.BlockSpec((t, 128), lambda p, i: (0, SB_PAIRS + p)),
         pl.BlockSpec((t, 128), lambda p, i: (0, 2 * SB_PAIRS + p)),
         pl.BlockSpec((None, nb, 128, tq), lambda p, i: (p, 0, 0, 0)),
         pl.BlockSpec((qb * tq, 128), lambda p, i: (i, p)),
         pl.BlockSpec((128, qb * tq), lambda p, i: (p, i)),
         pl.BlockSpec((2, nb, 1, qb * tq), lambda p, i: (p, 0, 0, i)),
         pl.BlockSpec(memory_space=pltpu.SMEM),
         pl.BlockSpec((tq, tq), lambda p, i: (0, 0)),
         pl.BlockSpec((tq, tq), lambda p, i: (0, 0))],
        [pl.BlockSpec((128, qb * tq), lambda p, i: (p, i)),
         pl.BlockSpec((t, 128), lambda p, i: (0, p)),
         pl.BlockSpec((t, 128), lambda p, i: (0, p))],
        [jax.ShapeDtypeStruct((SB_WIDTH, t), F32), jax.ShapeDtypeStruct((t, SB_WIDTH), F32),
         jax.ShapeDtypeStruct((t, SB_WIDTH), F32)],
        [], ("arbitrary", "arbitrary"), (qkv, qkv, qkv, ktb, dyc, dyct, cc, nvis, upper, lower), ride)


def _sg_bwd(ab, dyb, gn, wm, wmt, bfull, tm, name):
    t = ab.shape[0]
    nt = t // tm
    sel = (jnp.arange(SG_WIDTH)[:, None] // SG_HD == jnp.arange(CHUNK)[None, :]).astype(F32)

    def body(u_ref, v_ref, dy_ref, gn_ref, wm_ref, wmt_ref, b_ref, sel_ref,
             dup_ref, dvp_ref, dgn_ref, dw_ref, db_ref, dbacc_ref):
        i = pl.program_id(0)

        @pl.when(i == 0)
        def _():
            dgn_ref[...] = jnp.zeros_like(dgn_ref)
            dw_ref[...] = jnp.zeros_like(dw_ref)
            dbacc_ref[...] = jnp.zeros_like(dbacc_ref)

        tril = (lax.broadcasted_iota(jnp.int32, (CHUNK, CHUNK), 0) >= lax.broadcasted_iota(jnp.int32, (CHUNK, CHUNK), 1))
        gn_ = gn_ref[...]
        for c in range(tm // CHUNK):
            rows = slice(c * CHUNK, (c + 1) * CHUNK)
            up, vp, dy = u_ref[rows, :], v_ref[rows, :], dy_ref[rows, :]
            u, v = _gelu(up), _gelu(vp)
            r = _rstd(v)
            vn = (v * r * gn_).astype(BF16)
            sv = b_ref[...]
            for h in range(SG_HEADS):
                sv = sv + jnp.where(_head_lanes(h), _dot(wm_ref[h], vn), 0.0)
            dup_ref[rows, :] = dy * sv * _gelu_grad(up)
            dsv = dy * u
            dbacc_ref[...] += dsv
            dvn = jnp.zeros((CHUNK, SG_WIDTH), F32)
            for h in range(SG_HEADS):
                dsv_h = jnp.where(_head_lanes(h), dsv, 0.0).astype(BF16)
                dvn = dvn + _dot(wmt_ref[h], dsv_h)
                dw_ref[h] += jnp.where(tril, _dot_nt(dsv_h, vn), 0.0)
            dv, dgn = _rms_bwd(v, r, gn_, dvn)
            dgn_ref[...] += dgn
            dvp_ref[rows, :] = dv * _gelu_grad(vp)

        @pl.when(i == nt - 1)
        def _():
            db_ref[...] = jnp.dot(dbacc_ref[...], sel_ref[...], preferred_element_type=F32,
                                  precision=lax.Precision.HIGHEST)

    const = lambda shape: pl.BlockSpec(shape, lambda i: tuple(0 for _ in shape))
    return pl.pallas_call(
        body, name=name, grid=(nt,),
        in_specs=[pl.BlockSpec((tm, SG_WIDTH), lambda i: (i, 1)), pl.BlockSpec((tm, SG_WIDTH), lambda i: (i, 2)),
                  pl.BlockSpec((tm, SG_WIDTH), lambda i: (i, 0)), const((1, SG_WIDTH)),
                  const((SG_HEADS, CHUNK, CHUNK)), const((SG_HEADS, CHUNK, CHUNK)), const((CHUNK, SG_WIDTH)),
                  const((SG_WIDTH, CHUNK))],
        out_specs=[pl.BlockSpec((tm, SG_WIDTH), lambda i: (i, 0)), pl.BlockSpec((tm, SG_WIDTH), lambda i: (i, 0)),
                   const((1, SG_WIDTH)), const((SG_HEADS, CHUNK, CHUNK)), const((CHUNK, CHUNK))],
        out_shape=[jax.ShapeDtypeStruct((t, SG_WIDTH), F32), jax.ShapeDtypeStruct((t, SG_WIDTH), F32),
                   jax.ShapeDtypeStruct((1, SG_WIDTH), F32), jax.ShapeDtypeStruct((SG_HEADS, CHUNK, CHUNK), F32),
                   jax.ShapeDtypeStruct((CHUNK, CHUNK), F32)],
        scratch_shapes=[pltpu.VMEM((CHUNK, SG_WIDTH), F32)],
        compiler_params=_params(("arbitrary",)),
    )(ab, ab, dyb, gn, wm, wmt, bfull, sel)


def _pool_bwd(ab, dya, wbd, scale, tm, name):
    t = ab.shape[0]
    nt = t // tm
    hb = tm // POOL_HALO
    nh = t // POOL_HALO

    def body(cur_ref, prev_ref, dy_ref, dyn_ref, w_ref, s_ref, da_ref, dw_ref, ds_ref):
        i = pl.program_id(0)

        @pl.when(i == 0)
        def _():
            dw_ref[...] = jnp.zeros_like(dw_ref)
            ds_ref[...] = jnp.zeros_like(ds_ref)

        prev = jnp.where(i == 0, 0.0, prev_ref[...])
        d = _pool_diff(cur_ref[...], prev, i * tm).astype(BF16)
        dy = dy_ref[...]
        ds_ref[...] += jnp.sum(dy * _dot(d, w_ref[...]), axis=0, keepdims=True)
        dyn = jnp.where(i == nt - 1, 0.0, dyn_ref[...])
        dys = (jnp.concatenate([dy, dyn], axis=0) * s_ref[...]).astype(BF16)
        dw_ref[...] += _dot_tn(d, dys[:tm])
        dd = _dot_nt(dys, w_ref[...])
        fwd = _pool_window_sums(dd / _pool_count(i * tm, tm + POOL_HALO), True)
        da_ref[...] = fwd[:tm] - dd[:tm]

    return pl.pallas_call(
        body, name=name, grid=(nt,),
        in_specs=[pl.BlockSpec((tm, POOL_WIDTH), lambda i: (i, 0)),
                  pl.BlockSpec((POOL_HALO, POOL_WIDTH), lambda i: (jnp.maximum(i * hb - 1, 0), 0)),
                  pl.BlockSpec((tm, POOL_WIDTH), lambda i: (i, 0)),
                  pl.BlockSpec((POOL_HALO, POOL_WIDTH), lambda i: (jnp.minimum((i + 1) * hb, nh - 1), 0)),
                  pl.BlockSpec((POOL_WIDTH, POOL_WIDTH), lambda i: (0, 0)),
                  pl.BlockSpec((1, POOL_WIDTH), lambda i: (0, 0))],
        out_specs=[pl.BlockSpec((tm, POOL_WIDTH), lambda i: (i, 0)),
                   pl.BlockSpec((POOL_WIDTH, POOL_WIDTH), lambda i: (0, 0)),
                   pl.BlockSpec((1, POOL_WIDTH), lambda i: (0, 0))],
        out_shape=[jax.ShapeDtypeStruct((t, POOL_WIDTH), F32), jax.ShapeDtypeStruct((POOL_WIDTH, POOL_WIDTH), F32),
                   jax.ShapeDtypeStruct((1, POOL_WIDTH), F32)],
        compiler_params=_params(("arbitrary",)),
    )(ab, ab, dya, dya, wbd, scale)


def _inproj_bwd(dx1, x, g, da, dup, dvp, dqt, dk, dv, w, tm, name, ride=None):
    t, d = x.shape
    n = w.shape[1]
    nt = t // tm

    def body(dx1_ref, x_ref, g_ref, da_ref, du_ref, dv_ref, dqt_ref, dk_ref, dvv_ref, w_ref,
             dx_ref, h_ref, dp_ref, dg_ref):
        @pl.when(pl.program_id(0) == 0)
        def _():
            dg_ref[...] = jnp.zeros_like(dg_ref)

        dp = jnp.concatenate([da_ref[...], du_ref[...], dv_ref[...], dqt_ref[...].T, dk_ref[...], dvv_ref[...]],
                             axis=1).astype(BF16)
        dp_ref[...] = dp
        xx = x_ref[...]
        r = _rstd(xx)
        h_ref[...] = (xx * r * g_ref[...]).astype(BF16)
        dx, dg = _rms_bwd(xx, r, g_ref[...], _dot_nt(dp, w_ref[...]))
        dx_ref[...] = dx1_ref[...] + dx
        dg_ref[...] += dg

    row = lambda w_: pl.BlockSpec((tm, w_), lambda i: (i, 0))
    return _call(
        body, name, (nt,),
        [row(d), row(d), pl.BlockSpec((1, d), lambda i: (0, 0)), row(POOL_WIDTH), row(SG_WIDTH),
         row(SG_WIDTH), pl.BlockSpec((SB_WIDTH, tm), lambda i: (0, i)), row(SB_WIDTH), row(SB_WIDTH),
         pl.BlockSpec((d, n), lambda i: (0, 0))],
        [row(d), row(d), row(n), pl.BlockSpec((1, d), lambda i: (0, 0))],
        [jax.ShapeDtypeStruct((t, d), F32), jax.ShapeDtypeStruct((t, d), BF16),
         jax.ShapeDtypeStruct((t, n), BF16), jax.ShapeDtypeStruct((1, d), F32)],
        [], ("arbitrary",), (dx1, x, g, da, dup, dvp, dqt, dk, dv, w), ride)


def _full_w_in(gathered):
    return gathered.transpose(1, 0, 2).reshape(D_MODEL, IN_COLS)


_SMALL_SHAPES = ((D_MODEL,), (4, POOL_GW, POOL_GW), (POOL_WIDTH,), (SG_WIDTH,), (SG_HEADS, CHUNK, CHUNK),
                 (SG_HEADS, CHUNK), (D_MODEL,))
_SMALL_SIZES = tuple(functools.reduce(lambda p, q: p * q, shp) for shp in _SMALL_SHAPES)
_SMALL_ROWS = sum(_SMALL_SIZES) // 128
_NORM1_ROWS = D_MODEL // 128


def _pack_small_layer(arrs):
    return jnp.concatenate([a.reshape(-1) for a in arrs]).reshape(_SMALL_ROWS, 128)


def _pack_small(arrs, final):
    return jnp.concatenate([_pack_small_layer([a[l] for a in arrs]) for l in range(DEPTH)] + [final.reshape(-1, 128)])


def _unpack_small(buf):
    per_layer = []
    for l in range(DEPTH):
        flat, off, outs = buf[l * _SMALL_ROWS:(l + 1) * _SMALL_ROWS].reshape(-1), 0, []
        for shp, size in zip(_SMALL_SHAPES, _SMALL_SIZES):
            outs.append(flat[off:off + size].reshape(shp))
            off += size
        per_layer.append(outs)
    return [jnp.stack([per_layer[l][k] for l in range(DEPTH)]) for k in range(len(_SMALL_SHAPES))] + \
           [buf[DEPTH * _SMALL_ROWS:].reshape(-1)]


def _tiles(t):
    return min(512, t), min(256, t // 4), min(2048, t)


def _layer_fwd(xl, wi, wo, wud, small_w, l, ride_mlp=None):
    n1, pw, ps, sn, sw, sb, n2 = small_w
    tm, tq, _ = _tiles(xl.shape[0])
    wbd = jnp.zeros((4, POOL_GW, 4, POOL_GW), F32)
    for gi in range(4):
        wbd = wbd.at[gi, :, gi, :].set(pw[gi])
    wbd = wbd.reshape(POOL_WIDTH, POOL_WIDTH).astype(BF16)
    wm = (sw * jnp.tril(jnp.ones((CHUNK, CHUNK), F32))).astype(BF16)
    wmt = wm.transpose(0, 2, 1)
    bfull = jnp.repeat(sb.T, SG_HD, axis=1)
    g1, g2, psc, gn = n1[None, :], n2[None, :], ps[None, :], sn[None, :]

    if isinstance(wo, Exchange):
        (ab, qkv, ktb, vtb), (wo,) = _inproj_fwd(xl, g1, wi, tm, tq, f"inproj_fwd{l}", wo)
    else:
        ab, qkv, ktb, vtb = _inproj_fwd(xl, g1, wi, tm, tq, f"inproj_fwd{l}")
    ya = _pool_fwd(ab, wbd, psc, tm, f"pool_fwd{l}")
    yb = _sg_fwd(ab, gn, wm, bfull, tm, f"sg_fwd{l}")
    if isinstance(wud, Exchange):
        (yct, cc, nvis), (wu, wd) = _sba_fwd(qkv, vtb, tq, f"sba_fwd{l}", wud)
    else:
        (yct, cc, nvis), (wu, wd) = _sba_fwd(qkv, vtb, tq, f"sba_fwd{l}"), wud
    wo = wo.reshape(D_MODEL, D_MODEL)
    res = _outproj_mlp_fwd(xl, ya, yb, yct, wo, g2, wu, wd, min(MLP_ROWS, xl.shape[0]), f"mlp_fwd{l}", ride_mlp)
    (x1, u, x2), rode = res if ride_mlp is not None else (res, None)
    saved = dict(x=xl, ab=ab, qkv=qkv, ktb=ktb, cc=cc, nvis=nvis, ya=ya, yb=yb, yct=yct, x1=x1, u=u, wi=wi, wo=wo, wu=wu,
                 wd=wd, wbd=wbd, wm=wm, wmt=wmt, bfull=bfull, g1=g1, g2=g2, psc=psc, gn=gn)
    return x2, saved, rode


def _layer_bwd(dx, s, l, ride_mlp=None, scatter_in_attn=True, gather_small=False):
    tm, tq, tw = _tiles(dx.shape[0])
    ktb = s["ktb"]
    res = _mlp_bwd(dx, s["x1"], s["g2"], s["u"], s["wu"], s["wd"], min(MLP_ROWS, dx.shape[0]), f"mlp_bwd{l}", ride_mlp)
    (dx1, du, r, h2, dx2b, dn2), rode = res if ride_mlp is not None else (res, None)
    dw_up = _matmul_tn(h2, du, D_MODEL, 2 * FF_SHARD, tw, f"dw_up{l}", by_column_block=FF_SHARD)
    dw_down = _matmul_tn(r, dx2b, 1024, D_MODEL, tw, f"dw_down{l}").reshape(N_DEV, FF_SHARD, D_MODEL)
    dya, dyb, dyc, dyct, dx1b = _outproj_bwd(dx1, s["wo"], tm, f"outproj_bwd{l}")
    dw_out = jnp.concatenate([
        _matmul_tn(jnp.concatenate([s["ya"], s["yb"]], axis=1), dx1b, POOL_WIDTH + SG_WIDTH, D_MODEL, tw, f"dw_out_ab{l}"),
        _matmul_tn(s["yct"], dx1b, SB_WIDTH, D_MODEL, tw, f"dw_out_c{l}", a_transposed=True)]
    ).reshape(N_DEV, OUT_SHARD, D_MODEL)
    if scatter_in_attn:
        (dqt, dk, dv), (dw_out, dw_up, dw_down) = _sba_bwd(s["qkv"], ktb, dyc, dyct, s["cc"], s["nvis"], tq, f"sba_bwd{l}",
                                                           Exchange([dw_out, dw_up, dw_down], True))
    else:
        dqt, dk, dv = _sba_bwd(s["qkv"], ktb, dyc, dyct, s["cc"], s["nvis"], tq, f"sba_bwd{l}")
    dup, dvp, dgn, dwm, dbm = _sg_bwd(s["ab"], dyb, s["gn"], s["wm"], s["wmt"], s["bfull"], tm, f"sg_bwd{l}")
    da, dwbd, dpsc = _pool_bwd(s["ab"], dya, s["wbd"], s["psc"], tm, f"pool_bwd{l}")
    dpw = jnp.stack([dwbd[gi * POOL_GW:(gi + 1) * POOL_GW, gi * POOL_GW:(gi + 1) * POOL_GW] for gi in range(4)])
    small = _pack_small_layer([jnp.zeros((D_MODEL,), F32), dpw, dpsc[0], dgn[0], dwm, dbm[:, :SG_HEADS].T, dn2[0]])[_NORM1_ROWS:]
    res = _inproj_bwd(dx1, s["x"], s["g1"], da, dup, dvp, dqt, dk, dv, s["wi"], tm, f"inproj_bwd{l}",
                      Exchange([small], False) if gather_small else None)
    (dx, h1, dproj, dn1), small = (res[0], res[1][0]) if gather_small else (res, small)
    dw_in = _matmul_tn(h1, dproj, D_MODEL, IN_COLS // 3, tw, f"dw_in{l}")
    dw_in = dw_in.reshape(D_MODEL, N_DEV, IN_SHARD).transpose(1, 0, 2)
    return dx, (dw_in, dw_out, dw_up, dw_down), (dn1.reshape(_NORM1_ROWS, 128), small), rode


def kernel(x, norm1, w_in, pool_w, pool_scale, sg_norm, sg_w, sg_b, w_out, norm2, w_up, w_down, final_norm, loss_target, m_norm1, m_w_in, m_pool_w, m_pool_scale, m_sg_norm, m_sg_w, m_sg_b, m_w_out, m_norm2, m_w_up, m_w_down, m_final_norm, v_norm1, v_w_in, v_pool_w, v_pool_scale, v_sg_norm, v_sg_w, v_sg_b, v_w_out, v_norm2, v_w_up, v_w_down, v_final_norm):
    t = x.shape[1]
    tm = _tiles(t)[0]
    small_w = (norm1, pool_w, pool_scale, sg_norm, sg_w, sg_b, norm2)
    big_w = (w_in, w_out, w_up, w_down)
    big_m = (m_w_in, m_w_out, m_w_up, m_w_down)
    big_v = (v_w_in, v_w_out, v_w_up, v_w_down)
    shards = [[w[l].astype(BF16) for w in big_w] for l in range(DEPTH)]

    wi0 = _full_w_in(_exchange(shards[0][:1], False, "gather_w_in0")[0])
    x1, s0, g1 = _layer_fwd(x.reshape(t, D_MODEL), wi0, Exchange(shards[0][1:2], False), Exchange(shards[0][2:], False),
                            tuple(w[0] for w in small_w), 0, ride_mlp=Exchange(shards[1][:2], False))
    x2, s1, _ = _layer_fwd(x1, _full_w_in(g1[0]), g1[1], Exchange(shards[1][2:], False), tuple(w[1] for w in small_w), 1)
    loss_local, dx, dfinal = _loss_grad(x2, final_norm[None, :], loss_target.reshape(t, D_MODEL), tm, "loss_grad")
    loss = lax.psum(loss_local[0, 0], MESH_AXES)

    dx, parts1, small1, _ = _layer_bwd(dx, s1, 1)
    early = jnp.concatenate(list(small1) + [dfinal.reshape(-1, 128)])
    dx, parts0, (dn1, small0), (recv_in1, early) = _layer_bwd(
        dx, s0, 0, ride_mlp=Exchange([parts1[0], early], [True, False]), gather_small=True)
    grad_x = dx.reshape(x.shape)
    recv_in0, dn1 = _exchange([parts0[0], dn1], [True, False], "scatter_w_in0_gather_norm1_0")
    small_all = jnp.concatenate([dn1, small0, early], axis=1)
    received = [[recv_in0] + list(parts0[1:]), [recv_in1] + list(parts1[1:])]

    big = [None] * 4
    for l in reversed(range(DEPTH)):
        for k in range(4):
            big[k] = _reduce_adamw(received[l][k], big_w[k], big_m[k], big_v[k], l, big[k], f"adamw{k}_{l}")

    sm = _reduce_adamw(
        small_all,
        _pack_small(small_w, final_norm)[None],
        _pack_small([m_norm1, m_pool_w, m_pool_scale, m_sg_norm, m_sg_w, m_sg_b, m_norm2], m_final_norm)[None],
        _pack_small([v_norm1, v_pool_w, v_pool_scale, v_sg_norm, v_sg_w, v_sg_b, v_norm2], v_final_norm)[None],
        0, None, "adamw_replicated")

    out = [loss, grad_x]
    for k in range(4):
        n1, pw, ps, sn, sw, sb, n2, fn = _unpack_small(sm[k][0])
        out += [n1, big[0][k], pw, ps, sn, sw, sb, big[1][k], n2, big[2][k], big[3][k], fn]
    return tuple(out)
```

```python
import functools

import jax
import jax.numpy as jnp
from jax import lax
from jax.experimental import pallas as pl
from jax.experimental.pallas import tpu as pltpu

F32 = jnp.float32
BF16 = jnp.bfloat16

D_MODEL = 1024
DEPTH = 2
POOL_WIDTH = 256
SG_WIDTH = 256
SB_WIDTH = 512
POOL_WINDOWS = (2, 4, 8, 16)
POOL_GW = 64
POOL_HALO = 16
CHUNK = 128
SG_HEADS = 4
SG_HD = 64
SB_HD = 64
SB_PAIRS = SB_WIDTH // 128
AB_COLS = POOL_WIDTH + 2 * SG_WIDTH
IN_COLS = AB_COLS + 3 * SB_WIDTH
D_FF = 4096
EPS = 1e-6
N_DEV = 8
FF_SHARD = D_FF // N_DEV
IN_SHARD = IN_COLS // N_DEV
OUT_SHARD = D_MODEL // N_DEV
ADAM_LR = 0.001
ADAM_B1 = 0.9
ADAM_B2 = 0.999
ADAM_EPS = 1e-08
ADAM_WD = 0.01
ADAM_STEP = 10
VMEM_LIMIT = 56 * 1024 * 1024
MLP_ROWS = 1024
MESH_AXES = ("x", "y", "c")


def _dot(a, b):
    return jnp.dot(a, b, preferred_element_type=F32)


def _dot_nt(a, b):
    return lax.dot_general(a, b, (((1,), (1,)), ((), ())), preferred_element_type=F32)


def _dot_tn(a, b):
    return lax.dot_general(a, b, (((0,), (0,)), ((), ())), preferred_element_type=F32)


def _rstd(x):
    return lax.rsqrt(jnp.mean(x * x, axis=-1, keepdims=True) + EPS)


def _rms_bwd(x, r, g, dh):
    gq = dh * g
    dx = r * gq - x * (r * r * r) * jnp.mean(gq * x, axis=-1, keepdims=True)
    dg = jnp.sum(dh * x * r, axis=0, keepdims=True)
    return dx, dg


def _params(sem=None):
    kw = dict(vmem_limit_bytes=VMEM_LIMIT)
    if sem is not None:
        kw["dimension_semantics"] = sem
    return pltpu.CompilerParams(**kw)


def _row_tile(rows, cap):
    best = 8
    for t in range(8, min(rows, cap) + 1, 8):
        if rows % t == 0:
            best = t
    return best


def _peer(k):
    x, y, c = lax.axis_index("x"), lax.axis_index("y"), lax.axis_index("c")
    return (1 - x if k & 4 else x, 1 - y if k & 2 else y, 1 - c if k & 1 else c)


def _my_index():
    return 4 * lax.axis_index("x") + 2 * lax.axis_index("y") + lax.axis_index("c")


class Exchange:
    def __init__(self, arrs, scatter):
        self.arrs = list(arrs)
        self.scatter = list(scatter) if isinstance(scatter, (list, tuple)) else [scatter] * len(self.arrs)
        self.n = len(self.arrs)
        self.any_specs = [pl.BlockSpec(memory_space=pl.ANY)] * self.n
        self.out_shape = [jax.ShapeDtypeStruct((N_DEV,) + a.shape[-2:], a.dtype) for a in self.arrs]
        self.sems = [pltpu.SemaphoreType.DMA((self.n, N_DEV - 1)), pltpu.SemaphoreType.DMA((self.n, N_DEV - 1)),
                     pltpu.SemaphoreType.DMA((self.n,))]

    def _copies(self, ins, outs, sems):
        send_sems, recv_sems, local_sems = sems
        me = _my_index()
        local, remote = [], []
        for a in range(self.n):
            sc = self.scatter[a]
            local.append(pltpu.make_async_copy(ins[a].at[me] if sc else ins[a], outs[a].at[me], local_sems.at[a]))
            for k in range(1, N_DEV):
                px, py, pc = _peer(k)
                remote.append(pltpu.make_async_remote_copy(
                    src_ref=ins[a].at[4 * px + 2 * py + pc] if sc else ins[a], dst_ref=outs[a].at[me],
                    send_sem=send_sems.at[a, k - 1], recv_sem=recv_sems.at[a, k - 1],
                    device_id=(px, py, pc), device_id_type=pl.DeviceIdType.MESH))
        return local, remote

    def start(self, ins, outs, sems):
        local, remote = self._copies(ins, outs, sems)
        for cp in local + remote:
            cp.start()

    def wait(self, ins, outs, sems):
        local, remote = self._copies(ins, outs, sems)
        for cp in remote:
            cp.wait_recv()
        for cp in remote:
            cp.wait_send()
        for cp in local:
            cp.wait()

    def alone(self, name):
        n = self.n

        def body(*refs):
            ins, outs, sems = refs[:n], refs[n:2 * n], refs[2 * n:]
            self.start(ins, outs, sems)
            self.wait(ins, outs, sems)

        return pl.pallas_call(body, name=name, out_shape=self.out_shape, in_specs=self.any_specs,
                              out_specs=self.any_specs, scratch_shapes=self.sems)(*self.arrs)


def _exchange(arrs, scatter, name):
    return Exchange(arrs, scatter).alone(name)


def _gather_via_sibling(shard, name):
    r, c = shard.shape

    def body(x_ref, out_ref, send_sems, recv_sems, local_sem):
        x, y, core = lax.axis_index("x"), lax.axis_index("y"), lax.axis_index("c")
        me, sibling = (x, y, core), (x, y, 1 - core)
        chips = [(1 - x, y), (x, 1 - y), (1 - x, 1 - y)]

        def slot(px, py, pc):
            return out_ref.at[4 * px + 2 * py + pc]

        def copy(k, block, to, src=None):
            return pltpu.make_async_remote_copy(
                src_ref=slot(*block) if src is None else src, dst_ref=slot(*block),
                send_sem=send_sems.at[k], recv_sem=recv_sems.at[k], device_id=to, device_id_type=pl.DeviceIdType.MESH)

        mine = pltpu.make_async_copy(x_ref, slot(*me), local_sem)
        mine.start()
        first = [copy(0, me, sibling, src=x_ref)] + [copy(1 + j, me, (*chip, core), src=x_ref) for j, chip in enumerate(chips)]
        for cp in first:
            cp.start()
        passed = [copy(4 + j, (*chip, core), sibling) for j, chip in enumerate(chips)]
        for j, chip in enumerate(chips):
            copy(1 + j, (*chip, core), me).wait_recv()
            passed[j].start()
        copy(0, sibling, me).wait_recv()
        for j, chip in enumerate(chips):
            copy(4 + j, (*chip, 1 - core), me).wait_recv()
        for cp in first + passed:
            cp.wait_send()
        mine.wait()

    return pl.pallas_call(
        body, name=name, out_shape=jax.ShapeDtypeStruct((N_DEV, r, c), shard.dtype),
        in_specs=[pl.BlockSpec(memory_space=pltpu.VMEM)], out_specs=pl.BlockSpec(memory_space=pltpu.VMEM),
        scratch_shapes=[pltpu.SemaphoreType.DMA((N_DEV - 1,)), pltpu.SemaphoreType.DMA((N_DEV - 1,)),
                        pltpu.SemaphoreType.DMA],
        compiler_params=pltpu.CompilerParams(vmem_limit_bytes=VMEM_LIMIT),
    )(shard)


def _call(body, name, grid, in_specs, out_specs, out_shape, scratch_shapes, semantics, args, ride=None):
    if ride is None:
        return pl.pallas_call(body, name=name, grid=grid, in_specs=in_specs, out_specs=out_specs, out_shape=out_shape,
                              scratch_shapes=scratch_shapes, compiler_params=_params(semantics))(*args)
    single = not isinstance(out_shape, (list, tuple))
    out_specs, out_shape = ([out_specs], [out_shape]) if single else (list(out_specs), list(out_shape))
    n_in, n_out, n_scr, n = len(in_specs), len(out_specs), len(scratch_shapes), ride.n

    def riding(*refs):
        ins, cins = refs[:n_in], refs[n_in:n_in + n]
        outs, couts = refs[n_in + n:n_in + n + n_out], refs[n_in + n + n_out:n_in + 2 * n + n_out]
        scr, sems = refs[n_in + 2 * n + n_out:n_in + 2 * n + n_out + n_scr], refs[n_in + 2 * n + n_out + n_scr:]
        first = functools.reduce(lambda p, q: p & q, [pl.program_id(a) == 0 for a in range(len(grid))])
        last = functools.reduce(lambda p, q: p & q, [pl.program_id(a) == g - 1 for a, g in enumerate(grid)])

        @pl.when(first)
        def _():
            ride.start(cins, couts, sems)

        body(*ins, *outs, *scr)

        @pl.when(last)
        def _():
            ride.wait(cins, couts, sems)

    res = pl.pallas_call(
        riding, name=name, grid=grid, in_specs=list(in_specs) + ride.any_specs, out_specs=out_specs + ride.any_specs,
        out_shape=out_shape + ride.out_shape, scratch_shapes=list(scratch_shapes) + ride.sems,
        compiler_params=_params(("arbitrary",) * len(grid)))(*args, *ride.arrs)
    own = res[0] if single else list(res[:n_out])
    return own, list(res[n_out:])


def _reduce_adamw(parts, w, m, v, l, prev, name):
    _, rows, n = parts.shape
    tr = _row_tile(rows, max(8, (1 << 18) // n))
    c1 = 1.0 - ADAM_B1 ** ADAM_STEP
    c2 = 1.0 - ADAM_B2 ** ADAM_STEP

    def body(p_ref, w_ref, m_ref, v_ref, *rest):
        g_ref, d_ref, nm_ref, nv_ref = rest[-4:]
        g = p_ref[0].astype(F32)
        for s in range(1, N_DEV):
            g = g + p_ref[s].astype(F32)
        nm = ADAM_B1 * m_ref[...] + (1.0 - ADAM_B1) * g
        nv = ADAM_B2 * v_ref[...] + (1.0 - ADAM_B2) * (g * g)
        m_hat = nm / c1
        v_hat = nv / c2
        g_ref[...] = g
        d_ref[...] = -ADAM_LR * (m_hat / (jnp.sqrt(v_hat) + ADAM_EPS) + ADAM_WD * w_ref[...])
        nm_ref[...] = nm
        nv_ref[...] = nv

    blk = pl.BlockSpec((None, tr, n), lambda i: (l, i, 0))
    out = jax.ShapeDtypeStruct(w.shape, F32)
    prev = list(prev) if prev is not None else []
    return pl.pallas_call(
        body, name=name, grid=(rows // tr,),
        in_specs=[pl.BlockSpec((N_DEV, tr, n), lambda i: (0, i, 0)), blk, blk, blk] + [pl.BlockSpec(memory_space=pl.ANY)] * len(prev),
        out_specs=[blk, blk, blk, blk], out_shape=[out, out, out, out],
        input_output_aliases={4 + k: k for k in range(len(prev))},
        compiler_params=_params(("parallel",)),
    )(parts, w, m, v, *prev)


def _inproj_fwd(x, g, w, tm, tq, name, ride=None):
    t, d = x.shape
    n = w.shape[1]
    nb = t // tq
    per = tm // tq

    def body(x_ref, g_ref, w_ref, ab_ref, qkv_ref, kt_ref, vt_ref):
        xx = x_ref[...]
        h = (xx * _rstd(xx) * g_ref[...]).astype(BF16)
        ab_ref[...] = _dot(h, w_ref[:, :AB_COLS])
        qkv = _dot(h, w_ref[:, AB_COLS:])
        qkv_ref[...] = qkv.astype(BF16)
        for which, out_ref in ((1, kt_ref), (2, vt_ref)):
            for p in range(SB_PAIRS):
                for b in range(per):
                    cols = which * SB_WIDTH + p * 128
                    out_ref[p, b] = qkv[b * tq:(b + 1) * tq, cols:cols + 128].T.astype(BF16)

    tb = pl.BlockSpec((SB_PAIRS, per, 128, tq), lambda i: (0, i, 0, 0))
    tshape = jax.ShapeDtypeStruct((SB_PAIRS, nb, 128, tq), BF16)
    return _call(
        body, name, (t // tm,),
        [pl.BlockSpec((tm, d), lambda i: (i, 0)), pl.BlockSpec((1, d), lambda i: (0, 0)),
         pl.BlockSpec((d, n), lambda i: (0, 0))],
        [pl.BlockSpec((tm, AB_COLS), lambda i: (i, 0)), pl.BlockSpec((tm, n - AB_COLS), lambda i: (i, 0)), tb, tb],
        [jax.ShapeDtypeStruct((t, AB_COLS), F32), jax.ShapeDtypeStruct((t, n - AB_COLS), BF16), tshape, tshape],
        [], ("parallel",), (x, g, w), ride)


def _pool_window_sums(xx, forward):
    n = xx.shape[0]
    sh = (lambda k: n - k) if forward else (lambda k: k)
    s2 = xx + pltpu.roll(xx, sh(1), 0)
    s4 = s2 + pltpu.roll(s2, sh(2), 0)
    s8 = s4 + pltpu.roll(s4, sh(4), 0)
    s16 = s8 + pltpu.roll(s8, sh(8), 0)
    grp = lax.broadcasted_iota(jnp.int32, (1, POOL_WIDTH), 1) // POOL_GW
    return jnp.where(grp == 0, s2, jnp.where(grp == 1, s4, jnp.where(grp == 2, s8, s16)))


def _pool_count(t0, rows):
    grp = lax.broadcasted_iota(jnp.int32, (1, POOL_WIDTH), 1) // POOL_GW
    win = jnp.where(grp == 0, 2, jnp.where(grp == 1, 4, jnp.where(grp == 2, 8, 16)))
    tt = t0 + lax.broadcasted_iota(jnp.int32, (rows, 1), 0)
    return jnp.minimum(tt + 1, win).astype(F32)


def _pool_diff(cur, prev, t0):
    tm = cur.shape[0]
    sums = _pool_window_sums(jnp.concatenate([prev, cur], axis=0), False)[POOL_HALO:]
    return sums / _pool_count(t0, tm) - cur


def _pool_fwd(ab, wbd, scale, tm, name):
    t = ab.shape[0]
    hb = tm // POOL_HALO

    def body(cur_ref, prev_ref, w_ref, s_ref, y_ref):
        i = pl.program_id(0)
        prev = jnp.where(i == 0, 0.0, prev_ref[...])
        d = _pool_diff(cur_ref[...], prev, i * tm).astype(BF16)
        y_ref[...] = (_dot(d, w_ref[...]) * s_ref[...]).astype(BF16)

    return pl.pallas_call(
        body, name=name, grid=(t // tm,),
        in_specs=[pl.BlockSpec((tm, POOL_WIDTH), lambda i: (i, 0)),
                  pl.BlockSpec((POOL_HALO, POOL_WIDTH), lambda i: (jnp.maximum(i * hb - 1, 0), 0)),
                  pl.BlockSpec((POOL_WIDTH, POOL_WIDTH), lambda i: (0, 0)),
                  pl.BlockSpec((1, POOL_WIDTH), lambda i: (0, 0))],
        out_specs=pl.BlockSpec((tm, POOL_WIDTH), lambda i: (i, 0)),
        out_shape=jax.ShapeDtypeStruct((t, POOL_WIDTH), BF16),
        compiler_params=_params(("parallel",)),
    )(ab, ab, wbd, scale)


_GELU_K = 0.7978845608028654
_GELU_A = 0.044715


def _gelu(x):
    return 0.5 * x * (1.0 + jnp.tanh(_GELU_K * (x + _GELU_A * (x * x * x))))


def _gelu_grad(x):
    th = jnp.tanh(_GELU_K * (x + _GELU_A * (x * x * x)))
    return 0.5 * (1.0 + th) + 0.5 * x * (1.0 - th * th) * (_GELU_K * (1.0 + 3.0 * _GELU_A * (x * x)))


def _head_lanes(h):
    return lax.broadcasted_iota(jnp.int32, (1, SG_WIDTH), 1) // SG_HD == h


def _sg_fwd(ab, gn, wm, bfull, tm, name):
    t = ab.shape[0]

    def body(u_ref, v_ref, gn_ref, wm_ref, b_ref, y_ref):
        v = _gelu(v_ref[...])
        vn = (v * _rstd(v) * gn_ref[...]).astype(BF16)
        for c in range(tm // CHUNK):
            rows = slice(c * CHUNK, (c + 1) * CHUNK)
            vc = vn[rows]
            sv = b_ref[...]
            for h in range(SG_HEADS):
                sv = sv + jnp.where(_head_lanes(h), _dot(wm_ref[h], vc), 0.0)
            y_ref[rows, :] = (_gelu(u_ref[rows, :]) * sv).astype(BF16)

    return pl.pallas_call(
        body, name=name, grid=(t // tm,),
        in_specs=[pl.BlockSpec((tm, SG_WIDTH), lambda i: (i, 1)), pl.BlockSpec((tm, SG_WIDTH), lambda i: (i, 2)),
                  pl.BlockSpec((1, SG_WIDTH), lambda i: (0, 0)),
                  pl.BlockSpec((SG_HEADS, CHUNK, CHUNK), lambda i: (0, 0, 0)),
                  pl.BlockSpec((CHUNK, SG_WIDTH), lambda i: (0, 0))],
        out_specs=pl.BlockSpec((tm, SG_WIDTH), lambda i: (i, 0)),
        out_shape=jax.ShapeDtypeStruct((t, SG_WIDTH), BF16),
        compiler_params=_params(("parallel",)),
    )(ab, ab, gn, wm, bfull)


LOG2E = 1.4426950408889634
SB_SCALE = 0.125 * LOG2E
SB_DEAD_LOG2 = 152.0
SB_QUERY_BLOCKS_PER_STEP = 2


def _log2_sigmoids(y):
    neg_abs = lax.bitcast_convert_type(lax.bitcast_convert_type(y, jnp.uint32) | jnp.uint32(0x80000000), F32)
    lb = jnp.minimum(y, 0.0) - jnp.log(1.0 + jnp.exp2(neg_abs)) * LOG2E
    return lb, lb - y


def _split(x):
    hi = x.astype(BF16)
    return hi, (x - hi.astype(F32)).astype(BF16)


def _tri_dot(tri, x):
    hi, lo = _split(x)
    return _dot(tri, hi) + _dot(tri, lo)


def _sba_fwd(qkv, vtb, tq, name, ride=None):
    t = qkv.shape[0]
    nb = t // tq
    qb = SB_QUERY_BLOCKS_PER_STEP
    upper = (jnp.arange(tq)[None, :] > jnp.arange(tq)[:, None]).astype(BF16)

    def body(q_ref, k_ref, vt_ref, up_ref, ot_ref, c_ref, n_ref):
        for sub in range(qb):
            query_block(pl.program_id(1) * qb + sub, slice(sub * tq, (sub + 1) * tq), q_ref, k_ref, vt_ref, up_ref,
                        ot_ref, c_ref, n_ref)

    def query_block(i, mine, q_ref, k_ref, vt_ref, up_ref, ot_ref, c_ref, n_ref):
        q = q_ref[mine, :]
        up = up_ref[...]
        lane_head = lax.broadcasted_iota(jnp.int32, (1, 128), 1) // SB_HD
        sub_head = lax.broadcasted_iota(jnp.int32, (128, 1), 0) // SB_HD
        causal = (lax.broadcasted_iota(jnp.int32, (tq, tq), 0) < lax.broadcasted_iota(jnp.int32, (tq, tq), 1))
        qh = [jnp.where(lane_head == h, q, jnp.zeros_like(q)) for h in range(2)]

        def blocks(js, carry):
            cs, acc = list(carry[:2]), carry[2]
            kj = [k_ref[pl.ds(pl.multiple_of(j * tq, tq), tq), :] for j, _ in js]
            vt = [vt_ref[j] for j, _ in js]
            chains = [(b, h) for b in range(len(js)) for h in range(2)]
            z = [_dot_nt(kj[b], qh[h]) for b, h in chains]
            ls = [_log2_sigmoids(zz * SB_SCALE) for zz in z]
            lb = [x[0] for x in ls]
            l1 = [jnp.where(causal, x[1], 0.0) if js[b][1] else x[1] for x, (b, h) in zip(ls, chains)]
            after = [_tri_dot(up, x) for x in l1]
            a = []
            for n, (b, h) in enumerate(chains):
                c_ref[h, js[b][0], :, mine] = cs[h]
                an = jnp.exp2(lb[n] + after[n] + cs[h])
                a.append(jnp.where(causal, an, 0.0) if js[b][1] else an)
                cs[h] = cs[h] + after[n][0:1, :] + l1[n][0:1, :]
            alive = (jnp.max(jnp.maximum(cs[0], cs[1])) > -SB_DEAD_LOG2).astype(jnp.int32)
            for n, (b, h) in enumerate(chains):
                acc = acc + _dot(jnp.where(sub_head == h, vt[b], jnp.zeros_like(vt[b])), a[n].astype(BF16))
            return alive, cs[0], cs[1], acc

        def step(state):
            s, _, c0, c1, acc = state
            return (s + 1,) + blocks([(i - 1 - s, False)], (c0, c1, acc))

        zero = jnp.zeros((1, tq), F32)
        start = (zero, zero, jnp.zeros((128, tq), F32))
        joint = lax.cond(i > 0, lambda: blocks([(i, True), (i - 1, False)], start), lambda: blocks([(i, True)], start))
        state = lax.while_loop(lambda st: (st[0] < i) & (st[1] > 0), step, (jnp.minimum(i, 1),) + tuple(joint))
        ot_ref[:, mine] = state[4].astype(BF16)
        n_ref[pl.program_id(0), i] = (state[0] + 1).astype(F32)

    return _call(
        body, name, (SB_PAIRS, nb // qb),
        [pl.BlockSpec((qb * tq, 128), lambda p, i: (i, p)),
         pl.BlockSpec((t, 128), lambda p, i: (0, SB_PAIRS + p)),
         pl.BlockSpec((None, nb, 128, tq), lambda p, i: (p, 0, 0, 0)),
         pl.BlockSpec((tq, tq), lambda p, i: (0, 0))],
        [pl.BlockSpec((128, qb * tq), lambda p, i: (p, i)),
         pl.BlockSpec((2, nb, 1, qb * tq), lambda p, i: (p, 0, 0, i)),
         pl.BlockSpec(memory_space=pltpu.SMEM)],
        [jax.ShapeDtypeStruct((SB_WIDTH, t), BF16),
         jax.ShapeDtypeStruct((2 * SB_PAIRS, nb, 1, t), F32),
         jax.ShapeDtypeStruct((SB_PAIRS, nb), F32)],
        [], ("arbitrary", "arbitrary"), (qkv, qkv, vtb, upper), ride)


def _outproj_mlp_fwd(x, ya, yb, yct, wo, g2, wup, wdn, tm, name, ride=None):
    t, d = x.shape
    nf = wup.shape[0]

    def body(x_ref, ya_ref, yb_ref, yct_ref, wo_ref, g_ref, wu_ref, wd_ref, x1_ref, u_ref, x2_ref, h_ref, acc_ref):
        j = pl.program_id(1)

        @pl.when(j == 0)
        def _():
            x1 = (x_ref[...] + _dot(ya_ref[...], wo_ref[0:POOL_WIDTH, :])
                  + _dot(yb_ref[...], wo_ref[POOL_WIDTH:POOL_WIDTH + SG_WIDTH, :])
                  + _dot_tn(yct_ref[...], wo_ref[POOL_WIDTH + SG_WIDTH:, :]))
            x1_ref[...] = x1
            h_ref[...] = (x1 * _rstd(x1) * g_ref[...]).astype(BF16)
            acc_ref[...] = x1

        u = _dot(h_ref[...], wu_ref[...])
        u_ref[...] = u
        r = jnp.maximum(u, 0.0)
        acc_ref[...] += _dot((r * r).astype(BF16), wd_ref[...])

        @pl.when(j == nf - 1)
        def _():
            x2_ref[...] = acc_ref[...]

    row = lambda w: pl.BlockSpec((tm, w), lambda i, j: (i, 0))
    return _call(
        body, name, (t // tm, nf),
        [row(d), row(POOL_WIDTH), row(SG_WIDTH), pl.BlockSpec((SB_WIDTH, tm), lambda i, j: (0, i)),
         pl.BlockSpec((d, d), lambda i, j: (0, 0)), pl.BlockSpec((1, d), lambda i, j: (0, 0)),
         pl.BlockSpec((None, d, FF_SHARD), lambda i, j: (j, 0, 0)),
         pl.BlockSpec((None, FF_SHARD, d), lambda i, j: (j, 0, 0))],
        [row(d), pl.BlockSpec((tm, FF_SHARD), lambda i, j: (i, j)), row(d)],
        [jax.ShapeDtypeStruct((t, d), F32), jax.ShapeDtypeStruct((t, nf * FF_SHARD), F32),
         jax.ShapeDtypeStruct((t, d), F32)],
        [pltpu.VMEM((tm, d), BF16), pltpu.VMEM((tm, d), F32)],
        ("parallel", "arbitrary"), (x, ya, yb, yct, wo, g2, wup, wdn), ride)


def _loss_grad(x, g, target, tm, name):
    t, d = x.shape
    nt = t // tm

    def body(x_ref, g_ref, t_ref, loss_ref, dx_ref, dg_ref, sq_ref):
        i = pl.program_id(0)

        @pl.when(i == 0)
        def _():
            sq_ref[...] = jnp.zeros_like(sq_ref)
            dg_ref[...] = jnp.zeros_like(dg_ref)

        xx = x_ref[...]
        r = _rstd(xx)
        err = xx * r * g_ref[...] - t_ref[...]
        sq_ref[...] += jnp.sum(err * err, axis=0, keepdims=True)
        dx, dg = _rms_bwd(xx, r, g_ref[...], err * (1.0 / d))
        dx_ref[...] = dx
        dg_ref[...] += dg

        @pl.when(i == nt - 1)
        def _():
            loss_ref[...] = jnp.sum(sq_ref[...], axis=1, keepdims=True) * (0.5 / d)

    return pl.pallas_call(
        body, name=name, grid=(nt,),
        in_specs=[pl.BlockSpec((tm, d), lambda i: (i, 0)), pl.BlockSpec((1, d), lambda i: (0, 0)),
                  pl.BlockSpec((tm, d), lambda i: (i, 0))],
        out_specs=[pl.BlockSpec((1, 1), lambda i: (0, 0)), pl.BlockSpec((tm, d), lambda i: (i, 0)),
                   pl.BlockSpec((1, d), lambda i: (0, 0))],
        out_shape=[jax.ShapeDtypeStruct((1, 1), F32), jax.ShapeDtypeStruct((t, d), F32),
                   jax.ShapeDtypeStruct((1, d), F32)],
        scratch_shapes=[pltpu.VMEM((1, d), F32)],
        compiler_params=_params(("arbitrary",)),
    )(x, g, target)


def _mlp_bwd(dx2, x1, g2, u, wup, wdn, tm, name, ride=None):
    t, d = dx2.shape
    nf = wup.shape[0]
    nt = t // tm

    def body(dx2_ref, x1_ref, g_ref, u_ref, wu_ref, wd_ref, dx1_ref, du_ref, r_ref, h_ref, dxb_ref, dg_ref, acc_ref):
        i, j = pl.program_id(0), pl.program_id(1)

        @pl.when(j == 0)
        def _():
            x1 = x1_ref[...]
            h_ref[...] = (x1 * _rstd(x1) * g_ref[...]).astype(BF16)
            dxb_ref[...] = dx2_ref[...].astype(BF16)
            acc_ref[...] = jnp.zeros_like(acc_ref)

        @pl.when((i == 0) & (j == 0))
        def _():
            dg_ref[...] = jnp.zeros_like(dg_ref)

        dr = _dot_nt(dxb_ref[...], wd_ref[...])
        ru = jnp.maximum(u_ref[...], 0.0)
        du = (dr * (2.0 * ru)).astype(BF16)
        du_ref[...] = du
        r_ref[...] = (ru * ru).astype(BF16)
        acc_ref[...] += _dot_nt(du, wu_ref[...])

        @pl.when(j == nf - 1)
        def _():
            x1 = x1_ref[...]
            dx, dg = _rms_bwd(x1, _rstd(x1), g_ref[...], acc_ref[...])
            dx1_ref[...] = dx2_ref[...] + dx
            dg_ref[...] += dg

    row = lambda w: pl.BlockSpec((tm, w), lambda i, j: (i, 0))
    col = pl.BlockSpec((tm, FF_SHARD), lambda i, j: (i, j))
    return _call(
        body, name, (nt, nf),
        [row(d), row(d), pl.BlockSpec((1, d), lambda i, j: (0, 0)), col,
         pl.BlockSpec((None, d, FF_SHARD), lambda i, j: (j, 0, 0)),
         pl.BlockSpec((None, FF_SHARD, d), lambda i, j: (j, 0, 0))],
        [row(d), col, col, row(d), row(d), pl.BlockSpec((1, d), lambda i, j: (0, 0))],
        [jax.ShapeDtypeStruct((t, d), F32), jax.ShapeDtypeStruct((t, nf * FF_SHARD), BF16),
         jax.ShapeDtypeStruct((t, nf * FF_SHARD), BF16), jax.ShapeDtypeStruct((t, d), BF16),
         jax.ShapeDtypeStruct((t, d), BF16), jax.ShapeDtypeStruct((1, d), F32)],
        [pltpu.VMEM((tm, d), F32)], ("arbitrary", "arbitrary"), (dx2, x1, g2, u, wup, wdn), ride)


def _matmul_tn(a, b, bm, bn, bt, name, by_column_block=0, a_transposed=False):
    m, t = a.shape if a_transposed else a.shape[::-1]
    n = b.shape[1]
    nk = t // bt

    def body(a_ref, b_ref, o_ref, acc_ref):
        k = pl.program_id(2)

        @pl.when(k == 0)
        def _():
            acc_ref[...] = jnp.zeros_like(acc_ref)

        acc_ref[...] += _dot(a_ref[...], b_ref[...]) if a_transposed else _dot_tn(a_ref[...], b_ref[...])

        @pl.when(k == nk - 1)
        def _():
            if by_column_block:
                for c in range(bn // by_column_block):
                    o_ref[c] = acc_ref[:, c * by_column_block:(c + 1) * by_column_block].astype(BF16)
            else:
                o_ref[...] = acc_ref[...].astype(BF16)

    if by_column_block:
        out_spec = pl.BlockSpec((bn // by_column_block, bm, by_column_block), lambda i, j, k: (j, i, 0))
        out_shape = jax.ShapeDtypeStruct((n // by_column_block, m, by_column_block), BF16)
    else:
        out_spec = pl.BlockSpec((bm, bn), lambda i, j, k: (i, j))
        out_shape = jax.ShapeDtypeStruct((m, n), BF16)
    a_spec = (pl.BlockSpec((bm, bt), lambda i, j, k: (i, k)) if a_transposed
              else pl.BlockSpec((bt, bm), lambda i, j, k: (k, i)))
    return _call(body, name, (m // bm, n // bn, nk), [a_spec, pl.BlockSpec((bt, bn), lambda i, j, k: (k, j))],
                 out_spec, out_shape, [pltpu.VMEM((bm, bn), F32)], ("parallel", "parallel", "arbitrary"), (a, b))


def _outproj_bwd(dx1, wo, tm, name):
    t, d = dx1.shape
    c2 = POOL_WIDTH + SG_WIDTH

    def body(dx_ref, wo_ref, dya_ref, dyb_ref, dyc_ref, dyct_ref, dxb_ref):
        dxb = dx_ref[...].astype(BF16)
        dxb_ref[...] = dxb
        dya_ref[...] = _dot_nt(dxb, wo_ref[0:POOL_WIDTH, :])
        dyb_ref[...] = _dot_nt(dxb, wo_ref[POOL_WIDTH:c2, :])
        dyc_ref[...] = _dot_nt(dxb, wo_ref[c2:, :]).astype(BF16)
        dyct_ref[...] = _dot_nt(wo_ref[c2:, :], dxb).astype(BF16)

    row = lambda w: pl.BlockSpec((tm, w), lambda i: (i, 0))
    return pl.pallas_call(
        body, name=name, grid=(t // tm,),
        in_specs=[row(d), pl.BlockSpec((d, d), lambda i: (0, 0))],
        out_specs=[row(POOL_WIDTH), row(SG_WIDTH), row(SB_WIDTH), pl.BlockSpec((SB_WIDTH, tm), lambda i: (0, i)), row(d)],
        out_shape=[jax.ShapeDtypeStruct((t, POOL_WIDTH), F32), jax.ShapeDtypeStruct((t, SG_WIDTH), F32),
                   jax.ShapeDtypeStruct((t, SB_WIDTH), BF16), jax.ShapeDtypeStruct((SB_WIDTH, t), BF16),
                   jax.ShapeDtypeStruct((t, d), BF16)],
        compiler_params=_params(("parallel",)),
    )(dx1, wo)


def _sba_bwd(qkv, ktb, dyc, dyct, cc, nvis, tq, name, ride=None):
    t = qkv.shape[0]
    nb = t // tq
    idx = jnp.arange(tq)
    upper = (idx[None, :] > idx[:, None]).astype(BF16)
    lower = (idx[None, :] < idx[:, None]).astype(BF16)

    qb = SB_QUERY_BLOCKS_PER_STEP

    def body(q_ref, k_ref, v_ref, kt_ref, do_ref, dot_ref, c_ref, n_ref, up_ref, lo_ref, dqt_ref, dk_ref, dv_ref):
        @pl.when(pl.program_id(1) == 0)
        def _():
            dk_ref[...] = jnp.zeros_like(dk_ref)
            dv_ref[...] = jnp.zeros_like(dv_ref)

        for sub in range(qb):
            query_block(pl.program_id(1) * qb + sub, slice(sub * tq, (sub + 1) * tq), q_ref, k_ref, v_ref, kt_ref,
                        do_ref, dot_ref, c_ref, n_ref, up_ref, lo_ref, dqt_ref, dk_ref, dv_ref)

    def query_block(i, mine, q_ref, k_ref, v_ref, kt_ref, do_ref, dot_ref, c_ref, n_ref, up_ref, lo_ref,
                    dqt_ref, dk_ref, dv_ref):
        q = q_ref[mine, :]
        do = do_ref[mine, :]
        dot = dot_ref[:, mine]
        up = up_ref[...]
        lo = lo_ref[...]
        lane_head = lax.broadcasted_iota(jnp.int32, (1, 128), 1) // SB_HD
        sub_head = lax.broadcasted_iota(jnp.int32, (128, 1), 0) // SB_HD
        causal = (lax.broadcasted_iota(jnp.int32, (tq, tq), 0) < lax.broadcasted_iota(jnp.int32, (tq, tq), 1))
        hms = [lane_head == h for h in range(2)]
        qh = [jnp.where(hm, q, jnp.zeros_like(q)) for hm in hms]
        qs = [x * 0.125 for x in qh]
        doh = [jnp.where(hm, do, jnp.zeros_like(do)) for hm in hms]

        def blocks(js, carry):
            cgs, dqt = list(carry[:2]), carry[2]
            rows = [pl.ds(pl.multiple_of(j * tq, tq), tq) for j, _ in js]
            kj = [k_ref[r, :] for r in rows]
            vj = [v_ref[r, :] for r in rows]
            kt = [kt_ref[j] for j, _ in js]
            chains = [(b, h) for b in range(len(js)) for h in range(2)]
            z = [_dot_nt(kj[b], qh[h]) for b, h in chains]
            da = [_dot(jnp.where(hms[h], vj[b], jnp.zeros_like(vj[b])), dot) for b, h in chains]
            ls = [_log2_sigmoids(zz * SB_SCALE) for zz in z]
            lb = [x[0] for x in ls]
            l1 = [jnp.where(causal, x[1], 0.0) if js[b][1] else x[1] for x, (b, h) in zip(ls, chains)]
            after = [_tri_dot(up, x) for x in l1]
            a = [jnp.exp2(lb[n] + after[n] + c_ref[h, js[b][0], :, mine]) for n, (b, h) in enumerate(chains)]
            a = [jnp.where(causal, a[n], 0.0) if js[b][1] else a[n] for n, (b, h) in enumerate(chains)]
            g = [a[n] * da[n] for n in range(len(chains))]
            gloc = [_dot(lo, x.astype(BF16)) for x in g]
            dzb = []
            for n, (b, h) in enumerate(chains):
                gsum = gloc[n] + cgs[h]
                dz = g[n] - jnp.exp2(lb[n]) * (g[n] + gsum)
                dzb.append((jnp.where(causal, dz, 0.0) if js[b][1] else dz).astype(BF16))
                cgs[h] = gsum[tq - 1:tq, :] + g[n][tq - 1:tq, :]
            ab = [x.astype(BF16) for x in a]
            for n, (b, h) in enumerate(chains):
                dqt = dqt + _dot(jnp.where(sub_head == h, kt[b], jnp.zeros_like(kt[b])), dzb[n])
            for b in range(len(js)):
                dk_ref[rows[b], :] += _dot(dzb[2 * b], qs[0]) + _dot(dzb[2 * b + 1], qs[1])
                dv_ref[rows[b], :] += _dot(ab[2 * b], doh[0]) + _dot(ab[2 * b + 1], doh[1])
            return cgs[0], cgs[1], dqt

        n = jnp.clip(n_ref[pl.program_id(0), i].astype(jnp.int32), 1, i + 1)
        zero = jnp.zeros((1, tq), F32)
        carry = lax.fori_loop(jnp.minimum(i + 1 - n, i - 1), i - 1, lambda s, cr: blocks([(s, False)], cr),
                              (zero, zero, jnp.zeros((128, tq), F32)))
        carry = lax.cond(i > 0, lambda: blocks([(i - 1, False), (i, True)], carry), lambda: blocks([(i, True)], carry))
        dqt_ref[:, mine] = carry[2] * 0.125

    return _call(
        body, name, (SB_PAIRS, nb // qb),
        [pl.BlockSpec((qb * tq, 128), lambda p, i: (i, p)),
         pl.BlockSpec((t, 128), lambda p, i: (0, SB_PAIRS + p)),
         pl.BlockSpec((t, 128), lambda p, i: (0, 2 * SB_PAIRS + p)),
         pl.BlockSpec((None, nb, 128, tq), lambda p, i: (p, 0, 0, 0)),
         pl.BlockSpec((qb * tq, 128), lambda p, i: (i, p)),
         pl.BlockSpec((128, qb * tq), lambda p, i: (p, i)),
         pl.BlockSpec((2, nb, 1, qb * tq), lambda p, i: (p, 0, 0, i)),
         pl.BlockSpec(memory_space=pltpu.SMEM),
         pl.BlockSpec((tq, tq), lambda p, i: (0, 0)),
         pl.BlockSpec((tq, tq), lambda p, i: (0, 0))],
        [pl.BlockSpec((128, qb * tq), lambda p, i: (p, i)),
         pl.BlockSpec((t, 128), lambda p, i: (0, p)),
         pl.BlockSpec((t, 128), lambda p, i: (0, p))],
        [jax.ShapeDtypeStruct((SB_WIDTH, t), F32), jax.ShapeDtypeStruct((t, SB_WIDTH), F32),
         jax.ShapeDtypeStruct((t, SB_WIDTH), F32)],
        [], ("arbitrary", "arbitrary"), (qkv, qkv, qkv, ktb, dyc, dyct, cc, nvis, upper, lower), ride)


def _sg_bwd(ab, dyb, gn, wm, wmt, bfull, tm, name):
    t = ab.shape[0]
    nt = t // tm
    sel = (jnp.arange(SG_WIDTH)[:, None] // SG_HD == jnp.arange(CHUNK)[None, :]).astype(F32)

    def body(u_ref, v_ref, dy_ref, gn_ref, wm_ref, wmt_ref, b_ref, sel_ref,
             dup_ref, dvp_ref, dgn_ref, dw_ref, db_ref, dbacc_ref):
        i = pl.program_id(0)

        @pl.when(i == 0)
        def _():
            dgn_ref[...] = jnp.zeros_like(dgn_ref)
            dw_ref[...] = jnp.zeros_like(dw_ref)
            dbacc_ref[...] = jnp.zeros_like(dbacc_ref)

        tril = (lax.broadcasted_iota(jnp.int32, (CHUNK, CHUNK), 0) >= lax.broadcasted_iota(jnp.int32, (CHUNK, CHUNK), 1))
        gn_ = gn_ref[...]
        for c in range(tm // CHUNK):
            rows = slice(c * CHUNK, (c + 1) * CHUNK)
            up, vp, dy = u_ref[rows, :], v_ref[rows, :], dy_ref[rows, :]
            u, v = _gelu(up), _gelu(vp)
            r = _rstd(v)
            vn = (v * r * gn_).astype(BF16)
            sv = b_ref[...]
            for h in range(SG_HEADS):
                sv = sv + jnp.where(_head_lanes(h), _dot(wm_ref[h], vn), 0.0)
            dup_ref[rows, :] = dy * sv * _gelu_grad(up)
            dsv = dy * u
            dbacc_ref[...] += dsv
            dvn = jnp.zeros((CHUNK, SG_WIDTH), F32)
            for h in range(SG_HEADS):
                dsv_h = jnp.where(_head_lanes(h), dsv, 0.0).astype(BF16)
                dvn = dvn + _dot(wmt_ref[h], dsv_h)
                dw_ref[h] += jnp.where(tril, _dot_nt(dsv_h, vn), 0.0)
            dv, dgn = _rms_bwd(v, r, gn_, dvn)
            dgn_ref[...] += dgn
            dvp_ref[rows, :] = dv * _gelu_grad(vp)

        @pl.when(i == nt - 1)
        def _():
            db_ref[...] = jnp.dot(dbacc_ref[...], sel_ref[...], preferred_element_type=F32,
                                  precision=lax.Precision.HIGHEST)

    const = lambda shape: pl.BlockSpec(shape, lambda i: tuple(0 for _ in shape))
    return pl.pallas_call(
        body, name=name, grid=(nt,),
        in_specs=[pl.BlockSpec((tm, SG_WIDTH), lambda i: (i, 1)), pl.BlockSpec((tm, SG_WIDTH), lambda i: (i, 2)),
                  pl.BlockSpec((tm, SG_WIDTH), lambda i: (i, 0)), const((1, SG_WIDTH)),
                  const((SG_HEADS, CHUNK, CHUNK)), const((SG_HEADS, CHUNK, CHUNK)), const((CHUNK, SG_WIDTH)),
                  const((SG_WIDTH, CHUNK))],
        out_specs=[pl.BlockSpec((tm, SG_WIDTH), lambda i: (i, 0)), pl.BlockSpec((tm, SG_WIDTH), lambda i: (i, 0)),
                   const((1, SG_WIDTH)), const((SG_HEADS, CHUNK, CHUNK)), const((CHUNK, CHUNK))],
        out_shape=[jax.ShapeDtypeStruct((t, SG_WIDTH), F32), jax.ShapeDtypeStruct((t, SG_WIDTH), F32),
                   jax.ShapeDtypeStruct((1, SG_WIDTH), F32), jax.ShapeDtypeStruct((SG_HEADS, CHUNK, CHUNK), F32),
                   jax.ShapeDtypeStruct((CHUNK, CHUNK), F32)],
        scratch_shapes=[pltpu.VMEM((CHUNK, SG_WIDTH), F32)],
        compiler_params=_params(("arbitrary",)),
    )(ab, ab, dyb, gn, wm, wmt, bfull, sel)


def _pool_bwd(ab, dya, wbd, scale, tm, name):
    t = ab.shape[0]
    nt = t // tm
    hb = tm // POOL_HALO
    nh = t // POOL_HALO

    def body(cur_ref, prev_ref, dy_ref, dyn_ref, w_ref, s_ref, da_ref, dw_ref, ds_ref):
        i = pl.program_id(0)

        @pl.when(i == 0)
        def _():
            dw_ref[...] = jnp.zeros_like(dw_ref)
            ds_ref[...] = jnp.zeros_like(ds_ref)

        prev = jnp.where(i == 0, 0.0, prev_ref[...])
        d = _pool_diff(cur_ref[...], prev, i * tm).astype(BF16)
        dy = dy_ref[...]
        ds_ref[...] += jnp.sum(dy * _dot(d, w_ref[...]), axis=0, keepdims=True)
        dyn = jnp.where(i == nt - 1, 0.0, dyn_ref[...])
        dys = (jnp.concatenate([dy, dyn], axis=0) * s_ref[...]).astype(BF16)
        dw_ref[...] += _dot_tn(d, dys[:tm])
        dd = _dot_nt(dys, w_ref[...])
        fwd = _pool_window_sums(dd / _pool_count(i * tm, tm + POOL_HALO), True)
        da_ref[...] = fwd[:tm] - dd[:tm]

    return pl.pallas_call(
        body, name=name, grid=(nt,),
        in_specs=[pl.BlockSpec((tm, POOL_WIDTH), lambda i: (i, 0)),
                  pl.BlockSpec((POOL_HALO, POOL_WIDTH), lambda i: (jnp.maximum(i * hb - 1, 0), 0)),
                  pl.BlockSpec((tm, POOL_WIDTH), lambda i: (i, 0)),
                  pl.BlockSpec((POOL_HALO, POOL_WIDTH), lambda i: (jnp.minimum((i + 1) * hb, nh - 1), 0)),
                  pl.BlockSpec((POOL_WIDTH, POOL_WIDTH), lambda i: (0, 0)),
                  pl.BlockSpec((1, POOL_WIDTH), lambda i: (0, 0))],
        out_specs=[pl.BlockSpec((tm, POOL_WIDTH), lambda i: (i, 0)),
                   pl.BlockSpec((POOL_WIDTH, POOL_WIDTH), lambda i: (0, 0)),
                   pl.BlockSpec((1, POOL_WIDTH), lambda i: (0, 0))],
        out_shape=[jax.ShapeDtypeStruct((t, POOL_WIDTH), F32), jax.ShapeDtypeStruct((POOL_WIDTH, POOL_WIDTH), F32),
                   jax.ShapeDtypeStruct((1, POOL_WIDTH), F32)],
        compiler_params=_params(("arbitrary",)),
    )(ab, ab, dya, dya, wbd, scale)


def _inproj_bwd(dx1, x, g, da, dup, dvp, dqt, dk, dv, w, tm, name, ride=None):
    t, d = x.shape
    n = w.shape[1]
    nt = t // tm

    def body(dx1_ref, x_ref, g_ref, da_ref, du_ref, dv_ref, dqt_ref, dk_ref, dvv_ref, w_ref,
             dx_ref, h_ref, dp_ref, dg_ref):
        @pl.when(pl.program_id(0) == 0)
        def _():
            dg_ref[...] = jnp.zeros_like(dg_ref)

        dp = jnp.concatenate([da_ref[...], du_ref[...], dv_ref[...], dqt_ref[...].T, dk_ref[...], dvv_ref[...]],
                             axis=1).astype(BF16)
        dp_ref[...] = dp
        xx = x_ref[...]
        r = _rstd(xx)
        h_ref[...] = (xx * r * g_ref[...]).astype(BF16)
        dx, dg = _rms_bwd(xx, r, g_ref[...], _dot_nt(dp, w_ref[...]))
        dx_ref[...] = dx1_ref[...] + dx
        dg_ref[...] += dg

    row = lambda w_: pl.BlockSpec((tm, w_), lambda i: (i, 0))
    return _call(
        body, name, (nt,),
        [row(d), row(d), pl.BlockSpec((1, d), lambda i: (0, 0)), row(POOL_WIDTH), row(SG_WIDTH),
         row(SG_WIDTH), pl.BlockSpec((SB_WIDTH, tm), lambda i: (0, i)), row(SB_WIDTH), row(SB_WIDTH),
         pl.BlockSpec((d, n), lambda i: (0, 0))],
        [row(d), row(d), row(n), pl.BlockSpec((1, d), lambda i: (0, 0))],
        [jax.ShapeDtypeStruct((t, d), F32), jax.ShapeDtypeStruct((t, d), BF16),
         jax.ShapeDtypeStruct((t, n), BF16), jax.ShapeDtypeStruct((1, d), F32)],
        [], ("arbitrary",), (dx1, x, g, da, dup, dvp, dqt, dk, dv, w), ride)


def _full_w_in(gathered):
    return gathered.transpose(1, 0, 2).reshape(D_MODEL, IN_COLS)


_SMALL_SHAPES = ((D_MODEL,), (4, POOL_GW, POOL_GW), (POOL_WIDTH,), (SG_WIDTH,), (SG_HEADS, CHUNK, CHUNK),
                 (SG_HEADS, CHUNK), (D_MODEL,))
_SMALL_SIZES = tuple(functools.reduce(lambda p, q: p * q, shp) for shp in _SMALL_SHAPES)
_SMALL_ROWS = sum(_SMALL_SIZES) // 128
_NORM1_ROWS = D_MODEL // 128


def _pack_small_layer(arrs):
    return jnp.concatenate([a.reshape(-1) for a in arrs]).reshape(_SMALL_ROWS, 128)


def _pack_small(arrs, final):
    return jnp.concatenate([_pack_small_layer([a[l] for a in arrs]) for l in range(DEPTH)] + [final.reshape(-1, 128)])


def _unpack_small(buf):
    per_layer = []
    for l in range(DEPTH):
        flat, off, outs = buf[l * _SMALL_ROWS:(l + 1) * _SMALL_ROWS].reshape(-1), 0, []
        for shp, size in zip(_SMALL_SHAPES, _SMALL_SIZES):
            outs.append(flat[off:off + size].reshape(shp))
            off += size
        per_layer.append(outs)
    return [jnp.stack([per_layer[l][k] for l in range(DEPTH)]) for k in range(len(_SMALL_SHAPES))] + \
           [buf[DEPTH * _SMALL_ROWS:].reshape(-1)]


def _tiles(t):
    return min(512, t), min(256, t // 4), min(2048, t)


def _layer_fwd(xl, wi, wo, wud, small_w, l, ride_mlp=None):
    n1, pw, ps, sn, sw, sb, n2 = small_w
    tm, tq, _ = _tiles(xl.shape[0])
    wbd = jnp.zeros((4, POOL_GW, 4, POOL_GW), F32)
    for gi in range(4):
        wbd = wbd.at[gi, :, gi, :].set(pw[gi])
    wbd = wbd.reshape(POOL_WIDTH, POOL_WIDTH).astype(BF16)
    wm = (sw * jnp.tril(jnp.ones((CHUNK, CHUNK), F32))).astype(BF16)
    wmt = wm.transpose(0, 2, 1)
    bfull = jnp.repeat(sb.T, SG_HD, axis=1)
    g1, g2, psc, gn = n1[None, :], n2[None, :], ps[None, :], sn[None, :]

    if isinstance(wo, Exchange):
        (ab, qkv, ktb, vtb), (wo,) = _inproj_fwd(xl, g1, wi, tm, tq, f"inproj_fwd{l}", wo)
    else:
        ab, qkv, ktb, vtb = _inproj_fwd(xl, g1, wi, tm, tq, f"inproj_fwd{l}")
    ya = _pool_fwd(ab, wbd, psc, tm, f"pool_fwd{l}")
    yb = _sg_fwd(ab, gn, wm, bfull, tm, f"sg_fwd{l}")
    if isinstance(wud, Exchange):
        (yct, cc, nvis), (wu, wd) = _sba_fwd(qkv, vtb, tq, f"sba_fwd{l}", wud)
    else:
        (yct, cc, nvis), (wu, wd) = _sba_fwd(qkv, vtb, tq, f"sba_fwd{l}"), wud
    wo = wo.reshape(D_MODEL, D_MODEL)
    res = _outproj_mlp_fwd(xl, ya, yb, yct, wo, g2, wu, wd, min(MLP_ROWS, xl.shape[0]), f"mlp_fwd{l}", ride_mlp)
    (x1, u, x2), rode = res if ride_mlp is not None else (res, None)
    saved = dict(x=xl, ab=ab, qkv=qkv, ktb=ktb, cc=cc, nvis=nvis, ya=ya, yb=yb, yct=yct, x1=x1, u=u, wi=wi, wo=wo, wu=wu,
                 wd=wd, wbd=wbd, wm=wm, wmt=wmt, bfull=bfull, g1=g1, g2=g2, psc=psc, gn=gn)
    return x2, saved, rode


def _layer_bwd(dx, s, l, ride_mlp=None, scatter_in_attn=True, gather_small=False):
    tm, tq, tw = _tiles(dx.shape[0])
    ktb = s["ktb"]
    res = _mlp_bwd(dx, s["x1"], s["g2"], s["u"], s["wu"], s["wd"], min(MLP_ROWS, dx.shape[0]), f"mlp_bwd{l}", ride_mlp)
    (dx1, du, r, h2, dx2b, dn2), rode = res if ride_mlp is not None else (res, None)
    dw_up = _matmul_tn(h2, du, D_MODEL, 2 * FF_SHARD, tw, f"dw_up{l}", by_column_block=FF_SHARD)
    dw_down = _matmul_tn(r, dx2b, 1024, D_MODEL, tw, f"dw_down{l}").reshape(N_DEV, FF_SHARD, D_MODEL)
    dya, dyb, dyc, dyct, dx1b = _outproj_bwd(dx1, s["wo"], tm, f"outproj_bwd{l}")
    dw_out = jnp.concatenate([
        _matmul_tn(jnp.concatenate([s["ya"], s["yb"]], axis=1), dx1b, POOL_WIDTH + SG_WIDTH, D_MODEL, tw, f"dw_out_ab{l}"),
        _matmul_tn(s["yct"], dx1b, SB_WIDTH, D_MODEL, tw, f"dw_out_c{l}", a_transposed=True)]
    ).reshape(N_DEV, OUT_SHARD, D_MODEL)
    if scatter_in_attn:
        (dqt, dk, dv), (dw_out, dw_up, dw_down) = _sba_bwd(s["qkv"], ktb, dyc, dyct, s["cc"], s["nvis"], tq, f"sba_bwd{l}",
                                                           Exchange([dw_out, dw_up, dw_down], True))
    else:
        dqt, dk, dv = _sba_bwd(s["qkv"], ktb, dyc, dyct, s["cc"], s["nvis"], tq, f"sba_bwd{l}")
    dup, dvp, dgn, dwm, dbm = _sg_bwd(s["ab"], dyb, s["gn"], s["wm"], s["wmt"], s["bfull"], tm, f"sg_bwd{l}")
    da, dwbd, dpsc = _pool_bwd(s["ab"], dya, s["wbd"], s["psc"], tm, f"pool_bwd{l}")
    dpw = jnp.stack([dwbd[gi * POOL_GW:(gi + 1) * POOL_GW, gi * POOL_GW:(gi + 1) * POOL_GW] for gi in range(4)])
    small = _pack_small_layer([jnp.zeros((D_MODEL,), F32), dpw, dpsc[0], dgn[0], dwm, dbm[:, :SG_HEADS].T, dn2[0]])[_NORM1_ROWS:]
    res = _inproj_bwd(dx1, s["x"], s["g1"], da, dup, dvp, dqt, dk, dv, s["wi"], tm, f"inproj_bwd{l}",
                      Exchange([small], False) if gather_small else None)
    (dx, h1, dproj, dn1), small = (res[0], res[1][0]) if gather_small else (res, small)
    dw_in = _matmul_tn(h1, dproj, D_MODEL, IN_COLS // 3, tw, f"dw_in{l}")
    dw_in = dw_in.reshape(D_MODEL, N_DEV, IN_SHARD).transpose(1, 0, 2)
    return dx, (dw_in, dw_out, dw_up, dw_down), (dn1.reshape(_NORM1_ROWS, 128), small), rode


def kernel(x, norm1, w_in, pool_w, pool_scale, sg_norm, sg_w, sg_b, w_out, norm2, w_up, w_down, final_norm, loss_target, m_norm1, m_w_in, m_pool_w, m_pool_scale, m_sg_norm, m_sg_w, m_sg_b, m_w_out, m_norm2, m_w_up, m_w_down, m_final_norm, v_norm1, v_w_in, v_pool_w, v_pool_scale, v_sg_norm, v_sg_w, v_sg_b, v_w_out, v_norm2, v_w_up, v_w_down, v_final_norm):
    t = x.shape[1]
    tm = _tiles(t)[0]
    small_w = (norm1, pool_w, pool_scale, sg_norm, sg_w, sg_b, norm2)
    big_w = (w_in, w_out, w_up, w_down)
    big_m = (m_w_in, m_w_out, m_w_up, m_w_down)
    big_v = (v_w_in, v_w_out, v_w_up, v_w_down)
    shards = [[w[l].astype(BF16) for w in big_w] for l in range(DEPTH)]

    wi0 = _full_w_in(_gather_via_sibling(shards[0][0], "gather_w_in0"))
    x1, s0, g1 = _layer_fwd(x.reshape(t, D_MODEL), wi0, Exchange(shards[0][1:2], False), Exchange(shards[0][2:], False),
                            tuple(w[0] for w in small_w), 0, ride_mlp=Exchange(shards[1][:2], False))
    x2, s1, _ = _layer_fwd(x1, _full_w_in(g1[0]), g1[1], Exchange(shards[1][2:], False), tuple(w[1] for w in small_w), 1)
    loss_local, dx, dfinal = _loss_grad(x2, final_norm[None, :], loss_target.reshape(t, D_MODEL), tm, "loss_grad")
    loss = lax.psum(loss_local[0, 0], MESH_AXES)

    dx, parts1, small1, _ = _layer_bwd(dx, s1, 1)
    early = jnp.concatenate(list(small1) + [dfinal.reshape(-1, 128)])
    dx, parts0, (dn1, small0), (recv_in1, early) = _layer_bwd(
        dx, s0, 0, ride_mlp=Exchange([parts1[0], early], [True, False]), gather_small=True)
    grad_x = dx.reshape(x.shape)
    recv_in0, dn1 = _exchange([parts0[0], dn1], [True, False], "scatter_w_in0_gather_norm1_0")
    small_all = jnp.concatenate([dn1, small0, early], axis=1)
    received = [[recv_in0] + list(parts0[1:]), [recv_in1] + list(parts1[1:])]

    big = [None] * 4
    for l in reversed(range(DEPTH)):
        for k in range(4):
            big[k] = _reduce_adamw(received[l][k], big_w[k], big_m[k], big_v[k], l, big[k], f"adamw{k}_{l}")

    sm = _reduce_adamw(
        small_all,
        _pack_small(small_w, final_norm)[None],
        _pack_small([m_norm1, m_pool_w, m_pool_scale, m_sg_norm, m_sg_w, m_sg_b, m_norm2], m_final_norm)[None],
        _pack_small([v_norm1, v_pool_w, v_pool_scale, v_sg_norm, v_sg_w, v_sg_b, v_norm2], v_final_norm)[None],
        0, None, "adamw_replicated")

    out = [loss, grad_x]
    for k in range(4):
        n1, pw, ps, sn, sw, sb, n2, fn = _unpack_small(sm[k][0])
        out += [n1, big[0][k], pw, ps, sn, sw, sb, big[1][k], n2, big[2][k], big[3][k], fn]
    return tuple(out)
```

```python
import functools

import jax
import jax.numpy as jnp
from jax import lax
from jax.experimental import pallas as pl
from jax.experimental.pallas import tpu as pltpu

F32 = jnp.float32
BF16 = jnp.bfloat16

D_MODEL = 1024
DEPTH = 2
POOL_WIDTH = 256
SG_WIDTH = 256
SB_WIDTH = 512
POOL_WINDOWS = (2, 4, 8, 16)
POOL_GW = 64
POOL_HALO = 16
CHUNK = 128
SG_HEADS = 4
SG_HD = 64
SB_HD = 64
SB_PAIRS = SB_WIDTH // 128
AB_COLS = POOL_WIDTH + 2 * SG_WIDTH
IN_COLS = AB_COLS + 3 * SB_WIDTH
D_FF = 4096
EPS = 1e-6
N_DEV = 8
FF_SHARD = D_FF // N_DEV
IN_SHARD = IN_COLS // N_DEV
OUT_SHARD = D_MODEL // N_DEV
ADAM_LR = 0.001
ADAM_B1 = 0.9
ADAM_B2 = 0.999
ADAM_EPS = 1e-08
ADAM_WD = 0.01
ADAM_STEP = 10
VMEM_LIMIT = 56 * 1024 * 1024
MLP_ROWS = 1024
MESH_AXES = ("x", "y", "c")


def _dot(a, b):
    return jnp.dot(a, b, preferred_element_type=F32)


def _dot_nt(a, b):
    return lax.dot_general(a, b, (((1,), (1,)), ((), ())), preferred_element_type=F32)


def _dot_tn(a, b):
    return lax.dot_general(a, b, (((0,), (0,)), ((), ())), preferred_element_type=F32)


def _rstd(x):
    return lax.rsqrt(jnp.mean(x * x, axis=-1, keepdims=True) + EPS)


def _rms_bwd(x, r, g, dh):
    gq = dh * g
    dx = r * gq - x * (r * r * r) * jnp.mean(gq * x, axis=-1, keepdims=True)
    dg = jnp.sum(dh * x * r, axis=0, keepdims=True)
    return dx, dg


def _params(sem=None):
    kw = dict(vmem_limit_bytes=VMEM_LIMIT)
    if sem is not None:
        kw["dimension_semantics"] = sem
    return pltpu.CompilerParams(**kw)


def _row_tile(rows, cap):
    best = 8
    for t in range(8, min(rows, cap) + 1, 8):
        if rows % t == 0:
            best = t
    return best


def _peer(k):
    x, y, c = lax.axis_index("x"), lax.axis_index("y"), lax.axis_index("c")
    return (1 - x if k & 4 else x, 1 - y if k & 2 else y, 1 - c if k & 1 else c)


def _my_index():
    return 4 * lax.axis_index("x") + 2 * lax.axis_index("y") + lax.axis_index("c")


class Exchange:
    def __init__(self, arrs, scatter):
        self.arrs = list(arrs)
        self.scatter = list(scatter) if isinstance(scatter, (list, tuple)) else [scatter] * len(self.arrs)
        self.n = len(self.arrs)
        self.any_specs = [pl.BlockSpec(memory_space=pl.ANY)] * self.n
        self.out_shape = [jax.ShapeDtypeStruct((N_DEV,) + a.shape[-2:], a.dtype) for a in self.arrs]
        self.sems = [pltpu.SemaphoreType.DMA((self.n, N_DEV - 1)), pltpu.SemaphoreType.DMA((self.n, N_DEV - 1)),
                     pltpu.SemaphoreType.DMA((self.n,))]

    def _copies(self, ins, outs, sems):
        send_sems, recv_sems, local_sems = sems
        me = _my_index()
        local, remote = [], []
        for a in range(self.n):
            sc = self.scatter[a]
            local.append(pltpu.make_async_copy(ins[a].at[me] if sc else ins[a], outs[a].at[me], local_sems.at[a]))
            for k in range(1, N_DEV):
                px, py, pc = _peer(k)
                remote.append(pltpu.make_async_remote_copy(
                    src_ref=ins[a].at[4 * px + 2 * py + pc] if sc else ins[a], dst_ref=outs[a].at[me],
                    send_sem=send_sems.at[a, k - 1], recv_sem=recv_sems.at[a, k - 1],
                    device_id=(px, py, pc), device_id_type=pl.DeviceIdType.MESH))
        return local, remote

    def start(self, ins, outs, sems):
        local, remote = self._copies(ins, outs, sems)
        for cp in local + remote:
            cp.start()

    def wait(self, ins, outs, sems):
        local, remote = self._copies(ins, outs, sems)
        for cp in remote:
            cp.wait_recv()
        for cp in remote:
            cp.wait_send()
        for cp in local:
            cp.wait()

    def alone(self, name):
        n = self.n

        def body(*refs):
            ins, outs, sems = refs[:n], refs[n:2 * n], refs[2 * n:]
            self.start(ins, outs, sems)
            self.wait(ins, outs, sems)

        return pl.pallas_call(body, name=name, out_shape=self.out_shape, in_specs=self.any_specs,
                              out_specs=self.any_specs, scratch_shapes=self.sems)(*self.arrs)


def _exchange(arrs, scatter, name):
    return Exchange(arrs, scatter).alone(name)


def _gather_via_sibling(shard, name):
    r, c = shard.shape

    def body(x_ref, out_ref, send_sems, recv_sems, local_sem):
        x, y, core = lax.axis_index("x"), lax.axis_index("y"), lax.axis_index("c")
        me, sibling = (x, y, core), (x, y, 1 - core)
        chips = [(1 - x, y), (x, 1 - y), (1 - x, 1 - y)]

        def slot(px, py, pc):
            return out_ref.at[4 * px + 2 * py + pc]

        def copy(k, block, to, src=None):
            return pltpu.make_async_remote_copy(
                src_ref=slot(*block) if src is None else src, dst_ref=slot(*block),
                send_sem=send_sems.at[k], recv_sem=recv_sems.at[k], device_id=to, device_id_type=pl.DeviceIdType.MESH)

        mine = pltpu.make_async_copy(x_ref, slot(*me), local_sem)
        mine.start()
        first = [copy(0, me, sibling, src=x_ref)] + [copy(1 + j, me, (*chip, core), src=x_ref) for j, chip in enumerate(chips)]
        for cp in first:
            cp.start()
        passed = [copy(4 + j, (*chip, core), sibling) for j, chip in enumerate(chips)]
        for j, chip in enumerate(chips):
            copy(1 + j, (*chip, core), me).wait_recv()
            passed[j].start()
        copy(0, sibling, me).wait_recv()
        for j, chip in enumerate(chips):
            copy(4 + j, (*chip, 1 - core), me).wait_recv()
        for cp in first + passed:
            cp.wait_send()
        mine.wait()

    return pl.pallas_call(
        body, name=name, out_shape=jax.ShapeDtypeStruct((N_DEV, r, c), shard.dtype),
        in_specs=[pl.BlockSpec(memory_space=pltpu.VMEM)], out_specs=pl.BlockSpec(memory_space=pltpu.VMEM),
        scratch_shapes=[pltpu.SemaphoreType.DMA((N_DEV - 1,)), pltpu.SemaphoreType.DMA((N_DEV - 1,)),
                        pltpu.SemaphoreType.DMA],
        compiler_params=pltpu.CompilerParams(vmem_limit_bytes=VMEM_LIMIT),
    )(shard)


def _call(body, name, grid, in_specs, out_specs, out_shape, scratch_shapes, semantics, args, ride=None):
    if ride is None:
        return pl.pallas_call(body, name=name, grid=grid, in_specs=in_specs, out_specs=out_specs, out_shape=out_shape,
                              scratch_shapes=scratch_shapes, compiler_params=_params(semantics))(*args)
    single = not isinstance(out_shape, (list, tuple))
    out_specs, out_shape = ([out_specs], [out_shape]) if single else (list(out_specs), list(out_shape))
    n_in, n_out, n_scr, n = len(in_specs), len(out_specs), len(scratch_shapes), ride.n

    def riding(*refs):
        ins, cins = refs[:n_in], refs[n_in:n_in + n]
        outs, couts = refs[n_in + n:n_in + n + n_out], refs[n_in + n + n_out:n_in + 2 * n + n_out]
        scr, sems = refs[n_in + 2 * n + n_out:n_in + 2 * n + n_out + n_scr], refs[n_in + 2 * n + n_out + n_scr:]
        first = functools.reduce(lambda p, q: p & q, [pl.program_id(a) == 0 for a in range(len(grid))])
        last = functools.reduce(lambda p, q: p & q, [pl.program_id(a) == g - 1 for a, g in enumerate(grid)])

        @pl.when(first)
        def _():
            ride.start(cins, couts, sems)

        body(*ins, *outs, *scr)

        @pl.when(last)
        def _():
            ride.wait(cins, couts, sems)

    res = pl.pallas_call(
        riding, name=name, grid=grid, in_specs=list(in_specs) + ride.any_specs, out_specs=out_specs + ride.any_specs,
        out_shape=out_shape + ride.out_shape, scratch_shapes=list(scratch_shapes) + ride.sems,
        compiler_params=_params(("arbitrary",) * len(grid)))(*args, *ride.arrs)
    own = res[0] if single else list(res[:n_out])
    return own, list(res[n_out:])


def _reduce_adamw(parts, w, m, v, l, prev, name):
    _, rows, n = parts.shape
    tr = _row_tile(rows, max(8, (1 << 18) // n))
    c1 = 1.0 - ADAM_B1 ** ADAM_STEP
    c2 = 1.0 - ADAM_B2 ** ADAM_STEP

    def body(p_ref, w_ref, m_ref, v_ref, *rest):
        g_ref, d_ref, nm_ref, nv_ref = rest[-4:]
        g = p_ref[0].astype(F32)
        for s in range(1, N_DEV):
            g = g + p_ref[s].astype(F32)
        nm = ADAM_B1 * m_ref[...] + (1.0 - ADAM_B1) * g
        nv = ADAM_B2 * v_ref[...] + (1.0 - ADAM_B2) * (g * g)
        m_hat = nm / c1
        v_hat = nv / c2
        g_ref[...] = g
        d_ref[...] = -ADAM_LR * (m_hat / (jnp.sqrt(v_hat) + ADAM_EPS) + ADAM_WD * w_ref[...])
        nm_ref[...] = nm
        nv_ref[...] = nv

    blk = pl.BlockSpec((None, tr, n), lambda i: (l, i, 0))
    out = jax.ShapeDtypeStruct(w.shape, F32)
    prev = list(prev) if prev is not None else []
    return pl.pallas_call(
        body, name=name, grid=(rows // tr,),
        in_specs=[pl.BlockSpec((N_DEV, tr, n), lambda i: (0, i, 0)), blk, blk, blk] + [pl.BlockSpec(memory_space=pl.ANY)] * len(prev),
        out_specs=[blk, blk, blk, blk], out_shape=[out, out, out, out],
        input_output_aliases={4 + k: k for k in range(len(prev))},
        compiler_params=_params(("parallel",)),
    )(parts, w, m, v, *prev)


def _inproj_fwd(x, g, w, tm, tq, name, ride=None):
    t, d = x.shape
    n = w.shape[1]
    nb = t // tq
    per = tm // tq

    def body(x_ref, g_ref, w_ref, ab_ref, qkv_ref, kt_ref, vt_ref):
        xx = x_ref[...]
        h = (xx * _rstd(xx) * g_ref[...]).astype(BF16)
        ab_ref[...] = _dot(h, w_ref[:, :AB_COLS])
        qkv = _dot(h, w_ref[:, AB_COLS:])
        qkv_ref[...] = qkv.astype(BF16)
        for which, out_ref in ((1, kt_ref), (2, vt_ref)):
            for p in range(SB_PAIRS):
                for b in range(per):
                    cols = which * SB_WIDTH + p * 128
                    out_ref[p, b] = qkv[b * tq:(b + 1) * tq, cols:cols + 128].T.astype(BF16)

    tb = pl.BlockSpec((SB_PAIRS, per, 128, tq), lambda i: (0, i, 0, 0))
    tshape = jax.ShapeDtypeStruct((SB_PAIRS, nb, 128, tq), BF16)
    return _call(
        body, name, (t // tm,),
        [pl.BlockSpec((tm, d), lambda i: (i, 0)), pl.BlockSpec((1, d), lambda i: (0, 0)),
         pl.BlockSpec((d, n), lambda i: (0, 0))],
        [pl.BlockSpec((tm, AB_COLS), lambda i: (i, 0)), pl.BlockSpec((tm, n - AB_COLS), lambda i: (i, 0)), tb, tb],
        [jax.ShapeDtypeStruct((t, AB_COLS), F32), jax.ShapeDtypeStruct((t, n - AB_COLS), BF16), tshape, tshape],
        [], ("parallel",), (x, g, w), ride)


def _pool_window_sums(xx, forward):
    n = xx.shape[0]
    sh = (lambda k: n - k) if forward else (lambda k: k)
    s2 = xx + pltpu.roll(xx, sh(1), 0)
    s4 = s2 + pltpu.roll(s2, sh(2), 0)
    s8 = s4 + pltpu.roll(s4, sh(4), 0)
    s16 = s8 + pltpu.roll(s8, sh(8), 0)
    grp = lax.broadcasted_iota(jnp.int32, (1, POOL_WIDTH), 1) // POOL_GW
    return jnp.where(grp == 0, s2, jnp.where(grp == 1, s4, jnp.where(grp == 2, s8, s16)))


def _pool_count(t0, rows):
    grp = lax.broadcasted_iota(jnp.int32, (1, POOL_WIDTH), 1) // POOL_GW
    win = jnp.where(grp == 0, 2, jnp.where(grp == 1, 4, jnp.where(grp == 2, 8, 16)))
    tt = t0 + lax.broadcasted_iota(jnp.int32, (rows, 1), 0)
    return jnp.minimum(tt + 1, win).astype(F32)


def _pool_diff(cur, prev, t0):
    tm = cur.shape[0]
    sums = _pool_window_sums(jnp.concatenate([prev, cur], axis=0), False)[POOL_HALO:]
    return sums / _pool_count(t0, tm) - cur


def _pool_fwd(ab, wbd, scale, tm, name):
    t = ab.shape[0]
    hb = tm // POOL_HALO

    def body(cur_ref, prev_ref, w_ref, s_ref, y_ref):
        i = pl.program_id(0)
        prev = jnp.where(i == 0, 0.0, prev_ref[...])
        d = _pool_diff(cur_ref[...], prev, i * tm).astype(BF16)
        y_ref[...] = (_dot(d, w_ref[...]) * s_ref[...]).astype(BF16)

    return pl.pallas_call(
        body, name=name, grid=(t // tm,),
        in_specs=[pl.BlockSpec((tm, POOL_WIDTH), lambda i: (i, 0)),
                  pl.BlockSpec((POOL_HALO, POOL_WIDTH), lambda i: (jnp.maximum(i * hb - 1, 0), 0)),
                  pl.BlockSpec((POOL_WIDTH, POOL_WIDTH), lambda i: (0, 0)),
                  pl.BlockSpec((1, POOL_WIDTH), lambda i: (0, 0))],
        out_specs=pl.BlockSpec((tm, POOL_WIDTH), lambda i: (i, 0)),
        out_shape=jax.ShapeDtypeStruct((t, POOL_WIDTH), BF16),
        compiler_params=_params(("parallel",)),
    )(ab, ab, wbd, scale)


_GELU_K = 0.7978845608028654
_GELU_A = 0.044715


def _gelu(x):
    return 0.5 * x * (1.0 + jnp.tanh(_GELU_K * (x + _GELU_A * (x * x * x))))


def _gelu_grad(x):
    th = jnp.tanh(_GELU_K * (x + _GELU_A * (x * x * x)))
    return 0.5 * (1.0 + th) + 0.5 * x * (1.0 - th * th) * (_GELU_K * (1.0 + 3.0 * _GELU_A * (x * x)))


def _head_lanes(h):
    return lax.broadcasted_iota(jnp.int32, (1, SG_WIDTH), 1) // SG_HD == h


def _sg_fwd(ab, gn, wm, bfull, tm, name):
    t = ab.shape[0]

    def body(u_ref, v_ref, gn_ref, wm_ref, b_ref, y_ref):
        v = _gelu(v_ref[...])
        vn = (v * _rstd(v) * gn_ref[...]).astype(BF16)
        for c in range(tm // CHUNK):
            rows = slice(c * CHUNK, (c + 1) * CHUNK)
            vc = vn[rows]
            sv = b_ref[...]
            for h in range(SG_HEADS):
                sv = sv + jnp.where(_head_lanes(h), _dot(wm_ref[h], vc), 0.0)
            y_ref[rows, :] = (_gelu(u_ref[rows, :]) * sv).astype(BF16)

    return pl.pallas_call(
        body, name=name, grid=(t // tm,),
        in_specs=[pl.BlockSpec((tm, SG_WIDTH), lambda i: (i, 1)), pl.BlockSpec((tm, SG_WIDTH), lambda i: (i, 2)),
                  pl.BlockSpec((1, SG_WIDTH), lambda i: (0, 0)),
                  pl.BlockSpec((SG_HEADS, CHUNK, CHUNK), lambda i: (0, 0, 0)),
                  pl.BlockSpec((CHUNK, SG_WIDTH), lambda i: (0, 0))],
        out_specs=pl.BlockSpec((tm, SG_WIDTH), lambda i: (i, 0)),
        out_shape=jax.ShapeDtypeStruct((t, SG_WIDTH), BF16),
        compiler_params=_params(("parallel",)),
    )(ab, ab, gn, wm, bfull)


LOG2E = 1.4426950408889634
SB_SCALE = 0.125 * LOG2E
SB_DEAD_LOG2 = 152.0
SB_QUERY_BLOCKS_PER_STEP = 2


def _log2_sigmoids(y):
    neg_abs = lax.bitcast_convert_type(lax.bitcast_convert_type(y, jnp.uint32) | jnp.uint32(0x80000000), F32)
    lb = jnp.minimum(y, 0.0) - jnp.log(1.0 + jnp.exp2(neg_abs)) * LOG2E
    return lb, lb - y


def _split(x):
    hi = x.astype(BF16)
    return hi, (x - hi.astype(F32)).astype(BF16)


def _tri_dot(tri, x):
    hi, lo = _split(x)
    return _dot(tri, hi) + _dot(tri, lo)


def _sba_fwd(qkv, vtb, tq, name, ride=None):
    t = qkv.shape[0]
    nb = t // tq
    qb = SB_QUERY_BLOCKS_PER_STEP
    upper = (jnp.arange(tq)[None, :] > jnp.arange(tq)[:, None]).astype(BF16)

    def body(q_ref, k_ref, vt_ref, up_ref, ot_ref, c_ref, n_ref):
        pair, step = pl.program_id(0), pl.program_id(1)
        refs = (q_ref, k_ref, vt_ref, up_ref, ot_ref, c_ref, n_ref)
        subs = list(range(qb))

        @pl.when(step == 0)
        def _():
            for sub in subs:
                query_blocks([sub], pair, step, *refs)

        @pl.when(step > 0)
        def _():
            query_blocks(subs, pair, step, *refs)

    def query_blocks(subs, pair, step, q_ref, k_ref, vt_ref, up_ref, ot_ref, c_ref, n_ref):
        index = {sub: step * qb + sub for sub in subs}
        mine = {sub: slice(sub * tq, (sub + 1) * tq) for sub in subs}
        up = up_ref[...]
        lane_head = lax.broadcasted_iota(jnp.int32, (1, 128), 1) // SB_HD
        sub_head = lax.broadcasted_iota(jnp.int32, (128, 1), 0) // SB_HD
        causal = (lax.broadcasted_iota(jnp.int32, (tq, tq), 0) < lax.broadcasted_iota(jnp.int32, (tq, tq), 1))
        qh = {}
        for sub in subs:
            q = q_ref[mine[sub], :]
            qh[sub] = [jnp.where(lane_head == h, q, jnp.zeros_like(q)) for h in range(2)]

        def blocks(work, carry):
            cs = {sub: list(carry[sub][:2]) for sub in carry}
            acc = {sub: carry[sub][2] for sub in carry}
            kj = [k_ref[pl.ds(pl.multiple_of(j * tq, tq), tq), :] for _, j, _ in work]
            vt = [vt_ref[j] for _, j, _ in work]
            chains = [(w, h) for w in range(len(work)) for h in range(2)]
            z = [_dot_nt(kj[w], qh[work[w][0]][h]) for w, h in chains]
            ls = [_log2_sigmoids(zz * SB_SCALE) for zz in z]
            lb = [x[0] for x in ls]
            l1 = [jnp.where(causal, x[1], 0.0) if work[w][2] else x[1] for x, (w, h) in zip(ls, chains)]
            after = [_tri_dot(up, x) for x in l1]
            a = []
            for n, (w, h) in enumerate(chains):
                sub, j, diag = work[w]
                c_ref[h, j, :, mine[sub]] = cs[sub][h]
                an = jnp.exp2(lb[n] + after[n] + cs[sub][h])
                a.append(jnp.where(causal, an, 0.0) if diag else an)
                cs[sub][h] = cs[sub][h] + after[n][0:1, :] + l1[n][0:1, :]
            alive = {sub: (jnp.max(jnp.maximum(*cs[sub])) > -SB_DEAD_LOG2).astype(jnp.int32) for sub in cs}
            for n, (w, h) in enumerate(chains):
                sub = work[w][0]
                acc[sub] = acc[sub] + _dot(jnp.where(sub_head == h, vt[w], jnp.zeros_like(vt[w])), a[n].astype(BF16))
            return {sub: (alive[sub], cs[sub][0], cs[sub][1], acc[sub]) for sub in cs}

        zero = jnp.zeros((1, tq), F32)
        start = {sub: (zero, zero, jnp.zeros((128, tq), F32)) for sub in subs}
        both = lambda: blocks([(sub, index[sub] - b, b == 0) for sub in subs for b in range(2)], start)
        if len(subs) == 1:
            joint = lax.cond(index[subs[0]] > 0, both, lambda: blocks([(subs[0], index[subs[0]], True)], start))
        else:
            joint = both()
        for sub in subs:
            i = index[sub]

            def left(state, sub=sub, i=i):
                s, _, c0, c1, acc = state
                return (s + 1,) + blocks([(sub, i - 1 - s, False)], {sub: (c0, c1, acc)})[sub]

            state = lax.while_loop(lambda st, i=i: (st[0] < i) & (st[1] > 0), left, (jnp.minimum(i, 1),) + tuple(joint[sub]))
            ot_ref[:, mine[sub]] = state[4].astype(BF16)
            n_ref[pair, i] = (state[0] + 1).astype(F32)

    return _call(
        body, name, (SB_PAIRS, nb // qb),
        [pl.BlockSpec((qb * tq, 128), lambda p, i: (i, p)),
         pl.BlockSpec((t, 128), lambda p, i: (0, SB_PAIRS + p)),
         pl.BlockSpec((None, nb, 128, tq), lambda p, i: (p, 0, 0, 0)),
         pl.BlockSpec((tq, tq), lambda p, i: (0, 0))],
        [pl.BlockSpec((128, qb * tq), lambda p, i: (p, i)),
         pl.BlockSpec((2, nb, 1, qb * tq), lambda p, i: (p, 0, 0, i)),
         pl.BlockSpec(memory_space=pltpu.SMEM)],
        [jax.ShapeDtypeStruct((SB_WIDTH, t), BF16),
         jax.ShapeDtypeStruct((2 * SB_PAIRS, nb, 1, t), F32),
         jax.ShapeDtypeStruct((SB_PAIRS, nb), F32)],
        [], ("arbitrary", "arbitrary"), (qkv, qkv, vtb, upper), ride)


def _outproj_mlp_fwd(x, ya, yb, yct, wo, g2, wup, wdn, tm, name, ride=None):
    t, d = x.shape
    nf = wup.shape[0]

    def body(x_ref, ya_ref, yb_ref, yct_ref, wo_ref, g_ref, wu_ref, wd_ref, x1_ref, u_ref, x2_ref, h_ref, acc_ref):
        j = pl.program_id(1)

        @pl.when(j == 0)
        def _():
            x1 = (x_ref[...] + _dot(ya_ref[...], wo_ref[0:POOL_WIDTH, :])
                  + _dot(yb_ref[...], wo_ref[POOL_WIDTH:POOL_WIDTH + SG_WIDTH, :])
                  + _dot_tn(yct_ref[...], wo_ref[POOL_WIDTH + SG_WIDTH:, :]))
            x1_ref[...] = x1
            h_ref[...] = (x1 * _rstd(x1) * g_ref[...]).astype(BF16)
            acc_ref[...] = x1

        u = _dot(h_ref[...], wu_ref[...])
        u_ref[...] = u
        r = jnp.maximum(u, 0.0)
        acc_ref[...] += _dot((r * r).astype(BF16), wd_ref[...])

        @pl.when(j == nf - 1)
        def _():
            x2_ref[...] = acc_ref[...]

    row = lambda w: pl.BlockSpec((tm, w), lambda i, j: (i, 0))
    return _call(
        body, name, (t // tm, nf),
        [row(d), row(POOL_WIDTH), row(SG_WIDTH), pl.BlockSpec((SB_WIDTH, tm), lambda i, j: (0, i)),
         pl.BlockSpec((d, d), lambda i, j: (0, 0)), pl.BlockSpec((1, d), lambda i, j: (0, 0)),
         pl.BlockSpec((None, d, FF_SHARD), lambda i, j: (j, 0, 0)),
         pl.BlockSpec((None, FF_SHARD, d), lambda i, j: (j, 0, 0))],
        [row(d), pl.BlockSpec((tm, FF_SHARD), lambda i, j: (i, j)), row(d)],
        [jax.ShapeDtypeStruct((t, d), F32), jax.ShapeDtypeStruct((t, nf * FF_SHARD), F32),
         jax.ShapeDtypeStruct((t, d), F32)],
        [pltpu.VMEM((tm, d), BF16), pltpu.VMEM((tm, d), F32)],
        ("parallel", "arbitrary"), (x, ya, yb, yct, wo, g2, wup, wdn), ride)


def _loss_grad(x, g, target, tm, name):
    t, d = x.shape
    nt = t // tm

    def body(x_ref, g_ref, t_ref, loss_ref, dx_ref, dg_ref, sq_ref):
        i = pl.program_id(0)

        @pl.when(i == 0)
        def _():
            sq_ref[...] = jnp.zeros_like(sq_ref)
            dg_ref[...] = jnp.zeros_like(dg_ref)

        xx = x_ref[...]
        r = _rstd(xx)
        err = xx * r * g_ref[...] - t_ref[...]
        sq_ref[...] += jnp.sum(err * err, axis=0, keepdims=True)
        dx, dg = _rms_bwd(xx, r, g_ref[...], err * (1.0 / d))
        dx_ref[...] = dx
        dg_ref[...] += dg

        @pl.when(i == nt - 1)
        def _():
            loss_ref[...] = jnp.sum(sq_ref[...], axis=1, keepdims=True) * (0.5 / d)

    return pl.pallas_call(
        body, name=name, grid=(nt,),
        in_specs=[pl.BlockSpec((tm, d), lambda i: (i, 0)), pl.BlockSpec((1, d), lambda i: (0, 0)),
                  pl.BlockSpec((tm, d), lambda i: (i, 0))],
        out_specs=[pl.BlockSpec((1, 1), lambda i: (0, 0)), pl.BlockSpec((tm, d), lambda i: (i, 0)),
                   pl.BlockSpec((1, d), lambda i: (0, 0))],
        out_shape=[jax.ShapeDtypeStruct((1, 1), F32), jax.ShapeDtypeStruct((t, d), F32),
                   jax.ShapeDtypeStruct((1, d), F32)],
        scratch_shapes=[pltpu.VMEM((1, d), F32)],
        compiler_params=_params(("arbitrary",)),
    )(x, g, target)


def _mlp_bwd(dx2, x1, g2, u, wup, wdn, tm, name, ride=None):
    t, d = dx2.shape
    nf = wup.shape[0]
    nt = t // tm

    def body(dx2_ref, x1_ref, g_ref, u_ref, wu_ref, wd_ref, dx1_ref, du_ref, r_ref, h_ref, dxb_ref, dg_ref, acc_ref):
        i, j = pl.program_id(0), pl.program_id(1)

        @pl.when(j == 0)
        def _():
            x1 = x1_ref[...]
            h_ref[...] = (x1 * _rstd(x1) * g_ref[...]).astype(BF16)
            dxb_ref[...] = dx2_ref[...].astype(BF16)
            acc_ref[...] = jnp.zeros_like(acc_ref)

        @pl.when((i == 0) & (j == 0))
        def _():
            dg_ref[...] = jnp.zeros_like(dg_ref)

        dr = _dot_nt(dxb_ref[...], wd_ref[...])
        ru = jnp.maximum(u_ref[...], 0.0)
        du = (dr * (2.0 * ru)).astype(BF16)
        du_ref[...] = du
        r_ref[...] = (ru * ru).astype(BF16)
        acc_ref[...] += _dot_nt(du, wu_ref[...])

        @pl.when(j == nf - 1)
        def _():
            x1 = x1_ref[...]
            dx, dg = _rms_bwd(x1, _rstd(x1), g_ref[...], acc_ref[...])
            dx1_ref[...] = dx2_ref[...] + dx
            dg_ref[...] += dg

    row = lambda w: pl.BlockSpec((tm, w), lambda i, j: (i, 0))
    col = pl.BlockSpec((tm, FF_SHARD), lambda i, j: (i, j))
    return _call(
        body, name, (nt, nf),
        [row(d), row(d), pl.BlockSpec((1, d), lambda i, j: (0, 0)), col,
         pl.BlockSpec((None, d, FF_SHARD), lambda i, j: (j, 0, 0)),
         pl.BlockSpec((None, FF_SHARD, d), lambda i, j: (j, 0, 0))],
        [row(d), col, col, row(d), row(d), pl.BlockSpec((1, d), lambda i, j: (0, 0))],
        [jax.ShapeDtypeStruct((t, d), F32), jax.ShapeDtypeStruct((t, nf * FF_SHARD), BF16),
         jax.ShapeDtypeStruct((t, nf * FF_SHARD), BF16), jax.ShapeDtypeStruct((t, d), BF16),
         jax.ShapeDtypeStruct((t, d), BF16), jax.ShapeDtypeStruct((1, d), F32)],
        [pltpu.VMEM((tm, d), F32)], ("arbitrary", "arbitrary"), (dx2, x1, g2, u, wup, wdn), ride)


def _matmul_tn(a, b, bm, bn, bt, name, by_column_block=0, a_transposed=False):
    m, t = a.shape if a_transposed else a.shape[::-1]
    n = b.shape[1]
    nk = t // bt

    def body(a_ref, b_ref, o_ref, acc_ref):
        k = pl.program_id(2)

        @pl.when(k == 0)
        def _():
            acc_ref[...] = jnp.zeros_like(acc_ref)

        acc_ref[...] += _dot(a_ref[...], b_ref[...]) if a_transposed else _dot_tn(a_ref[...], b_ref[...])

        @pl.when(k == nk - 1)
        def _():
            if by_column_block:
                for c in range(bn // by_column_block):
                    o_ref[c] = acc_ref[:, c * by_column_block:(c + 1) * by_column_block].astype(BF16)
            else:
                o_ref[...] = acc_ref[...].astype(BF16)

    if by_column_block:
        out_spec = pl.BlockSpec((bn // by_column_block, bm, by_column_block), lambda i, j, k: (j, i, 0))
        out_shape = jax.ShapeDtypeStruct((n // by_column_block, m, by_column_block), BF16)
    else:
        out_spec = pl.BlockSpec((bm, bn), lambda i, j, k: (i, j))
        out_shape = jax.ShapeDtypeStruct((m, n), BF16)
    a_spec = (pl.BlockSpec((bm, bt), lambda i, j, k: (i, k)) if a_transposed
              else pl.BlockSpec((bt, bm), lambda i, j, k: (k, i)))
    return _call(body, name, (m // bm, n // bn, nk), [a_spec, pl.BlockSpec((bt, bn), lambda i, j, k: (k, j))],
                 out_spec, out_shape, [pltpu.VMEM((bm, bn), F32)], ("parallel", "parallel", "arbitrary"), (a, b))


def _outproj_bwd(dx1, wo, tm, name):
    t, d = dx1.shape
    c2 = POOL_WIDTH + SG_WIDTH

    def body(dx_ref, wo_ref, dya_ref, dyb_ref, dyc_ref, dyct_ref, dxb_ref):
        dxb = dx_ref[...].astype(BF16)
        dxb_ref[...] = dxb
        dya_ref[...] = _dot_nt(dxb, wo_ref[0:POOL_WIDTH, :])
        dyb_ref[...] = _dot_nt(dxb, wo_ref[POOL_WIDTH:c2, :])
        dyc_ref[...] = _dot_nt(dxb, wo_ref[c2:, :]).astype(BF16)
        dyct_ref[...] = _dot_nt(wo_ref[c2:, :], dxb).astype(BF16)

    row = lambda w: pl.BlockSpec((tm, w), lambda i: (i, 0))
    return pl.pallas_call(
        body, name=name, grid=(t // tm,),
        in_specs=[row(d), pl.BlockSpec((d, d), lambda i: (0, 0))],
        out_specs=[row(POOL_WIDTH), row(SG_WIDTH), row(SB_WIDTH), pl.BlockSpec((SB_WIDTH, tm), lambda i: (0, i)), row(d)],
        out_shape=[jax.ShapeDtypeStruct((t, POOL_WIDTH), F32), jax.ShapeDtypeStruct((t, SG_WIDTH), F32),
                   jax.ShapeDtypeStruct((t, SB_WIDTH), BF16), jax.ShapeDtypeStruct((SB_WIDTH, t), BF16),
                   jax.ShapeDtypeStruct((t, d), BF16)],
        compiler_params=_params(("parallel",)),
    )(dx1, wo)


def _sba_bwd(qkv, ktb, dyc, dyct, cc, nvis, tq, name, ride=None):
    t = qkv.shape[0]
    nb = t // tq
    idx = jnp.arange(tq)
    upper = (idx[None, :] > idx[:, None]).astype(BF16)
    lower = (idx[None, :] < idx[:, None]).astype(BF16)

    qb = SB_QUERY_BLOCKS_PER_STEP

    def body(q_ref, k_ref, v_ref, kt_ref, do_ref, dot_ref, c_ref, n_ref, up_ref, lo_ref, dqt_ref, dk_ref, dv_ref):
        pair, step = pl.program_id(0), pl.program_id(1)
        refs = (q_ref, k_ref, v_ref, kt_ref, do_ref, dot_ref, c_ref, n_ref, up_ref, lo_ref, dqt_ref, dk_ref, dv_ref)
        subs = list(range(qb))

        @pl.when(step == 0)
        def _():
            dk_ref[...] = jnp.zeros_like(dk_ref)
            dv_ref[...] = jnp.zeros_like(dv_ref)
            for sub in subs:
                query_blocks([sub], pair, step, *refs)

        @pl.when(step > 0)
        def _():
            query_blocks(subs, pair, step, *refs)

    def query_blocks(subs, pair, step, q_ref, k_ref, v_ref, kt_ref, do_ref, dot_ref, c_ref, n_ref, up_ref, lo_ref,
                     dqt_ref, dk_ref, dv_ref):
        index = {sub: step * qb + sub for sub in subs}
        mine = {sub: slice(sub * tq, (sub + 1) * tq) for sub in subs}
        up = up_ref[...]
        lo = lo_ref[...]
        lane_head = lax.broadcasted_iota(jnp.int32, (1, 128), 1) // SB_HD
        sub_head = lax.broadcasted_iota(jnp.int32, (128, 1), 0) // SB_HD
        causal = (lax.broadcasted_iota(jnp.int32, (tq, tq), 0) < lax.broadcasted_iota(jnp.int32, (tq, tq), 1))
        hms = [lane_head == h for h in range(2)]
        qh, qs, doh, dot = {}, {}, {}, {}
        for sub in subs:
            q, do = q_ref[mine[sub], :], do_ref[mine[sub], :]
            qh[sub] = [jnp.where(hm, q, jnp.zeros_like(q)) for hm in hms]
            qs[sub] = [x * 0.125 for x in qh[sub]]
            doh[sub] = [jnp.where(hm, do, jnp.zeros_like(do)) for hm in hms]
            dot[sub] = dot_ref[:, mine[sub]]

        def blocks(work, carry):
            cgs = {sub: list(carry[sub][:2]) for sub in carry}
            dqt = {sub: carry[sub][2] for sub in carry}
            rows = [pl.ds(pl.multiple_of(j * tq, tq), tq) for _, j, _ in work]
            kj = [k_ref[r, :] for r in rows]
            vj = [v_ref[r, :] for r in rows]
            kt = [kt_ref[j] for _, j, _ in work]
            chains = [(w, h) for w in range(len(work)) for h in range(2)]
            z = [_dot_nt(kj[w], qh[work[w][0]][h]) for w, h in chains]
            da = [_dot(jnp.where(hms[h], vj[w], jnp.zeros_like(vj[w])), dot[work[w][0]]) for w, h in chains]
            ls = [_log2_sigmoids(zz * SB_SCALE) for zz in z]
            lb = [x[0] for x in ls]
            l1 = [jnp.where(causal, x[1], 0.0) if work[w][2] else x[1] for x, (w, h) in zip(ls, chains)]
            after = [_tri_dot(up, x) for x in l1]
            a = [jnp.exp2(lb[n] + after[n] + c_ref[h, work[w][1], :, mine[work[w][0]]]) for n, (w, h) in enumerate(chains)]
            a = [jnp.where(causal, a[n], 0.0) if work[w][2] else a[n] for n, (w, h) in enumerate(chains)]
            g = [a[n] * da[n] for n in range(len(chains))]
            gloc = [_dot(lo, x.astype(BF16)) for x in g]
            dzb = []
            for n, (w, h) in enumerate(chains):
                sub, _, diag = work[w]
                gsum = gloc[n] + cgs[sub][h]
                dz = g[n] - jnp.exp2(lb[n]) * (g[n] + gsum)
                dzb.append((jnp.where(causal, dz, 0.0) if diag else dz).astype(BF16))
                cgs[sub][h] = gsum[tq - 1:tq, :] + g[n][tq - 1:tq, :]
            ab = [x.astype(BF16) for x in a]
            for n, (w, h) in enumerate(chains):
                sub = work[w][0]
                dqt[sub] = dqt[sub] + _dot(jnp.where(sub_head == h, kt[w], jnp.zeros_like(kt[w])), dzb[n])
            for w, (sub, _, _) in enumerate(work):
                dk_ref[rows[w], :] += _dot(dzb[2 * w], qs[sub][0]) + _dot(dzb[2 * w + 1], qs[sub][1])
                dv_ref[rows[w], :] += _dot(ab[2 * w], doh[sub][0]) + _dot(ab[2 * w + 1], doh[sub][1])
            return {sub: (cgs[sub][0], cgs[sub][1], dqt[sub]) for sub in cgs}

        zero = jnp.zeros((1, tq), F32)
        carry = {}
        for sub in subs:
            i = index[sub]
            n = jnp.clip(n_ref[pair, i].astype(jnp.int32), 1, i + 1)
            carry[sub] = lax.fori_loop(jnp.minimum(i + 1 - n, i - 1), i - 1,
                                       lambda s, cr, sub=sub: blocks([(sub, s, False)], {sub: cr})[sub],
                                       (zero, zero, jnp.zeros((128, tq), F32)))
        both = lambda: blocks([(sub, index[sub] - 1 + b, b == 1) for sub in subs for b in range(2)], carry)
        if len(subs) == 1:
            carry = lax.cond(index[subs[0]] > 0, both, lambda: blocks([(subs[0], index[subs[0]], True)], carry))
        else:
            carry = both()
        for sub in subs:
            dqt_ref[:, mine[sub]] = carry[sub][2] * 0.125

    return _call(
        body, name, (SB_PAIRS, nb // qb),
        [pl.BlockSpec((qb * tq, 128), lambda p, i: (i, p)),
         pl.BlockSpec((t, 128), lambda p, i: (0, SB_PAIRS + p)),
         pl.BlockSpec((t, 128), lambda p, i: (0, 2 * SB_PAIRS + p)),
         pl.BlockSpec((None, nb, 128, tq), lambda p, i: (p, 0, 0, 0)),
         pl.BlockSpec((qb * tq, 128), lambda p, i: (i, p)),
         pl.BlockSpec((128, qb * tq), lambda p, i: (p, i)),
         pl.BlockSpec((2, nb, 1, qb * tq), lambda p, i: (p, 0, 0, i)),
         pl.BlockSpec(memory_space=pltpu.SMEM),
         pl.BlockSpec((tq, tq), lambda p, i: (0, 0)),
         pl.BlockSpec((tq, tq), lambda p, i: (0, 0))],
        [pl.BlockSpec((128, qb * tq), lambda p, i: (p, i)),
         pl.BlockSpec((t, 128), lambda p, i: (0, p)),
         pl.BlockSpec((t, 128), lambda p, i: (0, p))],
        [jax.ShapeDtypeStruct((SB_WIDTH, t), F32), jax.ShapeDtypeStruct((t, SB_WIDTH), F32),
         jax.ShapeDtypeStruct((t, SB_WIDTH), F32)],
        [], ("arbitrary", "arbitrary"), (qkv, qkv, qkv, ktb, dyc, dyct, cc, nvis, upper, lower), ride)


def _sg_bwd(ab, dyb, gn, wm, wmt, bfull, tm, name):
    t = ab.shape[0]
    nt = t // tm
    sel = (jnp.arange(SG_WIDTH)[:, None] // SG_HD == jnp.arange(CHUNK)[None, :]).astype(F32)

    def body(u_ref, v_ref, dy_ref, gn_ref, wm_ref, wmt_ref, b_ref, sel_ref,
             dup_ref, dvp_ref, dgn_ref, dw_ref, db_ref, dbacc_ref):
        i = pl.program_id(0)

        @pl.when(i == 0)
        def _():
            dgn_ref[...] = jnp.zeros_like(dgn_ref)
            dw_ref[...] = jnp.zeros_like(dw_ref)
            dbacc_ref[...] = jnp.zeros_like(dbacc_ref)

        tril = (lax.broadcasted_iota(jnp.int32, (CHUNK, CHUNK), 0) >= lax.broadcasted_iota(jnp.int32, (CHUNK, CHUNK), 1))
        gn_ = gn_ref[...]
        for c in range(tm // CHUNK):
            rows = slice(c * CHUNK, (c + 1) * CHUNK)
            up, vp, dy = u_ref[rows, :], v_ref[rows, :], dy_ref[rows, :]
            u, v = _gelu(up), _gelu(vp)
            r = _rstd(v)
            vn = (v * r * gn_).astype(BF16)
            sv = b_ref[...]
            for h in range(SG_HEADS):
                sv = sv + jnp.where(_head_lanes(h), _dot(wm_ref[h], vn), 0.0)
            dup_ref[rows, :] = dy * sv * _gelu_grad(up)
            dsv = dy * u
            dbacc_ref[...] += dsv
            dvn = jnp.zeros((CHUNK, SG_WIDTH), F32)
            for h in range(SG_HEADS):
                dsv_h = jnp.where(_head_lanes(h), dsv, 0.0).astype(BF16)
                dvn = dvn + _dot(wmt_ref[h], dsv_h)
                dw_ref[h] += jnp.where(tril, _dot_nt(dsv_h, vn), 0.0)
            dv, dgn = _rms_bwd(v, r, gn_, dvn)
            dgn_ref[...] += dgn
            dvp_ref[rows, :] = dv * _gelu_grad(vp)

        @pl.when(i == nt - 1)
        def _():
            db_ref[...] = jnp.dot(dbacc_ref[...], sel_ref[...], preferred_element_type=F32,
                                  precision=lax.Precision.HIGHEST)

    const = lambda shape: pl.BlockSpec(shape, lambda i: tuple(0 for _ in shape))
    return pl.pallas_call(
        body, name=name, grid=(nt,),
        in_specs=[pl.BlockSpec((tm, SG_WIDTH), lambda i: (i, 1)), pl.BlockSpec((tm, SG_WIDTH), lambda i: (i, 2)),
                  pl.BlockSpec((tm, SG_WIDTH), lambda i: (i, 0)), const((1, SG_WIDTH)),
                  const((SG_HEADS, CHUNK, CHUNK)), const((SG_HEADS, CHUNK, CHUNK)), const((CHUNK, SG_WIDTH)),
                  const((SG_WIDTH, CHUNK))],
        out_specs=[pl.BlockSpec((tm, SG_WIDTH), lambda i: (i, 0)), pl.BlockSpec((tm, SG_WIDTH), lambda i: (i, 0)),
                   const((1, SG_WIDTH)), const((SG_HEADS, CHUNK, CHUNK)), const((CHUNK, CHUNK))],
        out_shape=[jax.ShapeDtypeStruct((t, SG_WIDTH), F32), jax.ShapeDtypeStruct((t, SG_WIDTH), F32),
                   jax.ShapeDtypeStruct((1, SG_WIDTH), F32), jax.ShapeDtypeStruct((SG_HEADS, CHUNK, CHUNK), F32),
                   jax.ShapeDtypeStruct((CHUNK, CHUNK), F32)],
        scratch_shapes=[pltpu.VMEM((CHUNK, SG_WIDTH), F32)],
        compiler_params=_params(("arbitrary",)),
    )(ab, ab, dyb, gn, wm, wmt, bfull, sel)


def _pool_bwd(ab, dya, wbd, scale, tm, name):
    t = ab.shape[0]
    nt = t // tm
    hb = tm // POOL_HALO
    nh = t // POOL_HALO

    def body(cur_ref, prev_ref, dy_ref, dyn_ref, w_ref, s_ref, da_ref, dw_ref, ds_ref):
        i = pl.program_id(0)

        @pl.when(i == 0)
        def _():
            dw_ref[...] = jnp.zeros_like(dw_ref)
            ds_ref[...] = jnp.zeros_like(ds_ref)

        prev = jnp.where(i == 0, 0.0, prev_ref[...])
        d = _pool_diff(cur_ref[...], prev, i * tm).astype(BF16)
        dy = dy_ref[...]
        ds_ref[...] += jnp.sum(dy * _dot(d, w_ref[...]), axis=0, keepdims=True)
        dyn = jnp.where(i == nt - 1, 0.0, dyn_ref[...])
        dys = (jnp.concatenate([dy, dyn], axis=0) * s_ref[...]).astype(BF16)
        dw_ref[...] += _dot_tn(d, dys[:tm])
        dd = _dot_nt(dys, w_ref[...])
        fwd = _pool_window_sums(dd / _pool_count(i * tm, tm + POOL_HALO), True)
        da_ref[...] = fwd[:tm] - dd[:tm]

    return pl.pallas_call(
        body, name=name, grid=(nt,),
        in_specs=[pl.BlockSpec((tm, POOL_WIDTH), lambda i: (i, 0)),
                  pl.BlockSpec((POOL_HALO, POOL_WIDTH), lambda i: (jnp.maximum(i * hb - 1, 0), 0)),
                  pl.BlockSpec((tm, POOL_WIDTH), lambda i: (i, 0)),
                  pl.BlockSpec((POOL_HALO, POOL_WIDTH), lambda i: (jnp.minimum((i + 1) * hb, nh - 1), 0)),
                  pl.BlockSpec((POOL_WIDTH, POOL_WIDTH), lambda i: (0, 0)),
                  pl.BlockSpec((1, POOL_WIDTH), lambda i: (0, 0))],
        out_specs=[pl.BlockSpec((tm, POOL_WIDTH), lambda i: (i, 0)),
                   pl.BlockSpec((POOL_WIDTH, POOL_WIDTH), lambda i: (0, 0)),
                   pl.BlockSpec((1, POOL_WIDTH), lambda i: (0, 0))],
        out_shape=[jax.ShapeDtypeStruct((t, POOL_WIDTH), F32), jax.ShapeDtypeStruct((POOL_WIDTH, POOL_WIDTH), F32),
                   jax.ShapeDtypeStruct((1, POOL_WIDTH), F32)],
        compiler_params=_params(("arbitrary",)),
    )(ab, ab, dya, dya, wbd, scale)


def _inproj_bwd(dx1, x, g, da, dup, dvp, dqt, dk, dv, w, tm, name, ride=None):
    t, d = x.shape
    n = w.shape[1]
    nt = t // tm

    def body(dx1_ref, x_ref, g_ref, da_ref, du_ref, dv_ref, dqt_ref, dk_ref, dvv_ref, w_ref,
             dx_ref, h_ref, dp_ref, dg_ref):
        @pl.when(pl.program_id(0) == 0)
        def _():
            dg_ref[...] = jnp.zeros_like(dg_ref)

        dp = jnp.concatenate([da_ref[...], du_ref[...], dv_ref[...], dqt_ref[...].T, dk_ref[...], dvv_ref[...]],
                             axis=1).astype(BF16)
        dp_ref[...] = dp
        xx = x_ref[...]
        r = _rstd(xx)
        h_ref[...] = (xx * r * g_ref[...]).astype(BF16)
        dx, dg = _rms_bwd(xx, r, g_ref[...], _dot_nt(dp, w_ref[...]))
        dx_ref[...] = dx1_ref[...] + dx
        dg_ref[...] += dg

    row = lambda w_: pl.BlockSpec((tm, w_), lambda i: (i, 0))
    return _call(
        body, name, (nt,),
        [row(d), row(d), pl.BlockSpec((1, d), lambda i: (0, 0)), row(POOL_WIDTH), row(SG_WIDTH),
         row(SG_WIDTH), pl.BlockSpec((SB_WIDTH, tm), lambda i: (0, i)), row(SB_WIDTH), row(SB_WIDTH),
         pl.BlockSpec((d, n), lambda i: (0, 0))],
        [row(d), row(d), row(n), pl.BlockSpec((1, d), lambda i: (0, 0))],
        [jax.ShapeDtypeStruct((t, d), F32), jax.ShapeDtypeStruct((t, d), BF16),
         jax.ShapeDtypeStruct((t, n), BF16), jax.ShapeDtypeStruct((1, d), F32)],
        [], ("arbitrary",), (dx1, x, g, da, dup, dvp, dqt, dk, dv, w), ride)


def _full_w_in(gathered):
    return gathered.transpose(1, 0, 2).reshape(D_MODEL, IN_COLS)


_SMALL_SHAPES = ((D_MODEL,), (4, POOL_GW, POOL_GW), (POOL_WIDTH,), (SG_WIDTH,), (SG_HEADS, CHUNK, CHUNK),
                 (SG_HEADS, CHUNK), (D_MODEL,))
_SMALL_SIZES = tuple(functools.reduce(lambda p, q: p * q, shp) for shp in _SMALL_SHAPES)
_SMALL_ROWS = sum(_SMALL_SIZES) // 128
_NORM1_ROWS = D_MODEL // 128


def _pack_small_layer(arrs):
    return jnp.concatenate([a.reshape(-1) for a in arrs]).reshape(_SMALL_ROWS, 128)


def _pack_small(arrs, final):
    return jnp.concatenate([_pack_small_layer([a[l] for a in arrs]) for l in range(DEPTH)] + [final.reshape(-1, 128)])


def _unpack_small(buf):
    per_layer = []
    for l in range(DEPTH):
        flat, off, outs = buf[l * _SMALL_ROWS:(l + 1) * _SMALL_ROWS].reshape(-1), 0, []
        for shp, size in zip(_SMALL_SHAPES, _SMALL_SIZES):
            outs.append(flat[off:off + size].reshape(shp))
            off += size
        per_layer.append(outs)
    return [jnp.stack([per_layer[l][k] for l in range(DEPTH)]) for k in range(len(_SMALL_SHAPES))] + \
           [buf[DEPTH * _SMALL_ROWS:].reshape(-1)]


def _tiles(t):
    return min(512, t), min(256, t // 4), min(2048, t)


def _layer_fwd(xl, wi, wo, wud, small_w, l, ride_mlp=None):
    n1, pw, ps, sn, sw, sb, n2 = small_w
    tm, tq, _ = _tiles(xl.shape[0])
    wbd = jnp.zeros((4, POOL_GW, 4, POOL_GW), F32)
    for gi in range(4):
        wbd = wbd.at[gi, :, gi, :].set(pw[gi])
    wbd = wbd.reshape(POOL_WIDTH, POOL_WIDTH).astype(BF16)
    wm = (sw * jnp.tril(jnp.ones((CHUNK, CHUNK), F32))).astype(BF16)
    wmt = wm.transpose(0, 2, 1)
    bfull = jnp.repeat(sb.T, SG_HD, axis=1)
    g1, g2, psc, gn = n1[None, :], n2[None, :], ps[None, :], sn[None, :]

    if isinstance(wo, Exchange):
        (ab, qkv, ktb, vtb), (wo,) = _inproj_fwd(xl, g1, wi, tm, tq, f"inproj_fwd{l}", wo)
    else:
        ab, qkv, ktb, vtb = _inproj_fwd(xl, g1, wi, tm, tq, f"inproj_fwd{l}")
    ya = _pool_fwd(ab, wbd, psc, tm, f"pool_fwd{l}")
    yb = _sg_fwd(ab, gn, wm, bfull, tm, f"sg_fwd{l}")
    if isinstance(wud, Exchange):
        (yct, cc, nvis), (wu, wd) = _sba_fwd(qkv, vtb, tq, f"sba_fwd{l}", wud)
    else:
        (yct, cc, nvis), (wu, wd) = _sba_fwd(qkv, vtb, tq, f"sba_fwd{l}"), wud
    wo = wo.reshape(D_MODEL, D_MODEL)
    res = _outproj_mlp_fwd(xl, ya, yb, yct, wo, g2, wu, wd, min(MLP_ROWS, xl.shape[0]), f"mlp_fwd{l}", ride_mlp)
    (x1, u, x2), rode = res if ride_mlp is not None else (res, None)
    saved = dict(x=xl, ab=ab, qkv=qkv, ktb=ktb, cc=cc, nvis=nvis, ya=ya, yb=yb, yct=yct, x1=x1, u=u, wi=wi, wo=wo, wu=wu,
                 wd=wd, wbd=wbd, wm=wm, wmt=wmt, bfull=bfull, g1=g1, g2=g2, psc=psc, gn=gn)
    return x2, saved, rode


def _layer_bwd(dx, s, l, ride_mlp=None, scatter_in_attn=True, gather_small=False):
    tm, tq, tw = _tiles(dx.shape[0])
    ktb = s["ktb"]
    res = _mlp_bwd(dx, s["x1"], s["g2"], s["u"], s["wu"], s["wd"], min(MLP_ROWS, dx.shape[0]), f"mlp_bwd{l}", ride_mlp)
    (dx1, du, r, h2, dx2b, dn2), rode = res if ride_mlp is not None else (res, None)
    dw_up = _matmul_tn(h2, du, D_MODEL, 2 * FF_SHARD, tw, f"dw_up{l}", by_column_block=FF_SHARD)
    dw_down = _matmul_tn(r, dx2b, 1024, D_MODEL, tw, f"dw_down{l}").reshape(N_DEV, FF_SHARD, D_MODEL)
    dya, dyb, dyc, dyct, dx1b = _outproj_bwd(dx1, s["wo"], tm, f"outproj_bwd{l}")
    dw_out = jnp.concatenate([
        _matmul_tn(jnp.concatenate([s["ya"], s["yb"]], axis=1), dx1b, POOL_WIDTH + SG_WIDTH, D_MODEL, tw, f"dw_out_ab{l}"),
        _matmul_tn(s["yct"], dx1b, SB_WIDTH, D_MODEL, tw, f"dw_out_c{l}", a_transposed=True)]
    ).reshape(N_DEV, OUT_SHARD, D_MODEL)
    if scatter_in_attn:
        (dqt, dk, dv), (dw_out, dw_up, dw_down) = _sba_bwd(s["qkv"], ktb, dyc, dyct, s["cc"], s["nvis"], tq, f"sba_bwd{l}",
                                                           Exchange([dw_out, dw_up, dw_down], True))
    else:
        dqt, dk, dv = _sba_bwd(s["qkv"], ktb, dyc, dyct, s["cc"], s["nvis"], tq, f"sba_bwd{l}")
    dup, dvp, dgn, dwm, dbm = _sg_bwd(s["ab"], dyb, s["gn"], s["wm"], s["wmt"], s["bfull"], tm, f"sg_bwd{l}")
    da, dwbd, dpsc = _pool_bwd(s["ab"], dya, s["wbd"], s["psc"], tm, f"pool_bwd{l}")
    dpw = jnp.stack([dwbd[gi * POOL_GW:(gi + 1) * POOL_GW, gi * POOL_GW:(gi + 1) * POOL_GW] for gi in range(4)])
    small = _pack_small_layer([jnp.zeros((D_MODEL,), F32), dpw, dpsc[0], dgn[0], dwm, dbm[:, :SG_HEADS].T, dn2[0]])[_NORM1_ROWS:]
    res = _inproj_bwd(dx1, s["x"], s["g1"], da, dup, dvp, dqt, dk, dv, s["wi"], tm, f"inproj_bwd{l}",
                      Exchange([small], False) if gather_small else None)
    (dx, h1, dproj, dn1), small = (res[0], res[1][0]) if gather_small else (res, small)
    dw_in = _matmul_tn(h1, dproj, D_MODEL, IN_COLS // 3, tw, f"dw_in{l}")
    dw_in = dw_in.reshape(D_MODEL, N_DEV, IN_SHARD).transpose(1, 0, 2)
    return dx, (dw_in, dw_out, dw_up, dw_down), (dn1.reshape(_NORM1_ROWS, 128), small), rode


def kernel(x, norm1, w_in, pool_w, pool_scale, sg_norm, sg_w, sg_b, w_out, norm2, w_up, w_down, final_norm, loss_target, m_norm1, m_w_in, m_pool_w, m_pool_scale, m_sg_norm, m_sg_w, m_sg_b, m_w_out, m_norm2, m_w_up, m_w_down, m_final_norm, v_norm1, v_w_in, v_pool_w, v_pool_scale, v_sg_norm, v_sg_w, v_sg_b, v_w_out, v_norm2, v_w_up, v_w_down, v_final_norm):
    t = x.shape[1]
    tm = _tiles(t)[0]
    small_w = (norm1, pool_w, pool_scale, sg_norm, sg_w, sg_b, norm2)
    big_w = (w_in, w_out, w_up, w_down)
    big_m = (m_w_in, m_w_out, m_w_up, m_w_down)
    big_v = (v_w_in, v_w_out, v_w_up, v_w_down)
    shards = [[w[l].astype(BF16) for w in big_w] for l in range(DEPTH)]

    wi0 = _full_w_in(_gather_via_sibling(shards[0][0], "gather_w_in0"))
    x1, s0, g1 = _layer_fwd(x.reshape(t, D_MODEL), wi0, Exchange(shards[0][1:2], False), Exchange(shards[0][2:], False),
                            tuple(w[0] for w in small_w), 0, ride_mlp=Exchange(shards[1][:2], False))
    x2, s1, _ = _layer_fwd(x1, _full_w_in(g1[0]), g1[1], Exchange(shards[1][2:], False), tuple(w[1] for w in small_w), 1)
    loss_local, dx, dfinal = _loss_grad(x2, final_norm[None, :], loss_target.reshape(t, D_MODEL), tm, "loss_grad")
    loss = lax.psum(loss_local[0, 0], MESH_AXES)

    dx, parts1, small1, _ = _layer_bwd(dx, s1, 1)
    early = jnp.concatenate(list(small1) + [dfinal.reshape(-1, 128)])
    dx, parts0, (dn1, small0), (recv_in1, early) = _layer_bwd(
        dx, s0, 0, ride_mlp=Exchange([parts1[0], early], [True, False]), gather_small=True)
    grad_x = dx.reshape(x.shape)
    recv_in0, dn1 = _exchange([parts0[0], dn1], [True, False], "scatter_w_in0_gather_norm1_0")
    small_all = jnp.concatenate([dn1, small0, early], axis=1)
    received = [[recv_in0] + list(parts0[1:]), [recv_in1] + list(parts1[1:])]

    big = [None] * 4
    for l in reversed(range(DEPTH)):
        for k in range(4):
            big[k] = _reduce_adamw(received[l][k], big_w[k], big_m[k], big_v[k], l, big[k], f"adamw{k}_{l}")

    sm = _reduce_adamw(
        small_all,
        _pack_small(small_w, final_norm)[None],
        _pack_small([m_norm1, m_pool_w, m_pool_scale, m_sg_norm, m_sg_w, m_sg_b, m_norm2], m_final_norm)[None],
        _pack_small([v_norm1, v_pool_w, v_pool_scale, v_sg_norm, v_sg_w, v_sg_b, v_norm2], v_final_norm)[None],
        0, None, "adamw_replicated")

    out = [loss, grad_x]
    for k in range(4):
        n1, pw, ps, sn, sw, sb, n2, fn = _unpack_small(sm[k][0])
        out += [n1, big[0][k], pw, ps, sn, sw, sb, big[1][k], n2, big[2][k], big[3][k], fn]
    return tuple(out)
```

```python
import functools

import jax
import jax.numpy as jnp
from jax import lax
from jax.experimental import pallas as pl
from jax.experimental.pallas import tpu as pltpu

F32 = jnp.float32
BF16 = jnp.bfloat16

D_MODEL = 1024
DEPTH = 2
POOL_WIDTH = 256
SG_WIDTH = 256
SB_WIDTH = 512
POOL_WINDOWS = (2, 4, 8, 16)
POOL_GW = 64
POOL_HALO = 16
CHUNK = 128
SG_HEADS = 4
SG_HD = 64
SB_HD = 64
SB_PAIRS = SB_WIDTH // 128
AB_COLS = POOL_WIDTH + 2 * SG_WIDTH
IN_COLS = AB_COLS + 3 * SB_WIDTH
D_FF = 4096
EPS = 1e-6
N_DEV = 8
FF_SHARD = D_FF // N_DEV
IN_SHARD = IN_COLS // N_DEV
OUT_SHARD = D_MODEL // N_DEV
ADAM_LR = 0.001
ADAM_B1 = 0.9
ADAM_B2 = 0.999
ADAM_EPS = 1e-08
ADAM_WD = 0.01
ADAM_STEP = 10
VMEM_LIMIT = 56 * 1024 * 1024
MLP_ROWS = 1024
MESH_AXES = ("x", "y", "c")


def _dot(a, b):
    return jnp.dot(a, b, preferred_element_type=F32)


def _dot_nt(a, b):
    return lax.dot_general(a, b, (((1,), (1,)), ((), ())), preferred_element_type=F32)


def _dot_tn(a, b):
    return lax.dot_general(a, b, (((0,), (0,)), ((), ())), preferred_element_type=F32)


def _rstd(x):
    return lax.rsqrt(jnp.mean(x * x, axis=-1, keepdims=True) + EPS)


def _rms_bwd(x, r, g, dh):
    gq = dh * g
    dx = r * gq - x * (r * r * r) * jnp.mean(gq * x, axis=-1, keepdims=True)
    dg = jnp.sum(dh * x * r, axis=0, keepdims=True)
    return dx, dg


def _params(sem=None):
    kw = dict(vmem_limit_bytes=VMEM_LIMIT)
    if sem is not None:
        kw["dimension_semantics"] = sem
    return pltpu.CompilerParams(**kw)


def _row_tile(rows, cap):
    best = 8
    for t in range(8, min(rows, cap) + 1, 8):
        if rows % t == 0:
            best = t
    return best


def _peer(k):
    x, y, c = lax.axis_index("x"), lax.axis_index("y"), lax.axis_index("c")
    return (1 - x if k & 4 else x, 1 - y if k & 2 else y, 1 - c if k & 1 else c)


def _my_index():
    return 4 * lax.axis_index("x") + 2 * lax.axis_index("y") + lax.axis_index("c")


class Exchange:
    def __init__(self, arrs, scatter):
        self.arrs = list(arrs)
        self.scatter = list(scatter) if isinstance(scatter, (list, tuple)) else [scatter] * len(self.arrs)
        self.n = len(self.arrs)
        self.any_specs = [pl.BlockSpec(memory_space=pl.ANY)] * self.n
        self.out_shape = [jax.ShapeDtypeStruct((N_DEV,) + a.shape[-2:], a.dtype) for a in self.arrs]
        self.sems = [pltpu.SemaphoreType.DMA((self.n, N_DEV - 1)), pltpu.SemaphoreType.DMA((self.n, N_DEV - 1)),
                     pltpu.SemaphoreType.DMA((self.n,))]

    def _copies(self, ins, outs, sems):
        send_sems, recv_sems, local_sems = sems
        me = _my_index()
        local, remote = [], []
        for a in range(self.n):
            sc = self.scatter[a]
            local.append(pltpu.make_async_copy(ins[a].at[me] if sc else ins[a], outs[a].at[me], local_sems.at[a]))
            for k in range(1, N_DEV):
                px, py, pc = _peer(k)
                remote.append(pltpu.make_async_remote_copy(
                    src_ref=ins[a].at[4 * px + 2 * py + pc] if sc else ins[a], dst_ref=outs[a].at[me],
                    send_sem=send_sems.at[a, k - 1], recv_sem=recv_sems.at[a, k - 1],
                    device_id=(px, py, pc), device_id_type=pl.DeviceIdType.MESH))
        return local, remote

    def start(self, ins, outs, sems):
        local, remote = self._copies(ins, outs, sems)
        for cp in local + remote:
            cp.start()

    def wait(self, ins, outs, sems):
        local, remote = self._copies(ins, outs, sems)
        for cp in remote:
            cp.wait_recv()
        for cp in remote:
            cp.wait_send()
        for cp in local:
            cp.wait()

    def alone(self, name):
        n = self.n

        def body(*refs):
            ins, outs, sems = refs[:n], refs[n:2 * n], refs[2 * n:]
            self.start(ins, outs, sems)
            self.wait(ins, outs, sems)

        return pl.pallas_call(body, name=name, out_shape=self.out_shape, in_specs=self.any_specs,
                              out_specs=self.any_specs, scratch_shapes=self.sems)(*self.arrs)


def _exchange(arrs, scatter, name):
    return Exchange(arrs, scatter).alone(name)


def _gather_via_sibling(shard, name):
    r, c = shard.shape

    def body(x_ref, out_ref, send_sems, recv_sems, local_sem):
        x, y, core = lax.axis_index("x"), lax.axis_index("y"), lax.axis_index("c")
        me, sibling = (x, y, core), (x, y, 1 - core)
        chips = [(1 - x, y), (x, 1 - y), (1 - x, 1 - y)]

        def slot(px, py, pc):
            return out_ref.at[4 * px + 2 * py + pc]

        def copy(k, block, to, src=None):
            return pltpu.make_async_remote_copy(
                src_ref=slot(*block) if src is None else src, dst_ref=slot(*block),
                send_sem=send_sems.at[k], recv_sem=recv_sems.at[k], device_id=to, device_id_type=pl.DeviceIdType.MESH)

        mine = pltpu.make_async_copy(x_ref, slot(*me), local_sem)
        mine.start()
        first = [copy(0, me, sibling, src=x_ref)] + [copy(1 + j, me, (*chip, core), src=x_ref) for j, chip in enumerate(chips)]
        for cp in first:
            cp.start()
        passed = [copy(4 + j, (*chip, core), sibling) for j, chip in enumerate(chips)]
        for j, chip in enumerate(chips):
            copy(1 + j, (*chip, core), me).wait_recv()
            passed[j].start()
        copy(0, sibling, me).wait_recv()
        for j, chip in enumerate(chips):
            copy(4 + j, (*chip, 1 - core), me).wait_recv()
        for cp in first + passed:
            cp.wait_send()
        mine.wait()

    return pl.pallas_call(
        body, name=name, out_shape=jax.ShapeDtypeStruct((N_DEV, r, c), shard.dtype),
        in_specs=[pl.BlockSpec(memory_space=pltpu.VMEM)], out_specs=pl.BlockSpec(memory_space=pltpu.VMEM),
        scratch_shapes=[pltpu.SemaphoreType.DMA((N_DEV - 1,)), pltpu.SemaphoreType.DMA((N_DEV - 1,)),
                        pltpu.SemaphoreType.DMA],
        compiler_params=pltpu.CompilerParams(vmem_limit_bytes=VMEM_LIMIT),
    )(shard)


def _call(body, name, grid, in_specs, out_specs, out_shape, scratch_shapes, semantics, args, ride=None):
    if ride is None:
        return pl.pallas_call(body, name=name, grid=grid, in_specs=in_specs, out_specs=out_specs, out_shape=out_shape,
                              scratch_shapes=scratch_shapes, compiler_params=_params(semantics))(*args)
    single = not isinstance(out_shape, (list, tuple))
    out_specs, out_shape = ([out_specs], [out_shape]) if single else (list(out_specs), list(out_shape))
    n_in, n_out, n_scr, n = len(in_specs), len(out_specs), len(scratch_shapes), ride.n

    def riding(*refs):
        ins, cins = refs[:n_in], refs[n_in:n_in + n]
        outs, couts = refs[n_in + n:n_in + n + n_out], refs[n_in + n + n_out:n_in + 2 * n + n_out]
        scr, sems = refs[n_in + 2 * n + n_out:n_in + 2 * n + n_out + n_scr], refs[n_in + 2 * n + n_out + n_scr:]
        first = functools.reduce(lambda p, q: p & q, [pl.program_id(a) == 0 for a in range(len(grid))])
        last = functools.reduce(lambda p, q: p & q, [pl.program_id(a) == g - 1 for a, g in enumerate(grid)])

        @pl.when(first)
        def _():
            ride.start(cins, couts, sems)

        body(*ins, *outs, *scr)

        @pl.when(last)
        def _():
            ride.wait(cins, couts, sems)

    res = pl.pallas_call(
        riding, name=name, grid=grid, in_specs=list(in_specs) + ride.any_specs, out_specs=out_specs + ride.any_specs,
        out_shape=out_shape + ride.out_shape, scratch_shapes=list(scratch_shapes) + ride.sems,
        compiler_params=_params(("arbitrary",) * len(grid)))(*args, *ride.arrs)
    own = res[0] if single else list(res[:n_out])
    return own, list(res[n_out:])


def _reduce_adamw(parts, w, m, v, l, prev, name):
    _, rows, n = parts.shape
    tr = _row_tile(rows, max(8, (1 << 18) // n))
    c1 = 1.0 - ADAM_B1 ** ADAM_STEP
    c2 = 1.0 - ADAM_B2 ** ADAM_STEP

    def body(p_ref, w_ref, m_ref, v_ref, *rest):
        g_ref, d_ref, nm_ref, nv_ref = rest[-4:]
        g = p_ref[0].astype(F32)
        for s in range(1, N_DEV):
            g = g + p_ref[s].astype(F32)
        nm = ADAM_B1 * m_ref[...] + (1.0 - ADAM_B1) * g
        nv = ADAM_B2 * v_ref[...] + (1.0 - ADAM_B2) * (g * g)
        m_hat = nm / c1
        v_hat = nv / c2
        g_ref[...] = g
        d_ref[...] = -ADAM_LR * (m_hat / (jnp.sqrt(v_hat) + ADAM_EPS) + ADAM_WD * w_ref[...])
        nm_ref[...] = nm
        nv_ref[...] = nv

    blk = pl.BlockSpec((None, tr, n), lambda i: (l, i, 0))
    out = jax.ShapeDtypeStruct(w.shape, F32)
    prev = list(prev) if prev is not None else []
    return pl.pallas_call(
        body, name=name, grid=(rows // tr,),
        in_specs=[pl.BlockSpec((N_DEV, tr, n), lambda i: (0, i, 0)), blk, blk, blk] + [pl.BlockSpec(memory_space=pl.ANY)] * len(prev),
        out_specs=[blk, blk, blk, blk], out_shape=[out, out, out, out],
        input_output_aliases={4 + k: k for k in range(len(prev))},
        compiler_params=_params(("parallel",)),
    )(parts, w, m, v, *prev)


def _inproj_fwd(x, g, w, tm, tq, name, ride=None):
    t, d = x.shape
    n = w.shape[1]
    nb = t // tq
    per = tm // tq

    def body(x_ref, g_ref, w_ref, ab_ref, qkv_ref, kt_ref, vt_ref):
        xx = x_ref[...]
        h = (xx * _rstd(xx) * g_ref[...]).astype(BF16)
        ab_ref[...] = _dot(h, w_ref[:, :AB_COLS])
        qkv = _dot(h, w_ref[:, AB_COLS:])
        qkv_ref[...] = qkv.astype(BF16)
        for which, out_ref in ((1, kt_ref), (2, vt_ref)):
            for p in range(SB_PAIRS):
                for b in range(per):
                    cols = which * SB_WIDTH + p * 128
                    out_ref[p, b] = qkv[b * tq:(b + 1) * tq, cols:cols + 128].T.astype(BF16)

    tb = pl.BlockSpec((SB_PAIRS, per, 128, tq), lambda i: (0, i, 0, 0))
    tshape = jax.ShapeDtypeStruct((SB_PAIRS, nb, 128, tq), BF16)
    return _call(
        body, name, (t // tm,),
        [pl.BlockSpec((tm, d), lambda i: (i, 0)), pl.BlockSpec((1, d), lambda i: (0, 0)),
         pl.BlockSpec((d, n), lambda i: (0, 0))],
        [pl.BlockSpec((tm, AB_COLS), lambda i: (i, 0)), pl.BlockSpec((tm, n - AB_COLS), lambda i: (i, 0)), tb, tb],
        [jax.ShapeDtypeStruct((t, AB_COLS), F32), jax.ShapeDtypeStruct((t, n - AB_COLS), BF16), tshape, tshape],
        [], ("parallel",), (x, g, w), ride)


def _pool_window_sums(xx, forward):
    n = xx.shape[0]
    sh = (lambda k: n - k) if forward else (lambda k: k)
    s2 = xx + pltpu.roll(xx, sh(1), 0)
    s4 = s2 + pltpu.roll(s2, sh(2), 0)
    s8 = s4 + pltpu.roll(s4, sh(4), 0)
    s16 = s8 + pltpu.roll(s8, sh(8), 0)
    grp = lax.broadcasted_iota(jnp.int32, (1, POOL_WIDTH), 1) // POOL_GW
    return jnp.where(grp == 0, s2, jnp.where(grp == 1, s4, jnp.where(grp == 2, s8, s16)))


def _pool_count(t0, rows):
    grp = lax.broadcasted_iota(jnp.int32, (1, POOL_WIDTH), 1) // POOL_GW
    win = jnp.where(grp == 0, 2, jnp.where(grp == 1, 4, jnp.where(grp == 2, 8, 16)))
    tt = t0 + lax.broadcasted_iota(jnp.int32, (rows, 1), 0)
    return jnp.minimum(tt + 1, win).astype(F32)


def _pool_diff(cur, prev, t0):
    tm = cur.shape[0]
    sums = _pool_window_sums(jnp.concatenate([prev, cur], axis=0), False)[POOL_HALO:]
    return sums / _pool_count(t0, tm) - cur


def _pool_fwd(ab, wbd, scale, tm, name):
    t = ab.shape[0]
    hb = tm // POOL_HALO

    def body(cur_ref, prev_ref, w_ref, s_ref, y_ref):
        i = pl.program_id(0)
        prev = jnp.where(i == 0, 0.0, prev_ref[...])
        d = _pool_diff(cur_ref[...], prev, i * tm).astype(BF16)
        y_ref[...] = (_dot(d, w_ref[...]) * s_ref[...]).astype(BF16)

    return pl.pallas_call(
        body, name=name, grid=(t // tm,),
        in_specs=[pl.BlockSpec((tm, POOL_WIDTH), lambda i: (i, 0)),
                  pl.BlockSpec((POOL_HALO, POOL_WIDTH), lambda i: (jnp.maximum(i * hb - 1, 0), 0)),
                  pl.BlockSpec((POOL_WIDTH, POOL_WIDTH), lambda i: (0, 0)),
                  pl.BlockSpec((1, POOL_WIDTH), lambda i: (0, 0))],
        out_specs=pl.BlockSpec((tm, POOL_WIDTH), lambda i: (i, 0)),
        out_shape=jax.ShapeDtypeStruct((t, POOL_WIDTH), BF16),
        compiler_params=_params(("parallel",)),
    )(ab, ab, wbd, scale)


_GELU_K = 0.7978845608028654
_GELU_A = 0.044715


def _gelu(x):
    return 0.5 * x * (1.0 + jnp.tanh(_GELU_K * (x + _GELU_A * (x * x * x))))


def _gelu_grad(x):
    th = jnp.tanh(_GELU_K * (x + _GELU_A * (x * x * x)))
    return 0.5 * (1.0 + th) + 0.5 * x * (1.0 - th * th) * (_GELU_K * (1.0 + 3.0 * _GELU_A * (x * x)))


def _head_lanes(h):
    return lax.broadcasted_iota(jnp.int32, (1, SG_WIDTH), 1) // SG_HD == h


def _sg_fwd(ab, gn, wm, bfull, tm, name):
    t = ab.shape[0]

    def body(u_ref, v_ref, gn_ref, wm_ref, b_ref, y_ref):
        v = _gelu(v_ref[...])
        vn = (v * _rstd(v) * gn_ref[...]).astype(BF16)
        for c in range(tm // CHUNK):
            rows = slice(c * CHUNK, (c + 1) * CHUNK)
            vc = vn[rows]
            sv = b_ref[...]
            for h in range(SG_HEADS):
                sv = sv + jnp.where(_head_lanes(h), _dot(wm_ref[h], vc), 0.0)
            y_ref[rows, :] = (_gelu(u_ref[rows, :]) * sv).astype(BF16)

    return pl.pallas_call(
        body, name=name, grid=(t // tm,),
        in_specs=[pl.BlockSpec((tm, SG_WIDTH), lambda i: (i, 1)), pl.BlockSpec((tm, SG_WIDTH), lambda i: (i, 2)),
                  pl.BlockSpec((1, SG_WIDTH), lambda i: (0, 0)),
                  pl.BlockSpec((SG_HEADS, CHUNK, CHUNK), lambda i: (0, 0, 0)),
                  pl.BlockSpec((CHUNK, SG_WIDTH), lambda i: (0, 0))],
        out_specs=pl.BlockSpec((tm, SG_WIDTH), lambda i: (i, 0)),
        out_shape=jax.ShapeDtypeStruct((t, SG_WIDTH), BF16),
        compiler_params=_params(("parallel",)),
    )(ab, ab, gn, wm, bfull)


LOG2E = 1.4426950408889634
SB_SCALE = 0.125 * LOG2E
SB_DEAD_LOG2 = 152.0
SB_QUERY_BLOCKS_PER_STEP = 2


def _log2_sigmoids(y):
    neg_abs = lax.bitcast_convert_type(lax.bitcast_convert_type(y, jnp.uint32) | jnp.uint32(0x80000000), F32)
    lb = jnp.minimum(y, 0.0) - jnp.log(1.0 + jnp.exp2(neg_abs)) * LOG2E
    return lb, lb - y


def _split(x):
    hi = x.astype(BF16)
    return hi, (x - hi.astype(F32)).astype(BF16)


def _tri_dot(tri, x):
    hi, lo = _split(x)
    return _dot(tri, hi) + _dot(tri, lo)


def _sba_fwd(qkv, vtb, tq, name, ride=None):
    t = qkv.shape[0]
    nb = t // tq
    qb = SB_QUERY_BLOCKS_PER_STEP
    upper = (jnp.arange(tq)[None, :] > jnp.arange(tq)[:, None]).astype(BF16)

    def body(q_ref, k_ref, vt_ref, up_ref, ot_ref, c_ref, n_ref):
        pair, step = pl.program_id(0), pl.program_id(1)
        refs = (q_ref, k_ref, vt_ref, up_ref, ot_ref, c_ref, n_ref)
        subs = list(range(qb))

        @pl.when(step == 0)
        def _():
            for sub in subs:
                query_blocks([sub], pair, step, *refs)

        @pl.when(step > 0)
        def _():
            query_blocks(subs, pair, step, *refs)

    def query_blocks(subs, pair, step, q_ref, k_ref, vt_ref, up_ref, ot_ref, c_ref, n_ref):
        index = {sub: step * qb + sub for sub in subs}
        mine = {sub: slice(sub * tq, (sub + 1) * tq) for sub in subs}
        up = up_ref[...]
        lane_head = lax.broadcasted_iota(jnp.int32, (1, 128), 1) // SB_HD
        sub_head = lax.broadcasted_iota(jnp.int32, (128, 1), 0) // SB_HD
        causal = (lax.broadcasted_iota(jnp.int32, (tq, tq), 0) < lax.broadcasted_iota(jnp.int32, (tq, tq), 1))
        qh = {}
        for sub in subs:
            q = q_ref[mine[sub], :]
            qh[sub] = [jnp.where(lane_head == h, q, jnp.zeros_like(q)) for h in range(2)]

        def blocks(work, carry):
            cs = {sub: list(carry[sub][:2]) for sub in carry}
            acc = {sub: carry[sub][2] for sub in carry}
            kj = [k_ref[pl.ds(pl.multiple_of(j * tq, tq), tq), :] for _, j, _ in work]
            vt = [vt_ref[j] for _, j, _ in work]
            chains = [(w, h) for w in range(len(work)) for h in range(2)]
            z = [_dot_nt(kj[w], qh[work[w][0]][h]) for w, h in chains]
            ls = [_log2_sigmoids(zz * SB_SCALE) for zz in z]
            lb = [x[0] for x in ls]
            l1 = [jnp.where(causal, x[1], 0.0) if work[w][2] else x[1] for x, (w, h) in zip(ls, chains)]
            after = [_tri_dot(up, x) for x in l1]
            a = []
            for n, (w, h) in enumerate(chains):
                sub, j, diag = work[w]
                c_ref[h, pl.ds(j, 1), mine[sub]] = cs[sub][h]
                an = jnp.exp2(lb[n] + after[n] + cs[sub][h])
                a.append(jnp.where(causal, an, 0.0) if diag else an)
                cs[sub][h] = cs[sub][h] + after[n][0:1, :] + l1[n][0:1, :]
            alive = {sub: (jnp.max(jnp.maximum(*cs[sub])) > -SB_DEAD_LOG2).astype(jnp.int32) for sub in cs}
            for n, (w, h) in enumerate(chains):
                sub = work[w][0]
                acc[sub] = acc[sub] + _dot(jnp.where(sub_head == h, vt[w], jnp.zeros_like(vt[w])), a[n].astype(BF16))
            return {sub: (alive[sub], cs[sub][0], cs[sub][1], acc[sub]) for sub in cs}

        zero = jnp.zeros((1, tq), F32)
        start = {sub: (zero, zero, jnp.zeros((128, tq), F32)) for sub in subs}
        both = lambda: blocks([(sub, index[sub] - b, b == 0) for sub in subs for b in range(2)], start)
        if len(subs) == 1:
            joint = lax.cond(index[subs[0]] > 0, both, lambda: blocks([(subs[0], index[subs[0]], True)], start))
        else:
            joint = both()
        for sub in subs:
            i = index[sub]

            def left(state, sub=sub, i=i):
                s, _, c0, c1, acc = state
                return (s + 1,) + blocks([(sub, i - 1 - s, False)], {sub: (c0, c1, acc)})[sub]

            state = lax.while_loop(lambda st, i=i: (st[0] < i) & (st[1] > 0), left, (jnp.minimum(i, 1),) + tuple(joint[sub]))
            ot_ref[:, mine[sub]] = state[4].astype(BF16)
            n_ref[pair, i] = (state[0] + 1).astype(F32)

    return _call(
        body, name, (SB_PAIRS, nb // qb),
        [pl.BlockSpec((qb * tq, 128), lambda p, i: (i, p)),
         pl.BlockSpec((t, 128), lambda p, i: (0, SB_PAIRS + p)),
         pl.BlockSpec((None, nb, 128, tq), lambda p, i: (p, 0, 0, 0)),
         pl.BlockSpec((tq, tq), lambda p, i: (0, 0))],
        [pl.BlockSpec((128, qb * tq), lambda p, i: (p, i)),
         pl.BlockSpec((2, nb, qb * tq), lambda p, i: (p, 0, i)),
         pl.BlockSpec(memory_space=pltpu.SMEM)],
        [jax.ShapeDtypeStruct((SB_WIDTH, t), BF16),
         jax.ShapeDtypeStruct((2 * SB_PAIRS, nb, t), F32),
         jax.ShapeDtypeStruct((SB_PAIRS, nb), F32)],
        [], ("arbitrary", "arbitrary"), (qkv, qkv, vtb, upper), ride)


def _outproj_mlp_fwd(x, ya, yb, yct, wo, g2, wup, wdn, tm, name, ride=None):
    t, d = x.shape
    nf = wup.shape[0]

    def body(x_ref, ya_ref, yb_ref, yct_ref, wo_ref, g_ref, wu_ref, wd_ref, x1_ref, u_ref, x2_ref, h_ref, acc_ref):
        j = pl.program_id(1)

        @pl.when(j == 0)
        def _():
            x1 = (x_ref[...] + _dot(ya_ref[...], wo_ref[0:POOL_WIDTH, :])
                  + _dot(yb_ref[...], wo_ref[POOL_WIDTH:POOL_WIDTH + SG_WIDTH, :])
                  + _dot_tn(yct_ref[...], wo_ref[POOL_WIDTH + SG_WIDTH:, :]))
            x1_ref[...] = x1
            h_ref[...] = (x1 * _rstd(x1) * g_ref[...]).astype(BF16)
            acc_ref[...] = x1

        u = _dot(h_ref[...], wu_ref[...])
        u_ref[...] = u
        r = jnp.maximum(u, 0.0)
        acc_ref[...] += _dot((r * r).astype(BF16), wd_ref[...])

        @pl.when(j == nf - 1)
        def _():
            x2_ref[...] = acc_ref[...]

    row = lambda w: pl.BlockSpec((tm, w), lambda i, j: (i, 0))
    return _call(
        body, name, (t // tm, nf),
        [row(d), row(POOL_WIDTH), row(SG_WIDTH), pl.BlockSpec((SB_WIDTH, tm), lambda i, j: (0, i)),
         pl.BlockSpec((d, d), lambda i, j: (0, 0)), pl.BlockSpec((1, d), lambda i, j: (0, 0)),
         pl.BlockSpec((None, d, FF_SHARD), lambda i, j: (j, 0, 0)),
         pl.BlockSpec((None, FF_SHARD, d), lambda i, j: (j, 0, 0))],
        [row(d), pl.BlockSpec((tm, FF_SHARD), lambda i, j: (i, j)), row(d)],
        [jax.ShapeDtypeStruct((t, d), F32), jax.ShapeDtypeStruct((t, nf * FF_SHARD), F32),
         jax.ShapeDtypeStruct((t, d), F32)],
        [pltpu.VMEM((tm, d), BF16), pltpu.VMEM((tm, d), F32)],
        ("parallel", "arbitrary"), (x, ya, yb, yct, wo, g2, wup, wdn), ride)


def _loss_grad(x, g, target, tm, name):
    t, d = x.shape
    nt = t // tm

    def body(x_ref, g_ref, t_ref, loss_ref, dx_ref, dg_ref, sq_ref):
        i = pl.program_id(0)

        @pl.when(i == 0)
        def _():
            sq_ref[...] = jnp.zeros_like(sq_ref)
            dg_ref[...] = jnp.zeros_like(dg_ref)

        xx = x_ref[...]
        r = _rstd(xx)
        err = xx * r * g_ref[...] - t_ref[...]
        sq_ref[...] += jnp.sum(err * err, axis=0, keepdims=True)
        dx, dg = _rms_bwd(xx, r, g_ref[...], err * (1.0 / d))
        dx_ref[...] = dx
        dg_ref[...] += dg

        @pl.when(i == nt - 1)
        def _():
            loss_ref[...] = jnp.sum(sq_ref[...], axis=1, keepdims=True) * (0.5 / d)

    return pl.pallas_call(
        body, name=name, grid=(nt,),
        in_specs=[pl.BlockSpec((tm, d), lambda i: (i, 0)), pl.BlockSpec((1, d), lambda i: (0, 0)),
                  pl.BlockSpec((tm, d), lambda i: (i, 0))],
        out_specs=[pl.BlockSpec((1, 1), lambda i: (0, 0)), pl.BlockSpec((tm, d), lambda i: (i, 0)),
                   pl.BlockSpec((1, d), lambda i: (0, 0))],
        out_shape=[jax.ShapeDtypeStruct((1, 1), F32), jax.ShapeDtypeStruct((t, d), F32),
                   jax.ShapeDtypeStruct((1, d), F32)],
        scratch_shapes=[pltpu.VMEM((1, d), F32)],
        compiler_params=_params(("arbitrary",)),
    )(x, g, target)


def _mlp_bwd(dx2, x1, g2, u, wup, wdn, tm, name, ride=None):
    t, d = dx2.shape
    nf = wup.shape[0]
    nt = t // tm

    def body(dx2_ref, x1_ref, g_ref, u_ref, wu_ref, wd_ref, dx1_ref, du_ref, r_ref, h_ref, dxb_ref, dg_ref, acc_ref):
        i, j = pl.program_id(0), pl.program_id(1)

        @pl.when(j == 0)
        def _():
            x1 = x1_ref[...]
            h_ref[...] = (x1 * _rstd(x1) * g_ref[...]).astype(BF16)
            dxb_ref[...] = dx2_ref[...].astype(BF16)
            acc_ref[...] = jnp.zeros_like(acc_ref)

        @pl.when((i == 0) & (j == 0))
        def _():
            dg_ref[...] = jnp.zeros_like(dg_ref)

        dr = _dot_nt(dxb_ref[...], wd_ref[...])
        ru = jnp.maximum(u_ref[...], 0.0)
        du = (dr * (2.0 * ru)).astype(BF16)
        du_ref[...] = du
        r_ref[...] = (ru * ru).astype(BF16)
        acc_ref[...] += _dot_nt(du, wu_ref[...])

        @pl.when(j == nf - 1)
        def _():
            x1 = x1_ref[...]
            dx, dg = _rms_bwd(x1, _rstd(x1), g_ref[...], acc_ref[...])
            dx1_ref[...] = dx2_ref[...] + dx
            dg_ref[...] += dg

    row = lambda w: pl.BlockSpec((tm, w), lambda i, j: (i, 0))
    col = pl.BlockSpec((tm, FF_SHARD), lambda i, j: (i, j))
    return _call(
        body, name, (nt, nf),
        [row(d), row(d), pl.BlockSpec((1, d), lambda i, j: (0, 0)), col,
         pl.BlockSpec((None, d, FF_SHARD), lambda i, j: (j, 0, 0)),
         pl.BlockSpec((None, FF_SHARD, d), lambda i, j: (j, 0, 0))],
        [row(d), col, col, row(d), row(d), pl.BlockSpec((1, d), lambda i, j: (0, 0))],
        [jax.ShapeDtypeStruct((t, d), F32), jax.ShapeDtypeStruct((t, nf * FF_SHARD), BF16),
         jax.ShapeDtypeStruct((t, nf * FF_SHARD), BF16), jax.ShapeDtypeStruct((t, d), BF16),
         jax.ShapeDtypeStruct((t, d), BF16), jax.ShapeDtypeStruct((1, d), F32)],
        [pltpu.VMEM((tm, d), F32)], ("arbitrary", "arbitrary"), (dx2, x1, g2, u, wup, wdn), ride)


def _matmul_tn(a, b, bm, bn, bt, name, by_column_block=0, a_transposed=False):
    m, t = a.shape if a_transposed else a.shape[::-1]
    n = b.shape[1]
    nk = t // bt

    def body(a_ref, b_ref, o_ref, acc_ref):
        k = pl.program_id(2)

        @pl.when(k == 0)
        def _():
            acc_ref[...] = jnp.zeros_like(acc_ref)

        acc_ref[...] += _dot(a_ref[...], b_ref[...]) if a_transposed else _dot_tn(a_ref[...], b_ref[...])

        @pl.when(k == nk - 1)
        def _():
            if by_column_block:
                for c in range(bn // by_column_block):
                    o_ref[c] = acc_ref[:, c * by_column_block:(c + 1) * by_column_block].astype(BF16)
            else:
                o_ref[...] = acc_ref[...].astype(BF16)

    if by_column_block:
        out_spec = pl.BlockSpec((bn // by_column_block, bm, by_column_block), lambda i, j, k: (j, i, 0))
        out_shape = jax.ShapeDtypeStruct((n // by_column_block, m, by_column_block), BF16)
    else:
        out_spec = pl.BlockSpec((bm, bn), lambda i, j, k: (i, j))
        out_shape = jax.ShapeDtypeStruct((m, n), BF16)
    a_spec = (pl.BlockSpec((bm, bt), lambda i, j, k: (i, k)) if a_transposed
              else pl.BlockSpec((bt, bm), lambda i, j, k: (k, i)))
    return _call(body, name, (m // bm, n // bn, nk), [a_spec, pl.BlockSpec((bt, bn), lambda i, j, k: (k, j))],
                 out_spec, out_shape, [pltpu.VMEM((bm, bn), F32)], ("parallel", "parallel", "arbitrary"), (a, b))


def _outproj_bwd(dx1, wo, tm, name):
    t, d = dx1.shape
    c2 = POOL_WIDTH + SG_WIDTH

    def body(dx_ref, wo_ref, dya_ref, dyb_ref, dyc_ref, dyct_ref, dxb_ref):
        dxb = dx_ref[...].astype(BF16)
        dxb_ref[...] = dxb
        dya_ref[...] = _dot_nt(dxb, wo_ref[0:POOL_WIDTH, :])
        dyb_ref[...] = _dot_nt(dxb, wo_ref[POOL_WIDTH:c2, :])
        dyc_ref[...] = _dot_nt(dxb, wo_ref[c2:, :]).astype(BF16)
        dyct_ref[...] = _dot_nt(wo_ref[c2:, :], dxb).astype(BF16)

    row = lambda w: pl.BlockSpec((tm, w), lambda i: (i, 0))
    return pl.pallas_call(
        body, name=name, grid=(t // tm,),
        in_specs=[row(d), pl.BlockSpec((d, d), lambda i: (0, 0))],
        out_specs=[row(POOL_WIDTH), row(SG_WIDTH), row(SB_WIDTH), pl.BlockSpec((SB_WIDTH, tm), lambda i: (0, i)), row(d)],
        out_shape=[jax.ShapeDtypeStruct((t, POOL_WIDTH), F32), jax.ShapeDtypeStruct((t, SG_WIDTH), F32),
                   jax.ShapeDtypeStruct((t, SB_WIDTH), BF16), jax.ShapeDtypeStruct((SB_WIDTH, t), BF16),
                   jax.ShapeDtypeStruct((t, d), BF16)],
        compiler_params=_params(("parallel",)),
    )(dx1, wo)


def _sba_bwd(qkv, ktb, dyc, dyct, cc, nvis, tq, name, ride=None):
    t = qkv.shape[0]
    nb = t // tq
    idx = jnp.arange(tq)
    upper = (idx[None, :] > idx[:, None]).astype(BF16)
    lower = (idx[None, :] < idx[:, None]).astype(BF16)

    qb = SB_QUERY_BLOCKS_PER_STEP

    def body(q_ref, k_ref, v_ref, kt_ref, do_ref, dot_ref, c_ref, n_ref, up_ref, lo_ref, dqt_ref, dk_ref, dv_ref):
        pair, step = pl.program_id(0), pl.program_id(1)
        refs = (q_ref, k_ref, v_ref, kt_ref, do_ref, dot_ref, c_ref, n_ref, up_ref, lo_ref, dqt_ref, dk_ref, dv_ref)
        subs = list(range(qb))

        @pl.when(step == 0)
        def _():
            dk_ref[...] = jnp.zeros_like(dk_ref)
            dv_ref[...] = jnp.zeros_like(dv_ref)
            for sub in subs:
                query_blocks([sub], pair, step, *refs)

        @pl.when(step > 0)
        def _():
            query_blocks(subs, pair, step, *refs)

    def query_blocks(subs, pair, step, q_ref, k_ref, v_ref, kt_ref, do_ref, dot_ref, c_ref, n_ref, up_ref, lo_ref,
                     dqt_ref, dk_ref, dv_ref):
        index = {sub: step * qb + sub for sub in subs}
        mine = {sub: slice(sub * tq, (sub + 1) * tq) for sub in subs}
        up = up_ref[...]
        lo = lo_ref[...]
        lane_head = lax.broadcasted_iota(jnp.int32, (1, 128), 1) // SB_HD
        sub_head = lax.broadcasted_iota(jnp.int32, (128, 1), 0) // SB_HD
        causal = (lax.broadcasted_iota(jnp.int32, (tq, tq), 0) < lax.broadcasted_iota(jnp.int32, (tq, tq), 1))
        hms = [lane_head == h for h in range(2)]
        qh, qs, doh, dot = {}, {}, {}, {}
        for sub in subs:
            q, do = q_ref[mine[sub], :], do_ref[mine[sub], :]
            qh[sub] = [jnp.where(hm, q, jnp.zeros_like(q)) for hm in hms]
            qs[sub] = [x * 0.125 for x in qh[sub]]
            doh[sub] = [jnp.where(hm, do, jnp.zeros_like(do)) for hm in hms]
            dot[sub] = dot_ref[:, mine[sub]]

        def blocks(work, carry):
            cgs = {sub: list(carry[sub][:2]) for sub in carry}
            dqt = {sub: carry[sub][2] for sub in carry}
            rows = [pl.ds(pl.multiple_of(j * tq, tq), tq) for _, j, _ in work]
            kj = [k_ref[r, :] for r in rows]
            vj = [v_ref[r, :] for r in rows]
            kt = [kt_ref[j] for _, j, _ in work]
            chains = [(w, h) for w in range(len(work)) for h in range(2)]
            z = [_dot_nt(kj[w], qh[work[w][0]][h]) for w, h in chains]
            da = [_dot(jnp.where(hms[h], vj[w], jnp.zeros_like(vj[w])), dot[work[w][0]]) for w, h in chains]
            ls = [_log2_sigmoids(zz * SB_SCALE) for zz in z]
            lb = [x[0] for x in ls]
            l1 = [jnp.where(causal, x[1], 0.0) if work[w][2] else x[1] for x, (w, h) in zip(ls, chains)]
            after = [_tri_dot(up, x) for x in l1]
            a = [jnp.exp2(lb[n] + after[n] + c_ref[h, pl.ds(work[w][1], 1), mine[work[w][0]]])
                 for n, (w, h) in enumerate(chains)]
            a = [jnp.where(causal, a[n], 0.0) if work[w][2] else a[n] for n, (w, h) in enumerate(chains)]
            g = [a[n] * da[n] for n in range(len(chains))]
            gloc = [_dot(lo, x.astype(BF16)) for x in g]
            dzb = []
            for n, (w, h) in enumerate(chains):
                sub, _, diag = work[w]
                gsum = gloc[n] + cgs[sub][h]
                dz = g[n] - jnp.exp2(lb[n]) * (g[n] + gsum)
                dzb.append((jnp.where(causal, dz, 0.0) if diag else dz).astype(BF16))
                cgs[sub][h] = gsum[tq - 1:tq, :] + g[n][tq - 1:tq, :]
            ab = [x.astype(BF16) for x in a]
            for n, (w, h) in enumerate(chains):
                sub = work[w][0]
                dqt[sub] = dqt[sub] + _dot(jnp.where(sub_head == h, kt[w], jnp.zeros_like(kt[w])), dzb[n])
            for w, (sub, _, _) in enumerate(work):
                dk_ref[rows[w], :] += _dot(dzb[2 * w], qs[sub][0]) + _dot(dzb[2 * w + 1], qs[sub][1])
                dv_ref[rows[w], :] += _dot(ab[2 * w], doh[sub][0]) + _dot(ab[2 * w + 1], doh[sub][1])
            return {sub: (cgs[sub][0], cgs[sub][1], dqt[sub]) for sub in cgs}

        zero = jnp.zeros((1, tq), F32)
        carry = {}
        for sub in subs:
            i = index[sub]
            n = jnp.clip(n_ref[pair, i].astype(jnp.int32), 1, i + 1)
            carry[sub] = lax.fori_loop(jnp.minimum(i + 1 - n, i - 1), i - 1,
                                       lambda s, cr, sub=sub: blocks([(sub, s, False)], {sub: cr})[sub],
                                       (zero, zero, jnp.zeros((128, tq), F32)))
        both = lambda: blocks([(sub, index[sub] - 1 + b, b == 1) for sub in subs for b in range(2)], carry)
        if len(subs) == 1:
            carry = lax.cond(index[subs[0]] > 0, both, lambda: blocks([(subs[0], index[subs[0]], True)], carry))
        else:
            carry = both()
        for sub in subs:
            dqt_ref[:, mine[sub]] = carry[sub][2] * 0.125

    return _call(
        body, name, (SB_PAIRS, nb // qb),
        [pl.BlockSpec((qb * tq, 128), lambda p, i: (i, p)),
         pl.BlockSpec((t, 128), lambda p, i: (0, SB_PAIRS + p)),
         pl.BlockSpec((t, 128), lambda p, i: (0, 2 * SB_PAIRS + p)),
         pl.BlockSpec((None, nb, 128, tq), lambda p, i: (p, 0, 0, 0)),
         pl.BlockSpec((qb * tq, 128), lambda p, i: (i, p)),
         pl.BlockSpec((128, qb * tq), lambda p, i: (p, i)),
         pl.BlockSpec((2, nb, qb * tq), lambda p, i: (p, 0, i)),
         pl.BlockSpec(memory_space=pltpu.SMEM),
         pl.BlockSpec((tq, tq), lambda p, i: (0, 0)),
         pl.BlockSpec((tq, tq), lambda p, i: (0, 0))],
        [pl.BlockSpec((128, qb * tq), lambda p, i: (p, i)),
         pl.BlockSpec((t, 128), lambda p, i: (0, p)),
         pl.BlockSpec((t, 128), lambda p, i: (0, p))],
        [jax.ShapeDtypeStruct((SB_WIDTH, t), F32), jax.ShapeDtypeStruct((t, SB_WIDTH), F32),
         jax.ShapeDtypeStruct((t, SB_WIDTH), F32)],
        [], ("arbitrary", "arbitrary"), (qkv, qkv, qkv, ktb, dyc, dyct, cc, nvis, upper, lower), ride)


def _sg_bwd(ab, dyb, gn, wm, wmt, bfull, tm, name):
    t = ab.shape[0]
    nt = t // tm
    sel = (jnp.arange(SG_WIDTH)[:, None] // SG_HD == jnp.arange(CHUNK)[None, :]).astype(F32)

    def body(u_ref, v_ref, dy_ref, gn_ref, wm_ref, wmt_ref, b_ref, sel_ref,
             dup_ref, dvp_ref, dgn_ref, dw_ref, db_ref, dbacc_ref):
        i = pl.program_id(0)

        @pl.when(i == 0)
        def _():
            dgn_ref[...] = jnp.zeros_like(dgn_ref)
            dw_ref[...] = jnp.zeros_like(dw_ref)
            dbacc_ref[...] = jnp.zeros_like(dbacc_ref)

        tril = (lax.broadcasted_iota(jnp.int32, (CHUNK, CHUNK), 0) >= lax.broadcasted_iota(jnp.int32, (CHUNK, CHUNK), 1))
        gn_ = gn_ref[...]
        for c in range(tm // CHUNK):
            rows = slice(c * CHUNK, (c + 1) * CHUNK)
            up, vp, dy = u_ref[rows, :], v_ref[rows, :], dy_ref[rows, :]
            u, v = _gelu(up), _gelu(vp)
            r = _rstd(v)
            vn = (v * r * gn_).astype(BF16)
            sv = b_ref[...]
            for h in range(SG_HEADS):
                sv = sv + jnp.where(_head_lanes(h), _dot(wm_ref[h], vn), 0.0)
            dup_ref[rows, :] = dy * sv * _gelu_grad(up)
            dsv = dy * u
            dbacc_ref[...] += dsv
            dvn = jnp.zeros((CHUNK, SG_WIDTH), F32)
            for h in range(SG_HEADS):
                dsv_h = jnp.where(_head_lanes(h), dsv, 0.0).astype(BF16)
                dvn = dvn + _dot(wmt_ref[h], dsv_h)
                dw_ref[h] += jnp.where(tril, _dot_nt(dsv_h, vn), 0.0)
            dv, dgn = _rms_bwd(v, r, gn_, dvn)
            dgn_ref[...] += dgn
            dvp_ref[rows, :] = dv * _gelu_grad(vp)

        @pl.when(i == nt - 1)
        def _():
            db_ref[...] = jnp.dot(dbacc_ref[...], sel_ref[...], preferred_element_type=F32,
                                  precision=lax.Precision.HIGHEST)

    const = lambda shape: pl.BlockSpec(shape, lambda i: tuple(0 for _ in shape))
    return pl.pallas_call(
        body, name=name, grid=(nt,),
        in_specs=[pl.BlockSpec((tm, SG_WIDTH), lambda i: (i, 1)), pl.BlockSpec((tm, SG_WIDTH), lambda i: (i, 2)),
                  pl.BlockSpec((tm, SG_WIDTH), lambda i: (i, 0)), const((1, SG_WIDTH)),
                  const((SG_HEADS, CHUNK, CHUNK)), const((SG_HEADS, CHUNK, CHUNK)), const((CHUNK, SG_WIDTH)),
                  const((SG_WIDTH, CHUNK))],
        out_specs=[pl.BlockSpec((tm, SG_WIDTH), lambda i: (i, 0)), pl.BlockSpec((tm, SG_WIDTH), lambda i: (i, 0)),
                   const((1, SG_WIDTH)), const((SG_HEADS, CHUNK, CHUNK)), const((CHUNK, CHUNK))],
        out_shape=[jax.ShapeDtypeStruct((t, SG_WIDTH), F32), jax.ShapeDtypeStruct((t, SG_WIDTH), F32),
                   jax.ShapeDtypeStruct((1, SG_WIDTH), F32), jax.ShapeDtypeStruct((SG_HEADS, CHUNK, CHUNK), F32),
                   jax.ShapeDtypeStruct((CHUNK, CHUNK), F32)],
        scratch_shapes=[pltpu.VMEM((CHUNK, SG_WIDTH), F32)],
        compiler_params=_params(("arbitrary",)),
    )(ab, ab, dyb, gn, wm, wmt, bfull, sel)


def _pool_bwd(ab, dya, wbd, scale, tm, name):
    t = ab.shape[0]
    nt = t // tm
    hb = tm // POOL_HALO
    nh = t // POOL_HALO

    def body(cur_ref, prev_ref, dy_ref, dyn_ref, w_ref, s_ref, da_ref, dw_ref, ds_ref):
        i = pl.program_id(0)

        @pl.when(i == 0)
        def _():
            dw_ref[...] = jnp.zeros_like(dw_ref)
            ds_ref[...] = jnp.zeros_like(ds_ref)

        prev = jnp.where(i == 0, 0.0, prev_ref[...])
        d = _pool_diff(cur_ref[...], prev, i * tm).astype(BF16)
        dy = dy_ref[...]
        ds_ref[...] += jnp.sum(dy * _dot(d, w_ref[...]), axis=0, keepdims=True)
        dyn = jnp.where(i == nt - 1, 0.0, dyn_ref[...])
        dys = (jnp.concatenate([dy, dyn], axis=0) * s_ref[...]).astype(BF16)
        dw_ref[...] += _dot_tn(d, dys[:tm])
        dd = _dot_nt(dys, w_ref[...])
        fwd = _pool_window_sums(dd / _pool_count(i * tm, tm + POOL_HALO), True)
        da_ref[...] = fwd[:tm] - dd[:tm]

    return pl.pallas_call(
        body, name=name, grid=(nt,),
        in_specs=[pl.BlockSpec((tm, POOL_WIDTH), lambda i: (i, 0)),
                  pl.BlockSpec((POOL_HALO, POOL_WIDTH), lambda i: (jnp.maximum(i * hb - 1, 0), 0)),
                  pl.BlockSpec((tm, POOL_WIDTH), lambda i: (i, 0)),
                  pl.BlockSpec((POOL_HALO, POOL_WIDTH), lambda i: (jnp.minimum((i + 1) * hb, nh - 1), 0)),
                  pl.BlockSpec((POOL_WIDTH, POOL_WIDTH), lambda i: (0, 0)),
                  pl.BlockSpec((1, POOL_WIDTH), lambda i: (0, 0))],
        out_specs=[pl.BlockSpec((tm, POOL_WIDTH), lambda i: (i, 0)),
                   pl.BlockSpec((POOL_WIDTH, POOL_WIDTH), lambda i: (0, 0)),
                   pl.BlockSpec((1, POOL_WIDTH), lambda i: (0, 0))],
        out_shape=[jax.ShapeDtypeStruct((t, POOL_WIDTH), F32), jax.ShapeDtypeStruct((POOL_WIDTH, POOL_WIDTH), F32),
                   jax.ShapeDtypeStruct((1, POOL_WIDTH), F32)],
        compiler_params=_params(("arbitrary",)),
    )(ab, ab, dya, dya, wbd, scale)


def _inproj_bwd(dx1, x, g, da, dup, dvp, dqt, dk, dv, w, tm, name, ride=None):
    t, d = x.shape
    n = w.shape[1]
    nt = t // tm

    def body(dx1_ref, x_ref, g_ref, da_ref, du_ref, dv_ref, dqt_ref, dk_ref, dvv_ref, w_ref,
             dx_ref, h_ref, dp_ref, dg_ref):
        @pl.when(pl.program_id(0) == 0)
        def _():
            dg_ref[...] = jnp.zeros_like(dg_ref)

        dp = jnp.concatenate([da_ref[...], du_ref[...], dv_ref[...], dqt_ref[...].T, dk_ref[...], dvv_ref[...]],
                             axis=1).astype(BF16)
        dp_ref[...] = dp
        xx = x_ref[...]
        r = _rstd(xx)
        h_ref[...] = (xx * r * g_ref[...]).astype(BF16)
        dx, dg = _rms_bwd(xx, r, g_ref[...], _dot_nt(dp, w_ref[...]))
        dx_ref[...] = dx1_ref[...] + dx
        dg_ref[...] += dg

    row = lambda w_: pl.BlockSpec((tm, w_), lambda i: (i, 0))
    return _call(
        body, name, (nt,),
        [row(d), row(d), pl.BlockSpec((1, d), lambda i: (0, 0)), row(POOL_WIDTH), row(SG_WIDTH),
         row(SG_WIDTH), pl.BlockSpec((SB_WIDTH, tm), lambda i: (0, i)), row(SB_WIDTH), row(SB_WIDTH),
         pl.BlockSpec((d, n), lambda i: (0, 0))],
        [row(d), row(d), row(n), pl.BlockSpec((1, d), lambda i: (0, 0))],
        [jax.ShapeDtypeStruct((t, d), F32), jax.ShapeDtypeStruct((t, d), BF16),
         jax.ShapeDtypeStruct((t, n), BF16), jax.ShapeDtypeStruct((1, d), F32)],
        [], ("arbitrary",), (dx1, x, g, da, dup, dvp, dqt, dk, dv, w), ride)


def _full_w_in(gathered):
    return gathered.transpose(1, 0, 2).reshape(D_MODEL, IN_COLS)


_SMALL_SHAPES = ((D_MODEL,), (4, POOL_GW, POOL_GW), (POOL_WIDTH,), (SG_WIDTH,), (SG_HEADS, CHUNK, CHUNK),
                 (SG_HEADS, CHUNK), (D_MODEL,))
_SMALL_SIZES = tuple(functools.reduce(lambda p, q: p * q, shp) for shp in _SMALL_SHAPES)
_SMALL_ROWS = sum(_SMALL_SIZES) // 128
_NORM1_ROWS = D_MODEL // 128


def _pack_small_layer(arrs):
    return jnp.concatenate([a.reshape(-1) for a in arrs]).reshape(_SMALL_ROWS, 128)


def _pack_small(arrs, final):
    return jnp.concatenate([_pack_small_layer([a[l] for a in arrs]) for l in range(DEPTH)] + [final.reshape(-1, 128)])


def _unpack_small(buf):
    per_layer = []
    for l in range(DEPTH):
        flat, off, outs = buf[l * _SMALL_ROWS:(l + 1) * _SMALL_ROWS].reshape(-1), 0, []
        for shp, size in zip(_SMALL_SHAPES, _SMALL_SIZES):
            outs.append(flat[off:off + size].reshape(shp))
            off += size
        per_layer.append(outs)
    return [jnp.stack([per_layer[l][k] for l in range(DEPTH)]) for k in range(len(_SMALL_SHAPES))] + \
           [buf[DEPTH * _SMALL_ROWS:].reshape(-1)]


def _tiles(t):
    return min(512, t), min(256, t // 4), min(2048, t)


def _layer_fwd(xl, wi, wo, wud, small_w, l, ride_mlp=None):
    n1, pw, ps, sn, sw, sb, n2 = small_w
    tm, tq, _ = _tiles(xl.shape[0])
    wbd = jnp.zeros((4, POOL_GW, 4, POOL_GW), F32)
    for gi in range(4):
        wbd = wbd.at[gi, :, gi, :].set(pw[gi])
    wbd = wbd.reshape(POOL_WIDTH, POOL_WIDTH).astype(BF16)
    wm = (sw * jnp.tril(jnp.ones((CHUNK, CHUNK), F32))).astype(BF16)
    wmt = wm.transpose(0, 2, 1)
    bfull = jnp.repeat(sb.T, SG_HD, axis=1)
    g1, g2, psc, gn = n1[None, :], n2[None, :], ps[None, :], sn[None, :]

    if isinstance(wo, Exchange):
        (ab, qkv, ktb, vtb), (wo,) = _inproj_fwd(xl, g1, wi, tm, tq, f"inproj_fwd{l}", wo)
    else:
        ab, qkv, ktb, vtb = _inproj_fwd(xl, g1, wi, tm, tq, f"inproj_fwd{l}")
    ya = _pool_fwd(ab, wbd, psc, tm, f"pool_fwd{l}")
    yb = _sg_fwd(ab, gn, wm, bfull, tm, f"sg_fwd{l}")
    if isinstance(wud, Exchange):
        (yct, cc, nvis), (wu, wd) = _sba_fwd(qkv, vtb, tq, f"sba_fwd{l}", wud)
    else:
        (yct, cc, nvis), (wu, wd) = _sba_fwd(qkv, vtb, tq, f"sba_fwd{l}"), wud
    wo = wo.reshape(D_MODEL, D_MODEL)
    res = _outproj_mlp_fwd(xl, ya, yb, yct, wo, g2, wu, wd, min(MLP_ROWS, xl.shape[0]), f"mlp_fwd{l}", ride_mlp)
    (x1, u, x2), rode = res if ride_mlp is not None else (res, None)
    saved = dict(x=xl, ab=ab, qkv=qkv, ktb=ktb, cc=cc, nvis=nvis, ya=ya, yb=yb, yct=yct, x1=x1, u=u, wi=wi, wo=wo, wu=wu,
                 wd=wd, wbd=wbd, wm=wm, wmt=wmt, bfull=bfull, g1=g1, g2=g2, psc=psc, gn=gn)
    return x2, saved, rode


def _layer_bwd(dx, s, l, ride_mlp=None, scatter_in_attn=True, gather_small=False):
    tm, tq, tw = _tiles(dx.shape[0])
    ktb = s["ktb"]
    res = _mlp_bwd(dx, s["x1"], s["g2"], s["u"], s["wu"], s["wd"], min(MLP_ROWS, dx.shape[0]), f"mlp_bwd{l}", ride_mlp)
    (dx1, du, r, h2, dx2b, dn2), rode = res if ride_mlp is not None else (res, None)
    dw_up = _matmul_tn(h2, du, D_MODEL, 2 * FF_SHARD, tw, f"dw_up{l}", by_column_block=FF_SHARD)
    dw_down = _matmul_tn(r, dx2b, 1024, D_MODEL, tw, f"dw_down{l}").reshape(N_DEV, FF_SHARD, D_MODEL)
    dya, dyb, dyc, dyct, dx1b = _outproj_bwd(dx1, s["wo"], tm, f"outproj_bwd{l}")
    dw_out = jnp.concatenate([
        _matmul_tn(jnp.concatenate([s["ya"], s["yb"]], axis=1), dx1b, POOL_WIDTH + SG_WIDTH, D_MODEL, tw, f"dw_out_ab{l}"),
        _matmul_tn(s["yct"], dx1b, SB_WIDTH, D_MODEL, tw, f"dw_out_c{l}", a_transposed=True)]
    ).reshape(N_DEV, OUT_SHARD, D_MODEL)
    if scatter_in_attn:
        (dqt, dk, dv), (dw_out, dw_up, dw_down) = _sba_bwd(s["qkv"], ktb, dyc, dyct, s["cc"], s["nvis"], tq, f"sba_bwd{l}",
                                                           Exchange([dw_out, dw_up, dw_down], True))
    else:
        dqt, dk, dv = _sba_bwd(s["qkv"], ktb, dyc, dyct, s["cc"], s["nvis"], tq, f"sba_bwd{l}")
    dup, dvp, dgn, dwm, dbm = _sg_bwd(s["ab"], dyb, s["gn"], s["wm"], s["wmt"], s["bfull"], tm, f"sg_bwd{l}")
    da, dwbd, dpsc = _pool_bwd(s["ab"], dya, s["wbd"], s["psc"], tm, f"pool_bwd{l}")
    dpw = jnp.stack([dwbd[gi * POOL_GW:(gi + 1) * POOL_GW, gi * POOL_GW:(gi + 1) * POOL_GW] for gi in range(4)])
    small = _pack_small_layer([jnp.zeros((D_MODEL,), F32), dpw, dpsc[0], dgn[0], dwm, dbm[:, :SG_HEADS].T, dn2[0]])[_NORM1_ROWS:]
    res = _inproj_bwd(dx1, s["x"], s["g1"], da, dup, dvp, dqt, dk, dv, s["wi"], tm, f"inproj_bwd{l}",
                      Exchange([small], False) if gather_small else None)
    (dx, h1, dproj, dn1), small = (res[0], res[1][0]) if gather_small else (res, small)
    dw_in = _matmul_tn(h1, dproj, D_MODEL, IN_COLS // 3, tw, f"dw_in{l}")
    dw_in = dw_in.reshape(D_MODEL, N_DEV, IN_SHARD).transpose(1, 0, 2)
    return dx, (dw_in, dw_out, dw_up, dw_down), (dn1.reshape(_NORM1_ROWS, 128), small), rode


def kernel(x, norm1, w_in, pool_w, pool_scale, sg_norm, sg_w, sg_b, w_out, norm2, w_up, w_down, final_norm, loss_target, m_norm1, m_w_in, m_pool_w, m_pool_scale, m_sg_norm, m_sg_w, m_sg_b, m_w_out, m_norm2, m_w_up, m_w_down, m_final_norm, v_norm1, v_w_in, v_pool_w, v_pool_scale, v_sg_norm, v_sg_w, v_sg_b, v_w_out, v_norm2, v_w_up, v_w_down, v_final_norm):
    t = x.shape[1]
    tm = _tiles(t)[0]
    small_w = (norm1, pool_w, pool_scale, sg_norm, sg_w, sg_b, norm2)
    big_w = (w_in, w_out, w_up, w_down)
    big_m = (m_w_in, m_w_out, m_w_up, m_w_down)
    big_v = (v_w_in, v_w_out, v_w_up, v_w_down)
    shards = [[w[l].astype(BF16) for w in big_w] for l in range(DEPTH)]

    wi0 = _full_w_in(_gather_via_sibling(shards[0][0], "gather_w_in0"))
    x1, s0, g1 = _layer_fwd(x.reshape(t, D_MODEL), wi0, Exchange(shards[0][1:2], False), Exchange(shards[0][2:], False),
                            tuple(w[0] for w in small_w), 0, ride_mlp=Exchange(shards[1][:2], False))
    x2, s1, _ = _layer_fwd(x1, _full_w_in(g1[0]), g1[1], Exchange(shards[1][2:], False), tuple(w[1] for w in small_w), 1)
    loss_local, dx, dfinal = _loss_grad(x2, final_norm[None, :], loss_target.reshape(t, D_MODEL), tm, "loss_grad")
    loss = lax.psum(loss_local[0, 0], MESH_AXES)

    dx, parts1, small1, _ = _layer_bwd(dx, s1, 1)
    early = jnp.concatenate(list(small1) + [dfinal.reshape(-1, 128)])
    dx, parts0, (dn1, small0), (recv_in1, early) = _layer_bwd(
        dx, s0, 0, ride_mlp=Exchange([parts1[0], early], [True, False]), gather_small=True)
    grad_x = dx.reshape(x.shape)
    recv_in0, dn1 = _exchange([parts0[0], dn1], [True, False], "scatter_w_in0_gather_norm1_0")
    small_all = jnp.concatenate([dn1, small0, early], axis=1)
    received = [[recv_in0] + list(parts0[1:]), [recv_in1] + list(parts1[1:])]

    big = [None] * 4
    for l in reversed(range(DEPTH)):
        for k in range(4):
            big[k] = _reduce_adamw(received[l][k], big_w[k], big_m[k], big_v[k], l, big[k], f"adamw{k}_{l}")

    sm = _reduce_adamw(
        small_all,
        _pack_small(small_w, final_norm)[None],
        _pack_small([m_norm1, m_pool_w, m_pool_scale, m_sg_norm, m_sg_w, m_sg_b, m_norm2], m_final_norm)[None],
        _pack_small([v_norm1, v_pool_w, v_pool_scale, v_sg_norm, v_sg_w, v_sg_b, v_norm2], v_final_norm)[None],
        0, None, "adamw_replicated")

    out = [loss, grad_x]
    for k in range(4):
        n1, pw, ps, sn, sw, sb, n2, fn = _unpack_small(sm[k][0])
        out += [n1, big[0][k], pw, ps, sn, sw, sb, big[1][k], n2, big[2][k], big[3][k], fn]
    return tuple(out)
```

```python
import functools

import jax
import jax.numpy as jnp
from jax import lax
from jax.experimental import pallas as pl
from jax.experimental.pallas import tpu as pltpu

F32 = jnp.float32
BF16 = jnp.bfloat16

D_MODEL = 1024
DEPTH = 2
POOL_WIDTH = 256
SG_WIDTH = 256
SB_WIDTH = 512
POOL_WINDOWS = (2, 4, 8, 16)
POOL_GW = 64
POOL_HALO = 16
CHUNK = 128
SG_HEADS = 4
SG_HD = 64
SB_HD = 64
SB_PAIRS = SB_WIDTH // 128
AB_COLS = POOL_WIDTH + 2 * SG_WIDTH
IN_COLS = AB_COLS + 3 * SB_WIDTH
D_FF = 4096
EPS = 1e-6
N_DEV = 8
FF_SHARD = D_FF // N_DEV
IN_SHARD = IN_COLS // N_DEV
OUT_SHARD = D_MODEL // N_DEV
ADAM_LR = 0.001
ADAM_B1 = 0.9
ADAM_B2 = 0.999
ADAM_EPS = 1e-08
ADAM_WD = 0.01
ADAM_STEP = 10
VMEM_LIMIT = 56 * 1024 * 1024
MLP_ROWS = 1024
MESH_AXES = ("x", "y", "c")


def _dot(a, b):
    return jnp.dot(a, b, preferred_element_type=F32)


def _dot_nt(a, b):
    return lax.dot_general(a, b, (((1,), (1,)), ((), ())), preferred_element_type=F32)


def _dot_tn(a, b):
    return lax.dot_general(a, b, (((0,), (0,)), ((), ())), preferred_element_type=F32)


def _rstd(x):
    return lax.rsqrt(jnp.mean(x * x, axis=-1, keepdims=True) + EPS)


def _rms_bwd(x, r, g, dh):
    gq = dh * g
    dx = r * gq - x * (r * r * r) * jnp.mean(gq * x, axis=-1, keepdims=True)
    dg = jnp.sum(dh * x * r, axis=0, keepdims=True)
    return dx, dg


def _params(sem=None):
    kw = dict(vmem_limit_bytes=VMEM_LIMIT)
    if sem is not None:
        kw["dimension_semantics"] = sem
    return pltpu.CompilerParams(**kw)


def _row_tile(rows, cap):
    best = 8
    for t in range(8, min(rows, cap) + 1, 8):
        if rows % t == 0:
            best = t
    return best


def _peer(k):
    x, y, c = lax.axis_index("x"), lax.axis_index("y"), lax.axis_index("c")
    return (1 - x if k & 4 else x, 1 - y if k & 2 else y, 1 - c if k & 1 else c)


def _my_index():
    return 4 * lax.axis_index("x") + 2 * lax.axis_index("y") + lax.axis_index("c")


class Exchange:
    def __init__(self, arrs, scatter):
        self.arrs = list(arrs)
        self.scatter = list(scatter) if isinstance(scatter, (list, tuple)) else [scatter] * len(self.arrs)
        self.n = len(self.arrs)
        self.any_specs = [pl.BlockSpec(memory_space=pl.ANY)] * self.n
        self.out_shape = [jax.ShapeDtypeStruct((N_DEV,) + a.shape[-2:], a.dtype) for a in self.arrs]
        self.sems = [pltpu.SemaphoreType.DMA((self.n, N_DEV - 1)), pltpu.SemaphoreType.DMA((self.n, N_DEV - 1)),
                     pltpu.SemaphoreType.DMA((self.n,))]

    def _copies(self, ins, outs, sems):
        send_sems, recv_sems, local_sems = sems
        me = _my_index()
        local, remote = [], []
        for a in range(self.n):
            sc = self.scatter[a]
            local.append(pltpu.make_async_copy(ins[a].at[me] if sc else ins[a], outs[a].at[me], local_sems.at[a]))
            for k in range(1, N_DEV):
                px, py, pc = _peer(k)
                remote.append(pltpu.make_async_remote_copy(
                    src_ref=ins[a].at[4 * px + 2 * py + pc] if sc else ins[a], dst_ref=outs[a].at[me],
                    send_sem=send_sems.at[a, k - 1], recv_sem=recv_sems.at[a, k - 1],
                    device_id=(px, py, pc), device_id_type=pl.DeviceIdType.MESH))
        return local, remote

    def start(self, ins, outs, sems):
        local, remote = self._copies(ins, outs, sems)
        for cp in local + remote:
            cp.start()

    def wait(self, ins, outs, sems):
        local, remote = self._copies(ins, outs, sems)
        for cp in remote:
            cp.wait_recv()
        for cp in remote:
            cp.wait_send()
        for cp in local:
            cp.wait()

    def alone(self, name):
        n = self.n

        def body(*refs):
            ins, outs, sems = refs[:n], refs[n:2 * n], refs[2 * n:]
            self.start(ins, outs, sems)
            self.wait(ins, outs, sems)

        return pl.pallas_call(body, name=name, out_shape=self.out_shape, in_specs=self.any_specs,
                              out_specs=self.any_specs, scratch_shapes=self.sems)(*self.arrs)


def _exchange(arrs, scatter, name):
    return Exchange(arrs, scatter).alone(name)


def _gather_via_sibling(shard, name):
    r, c = shard.shape

    def body(x_ref, out_ref, send_sems, recv_sems, local_sem):
        x, y, core = lax.axis_index("x"), lax.axis_index("y"), lax.axis_index("c")
        me, sibling = (x, y, core), (x, y, 1 - core)
        chips = [(1 - x, y), (x, 1 - y), (1 - x, 1 - y)]

        def slot(px, py, pc):
            return out_ref.at[4 * px + 2 * py + pc]

        def copy(k, block, to, src=None):
            return pltpu.make_async_remote_copy(
                src_ref=slot(*block) if src is None else src, dst_ref=slot(*block),
                send_sem=send_sems.at[k], recv_sem=recv_sems.at[k], device_id=to, device_id_type=pl.DeviceIdType.MESH)

        mine = pltpu.make_async_copy(x_ref, slot(*me), local_sem)
        mine.start()
        first = [copy(0, me, sibling, src=x_ref)] + [copy(1 + j, me, (*chip, core), src=x_ref) for j, chip in enumerate(chips)]
        for cp in first:
            cp.start()
        passed = [copy(4 + j, (*chip, core), sibling) for j, chip in enumerate(chips)]
        for j, chip in enumerate(chips):
            copy(1 + j, (*chip, core), me).wait_recv()
            passed[j].start()
        copy(0, sibling, me).wait_recv()
        for j, chip in enumerate(chips):
            copy(4 + j, (*chip, 1 - core), me).wait_recv()
        for cp in first + passed:
            cp.wait_send()
        mine.wait()

    return pl.pallas_call(
        body, name=name, out_shape=jax.ShapeDtypeStruct((N_DEV, r, c), shard.dtype),
        in_specs=[pl.BlockSpec(memory_space=pltpu.VMEM)], out_specs=pl.BlockSpec(memory_space=pltpu.VMEM),
        scratch_shapes=[pltpu.SemaphoreType.DMA((N_DEV - 1,)), pltpu.SemaphoreType.DMA((N_DEV - 1,)),
                        pltpu.SemaphoreType.DMA],
        compiler_params=pltpu.CompilerParams(vmem_limit_bytes=VMEM_LIMIT),
    )(shard)


def _call(body, name, grid, in_specs, out_specs, out_shape, scratch_shapes, semantics, args, ride=None):
    if ride is None:
        return pl.pallas_call(body, name=name, grid=grid, in_specs=in_specs, out_specs=out_specs, out_shape=out_shape,
                              scratch_shapes=scratch_shapes, compiler_params=_params(semantics))(*args)
    single = not isinstance(out_shape, (list, tuple))
    out_specs, out_shape = ([out_specs], [out_shape]) if single else (list(out_specs), list(out_shape))
    n_in, n_out, n_scr, n = len(in_specs), len(out_specs), len(scratch_shapes), ride.n

    def riding(*refs):
        ins, cins = refs[:n_in], refs[n_in:n_in + n]
        outs, couts = refs[n_in + n:n_in + n + n_out], refs[n_in + n + n_out:n_in + 2 * n + n_out]
        scr, sems = refs[n_in + 2 * n + n_out:n_in + 2 * n + n_out + n_scr], refs[n_in + 2 * n + n_out + n_scr:]
        first = functools.reduce(lambda p, q: p & q, [pl.program_id(a) == 0 for a in range(len(grid))])
        last = functools.reduce(lambda p, q: p & q, [pl.program_id(a) == g - 1 for a, g in enumerate(grid)])

        @pl.when(first)
        def _():
            ride.start(cins, couts, sems)

        body(*ins, *outs, *scr)

        @pl.when(last)
        def _():
            ride.wait(cins, couts, sems)

    res = pl.pallas_call(
        riding, name=name, grid=grid, in_specs=list(in_specs) + ride.any_specs, out_specs=out_specs + ride.any_specs,
        out_shape=out_shape + ride.out_shape, scratch_shapes=list(scratch_shapes) + ride.sems,
        compiler_params=_params(("arbitrary",) * len(grid)))(*args, *ride.arrs)
    own = res[0] if single else list(res[:n_out])
    return own, list(res[n_out:])


def _reduce_adamw(parts, w, m, v, l, prev, name):
    _, rows, n = parts.shape
    tr = _row_tile(rows, max(8, (1 << 18) // n))
    c1 = 1.0 - ADAM_B1 ** ADAM_STEP
    c2 = 1.0 - ADAM_B2 ** ADAM_STEP

    def body(p_ref, w_ref, m_ref, v_ref, *rest):
        g_ref, d_ref, nm_ref, nv_ref = rest[-4:]
        g = p_ref[0].astype(F32)
        for s in range(1, N_DEV):
            g = g + p_ref[s].astype(F32)
        nm = ADAM_B1 * m_ref[...] + (1.0 - ADAM_B1) * g
        nv = ADAM_B2 * v_ref[...] + (1.0 - ADAM_B2) * (g * g)
        m_hat = nm / c1
        v_hat = nv / c2
        g_ref[...] = g
        d_ref[...] = -ADAM_LR * (m_hat / (jnp.sqrt(v_hat) + ADAM_EPS) + ADAM_WD * w_ref[...])
        nm_ref[...] = nm
        nv_ref[...] = nv

    blk = pl.BlockSpec((None, tr, n), lambda i: (l, i, 0))
    out = jax.ShapeDtypeStruct(w.shape, F32)
    prev = list(prev) if prev is not None else []
    return pl.pallas_call(
        body, name=name, grid=(rows // tr,),
        in_specs=[pl.BlockSpec((N_DEV, tr, n), lambda i: (0, i, 0)), blk, blk, blk] + [pl.BlockSpec(memory_space=pl.ANY)] * len(prev),
        out_specs=[blk, blk, blk, blk], out_shape=[out, out, out, out],
        input_output_aliases={4 + k: k for k in range(len(prev))},
        compiler_params=_params(("parallel",)),
    )(parts, w, m, v, *prev)


def _inproj_fwd(x, g, w, tm, tq, name, ride=None):
    t, d = x.shape
    n = w.shape[1]
    nb = t // tq
    per = tm // tq

    def body(x_ref, g_ref, w_ref, ab_ref, qkv_ref, kt_ref, vt_ref):
        xx = x_ref[...]
        h = (xx * _rstd(xx) * g_ref[...]).astype(BF16)
        ab_ref[...] = _dot(h, w_ref[:, :AB_COLS])
        qkv = _dot(h, w_ref[:, AB_COLS:])
        qkv_ref[...] = qkv.astype(BF16)
        for which, out_ref in ((1, kt_ref), (2, vt_ref)):
            for p in range(SB_PAIRS):
                for b in range(per):
                    cols = which * SB_WIDTH + p * 128
                    out_ref[p, b] = qkv[b * tq:(b + 1) * tq, cols:cols + 128].T.astype(BF16)

    tb = pl.BlockSpec((SB_PAIRS, per, 128, tq), lambda i: (0, i, 0, 0))
    tshape = jax.ShapeDtypeStruct((SB_PAIRS, nb, 128, tq), BF16)
    return _call(
        body, name, (t // tm,),
        [pl.BlockSpec((tm, d), lambda i: (i, 0)), pl.BlockSpec((1, d), lambda i: (0, 0)),
         pl.BlockSpec((d, n), lambda i: (0, 0))],
        [pl.BlockSpec((tm, AB_COLS), lambda i: (i, 0)), pl.BlockSpec((tm, n - AB_COLS), lambda i: (i, 0)), tb, tb],
        [jax.ShapeDtypeStruct((t, AB_COLS), F32), jax.ShapeDtypeStruct((t, n - AB_COLS), BF16), tshape, tshape],
        [], ("parallel",), (x, g, w), ride)


def _pool_window_sums(xx, forward):
    n = xx.shape[0]
    sh = (lambda k: n - k) if forward else (lambda k: k)
    s2 = xx + pltpu.roll(xx, sh(1), 0)
    s4 = s2 + pltpu.roll(s2, sh(2), 0)
    s8 = s4 + pltpu.roll(s4, sh(4), 0)
    s16 = s8 + pltpu.roll(s8, sh(8), 0)
    grp = lax.broadcasted_iota(jnp.int32, (1, POOL_WIDTH), 1) // POOL_GW
    return jnp.where(grp == 0, s2, jnp.where(grp == 1, s4, jnp.where(grp == 2, s8, s16)))


def _pool_count(t0, rows):
    grp = lax.broadcasted_iota(jnp.int32, (1, POOL_WIDTH), 1) // POOL_GW
    win = jnp.where(grp == 0, 2, jnp.where(grp == 1, 4, jnp.where(grp == 2, 8, 16)))
    tt = t0 + lax.broadcasted_iota(jnp.int32, (rows, 1), 0)
    return jnp.minimum(tt + 1, win).astype(F32)


def _pool_diff(cur, prev, t0):
    tm = cur.shape[0]
    sums = _pool_window_sums(jnp.concatenate([prev, cur], axis=0), False)[POOL_HALO:]
    return sums / _pool_count(t0, tm) - cur


def _pool_fwd(ab, wbd, scale, tm, name):
    t = ab.shape[0]
    hb = tm // POOL_HALO

    def body(cur_ref, prev_ref, w_ref, s_ref, y_ref):
        i = pl.program_id(0)
        prev = jnp.where(i == 0, 0.0, prev_ref[...])
        d = _pool_diff(cur_ref[...], prev, i * tm).astype(BF16)
        y_ref[...] = (_dot(d, w_ref[...]) * s_ref[...]).astype(BF16)

    return pl.pallas_call(
        body, name=name, grid=(t // tm,),
        in_specs=[pl.BlockSpec((tm, POOL_WIDTH), lambda i: (i, 0)),
                  pl.BlockSpec((POOL_HALO, POOL_WIDTH), lambda i: (jnp.maximum(i * hb - 1, 0), 0)),
                  pl.BlockSpec((POOL_WIDTH, POOL_WIDTH), lambda i: (0, 0)),
                  pl.BlockSpec((1, POOL_WIDTH), lambda i: (0, 0))],
        out_specs=pl.BlockSpec((tm, POOL_WIDTH), lambda i: (i, 0)),
        out_shape=jax.ShapeDtypeStruct((t, POOL_WIDTH), BF16),
        compiler_params=_params(("parallel",)),
    )(ab, ab, wbd, scale)


_GELU_K = 0.7978845608028654
_GELU_A = 0.044715


def _gelu(x):
    return 0.5 * x * (1.0 + jnp.tanh(_GELU_K * (x + _GELU_A * (x * x * x))))


def _gelu_grad(x):
    th = jnp.tanh(_GELU_K * (x + _GELU_A * (x * x * x)))
    return 0.5 * (1.0 + th) + 0.5 * x * (1.0 - th * th) * (_GELU_K * (1.0 + 3.0 * _GELU_A * (x * x)))


def _head_lanes(h):
    return lax.broadcasted_iota(jnp.int32, (1, SG_WIDTH), 1) // SG_HD == h


def _sg_fwd(ab, gn, wm, bfull, tm, name):
    t = ab.shape[0]

    def body(u_ref, v_ref, gn_ref, wm_ref, b_ref, y_ref):
        v = _gelu(v_ref[...])
        vn = (v * _rstd(v) * gn_ref[...]).astype(BF16)
        for c in range(tm // CHUNK):
            rows = slice(c * CHUNK, (c + 1) * CHUNK)
            vc = vn[rows]
            sv = b_ref[...]
            for h in range(SG_HEADS):
                sv = sv + jnp.where(_head_lanes(h), _dot(wm_ref[h], vc), 0.0)
            y_ref[rows, :] = (_gelu(u_ref[rows, :]) * sv).astype(BF16)

    return pl.pallas_call(
        body, name=name, grid=(t // tm,),
        in_specs=[pl.BlockSpec((tm, SG_WIDTH), lambda i: (i, 1)), pl.BlockSpec((tm, SG_WIDTH), lambda i: (i, 2)),
                  pl.BlockSpec((1, SG_WIDTH), lambda i: (0, 0)),
                  pl.BlockSpec((SG_HEADS, CHUNK, CHUNK), lambda i: (0, 0, 0)),
                  pl.BlockSpec((CHUNK, SG_WIDTH), lambda i: (0, 0))],
        out_specs=pl.BlockSpec((tm, SG_WIDTH), lambda i: (i, 0)),
        out_shape=jax.ShapeDtypeStruct((t, SG_WIDTH), BF16),
        compiler_params=_params(("parallel",)),
    )(ab, ab, gn, wm, bfull)


LOG2E = 1.4426950408889634
SB_SCALE = 0.125 * LOG2E
SB_DEAD_LOG2 = 152.0
SB_QUERY_BLOCKS_PER_STEP = 2


_LOG2_1P_OVER_E = (1.4426947832107544, -0.7213107347488403, 0.4800737500190735, -0.3534546494483948,
                   0.2561796307563782, -0.15559986233711243, 0.06378752738237381, -0.012370623648166656)


def _log2_1p(e):
    q = _LOG2_1P_OVER_E[-1]
    for coef in reversed(_LOG2_1P_OVER_E[:-1]):
        q = q * e + coef
    return q * e


def _log2_sigmoids(y, log_on_valu=False):
    neg_abs = lax.bitcast_convert_type(lax.bitcast_convert_type(y, jnp.uint32) | jnp.uint32(0x80000000), F32)
    e = jnp.exp2(neg_abs)
    lb = jnp.minimum(y, 0.0) - (_log2_1p(e) if log_on_valu else jnp.log(1.0 + e) * LOG2E)
    return lb, lb - y


def _split(x):
    hi = x.astype(BF16)
    return hi, (x - hi.astype(F32)).astype(BF16)


def _tri_dot(tri, x):
    hi, lo = _split(x)
    return _dot(tri, hi) + _dot(tri, lo)


def _sba_fwd(qkv, vtb, tq, name, ride=None):
    t = qkv.shape[0]
    nb = t // tq
    qb = SB_QUERY_BLOCKS_PER_STEP
    upper = (jnp.arange(tq)[None, :] > jnp.arange(tq)[:, None]).astype(BF16)

    def body(q_ref, k_ref, vt_ref, up_ref, ot_ref, c_ref, n_ref):
        pair, step = pl.program_id(0), pl.program_id(1)
        refs = (q_ref, k_ref, vt_ref, up_ref, ot_ref, c_ref, n_ref)
        subs = list(range(qb))

        @pl.when(step == 0)
        def _():
            for sub in subs:
                query_blocks([sub], pair, step, *refs)

        @pl.when(step > 0)
        def _():
            query_blocks(subs, pair, step, *refs)

    def query_blocks(subs, pair, step, q_ref, k_ref, vt_ref, up_ref, ot_ref, c_ref, n_ref):
        index = {sub: step * qb + sub for sub in subs}
        mine = {sub: slice(sub * tq, (sub + 1) * tq) for sub in subs}
        up = up_ref[...]
        lane_head = lax.broadcasted_iota(jnp.int32, (1, 128), 1) // SB_HD
        sub_head = lax.broadcasted_iota(jnp.int32, (128, 1), 0) // SB_HD
        causal = (lax.broadcasted_iota(jnp.int32, (tq, tq), 0) < lax.broadcasted_iota(jnp.int32, (tq, tq), 1))
        qh = {}
        for sub in subs:
            q = q_ref[mine[sub], :]
            qh[sub] = [jnp.where(lane_head == h, q, jnp.zeros_like(q)) for h in range(2)]

        def blocks(work, carry):
            cs = {sub: list(carry[sub][:2]) for sub in carry}
            acc = {sub: carry[sub][2] for sub in carry}
            kj = [k_ref[pl.ds(pl.multiple_of(j * tq, tq), tq), :] for _, j, _ in work]
            vt = [vt_ref[j] for _, j, _ in work]
            chains = [(w, h) for w in range(len(work)) for h in range(2)]
            z = [_dot_nt(kj[w], qh[work[w][0]][h]) for w, h in chains]
            ls = [_log2_sigmoids(zz * SB_SCALE, log_on_valu=True) for zz in z]
            lb = [x[0] for x in ls]
            l1 = [jnp.where(causal, x[1], 0.0) if work[w][2] else x[1] for x, (w, h) in zip(ls, chains)]
            after = [_tri_dot(up, x) for x in l1]
            a = []
            for n, (w, h) in enumerate(chains):
                sub, j, diag = work[w]
                c_ref[h, pl.ds(j, 1), mine[sub]] = cs[sub][h]
                an = jnp.exp2(lb[n] + after[n] + cs[sub][h])
                a.append(jnp.where(causal, an, 0.0) if diag else an)
                cs[sub][h] = cs[sub][h] + after[n][0:1, :] + l1[n][0:1, :]
            alive = {sub: (jnp.max(jnp.maximum(*cs[sub])) > -SB_DEAD_LOG2).astype(jnp.int32) for sub in cs}
            for n, (w, h) in enumerate(chains):
                sub = work[w][0]
                acc[sub] = acc[sub] + _dot(jnp.where(sub_head == h, vt[w], jnp.zeros_like(vt[w])), a[n].astype(BF16))
            return {sub: (alive[sub], cs[sub][0], cs[sub][1], acc[sub]) for sub in cs}

        zero = jnp.zeros((1, tq), F32)
        start = {sub: (zero, zero, jnp.zeros((128, tq), F32)) for sub in subs}
        both = lambda: blocks([(sub, index[sub] - b, b == 0) for sub in subs for b in range(2)], start)
        if len(subs) == 1:
            joint = lax.cond(index[subs[0]] > 0, both, lambda: blocks([(subs[0], index[subs[0]], True)], start))
        else:
            joint = both()
        for sub in subs:
            i = index[sub]

            def left(state, sub=sub, i=i):
                s, _, c0, c1, acc = state
                return (s + 1,) + blocks([(sub, i - 1 - s, False)], {sub: (c0, c1, acc)})[sub]

            state = lax.while_loop(lambda st, i=i: (st[0] < i) & (st[1] > 0), left, (jnp.minimum(i, 1),) + tuple(joint[sub]))
            ot_ref[:, mine[sub]] = state[4].astype(BF16)
            n_ref[pair, i] = (state[0] + 1).astype(F32)

    return _call(
        body, name, (SB_PAIRS, nb // qb),
        [pl.BlockSpec((qb * tq, 128), lambda p, i: (i, p)),
         pl.BlockSpec((t, 128), lambda p, i: (0, SB_PAIRS + p)),
         pl.BlockSpec((None, nb, 128, tq), lambda p, i: (p, 0, 0, 0)),
         pl.BlockSpec((tq, tq), lambda p, i: (0, 0))],
        [pl.BlockSpec((128, qb * tq), lambda p, i: (p, i)),
         pl.BlockSpec((2, nb, qb * tq), lambda p, i: (p, 0, i)),
         pl.BlockSpec(memory_space=pltpu.SMEM)],
        [jax.ShapeDtypeStruct((SB_WIDTH, t), BF16),
         jax.ShapeDtypeStruct((2 * SB_PAIRS, nb, t), F32),
         jax.ShapeDtypeStruct((SB_PAIRS, nb), F32)],
        [], ("arbitrary", "arbitrary"), (qkv, qkv, vtb, upper), ride)


def _outproj_mlp_fwd(x, ya, yb, yct, wo, g2, wup, wdn, tm, name, ride=None):
    t, d = x.shape
    nf = wup.shape[0]

    def body(x_ref, ya_ref, yb_ref, yct_ref, wo_ref, g_ref, wu_ref, wd_ref, x1_ref, u_ref, x2_ref, h_ref, acc_ref):
        j = pl.program_id(1)

        @pl.when(j == 0)
        def _():
            x1 = (x_ref[...] + _dot(ya_ref[...], wo_ref[0:POOL_WIDTH, :])
                  + _dot(yb_ref[...], wo_ref[POOL_WIDTH:POOL_WIDTH + SG_WIDTH, :])
                  + _dot_tn(yct_ref[...], wo_ref[POOL_WIDTH + SG_WIDTH:, :]))
            x1_ref[...] = x1
            h_ref[...] = (x1 * _rstd(x1) * g_ref[...]).astype(BF16)
            acc_ref[...] = x1

        u = _dot(h_ref[...], wu_ref[...])
        u_ref[...] = u
        r = jnp.maximum(u, 0.0)
        acc_ref[...] += _dot((r * r).astype(BF16), wd_ref[...])

        @pl.when(j == nf - 1)
        def _():
            x2_ref[...] = acc_ref[...]

    row = lambda w: pl.BlockSpec((tm, w), lambda i, j: (i, 0))
    return _call(
        body, name, (t // tm, nf),
        [row(d), row(POOL_WIDTH), row(SG_WIDTH), pl.BlockSpec((SB_WIDTH, tm), lambda i, j: (0, i)),
         pl.BlockSpec((d, d), lambda i, j: (0, 0)), pl.BlockSpec((1, d), lambda i, j: (0, 0)),
         pl.BlockSpec((None, d, FF_SHARD), lambda i, j: (j, 0, 0)),
         pl.BlockSpec((None, FF_SHARD, d), lambda i, j: (j, 0, 0))],
        [row(d), pl.BlockSpec((tm, FF_SHARD), lambda i, j: (i, j)), row(d)],
        [jax.ShapeDtypeStruct((t, d), F32), jax.ShapeDtypeStruct((t, nf * FF_SHARD), F32),
         jax.ShapeDtypeStruct((t, d), F32)],
        [pltpu.VMEM((tm, d), BF16), pltpu.VMEM((tm, d), F32)],
        ("parallel", "arbitrary"), (x, ya, yb, yct, wo, g2, wup, wdn), ride)


def _loss_grad(x, g, target, tm, name):
    t, d = x.shape
    nt = t // tm

    def body(x_ref, g_ref, t_ref, loss_ref, dx_ref, dg_ref, sq_ref):
        i = pl.program_id(0)

        @pl.when(i == 0)
        def _():
            sq_ref[...] = jnp.zeros_like(sq_ref)
            dg_ref[...] = jnp.zeros_like(dg_ref)

        xx = x_ref[...]
        r = _rstd(xx)
        err = xx * r * g_ref[...] - t_ref[...]
        sq_ref[...] += jnp.sum(err * err, axis=0, keepdims=True)
        dx, dg = _rms_bwd(xx, r, g_ref[...], err * (1.0 / d))
        dx_ref[...] = dx
        dg_ref[...] += dg

        @pl.when(i == nt - 1)
        def _():
            loss_ref[...] = jnp.sum(sq_ref[...], axis=1, keepdims=True) * (0.5 / d)

    return pl.pallas_call(
        body, name=name, grid=(nt,),
        in_specs=[pl.BlockSpec((tm, d), lambda i: (i, 0)), pl.BlockSpec((1, d), lambda i: (0, 0)),
                  pl.BlockSpec((tm, d), lambda i: (i, 0))],
        out_specs=[pl.BlockSpec((1, 1), lambda i: (0, 0)), pl.BlockSpec((tm, d), lambda i: (i, 0)),
                   pl.BlockSpec((1, d), lambda i: (0, 0))],
        out_shape=[jax.ShapeDtypeStruct((1, 1), F32), jax.ShapeDtypeStruct((t, d), F32),
                   jax.ShapeDtypeStruct((1, d), F32)],
        scratch_shapes=[pltpu.VMEM((1, d), F32)],
        compiler_params=_params(("arbitrary",)),
    )(x, g, target)


def _mlp_bwd(dx2, x1, g2, u, wup, wdn, tm, name, ride=None):
    t, d = dx2.shape
    nf = wup.shape[0]
    nt = t // tm

    def body(dx2_ref, x1_ref, g_ref, u_ref, wu_ref, wd_ref, dx1_ref, du_ref, r_ref, h_ref, dxb_ref, dg_ref, acc_ref):
        i, j = pl.program_id(0), pl.program_id(1)

        @pl.when(j == 0)
        def _():
            x1 = x1_ref[...]
            h_ref[...] = (x1 * _rstd(x1) * g_ref[...]).astype(BF16)
            dxb_ref[...] = dx2_ref[...].astype(BF16)
            acc_ref[...] = jnp.zeros_like(acc_ref)

        @pl.when((i == 0) & (j == 0))
        def _():
            dg_ref[...] = jnp.zeros_like(dg_ref)

        dr = _dot_nt(dxb_ref[...], wd_ref[...])
        ru = jnp.maximum(u_ref[...], 0.0)
        du = (dr * (2.0 * ru)).astype(BF16)
        du_ref[...] = du
        r_ref[...] = (ru * ru).astype(BF16)
        acc_ref[...] += _dot_nt(du, wu_ref[...])

        @pl.when(j == nf - 1)
        def _():
            x1 = x1_ref[...]
            dx, dg = _rms_bwd(x1, _rstd(x1), g_ref[...], acc_ref[...])
            dx1_ref[...] = dx2_ref[...] + dx
            dg_ref[...] += dg

    row = lambda w: pl.BlockSpec((tm, w), lambda i, j: (i, 0))
    col = pl.BlockSpec((tm, FF_SHARD), lambda i, j: (i, j))
    return _call(
        body, name, (nt, nf),
        [row(d), row(d), pl.BlockSpec((1, d), lambda i, j: (0, 0)), col,
         pl.BlockSpec((None, d, FF_SHARD), lambda i, j: (j, 0, 0)),
         pl.BlockSpec((None, FF_SHARD, d), lambda i, j: (j, 0, 0))],
        [row(d), col, col, row(d), row(d), pl.BlockSpec((1, d), lambda i, j: (0, 0))],
        [jax.ShapeDtypeStruct((t, d), F32), jax.ShapeDtypeStruct((t, nf * FF_SHARD), BF16),
         jax.ShapeDtypeStruct((t, nf * FF_SHARD), BF16), jax.ShapeDtypeStruct((t, d), BF16),
         jax.ShapeDtypeStruct((t, d), BF16), jax.ShapeDtypeStruct((1, d), F32)],
        [pltpu.VMEM((tm, d), F32)], ("arbitrary", "arbitrary"), (dx2, x1, g2, u, wup, wdn), ride)


def _matmul_tn(a, b, bm, bn, bt, name, by_column_block=0, a_transposed=False):
    m, t = a.shape if a_transposed else a.shape[::-1]
    n = b.shape[1]
    nk = t // bt

    def body(a_ref, b_ref, o_ref, acc_ref):
        k = pl.program_id(2)

        @pl.when(k == 0)
        def _():
            acc_ref[...] = jnp.zeros_like(acc_ref)

        acc_ref[...] += _dot(a_ref[...], b_ref[...]) if a_transposed else _dot_tn(a_ref[...], b_ref[...])

        @pl.when(k == nk - 1)
        def _():
            if by_column_block:
                for c in range(bn // by_column_block):
                    o_ref[c] = acc_ref[:, c * by_column_block:(c + 1) * by_column_block].astype(BF16)
            else:
                o_ref[...] = acc_ref[...].astype(BF16)

    if by_column_block:
        out_spec = pl.BlockSpec((bn // by_column_block, bm, by_column_block), lambda i, j, k: (j, i, 0))
        out_shape = jax.ShapeDtypeStruct((n // by_column_block, m, by_column_block), BF16)
    else:
        out_spec = pl.BlockSpec((bm, bn), lambda i, j, k: (i, j))
        out_shape = jax.ShapeDtypeStruct((m, n), BF16)
    a_spec = (pl.BlockSpec((bm, bt), lambda i, j, k: (i, k)) if a_transposed
              else pl.BlockSpec((bt, bm), lambda i, j, k: (k, i)))
    return _call(body, name, (m // bm, n // bn, nk), [a_spec, pl.BlockSpec((bt, bn), lambda i, j, k: (k, j))],
                 out_spec, out_shape, [pltpu.VMEM((bm, bn), F32)], ("parallel", "parallel", "arbitrary"), (a, b))


def _outproj_bwd(dx1, wo, tm, name):
    t, d = dx1.shape
    c2 = POOL_WIDTH + SG_WIDTH

    def body(dx_ref, wo_ref, dya_ref, dyb_ref, dyc_ref, dyct_ref, dxb_ref):
        dxb = dx_ref[...].astype(BF16)
        dxb_ref[...] = dxb
        dya_ref[...] = _dot_nt(dxb, wo_ref[0:POOL_WIDTH, :])
        dyb_ref[...] = _dot_nt(dxb, wo_ref[POOL_WIDTH:c2, :])
        dyc_ref[...] = _dot_nt(dxb, wo_ref[c2:, :]).astype(BF16)
        dyct_ref[...] = _dot_nt(wo_ref[c2:, :], dxb).astype(BF16)

    row = lambda w: pl.BlockSpec((tm, w), lambda i: (i, 0))
    return pl.pallas_call(
        body, name=name, grid=(t // tm,),
        in_specs=[row(d), pl.BlockSpec((d, d), lambda i: (0, 0))],
        out_specs=[row(POOL_WIDTH), row(SG_WIDTH), row(SB_WIDTH), pl.BlockSpec((SB_WIDTH, tm), lambda i: (0, i)), row(d)],
        out_shape=[jax.ShapeDtypeStruct((t, POOL_WIDTH), F32), jax.ShapeDtypeStruct((t, SG_WIDTH), F32),
                   jax.ShapeDtypeStruct((t, SB_WIDTH), BF16), jax.ShapeDtypeStruct((SB_WIDTH, t), BF16),
                   jax.ShapeDtypeStruct((t, d), BF16)],
        compiler_params=_params(("parallel",)),
    )(dx1, wo)


def _sba_bwd(qkv, ktb, dyc, dyct, cc, nvis, tq, name, ride=None):
    t = qkv.shape[0]
    nb = t // tq
    idx = jnp.arange(tq)
    upper = (idx[None, :] > idx[:, None]).astype(BF16)
    lower = (idx[None, :] < idx[:, None]).astype(BF16)

    qb = SB_QUERY_BLOCKS_PER_STEP

    def body(q_ref, k_ref, v_ref, kt_ref, do_ref, dot_ref, c_ref, n_ref, up_ref, lo_ref, dqt_ref, dk_ref, dv_ref):
        pair, step = pl.program_id(0), pl.program_id(1)
        refs = (q_ref, k_ref, v_ref, kt_ref, do_ref, dot_ref, c_ref, n_ref, up_ref, lo_ref, dqt_ref, dk_ref, dv_ref)
        subs = list(range(qb))

        @pl.when(step == 0)
        def _():
            dk_ref[...] = jnp.zeros_like(dk_ref)
            dv_ref[...] = jnp.zeros_like(dv_ref)
            for sub in subs:
                query_blocks([sub], pair, step, *refs)

        @pl.when(step > 0)
        def _():
            query_blocks(subs, pair, step, *refs)

    def query_blocks(subs, pair, step, q_ref, k_ref, v_ref, kt_ref, do_ref, dot_ref, c_ref, n_ref, up_ref, lo_ref,
                     dqt_ref, dk_ref, dv_ref):
        index = {sub: step * qb + sub for sub in subs}
        mine = {sub: slice(sub * tq, (sub + 1) * tq) for sub in subs}
        up = up_ref[...]
        lo = lo_ref[...]
        lane_head = lax.broadcasted_iota(jnp.int32, (1, 128), 1) // SB_HD
        sub_head = lax.broadcasted_iota(jnp.int32, (128, 1), 0) // SB_HD
        causal = (lax.broadcasted_iota(jnp.int32, (tq, tq), 0) < lax.broadcasted_iota(jnp.int32, (tq, tq), 1))
        hms = [lane_head == h for h in range(2)]
        qh, qs, doh, dot = {}, {}, {}, {}
        for sub in subs:
            q, do = q_ref[mine[sub], :], do_ref[mine[sub], :]
            qh[sub] = [jnp.where(hm, q, jnp.zeros_like(q)) for hm in hms]
            qs[sub] = [x * 0.125 for x in qh[sub]]
            doh[sub] = [jnp.where(hm, do, jnp.zeros_like(do)) for hm in hms]
            dot[sub] = dot_ref[:, mine[sub]]

        def blocks(work, carry):
            cgs = {sub: list(carry[sub][:2]) for sub in carry}
            dqt = {sub: carry[sub][2] for sub in carry}
            rows = [pl.ds(pl.multiple_of(j * tq, tq), tq) for _, j, _ in work]
            kj = [k_ref[r, :] for r in rows]
            vj = [v_ref[r, :] for r in rows]
            kt = [kt_ref[j] for _, j, _ in work]
            chains = [(w, h) for w in range(len(work)) for h in range(2)]
            z = [_dot_nt(kj[w], qh[work[w][0]][h]) for w, h in chains]
            da = [_dot(jnp.where(hms[h], vj[w], jnp.zeros_like(vj[w])), dot[work[w][0]]) for w, h in chains]
            ls = [_log2_sigmoids(zz * SB_SCALE) for zz in z]
            lb = [x[0] for x in ls]
            l1 = [jnp.where(causal, x[1], 0.0) if work[w][2] else x[1] for x, (w, h) in zip(ls, chains)]
            after = [_tri_dot(up, x) for x in l1]
            a = [jnp.exp2(lb[n] + after[n] + c_ref[h, pl.ds(work[w][1], 1), mine[work[w][0]]])
                 for n, (w, h) in enumerate(chains)]
            a = [jnp.where(causal, a[n], 0.0) if work[w][2] else a[n] for n, (w, h) in enumerate(chains)]
            g = [a[n] * da[n] for n in range(len(chains))]
            gloc = [_dot(lo, x.astype(BF16)) for x in g]
            dzb = []
            for n, (w, h) in enumerate(chains):
                sub, _, diag = work[w]
                gsum = gloc[n] + cgs[sub][h]
                dz = g[n] - jnp.exp2(lb[n]) * (g[n] + gsum)
                dzb.append((jnp.where(causal, dz, 0.0) if diag else dz).astype(BF16))
                cgs[sub][h] = gsum[tq - 1:tq, :] + g[n][tq - 1:tq, :]
            ab = [x.astype(BF16) for x in a]
            for n, (w, h) in enumerate(chains):
                sub = work[w][0]
                dqt[sub] = dqt[sub] + _dot(jnp.where(sub_head == h, kt[w], jnp.zeros_like(kt[w])), dzb[n])
            for w, (sub, _, _) in enumerate(work):
                dk_ref[rows[w], :] += _dot(dzb[2 * w], qs[sub][0]) + _dot(dzb[2 * w + 1], qs[sub][1])
                dv_ref[rows[w], :] += _dot(ab[2 * w], doh[sub][0]) + _dot(ab[2 * w + 1], doh[sub][1])
            return {sub: (cgs[sub][0], cgs[sub][1], dqt[sub]) for sub in cgs}

        zero = jnp.zeros((1, tq), F32)
        carry = {}
        for sub in subs:
            i = index[sub]
            n = jnp.clip(n_ref[pair, i].astype(jnp.int32), 1, i + 1)
            carry[sub] = lax.fori_loop(jnp.minimum(i + 1 - n, i - 1), i - 1,
                                       lambda s, cr, sub=sub: blocks([(sub, s, False)], {sub: cr})[sub],
                                       (zero, zero, jnp.zeros((128, tq), F32)))
        both = lambda: blocks([(sub, index[sub] - 1 + b, b == 1) for sub in subs for b in range(2)], carry)
        if len(subs) == 1:
            carry = lax.cond(index[subs[0]] > 0, both, lambda: blocks([(subs[0], index[subs[0]], True)], carry))
        else:
            carry = both()
        for sub in subs:
            dqt_ref[:, mine[sub]] = carry[sub][2] * 0.125

    return _call(
        body, name, (SB_PAIRS, nb // qb),
        [pl.BlockSpec((qb * tq, 128), lambda p, i: (i, p)),
         pl.BlockSpec((t, 128), lambda p, i: (0, SB_PAIRS + p)),
         pl.BlockSpec((t, 128), lambda p, i: (0, 2 * SB_PAIRS + p)),
         pl.BlockSpec((None, nb, 128, tq), lambda p, i: (p, 0, 0, 0)),
         pl.BlockSpec((qb * tq, 128), lambda p, i: (i, p)),
         pl.BlockSpec((128, qb * tq), lambda p, i: (p, i)),
         pl.BlockSpec((2, nb, qb * tq), lambda p, i: (p, 0, i)),
         pl.BlockSpec(memory_space=pltpu.SMEM),
         pl.BlockSpec((tq, tq), lambda p, i: (0, 0)),
         pl.BlockSpec((tq, tq), lambda p, i: (0, 0))],
        [pl.BlockSpec((128, qb * tq), lambda p, i: (p, i)),
         pl.BlockSpec((t, 128), lambda p, i: (0, p)),
         pl.BlockSpec((t, 128), lambda p, i: (0, p))],
        [jax.ShapeDtypeStruct((SB_WIDTH, t), F32), jax.ShapeDtypeStruct((t, SB_WIDTH), F32),
         jax.ShapeDtypeStruct((t, SB_WIDTH), F32)],
        [], ("arbitrary", "arbitrary"), (qkv, qkv, qkv, ktb, dyc, dyct, cc, nvis, upper, lower), ride)


def _sg_bwd(ab, dyb, gn, wm, wmt, bfull, tm, name):
    t = ab.shape[0]
    nt = t // tm
    sel = (jnp.arange(SG_WIDTH)[:, None] // SG_HD == jnp.arange(CHUNK)[None, :]).astype(F32)

    def body(u_ref, v_ref, dy_ref, gn_ref, wm_ref, wmt_ref, b_ref, sel_ref,
             dup_ref, dvp_ref, dgn_ref, dw_ref, db_ref, dbacc_ref):
        i = pl.program_id(0)

        @pl.when(i == 0)
        def _():
            dgn_ref[...] = jnp.zeros_like(dgn_ref)
            dw_ref[...] = jnp.zeros_like(dw_ref)
            dbacc_ref[...] = jnp.zeros_like(dbacc_ref)

        tril = (lax.broadcasted_iota(jnp.int32, (CHUNK, CHUNK), 0) >= lax.broadcasted_iota(jnp.int32, (CHUNK, CHUNK), 1))
        gn_ = gn_ref[...]
        for c in range(tm // CHUNK):
            rows = slice(c * CHUNK, (c + 1) * CHUNK)
            up, vp, dy = u_ref[rows, :], v_ref[rows, :], dy_ref[rows, :]
            u, v = _gelu(up), _gelu(vp)
            r = _rstd(v)
            vn = (v * r * gn_).astype(BF16)
            sv = b_ref[...]
            for h in range(SG_HEADS):
                sv = sv + jnp.where(_head_lanes(h), _dot(wm_ref[h], vn), 0.0)
            dup_ref[rows, :] = dy * sv * _gelu_grad(up)
            dsv = dy * u
            dbacc_ref[...] += dsv
            dvn = jnp.zeros((CHUNK, SG_WIDTH), F32)
            for h in range(SG_HEADS):
                dsv_h = jnp.where(_head_lanes(h), dsv, 0.0).astype(BF16)
                dvn = dvn + _dot(wmt_ref[h], dsv_h)
                dw_ref[h] += jnp.where(tril, _dot_nt(dsv_h, vn), 0.0)
            dv, dgn = _rms_bwd(v, r, gn_, dvn)
            dgn_ref[...] += dgn
            dvp_ref[rows, :] = dv * _gelu_grad(vp)

        @pl.when(i == nt - 1)
        def _():
            db_ref[...] = jnp.dot(dbacc_ref[...], sel_ref[...], preferred_element_type=F32,
                                  precision=lax.Precision.HIGHEST)

    const = lambda shape: pl.BlockSpec(shape, lambda i: tuple(0 for _ in shape))
    return pl.pallas_call(
        body, name=name, grid=(nt,),
        in_specs=[pl.BlockSpec((tm, SG_WIDTH), lambda i: (i, 1)), pl.BlockSpec((tm, SG_WIDTH), lambda i: (i, 2)),
                  pl.BlockSpec((tm, SG_WIDTH), lambda i: (i, 0)), const((1, SG_WIDTH)),
                  const((SG_HEADS, CHUNK, CHUNK)), const((SG_HEADS, CHUNK, CHUNK)), const((CHUNK, SG_WIDTH)),
                  const((SG_WIDTH, CHUNK))],
        out_specs=[pl.BlockSpec((tm, SG_WIDTH), lambda i: (i, 0)), pl.BlockSpec((tm, SG_WIDTH), lambda i: (i, 0)),
                   const((1, SG_WIDTH)), const((SG_HEADS, CHUNK, CHUNK)), const((CHUNK, CHUNK))],
        out_shape=[jax.ShapeDtypeStruct((t, SG_WIDTH), F32), jax.ShapeDtypeStruct((t, SG_WIDTH), F32),
                   jax.ShapeDtypeStruct((1, SG_WIDTH), F32), jax.ShapeDtypeStruct((SG_HEADS, CHUNK, CHUNK), F32),
                   jax.ShapeDtypeStruct((CHUNK, CHUNK), F32)],
        scratch_shapes=[pltpu.VMEM((CHUNK, SG_WIDTH), F32)],
        compiler_params=_params(("arbitrary",)),
    )(ab, ab, dyb, gn, wm, wmt, bfull, sel)


def _pool_bwd(ab, dya, wbd, scale, tm, name):
    t = ab.shape[0]
    nt = t // tm
    hb = tm // POOL_HALO
    nh = t // POOL_HALO

    def body(cur_ref, prev_ref, dy_ref, dyn_ref, w_ref, s_ref, da_ref, dw_ref, ds_ref):
        i = pl.program_id(0)

        @pl.when(i == 0)
        def _():
            dw_ref[...] = jnp.zeros_like(dw_ref)
            ds_ref[...] = jnp.zeros_like(ds_ref)

        prev = jnp.where(i == 0, 0.0, prev_ref[...])
        d = _pool_diff(cur_ref[...], prev, i * tm).astype(BF16)
        dy = dy_ref[...]
        ds_ref[...] += jnp.sum(dy * _dot(d, w_ref[...]), axis=0, keepdims=True)
        dyn = jnp.where(i == nt - 1, 0.0, dyn_ref[...])
        dys = (jnp.concatenate([dy, dyn], axis=0) * s_ref[...]).astype(BF16)
        dw_ref[...] += _dot_tn(d, dys[:tm])
        dd = _dot_nt(dys, w_ref[...])
        fwd = _pool_window_sums(dd / _pool_count(i * tm, tm + POOL_HALO), True)
        da_ref[...] = fwd[:tm] - dd[:tm]

    return pl.pallas_call(
        body, name=name, grid=(nt,),
        in_specs=[pl.BlockSpec((tm, POOL_WIDTH), lambda i: (i, 0)),
                  pl.BlockSpec((POOL_HALO, POOL_WIDTH), lambda i: (jnp.maximum(i * hb - 1, 0), 0)),
                  pl.BlockSpec((tm, POOL_WIDTH), lambda i: (i, 0)),
                  pl.BlockSpec((POOL_HALO, POOL_WIDTH), lambda i: (jnp.minimum((i + 1) * hb, nh - 1), 0)),
                  pl.BlockSpec((POOL_WIDTH, POOL_WIDTH), lambda i: (0, 0)),
                  pl.BlockSpec((1, POOL_WIDTH), lambda i: (0, 0))],
        out_specs=[pl.BlockSpec((tm, POOL_WIDTH), lambda i: (i, 0)),
                   pl.BlockSpec((POOL_WIDTH, POOL_WIDTH), lambda i: (0, 0)),
                   pl.BlockSpec((1, POOL_WIDTH), lambda i: (0, 0))],
        out_shape=[jax.ShapeDtypeStruct((t, POOL_WIDTH), F32), jax.ShapeDtypeStruct((POOL_WIDTH, POOL_WIDTH), F32),
                   jax.ShapeDtypeStruct((1, POOL_WIDTH), F32)],
        compiler_params=_params(("arbitrary",)),
    )(ab, ab, dya, dya, wbd, scale)


def _inproj_bwd(dx1, x, g, da, dup, dvp, dqt, dk, dv, w, tm, name, ride=None):
    t, d = x.shape
    n = w.shape[1]
    nt = t // tm

    def body(dx1_ref, x_ref, g_ref, da_ref, du_ref, dv_ref, dqt_ref, dk_ref, dvv_ref, w_ref,
             dx_ref, h_ref, dp_ref, dg_ref):
        @pl.when(pl.program_id(0) == 0)
        def _():
            dg_ref[...] = jnp.zeros_like(dg_ref)

        dp = jnp.concatenate([da_ref[...], du_ref[...], dv_ref[...], dqt_ref[...].T, dk_ref[...], dvv_ref[...]],
                             axis=1).astype(BF16)
        dp_ref[...] = dp
        xx = x_ref[...]
        r = _rstd(xx)
        h_ref[...] = (xx * r * g_ref[...]).astype(BF16)
        dx, dg = _rms_bwd(xx, r, g_ref[...], _dot_nt(dp, w_ref[...]))
        dx_ref[...] = dx1_ref[...] + dx
        dg_ref[...] += dg

    row = lambda w_: pl.BlockSpec((tm, w_), lambda i: (i, 0))
    return _call(
        body, name, (nt,),
        [row(d), row(d), pl.BlockSpec((1, d), lambda i: (0, 0)), row(POOL_WIDTH), row(SG_WIDTH),
         row(SG_WIDTH), pl.BlockSpec((SB_WIDTH, tm), lambda i: (0, i)), row(SB_WIDTH), row(SB_WIDTH),
         pl.BlockSpec((d, n), lambda i: (0, 0))],
        [row(d), row(d), row(n), pl.BlockSpec((1, d), lambda i: (0, 0))],
        [jax.ShapeDtypeStruct((t, d), F32), jax.ShapeDtypeStruct((t, d), BF16),
         jax.ShapeDtypeStruct((t, n), BF16), jax.ShapeDtypeStruct((1, d), F32)],
        [], ("arbitrary",), (dx1, x, g, da, dup, dvp, dqt, dk, dv, w), ride)


def _full_w_in(gathered):
    return gathered.transpose(1, 0, 2).reshape(D_MODEL, IN_COLS)


_SMALL_SHAPES = ((D_MODEL,), (4, POOL_GW, POOL_GW), (POOL_WIDTH,), (SG_WIDTH,), (SG_HEADS, CHUNK, CHUNK),
                 (SG_HEADS, CHUNK), (D_MODEL,))
_SMALL_SIZES = tuple(functools.reduce(lambda p, q: p * q, shp) for shp in _SMALL_SHAPES)
_SMALL_ROWS = sum(_SMALL_SIZES) // 128
_NORM1_ROWS = D_MODEL // 128


def _pack_small_layer(arrs):
    return jnp.concatenate([a.reshape(-1) for a in arrs]).reshape(_SMALL_ROWS, 128)


def _pack_small(arrs, final):
    return jnp.concatenate([_pack_small_layer([a[l] for a in arrs]) for l in range(DEPTH)] + [final.reshape(-1, 128)])


def _unpack_small(buf):
    per_layer = []
    for l in range(DEPTH):
        flat, off, outs = buf[l * _SMALL_ROWS:(l + 1) * _SMALL_ROWS].reshape(-1), 0, []
        for shp, size in zip(_SMALL_SHAPES, _SMALL_SIZES):
            outs.append(flat[off:off + size].reshape(shp))
            off += size
        per_layer.append(outs)
    return [jnp.stack([per_layer[l][k] for l in range(DEPTH)]) for k in range(len(_SMALL_SHAPES))] + \
           [buf[DEPTH * _SMALL_ROWS:].reshape(-1)]


def _tiles(t):
    return min(512, t), min(256, t // 4), min(2048, t)


def _layer_fwd(xl, wi, wo, wud, small_w, l, ride_mlp=None):
    n1, pw, ps, sn, sw, sb, n2 = small_w
    tm, tq, _ = _tiles(xl.shape[0])
    wbd = jnp.zeros((4, POOL_GW, 4, POOL_GW), F32)
    for gi in range(4):
        wbd = wbd.at[gi, :, gi, :].set(pw[gi])
    wbd = wbd.reshape(POOL_WIDTH, POOL_WIDTH).astype(BF16)
    wm = (sw * jnp.tril(jnp.ones((CHUNK, CHUNK), F32))).astype(BF16)
    wmt = wm.transpose(0, 2, 1)
    bfull = jnp.repeat(sb.T, SG_HD, axis=1)
    g1, g2, psc, gn = n1[None, :], n2[None, :], ps[None, :], sn[None, :]

    if isinstance(wo, Exchange):
        (ab, qkv, ktb, vtb), (wo,) = _inproj_fwd(xl, g1, wi, tm, tq, f"inproj_fwd{l}", wo)
    else:
        ab, qkv, ktb, vtb = _inproj_fwd(xl, g1, wi, tm, tq, f"inproj_fwd{l}")
    ya = _pool_fwd(ab, wbd, psc, tm, f"pool_fwd{l}")
    yb = _sg_fwd(ab, gn, wm, bfull, tm, f"sg_fwd{l}")
    if isinstance(wud, Exchange):
        (yct, cc, nvis), (wu, wd) = _sba_fwd(qkv, vtb, tq, f"sba_fwd{l}", wud)
    else:
        (yct, cc, nvis), (wu, wd) = _sba_fwd(qkv, vtb, tq, f"sba_fwd{l}"), wud
    wo = wo.reshape(D_MODEL, D_MODEL)
    res = _outproj_mlp_fwd(xl, ya, yb, yct, wo, g2, wu, wd, min(MLP_ROWS, xl.shape[0]), f"mlp_fwd{l}", ride_mlp)
    (x1, u, x2), rode = res if ride_mlp is not None else (res, None)
    saved = dict(x=xl, ab=ab, qkv=qkv, ktb=ktb, cc=cc, nvis=nvis, ya=ya, yb=yb, yct=yct, x1=x1, u=u, wi=wi, wo=wo, wu=wu,
                 wd=wd, wbd=wbd, wm=wm, wmt=wmt, bfull=bfull, g1=g1, g2=g2, psc=psc, gn=gn)
    return x2, saved, rode


def _layer_bwd(dx, s, l, ride_mlp=None, scatter_in_attn=True, gather_small=False):
    tm, tq, tw = _tiles(dx.shape[0])
    ktb = s["ktb"]
    res = _mlp_bwd(dx, s["x1"], s["g2"], s["u"], s["wu"], s["wd"], min(MLP_ROWS, dx.shape[0]), f"mlp_bwd{l}", ride_mlp)
    (dx1, du, r, h2, dx2b, dn2), rode = res if ride_mlp is not None else (res, None)
    dw_up = _matmul_tn(h2, du, D_MODEL, 2 * FF_SHARD, tw, f"dw_up{l}", by_column_block=FF_SHARD)
    dw_down = _matmul_tn(r, dx2b, 1024, D_MODEL, tw, f"dw_down{l}").reshape(N_DEV, FF_SHARD, D_MODEL)
    dya, dyb, dyc, dyct, dx1b = _outproj_bwd(dx1, s["wo"], tm, f"outproj_bwd{l}")
    dw_out = jnp.concatenate([
        _matmul_tn(jnp.concatenate([s["ya"], s["yb"]], axis=1), dx1b, POOL_WIDTH + SG_WIDTH, D_MODEL, tw, f"dw_out_ab{l}"),
        _matmul_tn(s["yct"], dx1b, SB_WIDTH, D_MODEL, tw, f"dw_out_c{l}", a_transposed=True)]
    ).reshape(N_DEV, OUT_SHARD, D_MODEL)
    if scatter_in_attn:
        (dqt, dk, dv), (dw_out, dw_up, dw_down) = _sba_bwd(s["qkv"], ktb, dyc, dyct, s["cc"], s["nvis"], tq, f"sba_bwd{l}",
                                                           Exchange([dw_out, dw_up, dw_down], True))
    else:
        dqt, dk, dv = _sba_bwd(s["qkv"], ktb, dyc, dyct, s["cc"], s["nvis"], tq, f"sba_bwd{l}")
    dup, dvp, dgn, dwm, dbm = _sg_bwd(s["ab"], dyb, s["gn"], s["wm"], s["wmt"], s["bfull"], tm, f"sg_bwd{l}")
    da, dwbd, dpsc = _pool_bwd(s["ab"], dya, s["wbd"], s["psc"], tm, f"pool_bwd{l}")
    dpw = jnp.stack([dwbd[gi * POOL_GW:(gi + 1) * POOL_GW, gi * POOL_GW:(gi + 1) * POOL_GW] for gi in range(4)])
    small = _pack_small_layer([jnp.zeros((D_MODEL,), F32), dpw, dpsc[0], dgn[0], dwm, dbm[:, :SG_HEADS].T, dn2[0]])[_NORM1_ROWS:]
    res = _inproj_bwd(dx1, s["x"], s["g1"], da, dup, dvp, dqt, dk, dv, s["wi"], tm, f"inproj_bwd{l}",
                      Exchange([small], False) if gather_small else None)
    (dx, h1, dproj, dn1), small = (res[0], res[1][0]) if gather_small else (res, small)
    dw_in = _matmul_tn(h1, dproj, D_MODEL, IN_COLS // 3, tw, f"dw_in{l}")
    dw_in = dw_in.reshape(D_MODEL, N_DEV, IN_SHARD).transpose(1, 0, 2)
    return dx, (dw_in, dw_out, dw_up, dw_down), (dn1.reshape(_NORM1_ROWS, 128), small), rode


def kernel(x, norm1, w_in, pool_w, pool_scale, sg_norm, sg_w, sg_b, w_out, norm2, w_up, w_down, final_norm, loss_target, m_norm1, m_w_in, m_pool_w, m_pool_scale, m_sg_norm, m_sg_w, m_sg_b, m_w_out, m_norm2, m_w_up, m_w_down, m_final_norm, v_norm1, v_w_in, v_pool_w, v_pool_scale, v_sg_norm, v_sg_w, v_sg_b, v_w_out, v_norm2, v_w_up, v_w_down, v_final_norm):
    t = x.shape[1]
    tm = _tiles(t)[0]
    small_w = (norm1, pool_w, pool_scale, sg_norm, sg_w, sg_b, norm2)
    big_w = (w_in, w_out, w_up, w_down)
    big_m = (m_w_in, m_w_out, m_w_up, m_w_down)
    big_v = (v_w_in, v_w_out, v_w_up, v_w_down)
    shards = [[w[l].astype(BF16) for w in big_w] for l in range(DEPTH)]

    wi0 = _full_w_in(_gather_via_sibling(shards[0][0], "gather_w_in0"))
    x1, s0, g1 = _layer_fwd(x.reshape(t, D_MODEL), wi0, Exchange(shards[0][1:2], False), Exchange(shards[0][2:], False),
                            tuple(w[0] for w in small_w), 0, ride_mlp=Exchange(shards[1][:2], False))
    x2, s1, _ = _layer_fwd(x1, _full_w_in(g1[0]), g1[1], Exchange(shards[1][2:], False), tuple(w[1] for w in small_w), 1)
    loss_local, dx, dfinal = _loss_grad(x2, final_norm[None, :], loss_target.reshape(t, D_MODEL), tm, "loss_grad")
    loss = lax.psum(loss_local[0, 0], MESH_AXES)

    dx, parts1, small1, _ = _layer_bwd(dx, s1, 1)
    early = jnp.concatenate(list(small1) + [dfinal.reshape(-1, 128)])
    dx, parts0, (dn1, small0), (recv_in1, early) = _layer_bwd(
        dx, s0, 0, ride_mlp=Exchange([parts1[0], early], [True, False]), gather_small=True)
    grad_x = dx.reshape(x.shape)
    recv_in0, dn1 = _exchange([parts0[0], dn1], [True, False], "scatter_w_in0_gather_norm1_0")
    small_all = jnp.concatenate([dn1, small0, early], axis=1)
    received = [[recv_in0] + list(parts0[1:]), [recv_in1] + list(parts1[1:])]

    big = [None] * 4
    for l in reversed(range(DEPTH)):
        for k in range(4):
            big[k] = _reduce_adamw(received[l][k], big_w[k], big_m[k], big_v[k], l, big[k], f"adamw{k}_{l}")

    sm = _reduce_adamw(
        small_all,
        _pack_small(small_w, final_norm)[None],
        _pack_small([m_norm1, m_pool_w, m_pool_scale, m_sg_norm, m_sg_w, m_sg_b, m_norm2], m_final_norm)[None],
        _pack_small([v_norm1, v_pool_w, v_pool_scale, v_sg_norm, v_sg_w, v_sg_b, v_norm2], v_final_norm)[None],
        0, None, "adamw_replicated")

    out = [loss, grad_x]
    for k in range(4):
        n1, pw, ps, sn, sw, sb, n2, fn = _unpack_small(sm[k][0])
        out += [n1, big[0][k], pw, ps, sn, sw, sb, big[1][k], n2, big[2][k], big[3][k], fn]
    return tuple(out)
```

```python
import functools

import jax
import jax.numpy as jnp
from jax import lax
from jax.experimental import pallas as pl
from jax.experimental.pallas import tpu as pltpu

F32 = jnp.float32
BF16 = jnp.bfloat16

D_MODEL = 1024
DEPTH = 2
POOL_WIDTH = 256
SG_WIDTH = 256
SB_WIDTH = 512
POOL_WINDOWS = (2, 4, 8, 16)
POOL_GW = 64
POOL_HALO = 16
CHUNK = 128
SG_HEADS = 4
SG_HD = 64
SB_HD = 64
SB_PAIRS = SB_WIDTH // 128
AB_COLS = POOL_WIDTH + 2 * SG_WIDTH
IN_COLS = AB_COLS + 3 * SB_WIDTH
D_FF = 4096
EPS = 1e-6
N_DEV = 8
FF_SHARD = D_FF // N_DEV
IN_SHARD = IN_COLS // N_DEV
OUT_SHARD = D_MODEL // N_DEV
ADAM_LR = 0.001
ADAM_B1 = 0.9
ADAM_B2 = 0.999
ADAM_EPS = 1e-08
ADAM_WD = 0.01
ADAM_STEP = 10
VMEM_LIMIT = 56 * 1024 * 1024
MLP_ROWS = 1024
MESH_AXES = ("x", "y", "c")


def _dot(a, b):
    return jnp.dot(a, b, preferred_element_type=F32)


def _dot_nt(a, b):
    return lax.dot_general(a, b, (((1,), (1,)), ((), ())), preferred_element_type=F32)


def _dot_tn(a, b):
    return lax.dot_general(a, b, (((0,), (0,)), ((), ())), preferred_element_type=F32)


def _rstd(x):
    return lax.rsqrt(jnp.mean(x * x, axis=-1, keepdims=True) + EPS)


def _rms_bwd(x, r, g, dh):
    gq = dh * g
    dx = r * gq - x * (r * r * r) * jnp.mean(gq * x, axis=-1, keepdims=True)
    dg = jnp.sum(dh * x * r, axis=0, keepdims=True)
    return dx, dg


def _params(sem=None):
    kw = dict(vmem_limit_bytes=VMEM_LIMIT)
    if sem is not None:
        kw["dimension_semantics"] = sem
    return pltpu.CompilerParams(**kw)


def _row_tile(rows, cap):
    best = 8
    for t in range(8, min(rows, cap) + 1, 8):
        if rows % t == 0:
            best = t
    return best


def _peer(k):
    x, y, c = lax.axis_index("x"), lax.axis_index("y"), lax.axis_index("c")
    return (1 - x if k & 4 else x, 1 - y if k & 2 else y, 1 - c if k & 1 else c)


def _my_index():
    return 4 * lax.axis_index("x") + 2 * lax.axis_index("y") + lax.axis_index("c")


class Exchange:
    def __init__(self, arrs, scatter):
        self.arrs = list(arrs)
        self.scatter = list(scatter) if isinstance(scatter, (list, tuple)) else [scatter] * len(self.arrs)
        self.n = len(self.arrs)
        self.any_specs = [pl.BlockSpec(memory_space=pl.ANY)] * self.n
        self.out_shape = [jax.ShapeDtypeStruct((N_DEV,) + a.shape[-2:], a.dtype) for a in self.arrs]
        self.sems = [pltpu.SemaphoreType.DMA((self.n, N_DEV - 1)), pltpu.SemaphoreType.DMA((self.n, N_DEV - 1)),
                     pltpu.SemaphoreType.DMA((self.n,))]

    def _copies(self, ins, outs, sems):
        send_sems, recv_sems, local_sems = sems
        me = _my_index()
        local, remote = [], []
        for a in range(self.n):
            sc = self.scatter[a]
            local.append(pltpu.make_async_copy(ins[a].at[me] if sc else ins[a], outs[a].at[me], local_sems.at[a]))
            for k in range(1, N_DEV):
                px, py, pc = _peer(k)
                remote.append(pltpu.make_async_remote_copy(
                    src_ref=ins[a].at[4 * px + 2 * py + pc] if sc else ins[a], dst_ref=outs[a].at[me],
                    send_sem=send_sems.at[a, k - 1], recv_sem=recv_sems.at[a, k - 1],
                    device_id=(px, py, pc), device_id_type=pl.DeviceIdType.MESH))
        return local, remote

    def start(self, ins, outs, sems):
        local, remote = self._copies(ins, outs, sems)
        for cp in local + remote:
            cp.start()

    def wait(self, ins, outs, sems):
        local, remote = self._copies(ins, outs, sems)
        for cp in remote:
            cp.wait_recv()
        for cp in remote:
            cp.wait_send()
        for cp in local:
            cp.wait()

    def alone(self, name):
        n = self.n

        def body(*refs):
            ins, outs, sems = refs[:n], refs[n:2 * n], refs[2 * n:]
            self.start(ins, outs, sems)
            self.wait(ins, outs, sems)

        return pl.pallas_call(body, name=name, out_shape=self.out_shape, in_specs=self.any_specs,
                              out_specs=self.any_specs, scratch_shapes=self.sems)(*self.arrs)


def _exchange(arrs, scatter, name):
    return Exchange(arrs, scatter).alone(name)


def _gather_via_sibling(shard, name):
    r, c = shard.shape

    def body(x_ref, out_ref, send_sems, recv_sems, local_sem):
        x, y, core = lax.axis_index("x"), lax.axis_index("y"), lax.axis_index("c")
        me, sibling = (x, y, core), (x, y, 1 - core)
        chips = [(1 - x, y), (x, 1 - y), (1 - x, 1 - y)]

        def slot(px, py, pc):
            return out_ref.at[4 * px + 2 * py + pc]

        def copy(k, block, to, src=None):
            return pltpu.make_async_remote_copy(
                src_ref=slot(*block) if src is None else src, dst_ref=slot(*block),
                send_sem=send_sems.at[k], recv_sem=recv_sems.at[k], device_id=to, device_id_type=pl.DeviceIdType.MESH)

        mine = pltpu.make_async_copy(x_ref, slot(*me), local_sem)
        mine.start()
        first = [copy(0, me, sibling, src=x_ref)] + [copy(1 + j, me, (*chip, core), src=x_ref) for j, chip in enumerate(chips)]
        for cp in first:
            cp.start()
        passed = [copy(4 + j, (*chip, core), sibling) for j, chip in enumerate(chips)]
        for j, chip in enumerate(chips):
            copy(1 + j, (*chip, core), me).wait_recv()
            passed[j].start()
        copy(0, sibling, me).wait_recv()
        for j, chip in enumerate(chips):
            copy(4 + j, (*chip, 1 - core), me).wait_recv()
        for cp in first + passed:
            cp.wait_send()
        mine.wait()

    return pl.pallas_call(
        body, name=name, out_shape=jax.ShapeDtypeStruct((N_DEV, r, c), shard.dtype),
        in_specs=[pl.BlockSpec(memory_space=pltpu.VMEM)], out_specs=pl.BlockSpec(memory_space=pltpu.VMEM),
        scratch_shapes=[pltpu.SemaphoreType.DMA((N_DEV - 1,)), pltpu.SemaphoreType.DMA((N_DEV - 1,)),
                        pltpu.SemaphoreType.DMA],
        compiler_params=pltpu.CompilerParams(vmem_limit_bytes=VMEM_LIMIT),
    )(shard)


def _call(body, name, grid, in_specs, out_specs, out_shape, scratch_shapes, semantics, args, ride=None):
    if ride is None:
        return pl.pallas_call(body, name=name, grid=grid, in_specs=in_specs, out_specs=out_specs, out_shape=out_shape,
                              scratch_shapes=scratch_shapes, compiler_params=_params(semantics))(*args)
    single = not isinstance(out_shape, (list, tuple))
    out_specs, out_shape = ([out_specs], [out_shape]) if single else (list(out_specs), list(out_shape))
    n_in, n_out, n_scr, n = len(in_specs), len(out_specs), len(scratch_shapes), ride.n

    def riding(*refs):
        ins, cins = refs[:n_in], refs[n_in:n_in + n]
        outs, couts = refs[n_in + n:n_in + n + n_out], refs[n_in + n + n_out:n_in + 2 * n + n_out]
        scr, sems = refs[n_in + 2 * n + n_out:n_in + 2 * n + n_out + n_scr], refs[n_in + 2 * n + n_out + n_scr:]
        first = functools.reduce(lambda p, q: p & q, [pl.program_id(a) == 0 for a in range(len(grid))])
        last = functools.reduce(lambda p, q: p & q, [pl.program_id(a) == g - 1 for a, g in enumerate(grid)])

        @pl.when(first)
        def _():
            ride.start(cins, couts, sems)

        body(*ins, *outs, *scr)

        @pl.when(last)
        def _():
            ride.wait(cins, couts, sems)

    res = pl.pallas_call(
        riding, name=name, grid=grid, in_specs=list(in_specs) + ride.any_specs, out_specs=out_specs + ride.any_specs,
        out_shape=out_shape + ride.out_shape, scratch_shapes=list(scratch_shapes) + ride.sems,
        compiler_params=_params(("arbitrary",) * len(grid)))(*args, *ride.arrs)
    own = res[0] if single else list(res[:n_out])
    return own, list(res[n_out:])


def _reduce_adamw(parts, w, m, v, l, prev, name):
    _, rows, n = parts.shape
    tr = _row_tile(rows, max(8, (1 << 18) // n))
    c1 = 1.0 - ADAM_B1 ** ADAM_STEP
    c2 = 1.0 - ADAM_B2 ** ADAM_STEP

    def body(p_ref, w_ref, m_ref, v_ref, *rest):
        g_ref, d_ref, nm_ref, nv_ref = rest[-4:]
        g = p_ref[0].astype(F32)
        for s in range(1, N_DEV):
            g = g + p_ref[s].astype(F32)
        nm = ADAM_B1 * m_ref[...] + (1.0 - ADAM_B1) * g
        nv = ADAM_B2 * v_ref[...] + (1.0 - ADAM_B2) * (g * g)
        m_hat = nm / c1
        v_hat = nv / c2
        g_ref[...] = g
        d_ref[...] = -ADAM_LR * (m_hat / (jnp.sqrt(v_hat) + ADAM_EPS) + ADAM_WD * w_ref[...])
        nm_ref[...] = nm
        nv_ref[...] = nv

    blk = pl.BlockSpec((None, tr, n), lambda i: (l, i, 0))
    out = jax.ShapeDtypeStruct(w.shape, F32)
    prev = list(prev) if prev is not None else []
    return pl.pallas_call(
        body, name=name, grid=(rows // tr,),
        in_specs=[pl.BlockSpec((N_DEV, tr, n), lambda i: (0, i, 0)), blk, blk, blk] + [pl.BlockSpec(memory_space=pl.ANY)] * len(prev),
        out_specs=[blk, blk, blk, blk], out_shape=[out, out, out, out],
        input_output_aliases={4 + k: k for k in range(len(prev))},
        compiler_params=_params(("parallel",)),
    )(parts, w, m, v, *prev)


def _inproj_fwd(x, g, w, tm, tq, name, ride=None):
    t, d = x.shape
    n = w.shape[1]
    nb = t // tq
    per = tm // tq

    def body(x_ref, g_ref, w_ref, ab_ref, qkv_ref, kt_ref, vt_ref):
        xx = x_ref[...]
        h = (xx * _rstd(xx) * g_ref[...]).astype(BF16)
        ab_ref[...] = _dot(h, w_ref[:, :AB_COLS])
        qkv = _dot(h, w_ref[:, AB_COLS:])
        qkv_ref[...] = qkv.astype(BF16)
        for which, out_ref in ((1, kt_ref), (2, vt_ref)):
            for p in range(SB_PAIRS):
                for b in range(per):
                    cols = which * SB_WIDTH + p * 128
                    out_ref[p, b] = qkv[b * tq:(b + 1) * tq, cols:cols + 128].T.astype(BF16)

    tb = pl.BlockSpec((SB_PAIRS, per, 128, tq), lambda i: (0, i, 0, 0))
    tshape = jax.ShapeDtypeStruct((SB_PAIRS, nb, 128, tq), BF16)
    return _call(
        body, name, (t // tm,),
        [pl.BlockSpec((tm, d), lambda i: (i, 0)), pl.BlockSpec((1, d), lambda i: (0, 0)),
         pl.BlockSpec((d, n), lambda i: (0, 0))],
        [pl.BlockSpec((tm, AB_COLS), lambda i: (i, 0)), pl.BlockSpec((tm, n - AB_COLS), lambda i: (i, 0)), tb, tb],
        [jax.ShapeDtypeStruct((t, AB_COLS), F32), jax.ShapeDtypeStruct((t, n - AB_COLS), BF16), tshape, tshape],
        [], ("parallel",), (x, g, w), ride)


def _pool_window_sums(xx, forward):
    n = xx.shape[0]
    sh = (lambda k: n - k) if forward else (lambda k: k)
    s2 = xx + pltpu.roll(xx, sh(1), 0)
    s4 = s2 + pltpu.roll(s2, sh(2), 0)
    s8 = s4 + pltpu.roll(s4, sh(4), 0)
    s16 = s8 + pltpu.roll(s8, sh(8), 0)
    grp = lax.broadcasted_iota(jnp.int32, (1, POOL_WIDTH), 1) // POOL_GW
    return jnp.where(grp == 0, s2, jnp.where(grp == 1, s4, jnp.where(grp == 2, s8, s16)))


def _pool_count(t0, rows):
    grp = lax.broadcasted_iota(jnp.int32, (1, POOL_WIDTH), 1) // POOL_GW
    win = jnp.where(grp == 0, 2, jnp.where(grp == 1, 4, jnp.where(grp == 2, 8, 16)))
    tt = t0 + lax.broadcasted_iota(jnp.int32, (rows, 1), 0)
    return jnp.minimum(tt + 1, win).astype(F32)


def _pool_diff(cur, prev, t0):
    tm = cur.shape[0]
    sums = _pool_window_sums(jnp.concatenate([prev, cur], axis=0), False)[POOL_HALO:]
    return sums / _pool_count(t0, tm) - cur


def _pool_fwd(ab, wbd, scale, tm, name):
    t = ab.shape[0]
    hb = tm // POOL_HALO

    def body(cur_ref, prev_ref, w_ref, s_ref, y_ref):
        i = pl.program_id(0)
        prev = jnp.where(i == 0, 0.0, prev_ref[...])
        d = _pool_diff(cur_ref[...], prev, i * tm).astype(BF16)
        y_ref[...] = (_dot(d, w_ref[...]) * s_ref[...]).astype(BF16)

    return pl.pallas_call(
        body, name=name, grid=(t // tm,),
        in_specs=[pl.BlockSpec((tm, POOL_WIDTH), lambda i: (i, 0)),
                  pl.BlockSpec((POOL_HALO, POOL_WIDTH), lambda i: (jnp.maximum(i * hb - 1, 0), 0)),
                  pl.BlockSpec((POOL_WIDTH, POOL_WIDTH), lambda i: (0, 0)),
                  pl.BlockSpec((1, POOL_WIDTH), lambda i: (0, 0))],
        out_specs=pl.BlockSpec((tm, POOL_WIDTH), lambda i: (i, 0)),
        out_shape=jax.ShapeDtypeStruct((t, POOL_WIDTH), BF16),
        compiler_params=_params(("parallel",)),
    )(ab, ab, wbd, scale)


_GELU_K = 0.7978845608028654
_GELU_A = 0.044715


def _gelu(x):
    return 0.5 * x * (1.0 + jnp.tanh(_GELU_K * (x + _GELU_A * (x * x * x))))


def _gelu_grad(x):
    th = jnp.tanh(_GELU_K * (x + _GELU_A * (x * x * x)))
    return 0.5 * (1.0 + th) + 0.5 * x * (1.0 - th * th) * (_GELU_K * (1.0 + 3.0 * _GELU_A * (x * x)))


def _head_lanes(h):
    return lax.broadcasted_iota(jnp.int32, (1, SG_WIDTH), 1) // SG_HD == h


def _sg_fwd(ab, gn, wm, bfull, tm, name):
    t = ab.shape[0]

    def body(u_ref, v_ref, gn_ref, wm_ref, b_ref, y_ref):
        v = _gelu(v_ref[...])
        vn = (v * _rstd(v) * gn_ref[...]).astype(BF16)
        for c in range(tm // CHUNK):
            rows = slice(c * CHUNK, (c + 1) * CHUNK)
            vc = vn[rows]
            sv = b_ref[...]
            for h in range(SG_HEADS):
                sv = sv + jnp.where(_head_lanes(h), _dot(wm_ref[h], vc), 0.0)
            y_ref[rows, :] = (_gelu(u_ref[rows, :]) * sv).astype(BF16)

    return pl.pallas_call(
        body, name=name, grid=(t // tm,),
        in_specs=[pl.BlockSpec((tm, SG_WIDTH), lambda i: (i, 1)), pl.BlockSpec((tm, SG_WIDTH), lambda i: (i, 2)),
                  pl.BlockSpec((1, SG_WIDTH), lambda i: (0, 0)),
                  pl.BlockSpec((SG_HEADS, CHUNK, CHUNK), lambda i: (0, 0, 0)),
                  pl.BlockSpec((CHUNK, SG_WIDTH), lambda i: (0, 0))],
        out_specs=pl.BlockSpec((tm, SG_WIDTH), lambda i: (i, 0)),
        out_shape=jax.ShapeDtypeStruct((t, SG_WIDTH), BF16),
        compiler_params=_params(("parallel",)),
    )(ab, ab, gn, wm, bfull)


LOG2E = 1.4426950408889634
SB_SCALE = 0.125 * LOG2E
SB_DEAD_LOG2 = 152.0
SB_QUERY_BLOCKS_PER_STEP = 2


def _log2_sigmoids(y):
    neg_abs = lax.bitcast_convert_type(lax.bitcast_convert_type(y, jnp.uint32) | jnp.uint32(0x80000000), F32)
    lb = jnp.minimum(y, 0.0) - jnp.log(1.0 + jnp.exp2(neg_abs)) * LOG2E
    return lb, lb - y


def _split(x):
    hi = x.astype(BF16)
    return hi, (x - hi.astype(F32)).astype(BF16)


def _tri_dot(tri, x):
    hi, lo = _split(x)
    return _dot(tri, hi) + _dot(tri, lo)


def _sba_fwd(qkv, vtb, tq, name, ride=None):
    t = qkv.shape[0]
    nb = t // tq
    qb = SB_QUERY_BLOCKS_PER_STEP
    upper = (jnp.arange(tq)[None, :] > jnp.arange(tq)[:, None]).astype(BF16)

    def body(q_ref, k_ref, vt_ref, up_ref, ot_ref, c_ref, n_ref):
        pair, step = pl.program_id(0), pl.program_id(1)
        refs = (q_ref, k_ref, vt_ref, up_ref, ot_ref, c_ref, n_ref)
        subs = list(range(qb))

        @pl.when(step == 0)
        def _():
            for sub in subs:
                query_blocks([sub], pair, step, *refs)

        @pl.when(step > 0)
        def _():
            query_blocks(subs, pair, step, *refs)

    def query_blocks(subs, pair, step, q_ref, k_ref, vt_ref, up_ref, ot_ref, c_ref, n_ref):
        index = {sub: step * qb + sub for sub in subs}
        mine = {sub: slice(sub * tq, (sub + 1) * tq) for sub in subs}
        up = up_ref[...]
        lane_head = lax.broadcasted_iota(jnp.int32, (1, 128), 1) // SB_HD
        sub_head = lax.broadcasted_iota(jnp.int32, (128, 1), 0) // SB_HD
        causal = (lax.broadcasted_iota(jnp.int32, (tq, tq), 0) < lax.broadcasted_iota(jnp.int32, (tq, tq), 1))
        qh = {}
        for sub in subs:
            q = q_ref[mine[sub], :]
            qh[sub] = [jnp.where(lane_head == h, q, jnp.zeros_like(q)) for h in range(2)]

        def blocks(work, carry):
            cs = {sub: list(carry[sub][:2]) for sub in carry}
            acc = {sub: carry[sub][2] for sub in carry}
            kj = [k_ref[pl.ds(pl.multiple_of(j * tq, tq), tq), :] for _, j, _ in work]
            vt = [vt_ref[j] for _, j, _ in work]
            chains = [(w, h) for w in range(len(work)) for h in range(2)]
            z = [_dot_nt(kj[w], qh[work[w][0]][h]) for w, h in chains]
            ls = [_log2_sigmoids(zz * SB_SCALE) for zz in z]
            lb = [x[0] for x in ls]
            l1 = [jnp.where(causal, x[1], 0.0) if work[w][2] else x[1] for x, (w, h) in zip(ls, chains)]
            after = [_tri_dot(up, x) for x in l1]
            a = []
            for n, (w, h) in enumerate(chains):
                sub, j, diag = work[w]
                c_ref[h, pl.ds(j, 1), mine[sub]] = cs[sub][h]
                an = jnp.exp2(lb[n] + after[n] + cs[sub][h])
                a.append(jnp.where(causal, an, 0.0) if diag else an)
                cs[sub][h] = cs[sub][h] + after[n][0:1, :] + l1[n][0:1, :]
            alive = {sub: (jnp.max(jnp.maximum(*cs[sub])) > -SB_DEAD_LOG2).astype(jnp.int32) for sub in cs}
            for n, (w, h) in enumerate(chains):
                sub = work[w][0]
                acc[sub] = acc[sub] + _dot(jnp.where(sub_head == h, vt[w], jnp.zeros_like(vt[w])), a[n].astype(BF16))
            return {sub: (alive[sub], cs[sub][0], cs[sub][1], acc[sub]) for sub in cs}

        zero = jnp.zeros((1, tq), F32)
        start = {sub: (zero, zero, jnp.zeros((128, tq), F32)) for sub in subs}
        both = lambda: blocks([(sub, index[sub] - b, b == 0) for sub in subs for b in range(2)], start)
        if len(subs) == 1:
            joint = lax.cond(index[subs[0]] > 0, both, lambda: blocks([(subs[0], index[subs[0]], True)], start))
        else:
            joint = both()
        for sub in subs:
            i = index[sub]

            def left(state, sub=sub, i=i):
                s, _, c0, c1, acc = state
                return (s + 1,) + blocks([(sub, i - 1 - s, False)], {sub: (c0, c1, acc)})[sub]

            state = lax.while_loop(lambda st, i=i: (st[0] < i) & (st[1] > 0), left, (jnp.minimum(i, 1),) + tuple(joint[sub]))
            ot_ref[:, mine[sub]] = state[4].astype(BF16)
            n_ref[pair, i] = (state[0] + 1).astype(F32)

    return _call(
        body, name, (SB_PAIRS, nb // qb),
        [pl.BlockSpec((qb * tq, 128), lambda p, i: (i, p)),
         pl.BlockSpec((t, 128), lambda p, i: (0, SB_PAIRS + p)),
         pl.BlockSpec((None, nb, 128, tq), lambda p, i: (p, 0, 0, 0)),
         pl.BlockSpec((tq, tq), lambda p, i: (0, 0))],
        [pl.BlockSpec((128, qb * tq), lambda p, i: (p, i)),
         pl.BlockSpec((2, nb, qb * tq), lambda p, i: (p, 0, i)),
         pl.BlockSpec(memory_space=pltpu.SMEM)],
        [jax.ShapeDtypeStruct((SB_WIDTH, t), BF16),
         jax.ShapeDtypeStruct((2 * SB_PAIRS, nb, t), F32),
         jax.ShapeDtypeStruct((SB_PAIRS, nb), F32)],
        [], ("arbitrary", "arbitrary"), (qkv, qkv, vtb, upper), ride)


def _outproj_mlp_fwd(x, ya, yb, yct, wo, g2, wup, wdn, tm, name, ride=None):
    t, d = x.shape
    nf = wup.shape[0]

    def body(x_ref, ya_ref, yb_ref, yct_ref, wo_ref, g_ref, wu_ref, wd_ref, x1_ref, u_ref, x2_ref, h_ref, acc_ref):
        j = pl.program_id(1)

        @pl.when(j == 0)
        def _():
            x1 = (x_ref[...] + _dot(ya_ref[...], wo_ref[0:POOL_WIDTH, :])
                  + _dot(yb_ref[...], wo_ref[POOL_WIDTH:POOL_WIDTH + SG_WIDTH, :])
                  + _dot_tn(yct_ref[...], wo_ref[POOL_WIDTH + SG_WIDTH:, :]))
            x1_ref[...] = x1
            h_ref[...] = (x1 * _rstd(x1) * g_ref[...]).astype(BF16)
            acc_ref[...] = x1

        u = _dot(h_ref[...], wu_ref[...])
        u_ref[...] = u
        r = jnp.maximum(u, 0.0)
        acc_ref[...] += _dot((r * r).astype(BF16), wd_ref[...])

        @pl.when(j == nf - 1)
        def _():
            x2_ref[...] = acc_ref[...]

    row = lambda w: pl.BlockSpec((tm, w), lambda i, j: (i, 0))
    return _call(
        body, name, (t // tm, nf),
        [row(d), row(POOL_WIDTH), row(SG_WIDTH), pl.BlockSpec((SB_WIDTH, tm), lambda i, j: (0, i)),
         pl.BlockSpec((d, d), lambda i, j: (0, 0)), pl.BlockSpec((1, d), lambda i, j: (0, 0)),
         pl.BlockSpec((None, d, FF_SHARD), lambda i, j: (j, 0, 0)),
         pl.BlockSpec((None, FF_SHARD, d), lambda i, j: (j, 0, 0))],
        [row(d), pl.BlockSpec((tm, FF_SHARD), lambda i, j: (i, j)), row(d)],
        [jax.ShapeDtypeStruct((t, d), F32), jax.ShapeDtypeStruct((t, nf * FF_SHARD), F32),
         jax.ShapeDtypeStruct((t, d), F32)],
        [pltpu.VMEM((tm, d), BF16), pltpu.VMEM((tm, d), F32)],
        ("parallel", "arbitrary"), (x, ya, yb, yct, wo, g2, wup, wdn), ride)


def _loss_grad(x, g, target, tm, name):
    t, d = x.shape
    nt = t // tm

    def body(x_ref, g_ref, t_ref, loss_ref, dx_ref, dg_ref, sq_ref):
        i = pl.program_id(0)

        @pl.when(i == 0)
        def _():
            sq_ref[...] = jnp.zeros_like(sq_ref)
            dg_ref[...] = jnp.zeros_like(dg_ref)

        xx = x_ref[...]
        r = _rstd(xx)
        err = xx * r * g_ref[...] - t_ref[...]
        sq_ref[...] += jnp.sum(err * err, axis=0, keepdims=True)
        dx, dg = _rms_bwd(xx, r, g_ref[...], err * (1.0 / d))
        dx_ref[...] = dx
        dg_ref[...] += dg

        @pl.when(i == nt - 1)
        def _():
            loss_ref[...] = jnp.sum(sq_ref[...], axis=1, keepdims=True) * (0.5 / d)

    return pl.pallas_call(
        body, name=name, grid=(nt,),
        in_specs=[pl.BlockSpec((tm, d), lambda i: (i, 0)), pl.BlockSpec((1, d), lambda i: (0, 0)),
                  pl.BlockSpec((tm, d), lambda i: (i, 0))],
        out_specs=[pl.BlockSpec((1, 1), lambda i: (0, 0)), pl.BlockSpec((tm, d), lambda i: (i, 0)),
                   pl.BlockSpec((1, d), lambda i: (0, 0))],
        out_shape=[jax.ShapeDtypeStruct((1, 1), F32), jax.ShapeDtypeStruct((t, d), F32),
                   jax.ShapeDtypeStruct((1, d), F32)],
        scratch_shapes=[pltpu.VMEM((1, d), F32)],
        compiler_params=_params(("arbitrary",)),
    )(x, g, target)


def _mlp_bwd(dx2, x1, g2, u, wup, wdn, tm, name, ride=None):
    t, d = dx2.shape
    nf = wup.shape[0]
    nt = t // tm

    def body(dx2_ref, x1_ref, g_ref, u_ref, wu_ref, wd_ref, dx1_ref, du_ref, r_ref, h_ref, dxb_ref, dg_ref, acc_ref):
        i, j = pl.program_id(0), pl.program_id(1)

        @pl.when(j == 0)
        def _():
            x1 = x1_ref[...]
            h_ref[...] = (x1 * _rstd(x1) * g_ref[...]).astype(BF16)
            dxb_ref[...] = dx2_ref[...].astype(BF16)
            acc_ref[...] = jnp.zeros_like(acc_ref)

        @pl.when((i == 0) & (j == 0))
        def _():
            dg_ref[...] = jnp.zeros_like(dg_ref)

        dr = _dot_nt(dxb_ref[...], wd_ref[...])
        ru = jnp.maximum(u_ref[...], 0.0)
        du = (dr * (2.0 * ru)).astype(BF16)
        du_ref[...] = du
        r_ref[...] = (ru * ru).astype(BF16)
        acc_ref[...] += _dot_nt(du, wu_ref[...])

        @pl.when(j == nf - 1)
        def _():
            x1 = x1_ref[...]
            dx, dg = _rms_bwd(x1, _rstd(x1), g_ref[...], acc_ref[...])
            dx1_ref[...] = dx2_ref[...] + dx
            dg_ref[...] += dg

    row = lambda w: pl.BlockSpec((tm, w), lambda i, j: (i, 0))
    col = pl.BlockSpec((tm, FF_SHARD), lambda i, j: (i, j))
    return _call(
        body, name, (nt, nf),
        [row(d), row(d), pl.BlockSpec((1, d), lambda i, j: (0, 0)), col,
         pl.BlockSpec((None, d, FF_SHARD), lambda i, j: (j, 0, 0)),
         pl.BlockSpec((None, FF_SHARD, d), lambda i, j: (j, 0, 0))],
        [row(d), col, col, row(d), row(d), pl.BlockSpec((1, d), lambda i, j: (0, 0))],
        [jax.ShapeDtypeStruct((t, d), F32), jax.ShapeDtypeStruct((t, nf * FF_SHARD), BF16),
         jax.ShapeDtypeStruct((t, nf * FF_SHARD), BF16), jax.ShapeDtypeStruct((t, d), BF16),
         jax.ShapeDtypeStruct((t, d), BF16), jax.ShapeDtypeStruct((1, d), F32)],
        [pltpu.VMEM((tm, d), F32)], ("arbitrary", "arbitrary"), (dx2, x1, g2, u, wup, wdn), ride)


def _matmul_tn(a, b, bm, bn, bt, name, by_column_block=0, a_transposed=False):
    m, t = a.shape if a_transposed else a.shape[::-1]
    n = b.shape[1]
    nk = t // bt

    def body(a_ref, b_ref, o_ref, acc_ref):
        k = pl.program_id(2)

        @pl.when(k == 0)
        def _():
            acc_ref[...] = jnp.zeros_like(acc_ref)

        acc_ref[...] += _dot(a_ref[...], b_ref[...]) if a_transposed else _dot_tn(a_ref[...], b_ref[...])

        @pl.when(k == nk - 1)
        def _():
            if by_column_block:
                for c in range(bn // by_column_block):
                    o_ref[c] = acc_ref[:, c * by_column_block:(c + 1) * by_column_block].astype(BF16)
            else:
                o_ref[...] = acc_ref[...].astype(BF16)

    if by_column_block:
        out_spec = pl.BlockSpec((bn // by_column_block, bm, by_column_block), lambda i, j, k: (j, i, 0))
        out_shape = jax.ShapeDtypeStruct((n // by_column_block, m, by_column_block), BF16)
    else:
        out_spec = pl.BlockSpec((bm, bn), lambda i, j, k: (i, j))
        out_shape = jax.ShapeDtypeStruct((m, n), BF16)
    a_spec = (pl.BlockSpec((bm, bt), lambda i, j, k: (i, k)) if a_transposed
              else pl.BlockSpec((bt, bm), lambda i, j, k: (k, i)))
    return _call(body, name, (m // bm, n // bn, nk), [a_spec, pl.BlockSpec((bt, bn), lambda i, j, k: (k, j))],
                 out_spec, out_shape, [pltpu.VMEM((bm, bn), F32)], ("parallel", "parallel", "arbitrary"), (a, b))


def _outproj_bwd(dx1, wo, tm, name):
    t, d = dx1.shape
    c2 = POOL_WIDTH + SG_WIDTH

    def body(dx_ref, wo_ref, dya_ref, dyb_ref, dyc_ref, dyct_ref, dxb_ref):
        dxb = dx_ref[...].astype(BF16)
        dxb_ref[...] = dxb
        dya_ref[...] = _dot_nt(dxb, wo_ref[0:POOL_WIDTH, :])
        dyb_ref[...] = _dot_nt(dxb, wo_ref[POOL_WIDTH:c2, :])
        dyc_ref[...] = _dot_nt(dxb, wo_ref[c2:, :]).astype(BF16)
        dyct_ref[...] = _dot_nt(wo_ref[c2:, :], dxb).astype(BF16)

    row = lambda w: pl.BlockSpec((tm, w), lambda i: (i, 0))
    return pl.pallas_call(
        body, name=name, grid=(t // tm,),
        in_specs=[row(d), pl.BlockSpec((d, d), lambda i: (0, 0))],
        out_specs=[row(POOL_WIDTH), row(SG_WIDTH), row(SB_WIDTH), pl.BlockSpec((SB_WIDTH, tm), lambda i: (0, i)), row(d)],
        out_shape=[jax.ShapeDtypeStruct((t, POOL_WIDTH), F32), jax.ShapeDtypeStruct((t, SG_WIDTH), F32),
                   jax.ShapeDtypeStruct((t, SB_WIDTH), BF16), jax.ShapeDtypeStruct((SB_WIDTH, t), BF16),
                   jax.ShapeDtypeStruct((t, d), BF16)],
        compiler_params=_params(("parallel",)),
    )(dx1, wo)


def _sba_bwd(qkv, ktb, dyc, dyct, cc, nvis, tq, name, ride=None):
    t = qkv.shape[0]
    nb = t // tq
    idx = jnp.arange(tq)
    upper = (idx[None, :] > idx[:, None]).astype(BF16)
    lower = (idx[None, :] < idx[:, None]).astype(BF16)

    qb = SB_QUERY_BLOCKS_PER_STEP

    def body(q_ref, k_ref, v_ref, kt_ref, do_ref, dot_ref, c_ref, n_ref, up_ref, lo_ref, dqt_ref, dk_ref, dv_ref):
        pair, step = pl.program_id(0), pl.program_id(1)
        refs = (q_ref, k_ref, v_ref, kt_ref, do_ref, dot_ref, c_ref, n_ref, up_ref, lo_ref, dqt_ref, dk_ref, dv_ref)
        subs = list(range(qb))

        @pl.when(step == 0)
        def _():
            dk_ref[...] = jnp.zeros_like(dk_ref)
            dv_ref[...] = jnp.zeros_like(dv_ref)
            for sub in subs:
                query_blocks([sub], pair, step, *refs)

        @pl.when(step > 0)
        def _():
            query_blocks(subs, pair, step, *refs)

    def query_blocks(subs, pair, step, q_ref, k_ref, v_ref, kt_ref, do_ref, dot_ref, c_ref, n_ref, up_ref, lo_ref,
                     dqt_ref, dk_ref, dv_ref):
        index = {sub: step * qb + sub for sub in subs}
        mine = {sub: slice(sub * tq, (sub + 1) * tq) for sub in subs}
        up = up_ref[...]
        lo = lo_ref[...]
        lane_head = lax.broadcasted_iota(jnp.int32, (1, 128), 1) // SB_HD
        sub_head = lax.broadcasted_iota(jnp.int32, (128, 1), 0) // SB_HD
        causal = (lax.broadcasted_iota(jnp.int32, (tq, tq), 0) < lax.broadcasted_iota(jnp.int32, (tq, tq), 1))
        hms = [lane_head == h for h in range(2)]
        qh, qs, doh, dot = {}, {}, {}, {}
        for sub in subs:
            q, do = q_ref[mine[sub], :], do_ref[mine[sub], :]
            qh[sub] = [jnp.where(hm, q, jnp.zeros_like(q)) for hm in hms]
            qs[sub] = [x * 0.125 for x in qh[sub]]
            doh[sub] = [jnp.where(hm, do, jnp.zeros_like(do)) for hm in hms]
            dot[sub] = dot_ref[:, mine[sub]]

        def blocks(work, carry):
            cgs = {sub: list(carry[sub][:2]) for sub in carry}
            dqt = {sub: carry[sub][2] for sub in carry}
            rows = [pl.ds(pl.multiple_of(j * tq, tq), tq) for _, j, _ in work]
            kj = [k_ref[r, :] for r in rows]
            vj = [v_ref[r, :] for r in rows]
            kt = [kt_ref[j] for _, j, _ in work]
            chains = [(w, h) for w in range(len(work)) for h in range(2)]
            z = [_dot_nt(kj[w], qh[work[w][0]][h]) for w, h in chains]
            da = [_dot(jnp.where(hms[h], vj[w], jnp.zeros_like(vj[w])), dot[work[w][0]]) for w, h in chains]
            ls = [_log2_sigmoids(zz * SB_SCALE) for zz in z]
            lb = [x[0] for x in ls]
            l1 = [jnp.where(causal, x[1], 0.0) if work[w][2] else x[1] for x, (w, h) in zip(ls, chains)]
            after = [_tri_dot(up, x) for x in l1]
            a = [jnp.exp2(lb[n] + after[n] + c_ref[h, pl.ds(work[w][1], 1), mine[work[w][0]]])
                 for n, (w, h) in enumerate(chains)]
            a = [jnp.where(causal, a[n], 0.0) if work[w][2] else a[n] for n, (w, h) in enumerate(chains)]
            g = [a[n] * da[n] for n in range(len(chains))]
            gloc = [_dot(lo, x.astype(BF16)) for x in g]
            dzb = []
            for n, (w, h) in enumerate(chains):
                sub, _, diag = work[w]
                gsum = gloc[n] + cgs[sub][h]
                dz = g[n] - jnp.exp2(lb[n]) * (g[n] + gsum)
                dzb.append((jnp.where(causal, dz, 0.0) if diag else dz).astype(BF16))
                cgs[sub][h] = gsum[tq - 1:tq, :] + g[n][tq - 1:tq, :]
            ab = [x.astype(BF16) for x in a]
            for n, (w, h) in enumerate(chains):
                sub = work[w][0]
                dqt[sub] = dqt[sub] + _dot(jnp.where(sub_head == h, kt[w], jnp.zeros_like(kt[w])), dzb[n])
            for w, (sub, _, _) in enumerate(work):
                dk_ref[rows[w], :] += _dot(dzb[2 * w], qs[sub][0]) + _dot(dzb[2 * w + 1], qs[sub][1])
                dv_ref[rows[w], :] += _dot(ab[2 * w], doh[sub][0]) + _dot(ab[2 * w + 1], doh[sub][1])
            return {sub: (cgs[sub][0], cgs[sub][1], dqt[sub]) for sub in cgs}

        zero = jnp.zeros((1, tq), F32)
        carry = {}
        for sub in subs:
            i = index[sub]
            n = jnp.clip(n_ref[pair, i].astype(jnp.int32), 1, i + 1)
            carry[sub] = lax.fori_loop(jnp.minimum(i + 1 - n, i - 1), i - 1,
                                       lambda s, cr, sub=sub: blocks([(sub, s, False)], {sub: cr})[sub],
                                       (zero, zero, jnp.zeros((128, tq), F32)))
        both = lambda: blocks([(sub, index[sub] - 1 + b, b == 1) for sub in subs for b in range(2)], carry)
        if len(subs) == 1:
            carry = lax.cond(index[subs[0]] > 0, both, lambda: blocks([(subs[0], index[subs[0]], True)], carry))
        else:
            carry = both()
        for sub in subs:
            dqt_ref[:, mine[sub]] = carry[sub][2] * 0.125

    return _call(
        body, name, (SB_PAIRS, nb // qb),
        [pl.BlockSpec((qb * tq, 128), lambda p, i: (i, p)),
         pl.BlockSpec((t, 128), lambda p, i: (0, SB_PAIRS + p)),
         pl.BlockSpec((t, 128), lambda p, i: (0, 2 * SB_PAIRS + p)),
         pl.BlockSpec((None, nb, 128, tq), lambda p, i: (p, 0, 0, 0)),
         pl.BlockSpec((qb * tq, 128), lambda p, i: (i, p)),
         pl.BlockSpec((128, qb * tq), lambda p, i: (p, i)),
         pl.BlockSpec((2, nb, qb * tq), lambda p, i: (p, 0, i)),
         pl.BlockSpec(memory_space=pltpu.SMEM),
         pl.BlockSpec((tq, tq), lambda p, i: (0, 0)),
         pl.BlockSpec((tq, tq), lambda p, i: (0, 0))],
        [pl.BlockSpec((128, qb * tq), lambda p, i: (p, i)),
         pl.BlockSpec((t, 128), lambda p, i: (0, p)),
         pl.BlockSpec((t, 128), lambda p, i: (0, p))],
        [jax.ShapeDtypeStruct((SB_WIDTH, t), F32), jax.ShapeDtypeStruct((t, SB_WIDTH), F32),
         jax.ShapeDtypeStruct((t, SB_WIDTH), F32)],
        [], ("arbitrary", "arbitrary"), (qkv, qkv, qkv, ktb, dyc, dyct, cc, nvis, upper, lower), ride)


def _sg_bwd(ab, dyb, gn, wm, wmt, bfull, tm, name):
    t = ab.shape[0]
    nt = t // tm
    sel = (jnp.arange(SG_WIDTH)[:, None] // SG_HD == jnp.arange(CHUNK)[None, :]).astype(F32)

    def body(u_ref, v_ref, dy_ref, gn_ref, wm_ref, wmt_ref, b_ref, sel_ref,
             dup_ref, dvp_ref, dgn_ref, dw_ref, db_ref, dbacc_ref):
        i = pl.program_id(0)

        @pl.when(i == 0)
        def _():
            dgn_ref[...] = jnp.zeros_like(dgn_ref)
            dw_ref[...] = jnp.zeros_like(dw_ref)
            dbacc_ref[...] = jnp.zeros_like(dbacc_ref)

        tril = (lax.broadcasted_iota(jnp.int32, (CHUNK, CHUNK), 0) >= lax.broadcasted_iota(jnp.int32, (CHUNK, CHUNK), 1))
        gn_ = gn_ref[...]
        for c in range(tm // CHUNK):
            rows = slice(c * CHUNK, (c + 1) * CHUNK)
            up, vp, dy = u_ref[rows, :], v_ref[rows, :], dy_ref[rows, :]
            u, v = _gelu(up), _gelu(vp)
            r = _rstd(v)
            vn = (v * r * gn_).astype(BF16)
            sv = b_ref[...]
            for h in range(SG_HEADS):
                sv = sv + jnp.where(_head_lanes(h), _dot(wm_ref[h], vn), 0.0)
            dup_ref[rows, :] = dy * sv * _gelu_grad(up)
            dsv = dy * u
            dbacc_ref[...] += dsv
            dvn = jnp.zeros((CHUNK, SG_WIDTH), F32)
            for h in range(SG_HEADS):
                dsv_h = jnp.where(_head_lanes(h), dsv, 0.0).astype(BF16)
                dvn = dvn + _dot(wmt_ref[h], dsv_h)
                dw_ref[h] += jnp.where(tril, _dot_nt(dsv_h, vn), 0.0)
            dv, dgn = _rms_bwd(v, r, gn_, dvn)
            dgn_ref[...] += dgn
            dvp_ref[rows, :] = dv * _gelu_grad(vp)

        @pl.when(i == nt - 1)
        def _():
            db_ref[...] = jnp.dot(dbacc_ref[...], sel_ref[...], preferred_element_type=F32,
                                  precision=lax.Precision.HIGHEST)

    const = lambda shape: pl.BlockSpec(shape, lambda i: tuple(0 for _ in shape))
    return pl.pallas_call(
        body, name=name, grid=(nt,),
        in_specs=[pl.BlockSpec((tm, SG_WIDTH), lambda i: (i, 1)), pl.BlockSpec((tm, SG_WIDTH), lambda i: (i, 2)),
                  pl.BlockSpec((tm, SG_WIDTH), lambda i: (i, 0)), const((1, SG_WIDTH)),
                  const((SG_HEADS, CHUNK, CHUNK)), const((SG_HEADS, CHUNK, CHUNK)), const((CHUNK, SG_WIDTH)),
                  const((SG_WIDTH, CHUNK))],
        out_specs=[pl.BlockSpec((tm, SG_WIDTH), lambda i: (i, 0)), pl.BlockSpec((tm, SG_WIDTH), lambda i: (i, 0)),
                   const((1, SG_WIDTH)), const((SG_HEADS, CHUNK, CHUNK)), const((CHUNK, CHUNK))],
        out_shape=[jax.ShapeDtypeStruct((t, SG_WIDTH), F32), jax.ShapeDtypeStruct((t, SG_WIDTH), F32),
                   jax.ShapeDtypeStruct((1, SG_WIDTH), F32), jax.ShapeDtypeStruct((SG_HEADS, CHUNK, CHUNK), F32),
                   jax.ShapeDtypeStruct((CHUNK, CHUNK), F32)],
        scratch_shapes=[pltpu.VMEM((CHUNK, SG_WIDTH), F32)],
        compiler_params=_params(("arbitrary",)),
    )(ab, ab, dyb, gn, wm, wmt, bfull, sel)


def _pool_bwd(ab, dya, wbd, scale, tm, name):
    t = ab.shape[0]
    nt = t // tm
    hb = tm // POOL_HALO
    nh = t // POOL_HALO

    def body(cur_ref, prev_ref, dy_ref, dyn_ref, w_ref, s_ref, da_ref, dw_ref, ds_ref):
        i = pl.program_id(0)

        @pl.when(i == 0)
        def _():
            dw_ref[...] = jnp.zeros_like(dw_ref)
            ds_ref[...] = jnp.zeros_like(ds_ref)

        prev = jnp.where(i == 0, 0.0, prev_ref[...])
        d = _pool_diff(cur_ref[...], prev, i * tm).astype(BF16)
        dy = dy_ref[...]
        ds_ref[...] += jnp.sum(dy * _dot(d, w_ref[...]), axis=0, keepdims=True)
        dyn = jnp.where(i == nt - 1, 0.0, dyn_ref[...])
        dys = (jnp.concatenate([dy, dyn], axis=0) * s_ref[...]).astype(BF16)
        dw_ref[...] += _dot_tn(d, dys[:tm])
        dd = _dot_nt(dys, w_ref[...])
        fwd = _pool_window_sums(dd / _pool_count(i * tm, tm + POOL_HALO), True)
        da_ref[...] = fwd[:tm] - dd[:tm]

    return pl.pallas_call(
        body, name=name, grid=(nt,),
        in_specs=[pl.BlockSpec((tm, POOL_WIDTH), lambda i: (i, 0)),
                  pl.BlockSpec((POOL_HALO, POOL_WIDTH), lambda i: (jnp.maximum(i * hb - 1, 0), 0)),
                  pl.BlockSpec((tm, POOL_WIDTH), lambda i: (i, 0)),
                  pl.BlockSpec((POOL_HALO, POOL_WIDTH), lambda i: (jnp.minimum((i + 1) * hb, nh - 1), 0)),
                  pl.BlockSpec((POOL_WIDTH, POOL_WIDTH), lambda i: (0, 0)),
                  pl.BlockSpec((1, POOL_WIDTH), lambda i: (0, 0))],
        out_specs=[pl.BlockSpec((tm, POOL_WIDTH), lambda i: (i, 0)),
                   pl.BlockSpec((POOL_WIDTH, POOL_WIDTH), lambda i: (0, 0)),
                   pl.BlockSpec((1, POOL_WIDTH), lambda i: (0, 0))],
        out_shape=[jax.ShapeDtypeStruct((t, POOL_WIDTH), F32), jax.ShapeDtypeStruct((POOL_WIDTH, POOL_WIDTH), F32),
                   jax.ShapeDtypeStruct((1, POOL_WIDTH), F32)],
        compiler_params=_params(("arbitrary",)),
    )(ab, ab, dya, dya, wbd, scale)


def _inproj_bwd(dx1, x, g, da, dup, dvp, dqt, dk, dv, w, tm, name, ride=None):
    t, d = x.shape
    n = w.shape[1]
    nt = t // tm

    def body(dx1_ref, x_ref, g_ref, da_ref, du_ref, dv_ref, dqt_ref, dk_ref, dvv_ref, w_ref,
             dx_ref, h_ref, dp_ref, dg_ref):
        @pl.when(pl.program_id(0) == 0)
        def _():
            dg_ref[...] = jnp.zeros_like(dg_ref)

        dp = jnp.concatenate([da_ref[...], du_ref[...], dv_ref[...], dqt_ref[...].T, dk_ref[...], dvv_ref[...]],
                             axis=1).astype(BF16)
        dp_ref[...] = dp
        xx = x_ref[...]
        r = _rstd(xx)
        h_ref[...] = (xx * r * g_ref[...]).astype(BF16)
        dx, dg = _rms_bwd(xx, r, g_ref[...], _dot_nt(dp, w_ref[...]))
        dx_ref[...] = dx1_ref[...] + dx
        dg_ref[...] += dg

    row = lambda w_: pl.BlockSpec((tm, w_), lambda i: (i, 0))
    return _call(
        body, name, (nt,),
        [row(d), row(d), pl.BlockSpec((1, d), lambda i: (0, 0)), row(POOL_WIDTH), row(SG_WIDTH),
         row(SG_WIDTH), pl.BlockSpec((SB_WIDTH, tm), lambda i: (0, i)), row(SB_WIDTH), row(SB_WIDTH),
         pl.BlockSpec((d, n), lambda i: (0, 0))],
        [row(d), row(d), row(n), pl.BlockSpec((1, d), lambda i: (0, 0))],
        [jax.ShapeDtypeStruct((t, d), F32), jax.ShapeDtypeStruct((t, d), BF16),
         jax.ShapeDtypeStruct((t, n), BF16), jax.ShapeDtypeStruct((1, d), F32)],
        [], ("arbitrary",), (dx1, x, g, da, dup, dvp, dqt, dk, dv, w), ride)


def _full_w_in(gathered):
    return gathered.transpose(1, 0, 2).reshape(D_MODEL, IN_COLS)


_SMALL_SHAPES = ((D_MODEL,), (4, POOL_GW, POOL_GW), (POOL_WIDTH,), (SG_WIDTH,), (SG_HEADS, CHUNK, CHUNK),
                 (SG_HEADS, CHUNK), (D_MODEL,))
_SMALL_SIZES = tuple(functools.reduce(lambda p, q: p * q, shp) for shp in _SMALL_SHAPES)
_SMALL_ROWS = sum(_SMALL_SIZES) // 128
_NORM1_ROWS = D_MODEL // 128


def _pack_small_layer(arrs):
    return jnp.concatenate([a.reshape(-1) for a in arrs]).reshape(_SMALL_ROWS, 128)


def _pack_small(arrs, final):
    return jnp.concatenate([_pack_small_layer([a[l] for a in arrs]) for l in range(DEPTH)] + [final.reshape(-1, 128)])


def _unpack_small(buf):
    per_layer = []
    for l in range(DEPTH):
        flat, off, outs = buf[l * _SMALL_ROWS:(l + 1) * _SMALL_ROWS].reshape(-1), 0, []
        for shp, size in zip(_SMALL_SHAPES, _SMALL_SIZES):
            outs.append(flat[off:off + size].reshape(shp))
            off += size
        per_layer.append(outs)
    return [jnp.stack([per_layer[l][k] for l in range(DEPTH)]) for k in range(len(_SMALL_SHAPES))] + \
           [buf[DEPTH * _SMALL_ROWS:].reshape(-1)]


def _tiles(t):
    return min(512, t), min(256, t // 4), min(2048, t)


def _layer_fwd(xl, wi, wo, wu, wd, small_w, l, ride_mlp=None):
    n1, pw, ps, sn, sw, sb, n2 = small_w
    tm, tq, _ = _tiles(xl.shape[0])
    wbd = jnp.zeros((4, POOL_GW, 4, POOL_GW), F32)
    for gi in range(4):
        wbd = wbd.at[gi, :, gi, :].set(pw[gi])
    wbd = wbd.reshape(POOL_WIDTH, POOL_WIDTH).astype(BF16)
    wm = (sw * jnp.tril(jnp.ones((CHUNK, CHUNK), F32))).astype(BF16)
    wmt = wm.transpose(0, 2, 1)
    bfull = jnp.repeat(sb.T, SG_HD, axis=1)
    g1, g2, psc, gn = n1[None, :], n2[None, :], ps[None, :], sn[None, :]

    if wo.ndim == 2:
        (ab, qkv, ktb, vtb), (wo,) = _inproj_fwd(xl, g1, wi, tm, tq, f"inproj_fwd{l}", Exchange([wo], False))
    else:
        ab, qkv, ktb, vtb = _inproj_fwd(xl, g1, wi, tm, tq, f"inproj_fwd{l}")
    ya = _pool_fwd(ab, wbd, psc, tm, f"pool_fwd{l}")
    yb = _sg_fwd(ab, gn, wm, bfull, tm, f"sg_fwd{l}")
    shards = [w for w in (wu, wd) if w.ndim == 2]
    if shards:
        (yct, cc, nvis), got = _sba_fwd(qkv, vtb, tq, f"sba_fwd{l}", Exchange(shards, False))
        got = iter(got)
        wu, wd = (next(got) if w.ndim == 2 else w for w in (wu, wd))
    else:
        yct, cc, nvis = _sba_fwd(qkv, vtb, tq, f"sba_fwd{l}")
    wo = wo.reshape(D_MODEL, D_MODEL)
    res = _outproj_mlp_fwd(xl, ya, yb, yct, wo, g2, wu, wd, min(MLP_ROWS, xl.shape[0]), f"mlp_fwd{l}", ride_mlp)
    (x1, u, x2), rode = res if ride_mlp is not None else (res, None)
    saved = dict(x=xl, ab=ab, qkv=qkv, ktb=ktb, cc=cc, nvis=nvis, ya=ya, yb=yb, yct=yct, x1=x1, u=u, wi=wi, wo=wo, wu=wu,
                 wd=wd, wbd=wbd, wm=wm, wmt=wmt, bfull=bfull, g1=g1, g2=g2, psc=psc, gn=gn)
    return x2, saved, rode


def _layer_bwd(dx, s, l, ride_mlp=None, exchange=True, gather_small=False):
    tm, tq, tw = _tiles(dx.shape[0])
    ktb = s["ktb"]
    res = _mlp_bwd(dx, s["x1"], s["g2"], s["u"], s["wu"], s["wd"], min(MLP_ROWS, dx.shape[0]), f"mlp_bwd{l}", ride_mlp)
    (dx1, du, r, h2, dx2b, dn2), rode = res if ride_mlp is not None else (res, None)
    dw_up = _matmul_tn(h2, du, D_MODEL, 2 * FF_SHARD, tw, f"dw_up{l}", by_column_block=FF_SHARD)
    dw_down = _matmul_tn(r, dx2b, 1024, D_MODEL, tw, f"dw_down{l}").reshape(N_DEV, FF_SHARD, D_MODEL)
    dya, dyb, dyc, dyct, dx1b = _outproj_bwd(dx1, s["wo"], tm, f"outproj_bwd{l}")
    dw_out = jnp.concatenate([
        _matmul_tn(jnp.concatenate([s["ya"], s["yb"]], axis=1), dx1b, POOL_WIDTH + SG_WIDTH, D_MODEL, tw, f"dw_out_ab{l}"),
        _matmul_tn(s["yct"], dx1b, SB_WIDTH, D_MODEL, tw, f"dw_out_c{l}", a_transposed=True)]
    ).reshape(N_DEV, OUT_SHARD, D_MODEL)
    if exchange:
        (dqt, dk, dv), (dw_up, dw_down) = _sba_bwd(s["qkv"], ktb, dyc, dyct, s["cc"], s["nvis"], tq, f"sba_bwd{l}",
                                                   Exchange([dw_up, dw_down], True))
    else:
        dqt, dk, dv = _sba_bwd(s["qkv"], ktb, dyc, dyct, s["cc"], s["nvis"], tq, f"sba_bwd{l}")
    dup, dvp, dgn, dwm, dbm = _sg_bwd(s["ab"], dyb, s["gn"], s["wm"], s["wmt"], s["bfull"], tm, f"sg_bwd{l}")
    da, dwbd, dpsc = _pool_bwd(s["ab"], dya, s["wbd"], s["psc"], tm, f"pool_bwd{l}")
    dpw = jnp.stack([dwbd[gi * POOL_GW:(gi + 1) * POOL_GW, gi * POOL_GW:(gi + 1) * POOL_GW] for gi in range(4)])
    small = _pack_small_layer([jnp.zeros((D_MODEL,), F32), dpw, dpsc[0], dgn[0], dwm, dbm[:, :SG_HEADS].T, dn2[0]])[_NORM1_ROWS:]
    ride = Exchange([dw_out] + [small] * gather_small, [True] + [False] * gather_small) if exchange else None
    res = _inproj_bwd(dx1, s["x"], s["g1"], da, dup, dvp, dqt, dk, dv, s["wi"], tm, f"inproj_bwd{l}", ride)
    if exchange:
        (dx, h1, dproj, dn1), (dw_out, small) = res[0], (res[1] + [small])[:2]
    else:
        dx, h1, dproj, dn1 = res
    dw_in = _matmul_tn(h1, dproj, D_MODEL, IN_COLS // 3, tw, f"dw_in{l}")
    dw_in = dw_in.reshape(D_MODEL, N_DEV, IN_SHARD).transpose(1, 0, 2)
    return dx, (dw_in, dw_out, dw_up, dw_down), (dn1.reshape(_NORM1_ROWS, 128), small), rode


def kernel(x, norm1, w_in, pool_w, pool_scale, sg_norm, sg_w, sg_b, w_out, norm2, w_up, w_down, final_norm, loss_target, m_norm1, m_w_in, m_pool_w, m_pool_scale, m_sg_norm, m_sg_w, m_sg_b, m_w_out, m_norm2, m_w_up, m_w_down, m_final_norm, v_norm1, v_w_in, v_pool_w, v_pool_scale, v_sg_norm, v_sg_w, v_sg_b, v_w_out, v_norm2, v_w_up, v_w_down, v_final_norm):
    t = x.shape[1]
    tm = _tiles(t)[0]
    small_w = (norm1, pool_w, pool_scale, sg_norm, sg_w, sg_b, norm2)
    big_w = (w_in, w_out, w_up, w_down)
    big_m = (m_w_in, m_w_out, m_w_up, m_w_down)
    big_v = (v_w_in, v_w_out, v_w_up, v_w_down)
    shards = [[w[l].astype(BF16) for w in big_w] for l in range(DEPTH)]

    wi0 = _full_w_in(_gather_via_sibling(shards[0][0], "gather_w_in0"))
    sh_in1, sh_out1, sh_up1, sh_down1 = shards[1]
    x1, s0, (wi1, wo1, wd1) = _layer_fwd(x.reshape(t, D_MODEL), wi0, *shards[0][1:], tuple(w[0] for w in small_w), 0,
                                         ride_mlp=Exchange([sh_in1, sh_out1, sh_down1], False))
    x2, s1, _ = _layer_fwd(x1, _full_w_in(wi1), wo1, sh_up1, wd1, tuple(w[1] for w in small_w), 1)
    loss_local, dx, dfinal = _loss_grad(x2, final_norm[None, :], loss_target.reshape(t, D_MODEL), tm, "loss_grad")
    loss = lax.psum(loss_local[0, 0], MESH_AXES)

    dx, parts1, small1, _ = _layer_bwd(dx, s1, 1)
    early = jnp.concatenate(list(small1) + [dfinal.reshape(-1, 128)])
    dx, parts0, (dn1, small0), (recv_in1, early) = _layer_bwd(
        dx, s0, 0, ride_mlp=Exchange([parts1[0], early], [True, False]), gather_small=True)
    grad_x = dx.reshape(x.shape)
    recv_in0, dn1 = _exchange([parts0[0], dn1], [True, False], "scatter_w_in0_gather_norm1_0")
    small_all = jnp.concatenate([dn1, small0, early], axis=1)
    received = [[recv_in0] + list(parts0[1:]), [recv_in1] + list(parts1[1:])]

    big = [None] * 4
    for l in reversed(range(DEPTH)):
        for k in range(4):
            big[k] = _reduce_adamw(received[l][k], big_w[k], big_m[k], big_v[k], l, big[k], f"adamw{k}_{l}")

    sm = _reduce_adamw(
        small_all,
        _pack_small(small_w, final_norm)[None],
        _pack_small([m_norm1, m_pool_w, m_pool_scale, m_sg_norm, m_sg_w, m_sg_b, m_norm2], m_final_norm)[None],
        _pack_small([v_norm1, v_pool_w, v_pool_scale, v_sg_norm, v_sg_w, v_sg_b, v_norm2], v_final_norm)[None],
        0, None, "adamw_replicated")

    out = [loss, grad_x]
    for k in range(4):
        n1, pw, ps, sn, sw, sb, n2, fn = _unpack_small(sm[k][0])
        out += [n1, big[0][k], pw, ps, sn, sw, sb, big[1][k], n2, big[2][k], big[3][k], fn]
    return tuple(out)
```

```python
import functools

import jax
import jax.numpy as jnp
from jax import lax
from jax.experimental import pallas as pl
from jax.experimental.pallas import tpu as pltpu

F32 = jnp.float32
BF16 = jnp.bfloat16

D_MODEL = 1024
DEPTH = 2
POOL_WIDTH = 256
SG_WIDTH = 256
SB_WIDTH = 512
POOL_WINDOWS = (2, 4, 8, 16)
POOL_GW = 64
POOL_HALO = 16
CHUNK = 128
SG_HEADS = 4
SG_HD = 64
SB_HD = 64
SB_PAIRS = SB_WIDTH // 128
AB_COLS = POOL_WIDTH + 2 * SG_WIDTH
IN_COLS = AB_COLS + 3 * SB_WIDTH
D_FF = 4096
EPS = 1e-6
N_DEV = 8
FF_SHARD = D_FF // N_DEV
IN_SHARD = IN_COLS // N_DEV
OUT_SHARD = D_MODEL // N_DEV
ADAM_LR = 0.001
ADAM_B1 = 0.9
ADAM_B2 = 0.999
ADAM_EPS = 1e-08
ADAM_WD = 0.01
ADAM_STEP = 10
VMEM_LIMIT = 56 * 1024 * 1024
MLP_ROWS = 1024
MESH_AXES = ("x", "y", "c")


def _dot(a, b):
    return jnp.dot(a, b, preferred_element_type=F32)


def _dot_nt(a, b):
    return lax.dot_general(a, b, (((1,), (1,)), ((), ())), preferred_element_type=F32)


def _dot_tn(a, b):
    return lax.dot_general(a, b, (((0,), (0,)), ((), ())), preferred_element_type=F32)


def _rstd(x):
    return lax.rsqrt(jnp.mean(x * x, axis=-1, keepdims=True) + EPS)


def _rms_bwd(x, r, g, dh):
    gq = dh * g
    dx = r * gq - x * (r * r * r) * jnp.mean(gq * x, axis=-1, keepdims=True)
    dg = jnp.sum(dh * x * r, axis=0, keepdims=True)
    return dx, dg


def _params(sem=None):
    kw = dict(vmem_limit_bytes=VMEM_LIMIT)
    if sem is not None:
        kw["dimension_semantics"] = sem
    return pltpu.CompilerParams(**kw)


def _row_tile(rows, cap):
    best = 8
    for t in range(8, min(rows, cap) + 1, 8):
        if rows % t == 0:
            best = t
    return best


def _peer(k):
    x, y, c = lax.axis_index("x"), lax.axis_index("y"), lax.axis_index("c")
    return (1 - x if k & 4 else x, 1 - y if k & 2 else y, 1 - c if k & 1 else c)


def _my_index():
    return 4 * lax.axis_index("x") + 2 * lax.axis_index("y") + lax.axis_index("c")


class Exchange:
    def __init__(self, arrs, scatter):
        self.arrs = list(arrs)
        self.scatter = list(scatter) if isinstance(scatter, (list, tuple)) else [scatter] * len(self.arrs)
        self.n = len(self.arrs)
        self.any_specs = [pl.BlockSpec(memory_space=pl.ANY)] * self.n
        self.out_shape = [jax.ShapeDtypeStruct((N_DEV,) + a.shape[-2:], a.dtype) for a in self.arrs]
        self.sems = [pltpu.SemaphoreType.DMA((self.n, N_DEV - 1)), pltpu.SemaphoreType.DMA((self.n, N_DEV - 1)),
                     pltpu.SemaphoreType.DMA((self.n,))]

    def _copies(self, ins, outs, sems):
        send_sems, recv_sems, local_sems = sems
        me = _my_index()
        local, remote = [], []
        for a in range(self.n):
            sc = self.scatter[a]
            local.append(pltpu.make_async_copy(ins[a].at[me] if sc else ins[a], outs[a].at[me], local_sems.at[a]))
            for k in range(1, N_DEV):
                px, py, pc = _peer(k)
                remote.append(pltpu.make_async_remote_copy(
                    src_ref=ins[a].at[4 * px + 2 * py + pc] if sc else ins[a], dst_ref=outs[a].at[me],
                    send_sem=send_sems.at[a, k - 1], recv_sem=recv_sems.at[a, k - 1],
                    device_id=(px, py, pc), device_id_type=pl.DeviceIdType.MESH))
        return local, remote

    def start(self, ins, outs, sems):
        local, remote = self._copies(ins, outs, sems)
        for cp in local + remote:
            cp.start()

    def middle(self, ins, outs, sems):
        pass

    def wait(self, ins, outs, sems):
        local, remote = self._copies(ins, outs, sems)
        for cp in remote:
            cp.wait_recv()
        for cp in remote:
            cp.wait_send()
        for cp in local:
            cp.wait()

    def alone(self, name):
        n = self.n

        def body(*refs):
            ins, outs, sems = refs[:n], refs[n:2 * n], refs[2 * n:]
            self.start(ins, outs, sems)
            self.middle(ins, outs, sems)
            self.wait(ins, outs, sems)

        return pl.pallas_call(body, name=name, out_shape=self.out_shape, in_specs=self.any_specs,
                              out_specs=self.any_specs, scratch_shapes=self.sems)(*self.arrs)


class SiblingGather(Exchange):
    def __init__(self, arrs):
        super().__init__(arrs, False)

    def _plan(self, ins, outs, sems):
        send_sems, recv_sems, local_sems = sems
        x, y, core = lax.axis_index("x"), lax.axis_index("y"), lax.axis_index("c")
        me, sibling = (x, y, core), (x, y, 1 - core)
        chips = [(1 - x, y), (x, 1 - y), (1 - x, 1 - y)]
        plan = dict(local=[], first=[], arrived=[], passed=[], late=[])
        for a in range(self.n):
            slot = lambda p, a=a: outs[a].at[4 * p[0] + 2 * p[1] + p[2]]

            def copy(k, block, to, src=None, a=a, slot=slot):
                return pltpu.make_async_remote_copy(
                    src_ref=slot(block) if src is None else src, dst_ref=slot(block), send_sem=send_sems.at[a, k],
                    recv_sem=recv_sems.at[a, k], device_id=to, device_id_type=pl.DeviceIdType.MESH)

            plan["local"].append(pltpu.make_async_copy(ins[a], slot(me), local_sems.at[a]))
            plan["first"] += [copy(0, me, sibling, src=ins[a])] + [copy(1 + j, me, (*ch, core), src=ins[a])
                                                                  for j, ch in enumerate(chips)]
            plan["arrived"] += [copy(1 + j, (*ch, core), me) for j, ch in enumerate(chips)]
            plan["passed"] += [copy(4 + j, (*ch, core), sibling) for j, ch in enumerate(chips)]
            plan["late"] += [copy(0, sibling, me)] + [copy(4 + j, (*ch, 1 - core), me) for j, ch in enumerate(chips)]
        return plan

    def start(self, ins, outs, sems):
        plan = self._plan(ins, outs, sems)
        for cp in plan["local"] + plan["first"]:
            cp.start()

    def middle(self, ins, outs, sems):
        plan = self._plan(ins, outs, sems)
        for arrived, passed in zip(plan["arrived"], plan["passed"]):
            arrived.wait_recv()
            passed.start()

    def wait(self, ins, outs, sems):
        plan = self._plan(ins, outs, sems)
        for cp in plan["late"]:
            cp.wait_recv()
        for cp in plan["first"] + plan["passed"]:
            cp.wait_send()
        for cp in plan["local"]:
            cp.wait()


def _exchange(arrs, scatter, name):
    return Exchange(arrs, scatter).alone(name)


def _call(body, name, grid, in_specs, out_specs, out_shape, scratch_shapes, semantics, args, ride=None):
    if ride is None:
        return pl.pallas_call(body, name=name, grid=grid, in_specs=in_specs, out_specs=out_specs, out_shape=out_shape,
                              scratch_shapes=scratch_shapes, compiler_params=_params(semantics))(*args)
    single = not isinstance(out_shape, (list, tuple))
    out_specs, out_shape = ([out_specs], [out_shape]) if single else (list(out_specs), list(out_shape))
    n_in, n_out, n_scr, n = len(in_specs), len(out_specs), len(scratch_shapes), ride.n

    def riding(*refs):
        ins, cins = refs[:n_in], refs[n_in:n_in + n]
        outs, couts = refs[n_in + n:n_in + n + n_out], refs[n_in + n + n_out:n_in + 2 * n + n_out]
        scr, sems = refs[n_in + 2 * n + n_out:n_in + 2 * n + n_out + n_scr], refs[n_in + 2 * n + n_out + n_scr:]
        step = functools.reduce(lambda s, a: s * grid[a] + pl.program_id(a), range(len(grid)), 0)
        steps = functools.reduce(lambda p, q: p * q, grid)

        @pl.when(step == 0)
        def _():
            ride.start(cins, couts, sems)

        @pl.when(step == (3 * steps) // 4)
        def _():
            ride.middle(cins, couts, sems)

        body(*ins, *outs, *scr)

        @pl.when(step == steps - 1)
        def _():
            ride.wait(cins, couts, sems)

    res = pl.pallas_call(
        riding, name=name, grid=grid, in_specs=list(in_specs) + ride.any_specs, out_specs=out_specs + ride.any_specs,
        out_shape=out_shape + ride.out_shape, scratch_shapes=list(scratch_shapes) + ride.sems,
        compiler_params=_params(("arbitrary",) * len(grid)))(*args, *ride.arrs)
    own = res[0] if single else list(res[:n_out])
    return own, list(res[n_out:])


def _reduce_adamw(parts, w, m, v, l, prev, name):
    _, rows, n = parts.shape
    tr = _row_tile(rows, max(8, (1 << 18) // n))
    c1 = 1.0 - ADAM_B1 ** ADAM_STEP
    c2 = 1.0 - ADAM_B2 ** ADAM_STEP

    def body(p_ref, w_ref, m_ref, v_ref, *rest):
        g_ref, d_ref, nm_ref, nv_ref = rest[-4:]
        g = p_ref[0].astype(F32)
        for s in range(1, N_DEV):
            g = g + p_ref[s].astype(F32)
        nm = ADAM_B1 * m_ref[...] + (1.0 - ADAM_B1) * g
        nv = ADAM_B2 * v_ref[...] + (1.0 - ADAM_B2) * (g * g)
        m_hat = nm / c1
        v_hat = nv / c2
        g_ref[...] = g
        d_ref[...] = -ADAM_LR * (m_hat / (jnp.sqrt(v_hat) + ADAM_EPS) + ADAM_WD * w_ref[...])
        nm_ref[...] = nm
        nv_ref[...] = nv

    blk = pl.BlockSpec((None, tr, n), lambda i: (l, i, 0))
    out = jax.ShapeDtypeStruct(w.shape, F32)
    prev = list(prev) if prev is not None else []
    return pl.pallas_call(
        body, name=name, grid=(rows // tr,),
        in_specs=[pl.BlockSpec((N_DEV, tr, n), lambda i: (0, i, 0)), blk, blk, blk] + [pl.BlockSpec(memory_space=pl.ANY)] * len(prev),
        out_specs=[blk, blk, blk, blk], out_shape=[out, out, out, out],
        input_output_aliases={4 + k: k for k in range(len(prev))},
        compiler_params=_params(("parallel",)),
    )(parts, w, m, v, *prev)


def _inproj_fwd(x, g, w, tm, tq, name, ride=None):
    t, d = x.shape
    n = w.shape[1]
    nb = t // tq
    per = tm // tq

    def body(x_ref, g_ref, w_ref, ab_ref, qkv_ref, kt_ref, vt_ref):
        xx = x_ref[...]
        h = (xx * _rstd(xx) * g_ref[...]).astype(BF16)
        ab_ref[...] = _dot(h, w_ref[:, :AB_COLS])
        qkv = _dot(h, w_ref[:, AB_COLS:])
        qkv_ref[...] = qkv.astype(BF16)
        for which, out_ref in ((1, kt_ref), (2, vt_ref)):
            for p in range(SB_PAIRS):
                for b in range(per):
                    cols = which * SB_WIDTH + p * 128
                    out_ref[p, b] = qkv[b * tq:(b + 1) * tq, cols:cols + 128].T.astype(BF16)

    tb = pl.BlockSpec((SB_PAIRS, per, 128, tq), lambda i: (0, i, 0, 0))
    tshape = jax.ShapeDtypeStruct((SB_PAIRS, nb, 128, tq), BF16)
    return _call(
        body, name, (t // tm,),
        [pl.BlockSpec((tm, d), lambda i: (i, 0)), pl.BlockSpec((1, d), lambda i: (0, 0)),
         pl.BlockSpec((d, n), lambda i: (0, 0))],
        [pl.BlockSpec((tm, AB_COLS), lambda i: (i, 0)), pl.BlockSpec((tm, n - AB_COLS), lambda i: (i, 0)), tb, tb],
        [jax.ShapeDtypeStruct((t, AB_COLS), F32), jax.ShapeDtypeStruct((t, n - AB_COLS), BF16), tshape, tshape],
        [], ("parallel",), (x, g, w), ride)


def _pool_window_sums(xx, forward):
    n = xx.shape[0]
    sh = (lambda k: n - k) if forward else (lambda k: k)
    s2 = xx + pltpu.roll(xx, sh(1), 0)
    s4 = s2 + pltpu.roll(s2, sh(2), 0)
    s8 = s4 + pltpu.roll(s4, sh(4), 0)
    s16 = s8 + pltpu.roll(s8, sh(8), 0)
    grp = lax.broadcasted_iota(jnp.int32, (1, POOL_WIDTH), 1) // POOL_GW
    return jnp.where(grp == 0, s2, jnp.where(grp == 1, s4, jnp.where(grp == 2, s8, s16)))


def _pool_count(t0, rows):
    grp = lax.broadcasted_iota(jnp.int32, (1, POOL_WIDTH), 1) // POOL_GW
    win = jnp.where(grp == 0, 2, jnp.where(grp == 1, 4, jnp.where(grp == 2, 8, 16)))
    tt = t0 + lax.broadcasted_iota(jnp.int32, (rows, 1), 0)
    return jnp.minimum(tt + 1, win).astype(F32)


def _pool_diff(cur, prev, t0):
    tm = cur.shape[0]
    sums = _pool_window_sums(jnp.concatenate([prev, cur], axis=0), False)[POOL_HALO:]
    return sums / _pool_count(t0, tm) - cur


def _pool_fwd(ab, wbd, scale, tm, name):
    t = ab.shape[0]
    hb = tm // POOL_HALO

    def body(cur_ref, prev_ref, w_ref, s_ref, y_ref):
        i = pl.program_id(0)
        prev = jnp.where(i == 0, 0.0, prev_ref[...])
        d = _pool_diff(cur_ref[...], prev, i * tm).astype(BF16)
        y_ref[...] = (_dot(d, w_ref[...]) * s_ref[...]).astype(BF16)

    return pl.pallas_call(
        body, name=name, grid=(t // tm,),
        in_specs=[pl.BlockSpec((tm, POOL_WIDTH), lambda i: (i, 0)),
                  pl.BlockSpec((POOL_HALO, POOL_WIDTH), lambda i: (jnp.maximum(i * hb - 1, 0), 0)),
                  pl.BlockSpec((POOL_WIDTH, POOL_WIDTH), lambda i: (0, 0)),
                  pl.BlockSpec((1, POOL_WIDTH), lambda i: (0, 0))],
        out_specs=pl.BlockSpec((tm, POOL_WIDTH), lambda i: (i, 0)),
        out_shape=jax.ShapeDtypeStruct((t, POOL_WIDTH), BF16),
        compiler_params=_params(("parallel",)),
    )(ab, ab, wbd, scale)


_GELU_K = 0.7978845608028654
_GELU_A = 0.044715


def _gelu(x):
    return 0.5 * x * (1.0 + jnp.tanh(_GELU_K * (x + _GELU_A * (x * x * x))))


def _gelu_grad(x):
    th = jnp.tanh(_GELU_K * (x + _GELU_A * (x * x * x)))
    return 0.5 * (1.0 + th) + 0.5 * x * (1.0 - th * th) * (_GELU_K * (1.0 + 3.0 * _GELU_A * (x * x)))


def _head_lanes(h):
    return lax.broadcasted_iota(jnp.int32, (1, SG_WIDTH), 1) // SG_HD == h


def _sg_fwd(ab, gn, wm, bfull, tm, name):
    t = ab.shape[0]

    def body(u_ref, v_ref, gn_ref, wm_ref, b_ref, y_ref):
        v = _gelu(v_ref[...])
        vn = (v * _rstd(v) * gn_ref[...]).astype(BF16)
        for c in range(tm // CHUNK):
            rows = slice(c * CHUNK, (c + 1) * CHUNK)
            vc = vn[rows]
            sv = b_ref[...]
            for h in range(SG_HEADS):
                sv = sv + jnp.where(_head_lanes(h), _dot(wm_ref[h], vc), 0.0)
            y_ref[rows, :] = (_gelu(u_ref[rows, :]) * sv).astype(BF16)

    return pl.pallas_call(
        body, name=name, grid=(t // tm,),
        in_specs=[pl.BlockSpec((tm, SG_WIDTH), lambda i: (i, 1)), pl.BlockSpec((tm, SG_WIDTH), lambda i: (i, 2)),
                  pl.BlockSpec((1, SG_WIDTH), lambda i: (0, 0)),
                  pl.BlockSpec((SG_HEADS, CHUNK, CHUNK), lambda i: (0, 0, 0)),
                  pl.BlockSpec((CHUNK, SG_WIDTH), lambda i: (0, 0))],
        out_specs=pl.BlockSpec((tm, SG_WIDTH), lambda i: (i, 0)),
        out_shape=jax.ShapeDtypeStruct((t, SG_WIDTH), BF16),
        compiler_params=_params(("parallel",)),
    )(ab, ab, gn, wm, bfull)


LOG2E = 1.4426950408889634
SB_SCALE = 0.125 * LOG2E
SB_DEAD_LOG2 = 152.0
SB_QUERY_BLOCKS_PER_STEP = 2


def _log2_sigmoids(y):
    neg_abs = lax.bitcast_convert_type(lax.bitcast_convert_type(y, jnp.uint32) | jnp.uint32(0x80000000), F32)
    lb = jnp.minimum(y, 0.0) - jnp.log(1.0 + jnp.exp2(neg_abs)) * LOG2E
    return lb, lb - y


def _split(x):
    hi = x.astype(BF16)
    return hi, (x - hi.astype(F32)).astype(BF16)


def _tri_dot(tri, x):
    hi, lo = _split(x)
    return _dot(tri, hi) + _dot(tri, lo)


def _sba_fwd(qkv, vtb, tq, name, ride=None):
    t = qkv.shape[0]
    nb = t // tq
    qb = SB_QUERY_BLOCKS_PER_STEP
    upper = (jnp.arange(tq)[None, :] > jnp.arange(tq)[:, None]).astype(BF16)

    def body(q_ref, k_ref, vt_ref, up_ref, ot_ref, c_ref, n_ref):
        pair, step = pl.program_id(0), pl.program_id(1)
        refs = (q_ref, k_ref, vt_ref, up_ref, ot_ref, c_ref, n_ref)
        subs = list(range(qb))

        @pl.when(step == 0)
        def _():
            for sub in subs:
                query_blocks([sub], pair, step, *refs)

        @pl.when(step > 0)
        def _():
            query_blocks(subs, pair, step, *refs)

    def query_blocks(subs, pair, step, q_ref, k_ref, vt_ref, up_ref, ot_ref, c_ref, n_ref):
        index = {sub: step * qb + sub for sub in subs}
        mine = {sub: slice(sub * tq, (sub + 1) * tq) for sub in subs}
        up = up_ref[...]
        lane_head = lax.broadcasted_iota(jnp.int32, (1, 128), 1) // SB_HD
        sub_head = lax.broadcasted_iota(jnp.int32, (128, 1), 0) // SB_HD
        causal = (lax.broadcasted_iota(jnp.int32, (tq, tq), 0) < lax.broadcasted_iota(jnp.int32, (tq, tq), 1))
        qh = {}
        for sub in subs:
            q = q_ref[mine[sub], :]
            qh[sub] = [jnp.where(lane_head == h, q, jnp.zeros_like(q)) for h in range(2)]

        def blocks(work, carry):
            cs = {sub: list(carry[sub][:2]) for sub in carry}
            acc = {sub: carry[sub][2] for sub in carry}
            kj = [k_ref[pl.ds(pl.multiple_of(j * tq, tq), tq), :] for _, j, _ in work]
            vt = [vt_ref[j] for _, j, _ in work]
            chains = [(w, h) for w in range(len(work)) for h in range(2)]
            z = [_dot_nt(kj[w], qh[work[w][0]][h]) for w, h in chains]
            ls = [_log2_sigmoids(zz * SB_SCALE) for zz in z]
            lb = [x[0] for x in ls]
            l1 = [jnp.where(causal, x[1], 0.0) if work[w][2] else x[1] for x, (w, h) in zip(ls, chains)]
            after = [_tri_dot(up, x) for x in l1]
            a = []
            for n, (w, h) in enumerate(chains):
                sub, j, diag = work[w]
                c_ref[h, pl.ds(j, 1), mine[sub]] = cs[sub][h]
                an = jnp.exp2(lb[n] + after[n] + cs[sub][h])
                a.append(jnp.where(causal, an, 0.0) if diag else an)
                cs[sub][h] = cs[sub][h] + after[n][0:1, :] + l1[n][0:1, :]
            alive = {sub: (jnp.max(jnp.maximum(*cs[sub])) > -SB_DEAD_LOG2).astype(jnp.int32) for sub in cs}
            for n, (w, h) in enumerate(chains):
                sub = work[w][0]
                acc[sub] = acc[sub] + _dot(jnp.where(sub_head == h, vt[w], jnp.zeros_like(vt[w])), a[n].astype(BF16))
            return {sub: (alive[sub], cs[sub][0], cs[sub][1], acc[sub]) for sub in cs}

        zero = jnp.zeros((1, tq), F32)
        start = {sub: (zero, zero, jnp.zeros((128, tq), F32)) for sub in subs}
        both = lambda: blocks([(sub, index[sub] - b, b == 0) for sub in subs for b in range(2)], start)
        if len(subs) == 1:
            joint = lax.cond(index[subs[0]] > 0, both, lambda: blocks([(subs[0], index[subs[0]], True)], start))
        else:
            joint = both()
        for sub in subs:
            i = index[sub]

            def left(state, sub=sub, i=i):
                s, _, c0, c1, acc = state
                return (s + 1,) + blocks([(sub, i - 1 - s, False)], {sub: (c0, c1, acc)})[sub]

            state = lax.while_loop(lambda st, i=i: (st[0] < i) & (st[1] > 0), left, (jnp.minimum(i, 1),) + tuple(joint[sub]))
            ot_ref[:, mine[sub]] = state[4].astype(BF16)
            n_ref[pair, i] = (state[0] + 1).astype(F32)

    return _call(
        body, name, (SB_PAIRS, nb // qb),
        [pl.BlockSpec((qb * tq, 128), lambda p, i: (i, p)),
         pl.BlockSpec((t, 128), lambda p, i: (0, SB_PAIRS + p)),
         pl.BlockSpec((None, nb, 128, tq), lambda p, i: (p, 0, 0, 0)),
         pl.BlockSpec((tq, tq), lambda p, i: (0, 0))],
        [pl.BlockSpec((128, qb * tq), lambda p, i: (p, i)),
         pl.BlockSpec((2, nb, qb * tq), lambda p, i: (p, 0, i)),
         pl.BlockSpec(memory_space=pltpu.SMEM)],
        [jax.ShapeDtypeStruct((SB_WIDTH, t), BF16),
         jax.ShapeDtypeStruct((2 * SB_PAIRS, nb, t), F32),
         jax.ShapeDtypeStruct((SB_PAIRS, nb), F32)],
        [], ("arbitrary", "arbitrary"), (qkv, qkv, vtb, upper), ride)


def _outproj_mlp_fwd(x, ya, yb, yct, wo, g2, wup, wdn, tm, name, ride=None):
    t, d = x.shape
    nf = wup.shape[0]

    def body(x_ref, ya_ref, yb_ref, yct_ref, wo_ref, g_ref, wu_ref, wd_ref, x1_ref, u_ref, x2_ref, h_ref, acc_ref):
        j = pl.program_id(1)

        @pl.when(j == 0)
        def _():
            x1 = (x_ref[...] + _dot(ya_ref[...], wo_ref[0:POOL_WIDTH, :])
                  + _dot(yb_ref[...], wo_ref[POOL_WIDTH:POOL_WIDTH + SG_WIDTH, :])
                  + _dot_tn(yct_ref[...], wo_ref[POOL_WIDTH + SG_WIDTH:, :]))
            x1_ref[...] = x1
            h_ref[...] = (x1 * _rstd(x1) * g_ref[...]).astype(BF16)
            acc_ref[...] = x1

        u = _dot(h_ref[...], wu_ref[...])
        u_ref[...] = u
        r = jnp.maximum(u, 0.0)
        acc_ref[...] += _dot((r * r).astype(BF16), wd_ref[...])

        @pl.when(j == nf - 1)
        def _():
            x2_ref[...] = acc_ref[...]

    row = lambda w: pl.BlockSpec((tm, w), lambda i, j: (i, 0))
    return _call(
        body, name, (t // tm, nf),
        [row(d), row(POOL_WIDTH), row(SG_WIDTH), pl.BlockSpec((SB_WIDTH, tm), lambda i, j: (0, i)),
         pl.BlockSpec((d, d), lambda i, j: (0, 0)), pl.BlockSpec((1, d), lambda i, j: (0, 0)),
         pl.BlockSpec((None, d, FF_SHARD), lambda i, j: (j, 0, 0)),
         pl.BlockSpec((None, FF_SHARD, d), lambda i, j: (j, 0, 0))],
        [row(d), pl.BlockSpec((tm, FF_SHARD), lambda i, j: (i, j)), row(d)],
        [jax.ShapeDtypeStruct((t, d), F32), jax.ShapeDtypeStruct((t, nf * FF_SHARD), F32),
         jax.ShapeDtypeStruct((t, d), F32)],
        [pltpu.VMEM((tm, d), BF16), pltpu.VMEM((tm, d), F32)],
        ("parallel", "arbitrary"), (x, ya, yb, yct, wo, g2, wup, wdn), ride)


def _loss_grad(x, g, target, tm, name):
    t, d = x.shape
    nt = t // tm

    def body(x_ref, g_ref, t_ref, loss_ref, dx_ref, dg_ref, sq_ref):
        i = pl.program_id(0)

        @pl.when(i == 0)
        def _():
            sq_ref[...] = jnp.zeros_like(sq_ref)
            dg_ref[...] = jnp.zeros_like(dg_ref)

        xx = x_ref[...]
        r = _rstd(xx)
        err = xx * r * g_ref[...] - t_ref[...]
        sq_ref[...] += jnp.sum(err * err, axis=0, keepdims=True)
        dx, dg = _rms_bwd(xx, r, g_ref[...], err * (1.0 / d))
        dx_ref[...] = dx
        dg_ref[...] += dg

        @pl.when(i == nt - 1)
        def _():
            loss_ref[...] = jnp.sum(sq_ref[...], axis=1, keepdims=True) * (0.5 / d)

    return pl.pallas_call(
        body, name=name, grid=(nt,),
        in_specs=[pl.BlockSpec((tm, d), lambda i: (i, 0)), pl.BlockSpec((1, d), lambda i: (0, 0)),
                  pl.BlockSpec((tm, d), lambda i: (i, 0))],
        out_specs=[pl.BlockSpec((1, 1), lambda i: (0, 0)), pl.BlockSpec((tm, d), lambda i: (i, 0)),
                   pl.BlockSpec((1, d), lambda i: (0, 0))],
        out_shape=[jax.ShapeDtypeStruct((1, 1), F32), jax.ShapeDtypeStruct((t, d), F32),
                   jax.ShapeDtypeStruct((1, d), F32)],
        scratch_shapes=[pltpu.VMEM((1, d), F32)],
        compiler_params=_params(("arbitrary",)),
    )(x, g, target)


def _mlp_bwd(dx2, x1, g2, u, wup, wdn, tm, name, ride=None):
    t, d = dx2.shape
    nf = wup.shape[0]
    nt = t // tm

    def body(dx2_ref, x1_ref, g_ref, u_ref, wu_ref, wd_ref, dx1_ref, du_ref, r_ref, h_ref, dxb_ref, dg_ref, acc_ref):
        i, j = pl.program_id(0), pl.program_id(1)

        @pl.when(j == 0)
        def _():
            x1 = x1_ref[...]
            h_ref[...] = (x1 * _rstd(x1) * g_ref[...]).astype(BF16)
            dxb_ref[...] = dx2_ref[...].astype(BF16)
            acc_ref[...] = jnp.zeros_like(acc_ref)

        @pl.when((i == 0) & (j == 0))
        def _():
            dg_ref[...] = jnp.zeros_like(dg_ref)

        dr = _dot_nt(dxb_ref[...], wd_ref[...])
        ru = jnp.maximum(u_ref[...], 0.0)
        du = (dr * (2.0 * ru)).astype(BF16)
        du_ref[...] = du
        r_ref[...] = (ru * ru).astype(BF16)
        acc_ref[...] += _dot_nt(du, wu_ref[...])

        @pl.when(j == nf - 1)
        def _():
            x1 = x1_ref[...]
            dx, dg = _rms_bwd(x1, _rstd(x1), g_ref[...], acc_ref[...])
            dx1_ref[...] = dx2_ref[...] + dx
            dg_ref[...] += dg

    row = lambda w: pl.BlockSpec((tm, w), lambda i, j: (i, 0))
    col = pl.BlockSpec((tm, FF_SHARD), lambda i, j: (i, j))
    return _call(
        body, name, (nt, nf),
        [row(d), row(d), pl.BlockSpec((1, d), lambda i, j: (0, 0)), col,
         pl.BlockSpec((None, d, FF_SHARD), lambda i, j: (j, 0, 0)),
         pl.BlockSpec((None, FF_SHARD, d), lambda i, j: (j, 0, 0))],
        [row(d), col, col, row(d), row(d), pl.BlockSpec((1, d), lambda i, j: (0, 0))],
        [jax.ShapeDtypeStruct((t, d), F32), jax.ShapeDtypeStruct((t, nf * FF_SHARD), BF16),
         jax.ShapeDtypeStruct((t, nf * FF_SHARD), BF16), jax.ShapeDtypeStruct((t, d), BF16),
         jax.ShapeDtypeStruct((t, d), BF16), jax.ShapeDtypeStruct((1, d), F32)],
        [pltpu.VMEM((tm, d), F32)], ("arbitrary", "arbitrary"), (dx2, x1, g2, u, wup, wdn), ride)


def _matmul_tn(a, b, bm, bn, bt, name, by_column_block=0, a_transposed=False):
    m, t = a.shape if a_transposed else a.shape[::-1]
    n = b.shape[1]
    nk = t // bt

    def body(a_ref, b_ref, o_ref, acc_ref):
        k = pl.program_id(2)

        @pl.when(k == 0)
        def _():
            acc_ref[...] = jnp.zeros_like(acc_ref)

        acc_ref[...] += _dot(a_ref[...], b_ref[...]) if a_transposed else _dot_tn(a_ref[...], b_ref[...])

        @pl.when(k == nk - 1)
        def _():
            if by_column_block:
                for c in range(bn // by_column_block):
                    o_ref[c] = acc_ref[:, c * by_column_block:(c + 1) * by_column_block].astype(BF16)
            else:
                o_ref[...] = acc_ref[...].astype(BF16)

    if by_column_block:
        out_spec = pl.BlockSpec((bn // by_column_block, bm, by_column_block), lambda i, j, k: (j, i, 0))
        out_shape = jax.ShapeDtypeStruct((n // by_column_block, m, by_column_block), BF16)
    else:
        out_spec = pl.BlockSpec((bm, bn), lambda i, j, k: (i, j))
        out_shape = jax.ShapeDtypeStruct((m, n), BF16)
    a_spec = (pl.BlockSpec((bm, bt), lambda i, j, k: (i, k)) if a_transposed
              else pl.BlockSpec((bt, bm), lambda i, j, k: (k, i)))
    return _call(body, name, (m // bm, n // bn, nk), [a_spec, pl.BlockSpec((bt, bn), lambda i, j, k: (k, j))],
                 out_spec, out_shape, [pltpu.VMEM((bm, bn), F32)], ("parallel", "parallel", "arbitrary"), (a, b))


def _outproj_bwd(dx1, wo, tm, name):
    t, d = dx1.shape
    c2 = POOL_WIDTH + SG_WIDTH

    def body(dx_ref, wo_ref, dya_ref, dyb_ref, dyc_ref, dyct_ref, dxb_ref):
        dxb = dx_ref[...].astype(BF16)
        dxb_ref[...] = dxb
        dya_ref[...] = _dot_nt(dxb, wo_ref[0:POOL_WIDTH, :])
        dyb_ref[...] = _dot_nt(dxb, wo_ref[POOL_WIDTH:c2, :])
        dyc_ref[...] = _dot_nt(dxb, wo_ref[c2:, :]).astype(BF16)
        dyct_ref[...] = _dot_nt(wo_ref[c2:, :], dxb).astype(BF16)

    row = lambda w: pl.BlockSpec((tm, w), lambda i: (i, 0))
    return pl.pallas_call(
        body, name=name, grid=(t // tm,),
        in_specs=[row(d), pl.BlockSpec((d, d), lambda i: (0, 0))],
        out_specs=[row(POOL_WIDTH), row(SG_WIDTH), row(SB_WIDTH), pl.BlockSpec((SB_WIDTH, tm), lambda i: (0, i)), row(d)],
        out_shape=[jax.ShapeDtypeStruct((t, POOL_WIDTH), F32), jax.ShapeDtypeStruct((t, SG_WIDTH), F32),
                   jax.ShapeDtypeStruct((t, SB_WIDTH), BF16), jax.ShapeDtypeStruct((SB_WIDTH, t), BF16),
                   jax.ShapeDtypeStruct((t, d), BF16)],
        compiler_params=_params(("parallel",)),
    )(dx1, wo)


def _sba_bwd(qkv, ktb, dyc, dyct, cc, nvis, tq, name, ride=None):
    t = qkv.shape[0]
    nb = t // tq
    idx = jnp.arange(tq)
    upper = (idx[None, :] > idx[:, None]).astype(BF16)
    lower = (idx[None, :] < idx[:, None]).astype(BF16)

    qb = SB_QUERY_BLOCKS_PER_STEP

    def body(q_ref, k_ref, v_ref, kt_ref, do_ref, dot_ref, c_ref, n_ref, up_ref, lo_ref, dqt_ref, dk_ref, dv_ref):
        pair, step = pl.program_id(0), pl.program_id(1)
        refs = (q_ref, k_ref, v_ref, kt_ref, do_ref, dot_ref, c_ref, n_ref, up_ref, lo_ref, dqt_ref, dk_ref, dv_ref)
        subs = list(range(qb))

        @pl.when(step == 0)
        def _():
            dk_ref[...] = jnp.zeros_like(dk_ref)
            dv_ref[...] = jnp.zeros_like(dv_ref)
            for sub in subs:
                query_blocks([sub], pair, step, *refs)

        @pl.when(step > 0)
        def _():
            query_blocks(subs, pair, step, *refs)

    def query_blocks(subs, pair, step, q_ref, k_ref, v_ref, kt_ref, do_ref, dot_ref, c_ref, n_ref, up_ref, lo_ref,
                     dqt_ref, dk_ref, dv_ref):
        index = {sub: step * qb + sub for sub in subs}
        mine = {sub: slice(sub * tq, (sub + 1) * tq) for sub in subs}
        up = up_ref[...]
        lo = lo_ref[...]
        lane_head = lax.broadcasted_iota(jnp.int32, (1, 128), 1) // SB_HD
        sub_head = lax.broadcasted_iota(jnp.int32, (128, 1), 0) // SB_HD
        causal = (lax.broadcasted_iota(jnp.int32, (tq, tq), 0) < lax.broadcasted_iota(jnp.int32, (tq, tq), 1))
        hms = [lane_head == h for h in range(2)]
        qh, qs, doh, dot = {}, {}, {}, {}
        for sub in subs:
            q, do = q_ref[mine[sub], :], do_ref[mine[sub], :]
            qh[sub] = [jnp.where(hm, q, jnp.zeros_like(q)) for hm in hms]
            qs[sub] = [x * 0.125 for x in qh[sub]]
            doh[sub] = [jnp.where(hm, do, jnp.zeros_like(do)) for hm in hms]
            dot[sub] = dot_ref[:, mine[sub]]

        def blocks(work, carry):
            cgs = {sub: list(carry[sub][:2]) for sub in carry}
            dqt = {sub: carry[sub][2] for sub in carry}
            rows = [pl.ds(pl.multiple_of(j * tq, tq), tq) for _, j, _ in work]
            kj = [k_ref[r, :] for r in rows]
            vj = [v_ref[r, :] for r in rows]
            kt = [kt_ref[j] for _, j, _ in work]
            chains = [(w, h) for w in range(len(work)) for h in range(2)]
            z = [_dot_nt(kj[w], qh[work[w][0]][h]) for w, h in chains]
            da = [_dot(jnp.where(hms[h], vj[w], jnp.zeros_like(vj[w])), dot[work[w][0]]) for w, h in chains]
            ls = [_log2_sigmoids(zz * SB_SCALE) for zz in z]
            lb = [x[0] for x in ls]
            l1 = [jnp.where(causal, x[1], 0.0) if work[w][2] else x[1] for x, (w, h) in zip(ls, chains)]
            after = [_tri_dot(up, x) for x in l1]
            a = [jnp.exp2(lb[n] + after[n] + c_ref[h, pl.ds(work[w][1], 1), mine[work[w][0]]])
                 for n, (w, h) in enumerate(chains)]
            a = [jnp.where(causal, a[n], 0.0) if work[w][2] else a[n] for n, (w, h) in enumerate(chains)]
            g = [a[n] * da[n] for n in range(len(chains))]
            gloc = [_dot(lo, x.astype(BF16)) for x in g]
            dzb = []
            for n, (w, h) in enumerate(chains):
                sub, _, diag = work[w]
                gsum = gloc[n] + cgs[sub][h]
                dz = g[n] - jnp.exp2(lb[n]) * (g[n] + gsum)
                dzb.append((jnp.where(causal, dz, 0.0) if diag else dz).astype(BF16))
                cgs[sub][h] = gsum[tq - 1:tq, :] + g[n][tq - 1:tq, :]
            ab = [x.astype(BF16) for x in a]
            for n, (w, h) in enumerate(chains):
                sub = work[w][0]
                dqt[sub] = dqt[sub] + _dot(jnp.where(sub_head == h, kt[w], jnp.zeros_like(kt[w])), dzb[n])
            for w, (sub, _, _) in enumerate(work):
                dk_ref[rows[w], :] += _dot(dzb[2 * w], qs[sub][0]) + _dot(dzb[2 * w + 1], qs[sub][1])
                dv_ref[rows[w], :] += _dot(ab[2 * w], doh[sub][0]) + _dot(ab[2 * w + 1], doh[sub][1])
            return {sub: (cgs[sub][0], cgs[sub][1], dqt[sub]) for sub in cgs}

        zero = jnp.zeros((1, tq), F32)
        carry = {}
        for sub in subs:
            i = index[sub]
            n = jnp.clip(n_ref[pair, i].astype(jnp.int32), 1, i + 1)
            carry[sub] = lax.fori_loop(jnp.minimum(i + 1 - n, i - 1), i - 1,
                                       lambda s, cr, sub=sub: blocks([(sub, s, False)], {sub: cr})[sub],
                                       (zero, zero, jnp.zeros((128, tq), F32)))
        both = lambda: blocks([(sub, index[sub] - 1 + b, b == 1) for sub in subs for b in range(2)], carry)
        if len(subs) == 1:
            carry = lax.cond(index[subs[0]] > 0, both, lambda: blocks([(subs[0], index[subs[0]], True)], carry))
        else:
            carry = both()
        for sub in subs:
            dqt_ref[:, mine[sub]] = carry[sub][2] * 0.125

    return _call(
        body, name, (SB_PAIRS, nb // qb),
        [pl.BlockSpec((qb * tq, 128), lambda p, i: (i, p)),
         pl.BlockSpec((t, 128), lambda p, i: (0, SB_PAIRS + p)),
         pl.BlockSpec((t, 128), lambda p, i: (0, 2 * SB_PAIRS + p)),
         pl.BlockSpec((None, nb, 128, tq), lambda p, i: (p, 0, 0, 0)),
         pl.BlockSpec((qb * tq, 128), lambda p, i: (i, p)),
         pl.BlockSpec((128, qb * tq), lambda p, i: (p, i)),
         pl.BlockSpec((2, nb, qb * tq), lambda p, i: (p, 0, i)),
         pl.BlockSpec(memory_space=pltpu.SMEM),
         pl.BlockSpec((tq, tq), lambda p, i: (0, 0)),
         pl.BlockSpec((tq, tq), lambda p, i: (0, 0))],
        [pl.BlockSpec((128, qb * tq), lambda p, i: (p, i)),
         pl.BlockSpec((t, 128), lambda p, i: (0, p)),
         pl.BlockSpec((t, 128), lambda p, i: (0, p))],
        [jax.ShapeDtypeStruct((SB_WIDTH, t), F32), jax.ShapeDtypeStruct((t, SB_WIDTH), F32),
         jax.ShapeDtypeStruct((t, SB_WIDTH), F32)],
        [], ("arbitrary", "arbitrary"), (qkv, qkv, qkv, ktb, dyc, dyct, cc, nvis, upper, lower), ride)


def _sg_bwd(ab, dyb, gn, wm, wmt, bfull, tm, name):
    t = ab.shape[0]
    nt = t // tm
    sel = (jnp.arange(SG_WIDTH)[:, None] // SG_HD == jnp.arange(CHUNK)[None, :]).astype(F32)

    def body(u_ref, v_ref, dy_ref, gn_ref, wm_ref, wmt_ref, b_ref, sel_ref,
             dup_ref, dvp_ref, dgn_ref, dw_ref, db_ref, dbacc_ref):
        i = pl.program_id(0)

        @pl.when(i == 0)
        def _():
            dgn_ref[...] = jnp.zeros_like(dgn_ref)
            dw_ref[...] = jnp.zeros_like(dw_ref)
            dbacc_ref[...] = jnp.zeros_like(dbacc_ref)

        tril = (lax.broadcasted_iota(jnp.int32, (CHUNK, CHUNK), 0) >= lax.broadcasted_iota(jnp.int32, (CHUNK, CHUNK), 1))
        gn_ = gn_ref[...]
        for c in range(tm // CHUNK):
            rows = slice(c * CHUNK, (c + 1) * CHUNK)
            up, vp, dy = u_ref[rows, :], v_ref[rows, :], dy_ref[rows, :]
            u, v = _gelu(up), _gelu(vp)
            r = _rstd(v)
            vn = (v * r * gn_).astype(BF16)
            sv = b_ref[...]
            for h in range(SG_HEADS):
                sv = sv + jnp.where(_head_lanes(h), _dot(wm_ref[h], vn), 0.0)
            dup_ref[rows, :] = dy * sv * _gelu_grad(up)
            dsv = dy * u
            dbacc_ref[...] += dsv
            dvn = jnp.zeros((CHUNK, SG_WIDTH), F32)
            for h in range(SG_HEADS):
                dsv_h = jnp.where(_head_lanes(h), dsv, 0.0).astype(BF16)
                dvn = dvn + _dot(wmt_ref[h], dsv_h)
                dw_ref[h] += jnp.where(tril, _dot_nt(dsv_h, vn), 0.0)
            dv, dgn = _rms_bwd(v, r, gn_, dvn)
            dgn_ref[...] += dgn
            dvp_ref[rows, :] = dv * _gelu_grad(vp)

        @pl.when(i == nt - 1)
        def _():
            db_ref[...] = jnp.dot(dbacc_ref[...], sel_ref[...], preferred_element_type=F32,
                                  precision=lax.Precision.HIGHEST)

    const = lambda shape: pl.BlockSpec(shape, lambda i: tuple(0 for _ in shape))
    return pl.pallas_call(
        body, name=name, grid=(nt,),
        in_specs=[pl.BlockSpec((tm, SG_WIDTH), lambda i: (i, 1)), pl.BlockSpec((tm, SG_WIDTH), lambda i: (i, 2)),
                  pl.BlockSpec((tm, SG_WIDTH), lambda i: (i, 0)), const((1, SG_WIDTH)),
                  const((SG_HEADS, CHUNK, CHUNK)), const((SG_HEADS, CHUNK, CHUNK)), const((CHUNK, SG_WIDTH)),
                  const((SG_WIDTH, CHUNK))],
        out_specs=[pl.BlockSpec((tm, SG_WIDTH), lambda i: (i, 0)), pl.BlockSpec((tm, SG_WIDTH), lambda i: (i, 0)),
                   const((1, SG_WIDTH)), const((SG_HEADS, CHUNK, CHUNK)), const((CHUNK, CHUNK))],
        out_shape=[jax.ShapeDtypeStruct((t, SG_WIDTH), F32), jax.ShapeDtypeStruct((t, SG_WIDTH), F32),
                   jax.ShapeDtypeStruct((1, SG_WIDTH), F32), jax.ShapeDtypeStruct((SG_HEADS, CHUNK, CHUNK), F32),
                   jax.ShapeDtypeStruct((CHUNK, CHUNK), F32)],
        scratch_shapes=[pltpu.VMEM((CHUNK, SG_WIDTH), F32)],
        compiler_params=_params(("arbitrary",)),
    )(ab, ab, dyb, gn, wm, wmt, bfull, sel)


def _pool_bwd(ab, dya, wbd, scale, tm, name):
    t = ab.shape[0]
    nt = t // tm
    hb = tm // POOL_HALO
    nh = t // POOL_HALO

    def body(cur_ref, prev_ref, dy_ref, dyn_ref, w_ref, s_ref, da_ref, dw_ref, ds_ref):
        i = pl.program_id(0)

        @pl.when(i == 0)
        def _():
            dw_ref[...] = jnp.zeros_like(dw_ref)
            ds_ref[...] = jnp.zeros_like(ds_ref)

        prev = jnp.where(i == 0, 0.0, prev_ref[...])
        d = _pool_diff(cur_ref[...], prev, i * tm).astype(BF16)
        dy = dy_ref[...]
        ds_ref[...] += jnp.sum(dy * _dot(d, w_ref[...]), axis=0, keepdims=True)
        dyn = jnp.where(i == nt - 1, 0.0, dyn_ref[...])
        dys = (jnp.concatenate([dy, dyn], axis=0) * s_ref[...]).astype(BF16)
        dw_ref[...] += _dot_tn(d, dys[:tm])
        dd = _dot_nt(dys, w_ref[...])
        fwd = _pool_window_sums(dd / _pool_count(i * tm, tm + POOL_HALO), True)
        da_ref[...] = fwd[:tm] - dd[:tm]

    return pl.pallas_call(
        body, name=name, grid=(nt,),
        in_specs=[pl.BlockSpec((tm, POOL_WIDTH), lambda i: (i, 0)),
                  pl.BlockSpec((POOL_HALO, POOL_WIDTH), lambda i: (jnp.maximum(i * hb - 1, 0), 0)),
                  pl.BlockSpec((tm, POOL_WIDTH), lambda i: (i, 0)),
                  pl.BlockSpec((POOL_HALO, POOL_WIDTH), lambda i: (jnp.minimum((i + 1) * hb, nh - 1), 0)),
                  pl.BlockSpec((POOL_WIDTH, POOL_WIDTH), lambda i: (0, 0)),
                  pl.BlockSpec((1, POOL_WIDTH), lambda i: (0, 0))],
        out_specs=[pl.BlockSpec((tm, POOL_WIDTH), lambda i: (i, 0)),
                   pl.BlockSpec((POOL_WIDTH, POOL_WIDTH), lambda i: (0, 0)),
                   pl.BlockSpec((1, POOL_WIDTH), lambda i: (0, 0))],
        out_shape=[jax.ShapeDtypeStruct((t, POOL_WIDTH), F32), jax.ShapeDtypeStruct((POOL_WIDTH, POOL_WIDTH), F32),
                   jax.ShapeDtypeStruct((1, POOL_WIDTH), F32)],
        compiler_params=_params(("arbitrary",)),
    )(ab, ab, dya, dya, wbd, scale)


def _inproj_bwd(dx1, x, g, da, dup, dvp, dqt, dk, dv, w, tm, name, ride=None):
    t, d = x.shape
    n = w.shape[1]
    nt = t // tm

    def body(dx1_ref, x_ref, g_ref, da_ref, du_ref, dv_ref, dqt_ref, dk_ref, dvv_ref, w_ref,
             dx_ref, h_ref, dp_ref, dg_ref):
        @pl.when(pl.program_id(0) == 0)
        def _():
            dg_ref[...] = jnp.zeros_like(dg_ref)

        dp = jnp.concatenate([da_ref[...], du_ref[...], dv_ref[...], dqt_ref[...].T, dk_ref[...], dvv_ref[...]],
                             axis=1).astype(BF16)
        dp_ref[...] = dp
        xx = x_ref[...]
        r = _rstd(xx)
        h_ref[...] = (xx * r * g_ref[...]).astype(BF16)
        dx, dg = _rms_bwd(xx, r, g_ref[...], _dot_nt(dp, w_ref[...]))
        dx_ref[...] = dx1_ref[...] + dx
        dg_ref[...] += dg

    row = lambda w_: pl.BlockSpec((tm, w_), lambda i: (i, 0))
    return _call(
        body, name, (nt,),
        [row(d), row(d), pl.BlockSpec((1, d), lambda i: (0, 0)), row(POOL_WIDTH), row(SG_WIDTH),
         row(SG_WIDTH), pl.BlockSpec((SB_WIDTH, tm), lambda i: (0, i)), row(SB_WIDTH), row(SB_WIDTH),
         pl.BlockSpec((d, n), lambda i: (0, 0))],
        [row(d), row(d), row(n), pl.BlockSpec((1, d), lambda i: (0, 0))],
        [jax.ShapeDtypeStruct((t, d), F32), jax.ShapeDtypeStruct((t, d), BF16),
         jax.ShapeDtypeStruct((t, n), BF16), jax.ShapeDtypeStruct((1, d), F32)],
        [], ("arbitrary",), (dx1, x, g, da, dup, dvp, dqt, dk, dv, w), ride)


def _full_w_in(gathered):
    return gathered.transpose(1, 0, 2).reshape(D_MODEL, IN_COLS)


_SMALL_SHAPES = ((D_MODEL,), (4, POOL_GW, POOL_GW), (POOL_WIDTH,), (SG_WIDTH,), (SG_HEADS, CHUNK, CHUNK),
                 (SG_HEADS, CHUNK), (D_MODEL,))
_SMALL_SIZES = tuple(functools.reduce(lambda p, q: p * q, shp) for shp in _SMALL_SHAPES)
_SMALL_ROWS = sum(_SMALL_SIZES) // 128
_NORM1_ROWS = D_MODEL // 128


def _pack_small_layer(arrs):
    return jnp.concatenate([a.reshape(-1) for a in arrs]).reshape(_SMALL_ROWS, 128)


def _pack_small(arrs, final):
    return jnp.concatenate([_pack_small_layer([a[l] for a in arrs]) for l in range(DEPTH)] + [final.reshape(-1, 128)])


def _unpack_small(buf):
    per_layer = []
    for l in range(DEPTH):
        flat, off, outs = buf[l * _SMALL_ROWS:(l + 1) * _SMALL_ROWS].reshape(-1), 0, []
        for shp, size in zip(_SMALL_SHAPES, _SMALL_SIZES):
            outs.append(flat[off:off + size].reshape(shp))
            off += size
        per_layer.append(outs)
    return [jnp.stack([per_layer[l][k] for l in range(DEPTH)]) for k in range(len(_SMALL_SHAPES))] + \
           [buf[DEPTH * _SMALL_ROWS:].reshape(-1)]


def _tiles(t):
    return min(512, t), min(256, t // 4), min(2048, t)


def _layer_fwd(xl, wi, wo, wu, wd, small_w, l, ride_mlp=None):
    n1, pw, ps, sn, sw, sb, n2 = small_w
    tm, tq, _ = _tiles(xl.shape[0])
    wbd = jnp.zeros((4, POOL_GW, 4, POOL_GW), F32)
    for gi in range(4):
        wbd = wbd.at[gi, :, gi, :].set(pw[gi])
    wbd = wbd.reshape(POOL_WIDTH, POOL_WIDTH).astype(BF16)
    wm = (sw * jnp.tril(jnp.ones((CHUNK, CHUNK), F32))).astype(BF16)
    wmt = wm.transpose(0, 2, 1)
    bfull = jnp.repeat(sb.T, SG_HD, axis=1)
    g1, g2, psc, gn = n1[None, :], n2[None, :], ps[None, :], sn[None, :]

    if wo.ndim == 2:
        (ab, qkv, ktb, vtb), (wo,) = _inproj_fwd(xl, g1, wi, tm, tq, f"inproj_fwd{l}", Exchange([wo], False))
    else:
        ab, qkv, ktb, vtb = _inproj_fwd(xl, g1, wi, tm, tq, f"inproj_fwd{l}")
    ya = _pool_fwd(ab, wbd, psc, tm, f"pool_fwd{l}")
    yb = _sg_fwd(ab, gn, wm, bfull, tm, f"sg_fwd{l}")
    shards = [w for w in (wu, wd) if w.ndim == 2]
    if shards:
        (yct, cc, nvis), got = _sba_fwd(qkv, vtb, tq, f"sba_fwd{l}", SiblingGather(shards))
        got = iter(got)
        wu, wd = (next(got) if w.ndim == 2 else w for w in (wu, wd))
    else:
        yct, cc, nvis = _sba_fwd(qkv, vtb, tq, f"sba_fwd{l}")
    wo = wo.reshape(D_MODEL, D_MODEL)
    res = _outproj_mlp_fwd(xl, ya, yb, yct, wo, g2, wu, wd, min(MLP_ROWS, xl.shape[0]), f"mlp_fwd{l}", ride_mlp)
    (x1, u, x2), rode = res if ride_mlp is not None else (res, None)
    saved = dict(x=xl, ab=ab, qkv=qkv, ktb=ktb, cc=cc, nvis=nvis, ya=ya, yb=yb, yct=yct, x1=x1, u=u, wi=wi, wo=wo, wu=wu,
                 wd=wd, wbd=wbd, wm=wm, wmt=wmt, bfull=bfull, g1=g1, g2=g2, psc=psc, gn=gn)
    return x2, saved, rode


def _layer_bwd(dx, s, l, ride_mlp=None, exchange=True, gather_small=False):
    tm, tq, tw = _tiles(dx.shape[0])
    ktb = s["ktb"]
    res = _mlp_bwd(dx, s["x1"], s["g2"], s["u"], s["wu"], s["wd"], min(MLP_ROWS, dx.shape[0]), f"mlp_bwd{l}", ride_mlp)
    (dx1, du, r, h2, dx2b, dn2), rode = res if ride_mlp is not None else (res, None)
    dw_up = _matmul_tn(h2, du, D_MODEL, 2 * FF_SHARD, tw, f"dw_up{l}", by_column_block=FF_SHARD)
    dw_down = _matmul_tn(r, dx2b, 1024, D_MODEL, tw, f"dw_down{l}").reshape(N_DEV, FF_SHARD, D_MODEL)
    dya, dyb, dyc, dyct, dx1b = _outproj_bwd(dx1, s["wo"], tm, f"outproj_bwd{l}")
    dw_out = jnp.concatenate([
        _matmul_tn(jnp.concatenate([s["ya"], s["yb"]], axis=1), dx1b, POOL_WIDTH + SG_WIDTH, D_MODEL, tw, f"dw_out_ab{l}"),
        _matmul_tn(s["yct"], dx1b, SB_WIDTH, D_MODEL, tw, f"dw_out_c{l}", a_transposed=True)]
    ).reshape(N_DEV, OUT_SHARD, D_MODEL)
    if exchange:
        (dqt, dk, dv), (dw_up, dw_down) = _sba_bwd(s["qkv"], ktb, dyc, dyct, s["cc"], s["nvis"], tq, f"sba_bwd{l}",
                                                   Exchange([dw_up, dw_down], True))
    else:
        dqt, dk, dv = _sba_bwd(s["qkv"], ktb, dyc, dyct, s["cc"], s["nvis"], tq, f"sba_bwd{l}")
    dup, dvp, dgn, dwm, dbm = _sg_bwd(s["ab"], dyb, s["gn"], s["wm"], s["wmt"], s["bfull"], tm, f"sg_bwd{l}")
    da, dwbd, dpsc = _pool_bwd(s["ab"], dya, s["wbd"], s["psc"], tm, f"pool_bwd{l}")
    dpw = jnp.stack([dwbd[gi * POOL_GW:(gi + 1) * POOL_GW, gi * POOL_GW:(gi + 1) * POOL_GW] for gi in range(4)])
    small = _pack_small_layer([jnp.zeros((D_MODEL,), F32), dpw, dpsc[0], dgn[0], dwm, dbm[:, :SG_HEADS].T, dn2[0]])[_NORM1_ROWS:]
    ride = Exchange([dw_out] + [small] * gather_small, [True] + [False] * gather_small) if exchange else None
    res = _inproj_bwd(dx1, s["x"], s["g1"], da, dup, dvp, dqt, dk, dv, s["wi"], tm, f"inproj_bwd{l}", ride)
    if exchange:
        (dx, h1, dproj, dn1), (dw_out, small) = res[0], (res[1] + [small])[:2]
    else:
        dx, h1, dproj, dn1 = res
    dw_in = _matmul_tn(h1, dproj, D_MODEL, IN_COLS // 3, tw, f"dw_in{l}")
    dw_in = dw_in.reshape(D_MODEL, N_DEV, IN_SHARD).transpose(1, 0, 2)
    return dx, (dw_in, dw_out, dw_up, dw_down), (dn1.reshape(_NORM1_ROWS, 128), small), rode


def kernel(x, norm1, w_in, pool_w, pool_scale, sg_norm, sg_w, sg_b, w_out, norm2, w_up, w_down, final_norm, loss_target, m_norm1, m_w_in, m_pool_w, m_pool_scale, m_sg_norm, m_sg_w, m_sg_b, m_w_out, m_norm2, m_w_up, m_w_down, m_final_norm, v_norm1, v_w_in, v_pool_w, v_pool_scale, v_sg_norm, v_sg_w, v_sg_b, v_w_out, v_norm2, v_w_up, v_w_down, v_final_norm):
    t = x.shape[1]
    tm = _tiles(t)[0]
    small_w = (norm1, pool_w, pool_scale, sg_norm, sg_w, sg_b, norm2)
    big_w = (w_in, w_out, w_up, w_down)
    big_m = (m_w_in, m_w_out, m_w_up, m_w_down)
    big_v = (v_w_in, v_w_out, v_w_up, v_w_down)
    shards = [[w[l].astype(BF16) for w in big_w] for l in range(DEPTH)]

    wi0 = _full_w_in(SiblingGather(shards[0][:1]).alone("gather_w_in0")[0])
    sh_in1, sh_out1, sh_up1, sh_down1 = shards[1]
    x1, s0, (wi1, wo1, wd1) = _layer_fwd(x.reshape(t, D_MODEL), wi0, *shards[0][1:], tuple(w[0] for w in small_w), 0,
                                         ride_mlp=Exchange([sh_in1, sh_out1, sh_down1], False))
    x2, s1, _ = _layer_fwd(x1, _full_w_in(wi1), wo1, sh_up1, wd1, tuple(w[1] for w in small_w), 1)
    loss_local, dx, dfinal = _loss_grad(x2, final_norm[None, :], loss_target.reshape(t, D_MODEL), tm, "loss_grad")
    loss = lax.psum(loss_local[0, 0], MESH_AXES)

    dx, parts1, small1, _ = _layer_bwd(dx, s1, 1)
    early = jnp.concatenate(list(small1) + [dfinal.reshape(-1, 128)])
    dx, parts0, (dn1, small0), (recv_in1, early) = _layer_bwd(
        dx, s0, 0, ride_mlp=Exchange([parts1[0], early], [True, False]), gather_small=True)
    grad_x = dx.reshape(x.shape)
    recv_in0, dn1 = _exchange([parts0[0], dn1], [True, False], "scatter_w_in0_gather_norm1_0")
    small_all = jnp.concatenate([dn1, small0, early], axis=1)
    received = [[recv_in0] + list(parts0[1:]), [recv_in1] + list(parts1[1:])]

    big = [None] * 4
    for l in reversed(range(DEPTH)):
        for k in range(4):
            big[k] = _reduce_adamw(received[l][k], big_w[k], big_m[k], big_v[k], l, big[k], f"adamw{k}_{l}")

    sm = _reduce_adamw(
        small_all,
        _pack_small(small_w, final_norm)[None],
        _pack_small([m_norm1, m_pool_w, m_pool_scale, m_sg_norm, m_sg_w, m_sg_b, m_norm2], m_final_norm)[None],
        _pack_small([v_norm1, v_pool_w, v_pool_scale, v_sg_norm, v_sg_w, v_sg_b, v_norm2], v_final_norm)[None],
        0, None, "adamw_replicated")

    out = [loss, grad_x]
    for k in range(4):
        n1, pw, ps, sn, sw, sb, n2, fn = _unpack_small(sm[k][0])
        out += [n1, big[0][k], pw, ps, sn, sw, sb, big[1][k], n2, big[2][k], big[3][k], fn]
    return tuple(out)
```

```python
import functools

import jax
import jax.numpy as jnp
from jax import lax
from jax.experimental import pallas as pl
from jax.experimental.pallas import tpu as pltpu

F32 = jnp.float32
BF16 = jnp.bfloat16

D_MODEL = 1024
DEPTH = 2
POOL_WIDTH = 256
SG_WIDTH = 256
SB_WIDTH = 512
POOL_WINDOWS = (2, 4, 8, 16)
POOL_GW = 64
POOL_HALO = 16
CHUNK = 128
SG_HEADS = 4
SG_HD = 64
SB_HD = 64
SB_PAIRS = SB_WIDTH // 128
AB_COLS = POOL_WIDTH + 2 * SG_WIDTH
IN_COLS = AB_COLS + 3 * SB_WIDTH
D_FF = 4096
EPS = 1e-6
N_DEV = 8
FF_SHARD = D_FF // N_DEV
IN_SHARD = IN_COLS // N_DEV
OUT_SHARD = D_MODEL // N_DEV
ADAM_LR = 0.001
ADAM_B1 = 0.9
ADAM_B2 = 0.999
ADAM_EPS = 1e-08
ADAM_WD = 0.01
ADAM_STEP = 10
VMEM_LIMIT = 56 * 1024 * 1024
MLP_ROWS = 1024
MESH_AXES = ("x", "y", "c")


def _dot(a, b):
    return jnp.dot(a, b, preferred_element_type=F32)


def _dot_nt(a, b):
    return lax.dot_general(a, b, (((1,), (1,)), ((), ())), preferred_element_type=F32)


def _dot_tn(a, b):
    return lax.dot_general(a, b, (((0,), (0,)), ((), ())), preferred_element_type=F32)


def _rstd(x):
    return lax.rsqrt(jnp.mean(x * x, axis=-1, keepdims=True) + EPS)


def _rms_bwd(x, r, g, dh):
    gq = dh * g
    dx = r * gq - x * (r * r * r) * jnp.mean(gq * x, axis=-1, keepdims=True)
    dg = jnp.sum(dh * x * r, axis=0, keepdims=True)
    return dx, dg


def _params(sem=None):
    kw = dict(vmem_limit_bytes=VMEM_LIMIT)
    if sem is not None:
        kw["dimension_semantics"] = sem
    return pltpu.CompilerParams(**kw)


def _row_tile(rows, cap):
    best = 8
    for t in range(8, min(rows, cap) + 1, 8):
        if rows % t == 0:
            best = t
    return best


def _peer(k):
    x, y, c = lax.axis_index("x"), lax.axis_index("y"), lax.axis_index("c")
    return (1 - x if k & 4 else x, 1 - y if k & 2 else y, 1 - c if k & 1 else c)


def _my_index():
    return 4 * lax.axis_index("x") + 2 * lax.axis_index("y") + lax.axis_index("c")


class Exchange:
    def __init__(self, arrs, scatter):
        self.arrs = list(arrs)
        self.scatter = list(scatter) if isinstance(scatter, (list, tuple)) else [scatter] * len(self.arrs)
        self.n = len(self.arrs)
        self.any_specs = [pl.BlockSpec(memory_space=pl.ANY)] * self.n
        self.out_shape = [jax.ShapeDtypeStruct((N_DEV,) + a.shape[-2:], a.dtype) for a in self.arrs]
        self.sems = [pltpu.SemaphoreType.DMA((self.n, N_DEV - 1)), pltpu.SemaphoreType.DMA((self.n, N_DEV - 1)),
                     pltpu.SemaphoreType.DMA((self.n,))]

    def _copies(self, ins, outs, sems):
        send_sems, recv_sems, local_sems = sems
        me = _my_index()
        local, remote = [], []
        for a in range(self.n):
            sc = self.scatter[a]
            local.append(pltpu.make_async_copy(ins[a].at[me] if sc else ins[a], outs[a].at[me], local_sems.at[a]))
            for k in range(1, N_DEV):
                px, py, pc = _peer(k)
                remote.append(pltpu.make_async_remote_copy(
                    src_ref=ins[a].at[4 * px + 2 * py + pc] if sc else ins[a], dst_ref=outs[a].at[me],
                    send_sem=send_sems.at[a, k - 1], recv_sem=recv_sems.at[a, k - 1],
                    device_id=(px, py, pc), device_id_type=pl.DeviceIdType.MESH))
        return local, remote

    def start(self, ins, outs, sems):
        local, remote = self._copies(ins, outs, sems)
        for cp in local + remote:
            cp.start()

    def middle(self, ins, outs, sems):
        pass

    def wait(self, ins, outs, sems):
        local, remote = self._copies(ins, outs, sems)
        for cp in remote:
            cp.wait_recv()
        for cp in remote:
            cp.wait_send()
        for cp in local:
            cp.wait()

    def alone(self, name):
        n = self.n

        def body(*refs):
            ins, outs, sems = refs[:n], refs[n:2 * n], refs[2 * n:]
            self.start(ins, outs, sems)
            self.middle(ins, outs, sems)
            self.wait(ins, outs, sems)

        return pl.pallas_call(body, name=name, out_shape=self.out_shape, in_specs=self.any_specs,
                              out_specs=self.any_specs, scratch_shapes=self.sems)(*self.arrs)


class SiblingGather(Exchange):
    def __init__(self, arrs):
        super().__init__(arrs, False)

    def _plan(self, ins, outs, sems):
        send_sems, recv_sems, local_sems = sems
        x, y, core = lax.axis_index("x"), lax.axis_index("y"), lax.axis_index("c")
        me, sibling = (x, y, core), (x, y, 1 - core)
        chips = [(1 - x, y), (x, 1 - y), (1 - x, 1 - y)]
        plan = dict(local=[], first=[], arrived=[], passed=[], late=[])
        for a in range(self.n):
            slot = lambda p, a=a: outs[a].at[4 * p[0] + 2 * p[1] + p[2]]

            def copy(k, block, to, src=None, a=a, slot=slot):
                return pltpu.make_async_remote_copy(
                    src_ref=slot(block) if src is None else src, dst_ref=slot(block), send_sem=send_sems.at[a, k],
                    recv_sem=recv_sems.at[a, k], device_id=to, device_id_type=pl.DeviceIdType.MESH)

            plan["local"].append(pltpu.make_async_copy(ins[a], slot(me), local_sems.at[a]))
            plan["first"] += [copy(0, me, sibling, src=ins[a])] + [copy(1 + j, me, (*ch, core), src=ins[a])
                                                                  for j, ch in enumerate(chips)]
            plan["arrived"] += [copy(1 + j, (*ch, core), me) for j, ch in enumerate(chips)]
            plan["passed"] += [copy(4 + j, (*ch, core), sibling) for j, ch in enumerate(chips)]
            plan["late"] += [copy(0, sibling, me)] + [copy(4 + j, (*ch, 1 - core), me) for j, ch in enumerate(chips)]
        return plan

    def start(self, ins, outs, sems):
        plan = self._plan(ins, outs, sems)
        for cp in plan["local"] + plan["first"]:
            cp.start()

    def middle(self, ins, outs, sems):
        plan = self._plan(ins, outs, sems)
        for arrived, passed in zip(plan["arrived"], plan["passed"]):
            arrived.wait_recv()
            passed.start()

    def wait(self, ins, outs, sems):
        plan = self._plan(ins, outs, sems)
        for cp in plan["late"]:
            cp.wait_recv()
        for cp in plan["first"] + plan["passed"]:
            cp.wait_send()
        for cp in plan["local"]:
            cp.wait()


def _exchange(arrs, scatter, name):
    return Exchange(arrs, scatter).alone(name)


def _call(body, name, grid, in_specs, out_specs, out_shape, scratch_shapes, semantics, args, ride=None):
    if ride is None:
        return pl.pallas_call(body, name=name, grid=grid, in_specs=in_specs, out_specs=out_specs, out_shape=out_shape,
                              scratch_shapes=scratch_shapes, compiler_params=_params(semantics))(*args)
    single = not isinstance(out_shape, (list, tuple))
    out_specs, out_shape = ([out_specs], [out_shape]) if single else (list(out_specs), list(out_shape))
    n_in, n_out, n_scr, n = len(in_specs), len(out_specs), len(scratch_shapes), ride.n

    def riding(*refs):
        ins, cins = refs[:n_in], refs[n_in:n_in + n]
        outs, couts = refs[n_in + n:n_in + n + n_out], refs[n_in + n + n_out:n_in + 2 * n + n_out]
        scr, sems = refs[n_in + 2 * n + n_out:n_in + 2 * n + n_out + n_scr], refs[n_in + 2 * n + n_out + n_scr:]
        step = functools.reduce(lambda s, a: s * grid[a] + pl.program_id(a), range(len(grid)), 0)
        steps = functools.reduce(lambda p, q: p * q, grid)

        @pl.when(step == 0)
        def _():
            ride.start(cins, couts, sems)

        @pl.when(step == (3 * steps) // 4)
        def _():
            ride.middle(cins, couts, sems)

        body(*ins, *outs, *scr)

        @pl.when(step == steps - 1)
        def _():
            ride.wait(cins, couts, sems)

    res = pl.pallas_call(
        riding, name=name, grid=grid, in_specs=list(in_specs) + ride.any_specs, out_specs=out_specs + ride.any_specs,
        out_shape=out_shape + ride.out_shape, scratch_shapes=list(scratch_shapes) + ride.sems,
        compiler_params=_params(("arbitrary",) * len(grid)))(*args, *ride.arrs)
    own = res[0] if single else list(res[:n_out])
    return own, list(res[n_out:])


def _reduce_adamw(parts, w, m, v, l, prev, name):
    _, rows, n = parts.shape
    tr = _row_tile(rows, max(8, (1 << 18) // n))
    c1 = 1.0 - ADAM_B1 ** ADAM_STEP
    c2 = 1.0 - ADAM_B2 ** ADAM_STEP

    def body(p_ref, w_ref, m_ref, v_ref, *rest):
        g_ref, d_ref, nm_ref, nv_ref = rest[-4:]
        g = p_ref[0].astype(F32)
        for s in range(1, N_DEV):
            g = g + p_ref[s].astype(F32)
        nm = ADAM_B1 * m_ref[...] + (1.0 - ADAM_B1) * g
        nv = ADAM_B2 * v_ref[...] + (1.0 - ADAM_B2) * (g * g)
        m_hat = nm / c1
        v_hat = nv / c2
        g_ref[...] = g
        d_ref[...] = -ADAM_LR * (m_hat / (jnp.sqrt(v_hat) + ADAM_EPS) + ADAM_WD * w_ref[...])
        nm_ref[...] = nm
        nv_ref[...] = nv

    blk = pl.BlockSpec((None, tr, n), lambda i: (l, i, 0))
    out = jax.ShapeDtypeStruct(w.shape, F32)
    prev = list(prev) if prev is not None else []
    return pl.pallas_call(
        body, name=name, grid=(rows // tr,),
        in_specs=[pl.BlockSpec((N_DEV, tr, n), lambda i: (0, i, 0)), blk, blk, blk] + [pl.BlockSpec(memory_space=pl.ANY)] * len(prev),
        out_specs=[blk, blk, blk, blk], out_shape=[out, out, out, out],
        input_output_aliases={4 + k: k for k in range(len(prev))},
        compiler_params=_params(("parallel",)),
    )(parts, w, m, v, *prev)


def _inproj_fwd(x, g, w, tm, tq, name):
    t, d = x.shape
    n = w.shape[1]
    nb = t // tq
    per = tm // tq

    def body(x_ref, g_ref, w_ref, ab_ref, qkv_ref, kt_ref, vt_ref):
        xx = x_ref[...]
        h = (xx * _rstd(xx) * g_ref[...]).astype(BF16)
        ab_ref[...] = _dot(h, w_ref[:, :AB_COLS])
        qkv = _dot(h, w_ref[:, AB_COLS:])
        qkv_ref[...] = qkv.astype(BF16)
        for which, out_ref in ((1, kt_ref), (2, vt_ref)):
            for p in range(SB_PAIRS):
                for b in range(per):
                    cols = which * SB_WIDTH + p * 128
                    out_ref[p, b] = qkv[b * tq:(b + 1) * tq, cols:cols + 128].T.astype(BF16)

    tb = pl.BlockSpec((SB_PAIRS, per, 128, tq), lambda i: (0, i, 0, 0))
    tshape = jax.ShapeDtypeStruct((SB_PAIRS, nb, 128, tq), BF16)
    return _call(
        body, name, (t // tm,),
        [pl.BlockSpec((tm, d), lambda i: (i, 0)), pl.BlockSpec((1, d), lambda i: (0, 0)),
         pl.BlockSpec((d, n), lambda i: (0, 0))],
        [pl.BlockSpec((tm, AB_COLS), lambda i: (i, 0)), pl.BlockSpec((tm, n - AB_COLS), lambda i: (i, 0)), tb, tb],
        [jax.ShapeDtypeStruct((t, AB_COLS), F32), jax.ShapeDtypeStruct((t, n - AB_COLS), BF16), tshape, tshape],
        [], ("parallel",), (x, g, w))


def _pool_window_sums(xx, forward):
    n = xx.shape[0]
    sh = (lambda k: n - k) if forward else (lambda k: k)
    s2 = xx + pltpu.roll(xx, sh(1), 0)
    s4 = s2 + pltpu.roll(s2, sh(2), 0)
    s8 = s4 + pltpu.roll(s4, sh(4), 0)
    s16 = s8 + pltpu.roll(s8, sh(8), 0)
    grp = lax.broadcasted_iota(jnp.int32, (1, POOL_WIDTH), 1) // POOL_GW
    return jnp.where(grp == 0, s2, jnp.where(grp == 1, s4, jnp.where(grp == 2, s8, s16)))


def _pool_count(t0, rows):
    grp = lax.broadcasted_iota(jnp.int32, (1, POOL_WIDTH), 1) // POOL_GW
    win = jnp.where(grp == 0, 2, jnp.where(grp == 1, 4, jnp.where(grp == 2, 8, 16)))
    tt = t0 + lax.broadcasted_iota(jnp.int32, (rows, 1), 0)
    return jnp.minimum(tt + 1, win).astype(F32)


def _pool_diff(cur, prev, t0):
    tm = cur.shape[0]
    sums = _pool_window_sums(jnp.concatenate([prev, cur], axis=0), False)[POOL_HALO:]
    return sums / _pool_count(t0, tm) - cur


def _pool_fwd(ab, wbd, scale, tm, name):
    t = ab.shape[0]
    hb = tm // POOL_HALO

    def body(cur_ref, prev_ref, w_ref, s_ref, y_ref):
        i = pl.program_id(0)
        prev = jnp.where(i == 0, 0.0, prev_ref[...])
        d = _pool_diff(cur_ref[...], prev, i * tm).astype(BF16)
        y_ref[...] = (_dot(d, w_ref[...]) * s_ref[...]).astype(BF16)

    return pl.pallas_call(
        body, name=name, grid=(t // tm,),
        in_specs=[pl.BlockSpec((tm, POOL_WIDTH), lambda i: (i, 0)),
                  pl.BlockSpec((POOL_HALO, POOL_WIDTH), lambda i: (jnp.maximum(i * hb - 1, 0), 0)),
                  pl.BlockSpec((POOL_WIDTH, POOL_WIDTH), lambda i: (0, 0)),
                  pl.BlockSpec((1, POOL_WIDTH), lambda i: (0, 0))],
        out_specs=pl.BlockSpec((tm, POOL_WIDTH), lambda i: (i, 0)),
        out_shape=jax.ShapeDtypeStruct((t, POOL_WIDTH), BF16),
        compiler_params=_params(("parallel",)),
    )(ab, ab, wbd, scale)


_GELU_K = 0.7978845608028654
_GELU_A = 0.044715


def _gelu(x):
    return 0.5 * x * (1.0 + jnp.tanh(_GELU_K * (x + _GELU_A * (x * x * x))))


def _gelu_grad(x):
    th = jnp.tanh(_GELU_K * (x + _GELU_A * (x * x * x)))
    return 0.5 * (1.0 + th) + 0.5 * x * (1.0 - th * th) * (_GELU_K * (1.0 + 3.0 * _GELU_A * (x * x)))


def _head_lanes(h):
    return lax.broadcasted_iota(jnp.int32, (1, SG_WIDTH), 1) // SG_HD == h


def _sg_fwd(ab, gn, wm, bfull, tm, name):
    t = ab.shape[0]

    def body(u_ref, v_ref, gn_ref, wm_ref, b_ref, y_ref):
        v = _gelu(v_ref[...])
        vn = (v * _rstd(v) * gn_ref[...]).astype(BF16)
        for c in range(tm // CHUNK):
            rows = slice(c * CHUNK, (c + 1) * CHUNK)
            vc = vn[rows]
            sv = b_ref[...]
            for h in range(SG_HEADS):
                sv = sv + jnp.where(_head_lanes(h), _dot(wm_ref[h], vc), 0.0)
            y_ref[rows, :] = (_gelu(u_ref[rows, :]) * sv).astype(BF16)

    return pl.pallas_call(
        body, name=name, grid=(t // tm,),
        in_specs=[pl.BlockSpec((tm, SG_WIDTH), lambda i: (i, 1)), pl.BlockSpec((tm, SG_WIDTH), lambda i: (i, 2)),
                  pl.BlockSpec((1, SG_WIDTH), lambda i: (0, 0)),
                  pl.BlockSpec((SG_HEADS, CHUNK, CHUNK), lambda i: (0, 0, 0)),
                  pl.BlockSpec((CHUNK, SG_WIDTH), lambda i: (0, 0))],
        out_specs=pl.BlockSpec((tm, SG_WIDTH), lambda i: (i, 0)),
        out_shape=jax.ShapeDtypeStruct((t, SG_WIDTH), BF16),
        compiler_params=_params(("parallel",)),
    )(ab, ab, gn, wm, bfull)


LOG2E = 1.4426950408889634
SB_SCALE = 0.125 * LOG2E
SB_DEAD_LOG2 = 152.0
SB_QUERY_BLOCKS_PER_STEP = 2


def _log2_sigmoids(y):
    neg_abs = lax.bitcast_convert_type(lax.bitcast_convert_type(y, jnp.uint32) | jnp.uint32(0x80000000), F32)
    lb = jnp.minimum(y, 0.0) - jnp.log(1.0 + jnp.exp2(neg_abs)) * LOG2E
    return lb, lb - y


def _split(x):
    hi = x.astype(BF16)
    return hi, (x - hi.astype(F32)).astype(BF16)


def _tri_dot(tri, x):
    hi, lo = _split(x)
    return _dot(tri, hi) + _dot(tri, lo)


def _sba_fwd(qkv, vtb, tq, name, ride=None):
    t = qkv.shape[0]
    nb = t // tq
    qb = SB_QUERY_BLOCKS_PER_STEP
    upper = (jnp.arange(tq)[None, :] > jnp.arange(tq)[:, None]).astype(BF16)

    def body(q_ref, k_ref, vt_ref, up_ref, ot_ref, c_ref, n_ref):
        pair, step = pl.program_id(0), pl.program_id(1)
        refs = (q_ref, k_ref, vt_ref, up_ref, ot_ref, c_ref, n_ref)
        subs = list(range(qb))

        @pl.when(step == 0)
        def _():
            for sub in subs:
                query_blocks([sub], pair, step, *refs)

        @pl.when(step > 0)
        def _():
            query_blocks(subs, pair, step, *refs)

    def query_blocks(subs, pair, step, q_ref, k_ref, vt_ref, up_ref, ot_ref, c_ref, n_ref):
        index = {sub: step * qb + sub for sub in subs}
        mine = {sub: slice(sub * tq, (sub + 1) * tq) for sub in subs}
        up = up_ref[...]
        lane_head = lax.broadcasted_iota(jnp.int32, (1, 128), 1) // SB_HD
        sub_head = lax.broadcasted_iota(jnp.int32, (128, 1), 0) // SB_HD
        causal = (lax.broadcasted_iota(jnp.int32, (tq, tq), 0) < lax.broadcasted_iota(jnp.int32, (tq, tq), 1))
        qh = {}
        for sub in subs:
            q = q_ref[mine[sub], :]
            qh[sub] = [jnp.where(lane_head == h, q, jnp.zeros_like(q)) for h in range(2)]

        def blocks(work, carry):
            cs = {sub: list(carry[sub][:2]) for sub in carry}
            acc = {sub: carry[sub][2] for sub in carry}
            kj = [k_ref[pl.ds(pl.multiple_of(j * tq, tq), tq), :] for _, j, _ in work]
            vt = [vt_ref[j] for _, j, _ in work]
            chains = [(w, h) for w in range(len(work)) for h in range(2)]
            z = [_dot_nt(kj[w], qh[work[w][0]][h]) for w, h in chains]
            ls = [_log2_sigmoids(zz * SB_SCALE) for zz in z]
            lb = [x[0] for x in ls]
            l1 = [jnp.where(causal, x[1], 0.0) if work[w][2] else x[1] for x, (w, h) in zip(ls, chains)]
            after = [_tri_dot(up, x) for x in l1]
            a = []
            for n, (w, h) in enumerate(chains):
                sub, j, diag = work[w]
                c_ref[h, pl.ds(j, 1), mine[sub]] = cs[sub][h]
                an = jnp.exp2(lb[n] + after[n] + cs[sub][h])
                a.append(jnp.where(causal, an, 0.0) if diag else an)
                cs[sub][h] = cs[sub][h] + after[n][0:1, :] + l1[n][0:1, :]
            alive = {sub: (jnp.max(jnp.maximum(*cs[sub])) > -SB_DEAD_LOG2).astype(jnp.int32) for sub in cs}
            for n, (w, h) in enumerate(chains):
                sub = work[w][0]
                acc[sub] = acc[sub] + _dot(jnp.where(sub_head == h, vt[w], jnp.zeros_like(vt[w])), a[n].astype(BF16))
            return {sub: (alive[sub], cs[sub][0], cs[sub][1], acc[sub]) for sub in cs}

        zero = jnp.zeros((1, tq), F32)
        start = {sub: (zero, zero, jnp.zeros((128, tq), F32)) for sub in subs}
        both = lambda: blocks([(sub, index[sub] - b, b == 0) for sub in subs for b in range(2)], start)
        if len(subs) == 1:
            joint = lax.cond(index[subs[0]] > 0, both, lambda: blocks([(subs[0], index[subs[0]], True)], start))
        else:
            joint = both()
        for sub in subs:
            i = index[sub]

            def left(state, sub=sub, i=i):
                s, _, c0, c1, acc = state
                return (s + 1,) + blocks([(sub, i - 1 - s, False)], {sub: (c0, c1, acc)})[sub]

            state = lax.while_loop(lambda st, i=i: (st[0] < i) & (st[1] > 0), left, (jnp.minimum(i, 1),) + tuple(joint[sub]))
            ot_ref[:, mine[sub]] = state[4].astype(BF16)
            n_ref[pair, i] = (state[0] + 1).astype(F32)

    return _call(
        body, name, (SB_PAIRS, nb // qb),
        [pl.BlockSpec((qb * tq, 128), lambda p, i: (i, p)),
         pl.BlockSpec((t, 128), lambda p, i: (0, SB_PAIRS + p)),
         pl.BlockSpec((None, nb, 128, tq), lambda p, i: (p, 0, 0, 0)),
         pl.BlockSpec((tq, tq), lambda p, i: (0, 0))],
        [pl.BlockSpec((128, qb * tq), lambda p, i: (p, i)),
         pl.BlockSpec((2, nb, qb * tq), lambda p, i: (p, 0, i)),
         pl.BlockSpec(memory_space=pltpu.SMEM)],
        [jax.ShapeDtypeStruct((SB_WIDTH, t), BF16),
         jax.ShapeDtypeStruct((2 * SB_PAIRS, nb, t), F32),
         jax.ShapeDtypeStruct((SB_PAIRS, nb), F32)],
        [], ("arbitrary", "arbitrary"), (qkv, qkv, vtb, upper), ride)


def _outproj_mlp_fwd(x, ya, yb, yct, wo, g2, wup, wdn, tm, name, ride=None):
    t, d = x.shape
    nf = wup.shape[0]

    def body(x_ref, ya_ref, yb_ref, yct_ref, wo_ref, g_ref, wu_ref, wd_ref, x1_ref, u_ref, x2_ref, h_ref, acc_ref):
        j = pl.program_id(1)

        @pl.when(j == 0)
        def _():
            x1 = (x_ref[...] + _dot(ya_ref[...], wo_ref[0:POOL_WIDTH, :])
                  + _dot(yb_ref[...], wo_ref[POOL_WIDTH:POOL_WIDTH + SG_WIDTH, :])
                  + _dot_tn(yct_ref[...], wo_ref[POOL_WIDTH + SG_WIDTH:, :]))
            x1_ref[...] = x1
            h_ref[...] = (x1 * _rstd(x1) * g_ref[...]).astype(BF16)
            acc_ref[...] = x1

        u = _dot(h_ref[...], wu_ref[...])
        u_ref[...] = u
        r = jnp.maximum(u, 0.0)
        acc_ref[...] += _dot((r * r).astype(BF16), wd_ref[...])

        @pl.when(j == nf - 1)
        def _():
            x2_ref[...] = acc_ref[...]

    row = lambda w: pl.BlockSpec((tm, w), lambda i, j: (i, 0))
    return _call(
        body, name, (t // tm, nf),
        [row(d), row(POOL_WIDTH), row(SG_WIDTH), pl.BlockSpec((SB_WIDTH, tm), lambda i, j: (0, i)),
         pl.BlockSpec((d, d), lambda i, j: (0, 0)), pl.BlockSpec((1, d), lambda i, j: (0, 0)),
         pl.BlockSpec((None, d, FF_SHARD), lambda i, j: (j, 0, 0)),
         pl.BlockSpec((None, FF_SHARD, d), lambda i, j: (j, 0, 0))],
        [row(d), pl.BlockSpec((tm, FF_SHARD), lambda i, j: (i, j)), row(d)],
        [jax.ShapeDtypeStruct((t, d), F32), jax.ShapeDtypeStruct((t, nf * FF_SHARD), F32),
         jax.ShapeDtypeStruct((t, d), F32)],
        [pltpu.VMEM((tm, d), BF16), pltpu.VMEM((tm, d), F32)],
        ("parallel", "arbitrary"), (x, ya, yb, yct, wo, g2, wup, wdn), ride)


def _loss_grad(x, g, target, tm, name):
    t, d = x.shape
    nt = t // tm

    def body(x_ref, g_ref, t_ref, loss_ref, dx_ref, dg_ref, sq_ref):
        i = pl.program_id(0)

        @pl.when(i == 0)
        def _():
            sq_ref[...] = jnp.zeros_like(sq_ref)
            dg_ref[...] = jnp.zeros_like(dg_ref)

        xx = x_ref[...]
        r = _rstd(xx)
        err = xx * r * g_ref[...] - t_ref[...]
        sq_ref[...] += jnp.sum(err * err, axis=0, keepdims=True)
        dx, dg = _rms_bwd(xx, r, g_ref[...], err * (1.0 / d))
        dx_ref[...] = dx
        dg_ref[...] += dg

        @pl.when(i == nt - 1)
        def _():
            loss_ref[...] = jnp.sum(sq_ref[...], axis=1, keepdims=True) * (0.5 / d)

    return pl.pallas_call(
        body, name=name, grid=(nt,),
        in_specs=[pl.BlockSpec((tm, d), lambda i: (i, 0)), pl.BlockSpec((1, d), lambda i: (0, 0)),
                  pl.BlockSpec((tm, d), lambda i: (i, 0))],
        out_specs=[pl.BlockSpec((1, 1), lambda i: (0, 0)), pl.BlockSpec((tm, d), lambda i: (i, 0)),
                   pl.BlockSpec((1, d), lambda i: (0, 0))],
        out_shape=[jax.ShapeDtypeStruct((1, 1), F32), jax.ShapeDtypeStruct((t, d), F32),
                   jax.ShapeDtypeStruct((1, d), F32)],
        scratch_shapes=[pltpu.VMEM((1, d), F32)],
        compiler_params=_params(("arbitrary",)),
    )(x, g, target)


def _mlp_bwd(dx2, x1, g2, u, wup, wdn, tm, name, ride=None):
    t, d = dx2.shape
    nf = wup.shape[0]
    nt = t // tm

    def body(dx2_ref, x1_ref, g_ref, u_ref, wu_ref, wd_ref, dx1_ref, du_ref, r_ref, h_ref, dxb_ref, dg_ref, acc_ref):
        i, j = pl.program_id(0), pl.program_id(1)

        @pl.when(j == 0)
        def _():
            x1 = x1_ref[...]
            h_ref[...] = (x1 * _rstd(x1) * g_ref[...]).astype(BF16)
            dxb_ref[...] = dx2_ref[...].astype(BF16)
            acc_ref[...] = jnp.zeros_like(acc_ref)

        @pl.when((i == 0) & (j == 0))
        def _():
            dg_ref[...] = jnp.zeros_like(dg_ref)

        dr = _dot_nt(dxb_ref[...], wd_ref[...])
        ru = jnp.maximum(u_ref[...], 0.0)
        du = (dr * (2.0 * ru)).astype(BF16)
        du_ref[...] = du
        r_ref[...] = (ru * ru).astype(BF16)
        acc_ref[...] += _dot_nt(du, wu_ref[...])

        @pl.when(j == nf - 1)
        def _():
            x1 = x1_ref[...]
            dx, dg = _rms_bwd(x1, _rstd(x1), g_ref[...], acc_ref[...])
            dx1_ref[...] = dx2_ref[...] + dx
            dg_ref[...] += dg

    row = lambda w: pl.BlockSpec((tm, w), lambda i, j: (i, 0))
    col = pl.BlockSpec((tm, FF_SHARD), lambda i, j: (i, j))
    return _call(
        body, name, (nt, nf),
        [row(d), row(d), pl.BlockSpec((1, d), lambda i, j: (0, 0)), col,
         pl.BlockSpec((None, d, FF_SHARD), lambda i, j: (j, 0, 0)),
         pl.BlockSpec((None, FF_SHARD, d), lambda i, j: (j, 0, 0))],
        [row(d), col, col, row(d), row(d), pl.BlockSpec((1, d), lambda i, j: (0, 0))],
        [jax.ShapeDtypeStruct((t, d), F32), jax.ShapeDtypeStruct((t, nf * FF_SHARD), BF16),
         jax.ShapeDtypeStruct((t, nf * FF_SHARD), BF16), jax.ShapeDtypeStruct((t, d), BF16),
         jax.ShapeDtypeStruct((t, d), BF16), jax.ShapeDtypeStruct((1, d), F32)],
        [pltpu.VMEM((tm, d), F32)], ("arbitrary", "arbitrary"), (dx2, x1, g2, u, wup, wdn), ride)


def _matmul_tn(a, b, bm, bn, bt, name, by_column_block=0, a_transposed=False):
    m, t = a.shape if a_transposed else a.shape[::-1]
    n = b.shape[1]
    nk = t // bt

    def body(a_ref, b_ref, o_ref, acc_ref):
        k = pl.program_id(2)

        @pl.when(k == 0)
        def _():
            acc_ref[...] = jnp.zeros_like(acc_ref)

        acc_ref[...] += _dot(a_ref[...], b_ref[...]) if a_transposed else _dot_tn(a_ref[...], b_ref[...])

        @pl.when(k == nk - 1)
        def _():
            if by_column_block:
                for c in range(bn // by_column_block):
                    o_ref[c] = acc_ref[:, c * by_column_block:(c + 1) * by_column_block].astype(BF16)
            else:
                o_ref[...] = acc_ref[...].astype(BF16)

    if by_column_block:
        out_spec = pl.BlockSpec((bn // by_column_block, bm, by_column_block), lambda i, j, k: (j, i, 0))
        out_shape = jax.ShapeDtypeStruct((n // by_column_block, m, by_column_block), BF16)
    else:
        out_spec = pl.BlockSpec((bm, bn), lambda i, j, k: (i, j))
        out_shape = jax.ShapeDtypeStruct((m, n), BF16)
    a_spec = (pl.BlockSpec((bm, bt), lambda i, j, k: (i, k)) if a_transposed
              else pl.BlockSpec((bt, bm), lambda i, j, k: (k, i)))
    return _call(body, name, (m // bm, n // bn, nk), [a_spec, pl.BlockSpec((bt, bn), lambda i, j, k: (k, j))],
                 out_spec, out_shape, [pltpu.VMEM((bm, bn), F32)], ("parallel", "parallel", "arbitrary"), (a, b))


def _outproj_bwd(dx1, wo, tm, name):
    t, d = dx1.shape
    c2 = POOL_WIDTH + SG_WIDTH

    def body(dx_ref, wo_ref, dya_ref, dyb_ref, dyc_ref, dyct_ref, dxb_ref):
        dxb = dx_ref[...].astype(BF16)
        dxb_ref[...] = dxb
        dya_ref[...] = _dot_nt(dxb, wo_ref[0:POOL_WIDTH, :])
        dyb_ref[...] = _dot_nt(dxb, wo_ref[POOL_WIDTH:c2, :])
        dyc_ref[...] = _dot_nt(dxb, wo_ref[c2:, :]).astype(BF16)
        dyct_ref[...] = _dot_nt(wo_ref[c2:, :], dxb).astype(BF16)

    row = lambda w: pl.BlockSpec((tm, w), lambda i: (i, 0))
    return pl.pallas_call(
        body, name=name, grid=(t // tm,),
        in_specs=[row(d), pl.BlockSpec((d, d), lambda i: (0, 0))],
        out_specs=[row(POOL_WIDTH), row(SG_WIDTH), row(SB_WIDTH), pl.BlockSpec((SB_WIDTH, tm), lambda i: (0, i)), row(d)],
        out_shape=[jax.ShapeDtypeStruct((t, POOL_WIDTH), F32), jax.ShapeDtypeStruct((t, SG_WIDTH), F32),
                   jax.ShapeDtypeStruct((t, SB_WIDTH), BF16), jax.ShapeDtypeStruct((SB_WIDTH, t), BF16),
                   jax.ShapeDtypeStruct((t, d), BF16)],
        compiler_params=_params(("parallel",)),
    )(dx1, wo)


def _sba_bwd(qkv, ktb, dyc, dyct, cc, nvis, tq, name, ride=None):
    t = qkv.shape[0]
    nb = t // tq
    idx = jnp.arange(tq)
    upper = (idx[None, :] > idx[:, None]).astype(BF16)
    lower = (idx[None, :] < idx[:, None]).astype(BF16)

    qb = SB_QUERY_BLOCKS_PER_STEP

    def body(q_ref, k_ref, v_ref, kt_ref, do_ref, dot_ref, c_ref, n_ref, up_ref, lo_ref, dqt_ref, dk_ref, dv_ref):
        pair, step = pl.program_id(0), pl.program_id(1)
        refs = (q_ref, k_ref, v_ref, kt_ref, do_ref, dot_ref, c_ref, n_ref, up_ref, lo_ref, dqt_ref, dk_ref, dv_ref)
        subs = list(range(qb))

        @pl.when(step == 0)
        def _():
            dk_ref[...] = jnp.zeros_like(dk_ref)
            dv_ref[...] = jnp.zeros_like(dv_ref)
            for sub in subs:
                query_blocks([sub], pair, step, *refs)

        @pl.when(step > 0)
        def _():
            query_blocks(subs, pair, step, *refs)

    def query_blocks(subs, pair, step, q_ref, k_ref, v_ref, kt_ref, do_ref, dot_ref, c_ref, n_ref, up_ref, lo_ref,
                     dqt_ref, dk_ref, dv_ref):
        index = {sub: step * qb + sub for sub in subs}
        mine = {sub: slice(sub * tq, (sub + 1) * tq) for sub in subs}
        up = up_ref[...]
        lo = lo_ref[...]
        lane_head = lax.broadcasted_iota(jnp.int32, (1, 128), 1) // SB_HD
        sub_head = lax.broadcasted_iota(jnp.int32, (128, 1), 0) // SB_HD
        causal = (lax.broadcasted_iota(jnp.int32, (tq, tq), 0) < lax.broadcasted_iota(jnp.int32, (tq, tq), 1))
        hms = [lane_head == h for h in range(2)]
        qh, qs, doh, dot = {}, {}, {}, {}
        for sub in subs:
            q, do = q_ref[mine[sub], :], do_ref[mine[sub], :]
            qh[sub] = [jnp.where(hm, q, jnp.zeros_like(q)) for hm in hms]
            qs[sub] = [x * 0.125 for x in qh[sub]]
            doh[sub] = [jnp.where(hm, do, jnp.zeros_like(do)) for hm in hms]
            dot[sub] = dot_ref[:, mine[sub]]

        def blocks(work, carry):
            cgs = {sub: list(carry[sub][:2]) for sub in carry}
            dqt = {sub: carry[sub][2] for sub in carry}
            rows = [pl.ds(pl.multiple_of(j * tq, tq), tq) for _, j, _ in work]
            kj = [k_ref[r, :] for r in rows]
            vj = [v_ref[r, :] for r in rows]
            kt = [kt_ref[j] for _, j, _ in work]
            chains = [(w, h) for w in range(len(work)) for h in range(2)]
            z = [_dot_nt(kj[w], qh[work[w][0]][h]) for w, h in chains]
            da = [_dot(jnp.where(hms[h], vj[w], jnp.zeros_like(vj[w])), dot[work[w][0]]) for w, h in chains]
            ls = [_log2_sigmoids(zz * SB_SCALE) for zz in z]
            lb = [x[0] for x in ls]
            l1 = [jnp.where(causal, x[1], 0.0) if work[w][2] else x[1] for x, (w, h) in zip(ls, chains)]
            after = [_tri_dot(up, x) for x in l1]
            a = [jnp.exp2(lb[n] + after[n] + c_ref[h, pl.ds(work[w][1], 1), mine[work[w][0]]])
                 for n, (w, h) in enumerate(chains)]
            a = [jnp.where(causal, a[n], 0.0) if work[w][2] else a[n] for n, (w, h) in enumerate(chains)]
            g = [a[n] * da[n] for n in range(len(chains))]
            gloc = [_dot(lo, x.astype(BF16)) for x in g]
            dzb = []
            for n, (w, h) in enumerate(chains):
                sub, _, diag = work[w]
                gsum = gloc[n] + cgs[sub][h]
                dz = g[n] - jnp.exp2(lb[n]) * (g[n] + gsum)
                dzb.append((jnp.where(causal, dz, 0.0) if diag else dz).astype(BF16))
                cgs[sub][h] = gsum[tq - 1:tq, :] + g[n][tq - 1:tq, :]
            ab = [x.astype(BF16) for x in a]
            for n, (w, h) in enumerate(chains):
                sub = work[w][0]
                dqt[sub] = dqt[sub] + _dot(jnp.where(sub_head == h, kt[w], jnp.zeros_like(kt[w])), dzb[n])
            for w, (sub, _, _) in enumerate(work):
                dk_ref[rows[w], :] += _dot(dzb[2 * w], qs[sub][0]) + _dot(dzb[2 * w + 1], qs[sub][1])
                dv_ref[rows[w], :] += _dot(ab[2 * w], doh[sub][0]) + _dot(ab[2 * w + 1], doh[sub][1])
            return {sub: (cgs[sub][0], cgs[sub][1], dqt[sub]) for sub in cgs}

        zero = jnp.zeros((1, tq), F32)
        carry = {}
        for sub in subs:
            i = index[sub]
            n = jnp.clip(n_ref[pair, i].astype(jnp.int32), 1, i + 1)
            carry[sub] = lax.fori_loop(jnp.minimum(i + 1 - n, i - 1), i - 1,
                                       lambda s, cr, sub=sub: blocks([(sub, s, False)], {sub: cr})[sub],
                                       (zero, zero, jnp.zeros((128, tq), F32)))
        both = lambda: blocks([(sub, index[sub] - 1 + b, b == 1) for sub in subs for b in range(2)], carry)
        if len(subs) == 1:
            carry = lax.cond(index[subs[0]] > 0, both, lambda: blocks([(subs[0], index[subs[0]], True)], carry))
        else:
            carry = both()
        for sub in subs:
            dqt_ref[:, mine[sub]] = carry[sub][2] * 0.125

    return _call(
        body, name, (SB_PAIRS, nb // qb),
        [pl.BlockSpec((qb * tq, 128), lambda p, i: (i, p)),
         pl.BlockSpec((t, 128), lambda p, i: (0, SB_PAIRS + p)),
         pl.BlockSpec((t, 128), lambda p, i: (0, 2 * SB_PAIRS + p)),
         pl.BlockSpec((None, nb, 128, tq), lambda p, i: (p, 0, 0, 0)),
         pl.BlockSpec((qb * tq, 128), lambda p, i: (i, p)),
         pl.BlockSpec((128, qb * tq), lambda p, i: (p, i)),
         pl.BlockSpec((2, nb, qb * tq), lambda p, i: (p, 0, i)),
         pl.BlockSpec(memory_space=pltpu.SMEM),
         pl.BlockSpec((tq, tq), lambda p, i: (0, 0)),
         pl.BlockSpec((tq, tq), lambda p, i: (0, 0))],
        [pl.BlockSpec((128, qb * tq), lambda p, i: (p, i)),
         pl.BlockSpec((t, 128), lambda p, i: (0, p)),
         pl.BlockSpec((t, 128), lambda p, i: (0, p))],
        [jax.ShapeDtypeStruct((SB_WIDTH, t), F32), jax.ShapeDtypeStruct((t, SB_WIDTH), F32),
         jax.ShapeDtypeStruct((t, SB_WIDTH), F32)],
        [], ("arbitrary", "arbitrary"), (qkv, qkv, qkv, ktb, dyc, dyct, cc, nvis, upper, lower), ride)


def _sg_bwd(ab, dyb, gn, wm, wmt, bfull, tm, name):
    t = ab.shape[0]
    nt = t // tm
    sel = (jnp.arange(SG_WIDTH)[:, None] // SG_HD == jnp.arange(CHUNK)[None, :]).astype(F32)

    def body(u_ref, v_ref, dy_ref, gn_ref, wm_ref, wmt_ref, b_ref, sel_ref,
             dup_ref, dvp_ref, dgn_ref, dw_ref, db_ref, dbacc_ref):
        i = pl.program_id(0)

        @pl.when(i == 0)
        def _():
            dgn_ref[...] = jnp.zeros_like(dgn_ref)
            dw_ref[...] = jnp.zeros_like(dw_ref)
            dbacc_ref[...] = jnp.zeros_like(dbacc_ref)

        tril = (lax.broadcasted_iota(jnp.int32, (CHUNK, CHUNK), 0) >= lax.broadcasted_iota(jnp.int32, (CHUNK, CHUNK), 1))
        gn_ = gn_ref[...]
        for c in range(tm // CHUNK):
            rows = slice(c * CHUNK, (c + 1) * CHUNK)
            up, vp, dy = u_ref[rows, :], v_ref[rows, :], dy_ref[rows, :]
            u, v = _gelu(up), _gelu(vp)
            r = _rstd(v)
            vn = (v * r * gn_).astype(BF16)
            sv = b_ref[...]
            for h in range(SG_HEADS):
                sv = sv + jnp.where(_head_lanes(h), _dot(wm_ref[h], vn), 0.0)
            dup_ref[rows, :] = dy * sv * _gelu_grad(up)
            dsv = dy * u
            dbacc_ref[...] += dsv
            dvn = jnp.zeros((CHUNK, SG_WIDTH), F32)
            for h in range(SG_HEADS):
                dsv_h = jnp.where(_head_lanes(h), dsv, 0.0).astype(BF16)
                dvn = dvn + _dot(wmt_ref[h], dsv_h)
                dw_ref[h] += jnp.where(tril, _dot_nt(dsv_h, vn), 0.0)
            dv, dgn = _rms_bwd(v, r, gn_, dvn)
            dgn_ref[...] += dgn
            dvp_ref[rows, :] = dv * _gelu_grad(vp)

        @pl.when(i == nt - 1)
        def _():
            db_ref[...] = jnp.dot(dbacc_ref[...], sel_ref[...], preferred_element_type=F32,
                                  precision=lax.Precision.HIGHEST)

    const = lambda shape: pl.BlockSpec(shape, lambda i: tuple(0 for _ in shape))
    return pl.pallas_call(
        body, name=name, grid=(nt,),
        in_specs=[pl.BlockSpec((tm, SG_WIDTH), lambda i: (i, 1)), pl.BlockSpec((tm, SG_WIDTH), lambda i: (i, 2)),
                  pl.BlockSpec((tm, SG_WIDTH), lambda i: (i, 0)), const((1, SG_WIDTH)),
                  const((SG_HEADS, CHUNK, CHUNK)), const((SG_HEADS, CHUNK, CHUNK)), const((CHUNK, SG_WIDTH)),
                  const((SG_WIDTH, CHUNK))],
        out_specs=[pl.BlockSpec((tm, SG_WIDTH), lambda i: (i, 0)), pl.BlockSpec((tm, SG_WIDTH), lambda i: (i, 0)),
                   const((1, SG_WIDTH)), const((SG_HEADS, CHUNK, CHUNK)), const((CHUNK, CHUNK))],
        out_shape=[jax.ShapeDtypeStruct((t, SG_WIDTH), F32), jax.ShapeDtypeStruct((t, SG_WIDTH), F32),
                   jax.ShapeDtypeStruct((1, SG_WIDTH), F32), jax.ShapeDtypeStruct((SG_HEADS, CHUNK, CHUNK), F32),
                   jax.ShapeDtypeStruct((CHUNK, CHUNK), F32)],
        scratch_shapes=[pltpu.VMEM((CHUNK, SG_WIDTH), F32)],
        compiler_params=_params(("arbitrary",)),
    )(ab, ab, dyb, gn, wm, wmt, bfull, sel)


def _pool_bwd(ab, dya, wbd, scale, tm, name):
    t = ab.shape[0]
    nt = t // tm
    hb = tm // POOL_HALO
    nh = t // POOL_HALO

    def body(cur_ref, prev_ref, dy_ref, dyn_ref, w_ref, s_ref, da_ref, dw_ref, ds_ref):
        i = pl.program_id(0)

        @pl.when(i == 0)
        def _():
            dw_ref[...] = jnp.zeros_like(dw_ref)
            ds_ref[...] = jnp.zeros_like(ds_ref)

        prev = jnp.where(i == 0, 0.0, prev_ref[...])
        d = _pool_diff(cur_ref[...], prev, i * tm).astype(BF16)
        dy = dy_ref[...]
        ds_ref[...] += jnp.sum(dy * _dot(d, w_ref[...]), axis=0, keepdims=True)
        dyn = jnp.where(i == nt - 1, 0.0, dyn_ref[...])
        dys = (jnp.concatenate([dy, dyn], axis=0) * s_ref[...]).astype(BF16)
        dw_ref[...] += _dot_tn(d, dys[:tm])
        dd = _dot_nt(dys, w_ref[...])
        fwd = _pool_window_sums(dd / _pool_count(i * tm, tm + POOL_HALO), True)
        da_ref[...] = fwd[:tm] - dd[:tm]

    return pl.pallas_call(
        body, name=name, grid=(nt,),
        in_specs=[pl.BlockSpec((tm, POOL_WIDTH), lambda i: (i, 0)),
                  pl.BlockSpec((POOL_HALO, POOL_WIDTH), lambda i: (jnp.maximum(i * hb - 1, 0), 0)),
                  pl.BlockSpec((tm, POOL_WIDTH), lambda i: (i, 0)),
                  pl.BlockSpec((POOL_HALO, POOL_WIDTH), lambda i: (jnp.minimum((i + 1) * hb, nh - 1), 0)),
                  pl.BlockSpec((POOL_WIDTH, POOL_WIDTH), lambda i: (0, 0)),
                  pl.BlockSpec((1, POOL_WIDTH), lambda i: (0, 0))],
        out_specs=[pl.BlockSpec((tm, POOL_WIDTH), lambda i: (i, 0)),
                   pl.BlockSpec((POOL_WIDTH, POOL_WIDTH), lambda i: (0, 0)),
                   pl.BlockSpec((1, POOL_WIDTH), lambda i: (0, 0))],
        out_shape=[jax.ShapeDtypeStruct((t, POOL_WIDTH), F32), jax.ShapeDtypeStruct((POOL_WIDTH, POOL_WIDTH), F32),
                   jax.ShapeDtypeStruct((1, POOL_WIDTH), F32)],
        compiler_params=_params(("arbitrary",)),
    )(ab, ab, dya, dya, wbd, scale)


def _inproj_bwd(dx1, x, g, da, dup, dvp, dqt, dk, dv, w, tm, name, ride=None):
    t, d = x.shape
    n = w.shape[1]
    nt = t // tm

    def body(dx1_ref, x_ref, g_ref, da_ref, du_ref, dv_ref, dqt_ref, dk_ref, dvv_ref, w_ref,
             dx_ref, h_ref, dp_ref, dg_ref):
        @pl.when(pl.program_id(0) == 0)
        def _():
            dg_ref[...] = jnp.zeros_like(dg_ref)

        dp = jnp.concatenate([da_ref[...], du_ref[...], dv_ref[...], dqt_ref[...].T, dk_ref[...], dvv_ref[...]],
                             axis=1).astype(BF16)
        dp_ref[...] = dp
        xx = x_ref[...]
        r = _rstd(xx)
        h_ref[...] = (xx * r * g_ref[...]).astype(BF16)
        dx, dg = _rms_bwd(xx, r, g_ref[...], _dot_nt(dp, w_ref[...]))
        dx_ref[...] = dx1_ref[...] + dx
        dg_ref[...] += dg

    row = lambda w_: pl.BlockSpec((tm, w_), lambda i: (i, 0))
    return _call(
        body, name, (nt,),
        [row(d), row(d), pl.BlockSpec((1, d), lambda i: (0, 0)), row(POOL_WIDTH), row(SG_WIDTH),
         row(SG_WIDTH), pl.BlockSpec((SB_WIDTH, tm), lambda i: (0, i)), row(SB_WIDTH), row(SB_WIDTH),
         pl.BlockSpec((d, n), lambda i: (0, 0))],
        [row(d), row(d), row(n), pl.BlockSpec((1, d), lambda i: (0, 0))],
        [jax.ShapeDtypeStruct((t, d), F32), jax.ShapeDtypeStruct((t, d), BF16),
         jax.ShapeDtypeStruct((t, n), BF16), jax.ShapeDtypeStruct((1, d), F32)],
        [], ("arbitrary",), (dx1, x, g, da, dup, dvp, dqt, dk, dv, w), ride)


def _full_w_in(gathered):
    return gathered.transpose(1, 0, 2).reshape(D_MODEL, IN_COLS)


_SMALL_SHAPES = ((D_MODEL,), (4, POOL_GW, POOL_GW), (POOL_WIDTH,), (SG_WIDTH,), (SG_HEADS, CHUNK, CHUNK),
                 (SG_HEADS, CHUNK), (D_MODEL,))
_SMALL_SIZES = tuple(functools.reduce(lambda p, q: p * q, shp) for shp in _SMALL_SHAPES)
_SMALL_ROWS = sum(_SMALL_SIZES) // 128
_NORM1_ROWS = D_MODEL // 128


def _pack_small_layer(arrs):
    return jnp.concatenate([a.reshape(-1) for a in arrs]).reshape(_SMALL_ROWS, 128)


def _pack_small(arrs, final):
    return jnp.concatenate([_pack_small_layer([a[l] for a in arrs]) for l in range(DEPTH)] + [final.reshape(-1, 128)])


def _unpack_small(buf):
    per_layer = []
    for l in range(DEPTH):
        flat, off, outs = buf[l * _SMALL_ROWS:(l + 1) * _SMALL_ROWS].reshape(-1), 0, []
        for shp, size in zip(_SMALL_SHAPES, _SMALL_SIZES):
            outs.append(flat[off:off + size].reshape(shp))
            off += size
        per_layer.append(outs)
    return [jnp.stack([per_layer[l][k] for l in range(DEPTH)]) for k in range(len(_SMALL_SHAPES))] + \
           [buf[DEPTH * _SMALL_ROWS:].reshape(-1)]


def _tiles(t):
    return min(512, t), min(256, t // 4), min(2048, t)


def _layer_fwd(xl, wi, wo, wu, wd, small_w, l, ride_mlp=None):
    n1, pw, ps, sn, sw, sb, n2 = small_w
    tm, tq, _ = _tiles(xl.shape[0])
    wbd = jnp.zeros((4, POOL_GW, 4, POOL_GW), F32)
    for gi in range(4):
        wbd = wbd.at[gi, :, gi, :].set(pw[gi])
    wbd = wbd.reshape(POOL_WIDTH, POOL_WIDTH).astype(BF16)
    wm = (sw * jnp.tril(jnp.ones((CHUNK, CHUNK), F32))).astype(BF16)
    wmt = wm.transpose(0, 2, 1)
    bfull = jnp.repeat(sb.T, SG_HD, axis=1)
    g1, g2, psc, gn = n1[None, :], n2[None, :], ps[None, :], sn[None, :]

    ab, qkv, ktb, vtb = _inproj_fwd(xl, g1, wi, tm, tq, f"inproj_fwd{l}")
    ya = _pool_fwd(ab, wbd, psc, tm, f"pool_fwd{l}")
    yb = _sg_fwd(ab, gn, wm, bfull, tm, f"sg_fwd{l}")
    shards = [w for w in (wo, wu, wd) if w.ndim == 2]
    if shards:
        (yct, cc, nvis), got = _sba_fwd(qkv, vtb, tq, f"sba_fwd{l}", SiblingGather(shards))
        got = iter(got)
        wo, wu, wd = (next(got) if w.ndim == 2 else w for w in (wo, wu, wd))
    else:
        yct, cc, nvis = _sba_fwd(qkv, vtb, tq, f"sba_fwd{l}")
    wo = wo.reshape(D_MODEL, D_MODEL)
    res = _outproj_mlp_fwd(xl, ya, yb, yct, wo, g2, wu, wd, min(MLP_ROWS, xl.shape[0]), f"mlp_fwd{l}", ride_mlp)
    (x1, u, x2), rode = res if ride_mlp is not None else (res, None)
    saved = dict(x=xl, ab=ab, qkv=qkv, ktb=ktb, cc=cc, nvis=nvis, ya=ya, yb=yb, yct=yct, x1=x1, u=u, wi=wi, wo=wo, wu=wu,
                 wd=wd, wbd=wbd, wm=wm, wmt=wmt, bfull=bfull, g1=g1, g2=g2, psc=psc, gn=gn)
    return x2, saved, rode


def _layer_bwd(dx, s, l, ride_mlp=None, exchange=True, gather_small=False):
    tm, tq, tw = _tiles(dx.shape[0])
    ktb = s["ktb"]
    res = _mlp_bwd(dx, s["x1"], s["g2"], s["u"], s["wu"], s["wd"], min(MLP_ROWS, dx.shape[0]), f"mlp_bwd{l}", ride_mlp)
    (dx1, du, r, h2, dx2b, dn2), rode = res if ride_mlp is not None else (res, None)
    dw_up = _matmul_tn(h2, du, D_MODEL, 2 * FF_SHARD, tw, f"dw_up{l}", by_column_block=FF_SHARD)
    dw_down = _matmul_tn(r, dx2b, 1024, D_MODEL, tw, f"dw_down{l}").reshape(N_DEV, FF_SHARD, D_MODEL)
    dya, dyb, dyc, dyct, dx1b = _outproj_bwd(dx1, s["wo"], tm, f"outproj_bwd{l}")
    dw_out = jnp.concatenate([
        _matmul_tn(jnp.concatenate([s["ya"], s["yb"]], axis=1), dx1b, POOL_WIDTH + SG_WIDTH, D_MODEL, tw, f"dw_out_ab{l}"),
        _matmul_tn(s["yct"], dx1b, SB_WIDTH, D_MODEL, tw, f"dw_out_c{l}", a_transposed=True)]
    ).reshape(N_DEV, OUT_SHARD, D_MODEL)
    if exchange:
        (dqt, dk, dv), (dw_out, dw_up, dw_down) = _sba_bwd(s["qkv"], ktb, dyc, dyct, s["cc"], s["nvis"], tq, f"sba_bwd{l}",
                                                           Exchange([dw_out, dw_up, dw_down], True))
    else:
        dqt, dk, dv = _sba_bwd(s["qkv"], ktb, dyc, dyct, s["cc"], s["nvis"], tq, f"sba_bwd{l}")
    dup, dvp, dgn, dwm, dbm = _sg_bwd(s["ab"], dyb, s["gn"], s["wm"], s["wmt"], s["bfull"], tm, f"sg_bwd{l}")
    da, dwbd, dpsc = _pool_bwd(s["ab"], dya, s["wbd"], s["psc"], tm, f"pool_bwd{l}")
    dpw = jnp.stack([dwbd[gi * POOL_GW:(gi + 1) * POOL_GW, gi * POOL_GW:(gi + 1) * POOL_GW] for gi in range(4)])
    small = _pack_small_layer([jnp.zeros((D_MODEL,), F32), dpw, dpsc[0], dgn[0], dwm, dbm[:, :SG_HEADS].T, dn2[0]])[_NORM1_ROWS:]
    res = _inproj_bwd(dx1, s["x"], s["g1"], da, dup, dvp, dqt, dk, dv, s["wi"], tm, f"inproj_bwd{l}",
                      Exchange([small], False) if gather_small else None)
    (dx, h1, dproj, dn1), small = (res[0], res[1][0]) if gather_small else (res, small)
    dw_in = _matmul_tn(h1, dproj, D_MODEL, IN_COLS // 3, tw, f"dw_in{l}")
    dw_in = dw_in.reshape(D_MODEL, N_DEV, IN_SHARD).transpose(1, 0, 2)
    return dx, (dw_in, dw_out, dw_up, dw_down), (dn1.reshape(_NORM1_ROWS, 128), small), rode


def kernel(x, norm1, w_in, pool_w, pool_scale, sg_norm, sg_w, sg_b, w_out, norm2, w_up, w_down, final_norm, loss_target, m_norm1, m_w_in, m_pool_w, m_pool_scale, m_sg_norm, m_sg_w, m_sg_b, m_w_out, m_norm2, m_w_up, m_w_down, m_final_norm, v_norm1, v_w_in, v_pool_w, v_pool_scale, v_sg_norm, v_sg_w, v_sg_b, v_w_out, v_norm2, v_w_up, v_w_down, v_final_norm):
    t = x.shape[1]
    tm = _tiles(t)[0]
    small_w = (norm1, pool_w, pool_scale, sg_norm, sg_w, sg_b, norm2)
    big_w = (w_in, w_out, w_up, w_down)
    big_m = (m_w_in, m_w_out, m_w_up, m_w_down)
    big_v = (v_w_in, v_w_out, v_w_up, v_w_down)
    shards = [[w[l].astype(BF16) for w in big_w] for l in range(DEPTH)]

    wi0 = _full_w_in(SiblingGather(shards[0][:1]).alone("gather_w_in0")[0])
    sh_in1, sh_out1, sh_up1, sh_down1 = shards[1]
    x1, s0, (wi1, wo1, wd1) = _layer_fwd(x.reshape(t, D_MODEL), wi0, *shards[0][1:], tuple(w[0] for w in small_w), 0,
                                         ride_mlp=Exchange([sh_in1, sh_out1, sh_down1], False))
    x2, s1, _ = _layer_fwd(x1, _full_w_in(wi1), wo1, sh_up1, wd1, tuple(w[1] for w in small_w), 1)
    loss_local, dx, dfinal = _loss_grad(x2, final_norm[None, :], loss_target.reshape(t, D_MODEL), tm, "loss_grad")
    loss = lax.psum(loss_local[0, 0], MESH_AXES)

    dx, parts1, small1, _ = _layer_bwd(dx, s1, 1)
    early = jnp.concatenate(list(small1) + [dfinal.reshape(-1, 128)])
    dx, parts0, (dn1, small0), (recv_in1, early) = _layer_bwd(
        dx, s0, 0, ride_mlp=Exchange([parts1[0], early], [True, False]), gather_small=True)
    grad_x = dx.reshape(x.shape)
    recv_in0, dn1 = _exchange([parts0[0], dn1], [True, False], "scatter_w_in0_gather_norm1_0")
    small_all = jnp.concatenate([dn1, small0, early], axis=1)
    received = [[recv_in0] + list(parts0[1:]), [recv_in1] + list(parts1[1:])]

    big = [None] * 4
    for l in reversed(range(DEPTH)):
        for k in range(4):
            big[k] = _reduce_adamw(received[l][k], big_w[k], big_m[k], big_v[k], l, big[k], f"adamw{k}_{l}")

    sm = _reduce_adamw(
        small_all,
        _pack_small(small_w, final_norm)[None],
        _pack_small([m_norm1, m_pool_w, m_pool_scale, m_sg_norm, m_sg_w, m_sg_b, m_norm2], m_final_norm)[None],
        _pack_small([v_norm1, v_pool_w, v_pool_scale, v_sg_norm, v_sg_w, v_sg_b, v_norm2], v_final_norm)[None],
        0, None, "adamw_replicated")

    out = [loss, grad_x]
    for k in range(4):
        n1, pw, ps, sn, sw, sb, n2, fn = _unpack_small(sm[k][0])
        out += [n1, big[0][k], pw, ps, sn, sw, sb, big[1][k], n2, big[2][k], big[3][k], fn]
    return tuple(out)
```

```python
import functools

import jax
import jax.numpy as jnp
from jax import lax
from jax.experimental import pallas as pl
from jax.experimental.pallas import tpu as pltpu

F32 = jnp.float32
BF16 = jnp.bfloat16

D_MODEL = 1024
DEPTH = 2
POOL_WIDTH = 256
SG_WIDTH = 256
SB_WIDTH = 512
POOL_WINDOWS = (2, 4, 8, 16)
POOL_GW = 64
POOL_HALO = 16
CHUNK = 128
SG_HEADS = 4
SG_HD = 64
SB_HD = 64
SB_PAIRS = SB_WIDTH // 128
AB_COLS = POOL_WIDTH + 2 * SG_WIDTH
IN_COLS = AB_COLS + 3 * SB_WIDTH
D_FF = 4096
EPS = 1e-6
N_DEV = 8
FF_SHARD = D_FF // N_DEV
IN_SHARD = IN_COLS // N_DEV
OUT_SHARD = D_MODEL // N_DEV
ADAM_LR = 0.001
ADAM_B1 = 0.9
ADAM_B2 = 0.999
ADAM_EPS = 1e-08
ADAM_WD = 0.01
ADAM_STEP = 10
VMEM_LIMIT = 56 * 1024 * 1024
MLP_ROWS = 1024
MESH_AXES = ("x", "y", "c")


def _dot(a, b):
    return jnp.dot(a, b, preferred_element_type=F32)


def _dot_nt(a, b):
    return lax.dot_general(a, b, (((1,), (1,)), ((), ())), preferred_element_type=F32)


def _dot_tn(a, b):
    return lax.dot_general(a, b, (((0,), (0,)), ((), ())), preferred_element_type=F32)


def _rstd(x):
    return lax.rsqrt(jnp.mean(x * x, axis=-1, keepdims=True) + EPS)


def _rms_bwd(x, r, g, dh):
    gq = dh * g
    dx = r * gq - x * (r * r * r) * jnp.mean(gq * x, axis=-1, keepdims=True)
    dg = jnp.sum(dh * x * r, axis=0, keepdims=True)
    return dx, dg


def _params(sem=None):
    kw = dict(vmem_limit_bytes=VMEM_LIMIT)
    if sem is not None:
        kw["dimension_semantics"] = sem
    return pltpu.CompilerParams(**kw)


def _row_tile(rows, cap):
    best = 8
    for t in range(8, min(rows, cap) + 1, 8):
        if rows % t == 0:
            best = t
    return best


def _peer(k):
    x, y, c = lax.axis_index("x"), lax.axis_index("y"), lax.axis_index("c")
    return (1 - x if k & 4 else x, 1 - y if k & 2 else y, 1 - c if k & 1 else c)


def _my_index():
    return 4 * lax.axis_index("x") + 2 * lax.axis_index("y") + lax.axis_index("c")


class Exchange:
    def __init__(self, arrs, scatter):
        self.arrs = list(arrs)
        self.scatter = list(scatter) if isinstance(scatter, (list, tuple)) else [scatter] * len(self.arrs)
        self.n = len(self.arrs)
        self.any_specs = [pl.BlockSpec(memory_space=pl.ANY)] * self.n
        self.out_shape = [jax.ShapeDtypeStruct((N_DEV,) + a.shape[-2:], a.dtype) for a in self.arrs]
        self.sems = [pltpu.SemaphoreType.DMA((self.n, N_DEV - 1)), pltpu.SemaphoreType.DMA((self.n, N_DEV - 1)),
                     pltpu.SemaphoreType.DMA((self.n,))]

    def _copies(self, ins, outs, sems):
        send_sems, recv_sems, local_sems = sems
        me = _my_index()
        local, remote = [], []
        for a in range(self.n):
            sc = self.scatter[a]
            local.append(pltpu.make_async_copy(ins[a].at[me] if sc else ins[a], outs[a].at[me], local_sems.at[a]))
            for k in range(1, N_DEV):
                px, py, pc = _peer(k)
                remote.append(pltpu.make_async_remote_copy(
                    src_ref=ins[a].at[4 * px + 2 * py + pc] if sc else ins[a], dst_ref=outs[a].at[me],
                    send_sem=send_sems.at[a, k - 1], recv_sem=recv_sems.at[a, k - 1],
                    device_id=(px, py, pc), device_id_type=pl.DeviceIdType.MESH))
        return local, remote

    def start(self, ins, outs, sems):
        local, remote = self._copies(ins, outs, sems)
        for cp in local + remote:
            cp.start()

    def middle(self, ins, outs, sems):
        pass

    def wait(self, ins, outs, sems):
        local, remote = self._copies(ins, outs, sems)
        for cp in remote:
            cp.wait_recv()
        for cp in remote:
            cp.wait_send()
        for cp in local:
            cp.wait()

    def alone(self, name):
        n = self.n

        def body(*refs):
            ins, outs, sems = refs[:n], refs[n:2 * n], refs[2 * n:]
            self.start(ins, outs, sems)
            self.middle(ins, outs, sems)
            self.wait(ins, outs, sems)

        return pl.pallas_call(body, name=name, out_shape=self.out_shape, in_specs=self.any_specs,
                              out_specs=self.any_specs, scratch_shapes=self.sems)(*self.arrs)


class SiblingGather(Exchange):
    def __init__(self, arrs):
        super().__init__(arrs, False)

    def _plan(self, ins, outs, sems):
        send_sems, recv_sems, local_sems = sems
        x, y, core = lax.axis_index("x"), lax.axis_index("y"), lax.axis_index("c")
        me, sibling = (x, y, core), (x, y, 1 - core)
        chips = [(1 - x, y), (x, 1 - y), (1 - x, 1 - y)]
        plan = dict(local=[], first=[], arrived=[], passed=[], late=[])
        for a in range(self.n):
            slot = lambda p, a=a: outs[a].at[4 * p[0] + 2 * p[1] + p[2]]

            def copy(k, block, to, src=None, a=a, slot=slot):
                return pltpu.make_async_remote_copy(
                    src_ref=slot(block) if src is None else src, dst_ref=slot(block), send_sem=send_sems.at[a, k],
                    recv_sem=recv_sems.at[a, k], device_id=to, device_id_type=pl.DeviceIdType.MESH)

            plan["local"].append(pltpu.make_async_copy(ins[a], slot(me), local_sems.at[a]))
            plan["first"] += [copy(0, me, sibling, src=ins[a])] + [copy(1 + j, me, (*ch, core), src=ins[a])
                                                                  for j, ch in enumerate(chips)]
            plan["arrived"] += [copy(1 + j, (*ch, core), me) for j, ch in enumerate(chips)]
            plan["passed"] += [copy(4 + j, (*ch, core), sibling) for j, ch in enumerate(chips)]
            plan["late"] += [copy(0, sibling, me)] + [copy(4 + j, (*ch, 1 - core), me) for j, ch in enumerate(chips)]
        return plan

    def start(self, ins, outs, sems):
        plan = self._plan(ins, outs, sems)
        for cp in plan["local"] + plan["first"]:
            cp.start()

    def middle(self, ins, outs, sems):
        plan = self._plan(ins, outs, sems)
        for arrived, passed in zip(plan["arrived"], plan["passed"]):
            arrived.wait_recv()
            passed.start()

    def wait(self, ins, outs, sems):
        plan = self._plan(ins, outs, sems)
        for cp in plan["late"]:
            cp.wait_recv()
        for cp in plan["first"] + plan["passed"]:
            cp.wait_send()
        for cp in plan["local"]:
            cp.wait()


def _exchange(arrs, scatter, name):
    return Exchange(arrs, scatter).alone(name)


def _call(body, name, grid, in_specs, out_specs, out_shape, scratch_shapes, semantics, args, ride=None):
    if ride is None:
        return pl.pallas_call(body, name=name, grid=grid, in_specs=in_specs, out_specs=out_specs, out_shape=out_shape,
                              scratch_shapes=scratch_shapes, compiler_params=_params(semantics))(*args)
    single = not isinstance(out_shape, (list, tuple))
    out_specs, out_shape = ([out_specs], [out_shape]) if single else (list(out_specs), list(out_shape))
    n_in, n_out, n_scr, n = len(in_specs), len(out_specs), len(scratch_shapes), ride.n

    def riding(*refs):
        ins, cins = refs[:n_in], refs[n_in:n_in + n]
        outs, couts = refs[n_in + n:n_in + n + n_out], refs[n_in + n + n_out:n_in + 2 * n + n_out]
        scr, sems = refs[n_in + 2 * n + n_out:n_in + 2 * n + n_out + n_scr], refs[n_in + 2 * n + n_out + n_scr:]
        step = functools.reduce(lambda s, a: s * grid[a] + pl.program_id(a), range(len(grid)), 0)
        steps = functools.reduce(lambda p, q: p * q, grid)

        @pl.when(step == 0)
        def _():
            ride.start(cins, couts, sems)

        @pl.when(step == (3 * steps) // 4)
        def _():
            ride.middle(cins, couts, sems)

        body(*ins, *outs, *scr)

        @pl.when(step == steps - 1)
        def _():
            ride.wait(cins, couts, sems)

    res = pl.pallas_call(
        riding, name=name, grid=grid, in_specs=list(in_specs) + ride.any_specs, out_specs=out_specs + ride.any_specs,
        out_shape=out_shape + ride.out_shape, scratch_shapes=list(scratch_shapes) + ride.sems,
        compiler_params=_params(("arbitrary",) * len(grid)))(*args, *ride.arrs)
    own = res[0] if single else list(res[:n_out])
    return own, list(res[n_out:])


def _reduce_adamw(parts, w, m, v, l, prev, name):
    _, rows, n = parts.shape
    tr = _row_tile(rows, max(8, (1 << 18) // n))
    c1 = 1.0 - ADAM_B1 ** ADAM_STEP
    c2 = 1.0 - ADAM_B2 ** ADAM_STEP

    def body(p_ref, w_ref, m_ref, v_ref, *rest):
        g_ref, d_ref, nm_ref, nv_ref = rest[-4:]
        g = p_ref[0].astype(F32)
        for s in range(1, N_DEV):
            g = g + p_ref[s].astype(F32)
        nm = ADAM_B1 * m_ref[...] + (1.0 - ADAM_B1) * g
        nv = ADAM_B2 * v_ref[...] + (1.0 - ADAM_B2) * (g * g)
        m_hat = nm / c1
        v_hat = nv / c2
        g_ref[...] = g
        d_ref[...] = -ADAM_LR * (m_hat / (jnp.sqrt(v_hat) + ADAM_EPS) + ADAM_WD * w_ref[...])
        nm_ref[...] = nm
        nv_ref[...] = nv

    blk = pl.BlockSpec((None, tr, n), lambda i: (l, i, 0))
    out = jax.ShapeDtypeStruct(w.shape, F32)
    prev = list(prev) if prev is not None else []
    return pl.pallas_call(
        body, name=name, grid=(rows // tr,),
        in_specs=[pl.BlockSpec((N_DEV, tr, n), lambda i: (0, i, 0)), blk, blk, blk] + [pl.BlockSpec(memory_space=pl.ANY)] * len(prev),
        out_specs=[blk, blk, blk, blk], out_shape=[out, out, out, out],
        input_output_aliases={4 + k: k for k in range(len(prev))},
        compiler_params=_params(("parallel",)),
    )(parts, w, m, v, *prev)


def _inproj_fwd(x, g, w, tm, tq, name):
    t, d = x.shape
    n = w.shape[1]
    nb = t // tq
    per = tm // tq

    def body(x_ref, g_ref, w_ref, ab_ref, qkv_ref, kt_ref, vt_ref):
        xx = x_ref[...]
        h = (xx * _rstd(xx) * g_ref[...]).astype(BF16)
        ab_ref[...] = _dot(h, w_ref[:, :AB_COLS])
        qkv = _dot(h, w_ref[:, AB_COLS:])
        qkv_ref[...] = qkv.astype(BF16)
        for which, out_ref in ((1, kt_ref), (2, vt_ref)):
            for p in range(SB_PAIRS):
                for b in range(per):
                    cols = which * SB_WIDTH + p * 128
                    out_ref[p, b] = qkv[b * tq:(b + 1) * tq, cols:cols + 128].T.astype(BF16)

    tb = pl.BlockSpec((SB_PAIRS, per, 128, tq), lambda i: (0, i, 0, 0))
    tshape = jax.ShapeDtypeStruct((SB_PAIRS, nb, 128, tq), BF16)
    return _call(
        body, name, (t // tm,),
        [pl.BlockSpec((tm, d), lambda i: (i, 0)), pl.BlockSpec((1, d), lambda i: (0, 0)),
         pl.BlockSpec((d, n), lambda i: (0, 0))],
        [pl.BlockSpec((tm, AB_COLS), lambda i: (i, 0)), pl.BlockSpec((tm, n - AB_COLS), lambda i: (i, 0)), tb, tb],
        [jax.ShapeDtypeStruct((t, AB_COLS), F32), jax.ShapeDtypeStruct((t, n - AB_COLS), BF16), tshape, tshape],
        [], ("parallel",), (x, g, w))


def _pool_window_sums(xx, forward):
    n = xx.shape[0]
    sh = (lambda k: n - k) if forward else (lambda k: k)
    s2 = xx + pltpu.roll(xx, sh(1), 0)
    s4 = s2 + pltpu.roll(s2, sh(2), 0)
    s8 = s4 + pltpu.roll(s4, sh(4), 0)
    s16 = s8 + pltpu.roll(s8, sh(8), 0)
    grp = lax.broadcasted_iota(jnp.int32, (1, POOL_WIDTH), 1) // POOL_GW
    return jnp.where(grp == 0, s2, jnp.where(grp == 1, s4, jnp.where(grp == 2, s8, s16)))


def _pool_count(t0, rows):
    grp = lax.broadcasted_iota(jnp.int32, (1, POOL_WIDTH), 1) // POOL_GW
    win = jnp.where(grp == 0, 2, jnp.where(grp == 1, 4, jnp.where(grp == 2, 8, 16)))
    tt = t0 + lax.broadcasted_iota(jnp.int32, (rows, 1), 0)
    return jnp.minimum(tt + 1, win).astype(F32)


def _pool_diff(cur, prev, t0):
    tm = cur.shape[0]
    sums = _pool_window_sums(jnp.concatenate([prev, cur], axis=0), False)[POOL_HALO:]
    return sums / _pool_count(t0, tm) - cur


def _pool_fwd(ab, wbd, scale, tm, name):
    t = ab.shape[0]
    hb = tm // POOL_HALO

    def body(cur_ref, prev_ref, w_ref, s_ref, y_ref):
        i = pl.program_id(0)
        prev = jnp.where(i == 0, 0.0, prev_ref[...])
        d = _pool_diff(cur_ref[...], prev, i * tm).astype(BF16)
        y_ref[...] = (_dot(d, w_ref[...]) * s_ref[...]).astype(BF16)

    return pl.pallas_call(
        body, name=name, grid=(t // tm,),
        in_specs=[pl.BlockSpec((tm, POOL_WIDTH), lambda i: (i, 0)),
                  pl.BlockSpec((POOL_HALO, POOL_WIDTH), lambda i: (jnp.maximum(i * hb - 1, 0), 0)),
                  pl.BlockSpec((POOL_WIDTH, POOL_WIDTH), lambda i: (0, 0)),
                  pl.BlockSpec((1, POOL_WIDTH), lambda i: (0, 0))],
        out_specs=pl.BlockSpec((tm, POOL_WIDTH), lambda i: (i, 0)),
        out_shape=jax.ShapeDtypeStruct((t, POOL_WIDTH), BF16),
        compiler_params=_params(("parallel",)),
    )(ab, ab, wbd, scale)


_GELU_K = 0.7978845608028654
_GELU_A = 0.044715


def _gelu(x):
    return 0.5 * x * (1.0 + jnp.tanh(_GELU_K * (x + _GELU_A * (x * x * x))))


def _gelu_grad(x):
    th = jnp.tanh(_GELU_K * (x + _GELU_A * (x * x * x)))
    return 0.5 * (1.0 + th) + 0.5 * x * (1.0 - th * th) * (_GELU_K * (1.0 + 3.0 * _GELU_A * (x * x)))


def _head_lanes(h):
    return lax.broadcasted_iota(jnp.int32, (1, SG_WIDTH), 1) // SG_HD == h


def _sg_fwd(ab, gn, wm, bfull, tm, name):
    t = ab.shape[0]

    def body(u_ref, v_ref, gn_ref, wm_ref, b_ref, y_ref):
        v = _gelu(v_ref[...])
        vn = (v * _rstd(v) * gn_ref[...]).astype(BF16)
        for c in range(tm // CHUNK):
            rows = slice(c * CHUNK, (c + 1) * CHUNK)
            vc = vn[rows]
            sv = b_ref[...]
            for h in range(SG_HEADS):
                sv = sv + jnp.where(_head_lanes(h), _dot(wm_ref[h], vc), 0.0)
            y_ref[rows, :] = (_gelu(u_ref[rows, :]) * sv).astype(BF16)

    return pl.pallas_call(
        body, name=name, grid=(t // tm,),
        in_specs=[pl.BlockSpec((tm, SG_WIDTH), lambda i: (i, 1)), pl.BlockSpec((tm, SG_WIDTH), lambda i: (i, 2)),
                  pl.BlockSpec((1, SG_WIDTH), lambda i: (0, 0)),
                  pl.BlockSpec((SG_HEADS, CHUNK, CHUNK), lambda i: (0, 0, 0)),
                  pl.BlockSpec((CHUNK, SG_WIDTH), lambda i: (0, 0))],
        out_specs=pl.BlockSpec((tm, SG_WIDTH), lambda i: (i, 0)),
        out_shape=jax.ShapeDtypeStruct((t, SG_WIDTH), BF16),
        compiler_params=_params(("parallel",)),
    )(ab, ab, gn, wm, bfull)


LOG2E = 1.4426950408889634
SB_SCALE = 0.125 * LOG2E
SB_DEAD_LOG2 = 152.0
SB_QUERY_BLOCKS_PER_STEP = 2


def _log2_sigmoids(y):
    neg_abs = lax.bitcast_convert_type(lax.bitcast_convert_type(y, jnp.uint32) | jnp.uint32(0x80000000), F32)
    lb = jnp.minimum(y, 0.0) - jnp.log(1.0 + jnp.exp2(neg_abs)) * LOG2E
    return lb, lb - y


def _split(x):
    hi = x.astype(BF16)
    return hi, (x - hi.astype(F32)).astype(BF16)


def _tri_dot(tri, x):
    hi, lo = _split(x)
    return _dot(tri, hi) + _dot(tri, lo)


def _sba_fwd(qkv, vtb, tq, name, ride=None):
    t = qkv.shape[0]
    nb = t // tq
    qb = SB_QUERY_BLOCKS_PER_STEP
    upper = (jnp.arange(tq)[None, :] > jnp.arange(tq)[:, None]).astype(BF16)

    def body(q_ref, k_ref, vt_ref, up_ref, ot_ref, c_ref, n_ref):
        pair, step = pl.program_id(0), pl.program_id(1)
        refs = (q_ref, k_ref, vt_ref, up_ref, ot_ref, c_ref, n_ref)
        subs = list(range(qb))

        @pl.when(step == 0)
        def _():
            for sub in subs:
                query_blocks([sub], pair, step, *refs)

        @pl.when(step > 0)
        def _():
            query_blocks(subs, pair, step, *refs)

    def query_blocks(subs, pair, step, q_ref, k_ref, vt_ref, up_ref, ot_ref, c_ref, n_ref):
        index = {sub: step * qb + sub for sub in subs}
        mine = {sub: slice(sub * tq, (sub + 1) * tq) for sub in subs}
        up = up_ref[...]
        lane_head = lax.broadcasted_iota(jnp.int32, (1, 128), 1) // SB_HD
        sub_head = lax.broadcasted_iota(jnp.int32, (128, 1), 0) // SB_HD
        causal = (lax.broadcasted_iota(jnp.int32, (tq, tq), 0) < lax.broadcasted_iota(jnp.int32, (tq, tq), 1))
        qh = {}
        for sub in subs:
            q = q_ref[mine[sub], :]
            qh[sub] = [jnp.where(lane_head == h, q, jnp.zeros_like(q)) for h in range(2)]

        def blocks(work, carry):
            cs = {sub: list(carry[sub][:2]) for sub in carry}
            acc = {sub: carry[sub][2] for sub in carry}
            kj = [k_ref[pl.ds(pl.multiple_of(j * tq, tq), tq), :] for _, j, _ in work]
            vt = [vt_ref[j] for _, j, _ in work]
            chains = [(w, h) for w in range(len(work)) for h in range(2)]
            z = [_dot_nt(kj[w], qh[work[w][0]][h]) for w, h in chains]
            ls = [_log2_sigmoids(zz * SB_SCALE) for zz in z]
            lb = [x[0] for x in ls]
            l1 = [jnp.where(causal, x[1], 0.0) if work[w][2] else x[1] for x, (w, h) in zip(ls, chains)]
            after = [_tri_dot(up, x) for x in l1]
            a = []
            for n, (w, h) in enumerate(chains):
                sub, j, diag = work[w]
                c_ref[h, pl.ds(j, 1), mine[sub]] = cs[sub][h]
                an = jnp.exp2(lb[n] + after[n] + cs[sub][h])
                a.append(jnp.where(causal, an, 0.0) if diag else an)
                cs[sub][h] = cs[sub][h] + after[n][0:1, :] + l1[n][0:1, :]
            alive = {sub: (jnp.max(jnp.maximum(*cs[sub])) > -SB_DEAD_LOG2).astype(jnp.int32) for sub in cs}
            for n, (w, h) in enumerate(chains):
                sub = work[w][0]
                acc[sub] = acc[sub] + _dot(jnp.where(sub_head == h, vt[w], jnp.zeros_like(vt[w])), a[n].astype(BF16))
            return {sub: (alive[sub], cs[sub][0], cs[sub][1], acc[sub]) for sub in cs}

        zero = jnp.zeros((1, tq), F32)
        start = {sub: (zero, zero, jnp.zeros((128, tq), F32)) for sub in subs}
        both = lambda: blocks([(sub, index[sub] - b, b == 0) for sub in subs for b in range(2)], start)
        if len(subs) == 1:
            joint = lax.cond(index[subs[0]] > 0, both, lambda: blocks([(subs[0], index[subs[0]], True)], start))
        else:
            joint = both()
        for sub in subs:
            i = index[sub]

            def left(state, sub=sub, i=i):
                s, _, c0, c1, acc = state
                return (s + 1,) + blocks([(sub, i - 1 - s, False)], {sub: (c0, c1, acc)})[sub]

            state = lax.while_loop(lambda st, i=i: (st[0] < i) & (st[1] > 0), left, (jnp.minimum(i, 1),) + tuple(joint[sub]))
            ot_ref[:, mine[sub]] = state[4].astype(BF16)
            n_ref[pair, i] = (state[0] + 1).astype(F32)

    return _call(
        body, name, (SB_PAIRS, nb // qb),
        [pl.BlockSpec((qb * tq, 128), lambda p, i: (i, p)),
         pl.BlockSpec((t, 128), lambda p, i: (0, SB_PAIRS + p)),
         pl.BlockSpec((None, nb, 128, tq), lambda p, i: (p, 0, 0, 0)),
         pl.BlockSpec((tq, tq), lambda p, i: (0, 0))],
        [pl.BlockSpec((128, qb * tq), lambda p, i: (p, i)),
         pl.BlockSpec((2, nb, qb * tq), lambda p, i: (p, 0, i)),
         pl.BlockSpec(memory_space=pltpu.SMEM)],
        [jax.ShapeDtypeStruct((SB_WIDTH, t), BF16),
         jax.ShapeDtypeStruct((2 * SB_PAIRS, nb, t), F32),
         jax.ShapeDtypeStruct((SB_PAIRS, nb), F32)],
        [], ("arbitrary", "arbitrary"), (qkv, qkv, vtb, upper), ride)


def _outproj_mlp_fwd(x, ya, yb, yct, wo, g2, wup, wdn, tm, name, ride=None):
    t, d = x.shape
    nf = wup.shape[0]

    def body(x_ref, ya_ref, yb_ref, yct_ref, wo_ref, g_ref, wu_ref, wd_ref, x1_ref, u_ref, x2_ref, h_ref, acc_ref):
        j = pl.program_id(1)

        @pl.when(j == 0)
        def _():
            x1 = (x_ref[...] + _dot(ya_ref[...], wo_ref[0:POOL_WIDTH, :])
                  + _dot(yb_ref[...], wo_ref[POOL_WIDTH:POOL_WIDTH + SG_WIDTH, :])
                  + _dot_tn(yct_ref[...], wo_ref[POOL_WIDTH + SG_WIDTH:, :]))
            x1_ref[...] = x1
            h_ref[...] = (x1 * _rstd(x1) * g_ref[...]).astype(BF16)
            acc_ref[...] = x1

        u = _dot(h_ref[...], wu_ref[...])
        u_ref[...] = u
        r = jnp.maximum(u, 0.0)
        acc_ref[...] += _dot((r * r).astype(BF16), wd_ref[...])

        @pl.when(j == nf - 1)
        def _():
            x2_ref[...] = acc_ref[...]

    row = lambda w: pl.BlockSpec((tm, w), lambda i, j: (i, 0))
    return _call(
        body, name, (t // tm, nf),
        [row(d), row(POOL_WIDTH), row(SG_WIDTH), pl.BlockSpec((SB_WIDTH, tm), lambda i, j: (0, i)),
         pl.BlockSpec((d, d), lambda i, j: (0, 0)), pl.BlockSpec((1, d), lambda i, j: (0, 0)),
         pl.BlockSpec((None, d, FF_SHARD), lambda i, j: (j, 0, 0)),
         pl.BlockSpec((None, FF_SHARD, d), lambda i, j: (j, 0, 0))],
        [row(d), pl.BlockSpec((tm, FF_SHARD), lambda i, j: (i, j)), row(d)],
        [jax.ShapeDtypeStruct((t, d), F32), jax.ShapeDtypeStruct((t, nf * FF_SHARD), F32),
         jax.ShapeDtypeStruct((t, d), F32)],
        [pltpu.VMEM((tm, d), BF16), pltpu.VMEM((tm, d), F32)],
        ("parallel", "arbitrary"), (x, ya, yb, yct, wo, g2, wup, wdn), ride)


def _loss_grad(x, g, target, tm, name):
    t, d = x.shape
    nt = t // tm

    def body(x_ref, g_ref, t_ref, loss_ref, dx_ref, dg_ref, sq_ref):
        i = pl.program_id(0)

        @pl.when(i == 0)
        def _():
            sq_ref[...] = jnp.zeros_like(sq_ref)
            dg_ref[...] = jnp.zeros_like(dg_ref)

        xx = x_ref[...]
        r = _rstd(xx)
        err = xx * r * g_ref[...] - t_ref[...]
        sq_ref[...] += jnp.sum(err * err, axis=0, keepdims=True)
        dx, dg = _rms_bwd(xx, r, g_ref[...], err * (1.0 / d))
        dx_ref[...] = dx
        dg_ref[...] += dg

        @pl.when(i == nt - 1)
        def _():
            loss_ref[...] = jnp.sum(sq_ref[...], axis=1, keepdims=True) * (0.5 / d)

    return pl.pallas_call(
        body, name=name, grid=(nt,),
        in_specs=[pl.BlockSpec((tm, d), lambda i: (i, 0)), pl.BlockSpec((1, d), lambda i: (0, 0)),
                  pl.BlockSpec((tm, d), lambda i: (i, 0))],
        out_specs=[pl.BlockSpec((1, 1), lambda i: (0, 0)), pl.BlockSpec((tm, d), lambda i: (i, 0)),
                   pl.BlockSpec((1, d), lambda i: (0, 0))],
        out_shape=[jax.ShapeDtypeStruct((1, 1), F32), jax.ShapeDtypeStruct((t, d), F32),
                   jax.ShapeDtypeStruct((1, d), F32)],
        scratch_shapes=[pltpu.VMEM((1, d), F32)],
        compiler_params=_params(("arbitrary",)),
    )(x, g, target)


def _mlp_bwd(dx2, x1, g2, u, wup, wdn, tm, name, ride=None):
    t, d = dx2.shape
    nf = wup.shape[0]
    nt = t // tm

    def body(dx2_ref, x1_ref, g_ref, u_ref, wu_ref, wd_ref, dx1_ref, du_ref, r_ref, h_ref, dxb_ref, dg_ref, acc_ref):
        i, j = pl.program_id(0), pl.program_id(1)

        @pl.when(j == 0)
        def _():
            x1 = x1_ref[...]
            h_ref[...] = (x1 * _rstd(x1) * g_ref[...]).astype(BF16)
            dxb_ref[...] = dx2_ref[...].astype(BF16)
            acc_ref[...] = jnp.zeros_like(acc_ref)

        @pl.when((i == 0) & (j == 0))
        def _():
            dg_ref[...] = jnp.zeros_like(dg_ref)

        dr = _dot_nt(dxb_ref[...], wd_ref[...])
        ru = jnp.maximum(u_ref[...], 0.0)
        du = (dr * (2.0 * ru)).astype(BF16)
        du_ref[...] = du
        r_ref[...] = (ru * ru).astype(BF16)
        acc_ref[...] += _dot_nt(du, wu_ref[...])

        @pl.when(j == nf - 1)
        def _():
            x1 = x1_ref[...]
            dx, dg = _rms_bwd(x1, _rstd(x1), g_ref[...], acc_ref[...])
            dx1_ref[...] = dx2_ref[...] + dx
            dg_ref[...] += dg

    row = lambda w: pl.BlockSpec((tm, w), lambda i, j: (i, 0))
    col = pl.BlockSpec((tm, FF_SHARD), lambda i, j: (i, j))
    return _call(
        body, name, (nt, nf),
        [row(d), row(d), pl.BlockSpec((1, d), lambda i, j: (0, 0)), col,
         pl.BlockSpec((None, d, FF_SHARD), lambda i, j: (j, 0, 0)),
         pl.BlockSpec((None, FF_SHARD, d), lambda i, j: (j, 0, 0))],
        [row(d), col, col, row(d), row(d), pl.BlockSpec((1, d), lambda i, j: (0, 0))],
        [jax.ShapeDtypeStruct((t, d), F32), jax.ShapeDtypeStruct((t, nf * FF_SHARD), BF16),
         jax.ShapeDtypeStruct((t, nf * FF_SHARD), BF16), jax.ShapeDtypeStruct((t, d), BF16),
         jax.ShapeDtypeStruct((t, d), BF16), jax.ShapeDtypeStruct((1, d), F32)],
        [pltpu.VMEM((tm, d), F32)], ("arbitrary", "arbitrary"), (dx2, x1, g2, u, wup, wdn), ride)


def _matmul_tn(a, b, bm, bn, bt, name, by_column_block=0, a_transposed=False):
    m, t = a.shape if a_transposed else a.shape[::-1]
    n = b.shape[1]
    nk = t // bt

    def body(a_ref, b_ref, o_ref, acc_ref):
        k = pl.program_id(2)

        @pl.when(k == 0)
        def _():
            acc_ref[...] = jnp.zeros_like(acc_ref)

        acc_ref[...] += _dot(a_ref[...], b_ref[...]) if a_transposed else _dot_tn(a_ref[...], b_ref[...])

        @pl.when(k == nk - 1)
        def _():
            if by_column_block:
                for c in range(bn // by_column_block):
                    o_ref[c] = acc_ref[:, c * by_column_block:(c + 1) * by_column_block].astype(BF16)
            else:
                o_ref[...] = acc_ref[...].astype(BF16)

    if by_column_block:
        out_spec = pl.BlockSpec((bn // by_column_block, bm, by_column_block), lambda i, j, k: (j, i, 0))
        out_shape = jax.ShapeDtypeStruct((n // by_column_block, m, by_column_block), BF16)
    else:
        out_spec = pl.BlockSpec((bm, bn), lambda i, j, k: (i, j))
        out_shape = jax.ShapeDtypeStruct((m, n), BF16)
    a_spec = (pl.BlockSpec((bm, bt), lambda i, j, k: (i, k)) if a_transposed
              else pl.BlockSpec((bt, bm), lambda i, j, k: (k, i)))
    return _call(body, name, (m // bm, n // bn, nk), [a_spec, pl.BlockSpec((bt, bn), lambda i, j, k: (k, j))],
                 out_spec, out_shape, [pltpu.VMEM((bm, bn), F32)], ("parallel", "parallel", "arbitrary"), (a, b))


def _outproj_bwd(dx1, wo, tm, name):
    t, d = dx1.shape
    c2 = POOL_WIDTH + SG_WIDTH

    def body(dx_ref, wo_ref, dya_ref, dyb_ref, dyc_ref, dyct_ref, dxb_ref):
        dxb = dx_ref[...].astype(BF16)
        dxb_ref[...] = dxb
        dya_ref[...] = _dot_nt(dxb, wo_ref[0:POOL_WIDTH, :])
        dyb_ref[...] = _dot_nt(dxb, wo_ref[POOL_WIDTH:c2, :])
        dyc_ref[...] = _dot_nt(dxb, wo_ref[c2:, :]).astype(BF16)
        dyct_ref[...] = _dot_nt(wo_ref[c2:, :], dxb).astype(BF16)

    row = lambda w: pl.BlockSpec((tm, w), lambda i: (i, 0))
    return pl.pallas_call(
        body, name=name, grid=(t // tm,),
        in_specs=[row(d), pl.BlockSpec((d, d), lambda i: (0, 0))],
        out_specs=[row(POOL_WIDTH), row(SG_WIDTH), row(SB_WIDTH), pl.BlockSpec((SB_WIDTH, tm), lambda i: (0, i)), row(d)],
        out_shape=[jax.ShapeDtypeStruct((t, POOL_WIDTH), F32), jax.ShapeDtypeStruct((t, SG_WIDTH), F32),
                   jax.ShapeDtypeStruct((t, SB_WIDTH), BF16), jax.ShapeDtypeStruct((SB_WIDTH, t), BF16),
                   jax.ShapeDtypeStruct((t, d), BF16)],
        compiler_params=_params(("parallel",)),
    )(dx1, wo)


def _sba_bwd(qkv, ktb, dyc, dyct, cc, nvis, tq, name, ride=None):
    t = qkv.shape[0]
    nb = t // tq
    idx = jnp.arange(tq)
    upper = (idx[None, :] > idx[:, None]).astype(BF16)
    lower = (idx[None, :] < idx[:, None]).astype(BF16)

    qb = SB_QUERY_BLOCKS_PER_STEP

    def body(q_ref, k_ref, v_ref, kt_ref, do_ref, dot_ref, c_ref, n_ref, up_ref, lo_ref, dqt_ref, dk_ref, dv_ref):
        pair, step = pl.program_id(0), pl.program_id(1)
        refs = (q_ref, k_ref, v_ref, kt_ref, do_ref, dot_ref, c_ref, n_ref, up_ref, lo_ref, dqt_ref, dk_ref, dv_ref)
        subs = list(range(qb))

        @pl.when(step == 0)
        def _():
            dk_ref[...] = jnp.zeros_like(dk_ref)
            dv_ref[...] = jnp.zeros_like(dv_ref)
            for sub in subs:
                query_blocks([sub], pair, step, *refs)

        @pl.when(step > 0)
        def _():
            query_blocks(subs, pair, step, *refs)

    def query_blocks(subs, pair, step, q_ref, k_ref, v_ref, kt_ref, do_ref, dot_ref, c_ref, n_ref, up_ref, lo_ref,
                     dqt_ref, dk_ref, dv_ref):
        index = {sub: step * qb + sub for sub in subs}
        mine = {sub: slice(sub * tq, (sub + 1) * tq) for sub in subs}
        up = up_ref[...]
        lo = lo_ref[...]
        lane_head = lax.broadcasted_iota(jnp.int32, (1, 128), 1) // SB_HD
        sub_head = lax.broadcasted_iota(jnp.int32, (128, 1), 0) // SB_HD
        causal = (lax.broadcasted_iota(jnp.int32, (tq, tq), 0) < lax.broadcasted_iota(jnp.int32, (tq, tq), 1))
        hms = [lane_head == h for h in range(2)]
        qh, qs, doh, dot = {}, {}, {}, {}
        for sub in subs:
            q, do = q_ref[mine[sub], :], do_ref[mine[sub], :]
            qh[sub] = [jnp.where(hm, q, jnp.zeros_like(q)) for hm in hms]
            qs[sub] = [x * 0.125 for x in qh[sub]]
            doh[sub] = [jnp.where(hm, do, jnp.zeros_like(do)) for hm in hms]
            dot[sub] = dot_ref[:, mine[sub]]

        def blocks(work, carry):
            cgs = {sub: list(carry[sub][:2]) for sub in carry}
            dqt = {sub: carry[sub][2] for sub in carry}
            rows = [pl.ds(pl.multiple_of(j * tq, tq), tq) for _, j, _ in work]
            kj = [k_ref[r, :] for r in rows]
            vj = [v_ref[r, :] for r in rows]
            kt = [kt_ref[j] for _, j, _ in work]
            chains = [(w, h) for w in range(len(work)) for h in range(2)]
            z = [_dot_nt(kj[w], qh[work[w][0]][h]) for w, h in chains]
            da = [_dot(jnp.where(hms[h], vj[w], jnp.zeros_like(vj[w])), dot[work[w][0]]) for w, h in chains]
            ls = [_log2_sigmoids(zz * SB_SCALE) for zz in z]
            lb = [x[0] for x in ls]
            l1 = [jnp.where(causal, x[1], 0.0) if work[w][2] else x[1] for x, (w, h) in zip(ls, chains)]
            after = [_tri_dot(up, x) for x in l1]
            a = [jnp.exp2(lb[n] + after[n] + c_ref[h, pl.ds(work[w][1], 1), mine[work[w][0]]])
                 for n, (w, h) in enumerate(chains)]
            a = [jnp.where(causal, a[n], 0.0) if work[w][2] else a[n] for n, (w, h) in enumerate(chains)]
            g = [a[n] * da[n] for n in range(len(chains))]
            gloc = [_dot(lo, x.astype(BF16)) for x in g]
            dzb = []
            for n, (w, h) in enumerate(chains):
                sub, _, diag = work[w]
                gsum = gloc[n] + cgs[sub][h]
                dz = g[n] - jnp.exp2(lb[n]) * (g[n] + gsum)
                dzb.append((jnp.where(causal, dz, 0.0) if diag else dz).astype(BF16))
                cgs[sub][h] = gsum[tq - 1:tq, :] + g[n][tq - 1:tq, :]
            ab = [x.astype(BF16) for x in a]
            for n, (w, h) in enumerate(chains):
                sub = work[w][0]
                dqt[sub] = dqt[sub] + _dot(jnp.where(sub_head == h, kt[w], jnp.zeros_like(kt[w])), dzb[n])
            for w, (sub, _, _) in enumerate(work):
                dk_ref[rows[w], :] += _dot(dzb[2 * w], qs[sub][0]) + _dot(dzb[2 * w + 1], qs[sub][1])
                dv_ref[rows[w], :] += _dot(ab[2 * w], doh[sub][0]) + _dot(ab[2 * w + 1], doh[sub][1])
            return {sub: (cgs[sub][0], cgs[sub][1], dqt[sub]) for sub in cgs}

        zero = jnp.zeros((1, tq), F32)
        carry = {}
        for sub in subs:
            i = index[sub]
            n = jnp.clip(n_ref[pair, i].astype(jnp.int32), 1, i + 1)
            carry[sub] = lax.fori_loop(jnp.minimum(i + 1 - n, i - 1), i - 1,
                                       lambda s, cr, sub=sub: blocks([(sub, s, False)], {sub: cr})[sub],
                                       (zero, zero, jnp.zeros((128, tq), F32)))
        both = lambda: blocks([(sub, index[sub] - 1 + b, b == 1) for sub in subs for b in range(2)], carry)
        if len(subs) == 1:
            carry = lax.cond(index[subs[0]] > 0, both, lambda: blocks([(subs[0], index[subs[0]], True)], carry))
        else:
            carry = both()
        for sub in subs:
            dqt_ref[:, mine[sub]] = carry[sub][2] * 0.125

    return _call(
        body, name, (SB_PAIRS, nb // qb),
        [pl.BlockSpec((qb * tq, 128), lambda p, i: (i, p)),
         pl.BlockSpec((t, 128), lambda p, i: (0, SB_PAIRS + p)),
         pl.BlockSpec((t, 128), lambda p, i: (0, 2 * SB_PAIRS + p)),
         pl.BlockSpec((None, nb, 128, tq), lambda p, i: (p, 0, 0, 0)),
         pl.BlockSpec((qb * tq, 128), lambda p, i: (i, p)),
         pl.BlockSpec((128, qb * tq), lambda p, i: (p, i)),
         pl.BlockSpec((2, nb, qb * tq), lambda p, i: (p, 0, i)),
         pl.BlockSpec(memory_space=pltpu.SMEM),
         pl.BlockSpec((tq, tq), lambda p, i: (0, 0)),
         pl.BlockSpec((tq, tq), lambda p, i: (0, 0))],
        [pl.BlockSpec((128, qb * tq), lambda p, i: (p, i)),
         pl.BlockSpec((t, 128), lambda p, i: (0, p)),
         pl.BlockSpec((t, 128), lambda p, i: (0, p))],
        [jax.ShapeDtypeStruct((SB_WIDTH, t), F32), jax.ShapeDtypeStruct((t, SB_WIDTH), F32),
         jax.ShapeDtypeStruct((t, SB_WIDTH), F32)],
        [], ("arbitrary", "arbitrary"), (qkv, qkv, qkv, ktb, dyc, dyct, cc, nvis, upper, lower), ride)


def _sg_bwd(ab, dyb, gn, wm, wmt, bfull, tm, name):
    t = ab.shape[0]
    nt = t // tm
    sel = (jnp.arange(SG_WIDTH)[:, None] // SG_HD == jnp.arange(CHUNK)[None, :]).astype(F32)

    def body(u_ref, v_ref, dy_ref, gn_ref, wm_ref, wmt_ref, b_ref, sel_ref,
             dup_ref, dvp_ref, dgn_ref, dw_ref, db_ref, dbacc_ref):
        i = pl.program_id(0)

        @pl.when(i == 0)
        def _():
            dgn_ref[...] = jnp.zeros_like(dgn_ref)
            dw_ref[...] = jnp.zeros_like(dw_ref)
            dbacc_ref[...] = jnp.zeros_like(dbacc_ref)

        tril = (lax.broadcasted_iota(jnp.int32, (CHUNK, CHUNK), 0) >= lax.broadcasted_iota(jnp.int32, (CHUNK, CHUNK), 1))
        gn_ = gn_ref[...]
        for c in range(tm // CHUNK):
            rows = slice(c * CHUNK, (c + 1) * CHUNK)
            up, vp, dy = u_ref[rows, :], v_ref[rows, :], dy_ref[rows, :]
            u, v = _gelu(up), _gelu(vp)
            r = _rstd(v)
            vn = (v * r * gn_).astype(BF16)
            sv = b_ref[...]
            for h in range(SG_HEADS):
                sv = sv + jnp.where(_head_lanes(h), _dot(wm_ref[h], vn), 0.0)
            dup_ref[rows, :] = dy * sv * _gelu_grad(up)
            dsv = dy * u
            dbacc_ref[...] += dsv
            dvn = jnp.zeros((CHUNK, SG_WIDTH), F32)
            for h in range(SG_HEADS):
                dsv_h = jnp.where(_head_lanes(h), dsv, 0.0).astype(BF16)
                dvn = dvn + _dot(wmt_ref[h], dsv_h)
                dw_ref[h] += jnp.where(tril, _dot_nt(dsv_h, vn), 0.0)
            dv, dgn = _rms_bwd(v, r, gn_, dvn)
            dgn_ref[...] += dgn
            dvp_ref[rows, :] = dv * _gelu_grad(vp)

        @pl.when(i == nt - 1)
        def _():
            db_ref[...] = jnp.dot(dbacc_ref[...], sel_ref[...], preferred_element_type=F32,
                                  precision=lax.Precision.HIGHEST)

    const = lambda shape: pl.BlockSpec(shape, lambda i: tuple(0 for _ in shape))
    return pl.pallas_call(
        body, name=name, grid=(nt,),
        in_specs=[pl.BlockSpec((tm, SG_WIDTH), lambda i: (i, 1)), pl.BlockSpec((tm, SG_WIDTH), lambda i: (i, 2)),
                  pl.BlockSpec((tm, SG_WIDTH), lambda i: (i, 0)), const((1, SG_WIDTH)),
                  const((SG_HEADS, CHUNK, CHUNK)), const((SG_HEADS, CHUNK, CHUNK)), const((CHUNK, SG_WIDTH)),
                  const((SG_WIDTH, CHUNK))],
        out_specs=[pl.BlockSpec((tm, SG_WIDTH), lambda i: (i, 0)), pl.BlockSpec((tm, SG_WIDTH), lambda i: (i, 0)),
                   const((1, SG_WIDTH)), const((SG_HEADS, CHUNK, CHUNK)), const((CHUNK, CHUNK))],
        out_shape=[jax.ShapeDtypeStruct((t, SG_WIDTH), F32), jax.ShapeDtypeStruct((t, SG_WIDTH), F32),
                   jax.ShapeDtypeStruct((1, SG_WIDTH), F32), jax.ShapeDtypeStruct((SG_HEADS, CHUNK, CHUNK), F32),
                   jax.ShapeDtypeStruct((CHUNK, CHUNK), F32)],
        scratch_shapes=[pltpu.VMEM((CHUNK, SG_WIDTH), F32)],
        compiler_params=_params(("arbitrary",)),
    )(ab, ab, dyb, gn, wm, wmt, bfull, sel)


def _pool_bwd(ab, dya, wbd, scale, tm, name):
    t = ab.shape[0]
    nt = t // tm
    hb = tm // POOL_HALO
    nh = t // POOL_HALO

    def body(cur_ref, prev_ref, dy_ref, dyn_ref, w_ref, s_ref, da_ref, dw_ref, ds_ref):
        i = pl.program_id(0)

        @pl.when(i == 0)
        def _():
            dw_ref[...] = jnp.zeros_like(dw_ref)
            ds_ref[...] = jnp.zeros_like(ds_ref)

        prev = jnp.where(i == 0, 0.0, prev_ref[...])
        d = _pool_diff(cur_ref[...], prev, i * tm).astype(BF16)
        dy = dy_ref[...]
        ds_ref[...] += jnp.sum(dy * _dot(d, w_ref[...]), axis=0, keepdims=True)
        dyn = jnp.where(i == nt - 1, 0.0, dyn_ref[...])
        dys = (jnp.concatenate([dy, dyn], axis=0) * s_ref[...]).astype(BF16)
        dw_ref[...] += _dot_tn(d, dys[:tm])
        dd = _dot_nt(dys, w_ref[...])
        fwd = _pool_window_sums(dd / _pool_count(i * tm, tm + POOL_HALO), True)
        da_ref[...] = fwd[:tm] - dd[:tm]

    return pl.pallas_call(
        body, name=name, grid=(nt,),
        in_specs=[pl.BlockSpec((tm, POOL_WIDTH), lambda i: (i, 0)),
                  pl.BlockSpec((POOL_HALO, POOL_WIDTH), lambda i: (jnp.maximum(i * hb - 1, 0), 0)),
                  pl.BlockSpec((tm, POOL_WIDTH), lambda i: (i, 0)),
                  pl.BlockSpec((POOL_HALO, POOL_WIDTH), lambda i: (jnp.minimum((i + 1) * hb, nh - 1), 0)),
                  pl.BlockSpec((POOL_WIDTH, POOL_WIDTH), lambda i: (0, 0)),
                  pl.BlockSpec((1, POOL_WIDTH), lambda i: (0, 0))],
        out_specs=[pl.BlockSpec((tm, POOL_WIDTH), lambda i: (i, 0)),
                   pl.BlockSpec((POOL_WIDTH, POOL_WIDTH), lambda i: (0, 0)),
                   pl.BlockSpec((1, POOL_WIDTH), lambda i: (0, 0))],
        out_shape=[jax.ShapeDtypeStruct((t, POOL_WIDTH), F32), jax.ShapeDtypeStruct((POOL_WIDTH, POOL_WIDTH), F32),
                   jax.ShapeDtypeStruct((1, POOL_WIDTH), F32)],
        compiler_params=_params(("arbitrary",)),
    )(ab, ab, dya, dya, wbd, scale)


def _inproj_bwd(dx1, x, g, da, dup, dvp, dqt, dk, dv, w, tm, name, ride=None):
    t, d = x.shape
    n = w.shape[1]
    nt = t // tm

    def body(dx1_ref, x_ref, g_ref, da_ref, du_ref, dv_ref, dqt_ref, dk_ref, dvv_ref, w_ref,
             dx_ref, h_ref, dp_ref, dg_ref):
        @pl.when(pl.program_id(0) == 0)
        def _():
            dg_ref[...] = jnp.zeros_like(dg_ref)

        dp = jnp.concatenate([da_ref[...], du_ref[...], dv_ref[...], dqt_ref[...].T, dk_ref[...], dvv_ref[...]],
                             axis=1).astype(BF16)
        dp_ref[...] = dp
        xx = x_ref[...]
        r = _rstd(xx)
        h_ref[...] = (xx * r * g_ref[...]).astype(BF16)
        dx, dg = _rms_bwd(xx, r, g_ref[...], _dot_nt(dp, w_ref[...]))
        dx_ref[...] = dx1_ref[...] + dx
        dg_ref[...] += dg

    row = lambda w_: pl.BlockSpec((tm, w_), lambda i: (i, 0))
    return _call(
        body, name, (nt,),
        [row(d), row(d), pl.BlockSpec((1, d), lambda i: (0, 0)), row(POOL_WIDTH), row(SG_WIDTH),
         row(SG_WIDTH), pl.BlockSpec((SB_WIDTH, tm), lambda i: (0, i)), row(SB_WIDTH), row(SB_WIDTH),
         pl.BlockSpec((d, n), lambda i: (0, 0))],
        [row(d), row(d), row(n), pl.BlockSpec((1, d), lambda i: (0, 0))],
        [jax.ShapeDtypeStruct((t, d), F32), jax.ShapeDtypeStruct((t, d), BF16),
         jax.ShapeDtypeStruct((t, n), BF16), jax.ShapeDtypeStruct((1, d), F32)],
        [], ("arbitrary",), (dx1, x, g, da, dup, dvp, dqt, dk, dv, w), ride)


def _full_w_in(gathered):
    return gathered.transpose(1, 0, 2).reshape(D_MODEL, IN_COLS)


_SMALL_SHAPES = ((D_MODEL,), (4, POOL_GW, POOL_GW), (POOL_WIDTH,), (SG_WIDTH,), (SG_HEADS, CHUNK, CHUNK),
                 (SG_HEADS, CHUNK), (D_MODEL,))
_SMALL_SIZES = tuple(functools.reduce(lambda p, q: p * q, shp) for shp in _SMALL_SHAPES)
_SMALL_ROWS = sum(_SMALL_SIZES) // 128
_NORM1_ROWS = D_MODEL // 128


def _pack_small_layer(arrs):
    return jnp.concatenate([a.reshape(-1) for a in arrs]).reshape(_SMALL_ROWS, 128)


def _pack_small(arrs, final):
    return jnp.concatenate([_pack_small_layer([a[l] for a in arrs]) for l in range(DEPTH)] + [final.reshape(-1, 128)])


def _unpack_small(buf):
    per_layer = []
    for l in range(DEPTH):
        flat, off, outs = buf[l * _SMALL_ROWS:(l + 1) * _SMALL_ROWS].reshape(-1), 0, []
        for shp, size in zip(_SMALL_SHAPES, _SMALL_SIZES):
            outs.append(flat[off:off + size].reshape(shp))
            off += size
        per_layer.append(outs)
    return [jnp.stack([per_layer[l][k] for l in range(DEPTH)]) for k in range(len(_SMALL_SHAPES))] + \
           [buf[DEPTH * _SMALL_ROWS:].reshape(-1)]


def _tiles(t):
    return min(512, t), min(256, t // 4), min(4096, t)


def _layer_fwd(xl, wi, wo, wu, wd, small_w, l, ride_mlp=None):
    n1, pw, ps, sn, sw, sb, n2 = small_w
    tm, tq, _ = _tiles(xl.shape[0])
    wbd = jnp.zeros((4, POOL_GW, 4, POOL_GW), F32)
    for gi in range(4):
        wbd = wbd.at[gi, :, gi, :].set(pw[gi])
    wbd = wbd.reshape(POOL_WIDTH, POOL_WIDTH).astype(BF16)
    wm = (sw * jnp.tril(jnp.ones((CHUNK, CHUNK), F32))).astype(BF16)
    wmt = wm.transpose(0, 2, 1)
    bfull = jnp.repeat(sb.T, SG_HD, axis=1)
    g1, g2, psc, gn = n1[None, :], n2[None, :], ps[None, :], sn[None, :]

    ab, qkv, ktb, vtb = _inproj_fwd(xl, g1, wi, tm, tq, f"inproj_fwd{l}")
    ya = _pool_fwd(ab, wbd, psc, tm, f"pool_fwd{l}")
    yb = _sg_fwd(ab, gn, wm, bfull, tm, f"sg_fwd{l}")
    shards = [w for w in (wo, wu, wd) if w.ndim == 2]
    if shards:
        (yct, cc, nvis), got = _sba_fwd(qkv, vtb, tq, f"sba_fwd{l}", SiblingGather(shards))
        got = iter(got)
        wo, wu, wd = (next(got) if w.ndim == 2 else w for w in (wo, wu, wd))
    else:
        yct, cc, nvis = _sba_fwd(qkv, vtb, tq, f"sba_fwd{l}")
    wo = wo.reshape(D_MODEL, D_MODEL)
    res = _outproj_mlp_fwd(xl, ya, yb, yct, wo, g2, wu, wd, min(MLP_ROWS, xl.shape[0]), f"mlp_fwd{l}", ride_mlp)
    (x1, u, x2), rode = res if ride_mlp is not None else (res, None)
    saved = dict(x=xl, ab=ab, qkv=qkv, ktb=ktb, cc=cc, nvis=nvis, ya=ya, yb=yb, yct=yct, x1=x1, u=u, wi=wi, wo=wo, wu=wu,
                 wd=wd, wbd=wbd, wm=wm, wmt=wmt, bfull=bfull, g1=g1, g2=g2, psc=psc, gn=gn)
    return x2, saved, rode


def _layer_bwd(dx, s, l, ride_mlp=None, exchange=True, gather_small=False):
    tm, tq, tw = _tiles(dx.shape[0])
    ktb = s["ktb"]
    res = _mlp_bwd(dx, s["x1"], s["g2"], s["u"], s["wu"], s["wd"], min(MLP_ROWS, dx.shape[0]), f"mlp_bwd{l}", ride_mlp)
    (dx1, du, r, h2, dx2b, dn2), rode = res if ride_mlp is not None else (res, None)
    dw_up = _matmul_tn(h2, du, D_MODEL, 2 * FF_SHARD, tw, f"dw_up{l}", by_column_block=FF_SHARD)
    dw_down = _matmul_tn(r, dx2b, 1024, D_MODEL, tw, f"dw_down{l}").reshape(N_DEV, FF_SHARD, D_MODEL)
    dya, dyb, dyc, dyct, dx1b = _outproj_bwd(dx1, s["wo"], tm, f"outproj_bwd{l}")
    dw_out = jnp.concatenate([
        _matmul_tn(jnp.concatenate([s["ya"], s["yb"]], axis=1), dx1b, POOL_WIDTH + SG_WIDTH, D_MODEL, tw, f"dw_out_ab{l}"),
        _matmul_tn(s["yct"], dx1b, SB_WIDTH, D_MODEL, tw, f"dw_out_c{l}", a_transposed=True)]
    ).reshape(N_DEV, OUT_SHARD, D_MODEL)
    if exchange:
        (dqt, dk, dv), (dw_out, dw_up, dw_down) = _sba_bwd(s["qkv"], ktb, dyc, dyct, s["cc"], s["nvis"], tq, f"sba_bwd{l}",
                                                           Exchange([dw_out, dw_up, dw_down], True))
    else:
        dqt, dk, dv = _sba_bwd(s["qkv"], ktb, dyc, dyct, s["cc"], s["nvis"], tq, f"sba_bwd{l}")
    dup, dvp, dgn, dwm, dbm = _sg_bwd(s["ab"], dyb, s["gn"], s["wm"], s["wmt"], s["bfull"], tm, f"sg_bwd{l}")
    da, dwbd, dpsc = _pool_bwd(s["ab"], dya, s["wbd"], s["psc"], tm, f"pool_bwd{l}")
    dpw = jnp.stack([dwbd[gi * POOL_GW:(gi + 1) * POOL_GW, gi * POOL_GW:(gi + 1) * POOL_GW] for gi in range(4)])
    small = _pack_small_layer([jnp.zeros((D_MODEL,), F32), dpw, dpsc[0], dgn[0], dwm, dbm[:, :SG_HEADS].T, dn2[0]])[_NORM1_ROWS:]
    res = _inproj_bwd(dx1, s["x"], s["g1"], da, dup, dvp, dqt, dk, dv, s["wi"], tm, f"inproj_bwd{l}",
                      Exchange([small], False) if gather_small else None)
    (dx, h1, dproj, dn1), small = (res[0], res[1][0]) if gather_small else (res, small)
    dw_in = _matmul_tn(h1, dproj, D_MODEL, IN_COLS // 3, tw, f"dw_in{l}")
    dw_in = dw_in.reshape(D_MODEL, N_DEV, IN_SHARD).transpose(1, 0, 2)
    return dx, (dw_in, dw_out, dw_up, dw_down), (dn1.reshape(_NORM1_ROWS, 128), small), rode


def kernel(x, norm1, w_in, pool_w, pool_scale, sg_norm, sg_w, sg_b, w_out, norm2, w_up, w_down, final_norm, loss_target, m_norm1, m_w_in, m_pool_w, m_pool_scale, m_sg_norm, m_sg_w, m_sg_b, m_w_out, m_norm2, m_w_up, m_w_down, m_final_norm, v_norm1, v_w_in, v_pool_w, v_pool_scale, v_sg_norm, v_sg_w, v_sg_b, v_w_out, v_norm2, v_w_up, v_w_down, v_final_norm):
    t = x.shape[1]
    tm = _tiles(t)[0]
    small_w = (norm1, pool_w, pool_scale, sg_norm, sg_w, sg_b, norm2)
    big_w = (w_in, w_out, w_up, w_down)
    big_m = (m_w_in, m_w_out, m_w_up, m_w_down)
    big_v = (v_w_in, v_w_out, v_w_up, v_w_down)
    shards = [[w[l].astype(BF16) for w in big_w] for l in range(DEPTH)]

    wi0 = _full_w_in(SiblingGather(shards[0][:1]).alone("gather_w_in0")[0])
    sh_in1, sh_out1, sh_up1, sh_down1 = shards[1]
    x1, s0, (wi1, wo1, wd1) = _layer_fwd(x.reshape(t, D_MODEL), wi0, *shards[0][1:], tuple(w[0] for w in small_w), 0,
                                         ride_mlp=Exchange([sh_in1, sh_out1, sh_down1], False))
    x2, s1, _ = _layer_fwd(x1, _full_w_in(wi1), wo1, sh_up1, wd1, tuple(w[1] for w in small_w), 1)
    loss_local, dx, dfinal = _loss_grad(x2, final_norm[None, :], loss_target.reshape(t, D_MODEL), tm, "loss_grad")
    loss = lax.psum(loss_local[0, 0], MESH_AXES)

    dx, parts1, small1, _ = _layer_bwd(dx, s1, 1)
    early = jnp.concatenate(list(small1) + [dfinal.reshape(-1, 128)])
    dx, parts0, (dn1, small0), (recv_in1, early) = _layer_bwd(
        dx, s0, 0, ride_mlp=Exchange([parts1[0], early], [True, False]), gather_small=True)
    grad_x = dx.reshape(x.shape)
    recv_in0, dn1 = _exchange([parts0[0], dn1], [True, False], "scatter_w_in0_gather_norm1_0")
    small_all = jnp.concatenate([dn1, small0, early], axis=1)
    received = [[recv_in0] + list(parts0[1:]), [recv_in1] + list(parts1[1:])]

    big = [None] * 4
    for l in reversed(range(DEPTH)):
        for k in range(4):
            big[k] = _reduce_adamw(received[l][k], big_w[k], big_m[k], big_v[k], l, big[k], f"adamw{k}_{l}")

    sm = _reduce_adamw(
        small_all,
        _pack_small(small_w, final_norm)[None],
        _pack_small([m_norm1, m_pool_w, m_pool_scale, m_sg_norm, m_sg_w, m_sg_b, m_norm2], m_final_norm)[None],
        _pack_small([v_norm1, v_pool_w, v_pool_scale, v_sg_norm, v_sg_w, v_sg_b, v_norm2], v_final_norm)[None],
        0, None, "adamw_replicated")

    out = [loss, grad_x]
    for k in range(4):
        n1, pw, ps, sn, sw, sb, n2, fn = _unpack_small(sm[k][0])
        out += [n1, big[0][k], pw, ps, sn, sw, sb, big[1][k], n2, big[2][k], big[3][k], fn]
    return tuple(out)
```

```python
import functools

import jax
import jax.numpy as jnp
from jax import lax
from jax.experimental import pallas as pl
from jax.experimental.pallas import tpu as pltpu

F32 = jnp.float32
BF16 = jnp.bfloat16

D_MODEL = 1024
DEPTH = 2
POOL_WIDTH = 256
SG_WIDTH = 256
SB_WIDTH = 512
POOL_WINDOWS = (2, 4, 8, 16)
POOL_GW = 64
POOL_HALO = 16
CHUNK = 128
SG_HEADS = 4
SG_HD = 64
SB_HD = 64
SB_PAIRS = SB_WIDTH // 128
AB_COLS = POOL_WIDTH + 2 * SG_WIDTH
IN_COLS = AB_COLS + 3 * SB_WIDTH
D_FF = 4096
EPS = 1e-6
N_DEV = 8
FF_SHARD = D_FF // N_DEV
IN_SHARD = IN_COLS // N_DEV
OUT_SHARD = D_MODEL // N_DEV
ADAM_LR = 0.001
ADAM_B1 = 0.9
ADAM_B2 = 0.999
ADAM_EPS = 1e-08
ADAM_WD = 0.01
ADAM_STEP = 10
VMEM_LIMIT = 56 * 1024 * 1024
MLP_ROWS = 1024
MESH_AXES = ("x", "y", "c")


def _dot(a, b):
    return jnp.dot(a, b, preferred_element_type=F32)


def _dot_nt(a, b):
    return lax.dot_general(a, b, (((1,), (1,)), ((), ())), preferred_element_type=F32)


def _dot_tn(a, b):
    return lax.dot_general(a, b, (((0,), (0,)), ((), ())), preferred_element_type=F32)


def _rstd(x):
    return lax.rsqrt(jnp.mean(x * x, axis=-1, keepdims=True) + EPS)


def _rms_bwd(x, r, g, dh):
    gq = dh * g
    dx = r * gq - x * (r * r * r) * jnp.mean(gq * x, axis=-1, keepdims=True)
    dg = jnp.sum(dh * x * r, axis=0, keepdims=True)
    return dx, dg


def _params(sem=None):
    kw = dict(vmem_limit_bytes=VMEM_LIMIT)
    if sem is not None:
        kw["dimension_semantics"] = sem
    return pltpu.CompilerParams(**kw)


def _row_tile(rows, cap):
    best = 8
    for t in range(8, min(rows, cap) + 1, 8):
        if rows % t == 0:
            best = t
    return best


def _peer(k):
    x, y, c = lax.axis_index("x"), lax.axis_index("y"), lax.axis_index("c")
    return (1 - x if k & 4 else x, 1 - y if k & 2 else y, 1 - c if k & 1 else c)


def _my_index():
    return 4 * lax.axis_index("x") + 2 * lax.axis_index("y") + lax.axis_index("c")


class Exchange:
    def __init__(self, arrs, scatter):
        self.arrs = list(arrs)
        self.scatter = list(scatter) if isinstance(scatter, (list, tuple)) else [scatter] * len(self.arrs)
        self.n = len(self.arrs)
        self.any_specs = [pl.BlockSpec(memory_space=pl.ANY)] * self.n
        self.out_shape = [jax.ShapeDtypeStruct((N_DEV,) + a.shape[-2:], a.dtype) for a in self.arrs]
        self.sems = [pltpu.SemaphoreType.DMA((self.n, N_DEV - 1)), pltpu.SemaphoreType.DMA((self.n, N_DEV - 1)),
                     pltpu.SemaphoreType.DMA((self.n,))]

    def _copies(self, ins, outs, sems):
        send_sems, recv_sems, local_sems = sems
        me = _my_index()
        local, remote = [], []
        for a in range(self.n):
            sc = self.scatter[a]
            local.append(pltpu.make_async_copy(ins[a].at[me] if sc else ins[a], outs[a].at[me], local_sems.at[a]))
            for k in range(1, N_DEV):
                px, py, pc = _peer(k)
                remote.append(pltpu.make_async_remote_copy(
                    src_ref=ins[a].at[4 * px + 2 * py + pc] if sc else ins[a], dst_ref=outs[a].at[me],
                    send_sem=send_sems.at[a, k - 1], recv_sem=recv_sems.at[a, k - 1],
                    device_id=(px, py, pc), device_id_type=pl.DeviceIdType.MESH))
        return local, remote

    def start(self, ins, outs, sems):
        local, remote = self._copies(ins, outs, sems)
        for cp in local + remote:
            cp.start()

    def middle(self, ins, outs, sems):
        pass

    def wait(self, ins, outs, sems):
        local, remote = self._copies(ins, outs, sems)
        for cp in remote:
            cp.wait_recv()
        for cp in remote:
            cp.wait_send()
        for cp in local:
            cp.wait()

    def alone(self, name):
        n = self.n

        def body(*refs):
            ins, outs, sems = refs[:n], refs[n:2 * n], refs[2 * n:]
            self.start(ins, outs, sems)
            self.middle(ins, outs, sems)
            self.wait(ins, outs, sems)

        return pl.pallas_call(body, name=name, out_shape=self.out_shape, in_specs=self.any_specs,
                              out_specs=self.any_specs, scratch_shapes=self.sems)(*self.arrs)


class SiblingGather(Exchange):
    def __init__(self, arrs):
        super().__init__(arrs, False)

    def _plan(self, ins, outs, sems):
        send_sems, recv_sems, local_sems = sems
        x, y, core = lax.axis_index("x"), lax.axis_index("y"), lax.axis_index("c")
        me, sibling = (x, y, core), (x, y, 1 - core)
        chips = [(1 - x, y), (x, 1 - y), (1 - x, 1 - y)]
        plan = dict(local=[], first=[], arrived=[], passed=[], late=[])
        for a in range(self.n):
            slot = lambda p, a=a: outs[a].at[4 * p[0] + 2 * p[1] + p[2]]

            def copy(k, block, to, src=None, a=a, slot=slot):
                return pltpu.make_async_remote_copy(
                    src_ref=slot(block) if src is None else src, dst_ref=slot(block), send_sem=send_sems.at[a, k],
                    recv_sem=recv_sems.at[a, k], device_id=to, device_id_type=pl.DeviceIdType.MESH)

            plan["local"].append(pltpu.make_async_copy(ins[a], slot(me), local_sems.at[a]))
            plan["first"] += [copy(0, me, sibling, src=ins[a])] + [copy(1 + j, me, (*ch, core), src=ins[a])
                                                                  for j, ch in enumerate(chips)]
            plan["arrived"] += [copy(1 + j, (*ch, core), me) for j, ch in enumerate(chips)]
            plan["passed"] += [copy(4 + j, (*ch, core), sibling) for j, ch in enumerate(chips)]
            plan["late"] += [copy(0, sibling, me)] + [copy(4 + j, (*ch, 1 - core), me) for j, ch in enumerate(chips)]
        return plan

    def start(self, ins, outs, sems):
        plan = self._plan(ins, outs, sems)
        for cp in plan["local"] + plan["first"]:
            cp.start()

    def middle(self, ins, outs, sems):
        plan = self._plan(ins, outs, sems)
        for arrived, passed in zip(plan["arrived"], plan["passed"]):
            arrived.wait_recv()
            passed.start()

    def wait(self, ins, outs, sems):
        plan = self._plan(ins, outs, sems)
        for cp in plan["late"]:
            cp.wait_recv()
        for cp in plan["first"] + plan["passed"]:
            cp.wait_send()
        for cp in plan["local"]:
            cp.wait()


def _exchange(arrs, scatter, name):
    return Exchange(arrs, scatter).alone(name)


def _call(body, name, grid, in_specs, out_specs, out_shape, scratch_shapes, semantics, args, ride=None):
    if ride is None:
        return pl.pallas_call(body, name=name, grid=grid, in_specs=in_specs, out_specs=out_specs, out_shape=out_shape,
                              scratch_shapes=scratch_shapes, compiler_params=_params(semantics))(*args)
    single = not isinstance(out_shape, (list, tuple))
    out_specs, out_shape = ([out_specs], [out_shape]) if single else (list(out_specs), list(out_shape))
    n_in, n_out, n_scr, n = len(in_specs), len(out_specs), len(scratch_shapes), ride.n

    def riding(*refs):
        ins, cins = refs[:n_in], refs[n_in:n_in + n]
        outs, couts = refs[n_in + n:n_in + n + n_out], refs[n_in + n + n_out:n_in + 2 * n + n_out]
        scr, sems = refs[n_in + 2 * n + n_out:n_in + 2 * n + n_out + n_scr], refs[n_in + 2 * n + n_out + n_scr:]
        step = functools.reduce(lambda s, a: s * grid[a] + pl.program_id(a), range(len(grid)), 0)
        steps = functools.reduce(lambda p, q: p * q, grid)

        @pl.when(step == 0)
        def _():
            ride.start(cins, couts, sems)

        @pl.when(step == (3 * steps) // 4)
        def _():
            ride.middle(cins, couts, sems)

        body(*ins, *outs, *scr)

        @pl.when(step == steps - 1)
        def _():
            ride.wait(cins, couts, sems)

    res = pl.pallas_call(
        riding, name=name, grid=grid, in_specs=list(in_specs) + ride.any_specs, out_specs=out_specs + ride.any_specs,
        out_shape=out_shape + ride.out_shape, scratch_shapes=list(scratch_shapes) + ride.sems,
        compiler_params=_params(("arbitrary",) * len(grid)))(*args, *ride.arrs)
    own = res[0] if single else list(res[:n_out])
    return own, list(res[n_out:])


def _reduce_adamw(parts, w, m, v, l, prev, name, ride=None):
    _, rows, n = parts.shape
    tr = _row_tile(rows, max(8, (1 << 18) // n))
    c1 = 1.0 - ADAM_B1 ** ADAM_STEP
    c2 = 1.0 - ADAM_B2 ** ADAM_STEP

    def body(p_ref, w_ref, m_ref, v_ref, *rest):
        g_ref, d_ref, nm_ref, nv_ref = rest[-4:]
        g = p_ref[0].astype(F32)
        for s in range(1, N_DEV):
            g = g + p_ref[s].astype(F32)
        nm = ADAM_B1 * m_ref[...] + (1.0 - ADAM_B1) * g
        nv = ADAM_B2 * v_ref[...] + (1.0 - ADAM_B2) * (g * g)
        m_hat = nm / c1
        v_hat = nv / c2
        g_ref[...] = g
        d_ref[...] = -ADAM_LR * (m_hat / (jnp.sqrt(v_hat) + ADAM_EPS) + ADAM_WD * w_ref[...])
        nm_ref[...] = nm
        nv_ref[...] = nv

    blk = pl.BlockSpec((None, tr, n), lambda i: (l, i, 0))
    out = jax.ShapeDtypeStruct(w.shape, F32)
    if ride is not None:
        assert prev is None
        return _call(body, name, (rows // tr,), [pl.BlockSpec((N_DEV, tr, n), lambda i: (0, i, 0)), blk, blk, blk],
                     [blk, blk, blk, blk], [out, out, out, out], [], ("parallel",), (parts, w, m, v), ride)
    prev = list(prev) if prev is not None else []
    return pl.pallas_call(
        body, name=name, grid=(rows // tr,),
        in_specs=[pl.BlockSpec((N_DEV, tr, n), lambda i: (0, i, 0)), blk, blk, blk] + [pl.BlockSpec(memory_space=pl.ANY)] * len(prev),
        out_specs=[blk, blk, blk, blk], out_shape=[out, out, out, out],
        input_output_aliases={4 + k: k for k in range(len(prev))},
        compiler_params=_params(("parallel",)),
    )(parts, w, m, v, *prev)


def _inproj_fwd(x, g, w, tm, tq, name):
    t, d = x.shape
    n = w.shape[1]
    nb = t // tq
    per = tm // tq

    def body(x_ref, g_ref, w_ref, ab_ref, qkv_ref, kt_ref, vt_ref):
        xx = x_ref[...]
        h = (xx * _rstd(xx) * g_ref[...]).astype(BF16)
        ab_ref[...] = _dot(h, w_ref[:, :AB_COLS])
        qkv = _dot(h, w_ref[:, AB_COLS:])
        qkv_ref[...] = qkv.astype(BF16)
        for which, out_ref in ((1, kt_ref), (2, vt_ref)):
            for p in range(SB_PAIRS):
                for b in range(per):
                    cols = which * SB_WIDTH + p * 128
                    out_ref[p, b] = qkv[b * tq:(b + 1) * tq, cols:cols + 128].T.astype(BF16)

    tb = pl.BlockSpec((SB_PAIRS, per, 128, tq), lambda i: (0, i, 0, 0))
    tshape = jax.ShapeDtypeStruct((SB_PAIRS, nb, 128, tq), BF16)
    return _call(
        body, name, (t // tm,),
        [pl.BlockSpec((tm, d), lambda i: (i, 0)), pl.BlockSpec((1, d), lambda i: (0, 0)),
         pl.BlockSpec((d, n), lambda i: (0, 0))],
        [pl.BlockSpec((tm, AB_COLS), lambda i: (i, 0)), pl.BlockSpec((tm, n - AB_COLS), lambda i: (i, 0)), tb, tb],
        [jax.ShapeDtypeStruct((t, AB_COLS), F32), jax.ShapeDtypeStruct((t, n - AB_COLS), BF16), tshape, tshape],
        [], ("parallel",), (x, g, w))


def _pool_window_sums(xx, forward):
    n = xx.shape[0]
    sh = (lambda k: n - k) if forward else (lambda k: k)
    s2 = xx + pltpu.roll(xx, sh(1), 0)
    s4 = s2 + pltpu.roll(s2, sh(2), 0)
    s8 = s4 + pltpu.roll(s4, sh(4), 0)
    s16 = s8 + pltpu.roll(s8, sh(8), 0)
    grp = lax.broadcasted_iota(jnp.int32, (1, POOL_WIDTH), 1) // POOL_GW
    return jnp.where(grp == 0, s2, jnp.where(grp == 1, s4, jnp.where(grp == 2, s8, s16)))


def _pool_count(t0, rows):
    grp = lax.broadcasted_iota(jnp.int32, (1, POOL_WIDTH), 1) // POOL_GW
    win = jnp.where(grp == 0, 2, jnp.where(grp == 1, 4, jnp.where(grp == 2, 8, 16)))
    tt = t0 + lax.broadcasted_iota(jnp.int32, (rows, 1), 0)
    return jnp.minimum(tt + 1, win).astype(F32)


def _pool_diff(cur, prev, t0):
    tm = cur.shape[0]
    sums = _pool_window_sums(jnp.concatenate([prev, cur], axis=0), False)[POOL_HALO:]
    return sums / _pool_count(t0, tm) - cur


def _pool_fwd(ab, wbd, scale, tm, name):
    t = ab.shape[0]
    hb = tm // POOL_HALO

    def body(cur_ref, prev_ref, w_ref, s_ref, y_ref):
        i = pl.program_id(0)
        prev = jnp.where(i == 0, 0.0, prev_ref[...])
        d = _pool_diff(cur_ref[...], prev, i * tm).astype(BF16)
        y_ref[...] = (_dot(d, w_ref[...]) * s_ref[...]).astype(BF16)

    return pl.pallas_call(
        body, name=name, grid=(t // tm,),
        in_specs=[pl.BlockSpec((tm, POOL_WIDTH), lambda i: (i, 0)),
                  pl.BlockSpec((POOL_HALO, POOL_WIDTH), lambda i: (jnp.maximum(i * hb - 1, 0), 0)),
                  pl.BlockSpec((POOL_WIDTH, POOL_WIDTH), lambda i: (0, 0)),
                  pl.BlockSpec((1, POOL_WIDTH), lambda i: (0, 0))],
        out_specs=pl.BlockSpec((tm, POOL_WIDTH), lambda i: (i, 0)),
        out_shape=jax.ShapeDtypeStruct((t, POOL_WIDTH), BF16),
        compiler_params=_params(("parallel",)),
    )(ab, ab, wbd, scale)


_GELU_K = 0.7978845608028654
_GELU_A = 0.044715


def _gelu(x):
    return 0.5 * x * (1.0 + jnp.tanh(_GELU_K * (x + _GELU_A * (x * x * x))))


def _gelu_grad(x):
    th = jnp.tanh(_GELU_K * (x + _GELU_A * (x * x * x)))
    return 0.5 * (1.0 + th) + 0.5 * x * (1.0 - th * th) * (_GELU_K * (1.0 + 3.0 * _GELU_A * (x * x)))


def _head_lanes(h):
    return lax.broadcasted_iota(jnp.int32, (1, SG_WIDTH), 1) // SG_HD == h


def _sg_fwd(ab, gn, wm, bfull, tm, name):
    t = ab.shape[0]

    def body(u_ref, v_ref, gn_ref, wm_ref, b_ref, y_ref):
        v = _gelu(v_ref[...])
        vn = (v * _rstd(v) * gn_ref[...]).astype(BF16)
        for c in range(tm // CHUNK):
            rows = slice(c * CHUNK, (c + 1) * CHUNK)
            vc = vn[rows]
            sv = b_ref[...]
            for h in range(SG_HEADS):
                sv = sv + jnp.where(_head_lanes(h), _dot(wm_ref[h], vc), 0.0)
            y_ref[rows, :] = (_gelu(u_ref[rows, :]) * sv).astype(BF16)

    return pl.pallas_call(
        body, name=name, grid=(t // tm,),
        in_specs=[pl.BlockSpec((tm, SG_WIDTH), lambda i: (i, 1)), pl.BlockSpec((tm, SG_WIDTH), lambda i: (i, 2)),
                  pl.BlockSpec((1, SG_WIDTH), lambda i: (0, 0)),
                  pl.BlockSpec((SG_HEADS, CHUNK, CHUNK), lambda i: (0, 0, 0)),
                  pl.BlockSpec((CHUNK, SG_WIDTH), lambda i: (0, 0))],
        out_specs=pl.BlockSpec((tm, SG_WIDTH), lambda i: (i, 0)),
        out_shape=jax.ShapeDtypeStruct((t, SG_WIDTH), BF16),
        compiler_params=_params(("parallel",)),
    )(ab, ab, gn, wm, bfull)


LOG2E = 1.4426950408889634
SB_SCALE = 0.125 * LOG2E
SB_DEAD_LOG2 = 152.0
SB_QUERY_BLOCKS_PER_STEP = 2


def _log2_sigmoids(y):
    neg_abs = lax.bitcast_convert_type(lax.bitcast_convert_type(y, jnp.uint32) | jnp.uint32(0x80000000), F32)
    lb = jnp.minimum(y, 0.0) - jnp.log(1.0 + jnp.exp2(neg_abs)) * LOG2E
    return lb, lb - y


def _split(x):
    hi = x.astype(BF16)
    return hi, (x - hi.astype(F32)).astype(BF16)


def _tri_dot(tri, x):
    hi, lo = _split(x)
    return _dot(tri, hi) + _dot(tri, lo)


def _sba_fwd(qkv, vtb, tq, name, ride=None):
    t = qkv.shape[0]
    nb = t // tq
    qb = SB_QUERY_BLOCKS_PER_STEP
    upper = (jnp.arange(tq)[None, :] > jnp.arange(tq)[:, None]).astype(BF16)

    def body(q_ref, k_ref, vt_ref, up_ref, ot_ref, c_ref, n_ref):
        pair, step = pl.program_id(0), pl.program_id(1)
        refs = (q_ref, k_ref, vt_ref, up_ref, ot_ref, c_ref, n_ref)
        subs = list(range(qb))

        @pl.when(step == 0)
        def _():
            for sub in subs:
                query_blocks([sub], pair, step, *refs)

        @pl.when(step > 0)
        def _():
            query_blocks(subs, pair, step, *refs)

    def query_blocks(subs, pair, step, q_ref, k_ref, vt_ref, up_ref, ot_ref, c_ref, n_ref):
        index = {sub: step * qb + sub for sub in subs}
        mine = {sub: slice(sub * tq, (sub + 1) * tq) for sub in subs}
        up = up_ref[...]
        lane_head = lax.broadcasted_iota(jnp.int32, (1, 128), 1) // SB_HD
        sub_head = lax.broadcasted_iota(jnp.int32, (128, 1), 0) // SB_HD
        causal = (lax.broadcasted_iota(jnp.int32, (tq, tq), 0) < lax.broadcasted_iota(jnp.int32, (tq, tq), 1))
        qh = {}
        for sub in subs:
            q = q_ref[mine[sub], :]
            qh[sub] = [jnp.where(lane_head == h, q, jnp.zeros_like(q)) for h in range(2)]

        def blocks(work, carry):
            cs = {sub: list(carry[sub][:2]) for sub in carry}
            acc = {sub: carry[sub][2] for sub in carry}
            kj = [k_ref[pl.ds(pl.multiple_of(j * tq, tq), tq), :] for _, j, _ in work]
            vt = [vt_ref[j] for _, j, _ in work]
            chains = [(w, h) for w in range(len(work)) for h in range(2)]
            z = [_dot_nt(kj[w], qh[work[w][0]][h]) for w, h in chains]
            ls = [_log2_sigmoids(zz * SB_SCALE) for zz in z]
            lb = [x[0] for x in ls]
            l1 = [jnp.where(causal, x[1], 0.0) if work[w][2] else x[1] for x, (w, h) in zip(ls, chains)]
            after = [_tri_dot(up, x) for x in l1]
            a = []
            for n, (w, h) in enumerate(chains):
                sub, j, diag = work[w]
                c_ref[h, pl.ds(j, 1), mine[sub]] = cs[sub][h]
                an = jnp.exp2(lb[n] + after[n] + cs[sub][h])
                a.append(jnp.where(causal, an, 0.0) if diag else an)
                cs[sub][h] = cs[sub][h] + after[n][0:1, :] + l1[n][0:1, :]
            alive = {sub: (jnp.max(jnp.maximum(*cs[sub])) > -SB_DEAD_LOG2).astype(jnp.int32) for sub in cs}
            for n, (w, h) in enumerate(chains):
                sub = work[w][0]
                acc[sub] = acc[sub] + _dot(jnp.where(sub_head == h, vt[w], jnp.zeros_like(vt[w])), a[n].astype(BF16))
            return {sub: (alive[sub], cs[sub][0], cs[sub][1], acc[sub]) for sub in cs}

        zero = jnp.zeros((1, tq), F32)
        start = {sub: (zero, zero, jnp.zeros((128, tq), F32)) for sub in subs}
        both = lambda: blocks([(sub, index[sub] - b, b == 0) for sub in subs for b in range(2)], start)
        if len(subs) == 1:
            joint = lax.cond(index[subs[0]] > 0, both, lambda: blocks([(subs[0], index[subs[0]], True)], start))
        else:
            joint = both()
        for sub in subs:
            i = index[sub]

            def left(state, sub=sub, i=i):
                s, _, c0, c1, acc = state
                return (s + 1,) + blocks([(sub, i - 1 - s, False)], {sub: (c0, c1, acc)})[sub]

            state = lax.while_loop(lambda st, i=i: (st[0] < i) & (st[1] > 0), left, (jnp.minimum(i, 1),) + tuple(joint[sub]))
            ot_ref[:, mine[sub]] = state[4].astype(BF16)
            n_ref[pair, i] = (state[0] + 1).astype(F32)

    return _call(
        body, name, (SB_PAIRS, nb // qb),
        [pl.BlockSpec((qb * tq, 128), lambda p, i: (i, p)),
         pl.BlockSpec((t, 128), lambda p, i: (0, SB_PAIRS + p)),
         pl.BlockSpec((None, nb, 128, tq), lambda p, i: (p, 0, 0, 0)),
         pl.BlockSpec((tq, tq), lambda p, i: (0, 0))],
        [pl.BlockSpec((128, qb * tq), lambda p, i: (p, i)),
         pl.BlockSpec((2, nb, qb * tq), lambda p, i: (p, 0, i)),
         pl.BlockSpec(memory_space=pltpu.SMEM)],
        [jax.ShapeDtypeStruct((SB_WIDTH, t), BF16),
         jax.ShapeDtypeStruct((2 * SB_PAIRS, nb, t), F32),
         jax.ShapeDtypeStruct((SB_PAIRS, nb), F32)],
        [], ("arbitrary", "arbitrary"), (qkv, qkv, vtb, upper), ride)


def _outproj_mlp_fwd(x, ya, yb, yct, wo, g2, wup, wdn, tm, name, ride=None):
    t, d = x.shape
    nf = wup.shape[0]

    def body(x_ref, ya_ref, yb_ref, yct_ref, wo_ref, g_ref, wu_ref, wd_ref, x1_ref, u_ref, x2_ref, h_ref, acc_ref):
        j = pl.program_id(1)

        @pl.when(j == 0)
        def _():
            x1 = (x_ref[...] + _dot(ya_ref[...], wo_ref[0:POOL_WIDTH, :])
                  + _dot(yb_ref[...], wo_ref[POOL_WIDTH:POOL_WIDTH + SG_WIDTH, :])
                  + _dot_tn(yct_ref[...], wo_ref[POOL_WIDTH + SG_WIDTH:, :]))
            x1_ref[...] = x1
            h_ref[...] = (x1 * _rstd(x1) * g_ref[...]).astype(BF16)
            acc_ref[...] = x1

        u = _dot(h_ref[...], wu_ref[...])
        u_ref[...] = u
        r = jnp.maximum(u, 0.0)
        acc_ref[...] += _dot((r * r).astype(BF16), wd_ref[...])

        @pl.when(j == nf - 1)
        def _():
            x2_ref[...] = acc_ref[...]

    row = lambda w: pl.BlockSpec((tm, w), lambda i, j: (i, 0))
    return _call(
        body, name, (t // tm, nf),
        [row(d), row(POOL_WIDTH), row(SG_WIDTH), pl.BlockSpec((SB_WIDTH, tm), lambda i, j: (0, i)),
         pl.BlockSpec((d, d), lambda i, j: (0, 0)), pl.BlockSpec((1, d), lambda i, j: (0, 0)),
         pl.BlockSpec((None, d, FF_SHARD), lambda i, j: (j, 0, 0)),
         pl.BlockSpec((None, FF_SHARD, d), lambda i, j: (j, 0, 0))],
        [row(d), pl.BlockSpec((tm, FF_SHARD), lambda i, j: (i, j)), row(d)],
        [jax.ShapeDtypeStruct((t, d), F32), jax.ShapeDtypeStruct((t, nf * FF_SHARD), F32),
         jax.ShapeDtypeStruct((t, d), F32)],
        [pltpu.VMEM((tm, d), BF16), pltpu.VMEM((tm, d), F32)],
        ("parallel", "arbitrary"), (x, ya, yb, yct, wo, g2, wup, wdn), ride)


def _loss_grad(x, g, target, tm, name):
    t, d = x.shape
    nt = t // tm

    def body(x_ref, g_ref, t_ref, loss_ref, dx_ref, dg_ref, sq_ref):
        i = pl.program_id(0)

        @pl.when(i == 0)
        def _():
            sq_ref[...] = jnp.zeros_like(sq_ref)
            dg_ref[...] = jnp.zeros_like(dg_ref)

        xx = x_ref[...]
        r = _rstd(xx)
        err = xx * r * g_ref[...] - t_ref[...]
        sq_ref[...] += jnp.sum(err * err, axis=0, keepdims=True)
        dx, dg = _rms_bwd(xx, r, g_ref[...], err * (1.0 / d))
        dx_ref[...] = dx
        dg_ref[...] += dg

        @pl.when(i == nt - 1)
        def _():
            loss_ref[...] = jnp.sum(sq_ref[...], axis=1, keepdims=True) * (0.5 / d)

    return pl.pallas_call(
        body, name=name, grid=(nt,),
        in_specs=[pl.BlockSpec((tm, d), lambda i: (i, 0)), pl.BlockSpec((1, d), lambda i: (0, 0)),
                  pl.BlockSpec((tm, d), lambda i: (i, 0))],
        out_specs=[pl.BlockSpec((1, 1), lambda i: (0, 0)), pl.BlockSpec((tm, d), lambda i: (i, 0)),
                   pl.BlockSpec((1, d), lambda i: (0, 0))],
        out_shape=[jax.ShapeDtypeStruct((1, 1), F32), jax.ShapeDtypeStruct((t, d), F32),
                   jax.ShapeDtypeStruct((1, d), F32)],
        scratch_shapes=[pltpu.VMEM((1, d), F32)],
        compiler_params=_params(("arbitrary",)),
    )(x, g, target)


def _mlp_bwd(dx2, x1, g2, u, wup, wdn, tm, name, ride=None):
    t, d = dx2.shape
    nf = wup.shape[0]
    nt = t // tm

    def body(dx2_ref, x1_ref, g_ref, u_ref, wu_ref, wd_ref, dx1_ref, du_ref, r_ref, h_ref, dxb_ref, dg_ref, acc_ref):
        i, j = pl.program_id(0), pl.program_id(1)

        @pl.when(j == 0)
        def _():
            x1 = x1_ref[...]
            h_ref[...] = (x1 * _rstd(x1) * g_ref[...]).astype(BF16)
            dxb_ref[...] = dx2_ref[...].astype(BF16)
            acc_ref[...] = jnp.zeros_like(acc_ref)

        @pl.when((i == 0) & (j == 0))
        def _():
            dg_ref[...] = jnp.zeros_like(dg_ref)

        dr = _dot_nt(dxb_ref[...], wd_ref[...])
        ru = jnp.maximum(u_ref[...], 0.0)
        du = (dr * (2.0 * ru)).astype(BF16)
        du_ref[...] = du
        r_ref[...] = (ru * ru).astype(BF16)
        acc_ref[...] += _dot_nt(du, wu_ref[...])

        @pl.when(j == nf - 1)
        def _():
            x1 = x1_ref[...]
            dx, dg = _rms_bwd(x1, _rstd(x1), g_ref[...], acc_ref[...])
            dx1_ref[...] = dx2_ref[...] + dx
            dg_ref[...] += dg

    row = lambda w: pl.BlockSpec((tm, w), lambda i, j: (i, 0))
    col = pl.BlockSpec((tm, FF_SHARD), lambda i, j: (i, j))
    return _call(
        body, name, (nt, nf),
        [row(d), row(d), pl.BlockSpec((1, d), lambda i, j: (0, 0)), col,
         pl.BlockSpec((None, d, FF_SHARD), lambda i, j: (j, 0, 0)),
         pl.BlockSpec((None, FF_SHARD, d), lambda i, j: (j, 0, 0))],
        [row(d), col, col, row(d), row(d), pl.BlockSpec((1, d), lambda i, j: (0, 0))],
        [jax.ShapeDtypeStruct((t, d), F32), jax.ShapeDtypeStruct((t, nf * FF_SHARD), BF16),
         jax.ShapeDtypeStruct((t, nf * FF_SHARD), BF16), jax.ShapeDtypeStruct((t, d), BF16),
         jax.ShapeDtypeStruct((t, d), BF16), jax.ShapeDtypeStruct((1, d), F32)],
        [pltpu.VMEM((tm, d), F32)], ("arbitrary", "arbitrary"), (dx2, x1, g2, u, wup, wdn), ride)


def _matmul_tn(a, b, bm, bn, bt, name, by_column_block=0, a_transposed=False):
    m, t = a.shape if a_transposed else a.shape[::-1]
    n = b.shape[1]
    nk = t // bt

    def body(a_ref, b_ref, o_ref, acc_ref):
        k = pl.program_id(2)

        @pl.when(k == 0)
        def _():
            acc_ref[...] = jnp.zeros_like(acc_ref)

        acc_ref[...] += _dot(a_ref[...], b_ref[...]) if a_transposed else _dot_tn(a_ref[...], b_ref[...])

        @pl.when(k == nk - 1)
        def _():
            if by_column_block:
                for c in range(bn // by_column_block):
                    o_ref[c] = acc_ref[:, c * by_column_block:(c + 1) * by_column_block].astype(BF16)
            else:
                o_ref[...] = acc_ref[...].astype(BF16)

    if by_column_block:
        out_spec = pl.BlockSpec((bn // by_column_block, bm, by_column_block), lambda i, j, k: (j, i, 0))
        out_shape = jax.ShapeDtypeStruct((n // by_column_block, m, by_column_block), BF16)
    else:
        out_spec = pl.BlockSpec((bm, bn), lambda i, j, k: (i, j))
        out_shape = jax.ShapeDtypeStruct((m, n), BF16)
    a_spec = (pl.BlockSpec((bm, bt), lambda i, j, k: (i, k)) if a_transposed
              else pl.BlockSpec((bt, bm), lambda i, j, k: (k, i)))
    return _call(body, name, (m // bm, n // bn, nk), [a_spec, pl.BlockSpec((bt, bn), lambda i, j, k: (k, j))],
                 out_spec, out_shape, [pltpu.VMEM((bm, bn), F32)], ("parallel", "parallel", "arbitrary"), (a, b))


def _outproj_bwd(dx1, wo, tm, name):
    t, d = dx1.shape
    c2 = POOL_WIDTH + SG_WIDTH

    def body(dx_ref, wo_ref, dya_ref, dyb_ref, dyc_ref, dyct_ref, dxb_ref):
        dxb = dx_ref[...].astype(BF16)
        dxb_ref[...] = dxb
        dya_ref[...] = _dot_nt(dxb, wo_ref[0:POOL_WIDTH, :])
        dyb_ref[...] = _dot_nt(dxb, wo_ref[POOL_WIDTH:c2, :])
        dyc_ref[...] = _dot_nt(dxb, wo_ref[c2:, :]).astype(BF16)
        dyct_ref[...] = _dot_nt(wo_ref[c2:, :], dxb).astype(BF16)

    row = lambda w: pl.BlockSpec((tm, w), lambda i: (i, 0))
    return pl.pallas_call(
        body, name=name, grid=(t // tm,),
        in_specs=[row(d), pl.BlockSpec((d, d), lambda i: (0, 0))],
        out_specs=[row(POOL_WIDTH), row(SG_WIDTH), row(SB_WIDTH), pl.BlockSpec((SB_WIDTH, tm), lambda i: (0, i)), row(d)],
        out_shape=[jax.ShapeDtypeStruct((t, POOL_WIDTH), F32), jax.ShapeDtypeStruct((t, SG_WIDTH), F32),
                   jax.ShapeDtypeStruct((t, SB_WIDTH), BF16), jax.ShapeDtypeStruct((SB_WIDTH, t), BF16),
                   jax.ShapeDtypeStruct((t, d), BF16)],
        compiler_params=_params(("parallel",)),
    )(dx1, wo)


def _sba_bwd(qkv, ktb, dyc, dyct, cc, nvis, tq, name, ride=None):
    t = qkv.shape[0]
    nb = t // tq
    idx = jnp.arange(tq)
    upper = (idx[None, :] > idx[:, None]).astype(BF16)
    lower = (idx[None, :] < idx[:, None]).astype(BF16)

    qb = SB_QUERY_BLOCKS_PER_STEP

    def body(q_ref, k_ref, v_ref, kt_ref, do_ref, dot_ref, c_ref, n_ref, up_ref, lo_ref, dqt_ref, dk_ref, dv_ref):
        pair, step = pl.program_id(0), pl.program_id(1)
        refs = (q_ref, k_ref, v_ref, kt_ref, do_ref, dot_ref, c_ref, n_ref, up_ref, lo_ref, dqt_ref, dk_ref, dv_ref)
        subs = list(range(qb))

        @pl.when(step == 0)
        def _():
            dk_ref[...] = jnp.zeros_like(dk_ref)
            dv_ref[...] = jnp.zeros_like(dv_ref)
            for sub in subs:
                query_blocks([sub], pair, step, *refs)

        @pl.when(step > 0)
        def _():
            query_blocks(subs, pair, step, *refs)

    def query_blocks(subs, pair, step, q_ref, k_ref, v_ref, kt_ref, do_ref, dot_ref, c_ref, n_ref, up_ref, lo_ref,
                     dqt_ref, dk_ref, dv_ref):
        index = {sub: step * qb + sub for sub in subs}
        mine = {sub: slice(sub * tq, (sub + 1) * tq) for sub in subs}
        up = up_ref[...]
        lo = lo_ref[...]
        lane_head = lax.broadcasted_iota(jnp.int32, (1, 128), 1) // SB_HD
        sub_head = lax.broadcasted_iota(jnp.int32, (128, 1), 0) // SB_HD
        causal = (lax.broadcasted_iota(jnp.int32, (tq, tq), 0) < lax.broadcasted_iota(jnp.int32, (tq, tq), 1))
        hms = [lane_head == h for h in range(2)]
        qh, qs, doh, dot = {}, {}, {}, {}
        for sub in subs:
            q, do = q_ref[mine[sub], :], do_ref[mine[sub], :]
            qh[sub] = [jnp.where(hm, q, jnp.zeros_like(q)) for hm in hms]
            qs[sub] = [x * 0.125 for x in qh[sub]]
            doh[sub] = [jnp.where(hm, do, jnp.zeros_like(do)) for hm in hms]
            dot[sub] = dot_ref[:, mine[sub]]

        def blocks(work, carry):
            cgs = {sub: list(carry[sub][:2]) for sub in carry}
            dqt = {sub: carry[sub][2] for sub in carry}
            rows = [pl.ds(pl.multiple_of(j * tq, tq), tq) for _, j, _ in work]
            kj = [k_ref[r, :] for r in rows]
            vj = [v_ref[r, :] for r in rows]
            kt = [kt_ref[j] for _, j, _ in work]
            chains = [(w, h) for w in range(len(work)) for h in range(2)]
            z = [_dot_nt(kj[w], qh[work[w][0]][h]) for w, h in chains]
            da = [_dot(jnp.where(hms[h], vj[w], jnp.zeros_like(vj[w])), dot[work[w][0]]) for w, h in chains]
            ls = [_log2_sigmoids(zz * SB_SCALE) for zz in z]
            lb = [x[0] for x in ls]
            l1 = [jnp.where(causal, x[1], 0.0) if work[w][2] else x[1] for x, (w, h) in zip(ls, chains)]
            after = [_tri_dot(up, x) for x in l1]
            a = [jnp.exp2(lb[n] + after[n] + c_ref[h, pl.ds(work[w][1], 1), mine[work[w][0]]])
                 for n, (w, h) in enumerate(chains)]
            a = [jnp.where(causal, a[n], 0.0) if work[w][2] else a[n] for n, (w, h) in enumerate(chains)]
            g = [a[n] * da[n] for n in range(len(chains))]
            gloc = [_dot(lo, x.astype(BF16)) for x in g]
            dzb = []
            for n, (w, h) in enumerate(chains):
                sub, _, diag = work[w]
                gsum = gloc[n] + cgs[sub][h]
                dz = g[n] - jnp.exp2(lb[n]) * (g[n] + gsum)
                dzb.append((jnp.where(causal, dz, 0.0) if diag else dz).astype(BF16))
                cgs[sub][h] = gsum[tq - 1:tq, :] + g[n][tq - 1:tq, :]
            ab = [x.astype(BF16) for x in a]
            for n, (w, h) in enumerate(chains):
                sub = work[w][0]
                dqt[sub] = dqt[sub] + _dot(jnp.where(sub_head == h, kt[w], jnp.zeros_like(kt[w])), dzb[n])
            for w, (sub, _, _) in enumerate(work):
                dk_ref[rows[w], :] += _dot(dzb[2 * w], qs[sub][0]) + _dot(dzb[2 * w + 1], qs[sub][1])
                dv_ref[rows[w], :] += _dot(ab[2 * w], doh[sub][0]) + _dot(ab[2 * w + 1], doh[sub][1])
            return {sub: (cgs[sub][0], cgs[sub][1], dqt[sub]) for sub in cgs}

        zero = jnp.zeros((1, tq), F32)
        carry = {}
        for sub in subs:
            i = index[sub]
            n = jnp.clip(n_ref[pair, i].astype(jnp.int32), 1, i + 1)
            carry[sub] = lax.fori_loop(jnp.minimum(i + 1 - n, i - 1), i - 1,
                                       lambda s, cr, sub=sub: blocks([(sub, s, False)], {sub: cr})[sub],
                                       (zero, zero, jnp.zeros((128, tq), F32)))
        both = lambda: blocks([(sub, index[sub] - 1 + b, b == 1) for sub in subs for b in range(2)], carry)
        if len(subs) == 1:
            carry = lax.cond(index[subs[0]] > 0, both, lambda: blocks([(subs[0], index[subs[0]], True)], carry))
        else:
            carry = both()
        for sub in subs:
            dqt_ref[:, mine[sub]] = carry[sub][2] * 0.125

    return _call(
        body, name, (SB_PAIRS, nb // qb),
        [pl.BlockSpec((qb * tq, 128), lambda p, i: (i, p)),
         pl.BlockSpec((t, 128), lambda p, i: (0, SB_PAIRS + p)),
         pl.BlockSpec((t, 128), lambda p, i: (0, 2 * SB_PAIRS + p)),
         pl.BlockSpec((None, nb, 128, tq), lambda p, i: (p, 0, 0, 0)),
         pl.BlockSpec((qb * tq, 128), lambda p, i: (i, p)),
         pl.BlockSpec((128, qb * tq), lambda p, i: (p, i)),
         pl.BlockSpec((2, nb, qb * tq), lambda p, i: (p, 0, i)),
         pl.BlockSpec(memory_space=pltpu.SMEM),
         pl.BlockSpec((tq, tq), lambda p, i: (0, 0)),
         pl.BlockSpec((tq, tq), lambda p, i: (0, 0))],
        [pl.BlockSpec((128, qb * tq), lambda p, i: (p, i)),
         pl.BlockSpec((t, 128), lambda p, i: (0, p)),
         pl.BlockSpec((t, 128), lambda p, i: (0, p))],
        [jax.ShapeDtypeStruct((SB_WIDTH, t), F32), jax.ShapeDtypeStruct((t, SB_WIDTH), F32),
         jax.ShapeDtypeStruct((t, SB_WIDTH), F32)],
        [], ("arbitrary", "arbitrary"), (qkv, qkv, qkv, ktb, dyc, dyct, cc, nvis, upper, lower), ride)


def _sg_bwd(ab, dyb, gn, wm, wmt, bfull, tm, name):
    t = ab.shape[0]
    nt = t // tm
    sel = (jnp.arange(SG_WIDTH)[:, None] // SG_HD == jnp.arange(CHUNK)[None, :]).astype(F32)

    def body(u_ref, v_ref, dy_ref, gn_ref, wm_ref, wmt_ref, b_ref, sel_ref,
             dup_ref, dvp_ref, dgn_ref, dw_ref, db_ref, dbacc_ref):
        i = pl.program_id(0)

        @pl.when(i == 0)
        def _():
            dgn_ref[...] = jnp.zeros_like(dgn_ref)
            dw_ref[...] = jnp.zeros_like(dw_ref)
            dbacc_ref[...] = jnp.zeros_like(dbacc_ref)

        tril = (lax.broadcasted_iota(jnp.int32, (CHUNK, CHUNK), 0) >= lax.broadcasted_iota(jnp.int32, (CHUNK, CHUNK), 1))
        gn_ = gn_ref[...]
        for c in range(tm // CHUNK):
            rows = slice(c * CHUNK, (c + 1) * CHUNK)
            up, vp, dy = u_ref[rows, :], v_ref[rows, :], dy_ref[rows, :]
            u, v = _gelu(up), _gelu(vp)
            r = _rstd(v)
            vn = (v * r * gn_).astype(BF16)
            sv = b_ref[...]
            for h in range(SG_HEADS):
                sv = sv + jnp.where(_head_lanes(h), _dot(wm_ref[h], vn), 0.0)
            dup_ref[rows, :] = dy * sv * _gelu_grad(up)
            dsv = dy * u
            dbacc_ref[...] += dsv
            dvn = jnp.zeros((CHUNK, SG_WIDTH), F32)
            for h in range(SG_HEADS):
                dsv_h = jnp.where(_head_lanes(h), dsv, 0.0).astype(BF16)
                dvn = dvn + _dot(wmt_ref[h], dsv_h)
                dw_ref[h] += jnp.where(tril, _dot_nt(dsv_h, vn), 0.0)
            dv, dgn = _rms_bwd(v, r, gn_, dvn)
            dgn_ref[...] += dgn
            dvp_ref[rows, :] = dv * _gelu_grad(vp)

        @pl.when(i == nt - 1)
        def _():
            db_ref[...] = jnp.dot(dbacc_ref[...], sel_ref[...], preferred_element_type=F32,
                                  precision=lax.Precision.HIGHEST)

    const = lambda shape: pl.BlockSpec(shape, lambda i: tuple(0 for _ in shape))
    return pl.pallas_call(
        body, name=name, grid=(nt,),
        in_specs=[pl.BlockSpec((tm, SG_WIDTH), lambda i: (i, 1)), pl.BlockSpec((tm, SG_WIDTH), lambda i: (i, 2)),
                  pl.BlockSpec((tm, SG_WIDTH), lambda i: (i, 0)), const((1, SG_WIDTH)),
                  const((SG_HEADS, CHUNK, CHUNK)), const((SG_HEADS, CHUNK, CHUNK)), const((CHUNK, SG_WIDTH)),
                  const((SG_WIDTH, CHUNK))],
        out_specs=[pl.BlockSpec((tm, SG_WIDTH), lambda i: (i, 0)), pl.BlockSpec((tm, SG_WIDTH), lambda i: (i, 0)),
                   const((1, SG_WIDTH)), const((SG_HEADS, CHUNK, CHUNK)), const((CHUNK, CHUNK))],
        out_shape=[jax.ShapeDtypeStruct((t, SG_WIDTH), F32), jax.ShapeDtypeStruct((t, SG_WIDTH), F32),
                   jax.ShapeDtypeStruct((1, SG_WIDTH), F32), jax.ShapeDtypeStruct((SG_HEADS, CHUNK, CHUNK), F32),
                   jax.ShapeDtypeStruct((CHUNK, CHUNK), F32)],
        scratch_shapes=[pltpu.VMEM((CHUNK, SG_WIDTH), F32)],
        compiler_params=_params(("arbitrary",)),
    )(ab, ab, dyb, gn, wm, wmt, bfull, sel)


def _pool_bwd(ab, dya, wbd, scale, tm, name):
    t = ab.shape[0]
    nt = t // tm
    hb = tm // POOL_HALO
    nh = t // POOL_HALO

    def body(cur_ref, prev_ref, dy_ref, dyn_ref, w_ref, s_ref, da_ref, dw_ref, ds_ref):
        i = pl.program_id(0)

        @pl.when(i == 0)
        def _():
            dw_ref[...] = jnp.zeros_like(dw_ref)
            ds_ref[...] = jnp.zeros_like(ds_ref)

        prev = jnp.where(i == 0, 0.0, prev_ref[...])
        d = _pool_diff(cur_ref[...], prev, i * tm).astype(BF16)
        dy = dy_ref[...]
        ds_ref[...] += jnp.sum(dy * _dot(d, w_ref[...]), axis=0, keepdims=True)
        dyn = jnp.where(i == nt - 1, 0.0, dyn_ref[...])
        dys = (jnp.concatenate([dy, dyn], axis=0) * s_ref[...]).astype(BF16)
        dw_ref[...] += _dot_tn(d, dys[:tm])
        dd = _dot_nt(dys, w_ref[...])
        fwd = _pool_window_sums(dd / _pool_count(i * tm, tm + POOL_HALO), True)
        da_ref[...] = fwd[:tm] - dd[:tm]

    return pl.pallas_call(
        body, name=name, grid=(nt,),
        in_specs=[pl.BlockSpec((tm, POOL_WIDTH), lambda i: (i, 0)),
                  pl.BlockSpec((POOL_HALO, POOL_WIDTH), lambda i: (jnp.maximum(i * hb - 1, 0), 0)),
                  pl.BlockSpec((tm, POOL_WIDTH), lambda i: (i, 0)),
                  pl.BlockSpec((POOL_HALO, POOL_WIDTH), lambda i: (jnp.minimum((i + 1) * hb, nh - 1), 0)),
                  pl.BlockSpec((POOL_WIDTH, POOL_WIDTH), lambda i: (0, 0)),
                  pl.BlockSpec((1, POOL_WIDTH), lambda i: (0, 0))],
        out_specs=[pl.BlockSpec((tm, POOL_WIDTH), lambda i: (i, 0)),
                   pl.BlockSpec((POOL_WIDTH, POOL_WIDTH), lambda i: (0, 0)),
                   pl.BlockSpec((1, POOL_WIDTH), lambda i: (0, 0))],
        out_shape=[jax.ShapeDtypeStruct((t, POOL_WIDTH), F32), jax.ShapeDtypeStruct((POOL_WIDTH, POOL_WIDTH), F32),
                   jax.ShapeDtypeStruct((1, POOL_WIDTH), F32)],
        compiler_params=_params(("arbitrary",)),
    )(ab, ab, dya, dya, wbd, scale)


def _inproj_bwd(dx1, x, g, da, dup, dvp, dqt, dk, dv, w, tm, name, ride=None):
    t, d = x.shape
    n = w.shape[1]
    nt = t // tm

    def body(dx1_ref, x_ref, g_ref, da_ref, du_ref, dv_ref, dqt_ref, dk_ref, dvv_ref, w_ref,
             dx_ref, h_ref, dp_ref, dg_ref):
        @pl.when(pl.program_id(0) == 0)
        def _():
            dg_ref[...] = jnp.zeros_like(dg_ref)

        dp = jnp.concatenate([da_ref[...], du_ref[...], dv_ref[...], dqt_ref[...].T, dk_ref[...], dvv_ref[...]],
                             axis=1).astype(BF16)
        dp_ref[...] = dp
        xx = x_ref[...]
        r = _rstd(xx)
        h_ref[...] = (xx * r * g_ref[...]).astype(BF16)
        dx, dg = _rms_bwd(xx, r, g_ref[...], _dot_nt(dp, w_ref[...]))
        dx_ref[...] = dx1_ref[...] + dx
        dg_ref[...] += dg

    row = lambda w_: pl.BlockSpec((tm, w_), lambda i: (i, 0))
    return _call(
        body, name, (nt,),
        [row(d), row(d), pl.BlockSpec((1, d), lambda i: (0, 0)), row(POOL_WIDTH), row(SG_WIDTH),
         row(SG_WIDTH), pl.BlockSpec((SB_WIDTH, tm), lambda i: (0, i)), row(SB_WIDTH), row(SB_WIDTH),
         pl.BlockSpec((d, n), lambda i: (0, 0))],
        [row(d), row(d), row(n), pl.BlockSpec((1, d), lambda i: (0, 0))],
        [jax.ShapeDtypeStruct((t, d), F32), jax.ShapeDtypeStruct((t, d), BF16),
         jax.ShapeDtypeStruct((t, n), BF16), jax.ShapeDtypeStruct((1, d), F32)],
        [], ("arbitrary",), (dx1, x, g, da, dup, dvp, dqt, dk, dv, w), ride)


def _full_w_in(gathered):
    return gathered.transpose(1, 0, 2).reshape(D_MODEL, IN_COLS)


_SMALL_SHAPES = ((D_MODEL,), (4, POOL_GW, POOL_GW), (POOL_WIDTH,), (SG_WIDTH,), (SG_HEADS, CHUNK, CHUNK),
                 (SG_HEADS, CHUNK), (D_MODEL,))
_SMALL_SIZES = tuple(functools.reduce(lambda p, q: p * q, shp) for shp in _SMALL_SHAPES)
_SMALL_ROWS = sum(_SMALL_SIZES) // 128
_NORM1_ROWS = D_MODEL // 128


def _pack_small_layer(arrs):
    return jnp.concatenate([a.reshape(-1) for a in arrs]).reshape(_SMALL_ROWS, 128)


def _pack_small(arrs, final):
    return jnp.concatenate([_pack_small_layer([a[l] for a in arrs]) for l in range(DEPTH)] + [final.reshape(-1, 128)])


def _unpack_small(buf):
    per_layer = []
    for l in range(DEPTH):
        flat, off, outs = buf[l * _SMALL_ROWS:(l + 1) * _SMALL_ROWS].reshape(-1), 0, []
        for shp, size in zip(_SMALL_SHAPES, _SMALL_SIZES):
            outs.append(flat[off:off + size].reshape(shp))
            off += size
        per_layer.append(outs)
    return [jnp.stack([per_layer[l][k] for l in range(DEPTH)]) for k in range(len(_SMALL_SHAPES))] + \
           [buf[DEPTH * _SMALL_ROWS:].reshape(-1)]


def _tiles(t):
    return min(512, t), min(256, t // 4), min(4096, t)


def _layer_fwd(xl, wi, wo, wu, wd, small_w, l, ride_mlp=None):
    n1, pw, ps, sn, sw, sb, n2 = small_w
    tm, tq, _ = _tiles(xl.shape[0])
    wbd = jnp.zeros((4, POOL_GW, 4, POOL_GW), F32)
    for gi in range(4):
        wbd = wbd.at[gi, :, gi, :].set(pw[gi])
    wbd = wbd.reshape(POOL_WIDTH, POOL_WIDTH).astype(BF16)
    wm = (sw * jnp.tril(jnp.ones((CHUNK, CHUNK), F32))).astype(BF16)
    wmt = wm.transpose(0, 2, 1)
    bfull = jnp.repeat(sb.T, SG_HD, axis=1)
    g1, g2, psc, gn = n1[None, :], n2[None, :], ps[None, :], sn[None, :]

    ab, qkv, ktb, vtb = _inproj_fwd(xl, g1, wi, tm, tq, f"inproj_fwd{l}")
    ya = _pool_fwd(ab, wbd, psc, tm, f"pool_fwd{l}")
    yb = _sg_fwd(ab, gn, wm, bfull, tm, f"sg_fwd{l}")
    shards = [w for w in (wo, wu, wd) if w.ndim == 2]
    if shards:
        (yct, cc, nvis), got = _sba_fwd(qkv, vtb, tq, f"sba_fwd{l}", SiblingGather(shards))
        got = iter(got)
        wo, wu, wd = (next(got) if w.ndim == 2 else w for w in (wo, wu, wd))
    else:
        yct, cc, nvis = _sba_fwd(qkv, vtb, tq, f"sba_fwd{l}")
    wo = wo.reshape(D_MODEL, D_MODEL)
    res = _outproj_mlp_fwd(xl, ya, yb, yct, wo, g2, wu, wd, min(MLP_ROWS, xl.shape[0]), f"mlp_fwd{l}", ride_mlp)
    (x1, u, x2), rode = res if ride_mlp is not None else (res, None)
    saved = dict(x=xl, ab=ab, qkv=qkv, ktb=ktb, cc=cc, nvis=nvis, ya=ya, yb=yb, yct=yct, x1=x1, u=u, wi=wi, wo=wo, wu=wu,
                 wd=wd, wbd=wbd, wm=wm, wmt=wmt, bfull=bfull, g1=g1, g2=g2, psc=psc, gn=gn)
    return x2, saved, rode


def _layer_bwd(dx, s, l, ride_mlp=None, exchange=True, gather_small=False):
    tm, tq, tw = _tiles(dx.shape[0])
    ktb = s["ktb"]
    res = _mlp_bwd(dx, s["x1"], s["g2"], s["u"], s["wu"], s["wd"], min(MLP_ROWS, dx.shape[0]), f"mlp_bwd{l}", ride_mlp)
    (dx1, du, r, h2, dx2b, dn2), rode = res if ride_mlp is not None else (res, None)
    dw_up = _matmul_tn(h2, du, D_MODEL, 2 * FF_SHARD, tw, f"dw_up{l}", by_column_block=FF_SHARD)
    dw_down = _matmul_tn(r, dx2b, 1024, D_MODEL, tw, f"dw_down{l}").reshape(N_DEV, FF_SHARD, D_MODEL)
    dya, dyb, dyc, dyct, dx1b = _outproj_bwd(dx1, s["wo"], tm, f"outproj_bwd{l}")
    dw_out = jnp.concatenate([
        _matmul_tn(jnp.concatenate([s["ya"], s["yb"]], axis=1), dx1b, POOL_WIDTH + SG_WIDTH, D_MODEL, tw, f"dw_out_ab{l}"),
        _matmul_tn(s["yct"], dx1b, SB_WIDTH, D_MODEL, tw, f"dw_out_c{l}", a_transposed=True)]
    ).reshape(N_DEV, OUT_SHARD, D_MODEL)
    if exchange:
        (dqt, dk, dv), (dw_out, dw_up, dw_down) = _sba_bwd(s["qkv"], ktb, dyc, dyct, s["cc"], s["nvis"], tq, f"sba_bwd{l}",
                                                           Exchange([dw_out, dw_up, dw_down], True))
    else:
        dqt, dk, dv = _sba_bwd(s["qkv"], ktb, dyc, dyct, s["cc"], s["nvis"], tq, f"sba_bwd{l}")
    dup, dvp, dgn, dwm, dbm = _sg_bwd(s["ab"], dyb, s["gn"], s["wm"], s["wmt"], s["bfull"], tm, f"sg_bwd{l}")
    da, dwbd, dpsc = _pool_bwd(s["ab"], dya, s["wbd"], s["psc"], tm, f"pool_bwd{l}")
    dpw = jnp.stack([dwbd[gi * POOL_GW:(gi + 1) * POOL_GW, gi * POOL_GW:(gi + 1) * POOL_GW] for gi in range(4)])
    small = _pack_small_layer([jnp.zeros((D_MODEL,), F32), dpw, dpsc[0], dgn[0], dwm, dbm[:, :SG_HEADS].T, dn2[0]])[_NORM1_ROWS:]
    res = _inproj_bwd(dx1, s["x"], s["g1"], da, dup, dvp, dqt, dk, dv, s["wi"], tm, f"inproj_bwd{l}",
                      Exchange([small], False) if gather_small else None)
    (dx, h1, dproj, dn1), small = (res[0], res[1][0]) if gather_small else (res, small)
    dw_in = _matmul_tn(h1, dproj, D_MODEL, IN_COLS // 3, tw, f"dw_in{l}")
    dw_in = dw_in.reshape(D_MODEL, N_DEV, IN_SHARD).transpose(1, 0, 2)
    return dx, (dw_in, dw_out, dw_up, dw_down), (dn1.reshape(_NORM1_ROWS, 128), small), rode


def kernel(x, norm1, w_in, pool_w, pool_scale, sg_norm, sg_w, sg_b, w_out, norm2, w_up, w_down, final_norm, loss_target, m_norm1, m_w_in, m_pool_w, m_pool_scale, m_sg_norm, m_sg_w, m_sg_b, m_w_out, m_norm2, m_w_up, m_w_down, m_final_norm, v_norm1, v_w_in, v_pool_w, v_pool_scale, v_sg_norm, v_sg_w, v_sg_b, v_w_out, v_norm2, v_w_up, v_w_down, v_final_norm):
    t = x.shape[1]
    tm = _tiles(t)[0]
    small_w = (norm1, pool_w, pool_scale, sg_norm, sg_w, sg_b, norm2)
    big_w = (w_in, w_out, w_up, w_down)
    big_m = (m_w_in, m_w_out, m_w_up, m_w_down)
    big_v = (v_w_in, v_w_out, v_w_up, v_w_down)
    shards = [[w[l].astype(BF16) for w in big_w] for l in range(DEPTH)]

    wi0 = _full_w_in(SiblingGather(shards[0][:1]).alone("gather_w_in0")[0])
    sh_in1, sh_out1, sh_up1, sh_down1 = shards[1]
    x1, s0, (wi1, wo1, wd1) = _layer_fwd(x.reshape(t, D_MODEL), wi0, *shards[0][1:], tuple(w[0] for w in small_w), 0,
                                         ride_mlp=Exchange([sh_in1, sh_out1, sh_down1], False))
    x2, s1, _ = _layer_fwd(x1, _full_w_in(wi1), wo1, sh_up1, wd1, tuple(w[1] for w in small_w), 1)
    loss_local, dx, dfinal = _loss_grad(x2, final_norm[None, :], loss_target.reshape(t, D_MODEL), tm, "loss_grad")
    loss = lax.psum(loss_local[0, 0], MESH_AXES)

    dx, parts1, small1, _ = _layer_bwd(dx, s1, 1)
    early = jnp.concatenate(list(small1) + [dfinal.reshape(-1, 128)])
    dx, parts0, (dn1, small0), (recv_in1, early) = _layer_bwd(
        dx, s0, 0, ride_mlp=Exchange([parts1[0], early], [True, False]), gather_small=True)
    grad_x = dx.reshape(x.shape)

    received1 = [recv_in1] + list(parts1[1:])
    big = [None] * 4
    for k in range(3):
        big[k] = _reduce_adamw(received1[k], big_w[k], big_m[k], big_v[k], 1, None, f"adamw{k}_1")
    big[3], (recv_in0, dn1) = _reduce_adamw(received1[3], big_w[3], big_m[3], big_v[3], 1, None, "adamw3_1",
                                            Exchange([parts0[0], dn1], [True, False]))
    small_all = jnp.concatenate([dn1, small0, early], axis=1)
    received0 = [recv_in0] + list(parts0[1:])
    for k in range(4):
        big[k] = _reduce_adamw(received0[k], big_w[k], big_m[k], big_v[k], 0, big[k], f"adamw{k}_0")

    sm = _reduce_adamw(
        small_all,
        _pack_small(small_w, final_norm)[None],
        _pack_small([m_norm1, m_pool_w, m_pool_scale, m_sg_norm, m_sg_w, m_sg_b, m_norm2], m_final_norm)[None],
        _pack_small([v_norm1, v_pool_w, v_pool_scale, v_sg_norm, v_sg_w, v_sg_b, v_norm2], v_final_norm)[None],
        0, None, "adamw_replicated")

    out = [loss, grad_x]
    for k in range(4):
        n1, pw, ps, sn, sw, sb, n2, fn = _unpack_small(sm[k][0])
        out += [n1, big[0][k], pw, ps, sn, sw, sb, big[1][k], n2, big[2][k], big[3][k], fn]
    return tuple(out)
```

```python
import functools

import jax
import jax.numpy as jnp
from jax import lax
from jax.experimental import pallas as pl
from jax.experimental.pallas import tpu as pltpu

F32 = jnp.float32
BF16 = jnp.bfloat16

D_MODEL = 1024
DEPTH = 2
POOL_WIDTH = 256
SG_WIDTH = 256
SB_WIDTH = 512
POOL_WINDOWS = (2, 4, 8, 16)
POOL_GW = 64
POOL_HALO = 16
CHUNK = 128
SG_HEADS = 4
SG_HD = 64
SB_HD = 64
SB_PAIRS = SB_WIDTH // 128
AB_COLS = POOL_WIDTH + 2 * SG_WIDTH
IN_COLS = AB_COLS + 3 * SB_WIDTH
D_FF = 4096
EPS = 1e-6
N_DEV = 8
FF_SHARD = D_FF // N_DEV
IN_SHARD = IN_COLS // N_DEV
OUT_SHARD = D_MODEL // N_DEV
ADAM_LR = 0.001
ADAM_B1 = 0.9
ADAM_B2 = 0.999
ADAM_EPS = 1e-08
ADAM_WD = 0.01
ADAM_STEP = 10
VMEM_LIMIT = 56 * 1024 * 1024
MLP_ROWS = 1024
MESH_AXES = ("x", "y", "c")


def _dot(a, b):
    return jnp.dot(a, b, preferred_element_type=F32)


def _dot_nt(a, b):
    return lax.dot_general(a, b, (((1,), (1,)), ((), ())), preferred_element_type=F32)


def _dot_tn(a, b):
    return lax.dot_general(a, b, (((0,), (0,)), ((), ())), preferred_element_type=F32)


def _rstd(x):
    return lax.rsqrt(jnp.mean(x * x, axis=-1, keepdims=True) + EPS)


def _rms_bwd(x, r, g, dh):
    gq = dh * g
    dx = r * gq - x * (r * r * r) * jnp.mean(gq * x, axis=-1, keepdims=True)
    dg = jnp.sum(dh * x * r, axis=0, keepdims=True)
    return dx, dg


def _params(sem=None):
    kw = dict(vmem_limit_bytes=VMEM_LIMIT)
    if sem is not None:
        kw["dimension_semantics"] = sem
    return pltpu.CompilerParams(**kw)


def _row_tile(rows, cap):
    best = 8
    for t in range(8, min(rows, cap) + 1, 8):
        if rows % t == 0:
            best = t
    return best


def _peer(k):
    x, y, c = lax.axis_index("x"), lax.axis_index("y"), lax.axis_index("c")
    return (1 - x if k & 4 else x, 1 - y if k & 2 else y, 1 - c if k & 1 else c)


def _my_index():
    return 4 * lax.axis_index("x") + 2 * lax.axis_index("y") + lax.axis_index("c")


class Exchange:
    def __init__(self, arrs, scatter):
        self.arrs = list(arrs)
        self.scatter = list(scatter) if isinstance(scatter, (list, tuple)) else [scatter] * len(self.arrs)
        self.n = len(self.arrs)
        self.any_specs = [pl.BlockSpec(memory_space=pl.ANY)] * self.n
        self.out_shape = [jax.ShapeDtypeStruct((N_DEV,) + a.shape[-2:], a.dtype) for a in self.arrs]
        self.sems = [pltpu.SemaphoreType.DMA((self.n, N_DEV - 1)), pltpu.SemaphoreType.DMA((self.n, N_DEV - 1)),
                     pltpu.SemaphoreType.DMA((self.n,))]

    def _copies(self, ins, outs, sems):
        send_sems, recv_sems, local_sems = sems
        me = _my_index()
        local, remote = [], []
        for a in range(self.n):
            sc = self.scatter[a]
            local.append(pltpu.make_async_copy(ins[a].at[me] if sc else ins[a], outs[a].at[me], local_sems.at[a]))
            for k in range(1, N_DEV):
                px, py, pc = _peer(k)
                remote.append(pltpu.make_async_remote_copy(
                    src_ref=ins[a].at[4 * px + 2 * py + pc] if sc else ins[a], dst_ref=outs[a].at[me],
                    send_sem=send_sems.at[a, k - 1], recv_sem=recv_sems.at[a, k - 1],
                    device_id=(px, py, pc), device_id_type=pl.DeviceIdType.MESH))
        return local, remote

    def start(self, ins, outs, sems):
        local, remote = self._copies(ins, outs, sems)
        for cp in local + remote:
            cp.start()

    def middle(self, ins, outs, sems):
        pass

    def wait(self, ins, outs, sems):
        local, remote = self._copies(ins, outs, sems)
        for cp in remote:
            cp.wait_recv()
        for cp in remote:
            cp.wait_send()
        for cp in local:
            cp.wait()

    def alone(self, name):
        n = self.n

        def body(*refs):
            ins, outs, sems = refs[:n], refs[n:2 * n], refs[2 * n:]
            self.start(ins, outs, sems)
            self.middle(ins, outs, sems)
            self.wait(ins, outs, sems)

        return pl.pallas_call(body, name=name, out_shape=self.out_shape, in_specs=self.any_specs,
                              out_specs=self.any_specs, scratch_shapes=self.sems)(*self.arrs)


class SiblingGather(Exchange):
    def __init__(self, arrs):
        super().__init__(arrs, False)

    def _plan(self, ins, outs, sems):
        send_sems, recv_sems, local_sems = sems
        x, y, core = lax.axis_index("x"), lax.axis_index("y"), lax.axis_index("c")
        me, sibling = (x, y, core), (x, y, 1 - core)
        chips = [(1 - x, y), (x, 1 - y), (1 - x, 1 - y)]
        plan = dict(local=[], first=[], arrived=[], passed=[], late=[])
        for a in range(self.n):
            slot = lambda p, a=a: outs[a].at[4 * p[0] + 2 * p[1] + p[2]]

            def copy(k, block, to, src=None, a=a, slot=slot):
                return pltpu.make_async_remote_copy(
                    src_ref=slot(block) if src is None else src, dst_ref=slot(block), send_sem=send_sems.at[a, k],
                    recv_sem=recv_sems.at[a, k], device_id=to, device_id_type=pl.DeviceIdType.MESH)

            plan["local"].append(pltpu.make_async_copy(ins[a], slot(me), local_sems.at[a]))
            plan["first"] += [copy(0, me, sibling, src=ins[a])] + [copy(1 + j, me, (*ch, core), src=ins[a])
                                                                  for j, ch in enumerate(chips)]
            plan["arrived"] += [copy(1 + j, (*ch, core), me) for j, ch in enumerate(chips)]
            plan["passed"] += [copy(4 + j, (*ch, core), sibling) for j, ch in enumerate(chips)]
            plan["late"] += [copy(0, sibling, me)] + [copy(4 + j, (*ch, 1 - core), me) for j, ch in enumerate(chips)]
        return plan

    def start(self, ins, outs, sems):
        plan = self._plan(ins, outs, sems)
        for cp in plan["local"] + plan["first"]:
            cp.start()

    def middle(self, ins, outs, sems):
        plan = self._plan(ins, outs, sems)
        for arrived, passed in zip(plan["arrived"], plan["passed"]):
            arrived.wait_recv()
            passed.start()

    def wait(self, ins, outs, sems):
        plan = self._plan(ins, outs, sems)
        for cp in plan["late"]:
            cp.wait_recv()
        for cp in plan["first"] + plan["passed"]:
            cp.wait_send()
        for cp in plan["local"]:
            cp.wait()


def _exchange(arrs, scatter, name):
    return Exchange(arrs, scatter).alone(name)


def _call(body, name, grid, in_specs, out_specs, out_shape, scratch_shapes, semantics, args, ride=None):
    if ride is None:
        return pl.pallas_call(body, name=name, grid=grid, in_specs=in_specs, out_specs=out_specs, out_shape=out_shape,
                              scratch_shapes=scratch_shapes, compiler_params=_params(semantics))(*args)
    single = not isinstance(out_shape, (list, tuple))
    out_specs, out_shape = ([out_specs], [out_shape]) if single else (list(out_specs), list(out_shape))
    n_in, n_out, n_scr, n = len(in_specs), len(out_specs), len(scratch_shapes), ride.n

    def riding(*refs):
        ins, cins = refs[:n_in], refs[n_in:n_in + n]
        outs, couts = refs[n_in + n:n_in + n + n_out], refs[n_in + n + n_out:n_in + 2 * n + n_out]
        scr, sems = refs[n_in + 2 * n + n_out:n_in + 2 * n + n_out + n_scr], refs[n_in + 2 * n + n_out + n_scr:]
        step = functools.reduce(lambda s, a: s * grid[a] + pl.program_id(a), range(len(grid)), 0)
        steps = functools.reduce(lambda p, q: p * q, grid)

        @pl.when(step == 0)
        def _():
            ride.start(cins, couts, sems)

        @pl.when(step == (3 * steps) // 4)
        def _():
            ride.middle(cins, couts, sems)

        body(*ins, *outs, *scr)

        @pl.when(step == steps - 1)
        def _():
            ride.wait(cins, couts, sems)

    res = pl.pallas_call(
        riding, name=name, grid=grid, in_specs=list(in_specs) + ride.any_specs, out_specs=out_specs + ride.any_specs,
        out_shape=out_shape + ride.out_shape, scratch_shapes=list(scratch_shapes) + ride.sems,
        compiler_params=_params(("arbitrary",) * len(grid)))(*args, *ride.arrs)
    own = res[0] if single else list(res[:n_out])
    return own, list(res[n_out:])


def _reduce_adamw(parts, w, m, v, l, prev, name):
    _, rows, n = parts.shape
    tr = _row_tile(rows, max(8, (1 << 18) // n))
    c1 = 1.0 - ADAM_B1 ** ADAM_STEP
    c2 = 1.0 - ADAM_B2 ** ADAM_STEP

    def body(p_ref, w_ref, m_ref, v_ref, *rest):
        g_ref, d_ref, nm_ref, nv_ref = rest[-4:]
        g = p_ref[0].astype(F32)
        for s in range(1, N_DEV):
            g = g + p_ref[s].astype(F32)
        nm = ADAM_B1 * m_ref[...] + (1.0 - ADAM_B1) * g
        nv = ADAM_B2 * v_ref[...] + (1.0 - ADAM_B2) * (g * g)
        m_hat = nm / c1
        v_hat = nv / c2
        g_ref[...] = g
        d_ref[...] = -ADAM_LR * (m_hat / (jnp.sqrt(v_hat) + ADAM_EPS) + ADAM_WD * w_ref[...])
        nm_ref[...] = nm
        nv_ref[...] = nv

    blk = pl.BlockSpec((None, tr, n), lambda i: (l, i, 0))
    out = jax.ShapeDtypeStruct(w.shape, F32)
    prev = list(prev) if prev is not None else []
    return pl.pallas_call(
        body, name=name, grid=(rows // tr,),
        in_specs=[pl.BlockSpec((N_DEV, tr, n), lambda i: (0, i, 0)), blk, blk, blk] + [pl.BlockSpec(memory_space=pl.ANY)] * len(prev),
        out_specs=[blk, blk, blk, blk], out_shape=[out, out, out, out],
        input_output_aliases={4 + k: k for k in range(len(prev))},
        compiler_params=_params(("parallel",)),
    )(parts, w, m, v, *prev)


def _inproj_fwd(x, g, w, tm, tq, name):
    t, d = x.shape
    n = w.shape[1]
    nb = t // tq
    per = tm // tq

    def body(x_ref, g_ref, w_ref, ab_ref, qkv_ref, kt_ref, vt_ref):
        xx = x_ref[...]
        h = (xx * _rstd(xx) * g_ref[...]).astype(BF16)
        ab_ref[...] = _dot(h, w_ref[:, :AB_COLS])
        qkv = _dot(h, w_ref[:, AB_COLS:])
        qkv_ref[...] = qkv.astype(BF16)
        for which, out_ref in ((1, kt_ref), (2, vt_ref)):
            for p in range(SB_PAIRS):
                for b in range(per):
                    cols = which * SB_WIDTH + p * 128
                    out_ref[p, b] = qkv[b * tq:(b + 1) * tq, cols:cols + 128].T.astype(BF16)

    tb = pl.BlockSpec((SB_PAIRS, per, 128, tq), lambda i: (0, i, 0, 0))
    tshape = jax.ShapeDtypeStruct((SB_PAIRS, nb, 128, tq), BF16)
    return _call(
        body, name, (t // tm,),
        [pl.BlockSpec((tm, d), lambda i: (i, 0)), pl.BlockSpec((1, d), lambda i: (0, 0)),
         pl.BlockSpec((d, n), lambda i: (0, 0))],
        [pl.BlockSpec((tm, AB_COLS), lambda i: (i, 0)), pl.BlockSpec((tm, n - AB_COLS), lambda i: (i, 0)), tb, tb],
        [jax.ShapeDtypeStruct((t, AB_COLS), F32), jax.ShapeDtypeStruct((t, n - AB_COLS), BF16), tshape, tshape],
        [], ("parallel",), (x, g, w))


def _pool_window_sums(xx, forward):
    n = xx.shape[0]
    sh = (lambda k: n - k) if forward else (lambda k: k)
    s2 = xx + pltpu.roll(xx, sh(1), 0)
    s4 = s2 + pltpu.roll(s2, sh(2), 0)
    s8 = s4 + pltpu.roll(s4, sh(4), 0)
    s16 = s8 + pltpu.roll(s8, sh(8), 0)
    grp = lax.broadcasted_iota(jnp.int32, (1, POOL_WIDTH), 1) // POOL_GW
    return jnp.where(grp == 0, s2, jnp.where(grp == 1, s4, jnp.where(grp == 2, s8, s16)))


def _pool_count(t0, rows):
    grp = lax.broadcasted_iota(jnp.int32, (1, POOL_WIDTH), 1) // POOL_GW
    win = jnp.where(grp == 0, 2, jnp.where(grp == 1, 4, jnp.where(grp == 2, 8, 16)))
    tt = t0 + lax.broadcasted_iota(jnp.int32, (rows, 1), 0)
    return jnp.minimum(tt + 1, win).astype(F32)


def _pool_diff(cur, prev, t0):
    tm = cur.shape[0]
    sums = _pool_window_sums(jnp.concatenate([prev, cur], axis=0), False)[POOL_HALO:]
    return sums / _pool_count(t0, tm) - cur


def _pool_fwd(ab, wbd, scale, tm, name):
    t = ab.shape[0]
    hb = tm // POOL_HALO

    def body(cur_ref, prev_ref, w_ref, s_ref, y_ref):
        i = pl.program_id(0)
        prev = jnp.where(i == 0, 0.0, prev_ref[...])
        d = _pool_diff(cur_ref[...], prev, i * tm).astype(BF16)
        y_ref[...] = (_dot(d, w_ref[...]) * s_ref[...]).astype(BF16)

    return pl.pallas_call(
        body, name=name, grid=(t // tm,),
        in_specs=[pl.BlockSpec((tm, POOL_WIDTH), lambda i: (i, 0)),
                  pl.BlockSpec((POOL_HALO, POOL_WIDTH), lambda i: (jnp.maximum(i * hb - 1, 0), 0)),
                  pl.BlockSpec((POOL_WIDTH, POOL_WIDTH), lambda i: (0, 0)),
                  pl.BlockSpec((1, POOL_WIDTH), lambda i: (0, 0))],
        out_specs=pl.BlockSpec((tm, POOL_WIDTH), lambda i: (i, 0)),
        out_shape=jax.ShapeDtypeStruct((t, POOL_WIDTH), BF16),
        compiler_params=_params(("parallel",)),
    )(ab, ab, wbd, scale)


_GELU_K = 0.7978845608028654
_GELU_A = 0.044715


def _gelu(x):
    return 0.5 * x * (1.0 + jnp.tanh(_GELU_K * (x + _GELU_A * (x * x * x))))


def _gelu_grad(x):
    th = jnp.tanh(_GELU_K * (x + _GELU_A * (x * x * x)))
    return 0.5 * (1.0 + th) + 0.5 * x * (1.0 - th * th) * (_GELU_K * (1.0 + 3.0 * _GELU_A * (x * x)))


def _head_lanes(h):
    return lax.broadcasted_iota(jnp.int32, (1, SG_WIDTH), 1) // SG_HD == h


def _sg_fwd(ab, gn, wm, bfull, tm, name):
    t = ab.shape[0]

    def body(u_ref, v_ref, gn_ref, wm_ref, b_ref, y_ref):
        v = _gelu(v_ref[...])
        vn = (v * _rstd(v) * gn_ref[...]).astype(BF16)
        for c in range(tm // CHUNK):
            rows = slice(c * CHUNK, (c + 1) * CHUNK)
            vc = vn[rows]
            sv = b_ref[...]
            for h in range(SG_HEADS):
                sv = sv + jnp.where(_head_lanes(h), _dot(wm_ref[h], vc), 0.0)
            y_ref[rows, :] = (_gelu(u_ref[rows, :]) * sv).astype(BF16)

    return pl.pallas_call(
        body, name=name, grid=(t // tm,),
        in_specs=[pl.BlockSpec((tm, SG_WIDTH), lambda i: (i, 1)), pl.BlockSpec((tm, SG_WIDTH), lambda i: (i, 2)),
                  pl.BlockSpec((1, SG_WIDTH), lambda i: (0, 0)),
                  pl.BlockSpec((SG_HEADS, CHUNK, CHUNK), lambda i: (0, 0, 0)),
                  pl.BlockSpec((CHUNK, SG_WIDTH), lambda i: (0, 0))],
        out_specs=pl.BlockSpec((tm, SG_WIDTH), lambda i: (i, 0)),
        out_shape=jax.ShapeDtypeStruct((t, SG_WIDTH), BF16),
        compiler_params=_params(("parallel",)),
    )(ab, ab, gn, wm, bfull)


LOG2E = 1.4426950408889634
SB_SCALE = 0.125 * LOG2E
SB_DEAD_LOG2 = 152.0
SB_QUERY_BLOCKS_PER_STEP = 2


def _log2_sigmoids(y):
    neg_abs = lax.bitcast_convert_type(lax.bitcast_convert_type(y, jnp.uint32) | jnp.uint32(0x80000000), F32)
    lb = jnp.minimum(y, 0.0) - jnp.log(1.0 + jnp.exp2(neg_abs)) * LOG2E
    return lb, lb - y


def _split(x):
    hi = x.astype(BF16)
    return hi, (x - hi.astype(F32)).astype(BF16)


def _tri_dot(tri, x):
    hi, lo = _split(x)
    return _dot(tri, hi) + _dot(tri, lo)


def _sba_fwd(qkv, vtb, tq, name, ride=None):
    t = qkv.shape[0]
    nb = t // tq
    qb = SB_QUERY_BLOCKS_PER_STEP
    upper = (jnp.arange(tq)[None, :] > jnp.arange(tq)[:, None]).astype(BF16)

    def body(q_ref, k_ref, vt_ref, up_ref, ot_ref, c_ref, n_ref):
        pair, step = pl.program_id(0), pl.program_id(1)
        refs = (q_ref, k_ref, vt_ref, up_ref, ot_ref, c_ref, n_ref)
        subs = list(range(qb))

        @pl.when(step == 0)
        def _():
            for sub in subs:
                query_blocks([sub], pair, step, *refs)

        @pl.when(step > 0)
        def _():
            query_blocks(subs, pair, step, *refs)

    def query_blocks(subs, pair, step, q_ref, k_ref, vt_ref, up_ref, ot_ref, c_ref, n_ref):
        index = {sub: step * qb + sub for sub in subs}
        mine = {sub: slice(sub * tq, (sub + 1) * tq) for sub in subs}
        up = up_ref[...]
        lane_head = lax.broadcasted_iota(jnp.int32, (1, 128), 1) // SB_HD
        sub_head = lax.broadcasted_iota(jnp.int32, (128, 1), 0) // SB_HD
        causal = (lax.broadcasted_iota(jnp.int32, (tq, tq), 0) < lax.broadcasted_iota(jnp.int32, (tq, tq), 1))
        qh = {}
        for sub in subs:
            q = q_ref[mine[sub], :]
            qh[sub] = [jnp.where(lane_head == h, q, jnp.zeros_like(q)) for h in range(2)]

        def blocks(work, carry):
            cs = {sub: list(carry[sub][:2]) for sub in carry}
            acc = {sub: carry[sub][2] for sub in carry}
            kj = [k_ref[pl.ds(pl.multiple_of(j * tq, tq), tq), :] for _, j, _ in work]
            vt = [vt_ref[j] for _, j, _ in work]
            chains = [(w, h) for w in range(len(work)) for h in range(2)]
            z = [_dot_nt(kj[w], qh[work[w][0]][h]) for w, h in chains]
            ls = [_log2_sigmoids(zz * SB_SCALE) for zz in z]
            lb = [x[0] for x in ls]
            l1 = [jnp.where(causal, x[1], 0.0) if work[w][2] else x[1] for x, (w, h) in zip(ls, chains)]
            after = [_tri_dot(up, x) for x in l1]
            a = []
            for n, (w, h) in enumerate(chains):
                sub, j, diag = work[w]
                c_ref[h, pl.ds(j, 1), mine[sub]] = cs[sub][h]
                an = jnp.exp2(lb[n] + after[n] + cs[sub][h])
                a.append(jnp.where(causal, an, 0.0) if diag else an)
                cs[sub][h] = cs[sub][h] + after[n][0:1, :] + l1[n][0:1, :]
            alive = {sub: (jnp.max(jnp.maximum(*cs[sub])) > -SB_DEAD_LOG2).astype(jnp.int32) for sub in cs}
            for n, (w, h) in enumerate(chains):
                sub = work[w][0]
                acc[sub] = acc[sub] + _dot(jnp.where(sub_head == h, vt[w], jnp.zeros_like(vt[w])), a[n].astype(BF16))
            return {sub: (alive[sub], cs[sub][0], cs[sub][1], acc[sub]) for sub in cs}

        zero = jnp.zeros((1, tq), F32)
        start = {sub: (zero, zero, jnp.zeros((128, tq), F32)) for sub in subs}
        both = lambda: blocks([(sub, index[sub] - b, b == 0) for sub in subs for b in range(2)], start)
        if len(subs) == 1:
            joint = lax.cond(index[subs[0]] > 0, both, lambda: blocks([(subs[0], index[subs[0]], True)], start))
        else:
            joint = both()
        for sub in subs:
            i = index[sub]

            def left(state, sub=sub, i=i):
                s, _, c0, c1, acc = state
                return (s + 1,) + blocks([(sub, i - 1 - s, False)], {sub: (c0, c1, acc)})[sub]

            state = lax.while_loop(lambda st, i=i: (st[0] < i) & (st[1] > 0), left, (jnp.minimum(i, 1),) + tuple(joint[sub]))
            ot_ref[:, mine[sub]] = state[4].astype(BF16)
            n_ref[pair, i] = (state[0] + 1).astype(F32)

    return _call(
        body, name, (SB_PAIRS, nb // qb),
        [pl.BlockSpec((qb * tq, 128), lambda p, i: (i, p)),
         pl.BlockSpec((t, 128), lambda p, i: (0, SB_PAIRS + p)),
         pl.BlockSpec((None, nb, 128, tq), lambda p, i: (p, 0, 0, 0)),
         pl.BlockSpec((tq, tq), lambda p, i: (0, 0))],
        [pl.BlockSpec((128, qb * tq), lambda p, i: (p, i)),
         pl.BlockSpec((2, nb, qb * tq), lambda p, i: (p, 0, i)),
         pl.BlockSpec(memory_space=pltpu.SMEM)],
        [jax.ShapeDtypeStruct((SB_WIDTH, t), BF16),
         jax.ShapeDtypeStruct((2 * SB_PAIRS, nb, t), F32),
         jax.ShapeDtypeStruct((SB_PAIRS, nb), F32)],
        [], ("arbitrary", "arbitrary"), (qkv, qkv, vtb, upper), ride)


def _outproj_mlp_fwd(x, ya, yb, yct, wo, g2, wup, wdn, tm, name, ride=None):
    t, d = x.shape
    nf = wup.shape[0]

    def body(x_ref, ya_ref, yb_ref, yct_ref, wo_ref, g_ref, wu_ref, wd_ref, x1_ref, u_ref, x2_ref, h_ref, acc_ref):
        j = pl.program_id(1)

        @pl.when(j == 0)
        def _():
            x1 = (x_ref[...] + _dot(ya_ref[...], wo_ref[0:POOL_WIDTH, :])
                  + _dot(yb_ref[...], wo_ref[POOL_WIDTH:POOL_WIDTH + SG_WIDTH, :])
                  + _dot_tn(yct_ref[...], wo_ref[POOL_WIDTH + SG_WIDTH:, :]))
            x1_ref[...] = x1
            h_ref[...] = (x1 * _rstd(x1) * g_ref[...]).astype(BF16)
            acc_ref[...] = x1

        u = _dot(h_ref[...], wu_ref[...])
        u_ref[...] = u
        r = jnp.maximum(u, 0.0)
        acc_ref[...] += _dot((r * r).astype(BF16), wd_ref[...])

        @pl.when(j == nf - 1)
        def _():
            x2_ref[...] = acc_ref[...]

    row = lambda w: pl.BlockSpec((tm, w), lambda i, j: (i, 0))
    return _call(
        body, name, (t // tm, nf),
        [row(d), row(POOL_WIDTH), row(SG_WIDTH), pl.BlockSpec((SB_WIDTH, tm), lambda i, j: (0, i)),
         pl.BlockSpec((d, d), lambda i, j: (0, 0)), pl.BlockSpec((1, d), lambda i, j: (0, 0)),
         pl.BlockSpec((None, d, FF_SHARD), lambda i, j: (j, 0, 0)),
         pl.BlockSpec((None, FF_SHARD, d), lambda i, j: (j, 0, 0))],
        [row(d), pl.BlockSpec((tm, FF_SHARD), lambda i, j: (i, j)), row(d)],
        [jax.ShapeDtypeStruct((t, d), F32), jax.ShapeDtypeStruct((t, nf * FF_SHARD), F32),
         jax.ShapeDtypeStruct((t, d), F32)],
        [pltpu.VMEM((tm, d), BF16), pltpu.VMEM((tm, d), F32)],
        ("parallel", "arbitrary"), (x, ya, yb, yct, wo, g2, wup, wdn), ride)


def _loss_grad(x, g, target, tm, name):
    t, d = x.shape
    nt = t // tm

    def body(x_ref, g_ref, t_ref, loss_ref, dx_ref, dg_ref, sq_ref):
        i = pl.program_id(0)

        @pl.when(i == 0)
        def _():
            sq_ref[...] = jnp.zeros_like(sq_ref)
            dg_ref[...] = jnp.zeros_like(dg_ref)

        xx = x_ref[...]
        r = _rstd(xx)
        err = xx * r * g_ref[...] - t_ref[...]
        sq_ref[...] += jnp.sum(err * err, axis=0, keepdims=True)
        dx, dg = _rms_bwd(xx, r, g_ref[...], err * (1.0 / d))
        dx_ref[...] = dx
        dg_ref[...] += dg

        @pl.when(i == nt - 1)
        def _():
            loss_ref[...] = jnp.sum(sq_ref[...], axis=1, keepdims=True) * (0.5 / d)

    return pl.pallas_call(
        body, name=name, grid=(nt,),
        in_specs=[pl.BlockSpec((tm, d), lambda i: (i, 0)), pl.BlockSpec((1, d), lambda i: (0, 0)),
                  pl.BlockSpec((tm, d), lambda i: (i, 0))],
        out_specs=[pl.BlockSpec((1, 1), lambda i: (0, 0)), pl.BlockSpec((tm, d), lambda i: (i, 0)),
                   pl.BlockSpec((1, d), lambda i: (0, 0))],
        out_shape=[jax.ShapeDtypeStruct((1, 1), F32), jax.ShapeDtypeStruct((t, d), F32),
                   jax.ShapeDtypeStruct((1, d), F32)],
        scratch_shapes=[pltpu.VMEM((1, d), F32)],
        compiler_params=_params(("arbitrary",)),
    )(x, g, target)


def _mlp_bwd(dx2, x1, g2, u, wup, wdn, tm, name, ride=None):
    t, d = dx2.shape
    nf = wup.shape[0]
    nt = t // tm

    def body(dx2_ref, x1_ref, g_ref, u_ref, wu_ref, wd_ref, dx1_ref, du_ref, r_ref, h_ref, dxb_ref, dg_ref, acc_ref):
        i, j = pl.program_id(0), pl.program_id(1)

        @pl.when(j == 0)
        def _():
            dxb_ref[...] = dx2_ref[...].astype(BF16)
            acc_ref[...] = jnp.zeros_like(acc_ref)

        @pl.when((i == 0) & (j == 0))
        def _():
            dg_ref[...] = jnp.zeros_like(dg_ref)

        share = pl.ds(pl.multiple_of(j * (tm // nf), tm // nf), tm // nf)
        xs = x1_ref[share, :]
        h_ref[share, :] = (xs * _rstd(xs) * g_ref[...]).astype(BF16)
        dr = _dot_nt(dxb_ref[...], wd_ref[...])
        ru = jnp.maximum(u_ref[...], 0.0)
        du = (dr * (2.0 * ru)).astype(BF16)
        du_ref[...] = du
        r_ref[...] = (ru * ru).astype(BF16)
        acc_ref[...] += _dot_nt(du, wu_ref[...])

        @pl.when(j == nf - 1)
        def _():
            x1 = x1_ref[...]
            dx, dg = _rms_bwd(x1, _rstd(x1), g_ref[...], acc_ref[...])
            dx1_ref[...] = dx2_ref[...] + dx
            dg_ref[...] += dg

    row = lambda w: pl.BlockSpec((tm, w), lambda i, j: (i, 0))
    col = pl.BlockSpec((tm, FF_SHARD), lambda i, j: (i, j))
    return _call(
        body, name, (nt, nf),
        [row(d), row(d), pl.BlockSpec((1, d), lambda i, j: (0, 0)), col,
         pl.BlockSpec((None, d, FF_SHARD), lambda i, j: (j, 0, 0)),
         pl.BlockSpec((None, FF_SHARD, d), lambda i, j: (j, 0, 0))],
        [row(d), col, col, row(d), row(d), pl.BlockSpec((1, d), lambda i, j: (0, 0))],
        [jax.ShapeDtypeStruct((t, d), F32), jax.ShapeDtypeStruct((t, nf * FF_SHARD), BF16),
         jax.ShapeDtypeStruct((t, nf * FF_SHARD), BF16), jax.ShapeDtypeStruct((t, d), BF16),
         jax.ShapeDtypeStruct((t, d), BF16), jax.ShapeDtypeStruct((1, d), F32)],
        [pltpu.VMEM((tm, d), F32)], ("arbitrary", "arbitrary"), (dx2, x1, g2, u, wup, wdn), ride)


def _matmul_tn(a, b, bm, bn, bt, name, by_column_block=0, a_transposed=False):
    m, t = a.shape if a_transposed else a.shape[::-1]
    n = b.shape[1]
    nk = t // bt

    def body(a_ref, b_ref, o_ref, acc_ref):
        k = pl.program_id(2)

        @pl.when(k == 0)
        def _():
            acc_ref[...] = jnp.zeros_like(acc_ref)

        acc_ref[...] += _dot(a_ref[...], b_ref[...]) if a_transposed else _dot_tn(a_ref[...], b_ref[...])

        @pl.when(k == nk - 1)
        def _():
            if by_column_block:
                for c in range(bn // by_column_block):
                    o_ref[c] = acc_ref[:, c * by_column_block:(c + 1) * by_column_block].astype(BF16)
            else:
                o_ref[...] = acc_ref[...].astype(BF16)

    if by_column_block:
        out_spec = pl.BlockSpec((bn // by_column_block, bm, by_column_block), lambda i, j, k: (j, i, 0))
        out_shape = jax.ShapeDtypeStruct((n // by_column_block, m, by_column_block), BF16)
    else:
        out_spec = pl.BlockSpec((bm, bn), lambda i, j, k: (i, j))
        out_shape = jax.ShapeDtypeStruct((m, n), BF16)
    a_spec = (pl.BlockSpec((bm, bt), lambda i, j, k: (i, k)) if a_transposed
              else pl.BlockSpec((bt, bm), lambda i, j, k: (k, i)))
    return _call(body, name, (m // bm, n // bn, nk), [a_spec, pl.BlockSpec((bt, bn), lambda i, j, k: (k, j))],
                 out_spec, out_shape, [pltpu.VMEM((bm, bn), F32)], ("parallel", "parallel", "arbitrary"), (a, b))


def _outproj_bwd(dx1, wo, tm, name):
    t, d = dx1.shape
    c2 = POOL_WIDTH + SG_WIDTH

    def body(dx_ref, wo_ref, dya_ref, dyb_ref, dyc_ref, dyct_ref, dxb_ref):
        dxb = dx_ref[...].astype(BF16)
        dxb_ref[...] = dxb
        dya_ref[...] = _dot_nt(dxb, wo_ref[0:POOL_WIDTH, :])
        dyb_ref[...] = _dot_nt(dxb, wo_ref[POOL_WIDTH:c2, :])
        dyc_ref[...] = _dot_nt(dxb, wo_ref[c2:, :]).astype(BF16)
        dyct_ref[...] = _dot_nt(wo_ref[c2:, :], dxb).astype(BF16)

    row = lambda w: pl.BlockSpec((tm, w), lambda i: (i, 0))
    return pl.pallas_call(
        body, name=name, grid=(t // tm,),
        in_specs=[row(d), pl.BlockSpec((d, d), lambda i: (0, 0))],
        out_specs=[row(POOL_WIDTH), row(SG_WIDTH), row(SB_WIDTH), pl.BlockSpec((SB_WIDTH, tm), lambda i: (0, i)), row(d)],
        out_shape=[jax.ShapeDtypeStruct((t, POOL_WIDTH), F32), jax.ShapeDtypeStruct((t, SG_WIDTH), F32),
                   jax.ShapeDtypeStruct((t, SB_WIDTH), BF16), jax.ShapeDtypeStruct((SB_WIDTH, t), BF16),
                   jax.ShapeDtypeStruct((t, d), BF16)],
        compiler_params=_params(("parallel",)),
    )(dx1, wo)


def _sba_bwd(qkv, ktb, dyc, dyct, cc, nvis, tq, name, ride=None):
    t = qkv.shape[0]
    nb = t // tq
    idx = jnp.arange(tq)
    upper = (idx[None, :] > idx[:, None]).astype(BF16)
    lower = (idx[None, :] < idx[:, None]).astype(BF16)

    qb = SB_QUERY_BLOCKS_PER_STEP

    def body(q_ref, k_ref, v_ref, kt_ref, do_ref, dot_ref, c_ref, n_ref, up_ref, lo_ref, dqt_ref, dk_ref, dv_ref):
        pair, step = pl.program_id(0), pl.program_id(1)
        refs = (q_ref, k_ref, v_ref, kt_ref, do_ref, dot_ref, c_ref, n_ref, up_ref, lo_ref, dqt_ref, dk_ref, dv_ref)
        subs = list(range(qb))

        @pl.when(step == 0)
        def _():
            dk_ref[...] = jnp.zeros_like(dk_ref)
            dv_ref[...] = jnp.zeros_like(dv_ref)
            for sub in subs:
                query_blocks([sub], pair, step, *refs)

        @pl.when(step > 0)
        def _():
            query_blocks(subs, pair, step, *refs)

    def query_blocks(subs, pair, step, q_ref, k_ref, v_ref, kt_ref, do_ref, dot_ref, c_ref, n_ref, up_ref, lo_ref,
                     dqt_ref, dk_ref, dv_ref):
        index = {sub: step * qb + sub for sub in subs}
        mine = {sub: slice(sub * tq, (sub + 1) * tq) for sub in subs}
        up = up_ref[...]
        lo = lo_ref[...]
        lane_head = lax.broadcasted_iota(jnp.int32, (1, 128), 1) // SB_HD
        sub_head = lax.broadcasted_iota(jnp.int32, (128, 1), 0) // SB_HD
        causal = (lax.broadcasted_iota(jnp.int32, (tq, tq), 0) < lax.broadcasted_iota(jnp.int32, (tq, tq), 1))
        hms = [lane_head == h for h in range(2)]
        qh, qs, doh, dot = {}, {}, {}, {}
        for sub in subs:
            q, do = q_ref[mine[sub], :], do_ref[mine[sub], :]
            qh[sub] = [jnp.where(hm, q, jnp.zeros_like(q)) for hm in hms]
            qs[sub] = [x * 0.125 for x in qh[sub]]
            doh[sub] = [jnp.where(hm, do, jnp.zeros_like(do)) for hm in hms]
            dot[sub] = dot_ref[:, mine[sub]]

        def blocks(work, carry):
            cgs = {sub: list(carry[sub][:2]) for sub in carry}
            dqt = {sub: carry[sub][2] for sub in carry}
            rows = [pl.ds(pl.multiple_of(j * tq, tq), tq) for _, j, _ in work]
            kj = [k_ref[r, :] for r in rows]
            vj = [v_ref[r, :] for r in rows]
            kt = [kt_ref[j] for _, j, _ in work]
            chains = [(w, h) for w in range(len(work)) for h in range(2)]
            z = [_dot_nt(kj[w], qh[work[w][0]][h]) for w, h in chains]
            da = [_dot(jnp.where(hms[h], vj[w], jnp.zeros_like(vj[w])), dot[work[w][0]]) for w, h in chains]
            ls = [_log2_sigmoids(zz * SB_SCALE) for zz in z]
            lb = [x[0] for x in ls]
            l1 = [jnp.where(causal, x[1], 0.0) if work[w][2] else x[1] for x, (w, h) in zip(ls, chains)]
            after = [_tri_dot(up, x) for x in l1]
            a = [jnp.exp2(lb[n] + after[n] + c_ref[h, pl.ds(work[w][1], 1), mine[work[w][0]]])
                 for n, (w, h) in enumerate(chains)]
            a = [jnp.where(causal, a[n], 0.0) if work[w][2] else a[n] for n, (w, h) in enumerate(chains)]
            g = [a[n] * da[n] for n in range(len(chains))]
            gloc = [_dot(lo, x.astype(BF16)) for x in g]
            dzb = []
            for n, (w, h) in enumerate(chains):
                sub, _, diag = work[w]
                gsum = gloc[n] + cgs[sub][h]
                dz = g[n] - jnp.exp2(lb[n]) * (g[n] + gsum)
                dzb.append((jnp.where(causal, dz, 0.0) if diag else dz).astype(BF16))
                cgs[sub][h] = gsum[tq - 1:tq, :] + g[n][tq - 1:tq, :]
            ab = [x.astype(BF16) for x in a]
            for n, (w, h) in enumerate(chains):
                sub = work[w][0]
                dqt[sub] = dqt[sub] + _dot(jnp.where(sub_head == h, kt[w], jnp.zeros_like(kt[w])), dzb[n])
            for w, (sub, _, _) in enumerate(work):
                dk_ref[rows[w], :] += _dot(dzb[2 * w], qs[sub][0]) + _dot(dzb[2 * w + 1], qs[sub][1])
                dv_ref[rows[w], :] += _dot(ab[2 * w], doh[sub][0]) + _dot(ab[2 * w + 1], doh[sub][1])
            return {sub: (cgs[sub][0], cgs[sub][1], dqt[sub]) for sub in cgs}

        zero = jnp.zeros((1, tq), F32)
        carry = {}
        for sub in subs:
            i = index[sub]
            n = jnp.clip(n_ref[pair, i].astype(jnp.int32), 1, i + 1)
            carry[sub] = lax.fori_loop(jnp.minimum(i + 1 - n, i - 1), i - 1,
                                       lambda s, cr, sub=sub: blocks([(sub, s, False)], {sub: cr})[sub],
                                       (zero, zero, jnp.zeros((128, tq), F32)))
        both = lambda: blocks([(sub, index[sub] - 1 + b, b == 1) for sub in subs for b in range(2)], carry)
        if len(subs) == 1:
            carry = lax.cond(index[subs[0]] > 0, both, lambda: blocks([(subs[0], index[subs[0]], True)], carry))
        else:
            carry = both()
        for sub in subs:
            dqt_ref[:, mine[sub]] = carry[sub][2] * 0.125

    return _call(
        body, name, (SB_PAIRS, nb // qb),
        [pl.BlockSpec((qb * tq, 128), lambda p, i: (i, p)),
         pl.BlockSpec((t, 128), lambda p, i: (0, SB_PAIRS + p)),
         pl.BlockSpec((t, 128), lambda p, i: (0, 2 * SB_PAIRS + p)),
         pl.BlockSpec((None, nb, 128, tq), lambda p, i: (p, 0, 0, 0)),
         pl.BlockSpec((qb * tq, 128), lambda p, i: (i, p)),
         pl.BlockSpec((128, qb * tq), lambda p, i: (p, i)),
         pl.BlockSpec((2, nb, qb * tq), lambda p, i: (p, 0, i)),
         pl.BlockSpec(memory_space=pltpu.SMEM),
         pl.BlockSpec((tq, tq), lambda p, i: (0, 0)),
         pl.BlockSpec((tq, tq), lambda p, i: (0, 0))],
        [pl.BlockSpec((128, qb * tq), lambda p, i: (p, i)),
         pl.BlockSpec((t, 128), lambda p, i: (0, p)),
         pl.BlockSpec((t, 128), lambda p, i: (0, p))],
        [jax.ShapeDtypeStruct((SB_WIDTH, t), F32), jax.ShapeDtypeStruct((t, SB_WIDTH), F32),
         jax.ShapeDtypeStruct((t, SB_WIDTH), F32)],
        [], ("arbitrary", "arbitrary"), (qkv, qkv, qkv, ktb, dyc, dyct, cc, nvis, upper, lower), ride)


def _sg_bwd(ab, dyb, gn, wm, wmt, bfull, tm, name):
    t = ab.shape[0]
    nt = t // tm
    sel = (jnp.arange(SG_WIDTH)[:, None] // SG_HD == jnp.arange(CHUNK)[None, :]).astype(F32)

    def body(u_ref, v_ref, dy_ref, gn_ref, wm_ref, wmt_ref, b_ref, sel_ref,
             dup_ref, dvp_ref, dgn_ref, dw_ref, db_ref, dbacc_ref):
        i = pl.program_id(0)

        @pl.when(i == 0)
        def _():
            dgn_ref[...] = jnp.zeros_like(dgn_ref)
            dw_ref[...] = jnp.zeros_like(dw_ref)
            dbacc_ref[...] = jnp.zeros_like(dbacc_ref)

        tril = (lax.broadcasted_iota(jnp.int32, (CHUNK, CHUNK), 0) >= lax.broadcasted_iota(jnp.int32, (CHUNK, CHUNK), 1))
        gn_ = gn_ref[...]
        for c in range(tm // CHUNK):
            rows = slice(c * CHUNK, (c + 1) * CHUNK)
            up, vp, dy = u_ref[rows, :], v_ref[rows, :], dy_ref[rows, :]
            u, v = _gelu(up), _gelu(vp)
            r = _rstd(v)
            vn = (v * r * gn_).astype(BF16)
            sv = b_ref[...]
            for h in range(SG_HEADS):
                sv = sv + jnp.where(_head_lanes(h), _dot(wm_ref[h], vn), 0.0)
            dup_ref[rows, :] = dy * sv * _gelu_grad(up)
            dsv = dy * u
            dbacc_ref[...] += dsv
            dvn = jnp.zeros((CHUNK, SG_WIDTH), F32)
            for h in range(SG_HEADS):
                dsv_h = jnp.where(_head_lanes(h), dsv, 0.0).astype(BF16)
                dvn = dvn + _dot(wmt_ref[h], dsv_h)
                dw_ref[h] += jnp.where(tril, _dot_nt(dsv_h, vn), 0.0)
            dv, dgn = _rms_bwd(v, r, gn_, dvn)
            dgn_ref[...] += dgn
            dvp_ref[rows, :] = dv * _gelu_grad(vp)

        @pl.when(i == nt - 1)
        def _():
            db_ref[...] = jnp.dot(dbacc_ref[...], sel_ref[...], preferred_element_type=F32,
                                  precision=lax.Precision.HIGHEST)

    const = lambda shape: pl.BlockSpec(shape, lambda i: tuple(0 for _ in shape))
    return pl.pallas_call(
        body, name=name, grid=(nt,),
        in_specs=[pl.BlockSpec((tm, SG_WIDTH), lambda i: (i, 1)), pl.BlockSpec((tm, SG_WIDTH), lambda i: (i, 2)),
                  pl.BlockSpec((tm, SG_WIDTH), lambda i: (i, 0)), const((1, SG_WIDTH)),
                  const((SG_HEADS, CHUNK, CHUNK)), const((SG_HEADS, CHUNK, CHUNK)), const((CHUNK, SG_WIDTH)),
                  const((SG_WIDTH, CHUNK))],
        out_specs=[pl.BlockSpec((tm, SG_WIDTH), lambda i: (i, 0)), pl.BlockSpec((tm, SG_WIDTH), lambda i: (i, 0)),
                   const((1, SG_WIDTH)), const((SG_HEADS, CHUNK, CHUNK)), const((CHUNK, CHUNK))],
        out_shape=[jax.ShapeDtypeStruct((t, SG_WIDTH), F32), jax.ShapeDtypeStruct((t, SG_WIDTH), F32),
                   jax.ShapeDtypeStruct((1, SG_WIDTH), F32), jax.ShapeDtypeStruct((SG_HEADS, CHUNK, CHUNK), F32),
                   jax.ShapeDtypeStruct((CHUNK, CHUNK), F32)],
        scratch_shapes=[pltpu.VMEM((CHUNK, SG_WIDTH), F32)],
        compiler_params=_params(("arbitrary",)),
    )(ab, ab, dyb, gn, wm, wmt, bfull, sel)


def _pool_bwd(ab, dya, wbd, scale, tm, name):
    t = ab.shape[0]
    nt = t // tm
    hb = tm // POOL_HALO
    nh = t // POOL_HALO

    def body(cur_ref, prev_ref, dy_ref, dyn_ref, w_ref, s_ref, da_ref, dw_ref, ds_ref):
        i = pl.program_id(0)

        @pl.when(i == 0)
        def _():
            dw_ref[...] = jnp.zeros_like(dw_ref)
            ds_ref[...] = jnp.zeros_like(ds_ref)

        prev = jnp.where(i == 0, 0.0, prev_ref[...])
        d = _pool_diff(cur_ref[...], prev, i * tm).astype(BF16)
        dy = dy_ref[...]
        ds_ref[...] += jnp.sum(dy * _dot(d, w_ref[...]), axis=0, keepdims=True)
        dyn = jnp.where(i == nt - 1, 0.0, dyn_ref[...])
        dys = (jnp.concatenate([dy, dyn], axis=0) * s_ref[...]).astype(BF16)
        dw_ref[...] += _dot_tn(d, dys[:tm])
        dd = _dot_nt(dys, w_ref[...])
        fwd = _pool_window_sums(dd / _pool_count(i * tm, tm + POOL_HALO), True)
        da_ref[...] = fwd[:tm] - dd[:tm]

    return pl.pallas_call(
        body, name=name, grid=(nt,),
        in_specs=[pl.BlockSpec((tm, POOL_WIDTH), lambda i: (i, 0)),
                  pl.BlockSpec((POOL_HALO, POOL_WIDTH), lambda i: (jnp.maximum(i * hb - 1, 0), 0)),
                  pl.BlockSpec((tm, POOL_WIDTH), lambda i: (i, 0)),
                  pl.BlockSpec((POOL_HALO, POOL_WIDTH), lambda i: (jnp.minimum((i + 1) * hb, nh - 1), 0)),
                  pl.BlockSpec((POOL_WIDTH, POOL_WIDTH), lambda i: (0, 0)),
                  pl.BlockSpec((1, POOL_WIDTH), lambda i: (0, 0))],
        out_specs=[pl.BlockSpec((tm, POOL_WIDTH), lambda i: (i, 0)),
                   pl.BlockSpec((POOL_WIDTH, POOL_WIDTH), lambda i: (0, 0)),
                   pl.BlockSpec((1, POOL_WIDTH), lambda i: (0, 0))],
        out_shape=[jax.ShapeDtypeStruct((t, POOL_WIDTH), F32), jax.ShapeDtypeStruct((POOL_WIDTH, POOL_WIDTH), F32),
                   jax.ShapeDtypeStruct((1, POOL_WIDTH), F32)],
        compiler_params=_params(("arbitrary",)),
    )(ab, ab, dya, dya, wbd, scale)


def _inproj_bwd(dx1, x, g, da, dup, dvp, dqt, dk, dv, w, tm, name, ride=None):
    t, d = x.shape
    n = w.shape[1]
    nt = t // tm

    def body(dx1_ref, x_ref, g_ref, da_ref, du_ref, dv_ref, dqt_ref, dk_ref, dvv_ref, w_ref,
             dx_ref, h_ref, dp_ref, dg_ref):
        @pl.when(pl.program_id(0) == 0)
        def _():
            dg_ref[...] = jnp.zeros_like(dg_ref)

        dp = jnp.concatenate([da_ref[...], du_ref[...], dv_ref[...], dqt_ref[...].T, dk_ref[...], dvv_ref[...]],
                             axis=1).astype(BF16)
        dp_ref[...] = dp
        xx = x_ref[...]
        r = _rstd(xx)
        h_ref[...] = (xx * r * g_ref[...]).astype(BF16)
        dx, dg = _rms_bwd(xx, r, g_ref[...], _dot_nt(dp, w_ref[...]))
        dx_ref[...] = dx1_ref[...] + dx
        dg_ref[...] += dg

    row = lambda w_: pl.BlockSpec((tm, w_), lambda i: (i, 0))
    return _call(
        body, name, (nt,),
        [row(d), row(d), pl.BlockSpec((1, d), lambda i: (0, 0)), row(POOL_WIDTH), row(SG_WIDTH),
         row(SG_WIDTH), pl.BlockSpec((SB_WIDTH, tm), lambda i: (0, i)), row(SB_WIDTH), row(SB_WIDTH),
         pl.BlockSpec((d, n), lambda i: (0, 0))],
        [row(d), row(d), row(n), pl.BlockSpec((1, d), lambda i: (0, 0))],
        [jax.ShapeDtypeStruct((t, d), F32), jax.ShapeDtypeStruct((t, d), BF16),
         jax.ShapeDtypeStruct((t, n), BF16), jax.ShapeDtypeStruct((1, d), F32)],
        [], ("arbitrary",), (dx1, x, g, da, dup, dvp, dqt, dk, dv, w), ride)


def _full_w_in(gathered):
    return gathered.transpose(1, 0, 2).reshape(D_MODEL, IN_COLS)


_SMALL_SHAPES = ((D_MODEL,), (4, POOL_GW, POOL_GW), (POOL_WIDTH,), (SG_WIDTH,), (SG_HEADS, CHUNK, CHUNK),
                 (SG_HEADS, CHUNK), (D_MODEL,))
_SMALL_SIZES = tuple(functools.reduce(lambda p, q: p * q, shp) for shp in _SMALL_SHAPES)
_SMALL_ROWS = sum(_SMALL_SIZES) // 128
_NORM1_ROWS = D_MODEL // 128


def _pack_small_layer(arrs):
    return jnp.concatenate([a.reshape(-1) for a in arrs]).reshape(_SMALL_ROWS, 128)


def _pack_small(arrs, final):
    return jnp.concatenate([_pack_small_layer([a[l] for a in arrs]) for l in range(DEPTH)] + [final.reshape(-1, 128)])


def _unpack_small(buf):
    per_layer = []
    for l in range(DEPTH):
        flat, off, outs = buf[l * _SMALL_ROWS:(l + 1) * _SMALL_ROWS].reshape(-1), 0, []
        for shp, size in zip(_SMALL_SHAPES, _SMALL_SIZES):
            outs.append(flat[off:off + size].reshape(shp))
            off += size
        per_layer.append(outs)
    return [jnp.stack([per_layer[l][k] for l in range(DEPTH)]) for k in range(len(_SMALL_SHAPES))] + \
           [buf[DEPTH * _SMALL_ROWS:].reshape(-1)]


def _tiles(t):
    return min(512, t), min(256, t // 4), min(4096, t)


def _layer_fwd(xl, wi, wo, wu, wd, small_w, l, ride_mlp=None):
    n1, pw, ps, sn, sw, sb, n2 = small_w
    tm, tq, _ = _tiles(xl.shape[0])
    wbd = jnp.zeros((4, POOL_GW, 4, POOL_GW), F32)
    for gi in range(4):
        wbd = wbd.at[gi, :, gi, :].set(pw[gi])
    wbd = wbd.reshape(POOL_WIDTH, POOL_WIDTH).astype(BF16)
    wm = (sw * jnp.tril(jnp.ones((CHUNK, CHUNK), F32))).astype(BF16)
    wmt = wm.transpose(0, 2, 1)
    bfull = jnp.repeat(sb.T, SG_HD, axis=1)
    g1, g2, psc, gn = n1[None, :], n2[None, :], ps[None, :], sn[None, :]

    ab, qkv, ktb, vtb = _inproj_fwd(xl, g1, wi, tm, tq, f"inproj_fwd{l}")
    ya = _pool_fwd(ab, wbd, psc, tm, f"pool_fwd{l}")
    yb = _sg_fwd(ab, gn, wm, bfull, tm, f"sg_fwd{l}")
    shards = [w for w in (wo, wu, wd) if w.ndim == 2]
    if shards:
        (yct, cc, nvis), got = _sba_fwd(qkv, vtb, tq, f"sba_fwd{l}", SiblingGather(shards))
        got = iter(got)
        wo, wu, wd = (next(got) if w.ndim == 2 else w for w in (wo, wu, wd))
    else:
        yct, cc, nvis = _sba_fwd(qkv, vtb, tq, f"sba_fwd{l}")
    wo = wo.reshape(D_MODEL, D_MODEL)
    res = _outproj_mlp_fwd(xl, ya, yb, yct, wo, g2, wu, wd, min(MLP_ROWS, xl.shape[0]), f"mlp_fwd{l}", ride_mlp)
    (x1, u, x2), rode = res if ride_mlp is not None else (res, None)
    saved = dict(x=xl, ab=ab, qkv=qkv, ktb=ktb, cc=cc, nvis=nvis, ya=ya, yb=yb, yct=yct, x1=x1, u=u, wi=wi, wo=wo, wu=wu,
                 wd=wd, wbd=wbd, wm=wm, wmt=wmt, bfull=bfull, g1=g1, g2=g2, psc=psc, gn=gn)
    return x2, saved, rode


def _layer_bwd(dx, s, l, ride_mlp=None, exchange=True, gather_small=False):
    tm, tq, tw = _tiles(dx.shape[0])
    ktb = s["ktb"]
    res = _mlp_bwd(dx, s["x1"], s["g2"], s["u"], s["wu"], s["wd"], min(MLP_ROWS, dx.shape[0]), f"mlp_bwd{l}", ride_mlp)
    (dx1, du, r, h2, dx2b, dn2), rode = res if ride_mlp is not None else (res, None)
    dw_up = _matmul_tn(h2, du, D_MODEL, 2 * FF_SHARD, tw, f"dw_up{l}", by_column_block=FF_SHARD)
    dw_down = _matmul_tn(r, dx2b, 1024, D_MODEL, tw, f"dw_down{l}").reshape(N_DEV, FF_SHARD, D_MODEL)
    dya, dyb, dyc, dyct, dx1b = _outproj_bwd(dx1, s["wo"], tm, f"outproj_bwd{l}")
    dw_out = jnp.concatenate([
        _matmul_tn(jnp.concatenate([s["ya"], s["yb"]], axis=1), dx1b, POOL_WIDTH + SG_WIDTH, D_MODEL, tw, f"dw_out_ab{l}"),
        _matmul_tn(s["yct"], dx1b, SB_WIDTH, D_MODEL, tw, f"dw_out_c{l}", a_transposed=True)]
    ).reshape(N_DEV, OUT_SHARD, D_MODEL)
    if exchange:
        (dqt, dk, dv), (dw_out, dw_up, dw_down) = _sba_bwd(s["qkv"], ktb, dyc, dyct, s["cc"], s["nvis"], tq, f"sba_bwd{l}",
                                                           Exchange([dw_out, dw_up, dw_down], True))
    else:
        dqt, dk, dv = _sba_bwd(s["qkv"], ktb, dyc, dyct, s["cc"], s["nvis"], tq, f"sba_bwd{l}")
    dup, dvp, dgn, dwm, dbm = _sg_bwd(s["ab"], dyb, s["gn"], s["wm"], s["wmt"], s["bfull"], tm, f"sg_bwd{l}")
    da, dwbd, dpsc = _pool_bwd(s["ab"], dya, s["wbd"], s["psc"], tm, f"pool_bwd{l}")
    dpw = jnp.stack([dwbd[gi * POOL_GW:(gi + 1) * POOL_GW, gi * POOL_GW:(gi + 1) * POOL_GW] for gi in range(4)])
    small = _pack_small_layer([jnp.zeros((D_MODEL,), F32), dpw, dpsc[0], dgn[0], dwm, dbm[:, :SG_HEADS].T, dn2[0]])[_NORM1_ROWS:]
    res = _inproj_bwd(dx1, s["x"], s["g1"], da, dup, dvp, dqt, dk, dv, s["wi"], tm, f"inproj_bwd{l}",
                      Exchange([small], False) if gather_small else None)
    (dx, h1, dproj, dn1), small = (res[0], res[1][0]) if gather_small else (res, small)
    dw_in = _matmul_tn(h1, dproj, D_MODEL, IN_COLS // 3, tw, f"dw_in{l}")
    dw_in = dw_in.reshape(D_MODEL, N_DEV, IN_SHARD).transpose(1, 0, 2)
    return dx, (dw_in, dw_out, dw_up, dw_down), (dn1.reshape(_NORM1_ROWS, 128), small), rode


def kernel(x, norm1, w_in, pool_w, pool_scale, sg_norm, sg_w, sg_b, w_out, norm2, w_up, w_down, final_norm, loss_target, m_norm1, m_w_in, m_pool_w, m_pool_scale, m_sg_norm, m_sg_w, m_sg_b, m_w_out, m_norm2, m_w_up, m_w_down, m_final_norm, v_norm1, v_w_in, v_pool_w, v_pool_scale, v_sg_norm, v_sg_w, v_sg_b, v_w_out, v_norm2, v_w_up, v_w_down, v_final_norm):
    t = x.shape[1]
    tm = _tiles(t)[0]
    small_w = (norm1, pool_w, pool_scale, sg_norm, sg_w, sg_b, norm2)
    big_w = (w_in, w_out, w_up, w_down)
    big_m = (m_w_in, m_w_out, m_w_up, m_w_down)
    big_v = (v_w_in, v_w_out, v_w_up, v_w_down)
    shards = [[w[l].astype(BF16) for w in big_w] for l in range(DEPTH)]

    wi0 = _full_w_in(SiblingGather(shards[0][:1]).alone("gather_w_in0")[0])
    sh_in1, sh_out1, sh_up1, sh_down1 = shards[1]
    x1, s0, (wi1, wo1, wd1) = _layer_fwd(x.reshape(t, D_MODEL), wi0, *shards[0][1:], tuple(w[0] for w in small_w), 0,
                                         ride_mlp=Exchange([sh_in1, sh_out1, sh_down1], False))
    x2, s1, _ = _layer_fwd(x1, _full_w_in(wi1), wo1, sh_up1, wd1, tuple(w[1] for w in small_w), 1)
    loss_local, dx, dfinal = _loss_grad(x2, final_norm[None, :], loss_target.reshape(t, D_MODEL), tm, "loss_grad")
    loss = lax.psum(loss_local[0, 0], MESH_AXES)

    dx, parts1, small1, _ = _layer_bwd(dx, s1, 1)
    early = jnp.concatenate(list(small1) + [dfinal.reshape(-1, 128)])
    dx, parts0, (dn1, small0), (recv_in1, early) = _layer_bwd(
        dx, s0, 0, ride_mlp=Exchange([parts1[0], early], [True, False]), gather_small=True)
    grad_x = dx.reshape(x.shape)
    recv_in0, dn1 = _exchange([parts0[0], dn1], [True, False], "scatter_w_in0_gather_norm1_0")
    small_all = jnp.concatenate([dn1, small0, early], axis=1)
    received = [[recv_in0] + list(parts0[1:]), [recv_in1] + list(parts1[1:])]

    big = [None] * 4
    for l in reversed(range(DEPTH)):
        for k in range(4):
            big[k] = _reduce_adamw(received[l][k], big_w[k], big_m[k], big_v[k], l, big[k], f"adamw{k}_{l}")

    sm = _reduce_adamw(
        small_all,
        _pack_small(small_w, final_norm)[None],
        _pack_small([m_norm1, m_pool_w, m_pool_scale, m_sg_norm, m_sg_w, m_sg_b, m_norm2], m_final_norm)[None],
        _pack_small([v_norm1, v_pool_w, v_pool_scale, v_sg_norm, v_sg_w, v_sg_b, v_norm2], v_final_norm)[None],
        0, None, "adamw_replicated")

    out = [loss, grad_x]
    for k in range(4):
        n1, pw, ps, sn, sw, sb, n2, fn = _unpack_small(sm[k][0])
        out += [n1, big[0][k], pw, ps, sn, sw, sb, big[1][k], n2, big[2][k], big[3][k], fn]
    return tuple(out)
```

```python
import functools

import jax
import jax.numpy as jnp
from jax import lax
from jax.experimental import pallas as pl
from jax.experimental.pallas import tpu as pltpu

F32 = jnp.float32
BF16 = jnp.bfloat16

D_MODEL = 1024
DEPTH = 2
POOL_WIDTH = 256
SG_WIDTH = 256
SB_WIDTH = 512
POOL_WINDOWS = (2, 4, 8, 16)
POOL_GW = 64
POOL_HALO = 16
CHUNK = 128
SG_HEADS = 4
SG_HD = 64
SB_HD = 64
SB_PAIRS = SB_WIDTH // 128
AB_COLS = POOL_WIDTH + 2 * SG_WIDTH
IN_COLS = AB_COLS + 3 * SB_WIDTH
D_FF = 4096
EPS = 1e-6
N_DEV = 8
FF_SHARD = D_FF // N_DEV
IN_SHARD = IN_COLS // N_DEV
OUT_SHARD = D_MODEL // N_DEV
ADAM_LR = 0.001
ADAM_B1 = 0.9
ADAM_B2 = 0.999
ADAM_EPS = 1e-08
ADAM_WD = 0.01
ADAM_STEP = 10
VMEM_LIMIT = 56 * 1024 * 1024
MLP_ROWS = 1024
MESH_AXES = ("x", "y", "c")


def _dot(a, b):
    return jnp.dot(a, b, preferred_element_type=F32)


def _dot_nt(a, b):
    return lax.dot_general(a, b, (((1,), (1,)), ((), ())), preferred_element_type=F32)


def _dot_tn(a, b):
    return lax.dot_general(a, b, (((0,), (0,)), ((), ())), preferred_element_type=F32)


def _rstd(x):
    return lax.rsqrt(jnp.mean(x * x, axis=-1, keepdims=True) + EPS)


def _rms_bwd(x, r, g, dh):
    gq = dh * g
    dx = r * gq - x * (r * r * r) * jnp.mean(gq * x, axis=-1, keepdims=True)
    dg = jnp.sum(dh * x * r, axis=0, keepdims=True)
    return dx, dg


def _params(sem=None):
    kw = dict(vmem_limit_bytes=VMEM_LIMIT)
    if sem is not None:
        kw["dimension_semantics"] = sem
    return pltpu.CompilerParams(**kw)


def _row_tile(rows, cap):
    best = 8
    for t in range(8, min(rows, cap) + 1, 8):
        if rows % t == 0:
            best = t
    return best


def _peer(k):
    x, y, c = lax.axis_index("x"), lax.axis_index("y"), lax.axis_index("c")
    return (1 - x if k & 4 else x, 1 - y if k & 2 else y, 1 - c if k & 1 else c)


def _my_index():
    return 4 * lax.axis_index("x") + 2 * lax.axis_index("y") + lax.axis_index("c")


class Exchange:
    def __init__(self, arrs, scatter):
        self.arrs = list(arrs)
        self.scatter = list(scatter) if isinstance(scatter, (list, tuple)) else [scatter] * len(self.arrs)
        self.n = len(self.arrs)
        self.any_specs = [pl.BlockSpec(memory_space=pl.ANY)] * self.n
        self.out_shape = [jax.ShapeDtypeStruct((N_DEV,) + a.shape[-2:], a.dtype) for a in self.arrs]
        self.sems = [pltpu.SemaphoreType.DMA((self.n, N_DEV - 1)), pltpu.SemaphoreType.DMA((self.n, N_DEV - 1)),
                     pltpu.SemaphoreType.DMA((self.n,))]

    def _copies(self, ins, outs, sems):
        send_sems, recv_sems, local_sems = sems
        me = _my_index()
        local, remote = [], []
        for a in range(self.n):
            sc = self.scatter[a]
            local.append(pltpu.make_async_copy(ins[a].at[me] if sc else ins[a], outs[a].at[me], local_sems.at[a]))
            for k in range(1, N_DEV):
                px, py, pc = _peer(k)
                remote.append(pltpu.make_async_remote_copy(
                    src_ref=ins[a].at[4 * px + 2 * py + pc] if sc else ins[a], dst_ref=outs[a].at[me],
                    send_sem=send_sems.at[a, k - 1], recv_sem=recv_sems.at[a, k - 1],
                    device_id=(px, py, pc), device_id_type=pl.DeviceIdType.MESH))
        return local, remote

    def start(self, ins, outs, sems):
        local, remote = self._copies(ins, outs, sems)
        for cp in local + remote:
            cp.start()

    def middle(self, ins, outs, sems):
        pass

    def wait(self, ins, outs, sems):
        local, remote = self._copies(ins, outs, sems)
        for cp in remote:
            cp.wait_recv()
        for cp in remote:
            cp.wait_send()
        for cp in local:
            cp.wait()

    def alone(self, name):
        n = self.n

        def body(*refs):
            ins, outs, sems = refs[:n], refs[n:2 * n], refs[2 * n:]
            self.start(ins, outs, sems)
            self.middle(ins, outs, sems)
            self.wait(ins, outs, sems)

        return pl.pallas_call(body, name=name, out_shape=self.out_shape, in_specs=self.any_specs,
                              out_specs=self.any_specs, scratch_shapes=self.sems)(*self.arrs)


class SiblingGather(Exchange):
    def __init__(self, arrs):
        super().__init__(arrs, False)

    def _plan(self, ins, outs, sems):
        send_sems, recv_sems, local_sems = sems
        x, y, core = lax.axis_index("x"), lax.axis_index("y"), lax.axis_index("c")
        me, sibling = (x, y, core), (x, y, 1 - core)
        chips = [(1 - x, y), (x, 1 - y), (1 - x, 1 - y)]
        plan = dict(local=[], first=[], arrived=[], passed=[], late=[])
        for a in range(self.n):
            slot = lambda p, a=a: outs[a].at[4 * p[0] + 2 * p[1] + p[2]]

            def copy(k, block, to, src=None, a=a, slot=slot):
                return pltpu.make_async_remote_copy(
                    src_ref=slot(block) if src is None else src, dst_ref=slot(block), send_sem=send_sems.at[a, k],
                    recv_sem=recv_sems.at[a, k], device_id=to, device_id_type=pl.DeviceIdType.MESH)

            plan["local"].append(pltpu.make_async_copy(ins[a], slot(me), local_sems.at[a]))
            plan["first"] += [copy(0, me, sibling, src=ins[a])] + [copy(1 + j, me, (*ch, core), src=ins[a])
                                                                  for j, ch in enumerate(chips)]
            plan["arrived"] += [copy(1 + j, (*ch, core), me) for j, ch in enumerate(chips)]
            plan["passed"] += [copy(4 + j, (*ch, core), sibling) for j, ch in enumerate(chips)]
            plan["late"] += [copy(0, sibling, me)] + [copy(4 + j, (*ch, 1 - core), me) for j, ch in enumerate(chips)]
        return plan

    def start(self, ins, outs, sems):
        plan = self._plan(ins, outs, sems)
        for cp in plan["local"] + plan["first"]:
            cp.start()

    def middle(self, ins, outs, sems):
        plan = self._plan(ins, outs, sems)
        for arrived, passed in zip(plan["arrived"], plan["passed"]):
            arrived.wait_recv()
            passed.start()

    def wait(self, ins, outs, sems):
        plan = self._plan(ins, outs, sems)
        for cp in plan["late"]:
            cp.wait_recv()
        for cp in plan["first"] + plan["passed"]:
            cp.wait_send()
        for cp in plan["local"]:
            cp.wait()


def _exchange(arrs, scatter, name):
    return Exchange(arrs, scatter).alone(name)


def _call(body, name, grid, in_specs, out_specs, out_shape, scratch_shapes, semantics, args, ride=None):
    if ride is None:
        return pl.pallas_call(body, name=name, grid=grid, in_specs=in_specs, out_specs=out_specs, out_shape=out_shape,
                              scratch_shapes=scratch_shapes, compiler_params=_params(semantics))(*args)
    single = not isinstance(out_shape, (list, tuple))
    out_specs, out_shape = ([out_specs], [out_shape]) if single else (list(out_specs), list(out_shape))
    n_in, n_out, n_scr, n = len(in_specs), len(out_specs), len(scratch_shapes), ride.n

    def riding(*refs):
        ins, cins = refs[:n_in], refs[n_in:n_in + n]
        outs, couts = refs[n_in + n:n_in + n + n_out], refs[n_in + n + n_out:n_in + 2 * n + n_out]
        scr, sems = refs[n_in + 2 * n + n_out:n_in + 2 * n + n_out + n_scr], refs[n_in + 2 * n + n_out + n_scr:]
        step = functools.reduce(lambda s, a: s * grid[a] + pl.program_id(a), range(len(grid)), 0)
        steps = functools.reduce(lambda p, q: p * q, grid)

        @pl.when(step == 0)
        def _():
            ride.start(cins, couts, sems)

        @pl.when(step == (3 * steps) // 4)
        def _():
            ride.middle(cins, couts, sems)

        body(*ins, *outs, *scr)

        @pl.when(step == steps - 1)
        def _():
            ride.wait(cins, couts, sems)

    res = pl.pallas_call(
        riding, name=name, grid=grid, in_specs=list(in_specs) + ride.any_specs, out_specs=out_specs + ride.any_specs,
        out_shape=out_shape + ride.out_shape, scratch_shapes=list(scratch_shapes) + ride.sems,
        compiler_params=_params(("arbitrary",) * len(grid)))(*args, *ride.arrs)
    own = res[0] if single else list(res[:n_out])
    return own, list(res[n_out:])


def _reduce_adamw(parts, w, m, v, l, prev, name):
    _, rows, n = parts.shape
    tr = _row_tile(rows, max(8, (1 << 18) // n))
    c1 = 1.0 - ADAM_B1 ** ADAM_STEP
    c2 = 1.0 - ADAM_B2 ** ADAM_STEP

    def body(p_ref, w_ref, m_ref, v_ref, *rest):
        g_ref, d_ref, nm_ref, nv_ref = rest[-4:]
        g = p_ref[0].astype(F32)
        for s in range(1, N_DEV):
            g = g + p_ref[s].astype(F32)
        nm = ADAM_B1 * m_ref[...] + (1.0 - ADAM_B1) * g
        nv = ADAM_B2 * v_ref[...] + (1.0 - ADAM_B2) * (g * g)
        m_hat = nm / c1
        v_hat = nv / c2
        g_ref[...] = g
        d_ref[...] = -ADAM_LR * (m_hat / (jnp.sqrt(v_hat) + ADAM_EPS) + ADAM_WD * w_ref[...])
        nm_ref[...] = nm
        nv_ref[...] = nv

    blk = pl.BlockSpec((None, tr, n), lambda i: (l, i, 0))
    out = jax.ShapeDtypeStruct(w.shape, F32)
    prev = list(prev) if prev is not None else []
    return pl.pallas_call(
        body, name=name, grid=(rows // tr,),
        in_specs=[pl.BlockSpec((N_DEV, tr, n), lambda i: (0, i, 0)), blk, blk, blk] + [pl.BlockSpec(memory_space=pl.ANY)] * len(prev),
        out_specs=[blk, blk, blk, blk], out_shape=[out, out, out, out],
        input_output_aliases={4 + k: k for k in range(len(prev))},
        compiler_params=_params(("parallel",)),
    )(parts, w, m, v, *prev)


def _inproj_fwd(x, g, w, tm, tq, name):
    t, d = x.shape
    n = w.shape[1]
    nb = t // tq
    per = tm // tq

    def body(x_ref, g_ref, w_ref, ab_ref, qkv_ref, kt_ref, vt_ref):
        xx = x_ref[...]
        h = (xx * _rstd(xx) * g_ref[...]).astype(BF16)
        ab_ref[...] = _dot(h, w_ref[:, :AB_COLS])
        qkv = _dot(h, w_ref[:, AB_COLS:])
        qkv_ref[...] = qkv.astype(BF16)
        for which, out_ref in ((1, kt_ref), (2, vt_ref)):
            for p in range(SB_PAIRS):
                for b in range(per):
                    cols = which * SB_WIDTH + p * 128
                    out_ref[p, b] = qkv[b * tq:(b + 1) * tq, cols:cols + 128].T.astype(BF16)

    tb = pl.BlockSpec((SB_PAIRS, per, 128, tq), lambda i: (0, i, 0, 0))
    tshape = jax.ShapeDtypeStruct((SB_PAIRS, nb, 128, tq), BF16)
    return _call(
        body, name, (t // tm,),
        [pl.BlockSpec((tm, d), lambda i: (i, 0)), pl.BlockSpec((1, d), lambda i: (0, 0)),
         pl.BlockSpec((d, n), lambda i: (0, 0))],
        [pl.BlockSpec((tm, AB_COLS), lambda i: (i, 0)), pl.BlockSpec((tm, n - AB_COLS), lambda i: (i, 0)), tb, tb],
        [jax.ShapeDtypeStruct((t, AB_COLS), F32), jax.ShapeDtypeStruct((t, n - AB_COLS), BF16), tshape, tshape],
        [], ("parallel",), (x, g, w))


def _pool_window_sums(xx, forward):
    n = xx.shape[0]
    sh = (lambda k: n - k) if forward else (lambda k: k)
    s2 = xx + pltpu.roll(xx, sh(1), 0)
    s4 = s2 + pltpu.roll(s2, sh(2), 0)
    s8 = s4 + pltpu.roll(s4, sh(4), 0)
    s16 = s8 + pltpu.roll(s8, sh(8), 0)
    grp = lax.broadcasted_iota(jnp.int32, (1, POOL_WIDTH), 1) // POOL_GW
    return jnp.where(grp == 0, s2, jnp.where(grp == 1, s4, jnp.where(grp == 2, s8, s16)))


def _pool_count(t0, rows):
    grp = lax.broadcasted_iota(jnp.int32, (1, POOL_WIDTH), 1) // POOL_GW
    win = jnp.where(grp == 0, 2, jnp.where(grp == 1, 4, jnp.where(grp == 2, 8, 16)))
    tt = t0 + lax.broadcasted_iota(jnp.int32, (rows, 1), 0)
    return jnp.minimum(tt + 1, win).astype(F32)


def _pool_diff(cur, prev, t0):
    tm = cur.shape[0]
    sums = _pool_window_sums(jnp.concatenate([prev, cur], axis=0), False)[POOL_HALO:]
    return sums / _pool_count(t0, tm) - cur


def _pool_fwd(ab, wbd, scale, tm, name):
    t = ab.shape[0]
    hb = tm // POOL_HALO

    def body(cur_ref, prev_ref, w_ref, s_ref, y_ref):
        i = pl.program_id(0)
        prev = jnp.where(i == 0, 0.0, prev_ref[...])
        d = _pool_diff(cur_ref[...], prev, i * tm).astype(BF16)
        y_ref[...] = (_dot(d, w_ref[...]) * s_ref[...]).astype(BF16)

    return pl.pallas_call(
        body, name=name, grid=(t // tm,),
        in_specs=[pl.BlockSpec((tm, POOL_WIDTH), lambda i: (i, 0)),
                  pl.BlockSpec((POOL_HALO, POOL_WIDTH), lambda i: (jnp.maximum(i * hb - 1, 0), 0)),
                  pl.BlockSpec((POOL_WIDTH, POOL_WIDTH), lambda i: (0, 0)),
                  pl.BlockSpec((1, POOL_WIDTH), lambda i: (0, 0))],
        out_specs=pl.BlockSpec((tm, POOL_WIDTH), lambda i: (i, 0)),
        out_shape=jax.ShapeDtypeStruct((t, POOL_WIDTH), BF16),
        compiler_params=_params(("parallel",)),
    )(ab, ab, wbd, scale)


_GELU_K = 0.7978845608028654
_GELU_A = 0.044715


def _gelu(x):
    return 0.5 * x * (1.0 + jnp.tanh(_GELU_K * (x + _GELU_A * (x * x * x))))


def _gelu_grad(x):
    th = jnp.tanh(_GELU_K * (x + _GELU_A * (x * x * x)))
    return 0.5 * (1.0 + th) + 0.5 * x * (1.0 - th * th) * (_GELU_K * (1.0 + 3.0 * _GELU_A * (x * x)))


def _head_lanes(h):
    return lax.broadcasted_iota(jnp.int32, (1, SG_WIDTH), 1) // SG_HD == h


def _sg_fwd(ab, gn, wm, bfull, tm, name):
    t = ab.shape[0]

    def body(u_ref, v_ref, gn_ref, wm_ref, b_ref, y_ref):
        v = _gelu(v_ref[...])
        vn = (v * _rstd(v) * gn_ref[...]).astype(BF16)
        for c in range(tm // CHUNK):
            rows = slice(c * CHUNK, (c + 1) * CHUNK)
            vc = vn[rows]
            sv = b_ref[...]
            for h in range(SG_HEADS):
                sv = sv + jnp.where(_head_lanes(h), _dot(wm_ref[h], vc), 0.0)
            y_ref[rows, :] = (_gelu(u_ref[rows, :]) * sv).astype(BF16)

    return pl.pallas_call(
        body, name=name, grid=(t // tm,),
        in_specs=[pl.BlockSpec((tm, SG_WIDTH), lambda i: (i, 1)), pl.BlockSpec((tm, SG_WIDTH), lambda i: (i, 2)),
                  pl.BlockSpec((1, SG_WIDTH), lambda i: (0, 0)),
                  pl.BlockSpec((SG_HEADS, CHUNK, CHUNK), lambda i: (0, 0, 0)),
                  pl.BlockSpec((CHUNK, SG_WIDTH), lambda i: (0, 0))],
        out_specs=pl.BlockSpec((tm, SG_WIDTH), lambda i: (i, 0)),
        out_shape=jax.ShapeDtypeStruct((t, SG_WIDTH), BF16),
        compiler_params=_params(("parallel",)),
    )(ab, ab, gn, wm, bfull)


LOG2E = 1.4426950408889634
SB_SCALE = 0.125 * LOG2E
SB_DEAD_LOG2 = 152.0
SB_QUERY_BLOCKS_PER_STEP = 2


def _log2_sigmoids(y):
    neg_abs = lax.bitcast_convert_type(lax.bitcast_convert_type(y, jnp.uint32) | jnp.uint32(0x80000000), F32)
    lb = jnp.minimum(y, 0.0) - jnp.log(1.0 + jnp.exp2(neg_abs)) * LOG2E
    return lb, lb - y


def _split(x):
    hi = x.astype(BF16)
    return hi, (x - hi.astype(F32)).astype(BF16)


def _tri_dot(tri, x):
    hi, lo = _split(x)
    return _dot(tri, hi) + _dot(tri, lo)


def _sba_fwd(qkv, vtb, tq, name, ride=None):
    t = qkv.shape[0]
    nb = t // tq
    qb = SB_QUERY_BLOCKS_PER_STEP
    upper = (jnp.arange(tq)[None, :] > jnp.arange(tq)[:, None]).astype(BF16)

    def body(q_ref, k_ref, vt_ref, up_ref, ot_ref, c_ref, n_ref):
        pair, step = pl.program_id(0), pl.program_id(1)
        refs = (q_ref, k_ref, vt_ref, up_ref, ot_ref, c_ref, n_ref)
        subs = list(range(qb))

        @pl.when(step == 0)
        def _():
            for sub in subs:
                query_blocks([sub], pair, step, *refs)

        @pl.when(step > 0)
        def _():
            query_blocks(subs, pair, step, *refs)

    def query_blocks(subs, pair, step, q_ref, k_ref, vt_ref, up_ref, ot_ref, c_ref, n_ref):
        index = {sub: step * qb + sub for sub in subs}
        mine = {sub: slice(sub * tq, (sub + 1) * tq) for sub in subs}
        up = up_ref[...]
        lane_head = lax.broadcasted_iota(jnp.int32, (1, 128), 1) // SB_HD
        sub_head = lax.broadcasted_iota(jnp.int32, (128, 1), 0) // SB_HD
        causal = (lax.broadcasted_iota(jnp.int32, (tq, tq), 0) < lax.broadcasted_iota(jnp.int32, (tq, tq), 1))
        qh = {}
        for sub in subs:
            q = q_ref[mine[sub], :]
            qh[sub] = [jnp.where(lane_head == h, q, jnp.zeros_like(q)) for h in range(2)]

        def blocks(work, carry):
            cs = {sub: list(carry[sub][:2]) for sub in carry}
            acc = {sub: carry[sub][2] for sub in carry}
            kj = [k_ref[pl.ds(pl.multiple_of(j * tq, tq), tq), :] for _, j, _ in work]
            vt = [vt_ref[j] for _, j, _ in work]
            chains = [(w, h) for w in range(len(work)) for h in range(2)]
            z = [_dot_nt(kj[w], qh[work[w][0]][h]) for w, h in chains]
            ls = [_log2_sigmoids(zz * SB_SCALE) for zz in z]
            lb = [x[0] for x in ls]
            l1 = [jnp.where(causal, x[1], 0.0) if work[w][2] else x[1] for x, (w, h) in zip(ls, chains)]
            after = [_tri_dot(up, x) for x in l1]
            a = []
            for n, (w, h) in enumerate(chains):
                sub, j, diag = work[w]
                c_ref[h, pl.ds(j, 1), mine[sub]] = cs[sub][h]
                an = jnp.exp2(lb[n] + after[n] + cs[sub][h])
                a.append(jnp.where(causal, an, 0.0) if diag else an)
                cs[sub][h] = cs[sub][h] + after[n][0:1, :] + l1[n][0:1, :]
            alive = {sub: (jnp.max(jnp.maximum(*cs[sub])) > -SB_DEAD_LOG2).astype(jnp.int32) for sub in cs}
            for n, (w, h) in enumerate(chains):
                sub = work[w][0]
                acc[sub] = acc[sub] + _dot(jnp.where(sub_head == h, vt[w], jnp.zeros_like(vt[w])), a[n].astype(BF16))
            return {sub: (alive[sub], cs[sub][0], cs[sub][1], acc[sub]) for sub in cs}

        zero = jnp.zeros((1, tq), F32)
        start = {sub: (zero, zero, jnp.zeros((128, tq), F32)) for sub in subs}
        both = lambda: blocks([(sub, index[sub] - b, b == 0) for sub in subs for b in range(2)], start)
        if len(subs) == 1:
            joint = lax.cond(index[subs[0]] > 0, both, lambda: blocks([(subs[0], index[subs[0]], True)], start))
        else:
            joint = both()
        for sub in subs:
            i = index[sub]

            def left(state, sub=sub, i=i):
                s, _, c0, c1, acc = state
                return (s + 1,) + blocks([(sub, i - 1 - s, False)], {sub: (c0, c1, acc)})[sub]

            state = lax.while_loop(lambda st, i=i: (st[0] < i) & (st[1] > 0), left, (jnp.minimum(i, 1),) + tuple(joint[sub]))
            ot_ref[:, mine[sub]] = state[4].astype(BF16)
            n_ref[pair, i] = (state[0] + 1).astype(F32)

    return _call(
        body, name, (SB_PAIRS, nb // qb),
        [pl.BlockSpec((qb * tq, 128), lambda p, i: (i, p)),
         pl.BlockSpec((t, 128), lambda p, i: (0, SB_PAIRS + p)),
         pl.BlockSpec((None, nb, 128, tq), lambda p, i: (p, 0, 0, 0)),
         pl.BlockSpec((tq, tq), lambda p, i: (0, 0))],
        [pl.BlockSpec((128, qb * tq), lambda p, i: (p, i)),
         pl.BlockSpec((2, nb, qb * tq), lambda p, i: (p, 0, i)),
         pl.BlockSpec(memory_space=pltpu.SMEM)],
        [jax.ShapeDtypeStruct((SB_WIDTH, t), BF16),
         jax.ShapeDtypeStruct((2 * SB_PAIRS, nb, t), F32),
         jax.ShapeDtypeStruct((SB_PAIRS, nb), F32)],
        [], ("arbitrary", "arbitrary"), (qkv, qkv, vtb, upper), ride)


def _outproj_mlp_fwd(x, ya, yb, yct, wo, g2, wup, wdn, tm, name, ride=None):
    t, d = x.shape
    nf = wup.shape[0]

    def body(x_ref, ya_ref, yb_ref, yct_ref, wo_ref, g_ref, wu_ref, wd_ref, x1_ref, u_ref, x2_ref, h_ref, acc_ref):
        j = pl.program_id(1)

        @pl.when(j == 0)
        def _():
            x1 = (x_ref[...] + _dot(ya_ref[...], wo_ref[0:POOL_WIDTH, :])
                  + _dot(yb_ref[...], wo_ref[POOL_WIDTH:POOL_WIDTH + SG_WIDTH, :])
                  + _dot_tn(yct_ref[...], wo_ref[POOL_WIDTH + SG_WIDTH:, :]))
            x1_ref[...] = x1
            h_ref[...] = (x1 * _rstd(x1) * g_ref[...]).astype(BF16)
            acc_ref[...] = x1

        u = _dot(h_ref[...], wu_ref[...])
        u_ref[...] = u
        r = jnp.maximum(u, 0.0)
        acc_ref[...] += _dot((r * r).astype(BF16), wd_ref[...])

        @pl.when(j == nf - 1)
        def _():
            x2_ref[...] = acc_ref[...]

    row = lambda w: pl.BlockSpec((tm, w), lambda i, j: (i, 0))
    return _call(
        body, name, (t // tm, nf),
        [row(d), row(POOL_WIDTH), row(SG_WIDTH), pl.BlockSpec((SB_WIDTH, tm), lambda i, j: (0, i)),
         pl.BlockSpec((d, d), lambda i, j: (0, 0)), pl.BlockSpec((1, d), lambda i, j: (0, 0)),
         pl.BlockSpec((None, d, FF_SHARD), lambda i, j: (j, 0, 0)),
         pl.BlockSpec((None, FF_SHARD, d), lambda i, j: (j, 0, 0))],
        [row(d), pl.BlockSpec((tm, FF_SHARD), lambda i, j: (i, j)), row(d)],
        [jax.ShapeDtypeStruct((t, d), F32), jax.ShapeDtypeStruct((t, nf * FF_SHARD), F32),
         jax.ShapeDtypeStruct((t, d), F32)],
        [pltpu.VMEM((tm, d), BF16), pltpu.VMEM((tm, d), F32)],
        ("parallel", "arbitrary"), (x, ya, yb, yct, wo, g2, wup, wdn), ride)


def _loss_grad(x, g, target, tm, name):
    t, d = x.shape
    nt = t // tm

    def body(x_ref, g_ref, t_ref, loss_ref, dx_ref, dg_ref, sq_ref):
        i = pl.program_id(0)

        @pl.when(i == 0)
        def _():
            sq_ref[...] = jnp.zeros_like(sq_ref)
            dg_ref[...] = jnp.zeros_like(dg_ref)

        xx = x_ref[...]
        r = _rstd(xx)
        err = xx * r * g_ref[...] - t_ref[...]
        sq_ref[...] += jnp.sum(err * err, axis=0, keepdims=True)
        dx, dg = _rms_bwd(xx, r, g_ref[...], err * (1.0 / d))
        dx_ref[...] = dx
        dg_ref[...] += dg

        @pl.when(i == nt - 1)
        def _():
            loss_ref[...] = jnp.sum(sq_ref[...], axis=1, keepdims=True) * (0.5 / d)

    return pl.pallas_call(
        body, name=name, grid=(nt,),
        in_specs=[pl.BlockSpec((tm, d), lambda i: (i, 0)), pl.BlockSpec((1, d), lambda i: (0, 0)),
                  pl.BlockSpec((tm, d), lambda i: (i, 0))],
        out_specs=[pl.BlockSpec((1, 1), lambda i: (0, 0)), pl.BlockSpec((tm, d), lambda i: (i, 0)),
                   pl.BlockSpec((1, d), lambda i: (0, 0))],
        out_shape=[jax.ShapeDtypeStruct((1, 1), F32), jax.ShapeDtypeStruct((t, d), F32),
                   jax.ShapeDtypeStruct((1, d), F32)],
        scratch_shapes=[pltpu.VMEM((1, d), F32)],
        compiler_params=_params(("arbitrary",)),
    )(x, g, target)


def _mlp_bwd(dx2, x1, g2, u, wup, wdn, tm, name, ride=None):
    t, d = dx2.shape
    nf = wup.shape[0]
    nt = t // tm

    def body(dx2_ref, x1_ref, g_ref, u_ref, wu_ref, wd_ref, dx1_ref, du_ref, r_ref, h_ref, dxb_ref, dg_ref, acc_ref):
        i, j = pl.program_id(0), pl.program_id(1)

        @pl.when(j == 0)
        def _():
            dxb_ref[...] = dx2_ref[...].astype(BF16)
            acc_ref[...] = jnp.zeros_like(acc_ref)

        @pl.when((i == 0) & (j == 0))
        def _():
            dg_ref[...] = jnp.zeros_like(dg_ref)

        share = pl.ds(pl.multiple_of(j * (tm // nf), tm // nf), tm // nf)
        xs = x1_ref[share, :]
        h_ref[share, :] = (xs * _rstd(xs) * g_ref[...]).astype(BF16)
        dr = _dot_nt(dxb_ref[...], wd_ref[...])
        ru = jnp.maximum(u_ref[...], 0.0)
        du = (dr * (2.0 * ru)).astype(BF16)
        du_ref[...] = du
        r_ref[...] = (ru * ru).astype(BF16)
        acc_ref[...] += _dot_nt(du, wu_ref[...])

        @pl.when(j == nf - 1)
        def _():
            x1 = x1_ref[...]
            dx, dg = _rms_bwd(x1, _rstd(x1), g_ref[...], acc_ref[...])
            dx1_ref[...] = dx2_ref[...] + dx
            dg_ref[...] += dg

    row = lambda w: pl.BlockSpec((tm, w), lambda i, j: (i, 0))
    col = pl.BlockSpec((tm, FF_SHARD), lambda i, j: (i, j))
    return _call(
        body, name, (nt, nf),
        [row(d), row(d), pl.BlockSpec((1, d), lambda i, j: (0, 0)), col,
         pl.BlockSpec((None, d, FF_SHARD), lambda i, j: (j, 0, 0)),
         pl.BlockSpec((None, FF_SHARD, d), lambda i, j: (j, 0, 0))],
        [row(d), col, col, row(d), row(d), pl.BlockSpec((1, d), lambda i, j: (0, 0))],
        [jax.ShapeDtypeStruct((t, d), F32), jax.ShapeDtypeStruct((t, nf * FF_SHARD), BF16),
         jax.ShapeDtypeStruct((t, nf * FF_SHARD), BF16), jax.ShapeDtypeStruct((t, d), BF16),
         jax.ShapeDtypeStruct((t, d), BF16), jax.ShapeDtypeStruct((1, d), F32)],
        [pltpu.VMEM((tm, d), F32)], ("arbitrary", "arbitrary"), (dx2, x1, g2, u, wup, wdn), ride)


def _matmul_tn(a, b, bm, bn, bt, name, by_column_block=0, a_transposed=False):
    m, t = a.shape if a_transposed else a.shape[::-1]
    n = b.shape[1]
    nk = t // bt

    def body(a_ref, b_ref, o_ref, acc_ref):
        k = pl.program_id(2)

        @pl.when(k == 0)
        def _():
            acc_ref[...] = jnp.zeros_like(acc_ref)

        acc_ref[...] += _dot(a_ref[...], b_ref[...]) if a_transposed else _dot_tn(a_ref[...], b_ref[...])

        @pl.when(k == nk - 1)
        def _():
            if by_column_block:
                for c in range(bn // by_column_block):
                    o_ref[c] = acc_ref[:, c * by_column_block:(c + 1) * by_column_block].astype(BF16)
            else:
                o_ref[...] = acc_ref[...].astype(BF16)

    if by_column_block:
        out_spec = pl.BlockSpec((bn // by_column_block, bm, by_column_block), lambda i, j, k: (j, i, 0))
        out_shape = jax.ShapeDtypeStruct((n // by_column_block, m, by_column_block), BF16)
    else:
        out_spec = pl.BlockSpec((bm, bn), lambda i, j, k: (i, j))
        out_shape = jax.ShapeDtypeStruct((m, n), BF16)
    a_spec = (pl.BlockSpec((bm, bt), lambda i, j, k: (i, k)) if a_transposed
              else pl.BlockSpec((bt, bm), lambda i, j, k: (k, i)))
    return _call(body, name, (m // bm, n // bn, nk), [a_spec, pl.BlockSpec((bt, bn), lambda i, j, k: (k, j))],
                 out_spec, out_shape, [pltpu.VMEM((bm, bn), F32)], ("parallel", "parallel", "arbitrary"), (a, b))


def _outproj_bwd(dx1, wo, tm, name):
    t, d = dx1.shape
    c2 = POOL_WIDTH + SG_WIDTH

    def body(dx_ref, wo_ref, dya_ref, dyb_ref, dyc_ref, dyct_ref, dxb_ref):
        dxb = dx_ref[...].astype(BF16)
        dxb_ref[...] = dxb
        dya_ref[...] = _dot_nt(dxb, wo_ref[0:POOL_WIDTH, :])
        dyb_ref[...] = _dot_nt(dxb, wo_ref[POOL_WIDTH:c2, :])
        dyc_ref[...] = _dot_nt(dxb, wo_ref[c2:, :]).astype(BF16)
        dyct_ref[...] = _dot_nt(wo_ref[c2:, :], dxb).astype(BF16)

    row = lambda w: pl.BlockSpec((tm, w), lambda i: (i, 0))
    return pl.pallas_call(
        body, name=name, grid=(t // tm,),
        in_specs=[row(d), pl.BlockSpec((d, d), lambda i: (0, 0))],
        out_specs=[row(POOL_WIDTH), row(SG_WIDTH), row(SB_WIDTH), pl.BlockSpec((SB_WIDTH, tm), lambda i: (0, i)), row(d)],
        out_shape=[jax.ShapeDtypeStruct((t, POOL_WIDTH), F32), jax.ShapeDtypeStruct((t, SG_WIDTH), F32),
                   jax.ShapeDtypeStruct((t, SB_WIDTH), BF16), jax.ShapeDtypeStruct((SB_WIDTH, t), BF16),
                   jax.ShapeDtypeStruct((t, d), BF16)],
        compiler_params=_params(("parallel",)),
    )(dx1, wo)


def _sba_bwd(qkv, ktb, dyc, dyct, cc, nvis, tq, name, ride=None):
    t = qkv.shape[0]
    nb = t // tq
    idx = jnp.arange(tq)
    upper = (idx[None, :] > idx[:, None]).astype(BF16)
    lower = (idx[None, :] < idx[:, None]).astype(BF16)

    qb = SB_QUERY_BLOCKS_PER_STEP

    def body(q_ref, k_ref, v_ref, kt_ref, do_ref, dot_ref, c_ref, n_ref, up_ref, lo_ref, dqt_ref, dk_ref, dv_ref):
        pair, step = pl.program_id(0), pl.program_id(1)
        refs = (q_ref, k_ref, v_ref, kt_ref, do_ref, dot_ref, c_ref, n_ref, up_ref, lo_ref, dqt_ref, dk_ref, dv_ref)
        subs = list(range(qb))

        @pl.when(step == 0)
        def _():
            dk_ref[...] = jnp.zeros_like(dk_ref)
            dv_ref[...] = jnp.zeros_like(dv_ref)
            for sub in subs:
                query_blocks([sub], pair, step, *refs)

        @pl.when(step > 0)
        def _():
            query_blocks(subs, pair, step, *refs)

    def query_blocks(subs, pair, step, q_ref, k_ref, v_ref, kt_ref, do_ref, dot_ref, c_ref, n_ref, up_ref, lo_ref,
                     dqt_ref, dk_ref, dv_ref):
        index = {sub: step * qb + sub for sub in subs}
        mine = {sub: slice(sub * tq, (sub + 1) * tq) for sub in subs}
        up = up_ref[...]
        lo = lo_ref[...]
        lane_head = lax.broadcasted_iota(jnp.int32, (1, 128), 1) // SB_HD
        sub_head = lax.broadcasted_iota(jnp.int32, (128, 1), 0) // SB_HD
        causal = (lax.broadcasted_iota(jnp.int32, (tq, tq), 0) < lax.broadcasted_iota(jnp.int32, (tq, tq), 1))
        hms = [lane_head == h for h in range(2)]
        qh, qs, doh, dot = {}, {}, {}, {}
        for sub in subs:
            q, do = q_ref[mine[sub], :], do_ref[mine[sub], :]
            qh[sub] = [jnp.where(hm, q, jnp.zeros_like(q)) for hm in hms]
            qs[sub] = [x * 0.125 for x in qh[sub]]
            doh[sub] = [jnp.where(hm, do, jnp.zeros_like(do)) for hm in hms]
            dot[sub] = dot_ref[:, mine[sub]]

        def blocks(work, carry):
            cgs = {sub: list(carry[sub][:2]) for sub in carry}
            dqt = {sub: carry[sub][2] for sub in carry}
            rows = [pl.ds(pl.multiple_of(j * tq, tq), tq) for _, j, _ in work]
            kj = [k_ref[r, :] for r in rows]
            vj = [v_ref[r, :] for r in rows]
            kt = [kt_ref[j] for _, j, _ in work]
            chains = [(w, h) for w in range(len(work)) for h in range(2)]
            z = [_dot_nt(kj[w], qh[work[w][0]][h]) for w, h in chains]
            da = [_dot(jnp.where(hms[h], vj[w], jnp.zeros_like(vj[w])), dot[work[w][0]]) for w, h in chains]
            ls = [_log2_sigmoids(zz * SB_SCALE) for zz in z]
            lb = [x[0] for x in ls]
            l1 = [jnp.where(causal, x[1], 0.0) if work[w][2] else x[1] for x, (w, h) in zip(ls, chains)]
            after = [_tri_dot(up, x) for x in l1]
            a = [jnp.exp2(lb[n] + after[n] + c_ref[h, pl.ds(work[w][1], 1), mine[work[w][0]]])
                 for n, (w, h) in enumerate(chains)]
            a = [jnp.where(causal, a[n], 0.0) if work[w][2] else a[n] for n, (w, h) in enumerate(chains)]
            g = [a[n] * da[n] for n in range(len(chains))]
            gloc = [_dot(lo, x.astype(BF16)) for x in g]
            dzb = []
            for n, (w, h) in enumerate(chains):
                sub, _, diag = work[w]
                gsum = gloc[n] + cgs[sub][h]
                dz = g[n] - jnp.exp2(lb[n]) * (g[n] + gsum)
                dzb.append((jnp.where(causal, dz, 0.0) if diag else dz).astype(BF16))
                cgs[sub][h] = gsum[tq - 1:tq, :] + g[n][tq - 1:tq, :]
            ab = [x.astype(BF16) for x in a]
            for n, (w, h) in enumerate(chains):
                sub = work[w][0]
                dqt[sub] = dqt[sub] + _dot(jnp.where(sub_head == h, kt[w], jnp.zeros_like(kt[w])), dzb[n])
            for w, (sub, _, _) in enumerate(work):
                dk_ref[rows[w], :] += _dot(dzb[2 * w], qs[sub][0]) + _dot(dzb[2 * w + 1], qs[sub][1])
                dv_ref[rows[w], :] += _dot(ab[2 * w], doh[sub][0]) + _dot(ab[2 * w + 1], doh[sub][1])
            return {sub: (cgs[sub][0], cgs[sub][1], dqt[sub]) for sub in cgs}

        zero = jnp.zeros((1, tq), F32)
        carry = {}
        for sub in subs:
            i = index[sub]
            n = jnp.clip(n_ref[pair, i].astype(jnp.int32), 1, i + 1)
            carry[sub] = lax.fori_loop(jnp.minimum(i + 1 - n, i - 1), i - 1,
                                       lambda s, cr, sub=sub: blocks([(sub, s, False)], {sub: cr})[sub],
                                       (zero, zero, jnp.zeros((128, tq), F32)))
        both = lambda: blocks([(sub, index[sub] - 1 + b, b == 1) for sub in subs for b in range(2)], carry)
        if len(subs) == 1:
            carry = lax.cond(index[subs[0]] > 0, both, lambda: blocks([(subs[0], index[subs[0]], True)], carry))
        else:
            carry = both()
        for sub in subs:
            dqt_ref[:, mine[sub]] = carry[sub][2] * 0.125

    return _call(
        body, name, (SB_PAIRS, nb // qb),
        [pl.BlockSpec((qb * tq, 128), lambda p, i: (i, p)),
         pl.BlockSpec((t, 128), lambda p, i: (0, SB_PAIRS + p)),
         pl.BlockSpec((t, 128), lambda p, i: (0, 2 * SB_PAIRS + p)),
         pl.BlockSpec((None, nb, 128, tq), lambda p, i: (p, 0, 0, 0)),
         pl.BlockSpec((qb * tq, 128), lambda p, i: (i, p)),
         pl.BlockSpec((128, qb * tq), lambda p, i: (p, i)),
         pl.BlockSpec((2, nb, qb * tq), lambda p, i: (p, 0, i)),
         pl.BlockSpec(memory_space=pltpu.SMEM),
         pl.BlockSpec((tq, tq), lambda p, i: (0, 0)),
         pl.BlockSpec((tq, tq), lambda p, i: (0, 0))],
        [pl.BlockSpec((128, qb * tq), lambda p, i: (p, i)),
         pl.BlockSpec((t, 128), lambda p, i: (0, p)),
         pl.BlockSpec((t, 128), lambda p, i: (0, p))],
        [jax.ShapeDtypeStruct((SB_WIDTH, t), F32), jax.ShapeDtypeStruct((t, SB_WIDTH), F32),
         jax.ShapeDtypeStruct((t, SB_WIDTH), F32)],
        [], ("arbitrary", "arbitrary"), (qkv, qkv, qkv, ktb, dyc, dyct, cc, nvis, upper, lower), ride)


def _sg_bwd(ab, dyb, gn, wm, wmt, bfull, tm, name):
    t = ab.shape[0]
    nt = t // tm
    sel = (jnp.arange(SG_WIDTH)[:, None] // SG_HD == jnp.arange(CHUNK)[None, :]).astype(F32)

    def body(u_ref, v_ref, dy_ref, gn_ref, wm_ref, wmt_ref, b_ref, sel_ref,
             dup_ref, dvp_ref, dgn_ref, dw_ref, db_ref, dbacc_ref):
        i = pl.program_id(0)

        @pl.when(i == 0)
        def _():
            dgn_ref[...] = jnp.zeros_like(dgn_ref)
            dw_ref[...] = jnp.zeros_like(dw_ref)
            dbacc_ref[...] = jnp.zeros_like(dbacc_ref)

        tril = (lax.broadcasted_iota(jnp.int32, (CHUNK, CHUNK), 0) >= lax.broadcasted_iota(jnp.int32, (CHUNK, CHUNK), 1))
        gn_ = gn_ref[...]
        for c in range(tm // CHUNK):
            rows = slice(c * CHUNK, (c + 1) * CHUNK)
            up, vp, dy = u_ref[rows, :], v_ref[rows, :], dy_ref[rows, :]
            u, v = _gelu(up), _gelu(vp)
            r = _rstd(v)
            vn = (v * r * gn_).astype(BF16)
            sv = b_ref[...]
            for h in range(SG_HEADS):
                sv = sv + jnp.where(_head_lanes(h), _dot(wm_ref[h], vn), 0.0)
            dup_ref[rows, :] = dy * sv * _gelu_grad(up)
            dsv = dy * u
            dbacc_ref[...] += dsv
            dvn = jnp.zeros((CHUNK, SG_WIDTH), F32)
            for h in range(SG_HEADS):
                dsv_h = jnp.where(_head_lanes(h), dsv, 0.0).astype(BF16)
                dvn = dvn + _dot(wmt_ref[h], dsv_h)
                dw_ref[h] += jnp.where(tril, _dot_nt(dsv_h, vn), 0.0)
            dv, dgn = _rms_bwd(v, r, gn_, dvn)
            dgn_ref[...] += dgn
            dvp_ref[rows, :] = dv * _gelu_grad(vp)

        @pl.when(i == nt - 1)
        def _():
            db_ref[...] = jnp.dot(dbacc_ref[...], sel_ref[...], preferred_element_type=F32,
                                  precision=lax.Precision.HIGHEST)

    const = lambda shape: pl.BlockSpec(shape, lambda i: tuple(0 for _ in shape))
    return pl.pallas_call(
        body, name=name, grid=(nt,),
        in_specs=[pl.BlockSpec((tm, SG_WIDTH), lambda i: (i, 1)), pl.BlockSpec((tm, SG_WIDTH), lambda i: (i, 2)),
                  pl.BlockSpec((tm, SG_WIDTH), lambda i: (i, 0)), const((1, SG_WIDTH)),
                  const((SG_HEADS, CHUNK, CHUNK)), const((SG_HEADS, CHUNK, CHUNK)), const((CHUNK, SG_WIDTH)),
                  const((SG_WIDTH, CHUNK))],
        out_specs=[pl.BlockSpec((tm, SG_WIDTH), lambda i: (i, 0)), pl.BlockSpec((tm, SG_WIDTH), lambda i: (i, 0)),
                   const((1, SG_WIDTH)), const((SG_HEADS, CHUNK, CHUNK)), const((CHUNK, CHUNK))],
        out_shape=[jax.ShapeDtypeStruct((t, SG_WIDTH), F32), jax.ShapeDtypeStruct((t, SG_WIDTH), F32),
                   jax.ShapeDtypeStruct((1, SG_WIDTH), F32), jax.ShapeDtypeStruct((SG_HEADS, CHUNK, CHUNK), F32),
                   jax.ShapeDtypeStruct((CHUNK, CHUNK), F32)],
        scratch_shapes=[pltpu.VMEM((CHUNK, SG_WIDTH), F32)],
        compiler_params=_params(("arbitrary",)),
    )(ab, ab, dyb, gn, wm, wmt, bfull, sel)


def _pool_bwd(ab, dya, wbd, scale, tm, name):
    t = ab.shape[0]
    nt = t // tm
    hb = tm // POOL_HALO
    nh = t // POOL_HALO

    def body(cur_ref, prev_ref, dy_ref, dyn_ref, w_ref, s_ref, da_ref, dw_ref, ds_ref):
        i = pl.program_id(0)

        @pl.when(i == 0)
        def _():
            dw_ref[...] = jnp.zeros_like(dw_ref)
            ds_ref[...] = jnp.zeros_like(ds_ref)

        prev = jnp.where(i == 0, 0.0, prev_ref[...])
        d = _pool_diff(cur_ref[...], prev, i * tm).astype(BF16)
        dy = dy_ref[...]
        ds_ref[...] += jnp.sum(dy * _dot(d, w_ref[...]), axis=0, keepdims=True)
        dyn = jnp.where(i == nt - 1, 0.0, dyn_ref[...])
        dys = (jnp.concatenate([dy, dyn], axis=0) * s_ref[...]).astype(BF16)
        dw_ref[...] += _dot_tn(d, dys[:tm])
        dd = _dot_nt(dys, w_ref[...])
        fwd = _pool_window_sums(dd / _pool_count(i * tm, tm + POOL_HALO), True)
        da_ref[...] = fwd[:tm] - dd[:tm]

    return pl.pallas_call(
        body, name=name, grid=(nt,),
        in_specs=[pl.BlockSpec((tm, POOL_WIDTH), lambda i: (i, 0)),
                  pl.BlockSpec((POOL_HALO, POOL_WIDTH), lambda i: (jnp.maximum(i * hb - 1, 0), 0)),
                  pl.BlockSpec((tm, POOL_WIDTH), lambda i: (i, 0)),
                  pl.BlockSpec((POOL_HALO, POOL_WIDTH), lambda i: (jnp.minimum((i + 1) * hb, nh - 1), 0)),
                  pl.BlockSpec((POOL_WIDTH, POOL_WIDTH), lambda i: (0, 0)),
                  pl.BlockSpec((1, POOL_WIDTH), lambda i: (0, 0))],
        out_specs=[pl.BlockSpec((tm, POOL_WIDTH), lambda i: (i, 0)),
                   pl.BlockSpec((POOL_WIDTH, POOL_WIDTH), lambda i: (0, 0)),
                   pl.BlockSpec((1, POOL_WIDTH), lambda i: (0, 0))],
        out_shape=[jax.ShapeDtypeStruct((t, POOL_WIDTH), F32), jax.ShapeDtypeStruct((POOL_WIDTH, POOL_WIDTH), F32),
                   jax.ShapeDtypeStruct((1, POOL_WIDTH), F32)],
        compiler_params=_params(("arbitrary",)),
    )(ab, ab, dya, dya, wbd, scale)


def _inproj_bwd(dx1, x, g, da, dup, dvp, dqt, dk, dv, w, tm, name, ride=None):
    t, d = x.shape
    n = w.shape[1]
    nt = t // tm

    def body(dx1_ref, x_ref, g_ref, da_ref, du_ref, dv_ref, dqt_ref, dk_ref, dvv_ref, w_ref,
             dx_ref, h_ref, dp_ref, dg_ref):
        @pl.when(pl.program_id(0) == 0)
        def _():
            dg_ref[...] = jnp.zeros_like(dg_ref)

        dp = jnp.concatenate([da_ref[...], du_ref[...], dv_ref[...], dqt_ref[...].T, dk_ref[...], dvv_ref[...]],
                             axis=1).astype(BF16)
        dp_ref[...] = dp
        xx = x_ref[...]
        r = _rstd(xx)
        h_ref[...] = (xx * r * g_ref[...]).astype(BF16)
        dx, dg = _rms_bwd(xx, r, g_ref[...], _dot_nt(dp, w_ref[...]))
        dx_ref[...] = dx1_ref[...] + dx
        dg_ref[...] += dg

    row = lambda w_: pl.BlockSpec((tm, w_), lambda i: (i, 0))
    return _call(
        body, name, (nt,),
        [row(d), row(d), pl.BlockSpec((1, d), lambda i: (0, 0)), row(POOL_WIDTH), row(SG_WIDTH),
         row(SG_WIDTH), pl.BlockSpec((SB_WIDTH, tm), lambda i: (0, i)), row(SB_WIDTH), row(SB_WIDTH),
         pl.BlockSpec((d, n), lambda i: (0, 0))],
        [row(d), row(d), row(n), pl.BlockSpec((1, d), lambda i: (0, 0))],
        [jax.ShapeDtypeStruct((t, d), F32), jax.ShapeDtypeStruct((t, d), BF16),
         jax.ShapeDtypeStruct((t, n), BF16), jax.ShapeDtypeStruct((1, d), F32)],
        [], ("arbitrary",), (dx1, x, g, da, dup, dvp, dqt, dk, dv, w), ride)


def _full_w_in(gathered):
    return gathered.transpose(1, 0, 2).reshape(D_MODEL, IN_COLS)


_SMALL_SHAPES = ((D_MODEL,), (4, POOL_GW, POOL_GW), (POOL_WIDTH,), (SG_WIDTH,), (SG_HEADS, CHUNK, CHUNK),
                 (SG_HEADS, CHUNK), (D_MODEL,))
_SMALL_SIZES = tuple(functools.reduce(lambda p, q: p * q, shp) for shp in _SMALL_SHAPES)
_SMALL_ROWS = sum(_SMALL_SIZES) // 128
_NORM1_ROWS = D_MODEL // 128


def _pack_small_layer(arrs):
    return jnp.concatenate([a.reshape(-1) for a in arrs]).reshape(_SMALL_ROWS, 128)


def _pack_small(arrs, final):
    return jnp.concatenate([_pack_small_layer([a[l] for a in arrs]) for l in range(DEPTH)] + [final.reshape(-1, 128)])


def _unpack_small(buf):
    per_layer = []
    for l in range(DEPTH):
        flat, off, outs = buf[l * _SMALL_ROWS:(l + 1) * _SMALL_ROWS].reshape(-1), 0, []
        for shp, size in zip(_SMALL_SHAPES, _SMALL_SIZES):
            outs.append(flat[off:off + size].reshape(shp))
            off += size
        per_layer.append(outs)
    return [jnp.stack([per_layer[l][k] for l in range(DEPTH)]) for k in range(len(_SMALL_SHAPES))] + \
           [buf[DEPTH * _SMALL_ROWS:].reshape(-1)]


def _tiles(t):
    return min(512, t), min(256, t // 4), min(4096, t)


def _layer_fwd(xl, wi, wo, wu, wd, small_w, l, ride_mlp=None):
    n1, pw, ps, sn, sw, sb, n2 = small_w
    tm, tq, _ = _tiles(xl.shape[0])
    wbd = jnp.zeros((4, POOL_GW, 4, POOL_GW), F32)
    for gi in range(4):
        wbd = wbd.at[gi, :, gi, :].set(pw[gi])
    wbd = wbd.reshape(POOL_WIDTH, POOL_WIDTH).astype(BF16)
    wm = (sw * jnp.tril(jnp.ones((CHUNK, CHUNK), F32))).astype(BF16)
    wmt = wm.transpose(0, 2, 1)
    bfull = jnp.repeat(sb.T, SG_HD, axis=1)
    g1, g2, psc, gn = n1[None, :], n2[None, :], ps[None, :], sn[None, :]

    ab, qkv, ktb, vtb = _inproj_fwd(xl, g1, wi, min(MLP_ROWS, xl.shape[0]), tq, f"inproj_fwd{l}")
    ya = _pool_fwd(ab, wbd, psc, tm, f"pool_fwd{l}")
    yb = _sg_fwd(ab, gn, wm, bfull, tm, f"sg_fwd{l}")
    shards = [w for w in (wo, wu, wd) if w.ndim == 2]
    if shards:
        (yct, cc, nvis), got = _sba_fwd(qkv, vtb, tq, f"sba_fwd{l}", SiblingGather(shards))
        got = iter(got)
        wo, wu, wd = (next(got) if w.ndim == 2 else w for w in (wo, wu, wd))
    else:
        yct, cc, nvis = _sba_fwd(qkv, vtb, tq, f"sba_fwd{l}")
    wo = wo.reshape(D_MODEL, D_MODEL)
    res = _outproj_mlp_fwd(xl, ya, yb, yct, wo, g2, wu, wd, min(MLP_ROWS, xl.shape[0]), f"mlp_fwd{l}", ride_mlp)
    (x1, u, x2), rode = res if ride_mlp is not None else (res, None)
    saved = dict(x=xl, ab=ab, qkv=qkv, ktb=ktb, cc=cc, nvis=nvis, ya=ya, yb=yb, yct=yct, x1=x1, u=u, wi=wi, wo=wo, wu=wu,
                 wd=wd, wbd=wbd, wm=wm, wmt=wmt, bfull=bfull, g1=g1, g2=g2, psc=psc, gn=gn)
    return x2, saved, rode


def _layer_bwd(dx, s, l, ride_mlp=None, exchange=True, gather_small=False):
    tm, tq, tw = _tiles(dx.shape[0])
    ktb = s["ktb"]
    res = _mlp_bwd(dx, s["x1"], s["g2"], s["u"], s["wu"], s["wd"], min(MLP_ROWS, dx.shape[0]), f"mlp_bwd{l}", ride_mlp)
    (dx1, du, r, h2, dx2b, dn2), rode = res if ride_mlp is not None else (res, None)
    dw_up = _matmul_tn(h2, du, D_MODEL, 2 * FF_SHARD, tw, f"dw_up{l}", by_column_block=FF_SHARD)
    dw_down = _matmul_tn(r, dx2b, 1024, D_MODEL, tw, f"dw_down{l}").reshape(N_DEV, FF_SHARD, D_MODEL)
    dya, dyb, dyc, dyct, dx1b = _outproj_bwd(dx1, s["wo"], tm, f"outproj_bwd{l}")
    dw_out = jnp.concatenate([
        _matmul_tn(jnp.concatenate([s["ya"], s["yb"]], axis=1), dx1b, POOL_WIDTH + SG_WIDTH, D_MODEL, tw, f"dw_out_ab{l}"),
        _matmul_tn(s["yct"], dx1b, SB_WIDTH, D_MODEL, tw, f"dw_out_c{l}", a_transposed=True)]
    ).reshape(N_DEV, OUT_SHARD, D_MODEL)
    if exchange:
        (dqt, dk, dv), (dw_out, dw_up, dw_down) = _sba_bwd(s["qkv"], ktb, dyc, dyct, s["cc"], s["nvis"], tq, f"sba_bwd{l}",
                                                           Exchange([dw_out, dw_up, dw_down], True))
    else:
        dqt, dk, dv = _sba_bwd(s["qkv"], ktb, dyc, dyct, s["cc"], s["nvis"], tq, f"sba_bwd{l}")
    dup, dvp, dgn, dwm, dbm = _sg_bwd(s["ab"], dyb, s["gn"], s["wm"], s["wmt"], s["bfull"], tm, f"sg_bwd{l}")
    da, dwbd, dpsc = _pool_bwd(s["ab"], dya, s["wbd"], s["psc"], tm, f"pool_bwd{l}")
    dpw = jnp.stack([dwbd[gi * POOL_GW:(gi + 1) * POOL_GW, gi * POOL_GW:(gi + 1) * POOL_GW] for gi in range(4)])
    small = _pack_small_layer([jnp.zeros((D_MODEL,), F32), dpw, dpsc[0], dgn[0], dwm, dbm[:, :SG_HEADS].T, dn2[0]])[_NORM1_ROWS:]
    res = _inproj_bwd(dx1, s["x"], s["g1"], da, dup, dvp, dqt, dk, dv, s["wi"], tm, f"inproj_bwd{l}",
                      Exchange([small], False) if gather_small else None)
    (dx, h1, dproj, dn1), small = (res[0], res[1][0]) if gather_small else (res, small)
    dw_in = _matmul_tn(h1, dproj, D_MODEL, IN_COLS // 3, tw, f"dw_in{l}")
    dw_in = dw_in.reshape(D_MODEL, N_DEV, IN_SHARD).transpose(1, 0, 2)
    return dx, (dw_in, dw_out, dw_up, dw_down), (dn1.reshape(_NORM1_ROWS, 128), small), rode


def kernel(x, norm1, w_in, pool_w, pool_scale, sg_norm, sg_w, sg_b, w_out, norm2, w_up, w_down, final_norm, loss_target, m_norm1, m_w_in, m_pool_w, m_pool_scale, m_sg_norm, m_sg_w, m_sg_b, m_w_out, m_norm2, m_w_up, m_w_down, m_final_norm, v_norm1, v_w_in, v_pool_w, v_pool_scale, v_sg_norm, v_sg_w, v_sg_b, v_w_out, v_norm2, v_w_up, v_w_down, v_final_norm):
    t = x.shape[1]
    small_w =(norm1, pool_w, pool_scale, sg_norm, sg_w, sg_b, norm2)
    big_w = (w_in, w_out, w_up, w_down)
    big_m = (m_w_in, m_w_out, m_w_up, m_w_down)
    big_v = (v_w_in, v_w_out, v_w_up, v_w_down)
    shards = [[w[l].astype(BF16) for w in big_w] for l in range(DEPTH)]

    wi0 = _full_w_in(SiblingGather(shards[0][:1]).alone("gather_w_in0")[0])
    sh_in1, sh_out1, sh_up1, sh_down1 = shards[1]
    x1, s0, (wi1, wo1, wd1) = _layer_fwd(x.reshape(t, D_MODEL), wi0, *shards[0][1:], tuple(w[0] for w in small_w), 0,
                                         ride_mlp=Exchange([sh_in1, sh_out1, sh_down1], False))
    x2, s1, _ = _layer_fwd(x1, _full_w_in(wi1), wo1, sh_up1, wd1, tuple(w[1] for w in small_w), 1)
    loss_local, dx, dfinal = _loss_grad(x2, final_norm[None, :], loss_target.reshape(t, D_MODEL), min(MLP_ROWS, t), "loss_grad")
    loss = lax.psum(loss_local[0, 0], MESH_AXES)

    dx, parts1, small1, _ = _layer_bwd(dx, s1, 1)
    early = jnp.concatenate(list(small1) + [dfinal.reshape(-1, 128)])
    dx, parts0, (dn1, small0), (recv_in1, early) = _layer_bwd(
        dx, s0, 0, ride_mlp=Exchange([parts1[0], early], [True, False]), gather_small=True)
    grad_x = dx.reshape(x.shape)
    recv_in0, dn1 = _exchange([parts0[0], dn1], [True, False], "scatter_w_in0_gather_norm1_0")
    small_all = jnp.concatenate([dn1, small0, early], axis=1)
    received = [[recv_in0] + list(parts0[1:]), [recv_in1] + list(parts1[1:])]

    big = [None] * 4
    for l in reversed(range(DEPTH)):
        for k in range(4):
            big[k] = _reduce_adamw(received[l][k], big_w[k], big_m[k], big_v[k], l, big[k], f"adamw{k}_{l}")

    sm = _reduce_adamw(
        small_all,
        _pack_small(small_w, final_norm)[None],
        _pack_small([m_norm1, m_pool_w, m_pool_scale, m_sg_norm, m_sg_w, m_sg_b, m_norm2], m_final_norm)[None],
        _pack_small([v_norm1, v_pool_w, v_pool_scale, v_sg_norm, v_sg_w, v_sg_b, v_norm2], v_final_norm)[None],
        0, None, "adamw_replicated")

    out = [loss, grad_x]
    for k in range(4):
        n1, pw, ps, sn, sw, sb, n2, fn = _unpack_small(sm[k][0])
        out += [n1, big[0][k], pw, ps, sn, sw, sb, big[1][k], n2, big[2][k], big[3][k], fn]
    return tuple(out)
```

```python
import functools

import jax
import jax.numpy as jnp
from jax import lax
from jax.experimental import pallas as pl
from jax.experimental.pallas import tpu as pltpu

F32 = jnp.float32
BF16 = jnp.bfloat16

D_MODEL = 1024
DEPTH = 2
POOL_WIDTH = 256
SG_WIDTH = 256
SB_WIDTH = 512
POOL_WINDOWS = (2, 4, 8, 16)
POOL_GW = 64
POOL_HALO = 16
CHUNK = 128
SG_HEADS = 4
SG_HD = 64
SB_HD = 64
SB_PAIRS = SB_WIDTH // 128
AB_COLS = POOL_WIDTH + 2 * SG_WIDTH
IN_COLS = AB_COLS + 3 * SB_WIDTH
D_FF = 4096
EPS = 1e-6
N_DEV = 8
FF_SHARD = D_FF // N_DEV
IN_SHARD = IN_COLS // N_DEV
OUT_SHARD = D_MODEL // N_DEV
ADAM_LR = 0.001
ADAM_B1 = 0.9
ADAM_B2 = 0.999
ADAM_EPS = 1e-08
ADAM_WD = 0.01
ADAM_STEP = 10
VMEM_LIMIT = 56 * 1024 * 1024
MLP_ROWS = 1024
MESH_AXES = ("x", "y", "c")


def _dot(a, b):
    return jnp.dot(a, b, preferred_element_type=F32)


def _dot_nt(a, b):
    return lax.dot_general(a, b, (((1,), (1,)), ((), ())), preferred_element_type=F32)


def _dot_tn(a, b):
    return lax.dot_general(a, b, (((0,), (0,)), ((), ())), preferred_element_type=F32)


def _rstd(x):
    return lax.rsqrt(jnp.mean(x * x, axis=-1, keepdims=True) + EPS)


def _rms_bwd(x, r, g, dh):
    gq = dh * g
    dx = r * gq - x * (r * r * r) * jnp.mean(gq * x, axis=-1, keepdims=True)
    dg = jnp.sum(dh * x * r, axis=0, keepdims=True)
    return dx, dg


def _params(sem=None):
    kw = dict(vmem_limit_bytes=VMEM_LIMIT)
    if sem is not None:
        kw["dimension_semantics"] = sem
    return pltpu.CompilerParams(**kw)


def _row_tile(rows, cap):
    best = 8
    for t in range(8, min(rows, cap) + 1, 8):
        if rows % t == 0:
            best = t
    return best


def _peer(k):
    x, y, c = lax.axis_index("x"), lax.axis_index("y"), lax.axis_index("c")
    return (1 - x if k & 4 else x, 1 - y if k & 2 else y, 1 - c if k & 1 else c)


def _my_index():
    return 4 * lax.axis_index("x") + 2 * lax.axis_index("y") + lax.axis_index("c")


class Exchange:
    def __init__(self, arrs, scatter):
        self.arrs = list(arrs)
        self.scatter = list(scatter) if isinstance(scatter, (list, tuple)) else [scatter] * len(self.arrs)
        self.n = len(self.arrs)
        self.any_specs = [pl.BlockSpec(memory_space=pl.ANY)] * self.n
        self.out_shape = [jax.ShapeDtypeStruct((N_DEV,) + a.shape[-2:], a.dtype) for a in self.arrs]
        self.sems = [pltpu.SemaphoreType.DMA((self.n, N_DEV - 1)), pltpu.SemaphoreType.DMA((self.n, N_DEV - 1)),
                     pltpu.SemaphoreType.DMA((self.n,))]

    def _copies(self, ins, outs, sems):
        send_sems, recv_sems, local_sems = sems
        me = _my_index()
        local, remote = [], []
        for a in range(self.n):
            sc = self.scatter[a]
            local.append(pltpu.make_async_copy(ins[a].at[me] if sc else ins[a], outs[a].at[me], local_sems.at[a]))
            for k in range(1, N_DEV):
                px, py, pc = _peer(k)
                remote.append(pltpu.make_async_remote_copy(
                    src_ref=ins[a].at[4 * px + 2 * py + pc] if sc else ins[a], dst_ref=outs[a].at[me],
                    send_sem=send_sems.at[a, k - 1], recv_sem=recv_sems.at[a, k - 1],
                    device_id=(px, py, pc), device_id_type=pl.DeviceIdType.MESH))
        return local, remote

    def start(self, ins, outs, sems):
        local, remote = self._copies(ins, outs, sems)
        for cp in local + remote:
            cp.start()

    def middle(self, ins, outs, sems):
        pass

    def wait(self, ins, outs, sems):
        local, remote = self._copies(ins, outs, sems)
        for cp in remote:
            cp.wait_recv()
        for cp in remote:
            cp.wait_send()
        for cp in local:
            cp.wait()

    def alone(self, name):
        n = self.n

        def body(*refs):
            ins, outs, sems = refs[:n], refs[n:2 * n], refs[2 * n:]
            self.start(ins, outs, sems)
            self.middle(ins, outs, sems)
            self.wait(ins, outs, sems)

        return pl.pallas_call(body, name=name, out_shape=self.out_shape, in_specs=self.any_specs,
                              out_specs=self.any_specs, scratch_shapes=self.sems)(*self.arrs)


class SiblingGather(Exchange):
    def __init__(self, arrs):
        super().__init__(arrs, False)

    def _plan(self, ins, outs, sems):
        send_sems, recv_sems, local_sems = sems
        x, y, core = lax.axis_index("x"), lax.axis_index("y"), lax.axis_index("c")
        me, sibling = (x, y, core), (x, y, 1 - core)
        chips = [(1 - x, y), (x, 1 - y), (1 - x, 1 - y)]
        plan = dict(local=[], first=[], arrived=[], passed=[], late=[])
        for a in range(self.n):
            slot = lambda p, a=a: outs[a].at[4 * p[0] + 2 * p[1] + p[2]]

            def copy(k, block, to, src=None, a=a, slot=slot):
                return pltpu.make_async_remote_copy(
                    src_ref=slot(block) if src is None else src, dst_ref=slot(block), send_sem=send_sems.at[a, k],
                    recv_sem=recv_sems.at[a, k], device_id=to, device_id_type=pl.DeviceIdType.MESH)

            plan["local"].append(pltpu.make_async_copy(ins[a], slot(me), local_sems.at[a]))
            plan["first"] += [copy(0, me, sibling, src=ins[a])] + [copy(1 + j, me, (*ch, core), src=ins[a])
                                                                  for j, ch in enumerate(chips)]
            plan["arrived"] += [copy(1 + j, (*ch, core), me) for j, ch in enumerate(chips)]
            plan["passed"] += [copy(4 + j, (*ch, core), sibling) for j, ch in enumerate(chips)]
            plan["late"] += [copy(0, sibling, me)] + [copy(4 + j, (*ch, 1 - core), me) for j, ch in enumerate(chips)]
        return plan

    def start(self, ins, outs, sems):
        plan = self._plan(ins, outs, sems)
        for cp in plan["local"] + plan["first"]:
            cp.start()

    def middle(self, ins, outs, sems):
        plan = self._plan(ins, outs, sems)
        for arrived, passed in zip(plan["arrived"], plan["passed"]):
            arrived.wait_recv()
            passed.start()

    def wait(self, ins, outs, sems):
        plan = self._plan(ins, outs, sems)
        for cp in plan["late"]:
            cp.wait_recv()
        for cp in plan["first"] + plan["passed"]:
            cp.wait_send()
        for cp in plan["local"]:
            cp.wait()


def _exchange(arrs, scatter, name):
    return Exchange(arrs, scatter).alone(name)


def _call(body, name, grid, in_specs, out_specs, out_shape, scratch_shapes, semantics, args, ride=None):
    if ride is None:
        return pl.pallas_call(body, name=name, grid=grid, in_specs=in_specs, out_specs=out_specs, out_shape=out_shape,
                              scratch_shapes=scratch_shapes, compiler_params=_params(semantics))(*args)
    single = not isinstance(out_shape, (list, tuple))
    out_specs, out_shape = ([out_specs], [out_shape]) if single else (list(out_specs), list(out_shape))
    n_in, n_out, n_scr, n = len(in_specs), len(out_specs), len(scratch_shapes), ride.n

    def riding(*refs):
        ins, cins = refs[:n_in], refs[n_in:n_in + n]
        outs, couts = refs[n_in + n:n_in + n + n_out], refs[n_in + n + n_out:n_in + 2 * n + n_out]
        scr, sems = refs[n_in + 2 * n + n_out:n_in + 2 * n + n_out + n_scr], refs[n_in + 2 * n + n_out + n_scr:]
        step = functools.reduce(lambda s, a: s * grid[a] + pl.program_id(a), range(len(grid)), 0)
        steps = functools.reduce(lambda p, q: p * q, grid)

        @pl.when(step == 0)
        def _():
            ride.start(cins, couts, sems)

        @pl.when(step == (3 * steps) // 4)
        def _():
            ride.middle(cins, couts, sems)

        body(*ins, *outs, *scr)

        @pl.when(step == steps - 1)
        def _():
            ride.wait(cins, couts, sems)

    res = pl.pallas_call(
        riding, name=name, grid=grid, in_specs=list(in_specs) + ride.any_specs, out_specs=out_specs + ride.any_specs,
        out_shape=out_shape + ride.out_shape, scratch_shapes=list(scratch_shapes) + ride.sems,
        compiler_params=_params(("arbitrary",) * len(grid)))(*args, *ride.arrs)
    own = res[0] if single else list(res[:n_out])
    return own, list(res[n_out:])


def _reduce_adamw(parts, w, m, v, l, prev, name):
    _, rows, n = parts.shape
    tr = _row_tile(rows, max(8, (1 << 18) // n))
    c1 = 1.0 - ADAM_B1 ** ADAM_STEP
    c2 = 1.0 - ADAM_B2 ** ADAM_STEP

    def body(p_ref, w_ref, m_ref, v_ref, *rest):
        g_ref, d_ref, nm_ref, nv_ref = rest[-4:]
        g = p_ref[0].astype(F32)
        for s in range(1, N_DEV):
            g = g + p_ref[s].astype(F32)
        nm = ADAM_B1 * m_ref[...] + (1.0 - ADAM_B1) * g
        nv = ADAM_B2 * v_ref[...] + (1.0 - ADAM_B2) * (g * g)
        m_hat = nm / c1
        v_hat = nv / c2
        g_ref[...] = g
        d_ref[...] = -ADAM_LR * (m_hat / (jnp.sqrt(v_hat) + ADAM_EPS) + ADAM_WD * w_ref[...])
        nm_ref[...] = nm
        nv_ref[...] = nv

    blk = pl.BlockSpec((None, tr, n), lambda i: (l, i, 0))
    out = jax.ShapeDtypeStruct(w.shape, F32)
    prev = list(prev) if prev is not None else []
    return pl.pallas_call(
        body, name=name, grid=(rows // tr,),
        in_specs=[pl.BlockSpec((N_DEV, tr, n), lambda i: (0, i, 0)), blk, blk, blk] + [pl.BlockSpec(memory_space=pl.ANY)] * len(prev),
        out_specs=[blk, blk, blk, blk], out_shape=[out, out, out, out],
        input_output_aliases={4 + k: k for k in range(len(prev))},
        compiler_params=_params(("parallel",)),
    )(parts, w, m, v, *prev)


def _inproj_fwd(x, g, w, tm, tq, name):
    t, d = x.shape
    n = w.shape[1]
    nb = t // tq
    per = tm // tq

    def body(x_ref, g_ref, w_ref, ab_ref, qkv_ref, kt_ref, vt_ref):
        xx = x_ref[...]
        h = (xx * _rstd(xx) * g_ref[...]).astype(BF16)
        ab_ref[...] = _dot(h, w_ref[:, :AB_COLS])
        qkv = _dot(h, w_ref[:, AB_COLS:])
        qkv_ref[...] = qkv.astype(BF16)
        for which, out_ref in ((1, kt_ref), (2, vt_ref)):
            for p in range(SB_PAIRS):
                for b in range(per):
                    cols = which * SB_WIDTH + p * 128
                    out_ref[p, b] = qkv[b * tq:(b + 1) * tq, cols:cols + 128].T.astype(BF16)

    tb = pl.BlockSpec((SB_PAIRS, per, 128, tq), lambda i: (0, i, 0, 0))
    tshape = jax.ShapeDtypeStruct((SB_PAIRS, nb, 128, tq), BF16)
    return _call(
        body, name, (t // tm,),
        [pl.BlockSpec((tm, d), lambda i: (i, 0)), pl.BlockSpec((1, d), lambda i: (0, 0)),
         pl.BlockSpec((d, n), lambda i: (0, 0))],
        [pl.BlockSpec((tm, AB_COLS), lambda i: (i, 0)), pl.BlockSpec((tm, n - AB_COLS), lambda i: (i, 0)), tb, tb],
        [jax.ShapeDtypeStruct((t, AB_COLS), F32), jax.ShapeDtypeStruct((t, n - AB_COLS), BF16), tshape, tshape],
        [], ("parallel",), (x, g, w))


def _pool_window_sums(xx, forward):
    n = xx.shape[0]
    sh = (lambda k: n - k) if forward else (lambda k: k)
    s2 = xx + pltpu.roll(xx, sh(1), 0)
    s4 = s2 + pltpu.roll(s2, sh(2), 0)
    s8 = s4 + pltpu.roll(s4, sh(4), 0)
    s16 = s8 + pltpu.roll(s8, sh(8), 0)
    grp = lax.broadcasted_iota(jnp.int32, (1, POOL_WIDTH), 1) // POOL_GW
    return jnp.where(grp == 0, s2, jnp.where(grp == 1, s4, jnp.where(grp == 2, s8, s16)))


def _pool_count(t0, rows):
    grp = lax.broadcasted_iota(jnp.int32, (1, POOL_WIDTH), 1) // POOL_GW
    win = jnp.where(grp == 0, 2, jnp.where(grp == 1, 4, jnp.where(grp == 2, 8, 16)))
    tt = t0 + lax.broadcasted_iota(jnp.int32, (rows, 1), 0)
    return jnp.minimum(tt + 1, win).astype(F32)


def _pool_diff(cur, prev, t0):
    tm = cur.shape[0]
    sums = _pool_window_sums(jnp.concatenate([prev, cur], axis=0), False)[POOL_HALO:]
    return sums / _pool_count(t0, tm) - cur


def _pool_sg_fwd(ab, wbd, scale, gn, wm, bfull, tm, name):
    t = ab.shape[0]
    hb = tm // POOL_HALO

    def body(cur_ref, prev_ref, u_ref, v_ref, w_ref, s_ref, gn_ref, wm_ref, b_ref, y_ref):
        i = pl.program_id(0)
        prev = jnp.where(i == 0, 0.0, prev_ref[...])
        d = _pool_diff(cur_ref[...], prev, i * tm).astype(BF16)
        y_ref[:, :POOL_WIDTH] = (_dot(d, w_ref[...]) * s_ref[...]).astype(BF16)
        v = _gelu(v_ref[...])
        vn = (v * _rstd(v) * gn_ref[...]).astype(BF16)
        for c in range(tm // CHUNK):
            rows = slice(c * CHUNK, (c + 1) * CHUNK)
            vc = vn[rows]
            sv = b_ref[...]
            for h in range(SG_HEADS):
                sv = sv + jnp.where(_head_lanes(h), _dot(wm_ref[h], vc), 0.0)
            y_ref[rows, POOL_WIDTH:] = (_gelu(u_ref[rows, :]) * sv).astype(BF16)

    return pl.pallas_call(
        body, name=name, grid=(t // tm,),
        in_specs=[pl.BlockSpec((tm, POOL_WIDTH), lambda i: (i, 0)),
                  pl.BlockSpec((POOL_HALO, POOL_WIDTH), lambda i: (jnp.maximum(i * hb - 1, 0), 0)),
                  pl.BlockSpec((tm, SG_WIDTH), lambda i: (i, 1)), pl.BlockSpec((tm, SG_WIDTH), lambda i: (i, 2)),
                  pl.BlockSpec((POOL_WIDTH, POOL_WIDTH), lambda i: (0, 0)),
                  pl.BlockSpec((1, POOL_WIDTH), lambda i: (0, 0)),
                  pl.BlockSpec((1, SG_WIDTH), lambda i: (0, 0)),
                  pl.BlockSpec((SG_HEADS, CHUNK, CHUNK), lambda i: (0, 0, 0)),
                  pl.BlockSpec((CHUNK, SG_WIDTH), lambda i: (0, 0))],
        out_specs=pl.BlockSpec((tm, POOL_WIDTH + SG_WIDTH), lambda i: (i, 0)),
        out_shape=jax.ShapeDtypeStruct((t, POOL_WIDTH + SG_WIDTH), BF16),
        compiler_params=_params(("parallel",)),
    )(ab, ab, ab, ab, wbd, scale, gn, wm, bfull)


_GELU_K = 0.7978845608028654
_GELU_A = 0.044715


def _gelu(x):
    return 0.5 * x * (1.0 + jnp.tanh(_GELU_K * (x + _GELU_A * (x * x * x))))


def _gelu_grad(x):
    th = jnp.tanh(_GELU_K * (x + _GELU_A * (x * x * x)))
    return 0.5 * (1.0 + th) + 0.5 * x * (1.0 - th * th) * (_GELU_K * (1.0 + 3.0 * _GELU_A * (x * x)))


def _head_lanes(h):
    return lax.broadcasted_iota(jnp.int32, (1, SG_WIDTH), 1) // SG_HD == h


LOG2E = 1.4426950408889634
SB_SCALE = 0.125 * LOG2E
SB_DEAD_LOG2 = 152.0
SB_QUERY_BLOCKS_PER_STEP = 2


def _log2_sigmoids(y):
    neg_abs = lax.bitcast_convert_type(lax.bitcast_convert_type(y, jnp.uint32) | jnp.uint32(0x80000000), F32)
    lb = jnp.minimum(y, 0.0) - jnp.log(1.0 + jnp.exp2(neg_abs)) * LOG2E
    return lb, lb - y


def _split(x):
    hi = x.astype(BF16)
    return hi, (x - hi.astype(F32)).astype(BF16)


def _tri_dot(tri, x):
    hi, lo = _split(x)
    return _dot(tri, hi) + _dot(tri, lo)


def _sba_fwd(qkv, vtb, tq, name, ride=None):
    t = qkv.shape[0]
    nb = t // tq
    qb = SB_QUERY_BLOCKS_PER_STEP
    upper = (jnp.arange(tq)[None, :] > jnp.arange(tq)[:, None]).astype(BF16)

    def body(q_ref, k_ref, vt_ref, up_ref, ot_ref, c_ref, n_ref):
        pair, step = pl.program_id(0), pl.program_id(1)
        refs = (q_ref, k_ref, vt_ref, up_ref, ot_ref, c_ref, n_ref)
        subs = list(range(qb))

        @pl.when(step == 0)
        def _():
            for sub in subs:
                query_blocks([sub], pair, step, *refs)

        @pl.when(step > 0)
        def _():
            query_blocks(subs, pair, step, *refs)

    def query_blocks(subs, pair, step, q_ref, k_ref, vt_ref, up_ref, ot_ref, c_ref, n_ref):
        index = {sub: step * qb + sub for sub in subs}
        mine = {sub: slice(sub * tq, (sub + 1) * tq) for sub in subs}
        up = up_ref[...]
        lane_head = lax.broadcasted_iota(jnp.int32, (1, 128), 1) // SB_HD
        sub_head = lax.broadcasted_iota(jnp.int32, (128, 1), 0) // SB_HD
        causal = (lax.broadcasted_iota(jnp.int32, (tq, tq), 0) < lax.broadcasted_iota(jnp.int32, (tq, tq), 1))
        qh = {}
        for sub in subs:
            q = q_ref[mine[sub], :]
            qh[sub] = [jnp.where(lane_head == h, q, jnp.zeros_like(q)) for h in range(2)]

        def blocks(work, carry):
            cs = {sub: list(carry[sub][:2]) for sub in carry}
            acc = {sub: carry[sub][2] for sub in carry}
            kj = [k_ref[pl.ds(pl.multiple_of(j * tq, tq), tq), :] for _, j, _ in work]
            vt = [vt_ref[j] for _, j, _ in work]
            chains = [(w, h) for w in range(len(work)) for h in range(2)]
            z = [_dot_nt(kj[w], qh[work[w][0]][h]) for w, h in chains]
            ls = [_log2_sigmoids(zz * SB_SCALE) for zz in z]
            lb = [x[0] for x in ls]
            l1 = [jnp.where(causal, x[1], 0.0) if work[w][2] else x[1] for x, (w, h) in zip(ls, chains)]
            after = [_tri_dot(up, x) for x in l1]
            a = []
            for n, (w, h) in enumerate(chains):
                sub, j, diag = work[w]
                c_ref[h, pl.ds(j, 1), mine[sub]] = cs[sub][h]
                an = jnp.exp2(lb[n] + after[n] + cs[sub][h])
                a.append(jnp.where(causal, an, 0.0) if diag else an)
                cs[sub][h] = cs[sub][h] + after[n][0:1, :] + l1[n][0:1, :]
            alive = {sub: (jnp.max(jnp.maximum(*cs[sub])) > -SB_DEAD_LOG2).astype(jnp.int32) for sub in cs}
            for n, (w, h) in enumerate(chains):
                sub = work[w][0]
                acc[sub] = acc[sub] + _dot(jnp.where(sub_head == h, vt[w], jnp.zeros_like(vt[w])), a[n].astype(BF16))
            return {sub: (alive[sub], cs[sub][0], cs[sub][1], acc[sub]) for sub in cs}

        zero = jnp.zeros((1, tq), F32)
        start = {sub: (zero, zero, jnp.zeros((128, tq), F32)) for sub in subs}
        both = lambda: blocks([(sub, index[sub] - b, b == 0) for sub in subs for b in range(2)], start)
        if len(subs) == 1:
            joint = lax.cond(index[subs[0]] > 0, both, lambda: blocks([(subs[0], index[subs[0]], True)], start))
        else:
            joint = both()
        for sub in subs:
            i = index[sub]

            def left(state, sub=sub, i=i):
                s, _, c0, c1, acc = state
                return (s + 1,) + blocks([(sub, i - 1 - s, False)], {sub: (c0, c1, acc)})[sub]

            state = lax.while_loop(lambda st, i=i: (st[0] < i) & (st[1] > 0), left, (jnp.minimum(i, 1),) + tuple(joint[sub]))
            ot_ref[:, mine[sub]] = state[4].astype(BF16)
            n_ref[pair, i] = (state[0] + 1).astype(F32)

    return _call(
        body, name, (SB_PAIRS, nb // qb),
        [pl.BlockSpec((qb * tq, 128), lambda p, i: (i, p)),
         pl.BlockSpec((t, 128), lambda p, i: (0, SB_PAIRS + p)),
         pl.BlockSpec((None, nb, 128, tq), lambda p, i: (p, 0, 0, 0)),
         pl.BlockSpec((tq, tq), lambda p, i: (0, 0))],
        [pl.BlockSpec((128, qb * tq), lambda p, i: (p, i)),
         pl.BlockSpec((2, nb, qb * tq), lambda p, i: (p, 0, i)),
         pl.BlockSpec(memory_space=pltpu.SMEM)],
        [jax.ShapeDtypeStruct((SB_WIDTH, t), BF16),
         jax.ShapeDtypeStruct((2 * SB_PAIRS, nb, t), F32),
         jax.ShapeDtypeStruct((SB_PAIRS, nb), F32)],
        [], ("arbitrary", "arbitrary"), (qkv, qkv, vtb, upper), ride)


def _outproj_mlp_fwd(x, yab, yct, wo, g2, wup, wdn, tm, name, ride=None):
    t, d = x.shape
    nf = wup.shape[0]

    def body(x_ref, ya_ref, yb_ref, yct_ref, wo_ref, g_ref, wu_ref, wd_ref, x1_ref, u_ref, x2_ref, h_ref, acc_ref):
        j = pl.program_id(1)

        @pl.when(j == 0)
        def _():
            x1 = (x_ref[...] + _dot(ya_ref[...], wo_ref[0:POOL_WIDTH, :])
                  + _dot(yb_ref[...], wo_ref[POOL_WIDTH:POOL_WIDTH + SG_WIDTH, :])
                  + _dot_tn(yct_ref[...], wo_ref[POOL_WIDTH + SG_WIDTH:, :]))
            x1_ref[...] = x1
            h_ref[...] = (x1 * _rstd(x1) * g_ref[...]).astype(BF16)
            acc_ref[...] = x1

        u = _dot(h_ref[...], wu_ref[...])
        u_ref[...] = u
        r = jnp.maximum(u, 0.0)
        acc_ref[...] += _dot((r * r).astype(BF16), wd_ref[...])

        @pl.when(j == nf - 1)
        def _():
            x2_ref[...] = acc_ref[...]

    row = lambda w: pl.BlockSpec((tm, w), lambda i, j: (i, 0))
    return _call(
        body, name, (t // tm, nf),
        [row(d), row(POOL_WIDTH), pl.BlockSpec((tm, SG_WIDTH), lambda i, j: (i, 1)), pl.BlockSpec((SB_WIDTH, tm), lambda i, j: (0, i)),
         pl.BlockSpec((d, d), lambda i, j: (0, 0)), pl.BlockSpec((1, d), lambda i, j: (0, 0)),
         pl.BlockSpec((None, d, FF_SHARD), lambda i, j: (j, 0, 0)),
         pl.BlockSpec((None, FF_SHARD, d), lambda i, j: (j, 0, 0))],
        [row(d), pl.BlockSpec((tm, FF_SHARD), lambda i, j: (i, j)), row(d)],
        [jax.ShapeDtypeStruct((t, d), F32), jax.ShapeDtypeStruct((t, nf * FF_SHARD), F32),
         jax.ShapeDtypeStruct((t, d), F32)],
        [pltpu.VMEM((tm, d), BF16), pltpu.VMEM((tm, d), F32)],
        ("parallel", "arbitrary"), (x, yab, yab, yct, wo, g2, wup, wdn), ride)


def _loss_grad(x, g, target, tm, name):
    t, d = x.shape
    nt = t // tm

    def body(x_ref, g_ref, t_ref, loss_ref, dx_ref, dg_ref, sq_ref):
        i = pl.program_id(0)

        @pl.when(i == 0)
        def _():
            sq_ref[...] = jnp.zeros_like(sq_ref)
            dg_ref[...] = jnp.zeros_like(dg_ref)

        xx = x_ref[...]
        r = _rstd(xx)
        err = xx * r * g_ref[...] - t_ref[...]
        sq_ref[...] += jnp.sum(err * err, axis=0, keepdims=True)
        dx, dg = _rms_bwd(xx, r, g_ref[...], err * (1.0 / d))
        dx_ref[...] = dx
        dg_ref[...] += dg

        @pl.when(i == nt - 1)
        def _():
            loss_ref[...] = jnp.sum(sq_ref[...], axis=1, keepdims=True) * (0.5 / d)

    return pl.pallas_call(
        body, name=name, grid=(nt,),
        in_specs=[pl.BlockSpec((tm, d), lambda i: (i, 0)), pl.BlockSpec((1, d), lambda i: (0, 0)),
                  pl.BlockSpec((tm, d), lambda i: (i, 0))],
        out_specs=[pl.BlockSpec((1, 1), lambda i: (0, 0)), pl.BlockSpec((tm, d), lambda i: (i, 0)),
                   pl.BlockSpec((1, d), lambda i: (0, 0))],
        out_shape=[jax.ShapeDtypeStruct((1, 1), F32), jax.ShapeDtypeStruct((t, d), F32),
                   jax.ShapeDtypeStruct((1, d), F32)],
        scratch_shapes=[pltpu.VMEM((1, d), F32)],
        compiler_params=_params(("arbitrary",)),
    )(x, g, target)


def _mlp_bwd(dx2, x1, g2, u, wup, wdn, tm, name, ride=None):
    t, d = dx2.shape
    nf = wup.shape[0]
    nt = t // tm

    def body(dx2_ref, x1_ref, g_ref, u_ref, wu_ref, wd_ref, dx1_ref, du_ref, r_ref, h_ref, dxb_ref, dg_ref, acc_ref):
        i, j = pl.program_id(0), pl.program_id(1)

        @pl.when(j == 0)
        def _():
            dxb_ref[...] = dx2_ref[...].astype(BF16)
            acc_ref[...] = jnp.zeros_like(acc_ref)

        @pl.when((i == 0) & (j == 0))
        def _():
            dg_ref[...] = jnp.zeros_like(dg_ref)

        share = pl.ds(pl.multiple_of(j * (tm // nf), tm // nf), tm // nf)
        xs = x1_ref[share, :]
        h_ref[share, :] = (xs * _rstd(xs) * g_ref[...]).astype(BF16)
        dr = _dot_nt(dxb_ref[...], wd_ref[...])
        ru = jnp.maximum(u_ref[...], 0.0)
        du = (dr * (2.0 * ru)).astype(BF16)
        du_ref[...] = du
        r_ref[...] = (ru * ru).astype(BF16)
        acc_ref[...] += _dot_nt(du, wu_ref[...])

        @pl.when(j == nf - 1)
        def _():
            x1 = x1_ref[...]
            dx, dg = _rms_bwd(x1, _rstd(x1), g_ref[...], acc_ref[...])
            dx1_ref[...] = dx2_ref[...] + dx
            dg_ref[...] += dg

    row = lambda w: pl.BlockSpec((tm, w), lambda i, j: (i, 0))
    col = pl.BlockSpec((tm, FF_SHARD), lambda i, j: (i, j))
    return _call(
        body, name, (nt, nf),
        [row(d), row(d), pl.BlockSpec((1, d), lambda i, j: (0, 0)), col,
         pl.BlockSpec((None, d, FF_SHARD), lambda i, j: (j, 0, 0)),
         pl.BlockSpec((None, FF_SHARD, d), lambda i, j: (j, 0, 0))],
        [row(d), col, col, row(d), row(d), pl.BlockSpec((1, d), lambda i, j: (0, 0))],
        [jax.ShapeDtypeStruct((t, d), F32), jax.ShapeDtypeStruct((t, nf * FF_SHARD), BF16),
         jax.ShapeDtypeStruct((t, nf * FF_SHARD), BF16), jax.ShapeDtypeStruct((t, d), BF16),
         jax.ShapeDtypeStruct((t, d), BF16), jax.ShapeDtypeStruct((1, d), F32)],
        [pltpu.VMEM((tm, d), F32)], ("arbitrary", "arbitrary"), (dx2, x1, g2, u, wup, wdn), ride)


def _matmul_tn(a, b, bm, bn, bt, name, by_column_block=0, a_transposed=False):
    m, t = a.shape if a_transposed else a.shape[::-1]
    n = b.shape[1]
    nk = t // bt

    def body(a_ref, b_ref, o_ref, acc_ref):
        k = pl.program_id(2)

        @pl.when(k == 0)
        def _():
            acc_ref[...] = jnp.zeros_like(acc_ref)

        acc_ref[...] += _dot(a_ref[...], b_ref[...]) if a_transposed else _dot_tn(a_ref[...], b_ref[...])

        @pl.when(k == nk - 1)
        def _():
            if by_column_block:
                for c in range(bn // by_column_block):
                    o_ref[c] = acc_ref[:, c * by_column_block:(c + 1) * by_column_block].astype(BF16)
            else:
                o_ref[...] = acc_ref[...].astype(BF16)

    if by_column_block:
        out_spec = pl.BlockSpec((bn // by_column_block, bm, by_column_block), lambda i, j, k: (j, i, 0))
        out_shape = jax.ShapeDtypeStruct((n // by_column_block, m, by_column_block), BF16)
    else:
        out_spec = pl.BlockSpec((bm, bn), lambda i, j, k: (i, j))
        out_shape = jax.ShapeDtypeStruct((m, n), BF16)
    a_spec = (pl.BlockSpec((bm, bt), lambda i, j, k: (i, k)) if a_transposed
              else pl.BlockSpec((bt, bm), lambda i, j, k: (k, i)))
    return _call(body, name, (m // bm, n // bn, nk), [a_spec, pl.BlockSpec((bt, bn), lambda i, j, k: (k, j))],
                 out_spec, out_shape, [pltpu.VMEM((bm, bn), F32)], ("parallel", "parallel", "arbitrary"), (a, b))


def _outproj_bwd(dx1, wo, tm, name):
    t, d = dx1.shape
    c2 = POOL_WIDTH + SG_WIDTH

    def body(dx_ref, wo_ref, dya_ref, dyb_ref, dyc_ref, dyct_ref, dxb_ref):
        dxb = dx_ref[...].astype(BF16)
        dxb_ref[...] = dxb
        dya_ref[...] = _dot_nt(dxb, wo_ref[0:POOL_WIDTH, :])
        dyb_ref[...] = _dot_nt(dxb, wo_ref[POOL_WIDTH:c2, :])
        dyc_ref[...] = _dot_nt(dxb, wo_ref[c2:, :]).astype(BF16)
        dyct_ref[...] = _dot_nt(wo_ref[c2:, :], dxb).astype(BF16)

    row = lambda w: pl.BlockSpec((tm, w), lambda i: (i, 0))
    return pl.pallas_call(
        body, name=name, grid=(t // tm,),
        in_specs=[row(d), pl.BlockSpec((d, d), lambda i: (0, 0))],
        out_specs=[row(POOL_WIDTH), row(SG_WIDTH), row(SB_WIDTH), pl.BlockSpec((SB_WIDTH, tm), lambda i: (0, i)), row(d)],
        out_shape=[jax.ShapeDtypeStruct((t, POOL_WIDTH), F32), jax.ShapeDtypeStruct((t, SG_WIDTH), F32),
                   jax.ShapeDtypeStruct((t, SB_WIDTH), BF16), jax.ShapeDtypeStruct((SB_WIDTH, t), BF16),
                   jax.ShapeDtypeStruct((t, d), BF16)],
        compiler_params=_params(("parallel",)),
    )(dx1, wo)


def _sba_bwd(qkv, ktb, dyc, dyct, cc, nvis, tq, name, ride=None):
    t = qkv.shape[0]
    nb = t // tq
    idx = jnp.arange(tq)
    upper = (idx[None, :] > idx[:, None]).astype(BF16)
    lower = (idx[None, :] < idx[:, None]).astype(BF16)

    qb = SB_QUERY_BLOCKS_PER_STEP

    def body(q_ref, k_ref, v_ref, kt_ref, do_ref, dot_ref, c_ref, n_ref, up_ref, lo_ref, dqt_ref, dk_ref, dv_ref):
        pair, step = pl.program_id(0), pl.program_id(1)
        refs = (q_ref, k_ref, v_ref, kt_ref, do_ref, dot_ref, c_ref, n_ref, up_ref, lo_ref, dqt_ref, dk_ref, dv_ref)
        subs = list(range(qb))

        @pl.when(step == 0)
        def _():
            dk_ref[...] = jnp.zeros_like(dk_ref)
            dv_ref[...] = jnp.zeros_like(dv_ref)
            for sub in subs:
                query_blocks([sub], pair, step, *refs)

        @pl.when(step > 0)
        def _():
            query_blocks(subs, pair, step, *refs)

    def query_blocks(subs, pair, step, q_ref, k_ref, v_ref, kt_ref, do_ref, dot_ref, c_ref, n_ref, up_ref, lo_ref,
                     dqt_ref, dk_ref, dv_ref):
        index = {sub: step * qb + sub for sub in subs}
        mine = {sub: slice(sub * tq, (sub + 1) * tq) for sub in subs}
        up = up_ref[...]
        lo = lo_ref[...]
        lane_head = lax.broadcasted_iota(jnp.int32, (1, 128), 1) // SB_HD
        sub_head = lax.broadcasted_iota(jnp.int32, (128, 1), 0) // SB_HD
        causal = (lax.broadcasted_iota(jnp.int32, (tq, tq), 0) < lax.broadcasted_iota(jnp.int32, (tq, tq), 1))
        hms = [lane_head == h for h in range(2)]
        qh, qs, doh, dot = {}, {}, {}, {}
        for sub in subs:
            q, do = q_ref[mine[sub], :], do_ref[mine[sub], :]
            qh[sub] = [jnp.where(hm, q, jnp.zeros_like(q)) for hm in hms]
            qs[sub] = [x * 0.125 for x in qh[sub]]
            doh[sub] = [jnp.where(hm, do, jnp.zeros_like(do)) for hm in hms]
            dot[sub] = dot_ref[:, mine[sub]]

        def blocks(work, carry):
            cgs = {sub: list(carry[sub][:2]) for sub in carry}
            dqt = {sub: carry[sub][2] for sub in carry}
            rows = [pl.ds(pl.multiple_of(j * tq, tq), tq) for _, j, _ in work]
            kj = [k_ref[r, :] for r in rows]
            vj = [v_ref[r, :] for r in rows]
            kt = [kt_ref[j] for _, j, _ in work]
            chains = [(w, h) for w in range(len(work)) for h in range(2)]
            z = [_dot_nt(kj[w], qh[work[w][0]][h]) for w, h in chains]
            da = [_dot(jnp.where(hms[h], vj[w], jnp.zeros_like(vj[w])), dot[work[w][0]]) for w, h in chains]
            ls = [_log2_sigmoids(zz * SB_SCALE) for zz in z]
            lb = [x[0] for x in ls]
            l1 = [jnp.where(causal, x[1], 0.0) if work[w][2] else x[1] for x, (w, h) in zip(ls, chains)]
            after = [_tri_dot(up, x) for x in l1]
            a = [jnp.exp2(lb[n] + after[n] + c_ref[h, pl.ds(work[w][1], 1), mine[work[w][0]]])
                 for n, (w, h) in enumerate(chains)]
            a = [jnp.where(causal, a[n], 0.0) if work[w][2] else a[n] for n, (w, h) in enumerate(chains)]
            g = [a[n] * da[n] for n in range(len(chains))]
            gloc = [_dot(lo, x.astype(BF16)) for x in g]
            dzb = []
            for n, (w, h) in enumerate(chains):
                sub, _, diag = work[w]
                gsum = gloc[n] + cgs[sub][h]
                dz = g[n] - jnp.exp2(lb[n]) * (g[n] + gsum)
                dzb.append((jnp.where(causal, dz, 0.0) if diag else dz).astype(BF16))
                cgs[sub][h] = gsum[tq - 1:tq, :] + g[n][tq - 1:tq, :]
            ab = [x.astype(BF16) for x in a]
            for n, (w, h) in enumerate(chains):
                sub = work[w][0]
                dqt[sub] = dqt[sub] + _dot(jnp.where(sub_head == h, kt[w], jnp.zeros_like(kt[w])), dzb[n])
            for w, (sub, _, _) in enumerate(work):
                dk_ref[rows[w], :] += _dot(dzb[2 * w], qs[sub][0]) + _dot(dzb[2 * w + 1], qs[sub][1])
                dv_ref[rows[w], :] += _dot(ab[2 * w], doh[sub][0]) + _dot(ab[2 * w + 1], doh[sub][1])
            return {sub: (cgs[sub][0], cgs[sub][1], dqt[sub]) for sub in cgs}

        zero = jnp.zeros((1, tq), F32)
        carry = {}
        for sub in subs:
            i = index[sub]
            n = jnp.clip(n_ref[pair, i].astype(jnp.int32), 1, i + 1)
            carry[sub] = lax.fori_loop(jnp.minimum(i + 1 - n, i - 1), i - 1,
                                       lambda s, cr, sub=sub: blocks([(sub, s, False)], {sub: cr})[sub],
                                       (zero, zero, jnp.zeros((128, tq), F32)))
        both = lambda: blocks([(sub, index[sub] - 1 + b, b == 1) for sub in subs for b in range(2)], carry)
        if len(subs) == 1:
            carry = lax.cond(index[subs[0]] > 0, both, lambda: blocks([(subs[0], index[subs[0]], True)], carry))
        else:
            carry = both()
        for sub in subs:
            dqt_ref[:, mine[sub]] = carry[sub][2] * 0.125

    return _call(
        body, name, (SB_PAIRS, nb // qb),
        [pl.BlockSpec((qb * tq, 128), lambda p, i: (i, p)),
         pl.BlockSpec((t, 128), lambda p, i: (0, SB_PAIRS + p)),
         pl.BlockSpec((t, 128), lambda p, i: (0, 2 * SB_PAIRS + p)),
         pl.BlockSpec((None, nb, 128, tq), lambda p, i: (p, 0, 0, 0)),
         pl.BlockSpec((qb * tq, 128), lambda p, i: (i, p)),
         pl.BlockSpec((128, qb * tq), lambda p, i: (p, i)),
         pl.BlockSpec((2, nb, qb * tq), lambda p, i: (p, 0, i)),
         pl.BlockSpec(memory_space=pltpu.SMEM),
         pl.BlockSpec((tq, tq), lambda p, i: (0, 0)),
         pl.BlockSpec((tq, tq), lambda p, i: (0, 0))],
        [pl.BlockSpec((128, qb * tq), lambda p, i: (p, i)),
         pl.BlockSpec((t, 128), lambda p, i: (0, p)),
         pl.BlockSpec((t, 128), lambda p, i: (0, p))],
        [jax.ShapeDtypeStruct((SB_WIDTH, t), F32), jax.ShapeDtypeStruct((t, SB_WIDTH), F32),
         jax.ShapeDtypeStruct((t, SB_WIDTH), F32)],
        [], ("arbitrary", "arbitrary"), (qkv, qkv, qkv, ktb, dyc, dyct, cc, nvis, upper, lower), ride)


def _sg_bwd(ab, dyb, gn, wm, wmt, bfull, tm, name):
    t = ab.shape[0]
    nt = t // tm
    sel = (jnp.arange(SG_WIDTH)[:, None] // SG_HD == jnp.arange(CHUNK)[None, :]).astype(F32)

    def body(u_ref, v_ref, dy_ref, gn_ref, wm_ref, wmt_ref, b_ref, sel_ref,
             dup_ref, dvp_ref, dgn_ref, dw_ref, db_ref, dbacc_ref):
        i = pl.program_id(0)

        @pl.when(i == 0)
        def _():
            dgn_ref[...] = jnp.zeros_like(dgn_ref)
            dw_ref[...] = jnp.zeros_like(dw_ref)
            dbacc_ref[...] = jnp.zeros_like(dbacc_ref)

        tril = (lax.broadcasted_iota(jnp.int32, (CHUNK, CHUNK), 0) >= lax.broadcasted_iota(jnp.int32, (CHUNK, CHUNK), 1))
        gn_ = gn_ref[...]
        for c in range(tm // CHUNK):
            rows = slice(c * CHUNK, (c + 1) * CHUNK)
            up, vp, dy = u_ref[rows, :], v_ref[rows, :], dy_ref[rows, :]
            u, v = _gelu(up), _gelu(vp)
            r = _rstd(v)
            vn = (v * r * gn_).astype(BF16)
            sv = b_ref[...]
            for h in range(SG_HEADS):
                sv = sv + jnp.where(_head_lanes(h), _dot(wm_ref[h], vn), 0.0)
            dup_ref[rows, :] = dy * sv * _gelu_grad(up)
            dsv = dy * u
            dbacc_ref[...] += dsv
            dvn = jnp.zeros((CHUNK, SG_WIDTH), F32)
            for h in range(SG_HEADS):
                dsv_h = jnp.where(_head_lanes(h), dsv, 0.0).astype(BF16)
                dvn = dvn + _dot(wmt_ref[h], dsv_h)
                dw_ref[h] += jnp.where(tril, _dot_nt(dsv_h, vn), 0.0)
            dv, dgn = _rms_bwd(v, r, gn_, dvn)
            dgn_ref[...] += dgn
            dvp_ref[rows, :] = dv * _gelu_grad(vp)

        @pl.when(i == nt - 1)
        def _():
            db_ref[...] = jnp.dot(dbacc_ref[...], sel_ref[...], preferred_element_type=F32,
                                  precision=lax.Precision.HIGHEST)

    const = lambda shape: pl.BlockSpec(shape, lambda i: tuple(0 for _ in shape))
    return pl.pallas_call(
        body, name=name, grid=(nt,),
        in_specs=[pl.BlockSpec((tm, SG_WIDTH), lambda i: (i, 1)), pl.BlockSpec((tm, SG_WIDTH), lambda i: (i, 2)),
                  pl.BlockSpec((tm, SG_WIDTH), lambda i: (i, 0)), const((1, SG_WIDTH)),
                  const((SG_HEADS, CHUNK, CHUNK)), const((SG_HEADS, CHUNK, CHUNK)), const((CHUNK, SG_WIDTH)),
                  const((SG_WIDTH, CHUNK))],
        out_specs=[pl.BlockSpec((tm, SG_WIDTH), lambda i: (i, 0)), pl.BlockSpec((tm, SG_WIDTH), lambda i: (i, 0)),
                   const((1, SG_WIDTH)), const((SG_HEADS, CHUNK, CHUNK)), const((CHUNK, CHUNK))],
        out_shape=[jax.ShapeDtypeStruct((t, SG_WIDTH), F32), jax.ShapeDtypeStruct((t, SG_WIDTH), F32),
                   jax.ShapeDtypeStruct((1, SG_WIDTH), F32), jax.ShapeDtypeStruct((SG_HEADS, CHUNK, CHUNK), F32),
                   jax.ShapeDtypeStruct((CHUNK, CHUNK), F32)],
        scratch_shapes=[pltpu.VMEM((CHUNK, SG_WIDTH), F32)],
        compiler_params=_params(("arbitrary",)),
    )(ab, ab, dyb, gn, wm, wmt, bfull, sel)


def _pool_bwd(ab, dya, wbd, scale, tm, name):
    t = ab.shape[0]
    nt = t // tm
    hb = tm // POOL_HALO
    nh = t // POOL_HALO

    def body(cur_ref, prev_ref, dy_ref, dyn_ref, w_ref, s_ref, da_ref, dw_ref, ds_ref):
        i = pl.program_id(0)

        @pl.when(i == 0)
        def _():
            dw_ref[...] = jnp.zeros_like(dw_ref)
            ds_ref[...] = jnp.zeros_like(ds_ref)

        prev = jnp.where(i == 0, 0.0, prev_ref[...])
        d = _pool_diff(cur_ref[...], prev, i * tm).astype(BF16)
        dy = dy_ref[...]
        ds_ref[...] += jnp.sum(dy * _dot(d, w_ref[...]), axis=0, keepdims=True)
        dyn = jnp.where(i == nt - 1, 0.0, dyn_ref[...])
        dys = (jnp.concatenate([dy, dyn], axis=0) * s_ref[...]).astype(BF16)
        dw_ref[...] += _dot_tn(d, dys[:tm])
        dd = _dot_nt(dys, w_ref[...])
        fwd = _pool_window_sums(dd / _pool_count(i * tm, tm + POOL_HALO), True)
        da_ref[...] = fwd[:tm] - dd[:tm]

    return pl.pallas_call(
        body, name=name, grid=(nt,),
        in_specs=[pl.BlockSpec((tm, POOL_WIDTH), lambda i: (i, 0)),
                  pl.BlockSpec((POOL_HALO, POOL_WIDTH), lambda i: (jnp.maximum(i * hb - 1, 0), 0)),
                  pl.BlockSpec((tm, POOL_WIDTH), lambda i: (i, 0)),
                  pl.BlockSpec((POOL_HALO, POOL_WIDTH), lambda i: (jnp.minimum((i + 1) * hb, nh - 1), 0)),
                  pl.BlockSpec((POOL_WIDTH, POOL_WIDTH), lambda i: (0, 0)),
                  pl.BlockSpec((1, POOL_WIDTH), lambda i: (0, 0))],
        out_specs=[pl.BlockSpec((tm, POOL_WIDTH), lambda i: (i, 0)),
                   pl.BlockSpec((POOL_WIDTH, POOL_WIDTH), lambda i: (0, 0)),
                   pl.BlockSpec((1, POOL_WIDTH), lambda i: (0, 0))],
        out_shape=[jax.ShapeDtypeStruct((t, POOL_WIDTH), F32), jax.ShapeDtypeStruct((POOL_WIDTH, POOL_WIDTH), F32),
                   jax.ShapeDtypeStruct((1, POOL_WIDTH), F32)],
        compiler_params=_params(("arbitrary",)),
    )(ab, ab, dya, dya, wbd, scale)


def _inproj_bwd(dx1, x, g, da, dup, dvp, dqt, dk, dv, w, tm, name, ride=None):
    t, d = x.shape
    n = w.shape[1]
    nt = t // tm

    def body(dx1_ref, x_ref, g_ref, da_ref, du_ref, dv_ref, dqt_ref, dk_ref, dvv_ref, w_ref,
             dx_ref, h_ref, dp_ref, dg_ref):
        @pl.when(pl.program_id(0) == 0)
        def _():
            dg_ref[...] = jnp.zeros_like(dg_ref)

        dp = jnp.concatenate([da_ref[...], du_ref[...], dv_ref[...], dqt_ref[...].T, dk_ref[...], dvv_ref[...]],
                             axis=1).astype(BF16)
        dp_ref[...] = dp
        xx = x_ref[...]
        r = _rstd(xx)
        h_ref[...] = (xx * r * g_ref[...]).astype(BF16)
        dx, dg = _rms_bwd(xx, r, g_ref[...], _dot_nt(dp, w_ref[...]))
        dx_ref[...] = dx1_ref[...] + dx
        dg_ref[...] += dg

    row = lambda w_: pl.BlockSpec((tm, w_), lambda i: (i, 0))
    return _call(
        body, name, (nt,),
        [row(d), row(d), pl.BlockSpec((1, d), lambda i: (0, 0)), row(POOL_WIDTH), row(SG_WIDTH),
         row(SG_WIDTH), pl.BlockSpec((SB_WIDTH, tm), lambda i: (0, i)), row(SB_WIDTH), row(SB_WIDTH),
         pl.BlockSpec((d, n), lambda i: (0, 0))],
        [row(d), row(d), row(n), pl.BlockSpec((1, d), lambda i: (0, 0))],
        [jax.ShapeDtypeStruct((t, d), F32), jax.ShapeDtypeStruct((t, d), BF16),
         jax.ShapeDtypeStruct((t, n), BF16), jax.ShapeDtypeStruct((1, d), F32)],
        [], ("arbitrary",), (dx1, x, g, da, dup, dvp, dqt, dk, dv, w), ride)


def _full_w_in(gathered):
    return gathered.transpose(1, 0, 2).reshape(D_MODEL, IN_COLS)


_SMALL_SHAPES = ((D_MODEL,), (4, POOL_GW, POOL_GW), (POOL_WIDTH,), (SG_WIDTH,), (SG_HEADS, CHUNK, CHUNK),
                 (SG_HEADS, CHUNK), (D_MODEL,))
_SMALL_SIZES = tuple(functools.reduce(lambda p, q: p * q, shp) for shp in _SMALL_SHAPES)
_SMALL_ROWS = sum(_SMALL_SIZES) // 128
_NORM1_ROWS = D_MODEL // 128


def _pack_small_layer(arrs):
    return jnp.concatenate([a.reshape(-1) for a in arrs]).reshape(_SMALL_ROWS, 128)


def _pack_small(arrs, final):
    return jnp.concatenate([_pack_small_layer([a[l] for a in arrs]) for l in range(DEPTH)] + [final.reshape(-1, 128)])


def _unpack_small(buf):
    per_layer = []
    for l in range(DEPTH):
        flat, off, outs = buf[l * _SMALL_ROWS:(l + 1) * _SMALL_ROWS].reshape(-1), 0, []
        for shp, size in zip(_SMALL_SHAPES, _SMALL_SIZES):
            outs.append(flat[off:off + size].reshape(shp))
            off += size
        per_layer.append(outs)
    return [jnp.stack([per_layer[l][k] for l in range(DEPTH)]) for k in range(len(_SMALL_SHAPES))] + \
           [buf[DEPTH * _SMALL_ROWS:].reshape(-1)]


def _tiles(t):
    return min(512, t), min(256, t // 4), min(4096, t)


def _layer_fwd(xl, wi, wo, wu, wd, small_w, l, ride_mlp=None):
    n1, pw, ps, sn, sw, sb, n2 = small_w
    tm, tq, _ = _tiles(xl.shape[0])
    wbd = jnp.zeros((4, POOL_GW, 4, POOL_GW), F32)
    for gi in range(4):
        wbd = wbd.at[gi, :, gi, :].set(pw[gi])
    wbd = wbd.reshape(POOL_WIDTH, POOL_WIDTH).astype(BF16)
    wm = (sw * jnp.tril(jnp.ones((CHUNK, CHUNK), F32))).astype(BF16)
    wmt = wm.transpose(0, 2, 1)
    bfull = jnp.repeat(sb.T, SG_HD, axis=1)
    g1, g2, psc, gn = n1[None, :], n2[None, :], ps[None, :], sn[None, :]

    ab, qkv, ktb, vtb = _inproj_fwd(xl, g1, wi, min(MLP_ROWS, xl.shape[0]), tq, f"inproj_fwd{l}")
    yab = _pool_sg_fwd(ab, wbd, psc, gn, wm, bfull, tm, f"pool_sg_fwd{l}")
    shards = [w for w in (wo, wu, wd) if w.ndim == 2]
    if shards:
        (yct, cc, nvis), got = _sba_fwd(qkv, vtb, tq, f"sba_fwd{l}", SiblingGather(shards))
        got = iter(got)
        wo, wu, wd = (next(got) if w.ndim == 2 else w for w in (wo, wu, wd))
    else:
        yct, cc, nvis = _sba_fwd(qkv, vtb, tq, f"sba_fwd{l}")
    wo = wo.reshape(D_MODEL, D_MODEL)
    res = _outproj_mlp_fwd(xl, yab, yct, wo, g2, wu, wd, min(MLP_ROWS, xl.shape[0]), f"mlp_fwd{l}", ride_mlp)
    (x1, u, x2), rode = res if ride_mlp is not None else (res, None)
    saved = dict(x=xl, ab=ab, qkv=qkv, ktb=ktb, cc=cc, nvis=nvis, yab=yab, yct=yct, x1=x1, u=u, wi=wi, wo=wo, wu=wu,
                 wd=wd, wbd=wbd, wm=wm, wmt=wmt, bfull=bfull, g1=g1, g2=g2, psc=psc, gn=gn)
    return x2, saved, rode


def _layer_bwd(dx, s, l, ride_mlp=None, exchange=True, gather_small=False):
    tm, tq, tw = _tiles(dx.shape[0])
    ktb = s["ktb"]
    res = _mlp_bwd(dx, s["x1"], s["g2"], s["u"], s["wu"], s["wd"], min(MLP_ROWS, dx.shape[0]), f"mlp_bwd{l}", ride_mlp)
    (dx1, du, r, h2, dx2b, dn2), rode = res if ride_mlp is not None else (res, None)
    dw_up = _matmul_tn(h2, du, D_MODEL, 2 * FF_SHARD, tw, f"dw_up{l}", by_column_block=FF_SHARD)
    dw_down = _matmul_tn(r, dx2b, 1024, D_MODEL, tw, f"dw_down{l}").reshape(N_DEV, FF_SHARD, D_MODEL)
    dya, dyb, dyc, dyct, dx1b = _outproj_bwd(dx1, s["wo"], tm, f"outproj_bwd{l}")
    dw_out = jnp.concatenate([
        _matmul_tn(s["yab"], dx1b, POOL_WIDTH + SG_WIDTH, D_MODEL, tw, f"dw_out_ab{l}"),
        _matmul_tn(s["yct"], dx1b, SB_WIDTH, D_MODEL, tw, f"dw_out_c{l}", a_transposed=True)]
    ).reshape(N_DEV, OUT_SHARD, D_MODEL)
    if exchange:
        (dqt, dk, dv), (dw_out, dw_up, dw_down) = _sba_bwd(s["qkv"], ktb, dyc, dyct, s["cc"], s["nvis"], tq, f"sba_bwd{l}",
                                                           Exchange([dw_out, dw_up, dw_down], True))
    else:
        dqt, dk, dv = _sba_bwd(s["qkv"], ktb, dyc, dyct, s["cc"], s["nvis"], tq, f"sba_bwd{l}")
    dup, dvp, dgn, dwm, dbm = _sg_bwd(s["ab"], dyb, s["gn"], s["wm"], s["wmt"], s["bfull"], tm, f"sg_bwd{l}")
    da, dwbd, dpsc = _pool_bwd(s["ab"], dya, s["wbd"], s["psc"], tm, f"pool_bwd{l}")
    dpw = jnp.stack([dwbd[gi * POOL_GW:(gi + 1) * POOL_GW, gi * POOL_GW:(gi + 1) * POOL_GW] for gi in range(4)])
    small = _pack_small_layer([jnp.zeros((D_MODEL,), F32), dpw, dpsc[0], dgn[0], dwm, dbm[:, :SG_HEADS].T, dn2[0]])[_NORM1_ROWS:]
    res = _inproj_bwd(dx1, s["x"], s["g1"], da, dup, dvp, dqt, dk, dv, s["wi"], tm, f"inproj_bwd{l}",
                      Exchange([small], False) if gather_small else None)
    (dx, h1, dproj, dn1), small = (res[0], res[1][0]) if gather_small else (res, small)
    dw_in = _matmul_tn(h1, dproj, D_MODEL, IN_COLS // 3, tw, f"dw_in{l}")
    dw_in = dw_in.reshape(D_MODEL, N_DEV, IN_SHARD).transpose(1, 0, 2)
    return dx, (dw_in, dw_out, dw_up, dw_down), (dn1.reshape(_NORM1_ROWS, 128), small), rode


def kernel(x, norm1, w_in, pool_w, pool_scale, sg_norm, sg_w, sg_b, w_out, norm2, w_up, w_down, final_norm, loss_target, m_norm1, m_w_in, m_pool_w, m_pool_scale, m_sg_norm, m_sg_w, m_sg_b, m_w_out, m_norm2, m_w_up, m_w_down, m_final_norm, v_norm1, v_w_in, v_pool_w, v_pool_scale, v_sg_norm, v_sg_w, v_sg_b, v_w_out, v_norm2, v_w_up, v_w_down, v_final_norm):
    t = x.shape[1]
    small_w =(norm1, pool_w, pool_scale, sg_norm, sg_w, sg_b, norm2)
    big_w = (w_in, w_out, w_up, w_down)
    big_m = (m_w_in, m_w_out, m_w_up, m_w_down)
    big_v = (v_w_in, v_w_out, v_w_up, v_w_down)
    shards = [[w[l].astype(BF16) for w in big_w] for l in range(DEPTH)]

    wi0 = _full_w_in(SiblingGather(shards[0][:1]).alone("gather_w_in0")[0])
    sh_in1, sh_out1, sh_up1, sh_down1 = shards[1]
    x1, s0, (wi1, wo1, wd1) = _layer_fwd(x.reshape(t, D_MODEL), wi0, *shards[0][1:], tuple(w[0] for w in small_w), 0,
                                         ride_mlp=Exchange([sh_in1, sh_out1, sh_down1], False))
    x2, s1, _ = _layer_fwd(x1, _full_w_in(wi1), wo1, sh_up1, wd1, tuple(w[1] for w in small_w), 1)
    loss_local, dx, dfinal = _loss_grad(x2, final_norm[None, :], loss_target.reshape(t, D_MODEL), min(MLP_ROWS, t), "loss_grad")
    loss = lax.psum(loss_local[0, 0], MESH_AXES)

    dx, parts1, small1, _ = _layer_bwd(dx, s1, 1)
    early = jnp.concatenate(list(small1) + [dfinal.reshape(-1, 128)])
    dx, parts0, (dn1, small0), (recv_in1, early) = _layer_bwd(
        dx, s0, 0, ride_mlp=Exchange([parts1[0], early], [True, False]), gather_small=True)
    grad_x = dx.reshape(x.shape)
    recv_in0, dn1 = _exchange([parts0[0], dn1], [True, False], "scatter_w_in0_gather_norm1_0")
    small_all = jnp.concatenate([dn1, small0, early], axis=1)
    received = [[recv_in0] + list(parts0[1:]), [recv_in1] + list(parts1[1:])]

    big = [None] * 4
    for l in reversed(range(DEPTH)):
        for k in range(4):
            big[k] = _reduce_adamw(received[l][k], big_w[k], big_m[k], big_v[k], l, big[k], f"adamw{k}_{l}")

    sm = _reduce_adamw(
        small_all,
        _pack_small(small_w, final_norm)[None],
        _pack_small([m_norm1, m_pool_w, m_pool_scale, m_sg_norm, m_sg_w, m_sg_b, m_norm2], m_final_norm)[None],
        _pack_small([v_norm1, v_pool_w, v_pool_scale, v_sg_norm, v_sg_w, v_sg_b, v_norm2], v_final_norm)[None],
        0, None, "adamw_replicated")

    out = [loss, grad_x]
    for k in range(4):
        n1, pw, ps, sn, sw, sb, n2, fn = _unpack_small(sm[k][0])
        out += [n1, big[0][k], pw, ps, sn, sw, sb, big[1][k], n2, big[2][k], big[3][k], fn]
    return tuple(out)
```

```python
import functools

import jax
import jax.numpy as jnp
from jax import lax
from jax.experimental import pallas as pl
from jax.experimental.pallas import tpu as pltpu

F32 = jnp.float32
BF16 = jnp.bfloat16

D_MODEL = 1024
DEPTH = 2
POOL_WIDTH = 256
SG_WIDTH = 256
SB_WIDTH = 512
POOL_WINDOWS = (2, 4, 8, 16)
POOL_GW = 64
POOL_HALO = 16
CHUNK = 128
SG_HEADS = 4
SG_HD = 64
SB_HD = 64
SB_PAIRS = SB_WIDTH // 128
AB_COLS = POOL_WIDTH + 2 * SG_WIDTH
IN_COLS = AB_COLS + 3 * SB_WIDTH
D_FF = 4096
EPS = 1e-6
N_DEV = 8
FF_SHARD = D_FF // N_DEV
IN_SHARD = IN_COLS // N_DEV
OUT_SHARD = D_MODEL // N_DEV
ADAM_LR = 0.001
ADAM_B1 = 0.9
ADAM_B2 = 0.999
ADAM_EPS = 1e-08
ADAM_WD = 0.01
ADAM_STEP = 10
VMEM_LIMIT = 56 * 1024 * 1024
MLP_ROWS = 1024
MESH_AXES = ("x", "y", "c")


def _dot(a, b):
    return jnp.dot(a, b, preferred_element_type=F32)


def _dot_nt(a, b):
    return lax.dot_general(a, b, (((1,), (1,)), ((), ())), preferred_element_type=F32)


def _dot_tn(a, b):
    return lax.dot_general(a, b, (((0,), (0,)), ((), ())), preferred_element_type=F32)


def _rstd(x):
    return lax.rsqrt(jnp.mean(x * x, axis=-1, keepdims=True) + EPS)


def _rms_bwd(x, r, g, dh):
    gq = dh * g
    dx = r * gq - x * (r * r * r) * jnp.mean(gq * x, axis=-1, keepdims=True)
    dg = jnp.sum(dh * x * r, axis=0, keepdims=True)
    return dx, dg


def _params(sem=None):
    kw = dict(vmem_limit_bytes=VMEM_LIMIT)
    if sem is not None:
        kw["dimension_semantics"] = sem
    return pltpu.CompilerParams(**kw)


def _row_tile(rows, cap):
    best = 8
    for t in range(8, min(rows, cap) + 1, 8):
        if rows % t == 0:
            best = t
    return best


def _peer(k):
    x, y, c = lax.axis_index("x"), lax.axis_index("y"), lax.axis_index("c")
    return (1 - x if k & 4 else x, 1 - y if k & 2 else y, 1 - c if k & 1 else c)


def _my_index():
    return 4 * lax.axis_index("x") + 2 * lax.axis_index("y") + lax.axis_index("c")


class Exchange:
    def __init__(self, arrs, scatter):
        self.arrs = list(arrs)
        self.scatter = list(scatter) if isinstance(scatter, (list, tuple)) else [scatter] * len(self.arrs)
        self.n = len(self.arrs)
        self.any_specs = [pl.BlockSpec(memory_space=pl.ANY)] * self.n
        self.out_shape = [jax.ShapeDtypeStruct((N_DEV,) + a.shape[-2:], a.dtype) for a in self.arrs]
        self.sems = [pltpu.SemaphoreType.DMA((self.n, N_DEV - 1)), pltpu.SemaphoreType.DMA((self.n, N_DEV - 1)),
                     pltpu.SemaphoreType.DMA((self.n,))]

    def _copies(self, ins, outs, sems):
        send_sems, recv_sems, local_sems = sems
        me = _my_index()
        local, remote = [], []
        for a in range(self.n):
            sc = self.scatter[a]
            local.append(pltpu.make_async_copy(ins[a].at[me] if sc else ins[a], outs[a].at[me], local_sems.at[a]))
            for k in range(1, N_DEV):
                px, py, pc = _peer(k)
                remote.append(pltpu.make_async_remote_copy(
                    src_ref=ins[a].at[4 * px + 2 * py + pc] if sc else ins[a], dst_ref=outs[a].at[me],
                    send_sem=send_sems.at[a, k - 1], recv_sem=recv_sems.at[a, k - 1],
                    device_id=(px, py, pc), device_id_type=pl.DeviceIdType.MESH))
        return local, remote

    def start(self, ins, outs, sems):
        local, remote = self._copies(ins, outs, sems)
        for cp in local + remote:
            cp.start()

    def middle(self, ins, outs, sems):
        pass

    def wait(self, ins, outs, sems):
        local, remote = self._copies(ins, outs, sems)
        for cp in remote:
            cp.wait_recv()
        for cp in remote:
            cp.wait_send()
        for cp in local:
            cp.wait()

    def alone(self, name):
        n = self.n

        def body(*refs):
            ins, outs, sems = refs[:n], refs[n:2 * n], refs[2 * n:]
            self.start(ins, outs, sems)
            self.middle(ins, outs, sems)
            self.wait(ins, outs, sems)

        return pl.pallas_call(body, name=name, out_shape=self.out_shape, in_specs=self.any_specs,
                              out_specs=self.any_specs, scratch_shapes=self.sems)(*self.arrs)


class SiblingGather(Exchange):
    def __init__(self, arrs):
        super().__init__(arrs, False)

    def _plan(self, ins, outs, sems):
        send_sems, recv_sems, local_sems = sems
        x, y, core = lax.axis_index("x"), lax.axis_index("y"), lax.axis_index("c")
        me, sibling = (x, y, core), (x, y, 1 - core)
        chips = [(1 - x, y), (x, 1 - y), (1 - x, 1 - y)]
        plan = dict(local=[], first=[], arrived=[], passed=[], late=[])
        for a in range(self.n):
            slot = lambda p, a=a: outs[a].at[4 * p[0] + 2 * p[1] + p[2]]

            def copy(k, block, to, src=None, a=a, slot=slot):
                return pltpu.make_async_remote_copy(
                    src_ref=slot(block) if src is None else src, dst_ref=slot(block), send_sem=send_sems.at[a, k],
                    recv_sem=recv_sems.at[a, k], device_id=to, device_id_type=pl.DeviceIdType.MESH)

            plan["local"].append(pltpu.make_async_copy(ins[a], slot(me), local_sems.at[a]))
            plan["first"] += [copy(0, me, sibling, src=ins[a])] + [copy(1 + j, me, (*ch, core), src=ins[a])
                                                                  for j, ch in enumerate(chips)]
            plan["arrived"] += [copy(1 + j, (*ch, core), me) for j, ch in enumerate(chips)]
            plan["passed"] += [copy(4 + j, (*ch, core), sibling) for j, ch in enumerate(chips)]
            plan["late"] += [copy(0, sibling, me)] + [copy(4 + j, (*ch, 1 - core), me) for j, ch in enumerate(chips)]
        return plan

    def start(self, ins, outs, sems):
        plan = self._plan(ins, outs, sems)
        for cp in plan["local"] + plan["first"]:
            cp.start()

    def middle(self, ins, outs, sems):
        plan = self._plan(ins, outs, sems)
        for arrived, passed in zip(plan["arrived"], plan["passed"]):
            arrived.wait_recv()
            passed.start()

    def wait(self, ins, outs, sems):
        plan = self._plan(ins, outs, sems)
        for cp in plan["late"]:
            cp.wait_recv()
        for cp in plan["first"] + plan["passed"]:
            cp.wait_send()
        for cp in plan["local"]:
            cp.wait()


def _exchange(arrs, scatter, name):
    return Exchange(arrs, scatter).alone(name)


def _call(body, name, grid, in_specs, out_specs, out_shape, scratch_shapes, semantics, args, ride=None):
    if ride is None:
        return pl.pallas_call(body, name=name, grid=grid, in_specs=in_specs, out_specs=out_specs, out_shape=out_shape,
                              scratch_shapes=scratch_shapes, compiler_params=_params(semantics))(*args)
    single = not isinstance(out_shape, (list, tuple))
    out_specs, out_shape = ([out_specs], [out_shape]) if single else (list(out_specs), list(out_shape))
    n_in, n_out, n_scr, n = len(in_specs), len(out_specs), len(scratch_shapes), ride.n

    def riding(*refs):
        ins, cins = refs[:n_in], refs[n_in:n_in + n]
        outs, couts = refs[n_in + n:n_in + n + n_out], refs[n_in + n + n_out:n_in + 2 * n + n_out]
        scr, sems = refs[n_in + 2 * n + n_out:n_in + 2 * n + n_out + n_scr], refs[n_in + 2 * n + n_out + n_scr:]
        step = functools.reduce(lambda s, a: s * grid[a] + pl.program_id(a), range(len(grid)), 0)
        steps = functools.reduce(lambda p, q: p * q, grid)

        @pl.when(step == 0)
        def _():
            ride.start(cins, couts, sems)

        @pl.when(step == (3 * steps) // 4)
        def _():
            ride.middle(cins, couts, sems)

        body(*ins, *outs, *scr)

        @pl.when(step == steps - 1)
        def _():
            ride.wait(cins, couts, sems)

    res = pl.pallas_call(
        riding, name=name, grid=grid, in_specs=list(in_specs) + ride.any_specs, out_specs=out_specs + ride.any_specs,
        out_shape=out_shape + ride.out_shape, scratch_shapes=list(scratch_shapes) + ride.sems,
        compiler_params=_params(("arbitrary",) * len(grid)))(*args, *ride.arrs)
    own = res[0] if single else list(res[:n_out])
    return own, list(res[n_out:])


def _reduce_adamw(parts, w, m, v, l, prev, name):
    _, rows, n = parts.shape
    tr = _row_tile(rows, max(8, (1 << 18) // n))
    c1 = 1.0 - ADAM_B1 ** ADAM_STEP
    c2 = 1.0 - ADAM_B2 ** ADAM_STEP

    def body(p_ref, w_ref, m_ref, v_ref, *rest):
        g_ref, d_ref, nm_ref, nv_ref = rest[-4:]
        g = p_ref[0].astype(F32)
        for s in range(1, N_DEV):
            g = g + p_ref[s].astype(F32)
        nm = ADAM_B1 * m_ref[...] + (1.0 - ADAM_B1) * g
        nv = ADAM_B2 * v_ref[...] + (1.0 - ADAM_B2) * (g * g)
        m_hat = nm / c1
        v_hat = nv / c2
        g_ref[...] = g
        d_ref[...] = -ADAM_LR * (m_hat / (jnp.sqrt(v_hat) + ADAM_EPS) + ADAM_WD * w_ref[...])
        nm_ref[...] = nm
        nv_ref[...] = nv

    blk = pl.BlockSpec((None, tr, n), lambda i: (l, i, 0))
    out = jax.ShapeDtypeStruct(w.shape, F32)
    prev = list(prev) if prev is not None else []
    return pl.pallas_call(
        body, name=name, grid=(rows // tr,),
        in_specs=[pl.BlockSpec((N_DEV, tr, n), lambda i: (0, i, 0)), blk, blk, blk] + [pl.BlockSpec(memory_space=pl.ANY)] * len(prev),
        out_specs=[blk, blk, blk, blk], out_shape=[out, out, out, out],
        input_output_aliases={4 + k: k for k in range(len(prev))},
        compiler_params=_params(("parallel",)),
    )(parts, w, m, v, *prev)


def _inproj_fwd(x, g, w, tm, tq, name):
    t, d = x.shape
    n = w.shape[1]
    nb = t // tq
    per = tm // tq

    def body(x_ref, g_ref, w_ref, ab_ref, qkv_ref, kt_ref, vt_ref):
        xx = x_ref[...]
        h = (xx * _rstd(xx) * g_ref[...]).astype(BF16)
        ab_ref[...] = _dot(h, w_ref[:, :AB_COLS])
        qkv = _dot(h, w_ref[:, AB_COLS:])
        qkv_ref[...] = qkv.astype(BF16)
        for which, out_ref in ((1, kt_ref), (2, vt_ref)):
            for p in range(SB_PAIRS):
                for b in range(per):
                    cols = which * SB_WIDTH + p * 128
                    out_ref[p, b] = qkv[b * tq:(b + 1) * tq, cols:cols + 128].T.astype(BF16)

    tb = pl.BlockSpec((SB_PAIRS, per, 128, tq), lambda i: (0, i, 0, 0))
    tshape = jax.ShapeDtypeStruct((SB_PAIRS, nb, 128, tq), BF16)
    return _call(
        body, name, (t // tm,),
        [pl.BlockSpec((tm, d), lambda i: (i, 0)), pl.BlockSpec((1, d), lambda i: (0, 0)),
         pl.BlockSpec((d, n), lambda i: (0, 0))],
        [pl.BlockSpec((tm, AB_COLS), lambda i: (i, 0)), pl.BlockSpec((tm, n - AB_COLS), lambda i: (i, 0)), tb, tb],
        [jax.ShapeDtypeStruct((t, AB_COLS), F32), jax.ShapeDtypeStruct((t, n - AB_COLS), BF16), tshape, tshape],
        [], ("parallel",), (x, g, w))


def _pool_window_sums(xx, forward):
    n = xx.shape[0]
    sh = (lambda k: n - k) if forward else (lambda k: k)
    s2 = xx + pltpu.roll(xx, sh(1), 0)
    s4 = s2 + pltpu.roll(s2, sh(2), 0)
    s8 = s4 + pltpu.roll(s4, sh(4), 0)
    s16 = s8 + pltpu.roll(s8, sh(8), 0)
    grp = lax.broadcasted_iota(jnp.int32, (1, POOL_WIDTH), 1) // POOL_GW
    return jnp.where(grp == 0, s2, jnp.where(grp == 1, s4, jnp.where(grp == 2, s8, s16)))


def _pool_count(t0, rows):
    grp = lax.broadcasted_iota(jnp.int32, (1, POOL_WIDTH), 1) // POOL_GW
    win = jnp.where(grp == 0, 2, jnp.where(grp == 1, 4, jnp.where(grp == 2, 8, 16)))
    tt = t0 + lax.broadcasted_iota(jnp.int32, (rows, 1), 0)
    return jnp.minimum(tt + 1, win).astype(F32)


def _pool_diff(cur, prev, t0):
    tm = cur.shape[0]
    sums = _pool_window_sums(jnp.concatenate([prev, cur], axis=0), False)[POOL_HALO:]
    return sums / _pool_count(t0, tm) - cur


def _pool_sg_fwd(ab, wbd, scale, gn, wm, bfull, tm, name):
    t = ab.shape[0]
    hb = tm // POOL_HALO

    def body(cur_ref, prev_ref, u_ref, v_ref, w_ref, s_ref, gn_ref, wm_ref, b_ref, y_ref):
        i = pl.program_id(0)
        prev = jnp.where(i == 0, 0.0, prev_ref[...])
        d = _pool_diff(cur_ref[...], prev, i * tm).astype(BF16)
        y_ref[:, :POOL_WIDTH] = (_dot(d, w_ref[...]) * s_ref[...]).astype(BF16)
        v = _gelu(v_ref[...])
        vn = (v * _rstd(v) * gn_ref[...]).astype(BF16)
        for c in range(tm // CHUNK):
            rows = slice(c * CHUNK, (c + 1) * CHUNK)
            vc = vn[rows]
            sv = b_ref[...]
            for h in range(SG_HEADS):
                sv = sv + jnp.where(_head_lanes(h), _dot(wm_ref[h], vc), 0.0)
            y_ref[rows, POOL_WIDTH:] = (_gelu(u_ref[rows, :]) * sv).astype(BF16)

    return pl.pallas_call(
        body, name=name, grid=(t // tm,),
        in_specs=[pl.BlockSpec((tm, POOL_WIDTH), lambda i: (i, 0)),
                  pl.BlockSpec((POOL_HALO, POOL_WIDTH), lambda i: (jnp.maximum(i * hb - 1, 0), 0)),
                  pl.BlockSpec((tm, SG_WIDTH), lambda i: (i, 1)), pl.BlockSpec((tm, SG_WIDTH), lambda i: (i, 2)),
                  pl.BlockSpec((POOL_WIDTH, POOL_WIDTH), lambda i: (0, 0)),
                  pl.BlockSpec((1, POOL_WIDTH), lambda i: (0, 0)),
                  pl.BlockSpec((1, SG_WIDTH), lambda i: (0, 0)),
                  pl.BlockSpec((SG_HEADS, CHUNK, CHUNK), lambda i: (0, 0, 0)),
                  pl.BlockSpec((CHUNK, SG_WIDTH), lambda i: (0, 0))],
        out_specs=pl.BlockSpec((tm, POOL_WIDTH + SG_WIDTH), lambda i: (i, 0)),
        out_shape=jax.ShapeDtypeStruct((t, POOL_WIDTH + SG_WIDTH), BF16),
        compiler_params=_params(("parallel",)),
    )(ab, ab, ab, ab, wbd, scale, gn, wm, bfull)


_GELU_K = 0.7978845608028654
_GELU_A = 0.044715


def _gelu(x):
    return 0.5 * x * (1.0 + jnp.tanh(_GELU_K * (x + _GELU_A * (x * x * x))))


def _gelu_grad(x):
    th = jnp.tanh(_GELU_K * (x + _GELU_A * (x * x * x)))
    return 0.5 * (1.0 + th) + 0.5 * x * (1.0 - th * th) * (_GELU_K * (1.0 + 3.0 * _GELU_A * (x * x)))


def _head_lanes(h):
    return lax.broadcasted_iota(jnp.int32, (1, SG_WIDTH), 1) // SG_HD == h


LOG2E = 1.4426950408889634
SB_SCALE = 0.125 * LOG2E
SB_DEAD_LOG2 = 152.0
SB_QUERY_BLOCKS_PER_STEP = 2


def _log2_sigmoids(y):
    neg_abs = lax.bitcast_convert_type(lax.bitcast_convert_type(y, jnp.uint32) | jnp.uint32(0x80000000), F32)
    lb = jnp.minimum(y, 0.0) - jnp.log(1.0 + jnp.exp2(neg_abs)) * LOG2E
    return lb, lb - y


def _split(x):
    hi = x.astype(BF16)
    return hi, (x - hi.astype(F32)).astype(BF16)


def _tri_dot(tri, x):
    hi, lo = _split(x)
    return _dot(tri, hi) + _dot(tri, lo)


def _sba_fwd(qkv, vtb, tq, name, ride=None):
    t = qkv.shape[0]
    nb = t // tq
    qb = SB_QUERY_BLOCKS_PER_STEP
    upper = (jnp.arange(tq)[None, :] > jnp.arange(tq)[:, None]).astype(BF16)

    def body(q_ref, k_ref, vt_ref, up_ref, ot_ref, c_ref, n_ref):
        pair, step = pl.program_id(0), pl.program_id(1)
        refs = (q_ref, k_ref, vt_ref, up_ref, ot_ref, c_ref, n_ref)
        subs = list(range(qb))

        @pl.when(step == 0)
        def _():
            for sub in subs:
                query_blocks([sub], pair, step, *refs)

        @pl.when(step > 0)
        def _():
            query_blocks(subs, pair, step, *refs)

    def query_blocks(subs, pair, step, q_ref, k_ref, vt_ref, up_ref, ot_ref, c_ref, n_ref):
        index = {sub: step * qb + sub for sub in subs}
        mine = {sub: slice(sub * tq, (sub + 1) * tq) for sub in subs}
        up = up_ref[...]
        lane_head = lax.broadcasted_iota(jnp.int32, (1, 128), 1) // SB_HD
        sub_head = lax.broadcasted_iota(jnp.int32, (128, 1), 0) // SB_HD
        causal = (lax.broadcasted_iota(jnp.int32, (tq, tq), 0) < lax.broadcasted_iota(jnp.int32, (tq, tq), 1))
        qh = {}
        for sub in subs:
            q = q_ref[mine[sub], :]
            qh[sub] = [jnp.where(lane_head == h, q, jnp.zeros_like(q)) for h in range(2)]

        def blocks(work, carry):
            cs = {sub: list(carry[sub][:2]) for sub in carry}
            acc = {sub: carry[sub][2] for sub in carry}
            kj = [k_ref[pl.ds(pl.multiple_of(j * tq, tq), tq), :] for _, j, _ in work]
            vt = [vt_ref[j] for _, j, _ in work]
            chains = [(w, h) for w in range(len(work)) for h in range(2)]
            z = [_dot_nt(kj[w], qh[work[w][0]][h]) for w, h in chains]
            ls = [_log2_sigmoids(zz * SB_SCALE) for zz in z]
            lb = [x[0] for x in ls]
            l1 = [jnp.where(causal, x[1], 0.0) if work[w][2] else x[1] for x, (w, h) in zip(ls, chains)]
            after = [_tri_dot(up, x) for x in l1]
            a = []
            for n, (w, h) in enumerate(chains):
                sub, j, diag = work[w]
                c_ref[h, pl.ds(j, 1), mine[sub]] = cs[sub][h]
                an = jnp.exp2(lb[n] + after[n] + cs[sub][h])
                a.append(jnp.where(causal, an, 0.0) if diag else an)
                cs[sub][h] = cs[sub][h] + after[n][0:1, :] + l1[n][0:1, :]
            alive = {sub: (jnp.max(jnp.maximum(*cs[sub])) > -SB_DEAD_LOG2).astype(jnp.int32) for sub in cs}
            for n, (w, h) in enumerate(chains):
                sub = work[w][0]
                acc[sub] = acc[sub] + _dot(jnp.where(sub_head == h, vt[w], jnp.zeros_like(vt[w])), a[n].astype(BF16))
            return {sub: (alive[sub], cs[sub][0], cs[sub][1], acc[sub]) for sub in cs}

        zero = jnp.zeros((1, tq), F32)
        start = {sub: (zero, zero, jnp.zeros((128, tq), F32)) for sub in subs}
        both = lambda: blocks([(sub, index[sub] - b, b == 0) for sub in subs for b in range(2)], start)
        if len(subs) == 1:
            joint = lax.cond(index[subs[0]] > 0, both, lambda: blocks([(subs[0], index[subs[0]], True)], start))
        else:
            joint = both()
        for sub in subs:
            i = index[sub]

            def left(state, sub=sub, i=i):
                s, _, c0, c1, acc = state
                return (s + 1,) + blocks([(sub, i - 1 - s, False)], {sub: (c0, c1, acc)})[sub]

            state = lax.while_loop(lambda st, i=i: (st[0] < i) & (st[1] > 0), left, (jnp.minimum(i, 1),) + tuple(joint[sub]))
            ot_ref[:, mine[sub]] = state[4].astype(BF16)
            n_ref[pair, i] = (state[0] + 1).astype(F32)

    return _call(
        body, name, (SB_PAIRS, nb // qb),
        [pl.BlockSpec((qb * tq, 128), lambda p, i: (i, p)),
         pl.BlockSpec((t, 128), lambda p, i: (0, SB_PAIRS + p)),
         pl.BlockSpec((None, nb, 128, tq), lambda p, i: (p, 0, 0, 0)),
         pl.BlockSpec((tq, tq), lambda p, i: (0, 0))],
        [pl.BlockSpec((128, qb * tq), lambda p, i: (p, i)),
         pl.BlockSpec((2, nb, qb * tq), lambda p, i: (p, 0, i)),
         pl.BlockSpec(memory_space=pltpu.SMEM)],
        [jax.ShapeDtypeStruct((SB_WIDTH, t), BF16),
         jax.ShapeDtypeStruct((2 * SB_PAIRS, nb, t), F32),
         jax.ShapeDtypeStruct((SB_PAIRS, nb), F32)],
        [], ("arbitrary", "arbitrary"), (qkv, qkv, vtb, upper), ride)


def _outproj_mlp_fwd(x, yab, yct, wo, g2, wup, wdn, tm, name, ride=None):
    t, d = x.shape
    nf = wup.shape[0]

    def body(x_ref, ya_ref, yb_ref, yct_ref, wo_ref, g_ref, wu_ref, wd_ref, x1_ref, u_ref, x2_ref, h_ref, acc_ref):
        j = pl.program_id(1)

        @pl.when(j == 0)
        def _():
            x1 = (x_ref[...] + _dot(ya_ref[...], wo_ref[0:POOL_WIDTH, :])
                  + _dot(yb_ref[...], wo_ref[POOL_WIDTH:POOL_WIDTH + SG_WIDTH, :])
                  + _dot_tn(yct_ref[...], wo_ref[POOL_WIDTH + SG_WIDTH:, :]))
            x1_ref[...] = x1
            h_ref[...] = (x1 * _rstd(x1) * g_ref[...]).astype(BF16)
            acc_ref[...] = x1

        u = _dot(h_ref[...], wu_ref[...])
        u_ref[...] = u
        r = jnp.maximum(u, 0.0)
        acc_ref[...] += _dot((r * r).astype(BF16), wd_ref[...])

        @pl.when(j == nf - 1)
        def _():
            x2_ref[...] = acc_ref[...]

    row = lambda w: pl.BlockSpec((tm, w), lambda i, j: (i, 0))
    return _call(
        body, name, (t // tm, nf),
        [row(d), row(POOL_WIDTH), pl.BlockSpec((tm, SG_WIDTH), lambda i, j: (i, 1)), pl.BlockSpec((SB_WIDTH, tm), lambda i, j: (0, i)),
         pl.BlockSpec((d, d), lambda i, j: (0, 0)), pl.BlockSpec((1, d), lambda i, j: (0, 0)),
         pl.BlockSpec((None, d, FF_SHARD), lambda i, j: (j, 0, 0)),
         pl.BlockSpec((None, FF_SHARD, d), lambda i, j: (j, 0, 0))],
        [row(d), pl.BlockSpec((tm, FF_SHARD), lambda i, j: (i, j)), row(d)],
        [jax.ShapeDtypeStruct((t, d), F32), jax.ShapeDtypeStruct((t, nf * FF_SHARD), F32),
         jax.ShapeDtypeStruct((t, d), F32)],
        [pltpu.VMEM((tm, d), BF16), pltpu.VMEM((tm, d), F32)],
        ("parallel", "arbitrary"), (x, yab, yab, yct, wo, g2, wup, wdn), ride)


def _loss_grad(x, g, target, tm, name):
    t, d = x.shape
    nt = t // tm

    def body(x_ref, g_ref, t_ref, loss_ref, dx_ref, dg_ref, sq_ref):
        i = pl.program_id(0)

        @pl.when(i == 0)
        def _():
            sq_ref[...] = jnp.zeros_like(sq_ref)
            dg_ref[...] = jnp.zeros_like(dg_ref)

        xx = x_ref[...]
        r = _rstd(xx)
        err = xx * r * g_ref[...] - t_ref[...]
        sq_ref[...] += jnp.sum(err * err, axis=0, keepdims=True)
        dx, dg = _rms_bwd(xx, r, g_ref[...], err * (1.0 / d))
        dx_ref[...] = dx
        dg_ref[...] += dg

        @pl.when(i == nt - 1)
        def _():
            loss_ref[...] = jnp.sum(sq_ref[...], axis=1, keepdims=True) * (0.5 / d)

    return pl.pallas_call(
        body, name=name, grid=(nt,),
        in_specs=[pl.BlockSpec((tm, d), lambda i: (i, 0)), pl.BlockSpec((1, d), lambda i: (0, 0)),
                  pl.BlockSpec((tm, d), lambda i: (i, 0))],
        out_specs=[pl.BlockSpec((1, 1), lambda i: (0, 0)), pl.BlockSpec((tm, d), lambda i: (i, 0)),
                   pl.BlockSpec((1, d), lambda i: (0, 0))],
        out_shape=[jax.ShapeDtypeStruct((1, 1), F32), jax.ShapeDtypeStruct((t, d), F32),
                   jax.ShapeDtypeStruct((1, d), F32)],
        scratch_shapes=[pltpu.VMEM((1, d), F32)],
        compiler_params=_params(("arbitrary",)),
    )(x, g, target)


def _mlp_bwd(dx2, x1, g2, u, wup, wdn, tm, name, ride=None):
    t, d = dx2.shape
    nf = wup.shape[0]
    nt = t // tm

    def body(dx2_ref, x1_ref, g_ref, u_ref, wu_ref, wd_ref, dx1_ref, du_ref, r_ref, h_ref, dxb_ref, dg_ref, acc_ref):
        i, j = pl.program_id(0), pl.program_id(1)

        @pl.when(j == 0)
        def _():
            dxb_ref[...] = dx2_ref[...].astype(BF16)
            acc_ref[...] = jnp.zeros_like(acc_ref)

        @pl.when((i == 0) & (j == 0))
        def _():
            dg_ref[...] = jnp.zeros_like(dg_ref)

        share = pl.ds(pl.multiple_of(j * (tm // nf), tm // nf), tm // nf)
        xs = x1_ref[share, :]
        h_ref[share, :] = (xs * _rstd(xs) * g_ref[...]).astype(BF16)
        dr = _dot_nt(dxb_ref[...], wd_ref[...])
        ru = jnp.maximum(u_ref[...], 0.0)
        du = (dr * (2.0 * ru)).astype(BF16)
        du_ref[...] = du
        r_ref[...] = (ru * ru).astype(BF16)
        acc_ref[...] += _dot_nt(du, wu_ref[...])

        @pl.when(j == nf - 1)
        def _():
            x1 = x1_ref[...]
            dx, dg = _rms_bwd(x1, _rstd(x1), g_ref[...], acc_ref[...])
            dx1_ref[...] = dx2_ref[...] + dx
            dg_ref[...] += dg

    row = lambda w: pl.BlockSpec((tm, w), lambda i, j: (i, 0))
    col = pl.BlockSpec((tm, FF_SHARD), lambda i, j: (i, j))
    return _call(
        body, name, (nt, nf),
        [row(d), row(d), pl.BlockSpec((1, d), lambda i, j: (0, 0)), col,
         pl.BlockSpec((None, d, FF_SHARD), lambda i, j: (j, 0, 0)),
         pl.BlockSpec((None, FF_SHARD, d), lambda i, j: (j, 0, 0))],
        [row(d), col, col, row(d), row(d), pl.BlockSpec((1, d), lambda i, j: (0, 0))],
        [jax.ShapeDtypeStruct((t, d), F32), jax.ShapeDtypeStruct((t, nf * FF_SHARD), BF16),
         jax.ShapeDtypeStruct((t, nf * FF_SHARD), BF16), jax.ShapeDtypeStruct((t, d), BF16),
         jax.ShapeDtypeStruct((t, d), BF16), jax.ShapeDtypeStruct((1, d), F32)],
        [pltpu.VMEM((tm, d), F32)], ("arbitrary", "arbitrary"), (dx2, x1, g2, u, wup, wdn), ride)


def _matmul_tn(a, b, bm, bn, bt, name, by_column_block=0, a_transposed=False):
    m, t = a.shape if a_transposed else a.shape[::-1]
    n = b.shape[1]
    nk = t // bt

    def body(a_ref, b_ref, o_ref, acc_ref):
        k = pl.program_id(2)

        @pl.when(k == 0)
        def _():
            acc_ref[...] = jnp.zeros_like(acc_ref)

        acc_ref[...] += _dot(a_ref[...], b_ref[...]) if a_transposed else _dot_tn(a_ref[...], b_ref[...])

        @pl.when(k == nk - 1)
        def _():
            if by_column_block:
                for c in range(bn // by_column_block):
                    o_ref[c] = acc_ref[:, c * by_column_block:(c + 1) * by_column_block].astype(BF16)
            else:
                o_ref[...] = acc_ref[...].astype(BF16)

    if by_column_block:
        out_spec = pl.BlockSpec((bn // by_column_block, bm, by_column_block), lambda i, j, k: (j, i, 0))
        out_shape = jax.ShapeDtypeStruct((n // by_column_block, m, by_column_block), BF16)
    else:
        out_spec = pl.BlockSpec((bm, bn), lambda i, j, k: (i, j))
        out_shape = jax.ShapeDtypeStruct((m, n), BF16)
    a_spec = (pl.BlockSpec((bm, bt), lambda i, j, k: (i, k)) if a_transposed
              else pl.BlockSpec((bt, bm), lambda i, j, k: (k, i)))
    return _call(body, name, (m // bm, n // bn, nk), [a_spec, pl.BlockSpec((bt, bn), lambda i, j, k: (k, j))],
                 out_spec, out_shape, [pltpu.VMEM((bm, bn), F32)], ("parallel", "parallel", "arbitrary"), (a, b))


def _outproj_bwd(dx1, wo, tm, name):
    t, d = dx1.shape
    c2 = POOL_WIDTH + SG_WIDTH

    def body(dx_ref, wo_ref, dya_ref, dyb_ref, dyc_ref, dyct_ref, dxb_ref):
        dxb = dx_ref[...].astype(BF16)
        dxb_ref[...] = dxb
        dya_ref[...] = _dot_nt(dxb, wo_ref[0:POOL_WIDTH, :])
        dyb_ref[...] = _dot_nt(dxb, wo_ref[POOL_WIDTH:c2, :])
        dyc_ref[...] = _dot_nt(dxb, wo_ref[c2:, :]).astype(BF16)
        dyct_ref[...] = _dot_nt(wo_ref[c2:, :], dxb).astype(BF16)

    row = lambda w: pl.BlockSpec((tm, w), lambda i: (i, 0))
    return pl.pallas_call(
        body, name=name, grid=(t // tm,),
        in_specs=[row(d), pl.BlockSpec((d, d), lambda i: (0, 0))],
        out_specs=[row(POOL_WIDTH), row(SG_WIDTH), row(SB_WIDTH), pl.BlockSpec((SB_WIDTH, tm), lambda i: (0, i)), row(d)],
        out_shape=[jax.ShapeDtypeStruct((t, POOL_WIDTH), F32), jax.ShapeDtypeStruct((t, SG_WIDTH), F32),
                   jax.ShapeDtypeStruct((t, SB_WIDTH), BF16), jax.ShapeDtypeStruct((SB_WIDTH, t), BF16),
                   jax.ShapeDtypeStruct((t, d), BF16)],
        compiler_params=_params(("parallel",)),
    )(dx1, wo)


def _sba_bwd(qkv, ktb, dyc, dyct, cc, nvis, tq, name, ride=None):
    t = qkv.shape[0]
    nb = t // tq
    idx = jnp.arange(tq)
    upper = (idx[None, :] > idx[:, None]).astype(BF16)
    lower = (idx[None, :] < idx[:, None]).astype(BF16)

    qb = SB_QUERY_BLOCKS_PER_STEP

    def body(q_ref, k_ref, v_ref, kt_ref, do_ref, dot_ref, c_ref, n_ref, up_ref, lo_ref, dqt_ref, dk_ref, dv_ref):
        pair, step = pl.program_id(0), pl.program_id(1)
        refs = (q_ref, k_ref, v_ref, kt_ref, do_ref, dot_ref, c_ref, n_ref, up_ref, lo_ref, dqt_ref, dk_ref, dv_ref)
        subs = list(range(qb))

        @pl.when(step == 0)
        def _():
            dk_ref[...] = jnp.zeros_like(dk_ref)
            dv_ref[...] = jnp.zeros_like(dv_ref)
            for sub in subs:
                query_blocks([sub], pair, step, *refs)

        @pl.when(step > 0)
        def _():
            query_blocks(subs, pair, step, *refs)

    def query_blocks(subs, pair, step, q_ref, k_ref, v_ref, kt_ref, do_ref, dot_ref, c_ref, n_ref, up_ref, lo_ref,
                     dqt_ref, dk_ref, dv_ref):
        index = {sub: step * qb + sub for sub in subs}
        mine = {sub: slice(sub * tq, (sub + 1) * tq) for sub in subs}
        up = up_ref[...]
        lo = lo_ref[...]
        lane_head = lax.broadcasted_iota(jnp.int32, (1, 128), 1) // SB_HD
        sub_head = lax.broadcasted_iota(jnp.int32, (128, 1), 0) // SB_HD
        causal = (lax.broadcasted_iota(jnp.int32, (tq, tq), 0) < lax.broadcasted_iota(jnp.int32, (tq, tq), 1))
        hms = [lane_head == h for h in range(2)]
        qh, qs, doh, dot = {}, {}, {}, {}
        for sub in subs:
            q, do = q_ref[mine[sub], :], do_ref[mine[sub], :]
            qh[sub] = [jnp.where(hm, q, jnp.zeros_like(q)) for hm in hms]
            qs[sub] = [x * 0.125 for x in qh[sub]]
            doh[sub] = [jnp.where(hm, do, jnp.zeros_like(do)) for hm in hms]
            dot[sub] = dot_ref[:, mine[sub]]

        def blocks(work, carry):
            cgs = {sub: list(carry[sub][:2]) for sub in carry}
            dqt = {sub: carry[sub][2] for sub in carry}
            rows = [pl.ds(pl.multiple_of(j * tq, tq), tq) for _, j, _ in work]
            kj = [k_ref[r, :] for r in rows]
            vj = [v_ref[r, :] for r in rows]
            kt = [kt_ref[j] for _, j, _ in work]
            chains = [(w, h) for w in range(len(work)) for h in range(2)]
            z = [_dot_nt(kj[w], qh[work[w][0]][h]) for w, h in chains]
            da = [_dot(jnp.where(hms[h], vj[w], jnp.zeros_like(vj[w])), dot[work[w][0]]) for w, h in chains]
            ls = [_log2_sigmoids(zz * SB_SCALE) for zz in z]
            lb = [x[0] for x in ls]
            l1 = [jnp.where(causal, x[1], 0.0) if work[w][2] else x[1] for x, (w, h) in zip(ls, chains)]
            after = [_tri_dot(up, x) for x in l1]
            a = [jnp.exp2(lb[n] + after[n] + c_ref[h, pl.ds(work[w][1], 1), mine[work[w][0]]])
                 for n, (w, h) in enumerate(chains)]
            a = [jnp.where(causal, a[n], 0.0) if work[w][2] else a[n] for n, (w, h) in enumerate(chains)]
            g = [a[n] * da[n] for n in range(len(chains))]
            gloc = [_dot(lo, x.astype(BF16)) for x in g]
            dzb = []
            for n, (w, h) in enumerate(chains):
                sub, _, diag = work[w]
                gsum = gloc[n] + cgs[sub][h]
                dz = g[n] - jnp.exp2(lb[n]) * (g[n] + gsum)
                dzb.append((jnp.where(causal, dz, 0.0) if diag else dz).astype(BF16))
                cgs[sub][h] = gsum[tq - 1:tq, :] + g[n][tq - 1:tq, :]
            ab = [x.astype(BF16) for x in a]
            for n, (w, h) in enumerate(chains):
                sub = work[w][0]
                dqt[sub] = dqt[sub] + _dot(jnp.where(sub_head == h, kt[w], jnp.zeros_like(kt[w])), dzb[n])
            for w, (sub, _, _) in enumerate(work):
                dk_ref[rows[w], :] += _dot(dzb[2 * w], qs[sub][0]) + _dot(dzb[2 * w + 1], qs[sub][1])
                dv_ref[rows[w], :] += _dot(ab[2 * w], doh[sub][0]) + _dot(ab[2 * w + 1], doh[sub][1])
            return {sub: (cgs[sub][0], cgs[sub][1], dqt[sub]) for sub in cgs}

        zero = jnp.zeros((1, tq), F32)
        carry = {}
        for sub in subs:
            i = index[sub]
            n = jnp.clip(n_ref[pair, i].astype(jnp.int32), 1, i + 1)
            carry[sub] = lax.fori_loop(jnp.minimum(i + 1 - n, i - 1), i - 1,
                                       lambda s, cr, sub=sub: blocks([(sub, s, False)], {sub: cr})[sub],
                                       (zero, zero, jnp.zeros((128, tq), F32)))
        both = lambda: blocks([(sub, index[sub] - 1 + b, b == 1) for sub in subs for b in range(2)], carry)
        if len(subs) == 1:
            carry = lax.cond(index[subs[0]] > 0, both, lambda: blocks([(subs[0], index[subs[0]], True)], carry))
        else:
            carry = both()
        for sub in subs:
            dqt_ref[:, mine[sub]] = carry[sub][2] * 0.125

    return _call(
        body, name, (SB_PAIRS, nb // qb),
        [pl.BlockSpec((qb * tq, 128), lambda p, i: (i, p)),
         pl.BlockSpec((t, 128), lambda p, i: (0, SB_PAIRS + p)),
         pl.BlockSpec((t, 128), lambda p, i: (0, 2 * SB_PAIRS + p)),
         pl.BlockSpec((None, nb, 128, tq), lambda p, i: (p, 0, 0, 0)),
         pl.BlockSpec((qb * tq, 128), lambda p, i: (i, p)),
         pl.BlockSpec((128, qb * tq), lambda p, i: (p, i)),
         pl.BlockSpec((2, nb, qb * tq), lambda p, i: (p, 0, i)),
         pl.BlockSpec(memory_space=pltpu.SMEM),
         pl.BlockSpec((tq, tq), lambda p, i: (0, 0)),
         pl.BlockSpec((tq, tq), lambda p, i: (0, 0))],
        [pl.BlockSpec((128, qb * tq), lambda p, i: (p, i)),
         pl.BlockSpec((t, 128), lambda p, i: (0, p)),
         pl.BlockSpec((t, 128), lambda p, i: (0, p))],
        [jax.ShapeDtypeStruct((SB_WIDTH, t), F32), jax.ShapeDtypeStruct((t, SB_WIDTH), F32),
         jax.ShapeDtypeStruct((t, SB_WIDTH), F32)],
        [], ("arbitrary", "arbitrary"), (qkv, qkv, qkv, ktb, dyc, dyct, cc, nvis, upper, lower), ride)


def _pool_sg_bwd(ab, dya, dyb, wbd, scale, gn, wm, wmt, bfull, tm, name):
    t = ab.shape[0]
    nt = t // tm
    hb = tm // POOL_HALO
    nh = t // POOL_HALO
    sel = (jnp.arange(SG_WIDTH)[:, None] // SG_HD == jnp.arange(CHUNK)[None, :]).astype(F32)

    def body(cur_ref, prev_ref, dy_ref, dyn_ref, w_ref, s_ref, u_ref, v_ref, dyb_ref, gn_ref, wm_ref, wmt_ref, b_ref,
             sel_ref, da_ref, dw_ref, ds_ref, dup_ref, dvp_ref, dgn_ref, dwm_ref, db_ref, dbacc_ref):
        i = pl.program_id(0)

        @pl.when(i == 0)
        def _():
            dw_ref[...] = jnp.zeros_like(dw_ref)
            ds_ref[...] = jnp.zeros_like(ds_ref)
            dgn_ref[...] = jnp.zeros_like(dgn_ref)
            dwm_ref[...] = jnp.zeros_like(dwm_ref)
            dbacc_ref[...] = jnp.zeros_like(dbacc_ref)

        tril = (lax.broadcasted_iota(jnp.int32, (CHUNK, CHUNK), 0) >= lax.broadcasted_iota(jnp.int32, (CHUNK, CHUNK), 1))
        gn_ = gn_ref[...]
        for c in range(tm // CHUNK):
            rows = slice(c * CHUNK, (c + 1) * CHUNK)
            up, vp, dyc = u_ref[rows, :], v_ref[rows, :], dyb_ref[rows, :]
            u, v = _gelu(up), _gelu(vp)
            r = _rstd(v)
            vn = (v * r * gn_).astype(BF16)
            sv = b_ref[...]
            for h in range(SG_HEADS):
                sv = sv + jnp.where(_head_lanes(h), _dot(wm_ref[h], vn), 0.0)
            dup_ref[rows, :] = dyc * sv * _gelu_grad(up)
            dsv = dyc * u
            dbacc_ref[...] += dsv
            dvn = jnp.zeros((CHUNK, SG_WIDTH), F32)
            for h in range(SG_HEADS):
                dsv_h = jnp.where(_head_lanes(h), dsv, 0.0).astype(BF16)
                dvn = dvn + _dot(wmt_ref[h], dsv_h)
                dwm_ref[h] += jnp.where(tril, _dot_nt(dsv_h, vn), 0.0)
            dv, dgn = _rms_bwd(v, r, gn_, dvn)
            dgn_ref[...] += dgn
            dvp_ref[rows, :] = dv * _gelu_grad(vp)

        @pl.when(i == nt - 1)
        def _():
            db_ref[...] = jnp.dot(dbacc_ref[...], sel_ref[...], preferred_element_type=F32,
                                  precision=lax.Precision.HIGHEST)

        prev = jnp.where(i == 0, 0.0, prev_ref[...])
        d = _pool_diff(cur_ref[...], prev, i * tm).astype(BF16)
        dy = dy_ref[...]
        ds_ref[...] += jnp.sum(dy * _dot(d, w_ref[...]), axis=0, keepdims=True)
        dyn = jnp.where(i == nt - 1, 0.0, dyn_ref[...])
        dys = (jnp.concatenate([dy, dyn], axis=0) * s_ref[...]).astype(BF16)
        dw_ref[...] += _dot_tn(d, dys[:tm])
        dd = _dot_nt(dys, w_ref[...])
        fwd = _pool_window_sums(dd / _pool_count(i * tm, tm + POOL_HALO), True)
        da_ref[...] = fwd[:tm] - dd[:tm]

    const = lambda shape: pl.BlockSpec(shape, lambda i: tuple(0 for _ in shape))
    tile = pl.BlockSpec((tm, SG_WIDTH), lambda i: (i, 0))
    return pl.pallas_call(
        body, name=name, grid=(nt,),
        in_specs=[pl.BlockSpec((tm, POOL_WIDTH), lambda i: (i, 0)),
                  pl.BlockSpec((POOL_HALO, POOL_WIDTH), lambda i: (jnp.maximum(i * hb - 1, 0), 0)),
                  pl.BlockSpec((tm, POOL_WIDTH), lambda i: (i, 0)),
                  pl.BlockSpec((POOL_HALO, POOL_WIDTH), lambda i: (jnp.minimum((i + 1) * hb, nh - 1), 0)),
                  const((POOL_WIDTH, POOL_WIDTH)), const((1, POOL_WIDTH)),
                  pl.BlockSpec((tm, SG_WIDTH), lambda i: (i, 1)), pl.BlockSpec((tm, SG_WIDTH), lambda i: (i, 2)), tile,
                  const((1, SG_WIDTH)), const((SG_HEADS, CHUNK, CHUNK)), const((SG_HEADS, CHUNK, CHUNK)),
                  const((CHUNK, SG_WIDTH)), const((SG_WIDTH, CHUNK))],
        out_specs=[tile, const((POOL_WIDTH, POOL_WIDTH)), const((1, POOL_WIDTH)), tile, tile,
                   const((1, SG_WIDTH)), const((SG_HEADS, CHUNK, CHUNK)), const((CHUNK, CHUNK))],
        out_shape=[jax.ShapeDtypeStruct((t, POOL_WIDTH), F32), jax.ShapeDtypeStruct((POOL_WIDTH, POOL_WIDTH), F32),
                   jax.ShapeDtypeStruct((1, POOL_WIDTH), F32), jax.ShapeDtypeStruct((t, SG_WIDTH), F32),
                   jax.ShapeDtypeStruct((t, SG_WIDTH), F32), jax.ShapeDtypeStruct((1, SG_WIDTH), F32),
                   jax.ShapeDtypeStruct((SG_HEADS, CHUNK, CHUNK), F32), jax.ShapeDtypeStruct((CHUNK, CHUNK), F32)],
        scratch_shapes=[pltpu.VMEM((CHUNK, SG_WIDTH), F32)],
        compiler_params=_params(("arbitrary",)),
    )(ab, ab, dya, dya, wbd, scale, ab, ab, dyb, gn, wm, wmt, bfull, sel)


def _inproj_bwd(dx1, x, g, da, dup, dvp, dqt, dk, dv, w, tm, name, ride=None):
    t, d = x.shape
    n = w.shape[1]
    nt = t // tm

    def body(dx1_ref, x_ref, g_ref, da_ref, du_ref, dv_ref, dqt_ref, dk_ref, dvv_ref, w_ref,
             dx_ref, h_ref, dp_ref, dg_ref):
        @pl.when(pl.program_id(0) == 0)
        def _():
            dg_ref[...] = jnp.zeros_like(dg_ref)

        dp = jnp.concatenate([da_ref[...], du_ref[...], dv_ref[...], dqt_ref[...].T, dk_ref[...], dvv_ref[...]],
                             axis=1).astype(BF16)
        dp_ref[...] = dp
        xx = x_ref[...]
        r = _rstd(xx)
        h_ref[...] = (xx * r * g_ref[...]).astype(BF16)
        dx, dg = _rms_bwd(xx, r, g_ref[...], _dot_nt(dp, w_ref[...]))
        dx_ref[...] = dx1_ref[...] + dx
        dg_ref[...] += dg

    row = lambda w_: pl.BlockSpec((tm, w_), lambda i: (i, 0))
    return _call(
        body, name, (nt,),
        [row(d), row(d), pl.BlockSpec((1, d), lambda i: (0, 0)), row(POOL_WIDTH), row(SG_WIDTH),
         row(SG_WIDTH), pl.BlockSpec((SB_WIDTH, tm), lambda i: (0, i)), row(SB_WIDTH), row(SB_WIDTH),
         pl.BlockSpec((d, n), lambda i: (0, 0))],
        [row(d), row(d), row(n), pl.BlockSpec((1, d), lambda i: (0, 0))],
        [jax.ShapeDtypeStruct((t, d), F32), jax.ShapeDtypeStruct((t, d), BF16),
         jax.ShapeDtypeStruct((t, n), BF16), jax.ShapeDtypeStruct((1, d), F32)],
        [], ("arbitrary",), (dx1, x, g, da, dup, dvp, dqt, dk, dv, w), ride)


def _full_w_in(gathered):
    return gathered.transpose(1, 0, 2).reshape(D_MODEL, IN_COLS)


_SMALL_SHAPES = ((D_MODEL,), (4, POOL_GW, POOL_GW), (POOL_WIDTH,), (SG_WIDTH,), (SG_HEADS, CHUNK, CHUNK),
                 (SG_HEADS, CHUNK), (D_MODEL,))
_SMALL_SIZES = tuple(functools.reduce(lambda p, q: p * q, shp) for shp in _SMALL_SHAPES)
_SMALL_ROWS = sum(_SMALL_SIZES) // 128
_NORM1_ROWS = D_MODEL // 128


def _pack_small_layer(arrs):
    return jnp.concatenate([a.reshape(-1) for a in arrs]).reshape(_SMALL_ROWS, 128)


def _pack_small(arrs, final):
    return jnp.concatenate([_pack_small_layer([a[l] for a in arrs]) for l in range(DEPTH)] + [final.reshape(-1, 128)])


def _unpack_small(buf):
    per_layer = []
    for l in range(DEPTH):
        flat, off, outs = buf[l * _SMALL_ROWS:(l + 1) * _SMALL_ROWS].reshape(-1), 0, []
        for shp, size in zip(_SMALL_SHAPES, _SMALL_SIZES):
            outs.append(flat[off:off + size].reshape(shp))
            off += size
        per_layer.append(outs)
    return [jnp.stack([per_layer[l][k] for l in range(DEPTH)]) for k in range(len(_SMALL_SHAPES))] + \
           [buf[DEPTH * _SMALL_ROWS:].reshape(-1)]


def _tiles(t):
    return min(512, t), min(256, t // 4), min(4096, t)


def _layer_fwd(xl, wi, wo, wu, wd, small_w, l, ride_mlp=None):
    n1, pw, ps, sn, sw, sb, n2 = small_w
    tm, tq, _ = _tiles(xl.shape[0])
    wbd = jnp.zeros((4, POOL_GW, 4, POOL_GW), F32)
    for gi in range(4):
        wbd = wbd.at[gi, :, gi, :].set(pw[gi])
    wbd = wbd.reshape(POOL_WIDTH, POOL_WIDTH).astype(BF16)
    wm = (sw * jnp.tril(jnp.ones((CHUNK, CHUNK), F32))).astype(BF16)
    wmt = wm.transpose(0, 2, 1)
    bfull = jnp.repeat(sb.T, SG_HD, axis=1)
    g1, g2, psc, gn = n1[None, :], n2[None, :], ps[None, :], sn[None, :]

    ab, qkv, ktb, vtb = _inproj_fwd(xl, g1, wi, min(MLP_ROWS, xl.shape[0]), tq, f"inproj_fwd{l}")
    yab = _pool_sg_fwd(ab, wbd, psc, gn, wm, bfull, tm, f"pool_sg_fwd{l}")
    shards = [w for w in (wo, wu, wd) if w.ndim == 2]
    if shards:
        (yct, cc, nvis), got = _sba_fwd(qkv, vtb, tq, f"sba_fwd{l}", SiblingGather(shards))
        got = iter(got)
        wo, wu, wd = (next(got) if w.ndim == 2 else w for w in (wo, wu, wd))
    else:
        yct, cc, nvis = _sba_fwd(qkv, vtb, tq, f"sba_fwd{l}")
    wo = wo.reshape(D_MODEL, D_MODEL)
    res = _outproj_mlp_fwd(xl, yab, yct, wo, g2, wu, wd, min(MLP_ROWS, xl.shape[0]), f"mlp_fwd{l}", ride_mlp)
    (x1, u, x2), rode = res if ride_mlp is not None else (res, None)
    saved = dict(x=xl, ab=ab, qkv=qkv, ktb=ktb, cc=cc, nvis=nvis, yab=yab, yct=yct, x1=x1, u=u, wi=wi, wo=wo, wu=wu,
                 wd=wd, wbd=wbd, wm=wm, wmt=wmt, bfull=bfull, g1=g1, g2=g2, psc=psc, gn=gn)
    return x2, saved, rode


def _layer_bwd(dx, s, l, ride_mlp=None, exchange=True, gather_small=False):
    tm, tq, tw = _tiles(dx.shape[0])
    ktb = s["ktb"]
    res = _mlp_bwd(dx, s["x1"], s["g2"], s["u"], s["wu"], s["wd"], min(MLP_ROWS, dx.shape[0]), f"mlp_bwd{l}", ride_mlp)
    (dx1, du, r, h2, dx2b, dn2), rode = res if ride_mlp is not None else (res, None)
    dw_up = _matmul_tn(h2, du, D_MODEL, 2 * FF_SHARD, tw, f"dw_up{l}", by_column_block=FF_SHARD)
    dw_down = _matmul_tn(r, dx2b, 1024, D_MODEL, tw, f"dw_down{l}").reshape(N_DEV, FF_SHARD, D_MODEL)
    dya, dyb, dyc, dyct, dx1b = _outproj_bwd(dx1, s["wo"], tm, f"outproj_bwd{l}")
    dw_out = jnp.concatenate([
        _matmul_tn(s["yab"], dx1b, POOL_WIDTH + SG_WIDTH, D_MODEL, tw, f"dw_out_ab{l}"),
        _matmul_tn(s["yct"], dx1b, SB_WIDTH, D_MODEL, tw, f"dw_out_c{l}", a_transposed=True)]
    ).reshape(N_DEV, OUT_SHARD, D_MODEL)
    if exchange:
        (dqt, dk, dv), (dw_out, dw_up, dw_down) = _sba_bwd(s["qkv"], ktb, dyc, dyct, s["cc"], s["nvis"], tq, f"sba_bwd{l}",
                                                           Exchange([dw_out, dw_up, dw_down], True))
    else:
        dqt, dk, dv = _sba_bwd(s["qkv"], ktb, dyc, dyct, s["cc"], s["nvis"], tq, f"sba_bwd{l}")
    da, dwbd, dpsc, dup, dvp, dgn, dwm, dbm = _pool_sg_bwd(s["ab"], dya, dyb, s["wbd"], s["psc"], s["gn"], s["wm"], s["wmt"],
                                                           s["bfull"], tm, f"pool_sg_bwd{l}")
    dpw = jnp.stack([dwbd[gi * POOL_GW:(gi + 1) * POOL_GW, gi * POOL_GW:(gi + 1) * POOL_GW] for gi in range(4)])
    small = _pack_small_layer([jnp.zeros((D_MODEL,), F32), dpw, dpsc[0], dgn[0], dwm, dbm[:, :SG_HEADS].T, dn2[0]])[_NORM1_ROWS:]
    res = _inproj_bwd(dx1, s["x"], s["g1"], da, dup, dvp, dqt, dk, dv, s["wi"], tm, f"inproj_bwd{l}",
                      Exchange([small], False) if gather_small else None)
    (dx, h1, dproj, dn1), small = (res[0], res[1][0]) if gather_small else (res, small)
    dw_in = _matmul_tn(h1, dproj, D_MODEL, IN_COLS // 3, tw, f"dw_in{l}")
    dw_in = dw_in.reshape(D_MODEL, N_DEV, IN_SHARD).transpose(1, 0, 2)
    return dx, (dw_in, dw_out, dw_up, dw_down), (dn1.reshape(_NORM1_ROWS, 128), small), rode


def kernel(x, norm1, w_in, pool_w, pool_scale, sg_norm, sg_w, sg_b, w_out, norm2, w_up, w_down, final_norm, loss_target, m_norm1, m_w_in, m_pool_w, m_pool_scale, m_sg_norm, m_sg_w, m_sg_b, m_w_out, m_norm2, m_w_up, m_w_down, m_final_norm, v_norm1, v_w_in, v_pool_w, v_pool_scale, v_sg_norm, v_sg_w, v_sg_b, v_w_out, v_norm2, v_w_up, v_w_down, v_final_norm):
    t = x.shape[1]
    small_w =(norm1, pool_w, pool_scale, sg_norm, sg_w, sg_b, norm2)
    big_w = (w_in, w_out, w_up, w_down)
    big_m = (m_w_in, m_w_out, m_w_up, m_w_down)
    big_v = (v_w_in, v_w_out, v_w_up, v_w_down)
    shards = [[w[l].astype(BF16) for w in big_w] for l in range(DEPTH)]

    wi0 = _full_w_in(SiblingGather(shards[0][:1]).alone("gather_w_in0")[0])
    sh_in1, sh_out1, sh_up1, sh_down1 = shards[1]
    x1, s0, (wi1, wo1, wd1) = _layer_fwd(x.reshape(t, D_MODEL), wi0, *shards[0][1:], tuple(w[0] for w in small_w), 0,
                                         ride_mlp=Exchange([sh_in1, sh_out1, sh_down1], False))
    x2, s1, _ = _layer_fwd(x1, _full_w_in(wi1), wo1, sh_up1, wd1, tuple(w[1] for w in small_w), 1)
    loss_local, dx, dfinal = _loss_grad(x2, final_norm[None, :], loss_target.reshape(t, D_MODEL), min(MLP_ROWS, t), "loss_grad")
    loss = lax.psum(loss_local[0, 0], MESH_AXES)

    dx, parts1, small1, _ = _layer_bwd(dx, s1, 1)
    early = jnp.concatenate(list(small1) + [dfinal.reshape(-1, 128)])
    dx, parts0, (dn1, small0), (recv_in1, early) = _layer_bwd(
        dx, s0, 0, ride_mlp=Exchange([parts1[0], early], [True, False]), gather_small=True)
    grad_x = dx.reshape(x.shape)
    recv_in0, dn1 = _exchange([parts0[0], dn1], [True, False], "scatter_w_in0_gather_norm1_0")
    small_all = jnp.concatenate([dn1, small0, early], axis=1)
    received = [[recv_in0] + list(parts0[1:]), [recv_in1] + list(parts1[1:])]

    big = [None] * 4
    for l in reversed(range(DEPTH)):
        for k in range(4):
            big[k] = _reduce_adamw(received[l][k], big_w[k], big_m[k], big_v[k], l, big[k], f"adamw{k}_{l}")

    sm = _reduce_adamw(
        small_all,
        _pack_small(small_w, final_norm)[None],
        _pack_small([m_norm1, m_pool_w, m_pool_scale, m_sg_norm, m_sg_w, m_sg_b, m_norm2], m_final_norm)[None],
        _pack_small([v_norm1, v_pool_w, v_pool_scale, v_sg_norm, v_sg_w, v_sg_b, v_norm2], v_final_norm)[None],
        0, None, "adamw_replicated")

    out = [loss, grad_x]
    for k in range(4):
        n1, pw, ps, sn, sw, sb, n2, fn = _unpack_small(sm[k][0])
        out += [n1, big[0][k], pw, ps, sn, sw, sb, big[1][k], n2, big[2][k], big[3][k], fn]
    return tuple(out)
```
